```python
import jax
import jax.numpy as jnp
from jax import lax
import numpy as np

D_MODEL = 1024
BATCH = 8
SEQ = 2048
DEPTH = 1

GRID_W = 64
CTX_LEN = 256
HEAD_DIM = 64
A_Q_HEADS = D_MODEL // (2 * HEAD_DIM)
A_KV_HEADS = A_Q_HEADS // 4
B_Q_HEADS = D_MODEL // (2 * HEAD_DIM)
B_KV_HEADS = B_Q_HEADS // 4
A_Q_DIM = A_Q_HEADS * HEAD_DIM
A_KV_DIM = A_KV_HEADS * HEAD_DIM
B_Q_DIM = B_Q_HEADS * HEAD_DIM
B_KV_DIM = B_KV_HEADS * HEAD_DIM
PROJ_SIZES = (A_Q_DIM, A_KV_DIM, A_KV_DIM, B_Q_DIM, B_KV_DIM, B_KV_DIM)
IN_DIM = A_Q_DIM + 2 * A_KV_DIM + B_Q_DIM + 2 * B_KV_DIM
MIX_DIM = A_Q_DIM + B_Q_DIM
WINDOW = 128
BLOCK = 128
ROPE_THETA = 10000.0
N_GROUPS = 4
EXPERTS_PER_GROUP = 8
N_EXPERTS = N_GROUPS * EXPERTS_PER_GROUP
TOP_K_IN_GROUP = 2
EXPERT_FF = D_MODEL // 2
EPS = 1e-6
NEG_INF = -1e30

kernel_name = 'hymba_style_dual_attention_hier_moe_dit'


def rms_norm(x, g):
    xf = x.astype(jnp.float32)
    y = xf * lax.rsqrt(jnp.mean(xf * xf, axis=-1, keepdims=True) + EPS)
    return (y * g.astype(jnp.float32)).astype(x.dtype)


def axial_rope_tables(rows):
    row = jnp.broadcast_to(jnp.arange(rows, dtype=jnp.float32)[:, None], (rows, GRID_W)).reshape(-1)
    col = jnp.broadcast_to(jnp.arange(GRID_W, dtype=jnp.float32)[None, :], (rows, GRID_W)).reshape(-1)
    axis_dim = HEAD_DIM // 2
    inv_freq = ROPE_THETA ** (-jnp.arange(0, axis_dim, 2, dtype=jnp.float32) / axis_dim)
    ang = jnp.concatenate([row[:, None] * inv_freq, col[:, None] * inv_freq], axis=-1)
    return jnp.cos(ang), jnp.sin(ang)


def apply_rope(x, cos, sin):
    xf = x.astype(jnp.float32).reshape(*x.shape[:-1], HEAD_DIM // 2, 2)
    x0, x1 = xf[..., 0], xf[..., 1]
    cs = cos[None, :, None, :]
    sn = sin[None, :, None, :]
    out = jnp.stack([x0 * cs - x1 * sn, x0 * sn + x1 * cs], axis=-1)
    return out.reshape(x.shape).astype(x.dtype)


def project_heads(h, w_in):
    p = h @ w_in
    cuts, acc = [], 0
    for size in PROJ_SIZES[:-1]:
        acc += size
        cuts.append(acc)
    qa, ka, va, qb, kb, vb = jnp.split(p, cuts, axis=-1)
    lead = h.shape[:-1]
    return (qa.reshape(*lead, A_Q_HEADS, HEAD_DIM), ka.reshape(*lead, A_KV_HEADS, HEAD_DIM),
            va.reshape(*lead, A_KV_HEADS, HEAD_DIM), qb.reshape(*lead, B_Q_HEADS, HEAD_DIM),
            kb.reshape(*lead, B_KV_HEADS, HEAD_DIM), vb.reshape(*lead, B_KV_HEADS, HEAD_DIM))


def windowed_sink_attention(q, k, v, k_ctx, v_ctx, sink):
    bsz, n, hq, dh = q.shape
    hkv = k.shape[2]
    g = hq // hkv
    nb = n // BLOCK
    scale = dh ** -0.5
    qb = q.reshape(bsz, nb, BLOCK, hkv, g, dh)
    pad = ((0, 0), (BLOCK, BLOCK), (0, 0), (0, 0))
    kp = jnp.pad(k, pad).reshape(bsz, nb + 2, BLOCK, hkv, dh)
    vp = jnp.pad(v, pad).reshape(bsz, nb + 2, BLOCK, hkv, dh)
    kw = jnp.concatenate([kp[:, :-2], kp[:, 1:-1], kp[:, 2:]], axis=2)
    vw = jnp.concatenate([vp[:, :-2], vp[:, 1:-1], vp[:, 2:]], axis=2)
    qpos = jnp.arange(nb)[:, None, None] * BLOCK + jnp.arange(BLOCK)[None, :, None]
    kpos = jnp.arange(nb)[:, None, None] * BLOCK + jnp.arange(3 * BLOCK)[None, None, :] - BLOCK
    allowed = (jnp.abs(kpos - qpos) <= WINDOW) & (kpos >= 0) & (kpos < n)
    s_lat = jnp.einsum('bnqhgd,bnkhd->bnhgqk', qb, kw).astype(jnp.float32) * scale
    s_lat = jnp.where(allowed[None, :, None, None], s_lat, NEG_INF)
    s_ctx = jnp.einsum('bnqhgd,bchd->bnhgqc', qb, k_ctx).astype(jnp.float32) * scale
    s_sink = jnp.broadcast_to(sink.astype(jnp.float32).reshape(1, 1, hkv, g, 1, 1), s_lat.shape[:-1] + (1,))
    p = jax.nn.softmax(jnp.concatenate([s_lat, s_ctx, s_sink], axis=-1), axis=-1).astype(v.dtype)
    n_band = 3 * BLOCK
    p_lat = p[..., :n_band]
    p_ctx = p[..., n_band:n_band + k_ctx.shape[1]]
    o = (jnp.einsum('bnhgqk,bnkhd->bnqhgd', p_lat, vw)
         + jnp.einsum('bnhgqc,bchd->bnqhgd', p_ctx, v_ctx))
    return o.reshape(bsz, n, hq, dh)


def blocked_global_attention(q, k, v, k_ctx, v_ctx):
    bsz, n, hq, dh = q.shape
    hkv = k.shape[2]
    g = hq // hkv
    nb = n // BLOCK
    scale = dh ** -0.5
    k_all = jnp.concatenate([k, k_ctx], axis=1)
    v_all = jnp.concatenate([v, v_ctx], axis=1)
    q_blocks = jnp.moveaxis(q.reshape(bsz, nb, BLOCK, hkv, g, dh), 1, 0)

    def one_block(qblk):
        s = jnp.einsum('bqhgd,bkhd->bhgqk', qblk, k_all).astype(jnp.float32) * scale
        p = jax.nn.softmax(s, axis=-1).astype(v_all.dtype)
        return jnp.einsum('bhgqk,bkhd->bqhgd', p, v_all)

    o = lax.map(one_block, q_blocks)
    return jnp.moveaxis(o, 0, 1).reshape(bsz, n, hq, dh)


def context_attention(q, k, v, sink):
    bsz, n, hq, dh = q.shape
    hkv = k.shape[2]
    g = hq // hkv
    qg = q.reshape(bsz, n, hkv, g, dh)
    s = jnp.einsum('bqhgd,bkhd->bhgqk', qg, k).astype(jnp.float32) * dh ** -0.5
    if sink is not None:
        s_sink = jnp.broadcast_to(sink.astype(jnp.float32).reshape(1, hkv, g, 1, 1), s.shape[:-1] + (1,))
        s = jnp.concatenate([s, s_sink], axis=-1)
    p = jax.nn.softmax(s, axis=-1).astype(v.dtype)[..., :k.shape[1]]
    o = jnp.einsum('bhgqk,bkhd->bqhgd', p, v)
    return o.reshape(bsz, n, hq, dh)


def merge_head_groups(oa, ob, a_out_norm, b_out_norm, w_out):
    bsz, n = oa.shape[:2]
    oa = rms_norm(oa.reshape(bsz, n, A_Q_DIM), a_out_norm)
    ob = rms_norm(ob.reshape(bsz, n, B_Q_DIM), b_out_norm)
    return jnp.concatenate([oa, ob], axis=-1) @ w_out


def token_mixing(hx, hc, w_in, a_sink, b_q_norm, b_k_norm, a_out_norm, b_out_norm, w_out,
                 cos, sin, with_ctx_out):
    qa, ka, va, qb, kb, vb = project_heads(hx, w_in)
    qa_c, ka_c, va_c, qb_c, kb_c, vb_c = project_heads(hc, w_in)
    oa = windowed_sink_attention(apply_rope(qa, cos, sin), apply_rope(ka, cos, sin), va,
                                 ka_c, va_c, a_sink)
    kb_c = rms_norm(kb_c, b_k_norm)
    ob = blocked_global_attention(apply_rope(rms_norm(qb, b_q_norm), cos, sin),
                                  apply_rope(rms_norm(kb, b_k_norm), cos, sin), vb, kb_c, vb_c)
    out_x = merge_head_groups(oa, ob, a_out_norm, b_out_norm, w_out)
    if not with_ctx_out:
        return out_x, None
    oa_c = context_attention(qa_c, ka_c, va_c, a_sink)
    ob_c = context_attention(rms_norm(qb_c, b_q_norm), kb_c, vb_c, None)
    return out_x, merge_head_groups(oa_c, ob_c, a_out_norm, b_out_norm, w_out)


def hierarchical_moe(h, w_group, b_group, w_router, b_router, w_gate, w_up, w_down):
    g_logits = (h @ w_group + b_group).astype(jnp.float32)
    g_prob = jax.nn.softmax(g_logits, axis=-1)
    g_idx = jnp.argmax(g_logits, axis=-1)
    g_w = jnp.take_along_axis(g_prob, g_idx[..., None], axis=-1)
    e_logits = (h @ w_router + b_router).astype(jnp.float32)
    e_logits = e_logits.reshape(*h.shape[:-1], N_GROUPS, EXPERTS_PER_GROUP)
    sel = jnp.broadcast_to(g_idx[..., None, None], h.shape[:-1] + (1, EXPERTS_PER_GROUP))
    e_logits = jnp.take_along_axis(e_logits, sel, axis=-2)[..., 0, :]
    e_prob = jax.nn.softmax(e_logits, axis=-1)
    top_p, top_i = lax.top_k(e_prob, TOP_K_IN_GROUP)
    weights = g_w * top_p / jnp.sum(top_p, axis=-1, keepdims=True)
    expert_id = g_idx[..., None] * EXPERTS_PER_GROUP + top_i
    combine = jnp.sum(jax.nn.one_hot(expert_id, N_EXPERTS, dtype=jnp.float32) * weights[..., None],
                      axis=-2).astype(h.dtype)

    def per_sample(args):
        hs, cs = args
        gate = jnp.einsum('td,edf->tef', hs, w_gate)
        up = jnp.einsum('td,edf->tef', hs, w_up)
        act = jax.nn.silu(gate) * up * cs[..., None]
        return jnp.einsum('tef,efd->td', act, w_down)

    return lax.map(per_sample, (h, combine))


def setup_inputs(seed: int = 0) -> dict:
    key = jax.random.key(seed)
    ks = jax.random.split(key, 24)
    L, D = DEPTH, D_MODEL

    def normal(k, shape, scale):
        return jax.random.normal(k, shape, dtype=jnp.float32) * scale

    def gain(k, shape):
        return 1.0 + normal(k, shape, 0.02)

    return {
        'x': normal(ks[0], (BATCH, SEQ, D), 1.0),
        'c': normal(ks[1], (BATCH, D), 1.0),
        'ctx': normal(ks[2], (BATCH, CTX_LEN, D), 1.0),
        'c_ctx': normal(ks[3], (D,), 1.0),
        'w_mod': normal(ks[4], (L, D, 6 * D), 0.5 * D ** -0.5),
        'b_mod': normal(ks[5], (L, 6 * D), 0.02),
        'attn_pre_norm': gain(ks[6], (L, D)),
        'attn_post_norm': gain(ks[7], (L, D)),
        'w_in': normal(ks[8], (L, D, IN_DIM), D ** -0.5),
        'a_sink': normal(ks[9], (L, A_Q_HEADS), 0.5),
        'b_q_norm': gain(ks[10], (L, HEAD_DIM)),
        'b_k_norm': gain(ks[11], (L, HEAD_DIM)),
        'a_out_norm': gain(ks[12], (L, A_Q_DIM)),
        'b_out_norm': gain(ks[13], (L, B_Q_DIM)),
        'w_out': normal(ks[14], (L, MIX_DIM, D), MIX_DIM ** -0.5),
        'ffn_pre_norm': gain(ks[15], (L, D)),
        'ffn_post_norm': gain(ks[16], (L, D)),
        'w_group': normal(ks[17], (L, D, N_GROUPS), D ** -0.5),
        'b_group': normal(ks[18], (L, N_GROUPS), 0.01),
        'w_router': normal(ks[19], (L, D, N_EXPERTS), D ** -0.5),
        'b_router': normal(ks[20], (L, N_EXPERTS), 0.01),
        'w_gate': normal(ks[21], (L, N_EXPERTS, D, EXPERT_FF), D ** -0.5),
        'w_up': normal(ks[22], (L, N_EXPERTS, D, EXPERT_FF), D ** -0.5),
        'w_down': normal(ks[23], (L, N_EXPERTS, EXPERT_FF, D), EXPERT_FF ** -0.5),
    }


def reference(x, c, ctx, c_ctx, w_mod, b_mod, attn_pre_norm, attn_post_norm, w_in, a_sink,
              b_q_norm, b_k_norm, a_out_norm, b_out_norm, w_out, ffn_pre_norm, ffn_post_norm,
              w_group, b_group, w_router, b_router, w_gate, w_up, w_down):
    rows = x.shape[1] // GRID_W
    cos, sin = axial_rope_tables(rows)
    silu_c = jax.nn.silu(c)
    silu_c_ctx = jax.nn.silu(c_ctx)
    for l in range(DEPTH):
        last = l == DEPTH - 1
        mod_x = (silu_c @ w_mod[l] + b_mod[l])[:, None, :]
        mod_c = (silu_c_ctx @ w_mod[l] + b_mod[l])[None, None, :]
        sh1, sc1, g1, sh2, sc2, g2 = jnp.split(mod_x, 6, axis=-1)
        csh1, csc1, cg1, csh2, csc2, cg2 = jnp.split(mod_c, 6, axis=-1)
        hx = rms_norm(x, attn_pre_norm[l]) * (1 + sc1) + sh1
        hc = rms_norm(ctx, attn_pre_norm[l]) * (1 + csc1) + csh1
        ox, oc = token_mixing(hx, hc, w_in[l], a_sink[l], b_q_norm[l], b_k_norm[l], a_out_norm[l],
                              b_out_norm[l], w_out[l], cos, sin, not last)
        x = x + g1 * rms_norm(ox, attn_post_norm[l])
        hx = rms_norm(x, ffn_pre_norm[l]) * (1 + sc2) + sh2
        fx = hierarchical_moe(hx, w_group[l], b_group[l], w_router[l], b_router[l],
                              w_gate[l], w_up[l], w_down[l])
        x = x + g2 * rms_norm(fx, ffn_post_norm[l])
        if not last:
            ctx = ctx + cg1 * rms_norm(oc, attn_post_norm[l])
            hc = rms_norm(ctx, ffn_pre_norm[l]) * (1 + csc2) + csh2
            fc = hierarchical_moe(hc, w_group[l], b_group[l], w_router[l], b_router[l],
                                  w_gate[l], w_up[l], w_down[l])
            ctx = ctx + cg2 * rms_norm(fc, ffn_post_norm[l])
    return x
```

```python
import functools

import jax
import jax.numpy as jnp
from jax import lax
from jax.experimental import pallas as pl
from jax.experimental.pallas import tpu as pltpu

F32 = jnp.float32
BF16 = jnp.bfloat16

GRID_W = 64
HEAD_DIM = 64
KV_GROUP = 4
WINDOW = 128
ROPE_THETA = 10000.0
N_GROUPS = 4
EXPERTS_PER_GROUP = 8
N_EXPERTS = N_GROUPS * EXPERTS_PER_GROUP
EPS = 1e-6
NEG_INF = -1e30

LANES = 128
V7X_VMEM_LIMIT = 56 * 1024 * 1024

PROJ_TM = 512
ATTN_B_TQ = 256
OUT_TM = 256
EXPERT_TE = 256
MOVE_TM = 256


def _params(*sem):
    return pltpu.CompilerParams(dimension_semantics=sem, vmem_limit_bytes=V7X_VMEM_LIMIT)


def _dot(a, b):
    return jnp.dot(a, b, preferred_element_type=F32)


def _dot_nt(a, b):
    return lax.dot_general(a, b, (((1,), (1,)), ((), ())), preferred_element_type=F32)


def _rms(x):
    return x * lax.rsqrt(jnp.mean(x * x, axis=-1, keepdims=True) + EPS)


def _split_bf16(x):
    hi = x.astype(BF16)
    lo = (x - hi.astype(F32)).astype(BF16)
    return hi, lo


def _mod_kernel(c_ref, w_ref, b_ref, o_ref):
    cc = c_ref[...]
    s = cc * jax.nn.sigmoid(cc)
    s_hi, s_lo = _split_bf16(s)
    w_hi, w_lo = _split_bf16(w_ref[...])
    o_ref[...] = _dot(s_hi, w_hi) + _dot(s_lo, w_hi) + _dot(s_hi, w_lo) + b_ref[...]


def _modulation(cc, w_mod, b_mod):
    rows, d = cc.shape
    n = w_mod.shape[1]
    bn = 1024
    return pl.pallas_call(
        _mod_kernel,
        grid=(n // bn,),
        in_specs=[pl.BlockSpec((rows, d), lambda i: (0, 0)),
                  pl.BlockSpec((d, bn), lambda i: (0, i)),
                  pl.BlockSpec((1, bn), lambda i: (0, i))],
        out_specs=pl.BlockSpec((rows, bn), lambda i: (0, i)),
        out_shape=jax.ShapeDtypeStruct((rows, n), F32),
        compiler_params=_params("arbitrary"),
    )(cc, w_mod, b_mod.reshape(1, n))


def _rope(x, cos, sin_a, sin_b):
    return x * cos + pltpu.roll(x, LANES - 1, 1) * sin_a + pltpu.roll(x, 1, 1) * sin_b


def _head_norm(x, seg_ref, gain):
    ss = _dot((x * x).astype(BF16), seg_ref[...])
    return x * lax.rsqrt(ss * (1.0 / HEAD_DIM) + EPS) * gain


def _store_pair_variants(ref, t):
    lane = lax.broadcasted_iota(jnp.int32, t.shape, 1)
    lo = lane < HEAD_DIM
    sw = pltpu.roll(t, HEAD_DIM, 1)
    zero = jnp.zeros_like(t)
    ref[0] = jnp.where(lo, t, zero).astype(BF16)
    ref[1] = jnp.where(lo, zero, sw).astype(BF16)
    ref[2] = jnp.where(lo, sw, zero).astype(BF16)
    ref[3] = jnp.where(lo, zero, t).astype(BF16)


def _proj_kernel(x_ref, sc_ref, sh_ref, gpre_ref, w_ref, cos_ref, sa_ref, sb_ref,
                 qn_ref, kn_ref, seg_q_ref, seg_k_ref,
                 qa_ref, qb_ref, ka_ref, va_ref, kb_ref, vb_ref):
    h = _rms(x_ref[...]) * gpre_ref[...] * (1.0 + sc_ref[0]) + sh_ref[0]
    p = _dot(h.astype(BF16), w_ref[...])
    cos, sa, sb = cos_ref[...], sa_ref[...], sb_ref[...]
    nq = qa_ref.shape[1]
    q_scale = HEAD_DIM ** -0.5
    for c in range(nq // LANES):
        qa_ref[:, c * LANES:(c + 1) * LANES] = (
            _rope(p[:, c * LANES:(c + 1) * LANES], cos, sa, sb) * q_scale).astype(BF16)
    o = nq
    _store_pair_variants(ka_ref, _rope(p[:, o:o + LANES], cos, sa, sb))
    _store_pair_variants(va_ref, p[:, o + LANES:o + 2 * LANES])
    o += 2 * LANES
    qb = _head_norm(p[:, o:o + nq], seg_q_ref, qn_ref[...])
    for c in range(nq // LANES):
        qb_ref[:, c * LANES:(c + 1) * LANES] = (
            _rope(qb[:, c * LANES:(c + 1) * LANES], cos, sa, sb) * q_scale).astype(BF16)
    o += nq
    kb = _head_norm(p[:, o:o + LANES], seg_k_ref, kn_ref[...])
    _store_pair_variants(kb_ref, _rope(kb, cos, sa, sb))
    _store_pair_variants(vb_ref, p[:, o + LANES:o + 2 * LANES])


def _ctx_proj_kernel(x_ref, sc_ref, sh_ref, gpre_ref, w_ref, kn_ref, seg_k_ref,
                     ka_ref, va_ref, kb_ref, vb_ref):
    h = _rms(x_ref[...]) * gpre_ref[...] * (1.0 + sc_ref[...]) + sh_ref[...]
    p = _dot(h.astype(BF16), w_ref[...])
    _store_pair_variants(ka_ref, p[:, 0:LANES])
    _store_pair_variants(va_ref, p[:, LANES:2 * LANES])
    _store_pair_variants(kb_ref, _head_norm(p[:, 2 * LANES:3 * LANES], seg_k_ref, kn_ref[...]))
    _store_pair_variants(vb_ref, p[:, 3 * LANES:4 * LANES])


def _project_latents(x2, sc, sh, gpre, w_in, tables, qn, kn, seg_q, seg_k, seq):
    t, d = x2.shape
    tm = PROJ_TM
    tpb = seq // tm
    nq = seg_q.shape[0]
    const = lambda shape: pl.BlockSpec(shape, lambda i: (0,) * len(shape))
    per_batch = pl.BlockSpec((1, 1, d), lambda i: (i // tpb, 0, 0))
    table = pl.BlockSpec((tm, LANES), lambda i: (i % tpb, 0))
    kv_spec = pl.BlockSpec((4, tm, LANES), lambda i: (0, i, 0))
    kv_shape = jax.ShapeDtypeStruct((4, t, LANES), BF16)
    q_spec = pl.BlockSpec((tm, nq), lambda i: (i, 0))
    q_shape = jax.ShapeDtypeStruct((t, nq), BF16)
    return pl.pallas_call(
        _proj_kernel,
        grid=(t // tm,),
        in_specs=[pl.BlockSpec((tm, d), lambda i: (i, 0)), per_batch, per_batch, const((1, d)),
                  const(w_in.shape), table, table, table,
                  const((1, nq)), const((1, LANES)), const(seg_q.shape), const(seg_k.shape)],
        out_specs=[q_spec, q_spec, kv_spec, kv_spec, kv_spec, kv_spec],
        out_shape=[q_shape, q_shape, kv_shape, kv_shape, kv_shape, kv_shape],
        compiler_params=_params("arbitrary"),
    )(x2, sc, sh, gpre, w_in, *tables, qn, kn, seg_q, seg_k)


def _project_context(c2, sc, sh, gpre, w_kv, kn, seg_k, ctx_len):
    t, d = c2.shape
    const = lambda shape: pl.BlockSpec(shape, lambda i: (0,) * len(shape))
    kv_spec = pl.BlockSpec((4, ctx_len, LANES), lambda i: (0, i, 0))
    kv_shape = jax.ShapeDtypeStruct((4, t, LANES), BF16)
    return pl.pallas_call(
        _ctx_proj_kernel,
        grid=(t // ctx_len,),
        in_specs=[pl.BlockSpec((ctx_len, d), lambda i: (i, 0)), const((1, d)), const((1, d)),
                  const((1, d)), const(w_kv.shape), const((1, LANES)), const(seg_k.shape)],
        out_specs=[kv_spec] * 4,
        out_shape=[kv_shape] * 4,
        compiler_params=_params("arbitrary"),
    )(c2, sc, sh, gpre, w_kv, kn, seg_k)


def _attn_b_kernel(q_ref, k_ref, v_ref, kc_ref, vc_ref, o_ref):
    n_pairs = q_ref.shape[1] // LANES
    pairs_per_kv = KV_GROUP // 2
    for pair in range(n_pairs):
        g = pair // pairs_per_kv
        q2 = q_ref[:, pair * LANES:(pair + 1) * LANES]
        acc = None
        for half in range(2):
            kv = 2 * g + half
            s = _dot_nt(q2, k_ref[kv])
            sc = _dot_nt(q2, kc_ref[kv])
            m = jnp.maximum(jnp.max(s, axis=-1, keepdims=True), jnp.max(sc, axis=-1, keepdims=True))
            p = jnp.exp(s - m)
            pc = jnp.exp(sc - m)
            denom = jnp.sum(p, axis=-1, keepdims=True) + jnp.sum(pc, axis=-1, keepdims=True)
            o = (_dot(p.astype(BF16), v_ref[kv]) + _dot(pc.astype(BF16), vc_ref[kv])) / denom
            acc = o if acc is None else acc + o
        o_ref[:, pair * LANES:(pair + 1) * LANES] = acc.astype(BF16)


def _attention_b(qb, kb4, vb4, kcb4, vcb4, batch, seq, ctx_len):
    t, nq = qb.shape
    tq = ATTN_B_TQ
    nqb = seq // tq
    lat = pl.BlockSpec((4, seq, LANES), lambda b, i: (0, b, 0))
    cx = pl.BlockSpec((4, ctx_len, LANES), lambda b, i: (0, b, 0))
    qs = pl.BlockSpec((tq, nq), lambda b, i: (b * nqb + i, 0))
    return pl.pallas_call(
        _attn_b_kernel,
        grid=(batch, nqb),
        in_specs=[qs, lat, lat, cx, cx],
        out_specs=qs,
        out_shape=jax.ShapeDtypeStruct((t, nq), BF16),
        compiler_params=_params("arbitrary", "arbitrary"),
    )(qb, kb4, vb4, kcb4, vcb4)


def _attn_a_kernel(sink_ref, q_ref, kl_ref, km_ref, kr_ref, vl_ref, vm_ref, vr_ref,
                   kc_ref, vc_ref, o_ref, *, n_blocks):
    i = pl.program_id(1)
    blk = q_ref.shape[0]
    row = lax.broadcasted_iota(jnp.int32, (blk, blk), 0)
    col = lax.broadcasted_iota(jnp.int32, (blk, blk), 1)
    mask_l = (col >= row) & (i >= 1)
    mask_r = (col <= row) & (i <= n_blocks - 2)
    n_pairs = q_ref.shape[1] // LANES
    pairs_per_kv = KV_GROUP // 2
    rowmax = lambda s: jnp.max(s, axis=-1, keepdims=True)
    rowsum = lambda s: jnp.sum(s, axis=-1, keepdims=True)
    for pair in range(n_pairs):
        g = pair // pairs_per_kv
        q2 = q_ref[:, pair * LANES:(pair + 1) * LANES]
        acc = None
        for half in range(2):
            kv = 2 * g + half
            sink = sink_ref[2 * pair + half]
            sl = jnp.where(mask_l, _dot_nt(q2, kl_ref[kv]), NEG_INF)
            sm = _dot_nt(q2, km_ref[kv])
            sr = jnp.where(mask_r, _dot_nt(q2, kr_ref[kv]), NEG_INF)
            sc = _dot_nt(q2, kc_ref[kv])
            m = jnp.maximum(jnp.maximum(rowmax(sl), rowmax(sm)),
                            jnp.maximum(rowmax(sr), rowmax(sc)))
            m = jnp.maximum(m, sink)
            pl_, pm, pr, pc = (jnp.exp(s - m) for s in (sl, sm, sr, sc))
            denom = rowsum(pl_) + rowsum(pm) + rowsum(pr) + rowsum(pc) + jnp.exp(sink - m)
            o = (_dot(pl_.astype(BF16), vl_ref[kv]) + _dot(pm.astype(BF16), vm_ref[kv])
                 + _dot(pr.astype(BF16), vr_ref[kv]) + _dot(pc.astype(BF16), vc_ref[kv])) / denom
            acc = o if acc is None else acc + o
        o_ref[:, pair * LANES:(pair + 1) * LANES] = acc.astype(BF16)


def _attention_a(sink, qa, ka4, va4, kca4, vca4, batch, seq, ctx_len):
    t, nq = qa.shape
    blk = WINDOW
    nb = seq // blk
    left = pl.BlockSpec((4, blk, LANES), lambda b, i: (0, b * nb + jnp.maximum(i - 1, 0), 0))
    mid = pl.BlockSpec((4, blk, LANES), lambda b, i: (0, b * nb + i, 0))
    right = pl.BlockSpec((4, blk, LANES), lambda b, i: (0, b * nb + jnp.minimum(i + 1, nb - 1), 0))
    cx = pl.BlockSpec((4, ctx_len, LANES), lambda b, i: (0, b, 0))
    qs = pl.BlockSpec((blk, nq), lambda b, i: (b * nb + i, 0))
    return pl.pallas_call(
        functools.partial(_attn_a_kernel, n_blocks=nb),
        grid=(batch, nb),
        in_specs=[pl.BlockSpec(memory_space=pltpu.SMEM), qs, left, mid, right, left, mid, right, cx, cx],
        out_specs=qs,
        out_shape=jax.ShapeDtypeStruct((t, nq), BF16),
        compiler_params=_params("arbitrary", "arbitrary"),
    )(sink, qa, ka4, ka4, ka4, va4, va4, va4, kca4, vca4)


def _out_kernel(oa_ref, ob_ref, x_ref, g1_ref, sc2_ref, sh2_ref, ga_ref, gb_ref, gpost_ref,
                gpre2_ref, woa_ref, wob_ref, wrh_ref, wrl_ref, br_ref,
                x1_ref, h2_ref, rinfo_ref, cnt_ref, carry_ref):
    step = pl.program_id(0)

    @pl.when(step == 0)
    def _():
        carry_ref[...] = jnp.zeros_like(carry_ref)

    na = _rms(oa_ref[...].astype(F32)) * ga_ref[...]
    nb = _rms(ob_ref[...].astype(F32)) * gb_ref[...]
    ox = _dot(na.astype(BF16), woa_ref[...]) + _dot(nb.astype(BF16), wob_ref[...])
    x1 = x_ref[...] + g1_ref[0] * (_rms(ox) * gpost_ref[...])
    x1_ref[...] = x1
    h2 = _rms(x1) * gpre2_ref[...] * (1.0 + sc2_ref[0]) + sh2_ref[0]
    h2_ref[...] = h2

    h_hi, h_lo = _split_bf16(h2)
    logits = (_dot(h_hi, wrh_ref[...]) + _dot(h_lo, wrh_ref[...]) + _dot(h_hi, wrl_ref[...])
              + br_ref[...])
    tm = logits.shape[0]
    lane = lax.broadcasted_iota(jnp.int32, logits.shape, 1)
    lanef = lane.astype(F32)
    big = jnp.float32(1e9)
    ninf = jnp.float32(-jnp.inf)
    rowmax = lambda v: jnp.max(v, axis=-1, keepdims=True)
    rowmin = lambda v: jnp.min(v, axis=-1, keepdims=True)
    rowsum = lambda v: jnp.sum(v, axis=-1, keepdims=True)

    gmask = (lane >= N_EXPERTS) & (lane < N_EXPERTS + N_GROUPS)
    lg = jnp.where(gmask, logits, ninf)
    gmax = rowmax(lg)
    gidx = rowmin(jnp.where(lg == gmax, lanef, big)) - N_EXPERTS
    g_w = 1.0 / rowsum(jnp.exp(lg - gmax))
    lane_group = (lane // EXPERTS_PER_GROUP).astype(F32)
    emask = (lane < N_EXPERTS) & (lane_group == gidx)
    le = jnp.where(emask, logits, ninf)
    m1 = rowmax(le)
    i1 = rowmin(jnp.where(le == m1, lanef, big))
    le2 = jnp.where(lanef == i1, ninf, le)
    m2 = rowmax(le2)
    i2 = rowmin(jnp.where(le2 == m2, lanef, big))
    e2 = jnp.exp(m2 - m1)
    w0 = g_w / (1.0 + e2)
    w1 = g_w * e2 / (1.0 + e2)

    hit1 = lanef == i1
    hit2 = lanef == i2
    onehot = jnp.where(hit1 | hit2, 1.0, 0.0).astype(F32)
    r = lax.broadcasted_iota(jnp.int32, (tm, tm), 0)
    c = lax.broadcasted_iota(jnp.int32, (tm, tm), 1)
    strict_lower = jnp.where(r > c, 1.0, 0.0).astype(BF16)
    before = _dot(strict_lower, onehot.astype(BF16)) + carry_ref[...]
    rank0 = rowsum(jnp.where(hit1, before, 0.0))
    rank1 = rowsum(jnp.where(hit2, before, 0.0))
    carry_ref[...] += jnp.sum(onehot, axis=0, keepdims=True)
    cnt_ref[...] = carry_ref[...]

    info = jnp.zeros_like(logits)
    for k, val in enumerate((i1, i2, rank0, rank1, w0, w1)):
        info = jnp.where(lane == k, val, info)
    rinfo_ref[...] = info


def _out_and_route(oa, ob, x2, g1, sc2, sh2, ga, gb, gpost, gpre2, woa, wob, wrh, wrl, br, seq):
    t, d = x2.shape
    tm = OUT_TM
    tpb = seq // tm
    nq = oa.shape[1]
    const = lambda shape: pl.BlockSpec(shape, lambda i: (0,) * len(shape))
    per_batch = pl.BlockSpec((1, 1, d), lambda i: (i // tpb, 0, 0))
    rows = lambda n: pl.BlockSpec((tm, n), lambda i: (i, 0))
    return pl.pallas_call(
        _out_kernel,
        grid=(t // tm,),
        in_specs=[rows(nq), rows(nq), rows(d), per_batch, per_batch, per_batch,
                  const((1, nq)), const((1, nq)), const((1, d)), const((1, d)),
                  const(woa.shape), const(wob.shape), const(wrh.shape), const(wrl.shape),
                  const((1, LANES))],
        out_specs=[rows(d), rows(d), rows(LANES), const((1, LANES))],
        out_shape=[jax.ShapeDtypeStruct((t, d), F32), jax.ShapeDtypeStruct((t, d), F32),
                   jax.ShapeDtypeStruct((t, LANES), F32), jax.ShapeDtypeStruct((1, LANES), F32)],
        scratch_shapes=[pltpu.VMEM((1, LANES), F32)],
        compiler_params=_params("arbitrary"),
    )(oa, ob, x2, g1, sc2, sh2, ga, gb, gpost, gpre2, woa, wob, wrh, wrl, br)


def _dispatch_kernel(d0_ref, d1_ref, h_ref, init_ref, xs_ref, sem):
    del init_ref
    tm = h_ref.shape[0]
    base = pl.program_id(0) * tm

    def row_copy(r, dest):
        return pltpu.make_async_copy(h_ref.at[pl.ds(r, 1)], xs_ref.at[pl.ds(dest, 1)], sem)

    def issue(r, carry):
        row_copy(r, d0_ref[base + r]).start()
        row_copy(r, d1_ref[base + r]).start()
        return carry

    def drain(r, carry):
        row_copy(r, d0_ref[base + r]).wait()
        row_copy(r, d1_ref[base + r]).wait()
        return carry

    lax.fori_loop(0, tm, issue, 0)
    lax.fori_loop(0, tm, drain, 0)


def _dispatch(dest0, dest1, h2, n_rows):
    t, d = h2.shape
    tm = MOVE_TM
    init = jnp.zeros((n_rows, d), F32)
    return pl.pallas_call(
        _dispatch_kernel,
        grid_spec=pltpu.PrefetchScalarGridSpec(
            num_scalar_prefetch=2,
            grid=(t // tm,),
            in_specs=[pl.BlockSpec((tm, d), lambda i, d0, d1: (i, 0)),
                      pl.BlockSpec(memory_space=pl.ANY)],
            out_specs=pl.BlockSpec(memory_space=pl.ANY),
            scratch_shapes=[pltpu.SemaphoreType.DMA(())]),
        out_shape=jax.ShapeDtypeStruct((n_rows, d), F32),
        input_output_aliases={3: 0},
        compiler_params=_params("arbitrary"),
    )(dest0, dest1, h2, init)


def _expert_kernel(te_ref, nv_ref, xs_ref, wg_ref, wu_ref, wd_ref, ys_ref, wg_bf, wu_bf, wd_bf):
    j = pl.program_id(0)
    valid = j < nv_ref[0]
    changed = (j == 0) | (te_ref[j] != te_ref[jnp.maximum(j - 1, 0)])

    @pl.when(valid & changed)
    def _():
        wg_bf[...] = wg_ref[0].astype(BF16)
        wu_bf[...] = wu_ref[0].astype(BF16)
        wd_bf[...] = wd_ref[0].astype(BF16)

    @pl.when(valid)
    def _():
        xb = xs_ref[...].astype(BF16)
        gate = _dot(xb, wg_bf[...])
        up = _dot(xb, wu_bf[...])
        act = gate * jax.nn.sigmoid(gate) * up
        ys_ref[...] = _dot(act.astype(BF16), wd_bf[...])

    @pl.when(jnp.logical_not(valid))
    def _():
        ys_ref[...] = jnp.zeros_like(ys_ref)


def _expert_mlp(tile_expert, n_valid, xs, w_gate, w_up, w_down):
    n_rows, d = xs.shape
    te = EXPERT_TE
    ff = w_gate.shape[2]
    tile = lambda j, e, nv: (jnp.minimum(j, nv[0] - 1), 0)
    wsel = lambda j, e, nv: (e[j], 0, 0)
    return pl.pallas_call(
        _expert_kernel,
        grid_spec=pltpu.PrefetchScalarGridSpec(
            num_scalar_prefetch=2,
            grid=(n_rows // te,),
            in_specs=[pl.BlockSpec((te, d), tile),
                      pl.BlockSpec((1, d, ff), wsel), pl.BlockSpec((1, d, ff), wsel),
                      pl.BlockSpec((1, ff, d), wsel)],
            out_specs=pl.BlockSpec((te, d), lambda j, e, nv: (j, 0)),
            scratch_shapes=[pltpu.VMEM((d, ff), BF16), pltpu.VMEM((d, ff), BF16),
                            pltpu.VMEM((ff, d), BF16)]),
        out_shape=jax.ShapeDtypeStruct((n_rows, d), F32),
        compiler_params=_params("arbitrary"),
    )(tile_expert, n_valid, xs, w_gate, w_up, w_down)


def _combine_kernel(d0_ref, d1_ref, x1_ref, rinfo_ref, g2_ref, gpost_ref, ys_ref, o_ref,
                    b0_ref, b1_ref, sem):
    tm = x1_ref.shape[0]
    base = pl.program_id(0) * tm

    def row_copy(buf, r, src):
        return pltpu.make_async_copy(ys_ref.at[pl.ds(src, 1)], buf.at[pl.ds(r, 1)], sem)

    def issue(r, carry):
        row_copy(b0_ref, r, d0_ref[base + r]).start()
        row_copy(b1_ref, r, d1_ref[base + r]).start()
        return carry

    def drain(r, carry):
        row_copy(b0_ref, r, d0_ref[base + r]).wait()
        row_copy(b1_ref, r, d1_ref[base + r]).wait()
        return carry

    lax.fori_loop(0, tm, issue, 0)
    lax.fori_loop(0, tm, drain, 0)
    info = rinfo_ref[...]
    fx = info[:, 4:5] * b0_ref[...] + info[:, 5:6] * b1_ref[...]
    o_ref[...] = x1_ref[...] + g2_ref[0] * (_rms(fx) * gpost_ref[...])


def _combine(dest0, dest1, x1, rinfo, g2, gpost, ys, seq):
    t, d = x1.shape
    tm = MOVE_TM
    tpb = seq // tm
    return pl.pallas_call(
        _combine_kernel,
        grid_spec=pltpu.PrefetchScalarGridSpec(
            num_scalar_prefetch=2,
            grid=(t // tm,),
            in_specs=[pl.BlockSpec((tm, d), lambda i, d0, d1: (i, 0)),
                      pl.BlockSpec((tm, LANES), lambda i, d0, d1: (i, 0)),
                      pl.BlockSpec((1, 1, d), lambda i, d0, d1: (i // tpb, 0, 0)),
                      pl.BlockSpec((1, d), lambda i, d0, d1: (0, 0)),
                      pl.BlockSpec(memory_space=pl.ANY)],
            out_specs=pl.BlockSpec((tm, d), lambda i, d0, d1: (i, 0)),
            scratch_shapes=[pltpu.VMEM((tm, d), F32), pltpu.VMEM((tm, d), F32),
                            pltpu.SemaphoreType.DMA(())]),
        out_shape=jax.ShapeDtypeStruct((t, d), F32),
        compiler_params=_params("arbitrary"),
    )(dest0, dest1, x1, rinfo, g2, gpost, ys)


def _rope_tables(seq):
    pos = jnp.arange(seq, dtype=jnp.int32)
    row = (pos // GRID_W).astype(F32)
    col = (pos % GRID_W).astype(F32)
    axis_dim = HEAD_DIM // 2
    inv_freq = ROPE_THETA ** (-jnp.arange(0, axis_dim, 2, dtype=F32) / axis_dim)
    ang = jnp.concatenate([row[:, None] * inv_freq, col[:, None] * inv_freq], axis=-1)
    pair = (jnp.arange(LANES) % HEAD_DIM) // 2
    cos = jnp.cos(ang)[:, pair]
    sin = jnp.sin(ang)[:, pair]
    even = (jnp.arange(LANES) % 2) == 0
    return cos, jnp.where(even, -sin, 0.0), jnp.where(even, 0.0, sin)


def _segment_ones(n):
    seg = jnp.arange(n) // HEAD_DIM
    return (seg[:, None] == seg[None, :]).astype(BF16)


def kernel(x, c, ctx, c_ctx, w_mod, b_mod, attn_pre_norm, attn_post_norm, w_in, a_sink,
           b_q_norm, b_k_norm, a_out_norm, b_out_norm, w_out, ffn_pre_norm, ffn_post_norm,
           w_group, b_group, w_router, b_router, w_gate, w_up, w_down):
    batch, seq, d = x.shape
    ctx_len = ctx.shape[1]
    assert w_mod.shape[0] == 1, "single-layer stack only (context stream is never updated)"
    assert seq % PROJ_TM == 0 and seq % ATTN_B_TQ == 0 and seq % OUT_TM == 0 and seq % MOVE_TM == 0
    t = batch * seq
    nq = d // 2
    nkv = nq // KV_GROUP
    assert nkv == LANES and w_in.shape[2] == 2 * nq + 4 * nkv

    cc = jnp.concatenate([c, c_ctx[None, :], jnp.zeros((16 - batch - 1, d), F32)], axis=0)
    mod = _modulation(cc, w_mod[0], b_mod[0])
    sh1, sc1, g1, sh2, sc2, g2 = (m.reshape(batch, 1, d) for m in jnp.split(mod[:batch], 6, axis=-1))
    csh1, csc1 = (m.reshape(1, d) for m in jnp.split(mod[batch], 6)[:2])

    x2 = x.reshape(t, d)
    c2 = ctx.reshape(batch * ctx_len, d)
    gpre = attn_pre_norm[0].reshape(1, d)
    w_in_bf = w_in[0].astype(BF16)
    kv_cols = jnp.concatenate([w_in_bf[:, nq:nq + 2 * nkv], w_in_bf[:, 2 * nq + 2 * nkv:]], axis=1)
    qn = jnp.tile(b_q_norm[0], nq // HEAD_DIM).reshape(1, nq)
    kn = jnp.tile(b_k_norm[0], nkv // HEAD_DIM).reshape(1, nkv)
    seg_q, seg_k = _segment_ones(nq), _segment_ones(nkv)
    qa, qb, ka4, va4, kb4, vb4 = _project_latents(
        x2, sc1, sh1, gpre, w_in_bf, _rope_tables(seq), qn, kn, seg_q, seg_k, seq)
    kca4, vca4, kcb4, vcb4 = _project_context(c2, csc1, csh1, gpre, kv_cols, kn, seg_k, ctx_len)

    oa = _attention_a(a_sink[0], qa, ka4, va4, kca4, vca4, batch, seq, ctx_len)
    ob = _attention_b(qb, kb4, vb4, kcb4, vcb4, batch, seq, ctx_len)

    w_out_bf = w_out[0].astype(BF16)
    w_r = jnp.zeros((d, LANES), F32)
    w_r = w_r.at[:, :N_EXPERTS].set(w_router[0]).at[:, N_EXPERTS:N_EXPERTS + N_GROUPS].set(w_group[0])
    w_r_hi = w_r.astype(BF16)
    w_r_lo = (w_r - w_r_hi.astype(F32)).astype(BF16)
    b_r = jnp.zeros((1, LANES), F32)
    b_r = b_r.at[0, :N_EXPERTS].set(b_router[0]).at[0, N_EXPERTS:N_EXPERTS + N_GROUPS].set(b_group[0])
    x1, h2, rinfo, counts = _out_and_route(
        oa, ob, x2, g1, sc2, sh2, a_out_norm[0].reshape(1, nq), b_out_norm[0].reshape(1, nq),
        attn_post_norm[0].reshape(1, d), ffn_pre_norm[0].reshape(1, d),
        w_out_bf[:nq], w_out_bf[nq:], w_r_hi, w_r_lo, b_r, seq)

    te = EXPERT_TE
    n_tiles = -(-(2 * t + N_EXPERTS * (te - 1)) // te)
    n_rows = n_tiles * te
    cnt = counts[0, :N_EXPERTS].astype(jnp.int32)
    padded = ((cnt + te - 1) // te) * te
    ends = jnp.cumsum(padded)
    offs = ends - padded
    e0 = rinfo[:, 0].astype(jnp.int32)
    e1 = rinfo[:, 1].astype(jnp.int32)
    dest0 = offs[e0] + rinfo[:, 2].astype(jnp.int32)
    dest1 = offs[e1] + rinfo[:, 3].astype(jnp.int32)
    n_valid = (ends[-1] // te).astype(jnp.int32)
    tile_start = jnp.arange(n_tiles, dtype=jnp.int32) * te
    tile_expert = jnp.sum(ends[None, :] <= tile_start[:, None], axis=1).astype(jnp.int32)
    last_expert = tile_expert[jnp.maximum(n_valid - 1, 0)]
    tile_expert = jnp.where(tile_start < ends[-1], tile_expert, last_expert)

    xs = _dispatch(dest0, dest1, h2, n_rows)
    ys = _expert_mlp(tile_expert, n_valid.reshape(1), xs, w_gate[0], w_up[0], w_down[0])
    out = _combine(dest0, dest1, x1, rinfo, g2, ffn_post_norm[0].reshape(1, d), ys, seq)
    return out.reshape(batch, seq, d)
```

```python
import functools

import jax
import jax.numpy as jnp
from jax import lax
from jax.experimental import pallas as pl
from jax.experimental.pallas import tpu as pltpu

F32 = jnp.float32
BF16 = jnp.bfloat16

GRID_W = 64
HEAD_DIM = 64
KV_GROUP = 4
WINDOW = 128
ROPE_THETA = 10000.0
N_GROUPS = 4
EXPERTS_PER_GROUP = 8
N_EXPERTS = N_GROUPS * EXPERTS_PER_GROUP
EPS = 1e-6
NEG_INF = -1e30
LOG2_E = 1.4426950408889634

LANES = 128
V7X_VMEM_LIMIT = 56 * 1024 * 1024

PROJ_TM = 512
ATTN_A_TQ = 512
ATTN_B_TQ = 256
OUT_TM = 256
EXPERT_TE = 256
MOVE_TM = 256


def _params(*sem):
    return pltpu.CompilerParams(dimension_semantics=sem, vmem_limit_bytes=V7X_VMEM_LIMIT)


def _dot(a, b):
    return jnp.dot(a, b, preferred_element_type=F32)


def _dot_nt(a, b):
    return lax.dot_general(a, b, (((1,), (1,)), ((), ())), preferred_element_type=F32)


def _rms(x):
    return x * lax.rsqrt(jnp.mean(x * x, axis=-1, keepdims=True) + EPS)


def _split_bf16(x):
    hi = x.astype(BF16)
    lo = (x - hi.astype(F32)).astype(BF16)
    return hi, lo


def _mod_kernel(c_ref, w_ref, b_ref, o_ref):
    cc = c_ref[...]
    s = cc * jax.nn.sigmoid(cc)
    s_hi, s_lo = _split_bf16(s)
    w_hi, w_lo = _split_bf16(w_ref[...])
    o_ref[...] = _dot(s_hi, w_hi) + _dot(s_lo, w_hi) + _dot(s_hi, w_lo) + b_ref[...]


def _modulation(cc, w_mod, b_mod):
    rows, d = cc.shape
    n = w_mod.shape[1]
    bn = 1024
    return pl.pallas_call(
        _mod_kernel,
        grid=(n // bn,),
        in_specs=[pl.BlockSpec((rows, d), lambda i: (0, 0)),
                  pl.BlockSpec((d, bn), lambda i: (0, i)),
                  pl.BlockSpec((1, bn), lambda i: (0, i))],
        out_specs=pl.BlockSpec((rows, bn), lambda i: (0, i)),
        out_shape=jax.ShapeDtypeStruct((rows, n), F32),
        compiler_params=_params("arbitrary"),
    )(cc, w_mod, b_mod.reshape(1, n))


def _rope(x, cos, sin_a, sin_b):
    return x * cos + pltpu.roll(x, LANES - 1, 1) * sin_a + pltpu.roll(x, 1, 1) * sin_b


def _head_norm(x, seg_ref, gain):
    ss = _dot((x * x).astype(BF16), seg_ref[...])
    return x * lax.rsqrt(ss * (1.0 / HEAD_DIM) + EPS) * gain


def _store_pair_variants(ref, t):
    lane = lax.broadcasted_iota(jnp.int32, t.shape, 1)
    lo = lane < HEAD_DIM
    sw = pltpu.roll(t, HEAD_DIM, 1)
    zero = jnp.zeros_like(t)
    ref[0] = jnp.where(lo, t, zero).astype(BF16)
    ref[1] = jnp.where(lo, zero, sw).astype(BF16)
    ref[2] = jnp.where(lo, sw, zero).astype(BF16)
    ref[3] = jnp.where(lo, zero, t).astype(BF16)


def _proj_kernel(x_ref, sc_ref, sh_ref, gpre_ref, w_ref, cos_ref, sa_ref, sb_ref,
                 qn_ref, kn_ref, seg_q_ref, seg_k_ref,
                 qa_ref, qb_ref, ka_ref, va_ref, kb_ref, vb_ref):
    h = _rms(x_ref[...]) * gpre_ref[...] * (1.0 + sc_ref[0]) + sh_ref[0]
    p = _dot(h.astype(BF16), w_ref[...])
    cos, sa, sb = cos_ref[...], sa_ref[...], sb_ref[...]
    nq = qa_ref.shape[1]
    q_scale = HEAD_DIM ** -0.5 * LOG2_E
    for c in range(nq // LANES):
        qa_ref[:, c * LANES:(c + 1) * LANES] = (
            _rope(p[:, c * LANES:(c + 1) * LANES], cos, sa, sb) * q_scale).astype(BF16)
    o = nq
    _store_pair_variants(ka_ref, _rope(p[:, o:o + LANES], cos, sa, sb))
    _store_pair_variants(va_ref, p[:, o + LANES:o + 2 * LANES])
    o += 2 * LANES
    qb = _head_norm(p[:, o:o + nq], seg_q_ref, qn_ref[...])
    for c in range(nq // LANES):
        qb_ref[:, c * LANES:(c + 1) * LANES] = (
            _rope(qb[:, c * LANES:(c + 1) * LANES], cos, sa, sb) * q_scale).astype(BF16)
    o += nq
    kb = _head_norm(p[:, o:o + LANES], seg_k_ref, kn_ref[...])
    _store_pair_variants(kb_ref, _rope(kb, cos, sa, sb))
    _store_pair_variants(vb_ref, p[:, o + LANES:o + 2 * LANES])


def _ctx_proj_kernel(x_ref, sc_ref, sh_ref, gpre_ref, w_ref, kn_ref, seg_k_ref,
                     ka_ref, va_ref, kb_ref, vb_ref):
    h = _rms(x_ref[...]) * gpre_ref[...] * (1.0 + sc_ref[...]) + sh_ref[...]
    p = _dot(h.astype(BF16), w_ref[...])
    _store_pair_variants(ka_ref, p[:, 0:LANES])
    _store_pair_variants(va_ref, p[:, LANES:2 * LANES])
    _store_pair_variants(kb_ref, _head_norm(p[:, 2 * LANES:3 * LANES], seg_k_ref, kn_ref[...]))
    _store_pair_variants(vb_ref, p[:, 3 * LANES:4 * LANES])


def _project_latents(x2, sc, sh, gpre, w_in, tables, qn, kn, seg_q, seg_k, seq):
    t, d = x2.shape
    tm = PROJ_TM
    tpb = seq // tm
    nq = seg_q.shape[0]
    const = lambda shape: pl.BlockSpec(shape, lambda i: (0,) * len(shape))
    per_batch = pl.BlockSpec((1, 1, d), lambda i: (i // tpb, 0, 0))
    table = pl.BlockSpec((tm, LANES), lambda i: (i % tpb, 0))
    kv_spec = pl.BlockSpec((4, tm, LANES), lambda i: (0, i, 0))
    kv_shape = jax.ShapeDtypeStruct((4, t, LANES), BF16)
    q_spec = pl.BlockSpec((tm, nq), lambda i: (i, 0))
    q_shape = jax.ShapeDtypeStruct((t, nq), BF16)
    return pl.pallas_call(
        _proj_kernel,
        grid=(t // tm,),
        in_specs=[pl.BlockSpec((tm, d), lambda i: (i, 0)), per_batch, per_batch, const((1, d)),
                  const(w_in.shape), table, table, table,
                  const((1, nq)), const((1, LANES)), const(seg_q.shape), const(seg_k.shape)],
        out_specs=[q_spec, q_spec, kv_spec, kv_spec, kv_spec, kv_spec],
        out_shape=[q_shape, q_shape, kv_shape, kv_shape, kv_shape, kv_shape],
        compiler_params=_params("arbitrary"),
    )(x2, sc, sh, gpre, w_in, *tables, qn, kn, seg_q, seg_k)


def _project_context(c2, sc, sh, gpre, w_kv, kn, seg_k, ctx_len):
    t, d = c2.shape
    const = lambda shape: pl.BlockSpec(shape, lambda i: (0,) * len(shape))
    kv_spec = pl.BlockSpec((4, ctx_len, LANES), lambda i: (0, i, 0))
    kv_shape = jax.ShapeDtypeStruct((4, t, LANES), BF16)
    return pl.pallas_call(
        _ctx_proj_kernel,
        grid=(t // ctx_len,),
        in_specs=[pl.BlockSpec((ctx_len, d), lambda i: (i, 0)), const((1, d)), const((1, d)),
                  const((1, d)), const(w_kv.shape), const((1, LANES)), const(seg_k.shape)],
        out_specs=[kv_spec] * 4,
        out_shape=[kv_shape] * 4,
        compiler_params=_params("arbitrary"),
    )(c2, sc, sh, gpre, w_kv, kn, seg_k)


def _attn_b_kernel(q_ref, k_ref, v_ref, kc_ref, vc_ref, o_ref):
    n_pairs = q_ref.shape[1] // LANES
    pairs_per_kv = KV_GROUP // 2
    for pair in range(n_pairs):
        g = pair // pairs_per_kv
        q2 = q_ref[:, pair * LANES:(pair + 1) * LANES]
        acc = None
        for half in range(2):
            kv = 2 * g + half
            s = _dot_nt(q2, k_ref[kv])
            sc = _dot_nt(q2, kc_ref[kv])
            m = jnp.maximum(jnp.max(s, axis=-1, keepdims=True), jnp.max(sc, axis=-1, keepdims=True))
            p = jnp.exp2(s - m)
            pc = jnp.exp2(sc - m)
            denom = jnp.sum(p, axis=-1, keepdims=True) + jnp.sum(pc, axis=-1, keepdims=True)
            o = (_dot(p.astype(BF16), v_ref[kv]) + _dot(pc.astype(BF16), vc_ref[kv])) / denom
            acc = o if acc is None else acc + o
        o_ref[:, pair * LANES:(pair + 1) * LANES] = acc.astype(BF16)


def _attention_b(qb, kb4, vb4, kcb4, vcb4, batch, seq, ctx_len):
    t, nq = qb.shape
    tq = ATTN_B_TQ
    nqb = seq // tq
    lat = pl.BlockSpec((4, seq, LANES), lambda b, i: (0, b, 0))
    cx = pl.BlockSpec((4, ctx_len, LANES), lambda b, i: (0, b, 0))
    qs = pl.BlockSpec((tq, nq), lambda b, i: (b * nqb + i, 0))
    return pl.pallas_call(
        _attn_b_kernel,
        grid=(batch, nqb),
        in_specs=[qs, lat, lat, cx, cx],
        out_specs=qs,
        out_shape=jax.ShapeDtypeStruct((t, nq), BF16),
        compiler_params=_params("arbitrary", "arbitrary"),
    )(qb, kb4, vb4, kcb4, vcb4)


def _attn_a_kernel(sink_ref, q_ref, k_ref, v_ref, kc_ref, vc_ref, o_ref, bias_ref, *, seq):
    i = pl.program_id(1)
    tq = q_ref.shape[0]
    win = tq + 2 * WINDOW
    start = pl.multiple_of(jnp.clip(i * tq - WINDOW, 0, seq - win), WINDOW)
    qpos = i * tq + lax.broadcasted_iota(jnp.int32, (tq, win), 0)
    kpos = start + lax.broadcasted_iota(jnp.int32, (tq, win), 1)
    bias_ref[...] = jnp.where(jnp.abs(kpos - qpos) <= WINDOW, 0.0, NEG_INF).astype(F32)
    n_pairs = q_ref.shape[1] // LANES
    pairs_per_kv = KV_GROUP // 2
    rowmax = lambda s: jnp.max(s, axis=-1, keepdims=True)
    rowsum = lambda s: jnp.sum(s, axis=-1, keepdims=True)
    for pair in range(n_pairs):
        g = pair // pairs_per_kv
        q2 = q_ref[:, pair * LANES:(pair + 1) * LANES]
        acc = None
        for half in range(2):
            kv = 2 * g + half
            sink = sink_ref[2 * pair + half] * LOG2_E
            s = _dot_nt(q2, k_ref[kv, pl.ds(start, win), :]) + bias_ref[...]
            sc = _dot_nt(q2, kc_ref[kv])
            m = jnp.maximum(jnp.maximum(rowmax(s), rowmax(sc)), sink)
            p = jnp.exp2(s - m)
            pc = jnp.exp2(sc - m)
            denom = rowsum(p) + rowsum(pc) + jnp.exp2(sink - m)
            o = (_dot(p.astype(BF16), v_ref[kv, pl.ds(start, win), :])
                 + _dot(pc.astype(BF16), vc_ref[kv])) / denom
            acc = o if acc is None else acc + o
        o_ref[:, pair * LANES:(pair + 1) * LANES] = acc.astype(BF16)


def _attention_a(sink, qa, ka4, va4, kca4, vca4, batch, seq, ctx_len):
    t, nq = qa.shape
    tq = ATTN_A_TQ
    nqb = seq // tq
    lat = pl.BlockSpec((4, seq, LANES), lambda b, i: (0, b, 0))
    cx = pl.BlockSpec((4, ctx_len, LANES), lambda b, i: (0, b, 0))
    qs = pl.BlockSpec((tq, nq), lambda b, i: (b * nqb + i, 0))
    return pl.pallas_call(
        functools.partial(_attn_a_kernel, seq=seq),
        grid=(batch, nqb),
        in_specs=[pl.BlockSpec(memory_space=pltpu.SMEM), qs, lat, lat, cx, cx],
        out_specs=qs,
        out_shape=jax.ShapeDtypeStruct((t, nq), BF16),
        scratch_shapes=[pltpu.VMEM((tq, tq + 2 * WINDOW), F32)],
        compiler_params=_params("arbitrary", "arbitrary"),
    )(sink, qa, ka4, va4, kca4, vca4)


def _out_kernel(oa_ref, ob_ref, x_ref, g1_ref, sc2_ref, sh2_ref, ga_ref, gb_ref, gpost_ref,
                gpre2_ref, woa_ref, wob_ref, wrh_ref, wrl_ref, br_ref,
                x1_ref, h2_ref, rinfo_ref, rt_ref, cnt_ref, carry_ref):
    step = pl.program_id(0)

    @pl.when(step == 0)
    def _():
        carry_ref[...] = jnp.zeros_like(carry_ref)

    na = _rms(oa_ref[...].astype(F32)) * ga_ref[...]
    nb = _rms(ob_ref[...].astype(F32)) * gb_ref[...]
    ox = _dot(na.astype(BF16), woa_ref[...]) + _dot(nb.astype(BF16), wob_ref[...])
    x1 = x_ref[...] + g1_ref[0] * (_rms(ox) * gpost_ref[...])
    x1_ref[...] = x1
    h2 = _rms(x1) * gpre2_ref[...] * (1.0 + sc2_ref[0]) + sh2_ref[0]
    h2_ref[...] = h2

    h_hi, h_lo = _split_bf16(h2)
    logits = (_dot(h_hi, wrh_ref[...]) + _dot(h_lo, wrh_ref[...]) + _dot(h_hi, wrl_ref[...])
              + br_ref[...])
    tm = logits.shape[0]
    lane = lax.broadcasted_iota(jnp.int32, logits.shape, 1)
    lanef = lane.astype(F32)
    big = jnp.float32(1e9)
    ninf = jnp.float32(-jnp.inf)
    rowmax = lambda v: jnp.max(v, axis=-1, keepdims=True)
    rowmin = lambda v: jnp.min(v, axis=-1, keepdims=True)
    rowsum = lambda v: jnp.sum(v, axis=-1, keepdims=True)

    gmask = (lane >= N_EXPERTS) & (lane < N_EXPERTS + N_GROUPS)
    lg = jnp.where(gmask, logits, ninf)
    gmax = rowmax(lg)
    gidx = rowmin(jnp.where(lg == gmax, lanef, big)) - N_EXPERTS
    g_w = 1.0 / rowsum(jnp.exp(lg - gmax))
    lane_group = (lane // EXPERTS_PER_GROUP).astype(F32)
    emask = (lane < N_EXPERTS) & (lane_group == gidx)
    le = jnp.where(emask, logits, ninf)
    m1 = rowmax(le)
    i1 = rowmin(jnp.where(le == m1, lanef, big))
    le2 = jnp.where(lanef == i1, ninf, le)
    m2 = rowmax(le2)
    i2 = rowmin(jnp.where(le2 == m2, lanef, big))
    e2 = jnp.exp(m2 - m1)
    w0 = g_w / (1.0 + e2)
    w1 = g_w * e2 / (1.0 + e2)

    hit1 = lanef == i1
    hit2 = lanef == i2
    onehot = jnp.where(hit1 | hit2, 1.0, 0.0).astype(F32)
    r = lax.broadcasted_iota(jnp.int32, (tm, tm), 0)
    c = lax.broadcasted_iota(jnp.int32, (tm, tm), 1)
    strict_lower = jnp.where(r > c, 1.0, 0.0).astype(BF16)
    before = _dot(strict_lower, onehot.astype(BF16)) + carry_ref[...]
    rank0 = rowsum(jnp.where(hit1, before, 0.0))
    rank1 = rowsum(jnp.where(hit2, before, 0.0))
    carry_ref[...] += jnp.sum(onehot, axis=0, keepdims=True)
    cnt_ref[...] = carry_ref[...]

    info = jnp.zeros_like(logits)
    for k, val in enumerate((i1, i2, rank0, rank1, w0, w1)):
        info = jnp.where(lane == k, val, info)
    rinfo_ref[...] = info
    rt_ref[...] = info.T[0:8, :]


def _out_and_route(oa, ob, x2, g1, sc2, sh2, ga, gb, gpost, gpre2, woa, wob, wrh, wrl, br, seq):
    t, d = x2.shape
    tm = OUT_TM
    tpb = seq // tm
    nq = oa.shape[1]
    const = lambda shape: pl.BlockSpec(shape, lambda i: (0,) * len(shape))
    per_batch = pl.BlockSpec((1, 1, d), lambda i: (i // tpb, 0, 0))
    rows = lambda n: pl.BlockSpec((tm, n), lambda i: (i, 0))
    return pl.pallas_call(
        _out_kernel,
        grid=(t // tm,),
        in_specs=[rows(nq), rows(nq), rows(d), per_batch, per_batch, per_batch,
                  const((1, nq)), const((1, nq)), const((1, d)), const((1, d)),
                  const(woa.shape), const(wob.shape), const(wrh.shape), const(wrl.shape),
                  const((1, LANES))],
        out_specs=[rows(d), rows(d), rows(LANES), pl.BlockSpec((8, tm), lambda i: (0, i)),
                   const((1, LANES))],
        out_shape=[jax.ShapeDtypeStruct((t, d), F32), jax.ShapeDtypeStruct((t, d), F32),
                   jax.ShapeDtypeStruct((t, LANES), F32), jax.ShapeDtypeStruct((8, t), F32),
                   jax.ShapeDtypeStruct((1, LANES), F32)],
        scratch_shapes=[pltpu.VMEM((1, LANES), F32)],
        compiler_params=_params("arbitrary"),
    )(oa, ob, x2, g1, sc2, sh2, ga, gb, gpost, gpre2, woa, wob, wrh, wrl, br)


def _dest_kernel(offs_ref, rt_ref, d_ref):
    rt = rt_ref[...]
    base = jnp.zeros_like(rt)
    for e in range(N_EXPERTS):
        base = jnp.where(rt == float(e), offs_ref[e].astype(F32), base)
    d_ref[...] = (base + pltpu.roll(rt, 6, 0)).astype(jnp.int32)


def _destinations(offs, rt):
    return pl.pallas_call(
        _dest_kernel,
        in_specs=[pl.BlockSpec(memory_space=pltpu.SMEM), pl.BlockSpec(rt.shape, lambda: (0, 0))],
        out_specs=pl.BlockSpec(rt.shape, lambda: (0, 0)),
        out_shape=jax.ShapeDtypeStruct(rt.shape, jnp.int32),
    )(offs, rt)


def _dispatch_kernel(d0_ref, d1_ref, ends_ref, nv_ref, h_ref, xs_ref, zero_ref, sem, zsem,
                     *, te, n_tiles):
    tm = h_ref.shape[0]
    base = pl.program_id(0) * tm

    @pl.when(pl.program_id(0) == 0)
    def _():
        zero_ref[...] = jnp.zeros_like(zero_ref)

        def pad_copy(row0):
            return pltpu.make_async_copy(zero_ref, xs_ref.at[pl.ds(pl.multiple_of(row0, te), te)], zsem)

        def for_each_pad_tile(fn):
            def expert_pad(e, carry):
                end = ends_ref[e]
                prev = jnp.where(e > 0, ends_ref[jnp.maximum(e - 1, 0)], 0)

                @pl.when(end > prev)
                def _():
                    fn(pad_copy(end - te))
                return carry

            def tail_pad(j, carry):
                fn(pad_copy(j * te))
                return carry

            lax.fori_loop(0, N_EXPERTS, expert_pad, 0)
            lax.fori_loop(nv_ref[0], n_tiles, tail_pad, 0)

        for_each_pad_tile(lambda cp: cp.start())
        for_each_pad_tile(lambda cp: cp.wait())

    def row_copy(r, dest):
        return pltpu.make_async_copy(h_ref.at[pl.ds(r, 1)], xs_ref.at[pl.ds(dest, 1)], sem)

    def issue(r, carry):
        row_copy(r, d0_ref[base + r]).start()
        row_copy(r, d1_ref[base + r]).start()
        return carry

    lax.fori_loop(0, tm, issue, 0, unroll=8)
    for _ in range(2):
        pltpu.make_async_copy(h_ref, xs_ref.at[pl.ds(0, tm)], sem).wait()


def _dispatch(dest0, dest1, ends, n_valid, h2, n_rows):
    t, d = h2.shape
    tm = MOVE_TM
    te = EXPERT_TE
    return pl.pallas_call(
        functools.partial(_dispatch_kernel, te=te, n_tiles=n_rows // te),
        grid_spec=pltpu.PrefetchScalarGridSpec(
            num_scalar_prefetch=4,
            grid=(t // tm,),
            in_specs=[pl.BlockSpec((tm, d), lambda i, *_: (i, 0))],
            out_specs=pl.BlockSpec(memory_space=pl.ANY),
            scratch_shapes=[pltpu.VMEM((te, d), F32), pltpu.SemaphoreType.DMA(()),
                            pltpu.SemaphoreType.DMA(())]),
        out_shape=jax.ShapeDtypeStruct((n_rows, d), F32),
        compiler_params=_params("arbitrary"),
    )(dest0, dest1, ends, n_valid, h2)


def _expert_kernel(te_ref, nv_ref, xs_ref, wg_ref, wu_ref, wd_ref, ys_ref, wg_bf, wu_bf, wd_bf):
    j = pl.program_id(0)
    valid = j < nv_ref[0]
    changed = (j == 0) | (te_ref[j] != te_ref[jnp.maximum(j - 1, 0)])

    @pl.when(valid & changed)
    def _():
        wg_bf[...] = wg_ref[0].astype(BF16)
        wu_bf[...] = wu_ref[0].astype(BF16)
        wd_bf[...] = wd_ref[0].astype(BF16)

    @pl.when(valid)
    def _():
        xb = xs_ref[...].astype(BF16)
        gate = _dot(xb, wg_bf[...])
        up = _dot(xb, wu_bf[...])
        act = gate * jax.nn.sigmoid(gate) * up
        ys_ref[...] = _dot(act.astype(BF16), wd_bf[...])

    @pl.when(jnp.logical_not(valid))
    def _():
        ys_ref[...] = jnp.zeros_like(ys_ref)


def _expert_mlp(tile_expert, n_valid, xs, w_gate, w_up, w_down):
    n_rows, d = xs.shape
    te = EXPERT_TE
    ff = w_gate.shape[2]
    tile = lambda j, e, nv: (jnp.minimum(j, nv[0] - 1), 0)
    wsel = lambda j, e, nv: (e[j], 0, 0)
    return pl.pallas_call(
        _expert_kernel,
        grid_spec=pltpu.PrefetchScalarGridSpec(
            num_scalar_prefetch=2,
            grid=(n_rows // te,),
            in_specs=[pl.BlockSpec((te, d), tile),
                      pl.BlockSpec((1, d, ff), wsel), pl.BlockSpec((1, d, ff), wsel),
                      pl.BlockSpec((1, ff, d), wsel)],
            out_specs=pl.BlockSpec((te, d), lambda j, e, nv: (j, 0)),
            scratch_shapes=[pltpu.VMEM((d, ff), BF16), pltpu.VMEM((d, ff), BF16),
                            pltpu.VMEM((ff, d), BF16)]),
        out_shape=jax.ShapeDtypeStruct((n_rows, d), F32),
        compiler_params=_params("arbitrary"),
    )(tile_expert, n_valid, xs, w_gate, w_up, w_down)


def _combine_kernel(d0_ref, d1_ref, x1_ref, rinfo_ref, g2_ref, gpost_ref, ys_ref, o_ref,
                    b0_ref, b1_ref, sem):
    tm = x1_ref.shape[0]
    base = pl.program_id(0) * tm

    def row_copy(buf, r, src):
        return pltpu.make_async_copy(ys_ref.at[pl.ds(src, 1)], buf.at[pl.ds(r, 1)], sem)

    def issue(r, carry):
        row_copy(b0_ref, r, d0_ref[base + r]).start()
        row_copy(b1_ref, r, d1_ref[base + r]).start()
        return carry

    lax.fori_loop(0, tm, issue, 0, unroll=8)
    for buf in (b0_ref, b1_ref):
        pltpu.make_async_copy(ys_ref.at[pl.ds(0, tm)], buf, sem).wait()
    info = rinfo_ref[...]
    fx = info[:, 4:5] * b0_ref[...] + info[:, 5:6] * b1_ref[...]
    o_ref[...] = x1_ref[...] + g2_ref[0] * (_rms(fx) * gpost_ref[...])


def _combine(dest0, dest1, x1, rinfo, g2, gpost, ys, seq):
    t, d = x1.shape
    tm = MOVE_TM
    tpb = seq // tm
    return pl.pallas_call(
        _combine_kernel,
        grid_spec=pltpu.PrefetchScalarGridSpec(
            num_scalar_prefetch=2,
            grid=(t // tm,),
            in_specs=[pl.BlockSpec((tm, d), lambda i, d0, d1: (i, 0)),
                      pl.BlockSpec((tm, LANES), lambda i, d0, d1: (i, 0)),
                      pl.BlockSpec((1, 1, d), lambda i, d0, d1: (i // tpb, 0, 0)),
                      pl.BlockSpec((1, d), lambda i, d0, d1: (0, 0)),
                      pl.BlockSpec(memory_space=pl.ANY)],
            out_specs=pl.BlockSpec((tm, d), lambda i, d0, d1: (i, 0)),
            scratch_shapes=[pltpu.VMEM((tm, d), F32), pltpu.VMEM((tm, d), F32),
                            pltpu.SemaphoreType.DMA(())]),
        out_shape=jax.ShapeDtypeStruct((t, d), F32),
        compiler_params=_params("arbitrary"),
    )(dest0, dest1, x1, rinfo, g2, gpost, ys)


def _rope_tables(seq):
    pos = jnp.arange(seq, dtype=jnp.int32)
    row = (pos // GRID_W).astype(F32)
    col = (pos % GRID_W).astype(F32)
    axis_dim = HEAD_DIM // 2
    inv_freq = ROPE_THETA ** (-jnp.arange(0, axis_dim, 2, dtype=F32) / axis_dim)
    ang = jnp.concatenate([row[:, None] * inv_freq, col[:, None] * inv_freq], axis=-1)
    pair = (jnp.arange(LANES) % HEAD_DIM) // 2
    cos = jnp.cos(ang)[:, pair]
    sin = jnp.sin(ang)[:, pair]
    even = (jnp.arange(LANES) % 2) == 0
    return cos, jnp.where(even, -sin, 0.0), jnp.where(even, 0.0, sin)


def _segment_ones(n):
    seg = jnp.arange(n) // HEAD_DIM
    return (seg[:, None] == seg[None, :]).astype(BF16)


def kernel(x, c, ctx, c_ctx, w_mod, b_mod, attn_pre_norm, attn_post_norm, w_in, a_sink,
           b_q_norm, b_k_norm, a_out_norm, b_out_norm, w_out, ffn_pre_norm, ffn_post_norm,
           w_group, b_group, w_router, b_router, w_gate, w_up, w_down):
    batch, seq, d = x.shape
    ctx_len = ctx.shape[1]
    assert w_mod.shape[0] == 1, "single-layer stack only (context stream is never updated)"
    assert seq % ATTN_A_TQ == 0 and seq >= ATTN_A_TQ + 2 * WINDOW
    assert seq % PROJ_TM == 0 and seq % ATTN_B_TQ == 0 and seq % OUT_TM == 0 and seq % MOVE_TM == 0
    t = batch * seq
    nq = d // 2
    nkv = nq // KV_GROUP
    assert nkv == LANES and w_in.shape[2] == 2 * nq + 4 * nkv

    cc = jnp.concatenate([c, c_ctx[None, :], jnp.zeros((16 - batch - 1, d), F32)], axis=0)
    mod = _modulation(cc, w_mod[0], b_mod[0])
    sh1, sc1, g1, sh2, sc2, g2 = (m.reshape(batch, 1, d) for m in jnp.split(mod[:batch], 6, axis=-1))
    csh1, csc1 = (m.reshape(1, d) for m in jnp.split(mod[batch], 6)[:2])

    x2 = x.reshape(t, d)
    c2 = ctx.reshape(batch * ctx_len, d)
    gpre = attn_pre_norm[0].reshape(1, d)
    w_in_bf = w_in[0].astype(BF16)
    kv_cols = jnp.concatenate([w_in_bf[:, nq:nq + 2 * nkv], w_in_bf[:, 2 * nq + 2 * nkv:]], axis=1)
    qn = jnp.tile(b_q_norm[0], nq // HEAD_DIM).reshape(1, nq)
    kn = jnp.tile(b_k_norm[0], nkv // HEAD_DIM).reshape(1, nkv)
    seg_q, seg_k = _segment_ones(nq), _segment_ones(nkv)
    qa, qb, ka4, va4, kb4, vb4 = _project_latents(
        x2, sc1, sh1, gpre, w_in_bf, _rope_tables(seq), qn, kn, seg_q, seg_k, seq)
    kca4, vca4, kcb4, vcb4 = _project_context(c2, csc1, csh1, gpre, kv_cols, kn, seg_k, ctx_len)

    oa = _attention_a(a_sink[0], qa, ka4, va4, kca4, vca4, batch, seq, ctx_len)
    ob = _attention_b(qb, kb4, vb4, kcb4, vcb4, batch, seq, ctx_len)

    w_out_bf = w_out[0].astype(BF16)
    w_r = jnp.zeros((d, LANES), F32)
    w_r = w_r.at[:, :N_EXPERTS].set(w_router[0]).at[:, N_EXPERTS:N_EXPERTS + N_GROUPS].set(w_group[0])
    w_r_hi = w_r.astype(BF16)
    w_r_lo = (w_r - w_r_hi.astype(F32)).astype(BF16)
    b_r = jnp.zeros((1, LANES), F32)
    b_r = b_r.at[0, :N_EXPERTS].set(b_router[0]).at[0, N_EXPERTS:N_EXPERTS + N_GROUPS].set(b_group[0])
    x1, h2, rinfo, rt, counts = _out_and_route(
        oa, ob, x2, g1, sc2, sh2, a_out_norm[0].reshape(1, nq), b_out_norm[0].reshape(1, nq),
        attn_post_norm[0].reshape(1, d), ffn_pre_norm[0].reshape(1, d),
        w_out_bf[:nq], w_out_bf[nq:], w_r_hi, w_r_lo, b_r, seq)

    te = EXPERT_TE
    n_tiles = -(-(2 * t + N_EXPERTS * (te - 1)) // te)
    n_rows = n_tiles * te
    cnt = counts[0, :N_EXPERTS].astype(jnp.int32)
    padded = ((cnt + te - 1) // te) * te
    ends = jnp.cumsum(padded)
    offs = ends - padded
    dest = _destinations(offs, rt)
    dest0, dest1 = dest[0], dest[1]
    n_valid = (ends[-1] // te).astype(jnp.int32).reshape(1)
    tile_start = jnp.arange(n_tiles, dtype=jnp.int32) * te
    tile_expert = jnp.sum(ends[None, :] <= tile_start[:, None], axis=1).astype(jnp.int32)
    last_expert = tile_expert[jnp.maximum(n_valid[0] - 1, 0)]
    tile_expert = jnp.where(tile_start < ends[-1], tile_expert, last_expert)

    xs = _dispatch(dest0, dest1, ends.astype(jnp.int32), n_valid, h2, n_rows)
    ys = _expert_mlp(tile_expert, n_valid, xs, w_gate[0], w_up[0], w_down[0])
    out = _combine(dest0, dest1, x1, rinfo, g2, ffn_post_norm[0].reshape(1, d), ys, seq)
    return out.reshape(batch, seq, d)
```

```python
import functools

import jax
import jax.numpy as jnp
from jax import lax
from jax.experimental import pallas as pl
from jax.experimental.pallas import tpu as pltpu

F32 = jnp.float32
BF16 = jnp.bfloat16

GRID_W = 64
HEAD_DIM = 64
KV_GROUP = 4
WINDOW = 128
ROPE_THETA = 10000.0
N_GROUPS = 4
EXPERTS_PER_GROUP = 8
N_EXPERTS = N_GROUPS * EXPERTS_PER_GROUP
EPS = 1e-6
NEG_INF = -1e30
LOG2_E = 1.4426950408889634

LANES = 128
V7X_VMEM_LIMIT = 56 * 1024 * 1024

PROJ_TM = 512
ATTN_A_TQ = 512
ATTN_B_TQ = 256
OUT_TM = 512
EXPERT_TE = 256
MOVE_TM = OUT_TM


def _params(*sem):
    return pltpu.CompilerParams(dimension_semantics=sem, vmem_limit_bytes=V7X_VMEM_LIMIT)


def _dot(a, b):
    return jnp.dot(a, b, preferred_element_type=F32)


def _dot_nt(a, b):
    return lax.dot_general(a, b, (((1,), (1,)), ((), ())), preferred_element_type=F32)


def _rms(x):
    return x * lax.rsqrt(jnp.mean(x * x, axis=-1, keepdims=True) + EPS)


def _split_bf16(x):
    hi = x.astype(BF16)
    lo = (x - hi.astype(F32)).astype(BF16)
    return hi, lo


def _mod_kernel(c_ref, w_ref, b_ref, o_ref):
    cc = c_ref[...]
    s = cc * jax.nn.sigmoid(cc)
    s_hi, s_lo = _split_bf16(s)
    w_hi, w_lo = _split_bf16(w_ref[...])
    o_ref[...] = _dot(s_hi, w_hi) + _dot(s_lo, w_hi) + _dot(s_hi, w_lo) + b_ref[...]


def _modulation(cc, w_mod, b_mod):
    rows, d = cc.shape
    n = w_mod.shape[1]
    bn = 1024
    return pl.pallas_call(
        _mod_kernel,
        grid=(n // bn,),
        in_specs=[pl.BlockSpec((rows, d), lambda i: (0, 0)),
                  pl.BlockSpec((d, bn), lambda i: (0, i)),
                  pl.BlockSpec((1, bn), lambda i: (0, i))],
        out_specs=pl.BlockSpec((rows, bn), lambda i: (0, i)),
        out_shape=jax.ShapeDtypeStruct((rows, n), F32),
        compiler_params=_params("arbitrary"),
    )(cc, w_mod, b_mod.reshape(1, n))


def _rope(x, cos, sin_a, sin_b):
    return x * cos + pltpu.roll(x, LANES - 1, 1) * sin_a + pltpu.roll(x, 1, 1) * sin_b


def _head_norm(x, seg_ref, gain):
    ss = _dot((x * x).astype(BF16), seg_ref[...])
    return x * lax.rsqrt(ss * (1.0 / HEAD_DIM) + EPS) * gain


def _store_pair_variants(ref, t):
    lane = lax.broadcasted_iota(jnp.int32, t.shape, 1)
    lo = lane < HEAD_DIM
    sw = pltpu.roll(t, HEAD_DIM, 1)
    zero = jnp.zeros_like(t)
    ref[0] = jnp.where(lo, t, zero).astype(BF16)
    ref[1] = jnp.where(lo, zero, sw).astype(BF16)
    ref[2] = jnp.where(lo, sw, zero).astype(BF16)
    ref[3] = jnp.where(lo, zero, t).astype(BF16)


def _proj_kernel(x_ref, sc_ref, sh_ref, gpre_ref, w_ref, cos_ref, sa_ref, sb_ref,
                 qn_ref, kn_ref, seg_q_ref, seg_k_ref,
                 qa_ref, qb_ref, ka_ref, va_ref, kb_ref, vb_ref):
    h = _rms(x_ref[...]) * gpre_ref[...] * (1.0 + sc_ref[0]) + sh_ref[0]
    p = _dot(h.astype(BF16), w_ref[...])
    cos, sa, sb = cos_ref[...], sa_ref[...], sb_ref[...]
    nq = qa_ref.shape[1]
    q_scale = HEAD_DIM ** -0.5 * LOG2_E
    for c in range(nq // LANES):
        qa_ref[:, c * LANES:(c + 1) * LANES] = (
            _rope(p[:, c * LANES:(c + 1) * LANES], cos, sa, sb) * q_scale).astype(BF16)
    o = nq
    _store_pair_variants(ka_ref, _rope(p[:, o:o + LANES], cos, sa, sb))
    _store_pair_variants(va_ref, p[:, o + LANES:o + 2 * LANES])
    o += 2 * LANES
    qb = _head_norm(p[:, o:o + nq], seg_q_ref, qn_ref[...])
    for c in range(nq // LANES):
        qb_ref[:, c * LANES:(c + 1) * LANES] = (
            _rope(qb[:, c * LANES:(c + 1) * LANES], cos, sa, sb) * q_scale).astype(BF16)
    o += nq
    kb = _head_norm(p[:, o:o + LANES], seg_k_ref, kn_ref[...])
    _store_pair_variants(kb_ref, _rope(kb, cos, sa, sb))
    _store_pair_variants(vb_ref, p[:, o + LANES:o + 2 * LANES])


def _ctx_proj_kernel(x_ref, sc_ref, sh_ref, gpre_ref, w_ref, kn_ref, seg_k_ref,
                     ka_ref, va_ref, kb_ref, vb_ref):
    h = _rms(x_ref[...]) * gpre_ref[...] * (1.0 + sc_ref[...]) + sh_ref[...]
    p = _dot(h.astype(BF16), w_ref[...])
    _store_pair_variants(ka_ref, p[:, 0:LANES])
    _store_pair_variants(va_ref, p[:, LANES:2 * LANES])
    _store_pair_variants(kb_ref, _head_norm(p[:, 2 * LANES:3 * LANES], seg_k_ref, kn_ref[...]))
    _store_pair_variants(vb_ref, p[:, 3 * LANES:4 * LANES])


def _project_latents(x2, sc, sh, gpre, w_in, tables, qn, kn, seg_q, seg_k, seq):
    t, d = x2.shape
    tm = PROJ_TM
    tpb = seq // tm
    nq = seg_q.shape[0]
    const = lambda shape: pl.BlockSpec(shape, lambda i: (0,) * len(shape))
    per_batch = pl.BlockSpec((1, 1, d), lambda i: (i // tpb, 0, 0))
    table = pl.BlockSpec((tm, LANES), lambda i: (i % tpb, 0))
    kv_spec = pl.BlockSpec((4, tm, LANES), lambda i: (0, i, 0))
    kv_shape = jax.ShapeDtypeStruct((4, t, LANES), BF16)
    q_spec = pl.BlockSpec((tm, nq), lambda i: (i, 0))
    q_shape = jax.ShapeDtypeStruct((t, nq), BF16)
    return pl.pallas_call(
        _proj_kernel,
        grid=(t // tm,),
        in_specs=[pl.BlockSpec((tm, d), lambda i: (i, 0)), per_batch, per_batch, const((1, d)),
                  const(w_in.shape), table, table, table,
                  const((1, nq)), const((1, LANES)), const(seg_q.shape), const(seg_k.shape)],
        out_specs=[q_spec, q_spec, kv_spec, kv_spec, kv_spec, kv_spec],
        out_shape=[q_shape, q_shape, kv_shape, kv_shape, kv_shape, kv_shape],
        compiler_params=_params("arbitrary"),
    )(x2, sc, sh, gpre, w_in, *tables, qn, kn, seg_q, seg_k)


def _project_context(c2, sc, sh, gpre, w_kv, kn, seg_k, ctx_len):
    t, d = c2.shape
    const = lambda shape: pl.BlockSpec(shape, lambda i: (0,) * len(shape))
    kv_spec = pl.BlockSpec((4, ctx_len, LANES), lambda i: (0, i, 0))
    kv_shape = jax.ShapeDtypeStruct((4, t, LANES), BF16)
    return pl.pallas_call(
        _ctx_proj_kernel,
        grid=(t // ctx_len,),
        in_specs=[pl.BlockSpec((ctx_len, d), lambda i: (i, 0)), const((1, d)), const((1, d)),
                  const((1, d)), const(w_kv.shape), const((1, LANES)), const(seg_k.shape)],
        out_specs=[kv_spec] * 4,
        out_shape=[kv_shape] * 4,
        compiler_params=_params("arbitrary"),
    )(c2, sc, sh, gpre, w_kv, kn, seg_k)


def _attn_b_kernel(q_ref, k_ref, v_ref, kc_ref, vc_ref, o_ref):
    n_pairs = q_ref.shape[1] // LANES
    pairs_per_kv = KV_GROUP // 2
    for pair in range(n_pairs):
        g = pair // pairs_per_kv
        q2 = q_ref[:, pair * LANES:(pair + 1) * LANES]
        acc = None
        for half in range(2):
            kv = 2 * g + half
            s = _dot_nt(q2, k_ref[kv])
            sc = _dot_nt(q2, kc_ref[kv])
            m = jnp.maximum(jnp.max(s, axis=-1, keepdims=True), jnp.max(sc, axis=-1, keepdims=True))
            p = jnp.exp2(s - m)
            pc = jnp.exp2(sc - m)
            denom = jnp.sum(p, axis=-1, keepdims=True) + jnp.sum(pc, axis=-1, keepdims=True)
            o = (_dot(p.astype(BF16), v_ref[kv]) + _dot(pc.astype(BF16), vc_ref[kv])) / denom
            acc = o if acc is None else acc + o
        o_ref[:, pair * LANES:(pair + 1) * LANES] = acc.astype(BF16)


def _attention_b(qb, kb4, vb4, kcb4, vcb4, batch, seq, ctx_len):
    t, nq = qb.shape
    tq = ATTN_B_TQ
    nqb = seq // tq
    lat = pl.BlockSpec((4, seq, LANES), lambda b, i: (0, b, 0))
    cx = pl.BlockSpec((4, ctx_len, LANES), lambda b, i: (0, b, 0))
    qs = pl.BlockSpec((tq, nq), lambda b, i: (b * nqb + i, 0))
    return pl.pallas_call(
        _attn_b_kernel,
        grid=(batch, nqb),
        in_specs=[qs, lat, lat, cx, cx],
        out_specs=qs,
        out_shape=jax.ShapeDtypeStruct((t, nq), BF16),
        compiler_params=_params("arbitrary", "arbitrary"),
    )(qb, kb4, vb4, kcb4, vcb4)


def _attn_a_kernel(sink_ref, q_ref, k_ref, v_ref, kc_ref, vc_ref, o_ref, bias_ref, *, seq):
    i = pl.program_id(1)
    tq = q_ref.shape[0]
    win = tq + 2 * WINDOW
    start = pl.multiple_of(jnp.clip(i * tq - WINDOW, 0, seq - win), WINDOW)
    qpos = i * tq + lax.broadcasted_iota(jnp.int32, (tq, win), 0)
    kpos = start + lax.broadcasted_iota(jnp.int32, (tq, win), 1)
    bias_ref[...] = jnp.where(jnp.abs(kpos - qpos) <= WINDOW, 0.0, NEG_INF).astype(F32)
    n_pairs = q_ref.shape[1] // LANES
    pairs_per_kv = KV_GROUP // 2
    rowmax = lambda s: jnp.max(s, axis=-1, keepdims=True)
    rowsum = lambda s: jnp.sum(s, axis=-1, keepdims=True)
    for pair in range(n_pairs):
        g = pair // pairs_per_kv
        q2 = q_ref[:, pair * LANES:(pair + 1) * LANES]
        acc = None
        for half in range(2):
            kv = 2 * g + half
            sink = sink_ref[2 * pair + half] * LOG2_E
            s = _dot_nt(q2, k_ref[kv, pl.ds(start, win), :]) + bias_ref[...]
            sc = _dot_nt(q2, kc_ref[kv])
            m = jnp.maximum(jnp.maximum(rowmax(s), rowmax(sc)), sink)
            p = jnp.exp2(s - m)
            pc = jnp.exp2(sc - m)
            denom = rowsum(p) + rowsum(pc) + jnp.exp2(sink - m)
            o = (_dot(p.astype(BF16), v_ref[kv, pl.ds(start, win), :])
                 + _dot(pc.astype(BF16), vc_ref[kv])) / denom
            acc = o if acc is None else acc + o
        o_ref[:, pair * LANES:(pair + 1) * LANES] = acc.astype(BF16)


def _attention_a(sink, qa, ka4, va4, kca4, vca4, batch, seq, ctx_len):
    t, nq = qa.shape
    tq = ATTN_A_TQ
    nqb = seq // tq
    lat = pl.BlockSpec((4, seq, LANES), lambda b, i: (0, b, 0))
    cx = pl.BlockSpec((4, ctx_len, LANES), lambda b, i: (0, b, 0))
    qs = pl.BlockSpec((tq, nq), lambda b, i: (b * nqb + i, 0))
    return pl.pallas_call(
        functools.partial(_attn_a_kernel, seq=seq),
        grid=(batch, nqb),
        in_specs=[pl.BlockSpec(memory_space=pltpu.SMEM), qs, lat, lat, cx, cx],
        out_specs=qs,
        out_shape=jax.ShapeDtypeStruct((t, nq), BF16),
        scratch_shapes=[pltpu.VMEM((tq, tq + 2 * WINDOW), F32)],
        compiler_params=_params("arbitrary", "arbitrary"),
    )(sink, qa, ka4, va4, kca4, vca4)


def _out_kernel(oa_ref, ob_ref, x_ref, g1_ref, sc2_ref, sh2_ref, ga_ref, gb_ref, gpost_ref,
                gpre2_ref, woa_ref, wob_ref, wrh_ref, wrl_ref, br_ref,
                x1_ref, h2_ref, rinfo_ref, rt_ref, tcarry_ref, tcnt_ref, cnt_ref, carry_ref):
    step = pl.program_id(0)

    @pl.when(step == 0)
    def _():
        carry_ref[...] = jnp.zeros_like(carry_ref)

    na = _rms(oa_ref[...].astype(F32)) * ga_ref[...]
    nb = _rms(ob_ref[...].astype(F32)) * gb_ref[...]
    ox = _dot(na.astype(BF16), woa_ref[...]) + _dot(nb.astype(BF16), wob_ref[...])
    x1 = x_ref[...] + g1_ref[0] * (_rms(ox) * gpost_ref[...])
    x1_ref[...] = x1
    h2 = _rms(x1) * gpre2_ref[...] * (1.0 + sc2_ref[0]) + sh2_ref[0]
    h2_ref[...] = h2

    h_hi, h_lo = _split_bf16(h2)
    logits = (_dot(h_hi, wrh_ref[...]) + _dot(h_lo, wrh_ref[...]) + _dot(h_hi, wrl_ref[...])
              + br_ref[...])
    tm = logits.shape[0]
    lane = lax.broadcasted_iota(jnp.int32, logits.shape, 1)
    lanef = lane.astype(F32)
    big = jnp.float32(1e9)
    ninf = jnp.float32(-jnp.inf)
    rowmax = lambda v: jnp.max(v, axis=-1, keepdims=True)
    rowmin = lambda v: jnp.min(v, axis=-1, keepdims=True)
    rowsum = lambda v: jnp.sum(v, axis=-1, keepdims=True)

    gmask = (lane >= N_EXPERTS) & (lane < N_EXPERTS + N_GROUPS)
    lg = jnp.where(gmask, logits, ninf)
    gmax = rowmax(lg)
    gidx = rowmin(jnp.where(lg == gmax, lanef, big)) - N_EXPERTS
    g_w = 1.0 / rowsum(jnp.exp(lg - gmax))
    lane_group = (lane // EXPERTS_PER_GROUP).astype(F32)
    emask = (lane < N_EXPERTS) & (lane_group == gidx)
    le = jnp.where(emask, logits, ninf)
    m1 = rowmax(le)
    i1 = rowmin(jnp.where(le == m1, lanef, big))
    le2 = jnp.where(lanef == i1, ninf, le)
    m2 = rowmax(le2)
    i2 = rowmin(jnp.where(le2 == m2, lanef, big))
    e2 = jnp.exp(m2 - m1)
    w0 = g_w / (1.0 + e2)
    w1 = g_w * e2 / (1.0 + e2)

    hit1 = lanef == i1
    hit2 = lanef == i2
    onehot = jnp.where(hit1 | hit2, 1.0, 0.0).astype(F32)
    r = lax.broadcasted_iota(jnp.int32, (tm, tm), 0)
    c = lax.broadcasted_iota(jnp.int32, (tm, tm), 1)
    strict_lower = jnp.where(r > c, 1.0, 0.0).astype(BF16)
    within = _dot(strict_lower, onehot.astype(BF16))
    tile_cnt = jnp.sum(onehot, axis=0, keepdims=True)
    incl = jnp.broadcast_to(tile_cnt, (8, LANES))
    lane8 = lax.broadcasted_iota(jnp.int32, (8, LANES), 1)
    shift = 1
    while shift < LANES:
        incl = incl + jnp.where(lane8 >= shift, pltpu.roll(incl, shift, 1), 0.0)
        shift *= 2
    local = within + (incl[0:1] - tile_cnt)
    pos0 = rowsum(jnp.where(hit1, local, 0.0))
    pos1 = rowsum(jnp.where(hit2, local, 0.0))
    tcarry_ref[0] = carry_ref[...]
    tcnt_ref[0] = tile_cnt
    carry_ref[...] += tile_cnt
    cnt_ref[...] = carry_ref[...]

    info = jnp.zeros_like(logits)
    for k, val in enumerate((i1, i2, pos0, pos1, w0, w1)):
        info = jnp.where(lane == k, val, info)
    rinfo_ref[...] = info
    rt_ref[...] = info.T[0:8, :]


def _out_and_route(oa, ob, x2, g1, sc2, sh2, ga, gb, gpost, gpre2, woa, wob, wrh, wrl, br, seq):
    t, d = x2.shape
    tm = OUT_TM
    tpb = seq // tm
    nq = oa.shape[1]
    const = lambda shape: pl.BlockSpec(shape, lambda i: (0,) * len(shape))
    per_batch = pl.BlockSpec((1, 1, d), lambda i: (i // tpb, 0, 0))
    rows = lambda n: pl.BlockSpec((tm, n), lambda i: (i, 0))
    per_tile = pl.BlockSpec((1, 1, LANES), lambda i: (i, 0, 0))
    return pl.pallas_call(
        _out_kernel,
        grid=(t // tm,),
        in_specs=[rows(nq), rows(nq), rows(d), per_batch, per_batch, per_batch,
                  const((1, nq)), const((1, nq)), const((1, d)), const((1, d)),
                  const(woa.shape), const(wob.shape), const(wrh.shape), const(wrl.shape),
                  const((1, LANES))],
        out_specs=[rows(d), rows(d), rows(LANES), pl.BlockSpec((8, tm), lambda i: (0, i)),
                   per_tile, per_tile, const((1, LANES))],
        out_shape=[jax.ShapeDtypeStruct((t, d), F32), jax.ShapeDtypeStruct((t, d), F32),
                   jax.ShapeDtypeStruct((t, LANES), F32), jax.ShapeDtypeStruct((8, t), F32),
                   jax.ShapeDtypeStruct((t // tm, 1, LANES), F32),
                   jax.ShapeDtypeStruct((t // tm, 1, LANES), F32),
                   jax.ShapeDtypeStruct((1, LANES), F32)],
        scratch_shapes=[pltpu.VMEM((1, LANES), F32)],
        compiler_params=_params("arbitrary"),
    )(oa, ob, x2, g1, sc2, sh2, ga, gb, gpost, gpre2, woa, wob, wrh, wrl, br)


ROW_SUBLANES = 8


def _for_each_run_piece(rdst_ref, rlen_ref, tile, max_len, fn):
    n_bits = max_len.bit_length()

    def run(e, local):
        length = rlen_ref[tile * N_EXPERTS + e]
        dst = rdst_ref[tile * N_EXPERTS + e]
        for b in range(n_bits):
            size = 1 << b

            @pl.when(((length >> b) & 1) == 1)
            def _():
                done = length & (size - 1)
                fn(local + done, dst + done, size)
        return local + length

    lax.fori_loop(0, N_EXPERTS, run, 0)


def _token_rows(ref, row0, n_rows):
    start = row0 * ROW_SUBLANES
    if not isinstance(start, int):
        start = pl.multiple_of(start, ROW_SUBLANES)
    return ref.at[pl.ds(start, n_rows * ROW_SUBLANES)]


def _dispatch_kernel(rdst_ref, rlen_ref, ends_ref, nv_ref, h_ref, rt_ref, xs_ref,
                     sorted_ref, zero_ref, sem, zsem, *, te, n_tiles):
    k = pl.program_id(0)
    nk = pl.num_programs(0)
    tm = h_ref.shape[0]
    rows = 2 * tm
    slot = k % 2

    def wait_slot(s):
        pltpu.make_async_copy(sorted_ref.at[s], _token_rows(xs_ref, 0, rows), sem.at[s]).wait()

    @pl.when(k == 0)
    def _():
        zero_ref[...] = jnp.zeros_like(zero_ref)

        def pad_copy(row0):
            return pltpu.make_async_copy(zero_ref, _token_rows(xs_ref, row0, te), zsem)

        def for_each_pad_tile(fn):
            def expert_pad(e, carry):
                end = ends_ref[e]
                prev = jnp.where(e > 0, ends_ref[jnp.maximum(e - 1, 0)], 0)

                @pl.when(end > prev)
                def _():
                    fn(pad_copy(end - te))
                return carry

            def tail_pad(j, carry):
                fn(pad_copy(j * te))
                return carry

            lax.fori_loop(0, N_EXPERTS, expert_pad, 0)
            lax.fori_loop(nv_ref[0], n_tiles, tail_pad, 0)

        for_each_pad_tile(lambda cp: cp.start())
        for_each_pad_tile(lambda cp: cp.wait())

    @pl.when(k >= 2)
    def _():
        wait_slot(slot)

    pos0 = rt_ref[2:3, :]
    pos1 = rt_ref[3:4, :]
    r = lax.broadcasted_iota(jnp.int32, (rows, tm), 0).astype(F32)
    perm = jnp.where((r == pos0) | (r == pos1), 1.0, 0.0).astype(BF16)
    srt = _dot(perm, h_ref[...].astype(BF16))
    buf = sorted_ref.at[slot]
    for c in range(ROW_SUBLANES):
        buf[pl.ds(c, rows, stride=ROW_SUBLANES), :] = srt[:, c * LANES:(c + 1) * LANES]

    def copy_piece(local, dst, size):
        pltpu.make_async_copy(_token_rows(buf, local, size), _token_rows(xs_ref, dst, size),
                              sem.at[slot]).start()

    _for_each_run_piece(rdst_ref, rlen_ref, k, tm, copy_piece)

    @pl.when(k == nk - 1)
    def _():
        wait_slot(slot)

        @pl.when(nk >= 2)
        def _():
            wait_slot(1 - slot)


def _dispatch(run_dst, run_len, ends, n_valid, h2, rt, n_rows):
    t, d = h2.shape
    assert d == ROW_SUBLANES * LANES
    tm = MOVE_TM
    te = EXPERT_TE
    return pl.pallas_call(
        functools.partial(_dispatch_kernel, te=te, n_tiles=n_rows // te),
        grid_spec=pltpu.PrefetchScalarGridSpec(
            num_scalar_prefetch=4,
            grid=(t // tm,),
            in_specs=[pl.BlockSpec((tm, d), lambda i, *_: (i, 0)),
                      pl.BlockSpec((8, tm), lambda i, *_: (0, i))],
            out_specs=pl.BlockSpec(memory_space=pl.ANY),
            scratch_shapes=[pltpu.VMEM((2, 2 * tm * ROW_SUBLANES, LANES), F32),
                            pltpu.VMEM((te * ROW_SUBLANES, LANES), F32),
                            pltpu.SemaphoreType.DMA((2,)), pltpu.SemaphoreType.DMA(())]),
        out_shape=jax.ShapeDtypeStruct((n_rows * ROW_SUBLANES, LANES), F32),
        compiler_params=_params("arbitrary"),
    )(run_dst, run_len, ends, n_valid, h2, rt)


def _expert_kernel(te_ref, nv_ref, xs_ref, wg_ref, wu_ref, wd_ref, ys_ref, wg_bf, wu_bf, wd_bf):
    j = pl.program_id(0)
    valid = j < nv_ref[0]
    changed = (j == 0) | (te_ref[j] != te_ref[jnp.maximum(j - 1, 0)])

    @pl.when(valid & changed)
    def _():
        wg_bf[...] = wg_ref[0].astype(BF16)
        wu_bf[...] = wu_ref[0].astype(BF16)
        wd_bf[...] = wd_ref[0].astype(BF16)

    @pl.when(valid)
    def _():
        te = xs_ref.shape[0] // ROW_SUBLANES
        xb = jnp.concatenate(
            [xs_ref[pl.ds(c, te, stride=ROW_SUBLANES), :].astype(BF16) for c in range(ROW_SUBLANES)],
            axis=1)
        gate = _dot(xb, wg_bf[...])
        up = _dot(xb, wu_bf[...])
        act = gate * jax.nn.sigmoid(gate) * up
        y = _dot(act.astype(BF16), wd_bf[...])
        for c in range(ROW_SUBLANES):
            ys_ref[pl.ds(c, te, stride=ROW_SUBLANES), :] = y[:, c * LANES:(c + 1) * LANES]

    @pl.when(jnp.logical_not(valid))
    def _():
        ys_ref[...] = jnp.zeros_like(ys_ref)


def _expert_mlp(tile_expert, n_valid, xs, w_gate, w_up, w_down):
    te = EXPERT_TE
    d, ff = w_gate.shape[1:]
    n_rows = xs.shape[0] // ROW_SUBLANES
    blk = (te * ROW_SUBLANES, LANES)
    tile = lambda j, e, nv: (jnp.minimum(j, nv[0] - 1), 0)
    wsel = lambda j, e, nv: (e[j], 0, 0)
    return pl.pallas_call(
        _expert_kernel,
        grid_spec=pltpu.PrefetchScalarGridSpec(
            num_scalar_prefetch=2,
            grid=(n_rows // te,),
            in_specs=[pl.BlockSpec(blk, tile),
                      pl.BlockSpec((1, d, ff), wsel), pl.BlockSpec((1, d, ff), wsel),
                      pl.BlockSpec((1, ff, d), wsel)],
            out_specs=pl.BlockSpec(blk, lambda j, e, nv: (j, 0)),
            scratch_shapes=[pltpu.VMEM((d, ff), BF16), pltpu.VMEM((d, ff), BF16),
                            pltpu.VMEM((ff, d), BF16)]),
        out_shape=jax.ShapeDtypeStruct(xs.shape, F32),
        compiler_params=_params("arbitrary"),
    )(tile_expert, n_valid, xs, w_gate, w_up, w_down)


def _combine_kernel(rdst_ref, rlen_ref, x1_ref, rinfo_ref, g2_ref, gpost_ref, ys_ref, o_ref,
                    gath_ref, sem):
    k = pl.program_id(0)
    nk = pl.num_programs(0)
    tm = x1_ref.shape[0]
    rows = 2 * tm
    slot = k % 2

    def gather_runs(tile, s):
        buf = gath_ref.at[s]

        def copy_piece(local, src, size):
            pltpu.make_async_copy(_token_rows(ys_ref, src, size), _token_rows(buf, local, size),
                                  sem.at[s]).start()

        _for_each_run_piece(rdst_ref, rlen_ref, tile, tm, copy_piece)

    @pl.when(k == 0)
    def _():
        gather_runs(0, 0)

    @pl.when(k + 1 < nk)
    def _():
        gather_runs(k + 1, 1 - slot)

    buf = gath_ref.at[slot]
    pltpu.make_async_copy(_token_rows(ys_ref, 0, rows), buf, sem.at[slot]).wait()
    g = jnp.concatenate(
        [buf[pl.ds(c, rows, stride=ROW_SUBLANES), :].astype(BF16) for c in range(ROW_SUBLANES)],
        axis=1)
    info = rinfo_ref[...]
    col = lax.broadcasted_iota(jnp.int32, (tm, rows), 1).astype(F32)
    pick0 = jnp.where(col == info[:, 2:3], 1.0, 0.0).astype(BF16)
    pick1 = jnp.where(col == info[:, 3:4], 1.0, 0.0).astype(BF16)
    fx = info[:, 4:5] * _dot(pick0, g) + info[:, 5:6] * _dot(pick1, g)
    o_ref[...] = x1_ref[...] + g2_ref[0] * (_rms(fx) * gpost_ref[...])


def _combine(run_dst, run_len, x1, rinfo, g2, gpost, ys, seq):
    t, d = x1.shape
    tm = MOVE_TM
    tpb = seq // tm
    return pl.pallas_call(
        _combine_kernel,
        grid_spec=pltpu.PrefetchScalarGridSpec(
            num_scalar_prefetch=2,
            grid=(t // tm,),
            in_specs=[pl.BlockSpec((tm, d), lambda i, *_: (i, 0)),
                      pl.BlockSpec((tm, LANES), lambda i, *_: (i, 0)),
                      pl.BlockSpec((1, 1, d), lambda i, *_: (i // tpb, 0, 0)),
                      pl.BlockSpec((1, d), lambda i, *_: (0, 0)),
                      pl.BlockSpec(memory_space=pl.ANY)],
            out_specs=pl.BlockSpec((tm, d), lambda i, *_: (i, 0)),
            scratch_shapes=[pltpu.VMEM((2, 2 * tm * ROW_SUBLANES, LANES), F32),
                            pltpu.SemaphoreType.DMA((2,))]),
        out_shape=jax.ShapeDtypeStruct((t, d), F32),
        compiler_params=_params("arbitrary"),
    )(run_dst, run_len, x1, rinfo, g2, gpost, ys)


def _rope_tables(seq):
    pos = jnp.arange(seq, dtype=jnp.int32)
    row = (pos // GRID_W).astype(F32)
    col = (pos % GRID_W).astype(F32)
    axis_dim = HEAD_DIM // 2
    inv_freq = ROPE_THETA ** (-jnp.arange(0, axis_dim, 2, dtype=F32) / axis_dim)
    ang = jnp.concatenate([row[:, None] * inv_freq, col[:, None] * inv_freq], axis=-1)
    pair = (jnp.arange(LANES) % HEAD_DIM) // 2
    cos = jnp.cos(ang)[:, pair]
    sin = jnp.sin(ang)[:, pair]
    even = (jnp.arange(LANES) % 2) == 0
    return cos, jnp.where(even, -sin, 0.0), jnp.where(even, 0.0, sin)


def _segment_ones(n):
    seg = jnp.arange(n) // HEAD_DIM
    return (seg[:, None] == seg[None, :]).astype(BF16)


def kernel(x, c, ctx, c_ctx, w_mod, b_mod, attn_pre_norm, attn_post_norm, w_in, a_sink,
           b_q_norm, b_k_norm, a_out_norm, b_out_norm, w_out, ffn_pre_norm, ffn_post_norm,
           w_group, b_group, w_router, b_router, w_gate, w_up, w_down):
    batch, seq, d = x.shape
    ctx_len = ctx.shape[1]
    assert w_mod.shape[0] == 1, "single-layer stack only (context stream is never updated)"
    assert seq % ATTN_A_TQ == 0 and seq >= ATTN_A_TQ + 2 * WINDOW
    assert seq % PROJ_TM == 0 and seq % ATTN_B_TQ == 0 and seq % OUT_TM == 0 and seq % MOVE_TM == 0
    t = batch * seq
    nq = d // 2
    nkv = nq // KV_GROUP
    assert nkv == LANES and w_in.shape[2] == 2 * nq + 4 * nkv

    cc = jnp.concatenate([c, c_ctx[None, :], jnp.zeros((16 - batch - 1, d), F32)], axis=0)
    mod = _modulation(cc, w_mod[0], b_mod[0])
    sh1, sc1, g1, sh2, sc2, g2 = (m.reshape(batch, 1, d) for m in jnp.split(mod[:batch], 6, axis=-1))
    csh1, csc1 = (m.reshape(1, d) for m in jnp.split(mod[batch], 6)[:2])

    x2 = x.reshape(t, d)
    c2 = ctx.reshape(batch * ctx_len, d)
    gpre = attn_pre_norm[0].reshape(1, d)
    w_in_bf = w_in[0].astype(BF16)
    kv_cols = jnp.concatenate([w_in_bf[:, nq:nq + 2 * nkv], w_in_bf[:, 2 * nq + 2 * nkv:]], axis=1)
    qn = jnp.tile(b_q_norm[0], nq // HEAD_DIM).reshape(1, nq)
    kn = jnp.tile(b_k_norm[0], nkv // HEAD_DIM).reshape(1, nkv)
    seg_q, seg_k = _segment_ones(nq), _segment_ones(nkv)
    qa, qb, ka4, va4, kb4, vb4 = _project_latents(
        x2, sc1, sh1, gpre, w_in_bf, _rope_tables(seq), qn, kn, seg_q, seg_k, seq)
    kca4, vca4, kcb4, vcb4 = _project_context(c2, csc1, csh1, gpre, kv_cols, kn, seg_k, ctx_len)

    oa = _attention_a(a_sink[0], qa, ka4, va4, kca4, vca4, batch, seq, ctx_len)
    ob = _attention_b(qb, kb4, vb4, kcb4, vcb4, batch, seq, ctx_len)

    w_out_bf = w_out[0].astype(BF16)
    w_r = jnp.zeros((d, LANES), F32)
    w_r = w_r.at[:, :N_EXPERTS].set(w_router[0]).at[:, N_EXPERTS:N_EXPERTS + N_GROUPS].set(w_group[0])
    w_r_hi = w_r.astype(BF16)
    w_r_lo = (w_r - w_r_hi.astype(F32)).astype(BF16)
    b_r = jnp.zeros((1, LANES), F32)
    b_r = b_r.at[0, :N_EXPERTS].set(b_router[0]).at[0, N_EXPERTS:N_EXPERTS + N_GROUPS].set(b_group[0])
    x1, h2, rinfo, rt, tcarry, tcnt, counts = _out_and_route(
        oa, ob, x2, g1, sc2, sh2, a_out_norm[0].reshape(1, nq), b_out_norm[0].reshape(1, nq),
        attn_post_norm[0].reshape(1, d), ffn_pre_norm[0].reshape(1, d),
        w_out_bf[:nq], w_out_bf[nq:], w_r_hi, w_r_lo, b_r, seq)

    te = EXPERT_TE
    n_tiles = -(-(2 * t + N_EXPERTS * (te - 1)) // te)
    n_rows = n_tiles * te
    cnt = counts[0, :N_EXPERTS].astype(jnp.int32)
    padded = ((cnt + te - 1) // te) * te
    ends = jnp.cumsum(padded)
    offs = ends - padded
    run_dst = (offs[None, :] + tcarry[:, 0, :N_EXPERTS].astype(jnp.int32)).reshape(-1)
    run_len = tcnt[:, 0, :N_EXPERTS].astype(jnp.int32).reshape(-1)
    n_valid = (ends[-1] // te).astype(jnp.int32).reshape(1)
    tile_start = jnp.arange(n_tiles, dtype=jnp.int32) * te
    tile_expert = jnp.sum(ends[None, :] <= tile_start[:, None], axis=1).astype(jnp.int32)
    last_expert = tile_expert[jnp.maximum(n_valid[0] - 1, 0)]
    tile_expert = jnp.where(tile_start < ends[-1], tile_expert, last_expert)

    xs = _dispatch(run_dst, run_len, ends.astype(jnp.int32), n_valid, h2, rt, n_rows)
    ys = _expert_mlp(tile_expert, n_valid, xs, w_gate[0], w_up[0], w_down[0])
    out = _combine(run_dst, run_len, x1, rinfo, g2, ffn_post_norm[0].reshape(1, d), ys, seq)
    return out.reshape(batch, seq, d)
```

```python
import functools

import jax
import jax.numpy as jnp
from jax import lax
from jax.experimental import pallas as pl
from jax.experimental.pallas import tpu as pltpu

F32 = jnp.float32
BF16 = jnp.bfloat16

GRID_W = 64
HEAD_DIM = 64
KV_GROUP = 4
WINDOW = 128
ROPE_THETA = 10000.0
N_GROUPS = 4
EXPERTS_PER_GROUP = 8
N_EXPERTS = N_GROUPS * EXPERTS_PER_GROUP
EPS = 1e-6
NEG_INF = -1e30
LOG2_E = 1.4426950408889634

LANES = 128
V7X_VMEM_LIMIT = 56 * 1024 * 1024

PROJ_TM = 512
ATTN_A_TQ = 512
ATTN_B_TQ = 256
OUT_TM = 512
EXPERT_TE = 512
MOVE_TM = OUT_TM


def _params(*sem):
    return pltpu.CompilerParams(dimension_semantics=sem, vmem_limit_bytes=V7X_VMEM_LIMIT)


def _dot(a, b):
    return jnp.dot(a, b, preferred_element_type=F32)


def _dot_nt(a, b):
    return lax.dot_general(a, b, (((1,), (1,)), ((), ())), preferred_element_type=F32)


def _rms(x):
    return x * lax.rsqrt(jnp.mean(x * x, axis=-1, keepdims=True) + EPS)


def _split_bf16(x):
    hi = x.astype(BF16)
    lo = (x - hi.astype(F32)).astype(BF16)
    return hi, lo


def _mod_kernel(c_ref, w_ref, b_ref, o_ref):
    cc = c_ref[...]
    s = cc * jax.nn.sigmoid(cc)
    s_hi, s_lo = _split_bf16(s)
    w_hi, w_lo = _split_bf16(w_ref[...])
    o_ref[...] = _dot(s_hi, w_hi) + _dot(s_lo, w_hi) + _dot(s_hi, w_lo) + b_ref[...]


def _modulation(cc, w_mod, b_mod):
    rows, d = cc.shape
    n = w_mod.shape[1]
    bn = 1024
    return pl.pallas_call(
        _mod_kernel,
        grid=(n // bn,),
        in_specs=[pl.BlockSpec((rows, d), lambda i: (0, 0)),
                  pl.BlockSpec((d, bn), lambda i: (0, i)),
                  pl.BlockSpec((1, bn), lambda i: (0, i))],
        out_specs=pl.BlockSpec((rows, bn), lambda i: (0, i)),
        out_shape=jax.ShapeDtypeStruct((rows, n), F32),
        compiler_params=_params("arbitrary"),
    )(cc, w_mod, b_mod.reshape(1, n))


def _rope(x, cos, sin_a, sin_b):
    return x * cos + pltpu.roll(x, LANES - 1, 1) * sin_a + pltpu.roll(x, 1, 1) * sin_b


def _head_norm(x, seg_ref, gain):
    ss = _dot((x * x).astype(BF16), seg_ref[...])
    return x * lax.rsqrt(ss * (1.0 / HEAD_DIM) + EPS) * gain


def _store_pair_variants(ref, t):
    lane = lax.broadcasted_iota(jnp.int32, t.shape, 1)
    lo = lane < HEAD_DIM
    sw = pltpu.roll(t, HEAD_DIM, 1)
    zero = jnp.zeros_like(t)
    ref[0] = jnp.where(lo, t, zero).astype(BF16)
    ref[1] = jnp.where(lo, zero, sw).astype(BF16)
    ref[2] = jnp.where(lo, sw, zero).astype(BF16)
    ref[3] = jnp.where(lo, zero, t).astype(BF16)


def _proj_kernel(x_ref, sc_ref, sh_ref, gpre_ref, w_ref, cos_ref, sa_ref, sb_ref,
                 qn_ref, kn_ref, seg_q_ref, seg_k_ref,
                 qa_ref, qb_ref, ka_ref, va_ref, kb_ref, vb_ref):
    h = _rms(x_ref[...]) * gpre_ref[...] * (1.0 + sc_ref[0]) + sh_ref[0]
    p = _dot(h.astype(BF16), w_ref[...])
    cos, sa, sb = cos_ref[...], sa_ref[...], sb_ref[...]
    nq = qa_ref.shape[1]
    q_scale = HEAD_DIM ** -0.5 * LOG2_E
    for c in range(nq // LANES):
        qa_ref[:, c * LANES:(c + 1) * LANES] = (
            _rope(p[:, c * LANES:(c + 1) * LANES], cos, sa, sb) * q_scale).astype(BF16)
    o = nq
    _store_pair_variants(ka_ref, _rope(p[:, o:o + LANES], cos, sa, sb))
    _store_pair_variants(va_ref, p[:, o + LANES:o + 2 * LANES])
    o += 2 * LANES
    qb = _head_norm(p[:, o:o + nq], seg_q_ref, qn_ref[...])
    for c in range(nq // LANES):
        qb_ref[:, c * LANES:(c + 1) * LANES] = (
            _rope(qb[:, c * LANES:(c + 1) * LANES], cos, sa, sb) * q_scale).astype(BF16)
    o += nq
    kb = _head_norm(p[:, o:o + LANES], seg_k_ref, kn_ref[...])
    _store_pair_variants(kb_ref, _rope(kb, cos, sa, sb))
    _store_pair_variants(vb_ref, p[:, o + LANES:o + 2 * LANES])


def _ctx_proj_kernel(x_ref, sc_ref, sh_ref, gpre_ref, w_ref, kn_ref, seg_k_ref,
                     ka_ref, va_ref, kb_ref, vb_ref):
    h = _rms(x_ref[...]) * gpre_ref[...] * (1.0 + sc_ref[...]) + sh_ref[...]
    p = _dot(h.astype(BF16), w_ref[...])
    _store_pair_variants(ka_ref, p[:, 0:LANES])
    _store_pair_variants(va_ref, p[:, LANES:2 * LANES])
    _store_pair_variants(kb_ref, _head_norm(p[:, 2 * LANES:3 * LANES], seg_k_ref, kn_ref[...]))
    _store_pair_variants(vb_ref, p[:, 3 * LANES:4 * LANES])


def _project_latents(x2, sc, sh, gpre, w_in, tables, qn, kn, seg_q, seg_k, seq):
    t, d = x2.shape
    tm = PROJ_TM
    tpb = seq // tm
    nq = seg_q.shape[0]
    const = lambda shape: pl.BlockSpec(shape, lambda i: (0,) * len(shape))
    per_batch = pl.BlockSpec((1, 1, d), lambda i: (i // tpb, 0, 0))
    table = pl.BlockSpec((tm, LANES), lambda i: (i % tpb, 0))
    kv_spec = pl.BlockSpec((4, tm, LANES), lambda i: (0, i, 0))
    kv_shape = jax.ShapeDtypeStruct((4, t, LANES), BF16)
    q_spec = pl.BlockSpec((tm, nq), lambda i: (i, 0))
    q_shape = jax.ShapeDtypeStruct((t, nq), BF16)
    return pl.pallas_call(
        _proj_kernel,
        grid=(t // tm,),
        in_specs=[pl.BlockSpec((tm, d), lambda i: (i, 0)), per_batch, per_batch, const((1, d)),
                  const(w_in.shape), table, table, table,
                  const((1, nq)), const((1, LANES)), const(seg_q.shape), const(seg_k.shape)],
        out_specs=[q_spec, q_spec, kv_spec, kv_spec, kv_spec, kv_spec],
        out_shape=[q_shape, q_shape, kv_shape, kv_shape, kv_shape, kv_shape],
        compiler_params=_params("arbitrary"),
    )(x2, sc, sh, gpre, w_in, *tables, qn, kn, seg_q, seg_k)


def _project_context(c2, sc, sh, gpre, w_kv, kn, seg_k, ctx_len):
    t, d = c2.shape
    const = lambda shape: pl.BlockSpec(shape, lambda i: (0,) * len(shape))
    kv_spec = pl.BlockSpec((4, ctx_len, LANES), lambda i: (0, i, 0))
    kv_shape = jax.ShapeDtypeStruct((4, t, LANES), BF16)
    return pl.pallas_call(
        _ctx_proj_kernel,
        grid=(t // ctx_len,),
        in_specs=[pl.BlockSpec((ctx_len, d), lambda i: (i, 0)), const((1, d)), const((1, d)),
                  const((1, d)), const(w_kv.shape), const((1, LANES)), const(seg_k.shape)],
        out_specs=[kv_spec] * 4,
        out_shape=[kv_shape] * 4,
        compiler_params=_params("arbitrary"),
    )(c2, sc, sh, gpre, w_kv, kn, seg_k)


def _attn_b_kernel(q_ref, k_ref, v_ref, kc_ref, vc_ref, o_ref):
    n_pairs = q_ref.shape[1] // LANES
    pairs_per_kv = KV_GROUP // 2
    for pair in range(n_pairs):
        g = pair // pairs_per_kv
        q2 = q_ref[:, pair * LANES:(pair + 1) * LANES]
        acc = None
        for half in range(2):
            kv = 2 * g + half
            s = _dot_nt(q2, k_ref[kv])
            sc = _dot_nt(q2, kc_ref[kv])
            m = jnp.maximum(jnp.max(s, axis=-1, keepdims=True), jnp.max(sc, axis=-1, keepdims=True))
            p = jnp.exp2(s - m)
            pc = jnp.exp2(sc - m)
            denom = jnp.sum(p, axis=-1, keepdims=True) + jnp.sum(pc, axis=-1, keepdims=True)
            o = (_dot(p.astype(BF16), v_ref[kv]) + _dot(pc.astype(BF16), vc_ref[kv])) / denom
            acc = o if acc is None else acc + o
        o_ref[:, pair * LANES:(pair + 1) * LANES] = acc.astype(BF16)


def _attention_b(qb, kb4, vb4, kcb4, vcb4, batch, seq, ctx_len):
    t, nq = qb.shape
    tq = ATTN_B_TQ
    nqb = seq // tq
    lat = pl.BlockSpec((4, seq, LANES), lambda b, i: (0, b, 0))
    cx = pl.BlockSpec((4, ctx_len, LANES), lambda b, i: (0, b, 0))
    qs = pl.BlockSpec((tq, nq), lambda b, i: (b * nqb + i, 0))
    return pl.pallas_call(
        _attn_b_kernel,
        grid=(batch, nqb),
        in_specs=[qs, lat, lat, cx, cx],
        out_specs=qs,
        out_shape=jax.ShapeDtypeStruct((t, nq), BF16),
        compiler_params=_params("arbitrary", "arbitrary"),
    )(qb, kb4, vb4, kcb4, vcb4)


def _attn_a_kernel(sink_ref, q_ref, k_ref, v_ref, kc_ref, vc_ref, o_ref, bias_ref, *, seq):
    i = pl.program_id(1)
    tq = q_ref.shape[0]
    win = tq + 2 * WINDOW
    start = pl.multiple_of(jnp.clip(i * tq - WINDOW, 0, seq - win), WINDOW)
    qpos = i * tq + lax.broadcasted_iota(jnp.int32, (tq, win), 0)
    kpos = start + lax.broadcasted_iota(jnp.int32, (tq, win), 1)
    bias_ref[...] = jnp.where(jnp.abs(kpos - qpos) <= WINDOW, 0.0, NEG_INF).astype(F32)
    n_pairs = q_ref.shape[1] // LANES
    pairs_per_kv = KV_GROUP // 2
    rowmax = lambda s: jnp.max(s, axis=-1, keepdims=True)
    rowsum = lambda s: jnp.sum(s, axis=-1, keepdims=True)
    for pair in range(n_pairs):
        g = pair // pairs_per_kv
        q2 = q_ref[:, pair * LANES:(pair + 1) * LANES]
        acc = None
        for half in range(2):
            kv = 2 * g + half
            sink = sink_ref[2 * pair + half] * LOG2_E
            s = _dot_nt(q2, k_ref[kv, pl.ds(start, win), :]) + bias_ref[...]
            sc = _dot_nt(q2, kc_ref[kv])
            m = jnp.maximum(jnp.maximum(rowmax(s), rowmax(sc)), sink)
            p = jnp.exp2(s - m)
            pc = jnp.exp2(sc - m)
            denom = rowsum(p) + rowsum(pc) + jnp.exp2(sink - m)
            o = (_dot(p.astype(BF16), v_ref[kv, pl.ds(start, win), :])
                 + _dot(pc.astype(BF16), vc_ref[kv])) / denom
            acc = o if acc is None else acc + o
        o_ref[:, pair * LANES:(pair + 1) * LANES] = acc.astype(BF16)


def _attention_a(sink, qa, ka4, va4, kca4, vca4, batch, seq, ctx_len):
    t, nq = qa.shape
    tq = ATTN_A_TQ
    nqb = seq // tq
    lat = pl.BlockSpec((4, seq, LANES), lambda b, i: (0, b, 0))
    cx = pl.BlockSpec((4, ctx_len, LANES), lambda b, i: (0, b, 0))
    qs = pl.BlockSpec((tq, nq), lambda b, i: (b * nqb + i, 0))
    return pl.pallas_call(
        functools.partial(_attn_a_kernel, seq=seq),
        grid=(batch, nqb),
        in_specs=[pl.BlockSpec(memory_space=pltpu.SMEM), qs, lat, lat, cx, cx],
        out_specs=qs,
        out_shape=jax.ShapeDtypeStruct((t, nq), BF16),
        scratch_shapes=[pltpu.VMEM((tq, tq + 2 * WINDOW), F32)],
        compiler_params=_params("arbitrary", "arbitrary"),
    )(sink, qa, ka4, va4, kca4, vca4)


def _out_kernel(oa_ref, ob_ref, x_ref, g1_ref, sc2_ref, sh2_ref, ga_ref, gb_ref, gpost_ref,
                gpre2_ref, woa_ref, wob_ref, wrh_ref, wrl_ref, br_ref,
                x1_ref, h2_ref, rinfo_ref, rt_ref, tcarry_ref, tcnt_ref, cnt_ref, carry_ref):
    step = pl.program_id(0)

    @pl.when(step == 0)
    def _():
        carry_ref[...] = jnp.zeros_like(carry_ref)

    na = _rms(oa_ref[...].astype(F32)) * ga_ref[...]
    nb = _rms(ob_ref[...].astype(F32)) * gb_ref[...]
    ox = _dot(na.astype(BF16), woa_ref[...]) + _dot(nb.astype(BF16), wob_ref[...])
    x1 = x_ref[...] + g1_ref[0] * (_rms(ox) * gpost_ref[...])
    x1_ref[...] = x1
    h2 = _rms(x1) * gpre2_ref[...] * (1.0 + sc2_ref[0]) + sh2_ref[0]
    h2_ref[...] = h2

    h_hi, h_lo = _split_bf16(h2)
    logits = (_dot(h_hi, wrh_ref[...]) + _dot(h_lo, wrh_ref[...]) + _dot(h_hi, wrl_ref[...])
              + br_ref[...])
    tm = logits.shape[0]
    lane = lax.broadcasted_iota(jnp.int32, logits.shape, 1)
    lanef = lane.astype(F32)
    big = jnp.float32(1e9)
    ninf = jnp.float32(-jnp.inf)
    rowmax = lambda v: jnp.max(v, axis=-1, keepdims=True)
    rowmin = lambda v: jnp.min(v, axis=-1, keepdims=True)
    rowsum = lambda v: jnp.sum(v, axis=-1, keepdims=True)

    gmask = (lane >= N_EXPERTS) & (lane < N_EXPERTS + N_GROUPS)
    lg = jnp.where(gmask, logits, ninf)
    gmax = rowmax(lg)
    gidx = rowmin(jnp.where(lg == gmax, lanef, big)) - N_EXPERTS
    g_w = 1.0 / rowsum(jnp.exp(lg - gmax))
    lane_group = (lane // EXPERTS_PER_GROUP).astype(F32)
    emask = (lane < N_EXPERTS) & (lane_group == gidx)
    le = jnp.where(emask, logits, ninf)
    m1 = rowmax(le)
    i1 = rowmin(jnp.where(le == m1, lanef, big))
    le2 = jnp.where(lanef == i1, ninf, le)
    m2 = rowmax(le2)
    i2 = rowmin(jnp.where(le2 == m2, lanef, big))
    e2 = jnp.exp(m2 - m1)
    w0 = g_w / (1.0 + e2)
    w1 = g_w * e2 / (1.0 + e2)

    hit1 = lanef == i1
    hit2 = lanef == i2
    onehot = jnp.where(hit1 | hit2, 1.0, 0.0).astype(F32)
    r = lax.broadcasted_iota(jnp.int32, (tm, tm), 0)
    c = lax.broadcasted_iota(jnp.int32, (tm, tm), 1)
    strict_lower = jnp.where(r > c, 1.0, 0.0).astype(BF16)
    within = _dot(strict_lower, onehot.astype(BF16))
    tile_cnt = jnp.sum(onehot, axis=0, keepdims=True)
    incl = jnp.broadcast_to(tile_cnt, (8, LANES))
    lane8 = lax.broadcasted_iota(jnp.int32, (8, LANES), 1)
    shift = 1
    while shift < LANES:
        incl = incl + jnp.where(lane8 >= shift, pltpu.roll(incl, shift, 1), 0.0)
        shift *= 2
    local = within + (incl[0:1] - tile_cnt)
    pos0 = rowsum(jnp.where(hit1, local, 0.0))
    pos1 = rowsum(jnp.where(hit2, local, 0.0))
    tcarry_ref[0] = carry_ref[...]
    tcnt_ref[0] = tile_cnt
    carry_ref[...] += tile_cnt
    cnt_ref[...] = carry_ref[...]

    info = jnp.zeros_like(logits)
    for k, val in enumerate((i1, i2, pos0, pos1, w0, w1)):
        info = jnp.where(lane == k, val, info)
    rinfo_ref[...] = info
    rt_ref[...] = info.T[0:8, :]


def _out_and_route(oa, ob, x2, g1, sc2, sh2, ga, gb, gpost, gpre2, woa, wob, wrh, wrl, br, seq):
    t, d = x2.shape
    tm = OUT_TM
    tpb = seq // tm
    nq = oa.shape[1]
    const = lambda shape: pl.BlockSpec(shape, lambda i: (0,) * len(shape))
    per_batch = pl.BlockSpec((1, 1, d), lambda i: (i // tpb, 0, 0))
    rows = lambda n: pl.BlockSpec((tm, n), lambda i: (i, 0))
    per_tile = pl.BlockSpec((1, 1, LANES), lambda i: (i, 0, 0))
    return pl.pallas_call(
        _out_kernel,
        grid=(t // tm,),
        in_specs=[rows(nq), rows(nq), rows(d), per_batch, per_batch, per_batch,
                  const((1, nq)), const((1, nq)), const((1, d)), const((1, d)),
                  const(woa.shape), const(wob.shape), const(wrh.shape), const(wrl.shape),
                  const((1, LANES))],
        out_specs=[rows(d), rows(d), rows(LANES), pl.BlockSpec((8, tm), lambda i: (0, i)),
                   per_tile, per_tile, const((1, LANES))],
        out_shape=[jax.ShapeDtypeStruct((t, d), F32), jax.ShapeDtypeStruct((t, d), F32),
                   jax.ShapeDtypeStruct((t, LANES), F32), jax.ShapeDtypeStruct((8, t), F32),
                   jax.ShapeDtypeStruct((t // tm, 1, LANES), F32),
                   jax.ShapeDtypeStruct((t // tm, 1, LANES), F32),
                   jax.ShapeDtypeStruct((1, LANES), F32)],
        scratch_shapes=[pltpu.VMEM((1, LANES), F32)],
        compiler_params=_params("arbitrary"),
    )(oa, ob, x2, g1, sc2, sh2, ga, gb, gpost, gpre2, woa, wob, wrh, wrl, br)


ROW_SUBLANES = 8


def _for_each_run_piece(rdst_ref, rlen_ref, tile, max_len, fn):
    n_bits = max_len.bit_length()

    def run(e, local):
        length = rlen_ref[tile * N_EXPERTS + e]
        dst = rdst_ref[tile * N_EXPERTS + e]
        for b in range(n_bits):
            size = 1 << b

            @pl.when(((length >> b) & 1) == 1)
            def _():
                done = length & (size - 1)
                fn(local + done, dst + done, size)
        return local + length

    lax.fori_loop(0, N_EXPERTS, run, 0)


def _token_rows(ref, row0, n_rows):
    start = row0 * ROW_SUBLANES
    if not isinstance(start, int):
        start = pl.multiple_of(start, ROW_SUBLANES)
    return ref.at[pl.ds(start, n_rows * ROW_SUBLANES)]


def _dispatch_kernel(rdst_ref, rlen_ref, ends_ref, nv_ref, h_ref, rt_ref, xs_ref,
                     sorted_ref, zero_ref, sem, zsem, *, te, n_tiles):
    k = pl.program_id(0)
    nk = pl.num_programs(0)
    tm = h_ref.shape[0]
    rows = 2 * tm
    slot = k % 2

    def wait_slot(s):
        pltpu.make_async_copy(sorted_ref.at[s], _token_rows(xs_ref, 0, rows), sem.at[s]).wait()

    @pl.when(k == 0)
    def _():
        zero_ref[...] = jnp.zeros_like(zero_ref)

        def pad_copy(row0):
            return pltpu.make_async_copy(zero_ref, _token_rows(xs_ref, row0, te), zsem)

        def for_each_pad_tile(fn):
            def expert_pad(e, carry):
                end = ends_ref[e]
                prev = jnp.where(e > 0, ends_ref[jnp.maximum(e - 1, 0)], 0)

                @pl.when(end > prev)
                def _():
                    fn(pad_copy(end - te))
                return carry

            def tail_pad(j, carry):
                fn(pad_copy(j * te))
                return carry

            lax.fori_loop(0, N_EXPERTS, expert_pad, 0)
            lax.fori_loop(nv_ref[0], n_tiles, tail_pad, 0)

        for_each_pad_tile(lambda cp: cp.start())
        for_each_pad_tile(lambda cp: cp.wait())

    @pl.when(k >= 2)
    def _():
        wait_slot(slot)

    pos0 = rt_ref[2:3, :]
    pos1 = rt_ref[3:4, :]
    r = lax.broadcasted_iota(jnp.int32, (rows, tm), 0).astype(F32)
    perm = jnp.where((r == pos0) | (r == pos1), 1.0, 0.0).astype(BF16)
    srt = _dot(perm, h_ref[...].astype(BF16))
    buf = sorted_ref.at[slot]
    for c in range(ROW_SUBLANES):
        buf[pl.ds(c, rows, stride=ROW_SUBLANES), :] = srt[:, c * LANES:(c + 1) * LANES]

    def copy_piece(local, dst, size):
        pltpu.make_async_copy(_token_rows(buf, local, size), _token_rows(xs_ref, dst, size),
                              sem.at[slot]).start()

    _for_each_run_piece(rdst_ref, rlen_ref, k, tm, copy_piece)

    @pl.when(k == nk - 1)
    def _():
        wait_slot(slot)

        @pl.when(nk >= 2)
        def _():
            wait_slot(1 - slot)


def _dispatch(run_dst, run_len, ends, n_valid, h2, rt, n_rows):
    t, d = h2.shape
    assert d == ROW_SUBLANES * LANES
    tm = MOVE_TM
    te = EXPERT_TE
    return pl.pallas_call(
        functools.partial(_dispatch_kernel, te=te, n_tiles=n_rows // te),
        grid_spec=pltpu.PrefetchScalarGridSpec(
            num_scalar_prefetch=4,
            grid=(t // tm,),
            in_specs=[pl.BlockSpec((tm, d), lambda i, *_: (i, 0)),
                      pl.BlockSpec((8, tm), lambda i, *_: (0, i))],
            out_specs=pl.BlockSpec(memory_space=pl.ANY),
            scratch_shapes=[pltpu.VMEM((2, 2 * tm * ROW_SUBLANES, LANES), F32),
                            pltpu.VMEM((te * ROW_SUBLANES, LANES), F32),
                            pltpu.SemaphoreType.DMA((2,)), pltpu.SemaphoreType.DMA(())]),
        out_shape=jax.ShapeDtypeStruct((n_rows * ROW_SUBLANES, LANES), F32),
        compiler_params=_params("arbitrary"),
    )(run_dst, run_len, ends, n_valid, h2, rt)


def _expert_kernel(te_ref, nv_ref, xs_ref, wg_ref, wu_ref, wd_ref, ys_ref, wg_bf, wu_bf, wd_bf):
    j = pl.program_id(0)
    valid = j < nv_ref[0]
    changed = (j == 0) | (te_ref[j] != te_ref[jnp.maximum(j - 1, 0)])

    @pl.when(valid & changed)
    def _():
        wg_bf[...] = wg_ref[0].astype(BF16)
        wu_bf[...] = wu_ref[0].astype(BF16)
        wd_bf[...] = wd_ref[0].astype(BF16)

    @pl.when(valid)
    def _():
        te = xs_ref.shape[0] // ROW_SUBLANES
        xb = jnp.concatenate(
            [xs_ref[pl.ds(c, te, stride=ROW_SUBLANES), :].astype(BF16) for c in range(ROW_SUBLANES)],
            axis=1)
        gate = _dot(xb, wg_bf[...])
        up = _dot(xb, wu_bf[...])
        act = gate * jax.nn.sigmoid(gate) * up
        y = _dot(act.astype(BF16), wd_bf[...])
        for c in range(ROW_SUBLANES):
            ys_ref[pl.ds(c, te, stride=ROW_SUBLANES), :] = y[:, c * LANES:(c + 1) * LANES]

    @pl.when(jnp.logical_not(valid))
    def _():
        ys_ref[...] = jnp.zeros_like(ys_ref)


def _expert_mlp(tile_expert, n_valid, xs, w_gate, w_up, w_down):
    te = EXPERT_TE
    d, ff = w_gate.shape[1:]
    n_rows = xs.shape[0] // ROW_SUBLANES
    blk = (te * ROW_SUBLANES, LANES)
    tile = lambda j, e, nv: (jnp.minimum(j, nv[0] - 1), 0)
    wsel = lambda j, e, nv: (e[j], 0, 0)
    return pl.pallas_call(
        _expert_kernel,
        grid_spec=pltpu.PrefetchScalarGridSpec(
            num_scalar_prefetch=2,
            grid=(n_rows // te,),
            in_specs=[pl.BlockSpec(blk, tile),
                      pl.BlockSpec((1, d, ff), wsel), pl.BlockSpec((1, d, ff), wsel),
                      pl.BlockSpec((1, ff, d), wsel)],
            out_specs=pl.BlockSpec(blk, lambda j, e, nv: (j, 0)),
            scratch_shapes=[pltpu.VMEM((d, ff), BF16), pltpu.VMEM((d, ff), BF16),
                            pltpu.VMEM((ff, d), BF16)]),
        out_shape=jax.ShapeDtypeStruct(xs.shape, F32),
        compiler_params=_params("arbitrary"),
    )(tile_expert, n_valid, xs, w_gate, w_up, w_down)


def _combine_kernel(rdst_ref, rlen_ref, x1_ref, rinfo_ref, g2_ref, gpost_ref, ys_ref, o_ref,
                    gath_ref, sem):
    k = pl.program_id(0)
    nk = pl.num_programs(0)
    tm = x1_ref.shape[0]
    rows = 2 * tm
    slot = k % 2

    def gather_runs(tile, s):
        buf = gath_ref.at[s]

        def copy_piece(local, src, size):
            pltpu.make_async_copy(_token_rows(ys_ref, src, size), _token_rows(buf, local, size),
                                  sem.at[s]).start()

        _for_each_run_piece(rdst_ref, rlen_ref, tile, tm, copy_piece)

    @pl.when(k == 0)
    def _():
        gather_runs(0, 0)

    @pl.when(k + 1 < nk)
    def _():
        gather_runs(k + 1, 1 - slot)

    buf = gath_ref.at[slot]
    pltpu.make_async_copy(_token_rows(ys_ref, 0, rows), buf, sem.at[slot]).wait()
    g = jnp.concatenate(
        [buf[pl.ds(c, rows, stride=ROW_SUBLANES), :].astype(BF16) for c in range(ROW_SUBLANES)],
        axis=1)
    info = rinfo_ref[...]
    col = lax.broadcasted_iota(jnp.int32, (tm, rows), 1).astype(F32)
    pick0 = jnp.where(col == info[:, 2:3], 1.0, 0.0).astype(BF16)
    pick1 = jnp.where(col == info[:, 3:4], 1.0, 0.0).astype(BF16)
    fx = info[:, 4:5] * _dot(pick0, g) + info[:, 5:6] * _dot(pick1, g)
    o_ref[...] = x1_ref[...] + g2_ref[0] * (_rms(fx) * gpost_ref[...])


def _combine(run_dst, run_len, x1, rinfo, g2, gpost, ys, seq):
    t, d = x1.shape
    tm = MOVE_TM
    tpb = seq // tm
    return pl.pallas_call(
        _combine_kernel,
        grid_spec=pltpu.PrefetchScalarGridSpec(
            num_scalar_prefetch=2,
            grid=(t // tm,),
            in_specs=[pl.BlockSpec((tm, d), lambda i, *_: (i, 0)),
                      pl.BlockSpec((tm, LANES), lambda i, *_: (i, 0)),
                      pl.BlockSpec((1, 1, d), lambda i, *_: (i // tpb, 0, 0)),
                      pl.BlockSpec((1, d), lambda i, *_: (0, 0)),
                      pl.BlockSpec(memory_space=pl.ANY)],
            out_specs=pl.BlockSpec((tm, d), lambda i, *_: (i, 0)),
            scratch_shapes=[pltpu.VMEM((2, 2 * tm * ROW_SUBLANES, LANES), F32),
                            pltpu.SemaphoreType.DMA((2,))]),
        out_shape=jax.ShapeDtypeStruct((t, d), F32),
        compiler_params=_params("arbitrary"),
    )(run_dst, run_len, x1, rinfo, g2, gpost, ys)


def _rope_tables(seq):
    pos = jnp.arange(seq, dtype=jnp.int32)
    row = (pos // GRID_W).astype(F32)
    col = (pos % GRID_W).astype(F32)
    axis_dim = HEAD_DIM // 2
    inv_freq = ROPE_THETA ** (-jnp.arange(0, axis_dim, 2, dtype=F32) / axis_dim)
    ang = jnp.concatenate([row[:, None] * inv_freq, col[:, None] * inv_freq], axis=-1)
    pair = (jnp.arange(LANES) % HEAD_DIM) // 2
    cos = jnp.cos(ang)[:, pair]
    sin = jnp.sin(ang)[:, pair]
    even = (jnp.arange(LANES) % 2) == 0
    return cos, jnp.where(even, -sin, 0.0), jnp.where(even, 0.0, sin)


def _segment_ones(n):
    seg = jnp.arange(n) // HEAD_DIM
    return (seg[:, None] == seg[None, :]).astype(BF16)


def kernel(x, c, ctx, c_ctx, w_mod, b_mod, attn_pre_norm, attn_post_norm, w_in, a_sink,
           b_q_norm, b_k_norm, a_out_norm, b_out_norm, w_out, ffn_pre_norm, ffn_post_norm,
           w_group, b_group, w_router, b_router, w_gate, w_up, w_down):
    batch, seq, d = x.shape
    ctx_len = ctx.shape[1]
    assert w_mod.shape[0] == 1, "single-layer stack only (context stream is never updated)"
    assert seq % ATTN_A_TQ == 0 and seq >= ATTN_A_TQ + 2 * WINDOW
    assert seq % PROJ_TM == 0 and seq % ATTN_B_TQ == 0 and seq % OUT_TM == 0 and seq % MOVE_TM == 0
    t = batch * seq
    nq = d // 2
    nkv = nq // KV_GROUP
    assert nkv == LANES and w_in.shape[2] == 2 * nq + 4 * nkv

    cc = jnp.concatenate([c, c_ctx[None, :], jnp.zeros((16 - batch - 1, d), F32)], axis=0)
    mod = _modulation(cc, w_mod[0], b_mod[0])
    sh1, sc1, g1, sh2, sc2, g2 = (m.reshape(batch, 1, d) for m in jnp.split(mod[:batch], 6, axis=-1))
    csh1, csc1 = (m.reshape(1, d) for m in jnp.split(mod[batch], 6)[:2])

    x2 = x.reshape(t, d)
    c2 = ctx.reshape(batch * ctx_len, d)
    gpre = attn_pre_norm[0].reshape(1, d)
    w_in_bf = w_in[0].astype(BF16)
    kv_cols = jnp.concatenate([w_in_bf[:, nq:nq + 2 * nkv], w_in_bf[:, 2 * nq + 2 * nkv:]], axis=1)
    qn = jnp.tile(b_q_norm[0], nq // HEAD_DIM).reshape(1, nq)
    kn = jnp.tile(b_k_norm[0], nkv // HEAD_DIM).reshape(1, nkv)
    seg_q, seg_k = _segment_ones(nq), _segment_ones(nkv)
    qa, qb, ka4, va4, kb4, vb4 = _project_latents(
        x2, sc1, sh1, gpre, w_in_bf, _rope_tables(seq), qn, kn, seg_q, seg_k, seq)
    kca4, vca4, kcb4, vcb4 = _project_context(c2, csc1, csh1, gpre, kv_cols, kn, seg_k, ctx_len)

    oa = _attention_a(a_sink[0], qa, ka4, va4, kca4, vca4, batch, seq, ctx_len)
    ob = _attention_b(qb, kb4, vb4, kcb4, vcb4, batch, seq, ctx_len)

    w_out_bf = w_out[0].astype(BF16)
    w_r = jnp.zeros((d, LANES), F32)
    w_r = w_r.at[:, :N_EXPERTS].set(w_router[0]).at[:, N_EXPERTS:N_EXPERTS + N_GROUPS].set(w_group[0])
    w_r_hi = w_r.astype(BF16)
    w_r_lo = (w_r - w_r_hi.astype(F32)).astype(BF16)
    b_r = jnp.zeros((1, LANES), F32)
    b_r = b_r.at[0, :N_EXPERTS].set(b_router[0]).at[0, N_EXPERTS:N_EXPERTS + N_GROUPS].set(b_group[0])
    x1, h2, rinfo, rt, tcarry, tcnt, counts = _out_and_route(
        oa, ob, x2, g1, sc2, sh2, a_out_norm[0].reshape(1, nq), b_out_norm[0].reshape(1, nq),
        attn_post_norm[0].reshape(1, d), ffn_pre_norm[0].reshape(1, d),
        w_out_bf[:nq], w_out_bf[nq:], w_r_hi, w_r_lo, b_r, seq)

    te = EXPERT_TE
    n_tiles = -(-(2 * t + N_EXPERTS * (te - 1)) // te)
    n_rows = n_tiles * te
    cnt = counts[0, :N_EXPERTS].astype(jnp.int32)
    padded = ((cnt + te - 1) // te) * te
    ends = jnp.cumsum(padded)
    offs = ends - padded
    run_dst = (offs[None, :] + tcarry[:, 0, :N_EXPERTS].astype(jnp.int32)).reshape(-1)
    run_len = tcnt[:, 0, :N_EXPERTS].astype(jnp.int32).reshape(-1)
    n_valid = (ends[-1] // te).astype(jnp.int32).reshape(1)
    tile_start = jnp.arange(n_tiles, dtype=jnp.int32) * te
    tile_expert = jnp.sum(ends[None, :] <= tile_start[:, None], axis=1).astype(jnp.int32)
    last_expert = tile_expert[jnp.maximum(n_valid[0] - 1, 0)]
    tile_expert = jnp.where(tile_start < ends[-1], tile_expert, last_expert)

    xs = _dispatch(run_dst, run_len, ends.astype(jnp.int32), n_valid, h2, rt, n_rows)
    ys = _expert_mlp(tile_expert, n_valid, xs, w_gate[0], w_up[0], w_down[0])
    out = _combine(run_dst, run_len, x1, rinfo, g2, ffn_post_norm[0].reshape(1, d), ys, seq)
    return out.reshape(batch, seq, d)
```

```python
import functools

import jax
import jax.numpy as jnp
from jax import lax
from jax.experimental import pallas as pl
from jax.experimental.pallas import tpu as pltpu

F32 = jnp.float32
BF16 = jnp.bfloat16

GRID_W = 64
HEAD_DIM = 64
KV_GROUP = 4
WINDOW = 128
ROPE_THETA = 10000.0
N_GROUPS = 4
EXPERTS_PER_GROUP = 8
N_EXPERTS = N_GROUPS * EXPERTS_PER_GROUP
EPS = 1e-6
NEG_INF = -1e30
LOG2_E = 1.4426950408889634

LANES = 128
V7X_VMEM_LIMIT = 56 * 1024 * 1024

PROJ_TM = 512
ATTN_A_TQ = 512
ATTN_B_TQ = 512
OUT_TM = 512
EXPERT_TE = 512
MOVE_TM = OUT_TM


def _params(*sem):
    return pltpu.CompilerParams(dimension_semantics=sem, vmem_limit_bytes=V7X_VMEM_LIMIT)


def _dot(a, b):
    return jnp.dot(a, b, preferred_element_type=F32)


def _dot_nt(a, b):
    return lax.dot_general(a, b, (((1,), (1,)), ((), ())), preferred_element_type=F32)


def _rms(x):
    return x * lax.rsqrt(jnp.mean(x * x, axis=-1, keepdims=True) + EPS)


def _split_bf16(x):
    hi = x.astype(BF16)
    lo = (x - hi.astype(F32)).astype(BF16)
    return hi, lo


def _mod_kernel(c_ref, w_ref, b_ref, o_ref):
    cc = c_ref[...]
    s = cc * jax.nn.sigmoid(cc)
    s_hi, s_lo = _split_bf16(s)
    w_hi, w_lo = _split_bf16(w_ref[...])
    o_ref[...] = _dot(s_hi, w_hi) + _dot(s_lo, w_hi) + _dot(s_hi, w_lo) + b_ref[...]


def _modulation(cc, w_mod, b_mod):
    rows, d = cc.shape
    n = w_mod.shape[1]
    bn = 1024
    return pl.pallas_call(
        _mod_kernel,
        grid=(n // bn,),
        in_specs=[pl.BlockSpec((rows, d), lambda i: (0, 0)),
                  pl.BlockSpec((d, bn), lambda i: (0, i)),
                  pl.BlockSpec((1, bn), lambda i: (0, i))],
        out_specs=pl.BlockSpec((rows, bn), lambda i: (0, i)),
        out_shape=jax.ShapeDtypeStruct((rows, n), F32),
        compiler_params=_params("arbitrary"),
    )(cc, w_mod, b_mod.reshape(1, n))


def _rope(x, cos, sin_a, sin_b):
    return x * cos + pltpu.roll(x, LANES - 1, 1) * sin_a + pltpu.roll(x, 1, 1) * sin_b


def _head_norm(x, seg_ref, gain):
    ss = _dot((x * x).astype(BF16), seg_ref[...])
    return x * lax.rsqrt(ss * (1.0 / HEAD_DIM) + EPS) * gain


def _proj_kernel(x_ref, sc_ref, sh_ref, gpre_ref, w_ref, cos_ref, sa_ref, sb_ref,
                 qn_ref, kn_ref, seg_q_ref, seg_k_ref,
                 qat_ref, qbt_ref, ka_ref, vat_ref, kb_ref, vbt_ref):
    h = _rms(x_ref[...]) * gpre_ref[...] * (1.0 + sc_ref[0]) + sh_ref[0]
    p = _dot(h.astype(BF16), w_ref[...])
    cos, sa, sb = cos_ref[...], sa_ref[...], sb_ref[...]
    nq = qat_ref.shape[0]
    q_scale = HEAD_DIM ** -0.5 * LOG2_E
    for c in range(nq // LANES):
        qat_ref[c * LANES:(c + 1) * LANES, :] = (
            _rope(p[:, c * LANES:(c + 1) * LANES], cos, sa, sb) * q_scale).T.astype(BF16)
    o = nq
    ka_ref[...] = _rope(p[:, o:o + LANES], cos, sa, sb).astype(BF16)
    vat_ref[...] = p[:, o + LANES:o + 2 * LANES].T.astype(BF16)
    o += 2 * LANES
    qb = _head_norm(p[:, o:o + nq], seg_q_ref, qn_ref[...])
    for c in range(nq // LANES):
        qbt_ref[c * LANES:(c + 1) * LANES, :] = (
            _rope(qb[:, c * LANES:(c + 1) * LANES], cos, sa, sb) * q_scale).T.astype(BF16)
    o += nq
    kb = _head_norm(p[:, o:o + LANES], seg_k_ref, kn_ref[...])
    kb_ref[...] = _rope(kb, cos, sa, sb).astype(BF16)
    vbt_ref[...] = p[:, o + LANES:o + 2 * LANES].T.astype(BF16)


def _ctx_proj_kernel(x_ref, sc_ref, sh_ref, gpre_ref, w_ref, kn_ref, seg_k_ref,
                     ka_ref, vat_ref, kb_ref, vbt_ref):
    h = _rms(x_ref[...]) * gpre_ref[...] * (1.0 + sc_ref[...]) + sh_ref[...]
    p = _dot(h.astype(BF16), w_ref[...])
    ka_ref[...] = p[:, 0:LANES].astype(BF16)
    vat_ref[...] = p[:, LANES:2 * LANES].T.astype(BF16)
    kb_ref[...] = _head_norm(p[:, 2 * LANES:3 * LANES], seg_k_ref, kn_ref[...]).astype(BF16)
    vbt_ref[...] = p[:, 3 * LANES:4 * LANES].T.astype(BF16)


def _project_latents(x2, sc, sh, gpre, w_in, tables, qn, kn, seg_q, seg_k, seq):
    t, d = x2.shape
    tm = PROJ_TM
    tpb = seq // tm
    nq = seg_q.shape[0]
    const = lambda shape: pl.BlockSpec(shape, lambda i: (0,) * len(shape))
    per_batch = pl.BlockSpec((1, 1, d), lambda i: (i // tpb, 0, 0))
    table = pl.BlockSpec((tm, LANES), lambda i: (i % tpb, 0))
    k_spec = pl.BlockSpec((tm, LANES), lambda i: (i, 0))
    k_shape = jax.ShapeDtypeStruct((t, LANES), BF16)
    vt_spec = pl.BlockSpec((LANES, tm), lambda i: (0, i))
    vt_shape = jax.ShapeDtypeStruct((LANES, t), BF16)
    qt_spec = pl.BlockSpec((nq, tm), lambda i: (0, i))
    qt_shape = jax.ShapeDtypeStruct((nq, t), BF16)
    return pl.pallas_call(
        _proj_kernel,
        grid=(t // tm,),
        in_specs=[pl.BlockSpec((tm, d), lambda i: (i, 0)), per_batch, per_batch, const((1, d)),
                  const(w_in.shape), table, table, table,
                  const((1, nq)), const((1, LANES)), const(seg_q.shape), const(seg_k.shape)],
        out_specs=[qt_spec, qt_spec, k_spec, vt_spec, k_spec, vt_spec],
        out_shape=[qt_shape, qt_shape, k_shape, vt_shape, k_shape, vt_shape],
        compiler_params=_params("arbitrary"),
    )(x2, sc, sh, gpre, w_in, *tables, qn, kn, seg_q, seg_k)


def _project_context(c2, sc, sh, gpre, w_kv, kn, seg_k, ctx_len):
    t, d = c2.shape
    const = lambda shape: pl.BlockSpec(shape, lambda i: (0,) * len(shape))
    k_spec = pl.BlockSpec((ctx_len, LANES), lambda i: (i, 0))
    k_shape = jax.ShapeDtypeStruct((t, LANES), BF16)
    vt_spec = pl.BlockSpec((LANES, ctx_len), lambda i: (0, i))
    vt_shape = jax.ShapeDtypeStruct((LANES, t), BF16)
    return pl.pallas_call(
        _ctx_proj_kernel,
        grid=(t // ctx_len,),
        in_specs=[pl.BlockSpec((ctx_len, d), lambda i: (i, 0)), const((1, d)), const((1, d)),
                  const((1, d)), const(w_kv.shape), const((1, LANES)), const(seg_k.shape)],
        out_specs=[k_spec, vt_spec, k_spec, vt_spec],
        out_shape=[k_shape, vt_shape, k_shape, vt_shape],
        compiler_params=_params("arbitrary"),
    )(c2, sc, sh, gpre, w_kv, kn, seg_k)


def _attend_t(w, k, kc, vt, vct, bias=None, sink=None):
    st = _dot(k, w)
    sct = _dot(kc, w)
    if bias is not None:
        st = st + bias
    m = jnp.maximum(jnp.max(st, axis=0, keepdims=True), jnp.max(sct, axis=0, keepdims=True))
    if sink is not None:
        m = jnp.maximum(m, sink)
    pt = jnp.exp2(st - m)
    pct = jnp.exp2(sct - m)
    denom = jnp.sum(pt, axis=0, keepdims=True) + jnp.sum(pct, axis=0, keepdims=True)
    if sink is not None:
        denom = denom + jnp.exp2(sink - m)
    o2 = _dot(vt, pt.astype(BF16)) + _dot(vct, pct.astype(BF16))
    return o2, denom


def _all_heads_t(qt_ref, o_ref, attend):
    tq = qt_ref.shape[1]
    n_kv = LANES // HEAD_DIM
    zeros = jnp.zeros((HEAD_DIM, tq), BF16)
    outs = []
    for h in range(qt_ref.shape[0] // HEAD_DIM):
        g = h // KV_GROUP
        qh = qt_ref[h * HEAD_DIM:(h + 1) * HEAD_DIM, :]
        w = jnp.concatenate([zeros] * g + [qh] + [zeros] * (n_kv - 1 - g), axis=0)
        o2, denom = attend(h, w)
        outs.append(o2[g * HEAD_DIM:(g + 1) * HEAD_DIM, :] / denom)
    o_ref[...] = jnp.concatenate(outs, axis=0).T.astype(BF16)


def _attn_b_kernel(qt_ref, k_ref, vt_ref, kc_ref, vct_ref, o_ref):
    k, kc, vt, vct = k_ref[...], kc_ref[...], vt_ref[...], vct_ref[...]
    _all_heads_t(qt_ref, o_ref, lambda h, w: _attend_t(w, k, kc, vt, vct))


def _attn_specs(nq, tq, seq, ctx_len):
    nqb = seq // tq
    return dict(
        qt=pl.BlockSpec((nq, tq), lambda b, i: (0, b * nqb + i)),
        k=pl.BlockSpec((seq, LANES), lambda b, i: (b, 0)),
        vt=pl.BlockSpec((LANES, seq), lambda b, i: (0, b)),
        kc=pl.BlockSpec((ctx_len, LANES), lambda b, i: (b, 0)),
        vct=pl.BlockSpec((LANES, ctx_len), lambda b, i: (0, b)),
        out=pl.BlockSpec((tq, nq), lambda b, i: (b * nqb + i, 0)))


def _attention_b(qbt, kb, vbt, kcb, vcbt, batch, seq, ctx_len):
    nq, t = qbt.shape
    tq = ATTN_B_TQ
    sp = _attn_specs(nq, tq, seq, ctx_len)
    return pl.pallas_call(
        _attn_b_kernel,
        grid=(batch, seq // tq),
        in_specs=[sp["qt"], sp["k"], sp["vt"], sp["kc"], sp["vct"]],
        out_specs=sp["out"],
        out_shape=jax.ShapeDtypeStruct((t, nq), BF16),
        compiler_params=_params("arbitrary", "arbitrary"),
    )(qbt, kb, vbt, kcb, vcbt)


def _attn_a_kernel(sink_ref, qt_ref, k_ref, vt_ref, kc_ref, vct_ref, o_ref, bias_ref, *, seq):
    i = pl.program_id(1)
    tq = qt_ref.shape[1]
    win = tq + 2 * WINDOW
    start = pl.multiple_of(jnp.clip(i * tq - WINDOW, 0, seq - win), WINDOW)
    kpos = start + lax.broadcasted_iota(jnp.int32, (win, tq), 0)
    qpos = i * tq + lax.broadcasted_iota(jnp.int32, (win, tq), 1)
    bias_ref[...] = jnp.where(jnp.abs(kpos - qpos) <= WINDOW, 0.0, NEG_INF).astype(F32)
    k = k_ref[pl.ds(start, win), :]
    vt = vt_ref[:, pl.ds(start, win)]
    kc, vct = kc_ref[...], vct_ref[...]
    _all_heads_t(qt_ref, o_ref, lambda h, w: _attend_t(
        w, k, kc, vt, vct, bias=bias_ref[...], sink=sink_ref[h] * LOG2_E))


def _attention_a(sink, qat, ka, vat, kca, vcat, batch, seq, ctx_len):
    nq, t = qat.shape
    tq = ATTN_A_TQ
    sp = _attn_specs(nq, tq, seq, ctx_len)
    return pl.pallas_call(
        functools.partial(_attn_a_kernel, seq=seq),
        grid=(batch, seq // tq),
        in_specs=[pl.BlockSpec(memory_space=pltpu.SMEM),
                  sp["qt"], sp["k"], sp["vt"], sp["kc"], sp["vct"]],
        out_specs=sp["out"],
        out_shape=jax.ShapeDtypeStruct((t, nq), BF16),
        scratch_shapes=[pltpu.VMEM((tq + 2 * WINDOW, tq), F32)],
        compiler_params=_params("arbitrary", "arbitrary"),
    )(sink, qat, ka, vat, kca, vcat)


def _out_kernel(oa_ref, ob_ref, x_ref, g1_ref, sc2_ref, sh2_ref, ga_ref, gb_ref, gpost_ref,
                gpre2_ref, woa_ref, wob_ref, wrh_ref, wrl_ref, br_ref,
                x1_ref, h2_ref, rinfo_ref, rt_ref, tcarry_ref, tcnt_ref, cnt_ref, carry_ref):
    step = pl.program_id(0)

    @pl.when(step == 0)
    def _():
        carry_ref[...] = jnp.zeros_like(carry_ref)

    na = _rms(oa_ref[...].astype(F32)) * ga_ref[...]
    nb = _rms(ob_ref[...].astype(F32)) * gb_ref[...]
    ox = _dot(na.astype(BF16), woa_ref[...]) + _dot(nb.astype(BF16), wob_ref[...])
    x1 = x_ref[...] + g1_ref[0] * (_rms(ox) * gpost_ref[...])
    x1_ref[...] = x1
    h2 = _rms(x1) * gpre2_ref[...] * (1.0 + sc2_ref[0]) + sh2_ref[0]
    h2_ref[...] = h2

    h_hi, h_lo = _split_bf16(h2)
    logits = (_dot(h_hi, wrh_ref[...]) + _dot(h_lo, wrh_ref[...]) + _dot(h_hi, wrl_ref[...])
              + br_ref[...])
    tm = logits.shape[0]
    lane = lax.broadcasted_iota(jnp.int32, logits.shape, 1)
    lanef = lane.astype(F32)
    big = jnp.float32(1e9)
    ninf = jnp.float32(-jnp.inf)
    rowmax = lambda v: jnp.max(v, axis=-1, keepdims=True)
    rowmin = lambda v: jnp.min(v, axis=-1, keepdims=True)
    rowsum = lambda v: jnp.sum(v, axis=-1, keepdims=True)

    gmask = (lane >= N_EXPERTS) & (lane < N_EXPERTS + N_GROUPS)
    lg = jnp.where(gmask, logits, ninf)
    gmax = rowmax(lg)
    gidx = rowmin(jnp.where(lg == gmax, lanef, big)) - N_EXPERTS
    g_w = 1.0 / rowsum(jnp.exp(lg - gmax))
    lane_group = (lane // EXPERTS_PER_GROUP).astype(F32)
    emask = (lane < N_EXPERTS) & (lane_group == gidx)
    le = jnp.where(emask, logits, ninf)
    m1 = rowmax(le)
    i1 = rowmin(jnp.where(le == m1, lanef, big))
    le2 = jnp.where(lanef == i1, ninf, le)
    m2 = rowmax(le2)
    i2 = rowmin(jnp.where(le2 == m2, lanef, big))
    e2 = jnp.exp(m2 - m1)
    w0 = g_w / (1.0 + e2)
    w1 = g_w * e2 / (1.0 + e2)

    hit1 = lanef == i1
    hit2 = lanef == i2
    onehot = jnp.where(hit1 | hit2, 1.0, 0.0).astype(F32)
    r = lax.broadcasted_iota(jnp.int32, (tm, tm), 0)
    c = lax.broadcasted_iota(jnp.int32, (tm, tm), 1)
    strict_lower = jnp.where(r > c, 1.0, 0.0).astype(BF16)
    within = _dot(strict_lower, onehot.astype(BF16))
    tile_cnt = jnp.sum(onehot, axis=0, keepdims=True)
    incl = jnp.broadcast_to(tile_cnt, (8, LANES))
    lane8 = lax.broadcasted_iota(jnp.int32, (8, LANES), 1)
    shift = 1
    while shift < LANES:
        incl = incl + jnp.where(lane8 >= shift, pltpu.roll(incl, shift, 1), 0.0)
        shift *= 2
    local = within + (incl[0:1] - tile_cnt)
    pos0 = rowsum(jnp.where(hit1, local, 0.0))
    pos1 = rowsum(jnp.where(hit2, local, 0.0))
    tcarry_ref[0] = carry_ref[...]
    tcnt_ref[0] = tile_cnt
    carry_ref[...] += tile_cnt
    cnt_ref[...] = carry_ref[...]

    info = jnp.zeros_like(logits)
    for k, val in enumerate((i1, i2, pos0, pos1, w0, w1)):
        info = jnp.where(lane == k, val, info)
    rinfo_ref[...] = info
    rt_ref[...] = info.T[0:8, :]


def _out_and_route(oa, ob, x2, g1, sc2, sh2, ga, gb, gpost, gpre2, woa, wob, wrh, wrl, br, seq):
    t, d = x2.shape
    tm = OUT_TM
    tpb = seq // tm
    nq = oa.shape[1]
    const = lambda shape: pl.BlockSpec(shape, lambda i: (0,) * len(shape))
    per_batch = pl.BlockSpec((1, 1, d), lambda i: (i // tpb, 0, 0))
    rows = lambda n: pl.BlockSpec((tm, n), lambda i: (i, 0))
    per_tile = pl.BlockSpec((1, 1, LANES), lambda i: (i, 0, 0))
    return pl.pallas_call(
        _out_kernel,
        grid=(t // tm,),
        in_specs=[rows(nq), rows(nq), rows(d), per_batch, per_batch, per_batch,
                  const((1, nq)), const((1, nq)), const((1, d)), const((1, d)),
                  const(woa.shape), const(wob.shape), const(wrh.shape), const(wrl.shape),
                  const((1, LANES))],
        out_specs=[rows(d), rows(d), rows(LANES), pl.BlockSpec((8, tm), lambda i: (0, i)),
                   per_tile, per_tile, const((1, LANES))],
        out_shape=[jax.ShapeDtypeStruct((t, d), F32), jax.ShapeDtypeStruct((t, d), F32),
                   jax.ShapeDtypeStruct((t, LANES), F32), jax.ShapeDtypeStruct((8, t), F32),
                   jax.ShapeDtypeStruct((t // tm, 1, LANES), F32),
                   jax.ShapeDtypeStruct((t // tm, 1, LANES), F32),
                   jax.ShapeDtypeStruct((1, LANES), F32)],
        scratch_shapes=[pltpu.VMEM((1, LANES), F32)],
        compiler_params=_params("arbitrary"),
    )(oa, ob, x2, g1, sc2, sh2, ga, gb, gpost, gpre2, woa, wob, wrh, wrl, br)


ROW_SUBLANES = 8


def _for_each_run_piece(rdst_ref, rlen_ref, tile, max_len, fn):
    n_bits = max_len.bit_length()

    def run(e, local):
        length = rlen_ref[tile * N_EXPERTS + e]
        dst = rdst_ref[tile * N_EXPERTS + e]
        for b in range(n_bits):
            size = 1 << b

            @pl.when(((length >> b) & 1) == 1)
            def _():
                done = length & (size - 1)
                fn(local + done, dst + done, size)
        return local + length

    lax.fori_loop(0, N_EXPERTS, run, 0)


def _token_rows(ref, row0, n_rows):
    start = row0 * ROW_SUBLANES
    if not isinstance(start, int):
        start = pl.multiple_of(start, ROW_SUBLANES)
    return ref.at[pl.ds(start, n_rows * ROW_SUBLANES)]


def _dispatch_kernel(rdst_ref, rlen_ref, ends_ref, nv_ref, h_ref, rt_ref, xs_ref,
                     sorted_ref, zero_ref, sem, zsem, *, te, n_tiles):
    k = pl.program_id(0)
    nk = pl.num_programs(0)
    tm = h_ref.shape[0]
    rows = 2 * tm
    slot = k % 2

    def wait_slot(s):
        pltpu.make_async_copy(sorted_ref.at[s], _token_rows(xs_ref, 0, rows), sem.at[s]).wait()

    @pl.when(k == 0)
    def _():
        zero_ref[...] = jnp.zeros_like(zero_ref)

        def pad_copy(row0):
            return pltpu.make_async_copy(zero_ref, _token_rows(xs_ref, row0, te), zsem)

        def for_each_pad_tile(fn):
            def expert_pad(e, carry):
                end = ends_ref[e]
                prev = jnp.where(e > 0, ends_ref[jnp.maximum(e - 1, 0)], 0)

                @pl.when(end > prev)
                def _():
                    fn(pad_copy(end - te))
                return carry

            def tail_pad(j, carry):
                fn(pad_copy(j * te))
                return carry

            lax.fori_loop(0, N_EXPERTS, expert_pad, 0)
            lax.fori_loop(nv_ref[0], n_tiles, tail_pad, 0)

        for_each_pad_tile(lambda cp: cp.start())
        for_each_pad_tile(lambda cp: cp.wait())

    @pl.when(k >= 2)
    def _():
        wait_slot(slot)

    pos0 = rt_ref[2:3, :]
    pos1 = rt_ref[3:4, :]
    r = lax.broadcasted_iota(jnp.int32, (rows, tm), 0).astype(F32)
    perm = jnp.where((r == pos0) | (r == pos1), 1.0, 0.0).astype(BF16)
    srt = _dot(perm, h_ref[...].astype(BF16))
    buf = sorted_ref.at[slot]
    for c in range(ROW_SUBLANES):
        buf[pl.ds(c, rows, stride=ROW_SUBLANES), :] = srt[:, c * LANES:(c + 1) * LANES]

    def copy_piece(local, dst, size):
        pltpu.make_async_copy(_token_rows(buf, local, size), _token_rows(xs_ref, dst, size),
                              sem.at[slot]).start()

    _for_each_run_piece(rdst_ref, rlen_ref, k, tm, copy_piece)

    @pl.when(k == nk - 1)
    def _():
        wait_slot(slot)

        @pl.when(nk >= 2)
        def _():
            wait_slot(1 - slot)


def _dispatch(run_dst, run_len, ends, n_valid, h2, rt, n_rows):
    t, d = h2.shape
    assert d == ROW_SUBLANES * LANES
    tm = MOVE_TM
    te = EXPERT_TE
    return pl.pallas_call(
        functools.partial(_dispatch_kernel, te=te, n_tiles=n_rows // te),
        grid_spec=pltpu.PrefetchScalarGridSpec(
            num_scalar_prefetch=4,
            grid=(t // tm,),
            in_specs=[pl.BlockSpec((tm, d), lambda i, *_: (i, 0)),
                      pl.BlockSpec((8, tm), lambda i, *_: (0, i))],
            out_specs=pl.BlockSpec(memory_space=pl.ANY),
            scratch_shapes=[pltpu.VMEM((2, 2 * tm * ROW_SUBLANES, LANES), F32),
                            pltpu.VMEM((te * ROW_SUBLANES, LANES), F32),
                            pltpu.SemaphoreType.DMA((2,)), pltpu.SemaphoreType.DMA(())]),
        out_shape=jax.ShapeDtypeStruct((n_rows * ROW_SUBLANES, LANES), F32),
        compiler_params=_params("arbitrary"),
    )(run_dst, run_len, ends, n_valid, h2, rt)


def _expert_kernel(te_ref, nv_ref, xs_ref, wg_ref, wu_ref, wd_ref, ys_ref, wg_bf, wu_bf, wd_bf):
    j = pl.program_id(0)
    valid = j < nv_ref[0]
    changed = (j == 0) | (te_ref[j] != te_ref[jnp.maximum(j - 1, 0)])

    @pl.when(valid & changed)
    def _():
        wg_bf[...] = wg_ref[0].astype(BF16)
        wu_bf[...] = wu_ref[0].astype(BF16)
        wd_bf[...] = wd_ref[0].astype(BF16)

    @pl.when(valid)
    def _():
        te = xs_ref.shape[0] // ROW_SUBLANES
        xb = jnp.concatenate(
            [xs_ref[pl.ds(c, te, stride=ROW_SUBLANES), :].astype(BF16) for c in range(ROW_SUBLANES)],
            axis=1)
        gate = _dot(xb, wg_bf[...])
        up = _dot(xb, wu_bf[...])
        act = gate * jax.nn.sigmoid(gate) * up
        y = _dot(act.astype(BF16), wd_bf[...])
        for c in range(ROW_SUBLANES):
            ys_ref[pl.ds(c, te, stride=ROW_SUBLANES), :] = y[:, c * LANES:(c + 1) * LANES]

    @pl.when(jnp.logical_not(valid))
    def _():
        ys_ref[...] = jnp.zeros_like(ys_ref)


def _expert_mlp(tile_expert, n_valid, xs, w_gate, w_up, w_down):
    te = EXPERT_TE
    d, ff = w_gate.shape[1:]
    n_rows = xs.shape[0] // ROW_SUBLANES
    blk = (te * ROW_SUBLANES, LANES)
    tile = lambda j, e, nv: (jnp.minimum(j, nv[0] - 1), 0)
    wsel = lambda j, e, nv: (e[j], 0, 0)
    return pl.pallas_call(
        _expert_kernel,
        grid_spec=pltpu.PrefetchScalarGridSpec(
            num_scalar_prefetch=2,
            grid=(n_rows // te,),
            in_specs=[pl.BlockSpec(blk, tile),
                      pl.BlockSpec((1, d, ff), wsel), pl.BlockSpec((1, d, ff), wsel),
                      pl.BlockSpec((1, ff, d), wsel)],
            out_specs=pl.BlockSpec(blk, lambda j, e, nv: (j, 0)),
            scratch_shapes=[pltpu.VMEM((d, ff), BF16), pltpu.VMEM((d, ff), BF16),
                            pltpu.VMEM((ff, d), BF16)]),
        out_shape=jax.ShapeDtypeStruct(xs.shape, F32),
        compiler_params=_params("arbitrary"),
    )(tile_expert, n_valid, xs, w_gate, w_up, w_down)


def _combine_kernel(rdst_ref, rlen_ref, x1_ref, rinfo_ref, g2_ref, gpost_ref, ys_ref, o_ref,
                    gath_ref, sem):
    k = pl.program_id(0)
    nk = pl.num_programs(0)
    tm = x1_ref.shape[0]
    rows = 2 * tm
    slot = k % 2

    def gather_runs(tile, s):
        buf = gath_ref.at[s]

        def copy_piece(local, src, size):
            pltpu.make_async_copy(_token_rows(ys_ref, src, size), _token_rows(buf, local, size),
                                  sem.at[s]).start()

        _for_each_run_piece(rdst_ref, rlen_ref, tile, tm, copy_piece)

    @pl.when(k == 0)
    def _():
        gather_runs(0, 0)

    @pl.when(k + 1 < nk)
    def _():
        gather_runs(k + 1, 1 - slot)

    buf = gath_ref.at[slot]
    pltpu.make_async_copy(_token_rows(ys_ref, 0, rows), buf, sem.at[slot]).wait()
    g = jnp.concatenate(
        [buf[pl.ds(c, rows, stride=ROW_SUBLANES), :].astype(BF16) for c in range(ROW_SUBLANES)],
        axis=1)
    info = rinfo_ref[...]
    col = lax.broadcasted_iota(jnp.int32, (tm, rows), 1).astype(F32)
    pick0 = jnp.where(col == info[:, 2:3], 1.0, 0.0).astype(BF16)
    pick1 = jnp.where(col == info[:, 3:4], 1.0, 0.0).astype(BF16)
    fx = info[:, 4:5] * _dot(pick0, g) + info[:, 5:6] * _dot(pick1, g)
    o_ref[...] = x1_ref[...] + g2_ref[0] * (_rms(fx) * gpost_ref[...])


def _combine(run_dst, run_len, x1, rinfo, g2, gpost, ys, seq):
    t, d = x1.shape
    tm = MOVE_TM
    tpb = seq // tm
    return pl.pallas_call(
        _combine_kernel,
        grid_spec=pltpu.PrefetchScalarGridSpec(
            num_scalar_prefetch=2,
            grid=(t // tm,),
            in_specs=[pl.BlockSpec((tm, d), lambda i, *_: (i, 0)),
                      pl.BlockSpec((tm, LANES), lambda i, *_: (i, 0)),
                      pl.BlockSpec((1, 1, d), lambda i, *_: (i // tpb, 0, 0)),
                      pl.BlockSpec((1, d), lambda i, *_: (0, 0)),
                      pl.BlockSpec(memory_space=pl.ANY)],
            out_specs=pl.BlockSpec((tm, d), lambda i, *_: (i, 0)),
            scratch_shapes=[pltpu.VMEM((2, 2 * tm * ROW_SUBLANES, LANES), F32),
                            pltpu.SemaphoreType.DMA((2,))]),
        out_shape=jax.ShapeDtypeStruct((t, d), F32),
        compiler_params=_params("arbitrary"),
    )(run_dst, run_len, x1, rinfo, g2, gpost, ys)


def _rope_tables(seq):
    pos = jnp.arange(seq, dtype=jnp.int32)
    row = (pos // GRID_W).astype(F32)
    col = (pos % GRID_W).astype(F32)
    axis_dim = HEAD_DIM // 2
    inv_freq = ROPE_THETA ** (-jnp.arange(0, axis_dim, 2, dtype=F32) / axis_dim)
    ang = jnp.concatenate([row[:, None] * inv_freq, col[:, None] * inv_freq], axis=-1)
    pair = (jnp.arange(LANES) % HEAD_DIM) // 2
    cos = jnp.cos(ang)[:, pair]
    sin = jnp.sin(ang)[:, pair]
    even = (jnp.arange(LANES) % 2) == 0
    return cos, jnp.where(even, -sin, 0.0), jnp.where(even, 0.0, sin)


def _segment_ones(n):
    seg = jnp.arange(n) // HEAD_DIM
    return (seg[:, None] == seg[None, :]).astype(BF16)


def kernel(x, c, ctx, c_ctx, w_mod, b_mod, attn_pre_norm, attn_post_norm, w_in, a_sink,
           b_q_norm, b_k_norm, a_out_norm, b_out_norm, w_out, ffn_pre_norm, ffn_post_norm,
           w_group, b_group, w_router, b_router, w_gate, w_up, w_down):
    batch, seq, d = x.shape
    ctx_len = ctx.shape[1]
    assert w_mod.shape[0] == 1, "single-layer stack only (context stream is never updated)"
    assert seq % ATTN_A_TQ == 0 and seq >= ATTN_A_TQ + 2 * WINDOW
    assert seq % PROJ_TM == 0 and seq % ATTN_B_TQ == 0 and seq % OUT_TM == 0 and seq % MOVE_TM == 0
    t = batch * seq
    nq = d // 2
    nkv = nq // KV_GROUP
    assert nkv == LANES and w_in.shape[2] == 2 * nq + 4 * nkv

    cc = jnp.concatenate([c, c_ctx[None, :], jnp.zeros((16 - batch - 1, d), F32)], axis=0)
    mod = _modulation(cc, w_mod[0], b_mod[0])
    sh1, sc1, g1, sh2, sc2, g2 = (m.reshape(batch, 1, d) for m in jnp.split(mod[:batch], 6, axis=-1))
    csh1, csc1 = (m.reshape(1, d) for m in jnp.split(mod[batch], 6)[:2])

    x2 = x.reshape(t, d)
    c2 = ctx.reshape(batch * ctx_len, d)
    gpre = attn_pre_norm[0].reshape(1, d)
    w_in_bf = w_in[0].astype(BF16)
    kv_cols = jnp.concatenate([w_in_bf[:, nq:nq + 2 * nkv], w_in_bf[:, 2 * nq + 2 * nkv:]], axis=1)
    qn = jnp.tile(b_q_norm[0], nq // HEAD_DIM).reshape(1, nq)
    kn = jnp.tile(b_k_norm[0], nkv // HEAD_DIM).reshape(1, nkv)
    seg_q, seg_k = _segment_ones(nq), _segment_ones(nkv)
    qat, qbt, ka, vat, kb, vbt = _project_latents(
        x2, sc1, sh1, gpre, w_in_bf, _rope_tables(seq), qn, kn, seg_q, seg_k, seq)
    kca, vcat, kcb, vcbt = _project_context(c2, csc1, csh1, gpre, kv_cols, kn, seg_k, ctx_len)

    oa = _attention_a(a_sink[0], qat, ka, vat, kca, vcat, batch, seq, ctx_len)
    ob = _attention_b(qbt, kb, vbt, kcb, vcbt, batch, seq, ctx_len)

    w_out_bf = w_out[0].astype(BF16)
    w_r = jnp.zeros((d, LANES), F32)
    w_r = w_r.at[:, :N_EXPERTS].set(w_router[0]).at[:, N_EXPERTS:N_EXPERTS + N_GROUPS].set(w_group[0])
    w_r_hi = w_r.astype(BF16)
    w_r_lo = (w_r - w_r_hi.astype(F32)).astype(BF16)
    b_r = jnp.zeros((1, LANES), F32)
    b_r = b_r.at[0, :N_EXPERTS].set(b_router[0]).at[0, N_EXPERTS:N_EXPERTS + N_GROUPS].set(b_group[0])
    x1, h2, rinfo, rt, tcarry, tcnt, counts = _out_and_route(
        oa, ob, x2, g1, sc2, sh2, a_out_norm[0].reshape(1, nq), b_out_norm[0].reshape(1, nq),
        attn_post_norm[0].reshape(1, d), ffn_pre_norm[0].reshape(1, d),
        w_out_bf[:nq], w_out_bf[nq:], w_r_hi, w_r_lo, b_r, seq)

    te = EXPERT_TE
    n_tiles = -(-(2 * t + N_EXPERTS * (te - 1)) // te)
    n_rows = n_tiles * te
    cnt = counts[0, :N_EXPERTS].astype(jnp.int32)
    padded = ((cnt + te - 1) // te) * te
    ends = jnp.cumsum(padded)
    offs = ends - padded
    run_dst = (offs[None, :] + tcarry[:, 0, :N_EXPERTS].astype(jnp.int32)).reshape(-1)
    run_len = tcnt[:, 0, :N_EXPERTS].astype(jnp.int32).reshape(-1)
    n_valid = (ends[-1] // te).astype(jnp.int32).reshape(1)
    tile_start = jnp.arange(n_tiles, dtype=jnp.int32) * te
    tile_expert = jnp.sum(ends[None, :] <= tile_start[:, None], axis=1).astype(jnp.int32)
    last_expert = tile_expert[jnp.maximum(n_valid[0] - 1, 0)]
    tile_expert = jnp.where(tile_start < ends[-1], tile_expert, last_expert)

    xs = _dispatch(run_dst, run_len, ends.astype(jnp.int32), n_valid, h2, rt, n_rows)
    ys = _expert_mlp(tile_expert, n_valid, xs, w_gate[0], w_up[0], w_down[0])
    out = _combine(run_dst, run_len, x1, rinfo, g2, ffn_post_norm[0].reshape(1, d), ys, seq)
    return out.reshape(batch, seq, d)
```

```python
import functools

import jax
import jax.numpy as jnp
from jax import lax
from jax.experimental import pallas as pl
from jax.experimental.pallas import tpu as pltpu

F32 = jnp.float32
BF16 = jnp.bfloat16

GRID_W = 64
HEAD_DIM = 64
KV_GROUP = 4
WINDOW = 128
ROPE_THETA = 10000.0
N_GROUPS = 4
EXPERTS_PER_GROUP = 8
N_EXPERTS = N_GROUPS * EXPERTS_PER_GROUP
EPS = 1e-6
NEG_INF = -1e30
LOG2_E = 1.4426950408889634
SAFE_SOFTMAX_SHIFT = 40.0

LANES = 128
V7X_VMEM_LIMIT = 56 * 1024 * 1024

PROJ_TM = 512
ATTN_A_TQ = 512
ATTN_B_TQ = 512
OUT_TM = 512
EXPERT_TE = 512
MOVE_TM = OUT_TM


def _params(*sem):
    return pltpu.CompilerParams(dimension_semantics=sem, vmem_limit_bytes=V7X_VMEM_LIMIT)


def _dot(a, b):
    return jnp.dot(a, b, preferred_element_type=F32)


def _dot_nt(a, b):
    return lax.dot_general(a, b, (((1,), (1,)), ((), ())), preferred_element_type=F32)


def _rms(x):
    return x * lax.rsqrt(jnp.mean(x * x, axis=-1, keepdims=True) + EPS)


def _split_bf16(x):
    hi = x.astype(BF16)
    lo = (x - hi.astype(F32)).astype(BF16)
    return hi, lo


def _mod_kernel(c_ref, w_ref, b_ref, o_ref):
    cc = c_ref[...]
    s = cc * jax.nn.sigmoid(cc)
    s_hi, s_lo = _split_bf16(s)
    w_hi, w_lo = _split_bf16(w_ref[...])
    o_ref[...] = _dot(s_hi, w_hi) + _dot(s_lo, w_hi) + _dot(s_hi, w_lo) + b_ref[...]


def _modulation(cc, w_mod, b_mod):
    rows, d = cc.shape
    n = w_mod.shape[1]
    bn = 1024
    return pl.pallas_call(
        _mod_kernel,
        grid=(n // bn,),
        in_specs=[pl.BlockSpec((rows, d), lambda i: (0, 0)),
                  pl.BlockSpec((d, bn), lambda i: (0, i)),
                  pl.BlockSpec((1, bn), lambda i: (0, i))],
        out_specs=pl.BlockSpec((rows, bn), lambda i: (0, i)),
        out_shape=jax.ShapeDtypeStruct((rows, n), F32),
        compiler_params=_params("arbitrary"),
    )(cc, w_mod, b_mod.reshape(1, n))


def _rope(x, cos, sin_a, sin_b):
    return x * cos + pltpu.roll(x, LANES - 1, 1) * sin_a + pltpu.roll(x, 1, 1) * sin_b


def _head_norm(x, seg_ref, gain):
    ss = _dot((x * x).astype(BF16), seg_ref[...])
    return x * lax.rsqrt(ss * (1.0 / HEAD_DIM) + EPS) * gain


def _store_pair_variants(ref, t):
    lane = lax.broadcasted_iota(jnp.int32, t.shape, 1)
    lo = lane < HEAD_DIM
    sw = pltpu.roll(t, HEAD_DIM, 1)
    zero = jnp.zeros_like(t)
    ref[0] = jnp.where(lo, t, zero).astype(BF16)
    ref[1] = jnp.where(lo, zero, sw).astype(BF16)
    ref[2] = jnp.where(lo, sw, zero).astype(BF16)
    ref[3] = jnp.where(lo, zero, t).astype(BF16)


def _proj_kernel(x_ref, sc_ref, sh_ref, gpre_ref, w_ref, cos_ref, sa_ref, sb_ref,
                 qn_ref, kn_ref, seg_q_ref, seg_k_ref,
                 qa_ref, ka_ref, va_ref, qbt_ref, kb_ref, vbt_ref):
    h = _rms(x_ref[...]) * gpre_ref[...] * (1.0 + sc_ref[0]) + sh_ref[0]
    p = _dot(h.astype(BF16), w_ref[...])
    cos, sa, sb = cos_ref[...], sa_ref[...], sb_ref[...]
    nq = qa_ref.shape[1]
    q_scale = HEAD_DIM ** -0.5 * LOG2_E
    for c in range(nq // LANES):
        qa_ref[:, c * LANES:(c + 1) * LANES] = (
            _rope(p[:, c * LANES:(c + 1) * LANES], cos, sa, sb) * q_scale).astype(BF16)
    o = nq
    _store_pair_variants(ka_ref, _rope(p[:, o:o + LANES], cos, sa, sb))
    _store_pair_variants(va_ref, p[:, o + LANES:o + 2 * LANES])
    o += 2 * LANES
    qb = _head_norm(p[:, o:o + nq], seg_q_ref, qn_ref[...])
    for c in range(nq // LANES):
        qbt_ref[c * LANES:(c + 1) * LANES, :] = (
            _rope(qb[:, c * LANES:(c + 1) * LANES], cos, sa, sb) * q_scale).T.astype(BF16)
    o += nq
    kb = _head_norm(p[:, o:o + LANES], seg_k_ref, kn_ref[...])
    kb_ref[...] = _rope(kb, cos, sa, sb).astype(BF16)
    vbt_ref[...] = p[:, o + LANES:o + 2 * LANES].T.astype(BF16)


def _ctx_proj_kernel(x_ref, sc_ref, sh_ref, gpre_ref, w_ref, kn_ref, seg_k_ref,
                     ka_ref, va_ref, kb_ref, vbt_ref):
    h = _rms(x_ref[...]) * gpre_ref[...] * (1.0 + sc_ref[...]) + sh_ref[...]
    p = _dot(h.astype(BF16), w_ref[...])
    _store_pair_variants(ka_ref, p[:, 0:LANES])
    _store_pair_variants(va_ref, p[:, LANES:2 * LANES])
    kb_ref[...] = _head_norm(p[:, 2 * LANES:3 * LANES], seg_k_ref, kn_ref[...]).astype(BF16)
    vbt_ref[...] = p[:, 3 * LANES:4 * LANES].T.astype(BF16)


def _project_latents(x2, sc, sh, gpre, w_in, tables, qn, kn, seg_q, seg_k, seq):
    t, d = x2.shape
    tm = PROJ_TM
    tpb = seq // tm
    nq = seg_q.shape[0]
    const = lambda shape: pl.BlockSpec(shape, lambda i: (0,) * len(shape))
    per_batch = pl.BlockSpec((1, 1, d), lambda i: (i // tpb, 0, 0))
    table = pl.BlockSpec((tm, LANES), lambda i: (i % tpb, 0))
    k_spec = pl.BlockSpec((tm, LANES), lambda i: (i, 0))
    k_shape = jax.ShapeDtypeStruct((t, LANES), BF16)
    vt_spec = pl.BlockSpec((LANES, tm), lambda i: (0, i))
    vt_shape = jax.ShapeDtypeStruct((LANES, t), BF16)
    qt_spec = pl.BlockSpec((nq, tm), lambda i: (0, i))
    qt_shape = jax.ShapeDtypeStruct((nq, t), BF16)
    q_spec = pl.BlockSpec((tm, nq), lambda i: (i, 0))
    q_shape = jax.ShapeDtypeStruct((t, nq), BF16)
    kv4_spec = pl.BlockSpec((4, tm, LANES), lambda i: (0, i, 0))
    kv4_shape = jax.ShapeDtypeStruct((4, t, LANES), BF16)
    return pl.pallas_call(
        _proj_kernel,
        grid=(t // tm,),
        in_specs=[pl.BlockSpec((tm, d), lambda i: (i, 0)), per_batch, per_batch, const((1, d)),
                  const(w_in.shape), table, table, table,
                  const((1, nq)), const((1, LANES)), const(seg_q.shape), const(seg_k.shape)],
        out_specs=[q_spec, kv4_spec, kv4_spec, qt_spec, k_spec, vt_spec],
        out_shape=[q_shape, kv4_shape, kv4_shape, qt_shape, k_shape, vt_shape],
        compiler_params=_params("arbitrary"),
    )(x2, sc, sh, gpre, w_in, *tables, qn, kn, seg_q, seg_k)


def _project_context(c2, sc, sh, gpre, w_kv, kn, seg_k, ctx_len):
    t, d = c2.shape
    const = lambda shape: pl.BlockSpec(shape, lambda i: (0,) * len(shape))
    k_spec = pl.BlockSpec((ctx_len, LANES), lambda i: (i, 0))
    k_shape = jax.ShapeDtypeStruct((t, LANES), BF16)
    vt_spec = pl.BlockSpec((LANES, ctx_len), lambda i: (0, i))
    vt_shape = jax.ShapeDtypeStruct((LANES, t), BF16)
    kv4_spec = pl.BlockSpec((4, ctx_len, LANES), lambda i: (0, i, 0))
    kv4_shape = jax.ShapeDtypeStruct((4, t, LANES), BF16)
    return pl.pallas_call(
        _ctx_proj_kernel,
        grid=(t // ctx_len,),
        in_specs=[pl.BlockSpec((ctx_len, d), lambda i: (i, 0)), const((1, d)), const((1, d)),
                  const((1, d)), const(w_kv.shape), const((1, LANES)), const(seg_k.shape)],
        out_specs=[kv4_spec, kv4_spec, k_spec, vt_spec],
        out_shape=[kv4_shape, kv4_shape, k_shape, vt_shape],
        compiler_params=_params("arbitrary"),
    )(c2, sc, sh, gpre, w_kv, kn, seg_k)


def _attend_t(w, k, kc, vt, vct, shift=None):
    st = _dot(k, w)
    sct = _dot(kc, w)
    if shift is None:
        shift = jnp.maximum(jnp.max(st, axis=0, keepdims=True),
                            jnp.max(sct, axis=0, keepdims=True))
    pt = jnp.exp2(st - shift)
    pct = jnp.exp2(sct - shift)
    denom = jnp.sum(pt, axis=0, keepdims=True) + jnp.sum(pct, axis=0, keepdims=True)
    o2 = _dot(vt, pt.astype(BF16)) + _dot(vct, pct.astype(BF16))
    return o2, denom


def _all_heads_t(qt_ref, o_ref, attend):
    tq = qt_ref.shape[1]
    n_kv = LANES // HEAD_DIM
    zeros = jnp.zeros((HEAD_DIM, tq), BF16)
    outs = []
    for h in range(qt_ref.shape[0] // HEAD_DIM):
        g = h // KV_GROUP
        qh = qt_ref[h * HEAD_DIM:(h + 1) * HEAD_DIM, :]
        w = jnp.concatenate([zeros] * g + [qh] + [zeros] * (n_kv - 1 - g), axis=0)
        o2, denom = attend(h, w)
        outs.append(o2[g * HEAD_DIM:(g + 1) * HEAD_DIM, :] / denom)
    o_ref[...] = jnp.concatenate(outs, axis=0).T.astype(BF16)


def _attn_b_kernel(bound_ref, qt_ref, k_ref, vt_ref, kc_ref, vct_ref, o_ref):
    k, kc, vt, vct = k_ref[...], kc_ref[...], vt_ref[...], vct_ref[...]
    bound = bound_ref[0]

    @pl.when(bound <= SAFE_SOFTMAX_SHIFT)
    def _():
        _all_heads_t(qt_ref, o_ref, lambda h, w: _attend_t(w, k, kc, vt, vct, shift=bound))

    @pl.when(jnp.logical_not(bound <= SAFE_SOFTMAX_SHIFT))
    def _():
        _all_heads_t(qt_ref, o_ref, lambda h, w: _attend_t(w, k, kc, vt, vct))


def _attn_specs(nq, tq, seq, ctx_len):
    nqb = seq // tq
    return dict(
        qt=pl.BlockSpec((nq, tq), lambda b, i: (0, b * nqb + i)),
        k=pl.BlockSpec((seq, LANES), lambda b, i: (b, 0)),
        vt=pl.BlockSpec((LANES, seq), lambda b, i: (0, b)),
        kc=pl.BlockSpec((ctx_len, LANES), lambda b, i: (b, 0)),
        vct=pl.BlockSpec((LANES, ctx_len), lambda b, i: (0, b)),
        out=pl.BlockSpec((tq, nq), lambda b, i: (b * nqb + i, 0)))


def _attention_b(score_bound, qbt, kb, vbt, kcb, vcbt, batch, seq, ctx_len):
    nq, t = qbt.shape
    tq = ATTN_B_TQ
    sp = _attn_specs(nq, tq, seq, ctx_len)
    return pl.pallas_call(
        _attn_b_kernel,
        grid=(batch, seq // tq),
        in_specs=[pl.BlockSpec(memory_space=pltpu.SMEM),
                  sp["qt"], sp["k"], sp["vt"], sp["kc"], sp["vct"]],
        out_specs=sp["out"],
        out_shape=jax.ShapeDtypeStruct((t, nq), BF16),
        compiler_params=_params("arbitrary", "arbitrary"),
    )(score_bound, qbt, kb, vbt, kcb, vcbt)


def _attn_a_kernel(sink_ref, q_ref, k_ref, v_ref, kc_ref, vc_ref, o_ref, bias_ref, *, seq):
    i = pl.program_id(1)
    tq = q_ref.shape[0]
    win = tq + 2 * WINDOW
    start = pl.multiple_of(jnp.clip(i * tq - WINDOW, 0, seq - win), WINDOW)
    qpos = i * tq + lax.broadcasted_iota(jnp.int32, (tq, win), 0)
    kpos = start + lax.broadcasted_iota(jnp.int32, (tq, win), 1)
    bias_ref[...] = jnp.where(jnp.abs(kpos - qpos) <= WINDOW, 0.0, NEG_INF).astype(F32)
    n_pairs = q_ref.shape[1] // LANES
    pairs_per_kv = KV_GROUP // 2
    rowmax = lambda s: jnp.max(s, axis=-1, keepdims=True)
    rowsum = lambda s: jnp.sum(s, axis=-1, keepdims=True)
    for pair in range(n_pairs):
        g = pair // pairs_per_kv
        q2 = q_ref[:, pair * LANES:(pair + 1) * LANES]
        acc = None
        for half in range(2):
            kv = 2 * g + half
            sink = sink_ref[2 * pair + half] * LOG2_E
            s = _dot_nt(q2, k_ref[kv, pl.ds(start, win), :]) + bias_ref[...]
            sc = _dot_nt(q2, kc_ref[kv])
            m = jnp.maximum(jnp.maximum(rowmax(s), rowmax(sc)), sink)
            p = jnp.exp2(s - m)
            pc = jnp.exp2(sc - m)
            denom = rowsum(p) + rowsum(pc) + jnp.exp2(sink - m)
            o = (_dot(p.astype(BF16), v_ref[kv, pl.ds(start, win), :])
                 + _dot(pc.astype(BF16), vc_ref[kv])) / denom
            acc = o if acc is None else acc + o
        o_ref[:, pair * LANES:(pair + 1) * LANES] = acc.astype(BF16)


def _attention_a(sink, qa, ka4, va4, kca4, vca4, batch, seq, ctx_len):
    t, nq = qa.shape
    tq = ATTN_A_TQ
    nqb = seq // tq
    lat = pl.BlockSpec((4, seq, LANES), lambda b, i: (0, b, 0))
    cx = pl.BlockSpec((4, ctx_len, LANES), lambda b, i: (0, b, 0))
    qs = pl.BlockSpec((tq, nq), lambda b, i: (b * nqb + i, 0))
    return pl.pallas_call(
        functools.partial(_attn_a_kernel, seq=seq),
        grid=(batch, nqb),
        in_specs=[pl.BlockSpec(memory_space=pltpu.SMEM), qs, lat, lat, cx, cx],
        out_specs=qs,
        out_shape=jax.ShapeDtypeStruct((t, nq), BF16),
        scratch_shapes=[pltpu.VMEM((tq, tq + 2 * WINDOW), F32)],
        compiler_params=_params("arbitrary", "arbitrary"),
    )(sink, qa, ka4, va4, kca4, vca4)


def _out_kernel(oa_ref, ob_ref, x_ref, g1_ref, sc2_ref, sh2_ref, ga_ref, gb_ref, gpost_ref,
                gpre2_ref, woa_ref, wob_ref, wrh_ref, wrl_ref, br_ref,
                x1_ref, h2_ref, rinfo_ref, rt_ref, tcarry_ref, tcnt_ref, cnt_ref, carry_ref):
    step = pl.program_id(0)

    @pl.when(step == 0)
    def _():
        carry_ref[...] = jnp.zeros_like(carry_ref)

    na = _rms(oa_ref[...].astype(F32)) * ga_ref[...]
    nb = _rms(ob_ref[...].astype(F32)) * gb_ref[...]
    ox = _dot(na.astype(BF16), woa_ref[...]) + _dot(nb.astype(BF16), wob_ref[...])
    x1 = x_ref[...] + g1_ref[0] * (_rms(ox) * gpost_ref[...])
    x1_ref[...] = x1
    h2 = _rms(x1) * gpre2_ref[...] * (1.0 + sc2_ref[0]) + sh2_ref[0]
    h2_ref[...] = h2

    h_hi, h_lo = _split_bf16(h2)
    logits = (_dot(h_hi, wrh_ref[...]) + _dot(h_lo, wrh_ref[...]) + _dot(h_hi, wrl_ref[...])
              + br_ref[...])
    tm = logits.shape[0]
    lane = lax.broadcasted_iota(jnp.int32, logits.shape, 1)
    lanef = lane.astype(F32)
    big = jnp.float32(1e9)
    ninf = jnp.float32(-jnp.inf)
    rowmax = lambda v: jnp.max(v, axis=-1, keepdims=True)
    rowmin = lambda v: jnp.min(v, axis=-1, keepdims=True)
    rowsum = lambda v: jnp.sum(v, axis=-1, keepdims=True)

    gmask = (lane >= N_EXPERTS) & (lane < N_EXPERTS + N_GROUPS)
    lg = jnp.where(gmask, logits, ninf)
    gmax = rowmax(lg)
    gidx = rowmin(jnp.where(lg == gmax, lanef, big)) - N_EXPERTS
    g_w = 1.0 / rowsum(jnp.exp(lg - gmax))
    lane_group = (lane // EXPERTS_PER_GROUP).astype(F32)
    emask = (lane < N_EXPERTS) & (lane_group == gidx)
    le = jnp.where(emask, logits, ninf)
    m1 = rowmax(le)
    i1 = rowmin(jnp.where(le == m1, lanef, big))
    le2 = jnp.where(lanef == i1, ninf, le)
    m2 = rowmax(le2)
    i2 = rowmin(jnp.where(le2 == m2, lanef, big))
    e2 = jnp.exp(m2 - m1)
    w0 = g_w / (1.0 + e2)
    w1 = g_w * e2 / (1.0 + e2)

    hit1 = lanef == i1
    hit2 = lanef == i2
    onehot = jnp.where(hit1 | hit2, 1.0, 0.0).astype(F32)
    r = lax.broadcasted_iota(jnp.int32, (tm, tm), 0)
    c = lax.broadcasted_iota(jnp.int32, (tm, tm), 1)
    strict_lower = jnp.where(r > c, 1.0, 0.0).astype(BF16)
    within = _dot(strict_lower, onehot.astype(BF16))
    tile_cnt = jnp.sum(onehot, axis=0, keepdims=True)
    incl = jnp.broadcast_to(tile_cnt, (8, LANES))
    lane8 = lax.broadcasted_iota(jnp.int32, (8, LANES), 1)
    shift = 1
    while shift < LANES:
        incl = incl + jnp.where(lane8 >= shift, pltpu.roll(incl, shift, 1), 0.0)
        shift *= 2
    local = within + (incl[0:1] - tile_cnt)
    pos0 = rowsum(jnp.where(hit1, local, 0.0))
    pos1 = rowsum(jnp.where(hit2, local, 0.0))
    tcarry_ref[0] = carry_ref[...]
    tcnt_ref[0] = tile_cnt
    carry_ref[...] += tile_cnt
    cnt_ref[...] = carry_ref[...]

    info = jnp.zeros_like(logits)
    for k, val in enumerate((i1, i2, pos0, pos1, w0, w1)):
        info = jnp.where(lane == k, val, info)
    rinfo_ref[...] = info
    rt_ref[...] = info.T[0:8, :]


def _out_and_route(oa, ob, x2, g1, sc2, sh2, ga, gb, gpost, gpre2, woa, wob, wrh, wrl, br, seq):
    t, d = x2.shape
    tm = OUT_TM
    tpb = seq // tm
    nq = oa.shape[1]
    const = lambda shape: pl.BlockSpec(shape, lambda i: (0,) * len(shape))
    per_batch = pl.BlockSpec((1, 1, d), lambda i: (i // tpb, 0, 0))
    rows = lambda n: pl.BlockSpec((tm, n), lambda i: (i, 0))
    per_tile = pl.BlockSpec((1, 1, LANES), lambda i: (i, 0, 0))
    return pl.pallas_call(
        _out_kernel,
        grid=(t // tm,),
        in_specs=[rows(nq), rows(nq), rows(d), per_batch, per_batch, per_batch,
                  const((1, nq)), const((1, nq)), const((1, d)), const((1, d)),
                  const(woa.shape), const(wob.shape), const(wrh.shape), const(wrl.shape),
                  const((1, LANES))],
        out_specs=[rows(d), rows(d), rows(LANES), pl.BlockSpec((8, tm), lambda i: (0, i)),
                   per_tile, per_tile, const((1, LANES))],
        out_shape=[jax.ShapeDtypeStruct((t, d), F32), jax.ShapeDtypeStruct((t, d), F32),
                   jax.ShapeDtypeStruct((t, LANES), F32), jax.ShapeDtypeStruct((8, t), F32),
                   jax.ShapeDtypeStruct((t // tm, 1, LANES), F32),
                   jax.ShapeDtypeStruct((t // tm, 1, LANES), F32),
                   jax.ShapeDtypeStruct((1, LANES), F32)],
        scratch_shapes=[pltpu.VMEM((1, LANES), F32)],
        compiler_params=_params("arbitrary"),
    )(oa, ob, x2, g1, sc2, sh2, ga, gb, gpost, gpre2, woa, wob, wrh, wrl, br)


ROW_SUBLANES = 8


def _for_each_run_piece(rdst_ref, rlen_ref, tile, max_len, fn):
    n_bits = max_len.bit_length()

    def run(e, local):
        length = rlen_ref[tile * N_EXPERTS + e]
        dst = rdst_ref[tile * N_EXPERTS + e]
        for b in range(n_bits):
            size = 1 << b

            @pl.when(((length >> b) & 1) == 1)
            def _():
                done = length & (size - 1)
                fn(local + done, dst + done, size)
        return local + length

    lax.fori_loop(0, N_EXPERTS, run, 0)


def _token_rows(ref, row0, n_rows):
    start = row0 * ROW_SUBLANES
    if not isinstance(start, int):
        start = pl.multiple_of(start, ROW_SUBLANES)
    return ref.at[pl.ds(start, n_rows * ROW_SUBLANES)]


def _dispatch_kernel(rdst_ref, rlen_ref, ends_ref, nv_ref, h_ref, rt_ref, xs_ref,
                     sorted_ref, zero_ref, sem, zsem, *, te, n_tiles):
    k = pl.program_id(0)
    nk = pl.num_programs(0)
    tm = h_ref.shape[0]
    rows = 2 * tm
    slot = k % 2

    def wait_slot(s):
        pltpu.make_async_copy(sorted_ref.at[s], _token_rows(xs_ref, 0, rows), sem.at[s]).wait()

    @pl.when(k == 0)
    def _():
        zero_ref[...] = jnp.zeros_like(zero_ref)

        def pad_copy(row0):
            return pltpu.make_async_copy(zero_ref, _token_rows(xs_ref, row0, te), zsem)

        def for_each_pad_tile(fn):
            def expert_pad(e, carry):
                end = ends_ref[e]
                prev = jnp.where(e > 0, ends_ref[jnp.maximum(e - 1, 0)], 0)

                @pl.when(end > prev)
                def _():
                    fn(pad_copy(end - te))
                return carry

            def tail_pad(j, carry):
                fn(pad_copy(j * te))
                return carry

            lax.fori_loop(0, N_EXPERTS, expert_pad, 0)
            lax.fori_loop(nv_ref[0], n_tiles, tail_pad, 0)

        for_each_pad_tile(lambda cp: cp.start())
        for_each_pad_tile(lambda cp: cp.wait())

    @pl.when(k >= 2)
    def _():
        wait_slot(slot)

    pos0 = rt_ref[2:3, :]
    pos1 = rt_ref[3:4, :]
    r = lax.broadcasted_iota(jnp.int32, (rows, tm), 0).astype(F32)
    perm = jnp.where((r == pos0) | (r == pos1), 1.0, 0.0).astype(BF16)
    srt = _dot(perm, h_ref[...].astype(BF16))
    buf = sorted_ref.at[slot]
    for c in range(ROW_SUBLANES):
        buf[pl.ds(c, rows, stride=ROW_SUBLANES), :] = srt[:, c * LANES:(c + 1) * LANES]

    def copy_piece(local, dst, size):
        pltpu.make_async_copy(_token_rows(buf, local, size), _token_rows(xs_ref, dst, size),
                              sem.at[slot]).start()

    _for_each_run_piece(rdst_ref, rlen_ref, k, tm, copy_piece)

    @pl.when(k == nk - 1)
    def _():
        wait_slot(slot)

        @pl.when(nk >= 2)
        def _():
            wait_slot(1 - slot)


def _dispatch(run_dst, run_len, ends, n_valid, h2, rt, n_rows):
    t, d = h2.shape
    assert d == ROW_SUBLANES * LANES
    tm = MOVE_TM
    te = EXPERT_TE
    return pl.pallas_call(
        functools.partial(_dispatch_kernel, te=te, n_tiles=n_rows // te),
        grid_spec=pltpu.PrefetchScalarGridSpec(
            num_scalar_prefetch=4,
            grid=(t // tm,),
            in_specs=[pl.BlockSpec((tm, d), lambda i, *_: (i, 0)),
                      pl.BlockSpec((8, tm), lambda i, *_: (0, i))],
            out_specs=pl.BlockSpec(memory_space=pl.ANY),
            scratch_shapes=[pltpu.VMEM((2, 2 * tm * ROW_SUBLANES, LANES), F32),
                            pltpu.VMEM((te * ROW_SUBLANES, LANES), F32),
                            pltpu.SemaphoreType.DMA((2,)), pltpu.SemaphoreType.DMA(())]),
        out_shape=jax.ShapeDtypeStruct((n_rows * ROW_SUBLANES, LANES), F32),
        compiler_params=_params("arbitrary"),
    )(run_dst, run_len, ends, n_valid, h2, rt)


def _expert_kernel(te_ref, nv_ref, xs_ref, wg_ref, wu_ref, wd_ref, ys_ref, wg_bf, wu_bf, wd_bf):
    j = pl.program_id(0)
    valid = j < nv_ref[0]
    changed = (j == 0) | (te_ref[j] != te_ref[jnp.maximum(j - 1, 0)])

    @pl.when(valid & changed)
    def _():
        wg_bf[...] = wg_ref[0].astype(BF16)
        wu_bf[...] = wu_ref[0].astype(BF16)
        wd_bf[...] = wd_ref[0].astype(BF16)

    @pl.when(valid)
    def _():
        te = xs_ref.shape[0] // ROW_SUBLANES
        xb = jnp.concatenate(
            [xs_ref[pl.ds(c, te, stride=ROW_SUBLANES), :].astype(BF16) for c in range(ROW_SUBLANES)],
            axis=1)
        gate = _dot(xb, wg_bf[...])
        up = _dot(xb, wu_bf[...])
        act = gate * jax.nn.sigmoid(gate) * up
        y = _dot(act.astype(BF16), wd_bf[...])
        for c in range(ROW_SUBLANES):
            ys_ref[pl.ds(c, te, stride=ROW_SUBLANES), :] = y[:, c * LANES:(c + 1) * LANES]

    @pl.when(jnp.logical_not(valid))
    def _():
        ys_ref[...] = jnp.zeros_like(ys_ref)


def _expert_mlp(tile_expert, n_valid, xs, w_gate, w_up, w_down):
    te = EXPERT_TE
    d, ff = w_gate.shape[1:]
    n_rows = xs.shape[0] // ROW_SUBLANES
    blk = (te * ROW_SUBLANES, LANES)
    tile = lambda j, e, nv: (jnp.minimum(j, nv[0] - 1), 0)
    wsel = lambda j, e, nv: (e[j], 0, 0)
    return pl.pallas_call(
        _expert_kernel,
        grid_spec=pltpu.PrefetchScalarGridSpec(
            num_scalar_prefetch=2,
            grid=(n_rows // te,),
            in_specs=[pl.BlockSpec(blk, tile),
                      pl.BlockSpec((1, d, ff), wsel), pl.BlockSpec((1, d, ff), wsel),
                      pl.BlockSpec((1, ff, d), wsel)],
            out_specs=pl.BlockSpec(blk, lambda j, e, nv: (j, 0)),
            scratch_shapes=[pltpu.VMEM((d, ff), BF16), pltpu.VMEM((d, ff), BF16),
                            pltpu.VMEM((ff, d), BF16)]),
        out_shape=jax.ShapeDtypeStruct(xs.shape, F32),
        compiler_params=_params("arbitrary"),
    )(tile_expert, n_valid, xs, w_gate, w_up, w_down)


def _combine_kernel(rdst_ref, rlen_ref, x1_ref, rinfo_ref, g2_ref, gpost_ref, ys_ref, o_ref,
                    gath_ref, sem):
    k = pl.program_id(0)
    nk = pl.num_programs(0)
    tm = x1_ref.shape[0]
    rows = 2 * tm
    slot = k % 2

    def gather_runs(tile, s):
        buf = gath_ref.at[s]

        def copy_piece(local, src, size):
            pltpu.make_async_copy(_token_rows(ys_ref, src, size), _token_rows(buf, local, size),
                                  sem.at[s]).start()

        _for_each_run_piece(rdst_ref, rlen_ref, tile, tm, copy_piece)

    @pl.when(k == 0)
    def _():
        gather_runs(0, 0)

    @pl.when(k + 1 < nk)
    def _():
        gather_runs(k + 1, 1 - slot)

    buf = gath_ref.at[slot]
    pltpu.make_async_copy(_token_rows(ys_ref, 0, rows), buf, sem.at[slot]).wait()
    g = jnp.concatenate(
        [buf[pl.ds(c, rows, stride=ROW_SUBLANES), :].astype(BF16) for c in range(ROW_SUBLANES)],
        axis=1)
    info = rinfo_ref[...]
    col = lax.broadcasted_iota(jnp.int32, (tm, rows), 1).astype(F32)
    pick0 = jnp.where(col == info[:, 2:3], 1.0, 0.0).astype(BF16)
    pick1 = jnp.where(col == info[:, 3:4], 1.0, 0.0).astype(BF16)
    fx = info[:, 4:5] * _dot(pick0, g) + info[:, 5:6] * _dot(pick1, g)
    o_ref[...] = x1_ref[...] + g2_ref[0] * (_rms(fx) * gpost_ref[...])


def _combine(run_dst, run_len, x1, rinfo, g2, gpost, ys, seq):
    t, d = x1.shape
    tm = MOVE_TM
    tpb = seq // tm
    return pl.pallas_call(
        _combine_kernel,
        grid_spec=pltpu.PrefetchScalarGridSpec(
            num_scalar_prefetch=2,
            grid=(t // tm,),
            in_specs=[pl.BlockSpec((tm, d), lambda i, *_: (i, 0)),
                      pl.BlockSpec((tm, LANES), lambda i, *_: (i, 0)),
                      pl.BlockSpec((1, 1, d), lambda i, *_: (i // tpb, 0, 0)),
                      pl.BlockSpec((1, d), lambda i, *_: (0, 0)),
                      pl.BlockSpec(memory_space=pl.ANY)],
            out_specs=pl.BlockSpec((tm, d), lambda i, *_: (i, 0)),
            scratch_shapes=[pltpu.VMEM((2, 2 * tm * ROW_SUBLANES, LANES), F32),
                            pltpu.SemaphoreType.DMA((2,))]),
        out_shape=jax.ShapeDtypeStruct((t, d), F32),
        compiler_params=_params("arbitrary"),
    )(run_dst, run_len, x1, rinfo, g2, gpost, ys)


def _rope_tables(seq):
    pos = jnp.arange(seq, dtype=jnp.int32)
    row = (pos // GRID_W).astype(F32)
    col = (pos % GRID_W).astype(F32)
    axis_dim = HEAD_DIM // 2
    inv_freq = ROPE_THETA ** (-jnp.arange(0, axis_dim, 2, dtype=F32) / axis_dim)
    ang = jnp.concatenate([row[:, None] * inv_freq, col[:, None] * inv_freq], axis=-1)
    pair = (jnp.arange(LANES) % HEAD_DIM) // 2
    cos = jnp.cos(ang)[:, pair]
    sin = jnp.sin(ang)[:, pair]
    even = (jnp.arange(LANES) % 2) == 0
    return cos, jnp.where(even, -sin, 0.0), jnp.where(even, 0.0, sin)


def _segment_ones(n):
    seg = jnp.arange(n) // HEAD_DIM
    return (seg[:, None] == seg[None, :]).astype(BF16)


def kernel(x, c, ctx, c_ctx, w_mod, b_mod, attn_pre_norm, attn_post_norm, w_in, a_sink,
           b_q_norm, b_k_norm, a_out_norm, b_out_norm, w_out, ffn_pre_norm, ffn_post_norm,
           w_group, b_group, w_router, b_router, w_gate, w_up, w_down):
    batch, seq, d = x.shape
    ctx_len = ctx.shape[1]
    assert w_mod.shape[0] == 1, "single-layer stack only (context stream is never updated)"
    assert seq % ATTN_A_TQ == 0 and seq >= ATTN_A_TQ + 2 * WINDOW
    assert seq % PROJ_TM == 0 and seq % ATTN_B_TQ == 0 and seq % OUT_TM == 0 and seq % MOVE_TM == 0
    t = batch * seq
    nq = d // 2
    nkv = nq // KV_GROUP
    assert nkv == LANES and w_in.shape[2] == 2 * nq + 4 * nkv

    cc = jnp.concatenate([c, c_ctx[None, :], jnp.zeros((16 - batch - 1, d), F32)], axis=0)
    mod = _modulation(cc, w_mod[0], b_mod[0])
    sh1, sc1, g1, sh2, sc2, g2 = (m.reshape(batch, 1, d) for m in jnp.split(mod[:batch], 6, axis=-1))
    csh1, csc1 = (m.reshape(1, d) for m in jnp.split(mod[batch], 6)[:2])

    x2 = x.reshape(t, d)
    c2 = ctx.reshape(batch * ctx_len, d)
    gpre = attn_pre_norm[0].reshape(1, d)
    w_in_bf = w_in[0].astype(BF16)
    kv_cols = jnp.concatenate([w_in_bf[:, nq:nq + 2 * nkv], w_in_bf[:, 2 * nq + 2 * nkv:]], axis=1)
    qn = jnp.tile(b_q_norm[0], nq // HEAD_DIM).reshape(1, nq)
    kn = jnp.tile(b_k_norm[0], nkv // HEAD_DIM).reshape(1, nkv)
    seg_q, seg_k = _segment_ones(nq), _segment_ones(nkv)
    qa, ka4, va4, qbt, kb, vbt = _project_latents(
        x2, sc1, sh1, gpre, w_in_bf, _rope_tables(seq), qn, kn, seg_q, seg_k, seq)
    kca4, vca4, kcb, vcbt = _project_context(c2, csc1, csh1, gpre, kv_cols, kn, seg_k, ctx_len)

    oa = _attention_a(a_sink[0], qa, ka4, va4, kca4, vca4, batch, seq, ctx_len)
    score_bound = (1.01 * HEAD_DIM ** 0.5 * LOG2_E
                   * jnp.max(jnp.abs(b_q_norm[0])) * jnp.max(jnp.abs(b_k_norm[0]))).reshape(1)
    ob = _attention_b(score_bound, qbt, kb, vbt, kcb, vcbt, batch, seq, ctx_len)

    w_out_bf = w_out[0].astype(BF16)
    w_r = jnp.zeros((d, LANES), F32)
    w_r = w_r.at[:, :N_EXPERTS].set(w_router[0]).at[:, N_EXPERTS:N_EXPERTS + N_GROUPS].set(w_group[0])
    w_r_hi = w_r.astype(BF16)
    w_r_lo = (w_r - w_r_hi.astype(F32)).astype(BF16)
    b_r = jnp.zeros((1, LANES), F32)
    b_r = b_r.at[0, :N_EXPERTS].set(b_router[0]).at[0, N_EXPERTS:N_EXPERTS + N_GROUPS].set(b_group[0])
    x1, h2, rinfo, rt, tcarry, tcnt, counts = _out_and_route(
        oa, ob, x2, g1, sc2, sh2, a_out_norm[0].reshape(1, nq), b_out_norm[0].reshape(1, nq),
        attn_post_norm[0].reshape(1, d), ffn_pre_norm[0].reshape(1, d),
        w_out_bf[:nq], w_out_bf[nq:], w_r_hi, w_r_lo, b_r, seq)

    te = EXPERT_TE
    n_tiles = -(-(2 * t + N_EXPERTS * (te - 1)) // te)
    n_rows = n_tiles * te
    cnt = counts[0, :N_EXPERTS].astype(jnp.int32)
    padded = ((cnt + te - 1) // te) * te
    ends = jnp.cumsum(padded)
    offs = ends - padded
    run_dst = (offs[None, :] + tcarry[:, 0, :N_EXPERTS].astype(jnp.int32)).reshape(-1)
    run_len = tcnt[:, 0, :N_EXPERTS].astype(jnp.int32).reshape(-1)
    n_valid = (ends[-1] // te).astype(jnp.int32).reshape(1)
    tile_start = jnp.arange(n_tiles, dtype=jnp.int32) * te
    tile_expert = jnp.sum(ends[None, :] <= tile_start[:, None], axis=1).astype(jnp.int32)
    last_expert = tile_expert[jnp.maximum(n_valid[0] - 1, 0)]
    tile_expert = jnp.where(tile_start < ends[-1], tile_expert, last_expert)

    xs = _dispatch(run_dst, run_len, ends.astype(jnp.int32), n_valid, h2, rt, n_rows)
    ys = _expert_mlp(tile_expert, n_valid, xs, w_gate[0], w_up[0], w_down[0])
    out = _combine(run_dst, run_len, x1, rinfo, g2, ffn_post_norm[0].reshape(1, d), ys, seq)
    return out.reshape(batch, seq, d)
```

```python
import functools

import jax
import jax.numpy as jnp
from jax import lax
from jax.experimental import pallas as pl
from jax.experimental.pallas import tpu as pltpu

F32 = jnp.float32
BF16 = jnp.bfloat16

GRID_W = 64
HEAD_DIM = 64
KV_GROUP = 4
WINDOW = 128
ROPE_THETA = 10000.0
N_GROUPS = 4
EXPERTS_PER_GROUP = 8
N_EXPERTS = N_GROUPS * EXPERTS_PER_GROUP
EPS = 1e-6
NEG_INF = -1e30
LOG2_E = 1.4426950408889634
SAFE_SOFTMAX_SHIFT = 40.0

LANES = 128
V7X_VMEM_LIMIT = 56 * 1024 * 1024

PROJ_TM = 512
ATTN_A_TQ = 512
ATTN_B_TQ = 256
OUT_TM = 512
EXPERT_TE = 512
MOVE_TM = OUT_TM


def _params(*sem):
    return pltpu.CompilerParams(dimension_semantics=sem, vmem_limit_bytes=V7X_VMEM_LIMIT)


def _dot(a, b):
    return jnp.dot(a, b, preferred_element_type=F32)


def _dot_nt(a, b):
    return lax.dot_general(a, b, (((1,), (1,)), ((), ())), preferred_element_type=F32)


def _rms(x):
    return x * lax.rsqrt(jnp.mean(x * x, axis=-1, keepdims=True) + EPS)


def _split_bf16(x):
    hi = x.astype(BF16)
    lo = (x - hi.astype(F32)).astype(BF16)
    return hi, lo


def _mod_kernel(c_ref, w_ref, b_ref, o_ref):
    cc = c_ref[...]
    s = cc * jax.nn.sigmoid(cc)
    s_hi, s_lo = _split_bf16(s)
    w_hi, w_lo = _split_bf16(w_ref[...])
    o_ref[...] = _dot(s_hi, w_hi) + _dot(s_lo, w_hi) + _dot(s_hi, w_lo) + b_ref[...]


def _modulation(cc, w_mod, b_mod):
    rows, d = cc.shape
    n = w_mod.shape[1]
    bn = 1024
    return pl.pallas_call(
        _mod_kernel,
        grid=(n // bn,),
        in_specs=[pl.BlockSpec((rows, d), lambda i: (0, 0)),
                  pl.BlockSpec((d, bn), lambda i: (0, i)),
                  pl.BlockSpec((1, bn), lambda i: (0, i))],
        out_specs=pl.BlockSpec((rows, bn), lambda i: (0, i)),
        out_shape=jax.ShapeDtypeStruct((rows, n), F32),
        compiler_params=_params("arbitrary"),
    )(cc, w_mod, b_mod.reshape(1, n))


def _rope(x, cos, sin_a, sin_b):
    return x * cos + pltpu.roll(x, LANES - 1, 1) * sin_a + pltpu.roll(x, 1, 1) * sin_b


def _head_norm(x, seg_ref, gain):
    ss = _dot((x * x).astype(BF16), seg_ref[...])
    return x * lax.rsqrt(ss * (1.0 / HEAD_DIM) + EPS) * gain


def _store_pair_variants(ref, t):
    lane = lax.broadcasted_iota(jnp.int32, t.shape, 1)
    lo = lane < HEAD_DIM
    sw = pltpu.roll(t, HEAD_DIM, 1)
    zero = jnp.zeros_like(t)
    ref[0] = jnp.where(lo, t, zero).astype(BF16)
    ref[1] = jnp.where(lo, zero, sw).astype(BF16)
    ref[2] = jnp.where(lo, sw, zero).astype(BF16)
    ref[3] = jnp.where(lo, zero, t).astype(BF16)


def _proj_kernel(x_ref, sc_ref, sh_ref, gpre_ref, w_ref, cos_ref, sa_ref, sb_ref,
                 qn_ref, kn_ref, seg_q_ref, seg_k_ref,
                 qa_ref, ka_ref, va_ref, qbt_ref, kb_ref, vbt_ref):
    h = _rms(x_ref[...]) * gpre_ref[...] * (1.0 + sc_ref[0]) + sh_ref[0]
    p = _dot(h.astype(BF16), w_ref[...])
    cos, sa, sb = cos_ref[...], sa_ref[...], sb_ref[...]
    nq = qa_ref.shape[1]
    q_scale = HEAD_DIM ** -0.5 * LOG2_E
    for c in range(nq // LANES):
        qa_ref[:, c * LANES:(c + 1) * LANES] = (
            _rope(p[:, c * LANES:(c + 1) * LANES], cos, sa, sb) * q_scale).astype(BF16)
    o = nq
    _store_pair_variants(ka_ref, _rope(p[:, o:o + LANES], cos, sa, sb))
    _store_pair_variants(va_ref, p[:, o + LANES:o + 2 * LANES])
    o += 2 * LANES
    qb = _head_norm(p[:, o:o + nq], seg_q_ref, qn_ref[...])
    for c in range(nq // LANES):
        qbt_ref[c * LANES:(c + 1) * LANES, :] = (
            _rope(qb[:, c * LANES:(c + 1) * LANES], cos, sa, sb) * q_scale).T.astype(BF16)
    o += nq
    kb = _head_norm(p[:, o:o + LANES], seg_k_ref, kn_ref[...])
    kb_ref[...] = _rope(kb, cos, sa, sb).astype(BF16)
    vbt_ref[...] = p[:, o + LANES:o + 2 * LANES].T.astype(BF16)


def _ctx_proj_kernel(x_ref, sc_ref, sh_ref, gpre_ref, w_ref, kn_ref, seg_k_ref,
                     ka_ref, va_ref, kb_ref, vbt_ref):
    h = _rms(x_ref[...]) * gpre_ref[...] * (1.0 + sc_ref[...]) + sh_ref[...]
    p = _dot(h.astype(BF16), w_ref[...])
    _store_pair_variants(ka_ref, p[:, 0:LANES])
    _store_pair_variants(va_ref, p[:, LANES:2 * LANES])
    kb_ref[...] = _head_norm(p[:, 2 * LANES:3 * LANES], seg_k_ref, kn_ref[...]).astype(BF16)
    vbt_ref[...] = p[:, 3 * LANES:4 * LANES].T.astype(BF16)


def _project_latents(x2, sc, sh, gpre, w_in, tables, qn, kn, seg_q, seg_k, seq):
    t, d = x2.shape
    tm = PROJ_TM
    tpb = seq // tm
    nq = seg_q.shape[0]
    const = lambda shape: pl.BlockSpec(shape, lambda i: (0,) * len(shape))
    per_batch = pl.BlockSpec((1, 1, d), lambda i: (i // tpb, 0, 0))
    table = pl.BlockSpec((tm, LANES), lambda i: (i % tpb, 0))
    k_spec = pl.BlockSpec((tm, LANES), lambda i: (i, 0))
    k_shape = jax.ShapeDtypeStruct((t, LANES), BF16)
    vt_spec = pl.BlockSpec((LANES, tm), lambda i: (0, i))
    vt_shape = jax.ShapeDtypeStruct((LANES, t), BF16)
    qt_spec = pl.BlockSpec((nq, tm), lambda i: (0, i))
    qt_shape = jax.ShapeDtypeStruct((nq, t), BF16)
    q_spec = pl.BlockSpec((tm, nq), lambda i: (i, 0))
    q_shape = jax.ShapeDtypeStruct((t, nq), BF16)
    kv4_spec = pl.BlockSpec((4, tm, LANES), lambda i: (0, i, 0))
    kv4_shape = jax.ShapeDtypeStruct((4, t, LANES), BF16)
    return pl.pallas_call(
        _proj_kernel,
        grid=(t // tm,),
        in_specs=[pl.BlockSpec((tm, d), lambda i: (i, 0)), per_batch, per_batch, const((1, d)),
                  const(w_in.shape), table, table, table,
                  const((1, nq)), const((1, LANES)), const(seg_q.shape), const(seg_k.shape)],
        out_specs=[q_spec, kv4_spec, kv4_spec, qt_spec, k_spec, vt_spec],
        out_shape=[q_shape, kv4_shape, kv4_shape, qt_shape, k_shape, vt_shape],
        compiler_params=_params("arbitrary"),
    )(x2, sc, sh, gpre, w_in, *tables, qn, kn, seg_q, seg_k)


def _project_context(c2, sc, sh, gpre, w_kv, kn, seg_k, ctx_len):
    t, d = c2.shape
    const = lambda shape: pl.BlockSpec(shape, lambda i: (0,) * len(shape))
    k_spec = pl.BlockSpec((ctx_len, LANES), lambda i: (i, 0))
    k_shape = jax.ShapeDtypeStruct((t, LANES), BF16)
    vt_spec = pl.BlockSpec((LANES, ctx_len), lambda i: (0, i))
    vt_shape = jax.ShapeDtypeStruct((LANES, t), BF16)
    kv4_spec = pl.BlockSpec((4, ctx_len, LANES), lambda i: (0, i, 0))
    kv4_shape = jax.ShapeDtypeStruct((4, t, LANES), BF16)
    return pl.pallas_call(
        _ctx_proj_kernel,
        grid=(t // ctx_len,),
        in_specs=[pl.BlockSpec((ctx_len, d), lambda i: (i, 0)), const((1, d)), const((1, d)),
                  const((1, d)), const(w_kv.shape), const((1, LANES)), const(seg_k.shape)],
        out_specs=[kv4_spec, kv4_spec, k_spec, vt_spec],
        out_shape=[kv4_shape, kv4_shape, k_shape, vt_shape],
        compiler_params=_params("arbitrary"),
    )(c2, sc, sh, gpre, w_kv, kn, seg_k)


def _attend_t(w, k, kc, vt, vct, shift=None):
    st = _dot(k, w)
    sct = _dot(kc, w)
    if shift is None:
        shift = jnp.maximum(jnp.max(st, axis=0, keepdims=True),
                            jnp.max(sct, axis=0, keepdims=True))
    pt = jnp.exp2(st - shift)
    pct = jnp.exp2(sct - shift)
    denom = jnp.sum(pt, axis=0, keepdims=True) + jnp.sum(pct, axis=0, keepdims=True)
    o2 = _dot(vt, pt.astype(BF16)) + _dot(vct, pct.astype(BF16))
    return o2, denom


def _all_heads_t(qt_ref, o_ref, attend):
    tq = qt_ref.shape[1]
    n_kv = LANES // HEAD_DIM
    zeros = jnp.zeros((HEAD_DIM, tq), BF16)
    outs = []
    for h in range(qt_ref.shape[0] // HEAD_DIM):
        g = h // KV_GROUP
        qh = qt_ref[h * HEAD_DIM:(h + 1) * HEAD_DIM, :]
        w = jnp.concatenate([zeros] * g + [qh] + [zeros] * (n_kv - 1 - g), axis=0)
        o2, denom = attend(h, w)
        outs.append(o2[g * HEAD_DIM:(g + 1) * HEAD_DIM, :] / denom)
    o_ref[...] = jnp.concatenate(outs, axis=0).T.astype(BF16)


def _all_heads_fused_t(qt_ref, o_ref, attend):
    tq = qt_ref.shape[1]
    n_heads = qt_ref.shape[0] // HEAD_DIM
    n_kv = LANES // HEAD_DIM
    rows = []
    for g in range(n_kv):
        heads = [qt_ref[h * HEAD_DIM:(h + 1) * HEAD_DIM, :] if h // KV_GROUP == g
                 else jnp.zeros((HEAD_DIM, tq), BF16) for h in range(n_heads)]
        rows.append(jnp.concatenate(heads, axis=1))
    w = jnp.concatenate(rows, axis=0)
    o2, denom = attend(0, w)
    o2 = o2 / denom
    outs = [o2[(h // KV_GROUP) * HEAD_DIM:(h // KV_GROUP + 1) * HEAD_DIM, h * tq:(h + 1) * tq]
            for h in range(n_heads)]
    o_ref[...] = jnp.concatenate(outs, axis=0).T.astype(BF16)


def _attn_b_kernel(bound_ref, qt_ref, k_ref, vt_ref, kc_ref, vct_ref, o_ref):
    k, kc, vt, vct = k_ref[...], kc_ref[...], vt_ref[...], vct_ref[...]
    bound = bound_ref[0]

    @pl.when(bound <= SAFE_SOFTMAX_SHIFT)
    def _():
        _all_heads_fused_t(qt_ref, o_ref, lambda h, w: _attend_t(w, k, kc, vt, vct, shift=bound))

    @pl.when(jnp.logical_not(bound <= SAFE_SOFTMAX_SHIFT))
    def _():
        _all_heads_t(qt_ref, o_ref, lambda h, w: _attend_t(w, k, kc, vt, vct))


def _attn_specs(nq, tq, seq, ctx_len):
    nqb = seq // tq
    return dict(
        qt=pl.BlockSpec((nq, tq), lambda b, i: (0, b * nqb + i)),
        k=pl.BlockSpec((seq, LANES), lambda b, i: (b, 0)),
        vt=pl.BlockSpec((LANES, seq), lambda b, i: (0, b)),
        kc=pl.BlockSpec((ctx_len, LANES), lambda b, i: (b, 0)),
        vct=pl.BlockSpec((LANES, ctx_len), lambda b, i: (0, b)),
        out=pl.BlockSpec((tq, nq), lambda b, i: (b * nqb + i, 0)))


def _attention_b(score_bound, qbt, kb, vbt, kcb, vcbt, batch, seq, ctx_len):
    nq, t = qbt.shape
    tq = ATTN_B_TQ
    sp = _attn_specs(nq, tq, seq, ctx_len)
    return pl.pallas_call(
        _attn_b_kernel,
        grid=(batch, seq // tq),
        in_specs=[pl.BlockSpec(memory_space=pltpu.SMEM),
                  sp["qt"], sp["k"], sp["vt"], sp["kc"], sp["vct"]],
        out_specs=sp["out"],
        out_shape=jax.ShapeDtypeStruct((t, nq), BF16),
        compiler_params=_params("arbitrary", "arbitrary"),
    )(score_bound, qbt, kb, vbt, kcb, vcbt)


def _attn_a_kernel(sink_ref, q_ref, k_ref, v_ref, kc_ref, vc_ref, o_ref, bias_ref, *, seq):
    i = pl.program_id(1)
    tq = q_ref.shape[0]
    win = tq + 2 * WINDOW
    start = pl.multiple_of(jnp.clip(i * tq - WINDOW, 0, seq - win), WINDOW)
    qpos = i * tq + lax.broadcasted_iota(jnp.int32, (tq, win), 0)
    kpos = start + lax.broadcasted_iota(jnp.int32, (tq, win), 1)
    bias_ref[...] = jnp.where(jnp.abs(kpos - qpos) <= WINDOW, 0.0, NEG_INF).astype(F32)
    n_pairs = q_ref.shape[1] // LANES
    pairs_per_kv = KV_GROUP // 2
    rowmax = lambda s: jnp.max(s, axis=-1, keepdims=True)
    rowsum = lambda s: jnp.sum(s, axis=-1, keepdims=True)
    for pair in range(n_pairs):
        g = pair // pairs_per_kv
        q2 = q_ref[:, pair * LANES:(pair + 1) * LANES]
        acc = None
        for half in range(2):
            kv = 2 * g + half
            sink = sink_ref[2 * pair + half] * LOG2_E
            s = _dot_nt(q2, k_ref[kv, pl.ds(start, win), :]) + bias_ref[...]
            sc = _dot_nt(q2, kc_ref[kv])
            m = jnp.maximum(jnp.maximum(rowmax(s), rowmax(sc)), sink)
            p = jnp.exp2(s - m)
            pc = jnp.exp2(sc - m)
            denom = rowsum(p) + rowsum(pc) + jnp.exp2(sink - m)
            o = (_dot(p.astype(BF16), v_ref[kv, pl.ds(start, win), :])
                 + _dot(pc.astype(BF16), vc_ref[kv])) / denom
            acc = o if acc is None else acc + o
        o_ref[:, pair * LANES:(pair + 1) * LANES] = acc.astype(BF16)


def _attention_a(sink, qa, ka4, va4, kca4, vca4, batch, seq, ctx_len):
    t, nq = qa.shape
    tq = ATTN_A_TQ
    nqb = seq // tq
    lat = pl.BlockSpec((4, seq, LANES), lambda b, i: (0, b, 0))
    cx = pl.BlockSpec((4, ctx_len, LANES), lambda b, i: (0, b, 0))
    qs = pl.BlockSpec((tq, nq), lambda b, i: (b * nqb + i, 0))
    return pl.pallas_call(
        functools.partial(_attn_a_kernel, seq=seq),
        grid=(batch, nqb),
        in_specs=[pl.BlockSpec(memory_space=pltpu.SMEM), qs, lat, lat, cx, cx],
        out_specs=qs,
        out_shape=jax.ShapeDtypeStruct((t, nq), BF16),
        scratch_shapes=[pltpu.VMEM((tq, tq + 2 * WINDOW), F32)],
        compiler_params=_params("arbitrary", "arbitrary"),
    )(sink, qa, ka4, va4, kca4, vca4)


def _out_kernel(oa_ref, ob_ref, x_ref, g1_ref, sc2_ref, sh2_ref, ga_ref, gb_ref, gpost_ref,
                gpre2_ref, woa_ref, wob_ref, wrh_ref, wrl_ref, br_ref,
                x1_ref, h2_ref, rinfo_ref, rt_ref, tcarry_ref, tcnt_ref, cnt_ref, carry_ref):
    step = pl.program_id(0)

    @pl.when(step == 0)
    def _():
        carry_ref[...] = jnp.zeros_like(carry_ref)

    na = _rms(oa_ref[...].astype(F32)) * ga_ref[...]
    nb = _rms(ob_ref[...].astype(F32)) * gb_ref[...]
    ox = _dot(na.astype(BF16), woa_ref[...]) + _dot(nb.astype(BF16), wob_ref[...])
    x1 = x_ref[...] + g1_ref[0] * (_rms(ox) * gpost_ref[...])
    x1_ref[...] = x1
    h2 = _rms(x1) * gpre2_ref[...] * (1.0 + sc2_ref[0]) + sh2_ref[0]
    h2_ref[...] = h2

    h_hi, h_lo = _split_bf16(h2)
    logits = (_dot(h_hi, wrh_ref[...]) + _dot(h_lo, wrh_ref[...]) + _dot(h_hi, wrl_ref[...])
              + br_ref[...])
    tm = logits.shape[0]
    lane = lax.broadcasted_iota(jnp.int32, logits.shape, 1)
    lanef = lane.astype(F32)
    big = jnp.float32(1e9)
    ninf = jnp.float32(-jnp.inf)
    rowmax = lambda v: jnp.max(v, axis=-1, keepdims=True)
    rowmin = lambda v: jnp.min(v, axis=-1, keepdims=True)
    rowsum = lambda v: jnp.sum(v, axis=-1, keepdims=True)

    gmask = (lane >= N_EXPERTS) & (lane < N_EXPERTS + N_GROUPS)
    lg = jnp.where(gmask, logits, ninf)
    gmax = rowmax(lg)
    gidx = rowmin(jnp.where(lg == gmax, lanef, big)) - N_EXPERTS
    g_w = 1.0 / rowsum(jnp.exp(lg - gmax))
    lane_group = (lane // EXPERTS_PER_GROUP).astype(F32)
    emask = (lane < N_EXPERTS) & (lane_group == gidx)
    le = jnp.where(emask, logits, ninf)
    m1 = rowmax(le)
    i1 = rowmin(jnp.where(le == m1, lanef, big))
    le2 = jnp.where(lanef == i1, ninf, le)
    m2 = rowmax(le2)
    i2 = rowmin(jnp.where(le2 == m2, lanef, big))
    e2 = jnp.exp(m2 - m1)
    w0 = g_w / (1.0 + e2)
    w1 = g_w * e2 / (1.0 + e2)

    hit1 = lanef == i1
    hit2 = lanef == i2
    onehot = jnp.where(hit1 | hit2, 1.0, 0.0).astype(F32)
    r = lax.broadcasted_iota(jnp.int32, (tm, tm), 0)
    c = lax.broadcasted_iota(jnp.int32, (tm, tm), 1)
    strict_lower = jnp.where(r > c, 1.0, 0.0).astype(BF16)
    within = _dot(strict_lower, onehot.astype(BF16))
    tile_cnt = jnp.sum(onehot, axis=0, keepdims=True)
    incl = jnp.broadcast_to(tile_cnt, (8, LANES))
    lane8 = lax.broadcasted_iota(jnp.int32, (8, LANES), 1)
    shift = 1
    while shift < LANES:
        incl = incl + jnp.where(lane8 >= shift, pltpu.roll(incl, shift, 1), 0.0)
        shift *= 2
    local = within + (incl[0:1] - tile_cnt)
    pos0 = rowsum(jnp.where(hit1, local, 0.0))
    pos1 = rowsum(jnp.where(hit2, local, 0.0))
    tcarry_ref[0] = carry_ref[...]
    tcnt_ref[0] = tile_cnt
    carry_ref[...] += tile_cnt
    cnt_ref[...] = carry_ref[...]

    info = jnp.zeros_like(logits)
    for k, val in enumerate((i1, i2, pos0, pos1, w0, w1)):
        info = jnp.where(lane == k, val, info)
    rinfo_ref[...] = info
    rt_ref[...] = info.T[0:8, :]


def _out_and_route(oa, ob, x2, g1, sc2, sh2, ga, gb, gpost, gpre2, woa, wob, wrh, wrl, br, seq):
    t, d = x2.shape
    tm = OUT_TM
    tpb = seq // tm
    nq = oa.shape[1]
    const = lambda shape: pl.BlockSpec(shape, lambda i: (0,) * len(shape))
    per_batch = pl.BlockSpec((1, 1, d), lambda i: (i // tpb, 0, 0))
    rows = lambda n: pl.BlockSpec((tm, n), lambda i: (i, 0))
    per_tile = pl.BlockSpec((1, 1, LANES), lambda i: (i, 0, 0))
    return pl.pallas_call(
        _out_kernel,
        grid=(t // tm,),
        in_specs=[rows(nq), rows(nq), rows(d), per_batch, per_batch, per_batch,
                  const((1, nq)), const((1, nq)), const((1, d)), const((1, d)),
                  const(woa.shape), const(wob.shape), const(wrh.shape), const(wrl.shape),
                  const((1, LANES))],
        out_specs=[rows(d), rows(d), rows(LANES), pl.BlockSpec((8, tm), lambda i: (0, i)),
                   per_tile, per_tile, const((1, LANES))],
        out_shape=[jax.ShapeDtypeStruct((t, d), F32), jax.ShapeDtypeStruct((t, d), F32),
                   jax.ShapeDtypeStruct((t, LANES), F32), jax.ShapeDtypeStruct((8, t), F32),
                   jax.ShapeDtypeStruct((t // tm, 1, LANES), F32),
                   jax.ShapeDtypeStruct((t // tm, 1, LANES), F32),
                   jax.ShapeDtypeStruct((1, LANES), F32)],
        scratch_shapes=[pltpu.VMEM((1, LANES), F32)],
        compiler_params=_params("arbitrary"),
    )(oa, ob, x2, g1, sc2, sh2, ga, gb, gpost, gpre2, woa, wob, wrh, wrl, br)


ROW_SUBLANES = 8


def _for_each_run_piece(rdst_ref, rlen_ref, tile, max_len, fn):
    n_bits = max_len.bit_length()

    def run(e, local):
        length = rlen_ref[tile * N_EXPERTS + e]
        dst = rdst_ref[tile * N_EXPERTS + e]
        for b in range(n_bits):
            size = 1 << b

            @pl.when(((length >> b) & 1) == 1)
            def _():
                done = length & (size - 1)
                fn(local + done, dst + done, size)
        return local + length

    lax.fori_loop(0, N_EXPERTS, run, 0)


def _token_rows(ref, row0, n_rows):
    start = row0 * ROW_SUBLANES
    if not isinstance(start, int):
        start = pl.multiple_of(start, ROW_SUBLANES)
    return ref.at[pl.ds(start, n_rows * ROW_SUBLANES)]


def _dispatch_kernel(rdst_ref, rlen_ref, ends_ref, nv_ref, h_ref, rt_ref, xs_ref,
                     sorted_ref, zero_ref, sem, zsem, *, te, n_tiles):
    k = pl.program_id(0)
    nk = pl.num_programs(0)
    tm = h_ref.shape[0]
    rows = 2 * tm
    slot = k % 2

    def wait_slot(s):
        pltpu.make_async_copy(sorted_ref.at[s], _token_rows(xs_ref, 0, rows), sem.at[s]).wait()

    @pl.when(k == 0)
    def _():
        zero_ref[...] = jnp.zeros_like(zero_ref)

        def pad_copy(row0):
            return pltpu.make_async_copy(zero_ref, _token_rows(xs_ref, row0, te), zsem)

        def for_each_pad_tile(fn):
            def expert_pad(e, carry):
                end = ends_ref[e]
                prev = jnp.where(e > 0, ends_ref[jnp.maximum(e - 1, 0)], 0)

                @pl.when(end > prev)
                def _():
                    fn(pad_copy(end - te))
                return carry

            def tail_pad(j, carry):
                fn(pad_copy(j * te))
                return carry

            lax.fori_loop(0, N_EXPERTS, expert_pad, 0)
            lax.fori_loop(nv_ref[0], n_tiles, tail_pad, 0)

        for_each_pad_tile(lambda cp: cp.start())
        for_each_pad_tile(lambda cp: cp.wait())

    @pl.when(k >= 2)
    def _():
        wait_slot(slot)

    pos0 = rt_ref[2:3, :]
    pos1 = rt_ref[3:4, :]
    r = lax.broadcasted_iota(jnp.int32, (rows, tm), 0).astype(F32)
    perm = jnp.where((r == pos0) | (r == pos1), 1.0, 0.0).astype(BF16)
    srt = _dot(perm, h_ref[...].astype(BF16))
    buf = sorted_ref.at[slot]
    for c in range(ROW_SUBLANES):
        buf[pl.ds(c, rows, stride=ROW_SUBLANES), :] = srt[:, c * LANES:(c + 1) * LANES]

    def copy_piece(local, dst, size):
        pltpu.make_async_copy(_token_rows(buf, local, size), _token_rows(xs_ref, dst, size),
                              sem.at[slot]).start()

    _for_each_run_piece(rdst_ref, rlen_ref, k, tm, copy_piece)

    @pl.when(k == nk - 1)
    def _():
        wait_slot(slot)

        @pl.when(nk >= 2)
        def _():
            wait_slot(1 - slot)


def _dispatch(run_dst, run_len, ends, n_valid, h2, rt, n_rows):
    t, d = h2.shape
    assert d == ROW_SUBLANES * LANES
    tm = MOVE_TM
    te = EXPERT_TE
    return pl.pallas_call(
        functools.partial(_dispatch_kernel, te=te, n_tiles=n_rows // te),
        grid_spec=pltpu.PrefetchScalarGridSpec(
            num_scalar_prefetch=4,
            grid=(t // tm,),
            in_specs=[pl.BlockSpec((tm, d), lambda i, *_: (i, 0)),
                      pl.BlockSpec((8, tm), lambda i, *_: (0, i))],
            out_specs=pl.BlockSpec(memory_space=pl.ANY),
            scratch_shapes=[pltpu.VMEM((2, 2 * tm * ROW_SUBLANES, LANES), F32),
                            pltpu.VMEM((te * ROW_SUBLANES, LANES), F32),
                            pltpu.SemaphoreType.DMA((2,)), pltpu.SemaphoreType.DMA(())]),
        out_shape=jax.ShapeDtypeStruct((n_rows * ROW_SUBLANES, LANES), F32),
        compiler_params=_params("arbitrary"),
    )(run_dst, run_len, ends, n_valid, h2, rt)


def _expert_kernel(te_ref, nv_ref, xs_ref, wg_ref, wu_ref, wd_ref, ys_ref, wg_bf, wu_bf, wd_bf):
    j = pl.program_id(0)
    valid = j < nv_ref[0]
    changed = (j == 0) | (te_ref[j] != te_ref[jnp.maximum(j - 1, 0)])

    @pl.when(valid & changed)
    def _():
        wg_bf[...] = wg_ref[0].astype(BF16)
        wu_bf[...] = wu_ref[0].astype(BF16)
        wd_bf[...] = wd_ref[0].astype(BF16)

    @pl.when(valid)
    def _():
        te = xs_ref.shape[0] // ROW_SUBLANES
        xb = jnp.concatenate(
            [xs_ref[pl.ds(c, te, stride=ROW_SUBLANES), :].astype(BF16) for c in range(ROW_SUBLANES)],
            axis=1)
        gate = _dot(xb, wg_bf[...])
        up = _dot(xb, wu_bf[...])
        act = gate * jax.nn.sigmoid(gate) * up
        y = _dot(act.astype(BF16), wd_bf[...])
        for c in range(ROW_SUBLANES):
            ys_ref[pl.ds(c, te, stride=ROW_SUBLANES), :] = y[:, c * LANES:(c + 1) * LANES]

    @pl.when(jnp.logical_not(valid))
    def _():
        ys_ref[...] = jnp.zeros_like(ys_ref)


def _expert_mlp(tile_expert, n_valid, xs, w_gate, w_up, w_down):
    te = EXPERT_TE
    d, ff = w_gate.shape[1:]
    n_rows = xs.shape[0] // ROW_SUBLANES
    blk = (te * ROW_SUBLANES, LANES)
    tile = lambda j, e, nv: (jnp.minimum(j, nv[0] - 1), 0)
    wsel = lambda j, e, nv: (e[j], 0, 0)
    return pl.pallas_call(
        _expert_kernel,
        grid_spec=pltpu.PrefetchScalarGridSpec(
            num_scalar_prefetch=2,
            grid=(n_rows // te,),
            in_specs=[pl.BlockSpec(blk, tile),
                      pl.BlockSpec((1, d, ff), wsel), pl.BlockSpec((1, d, ff), wsel),
                      pl.BlockSpec((1, ff, d), wsel)],
            out_specs=pl.BlockSpec(blk, lambda j, e, nv: (j, 0)),
            scratch_shapes=[pltpu.VMEM((d, ff), BF16), pltpu.VMEM((d, ff), BF16),
                            pltpu.VMEM((ff, d), BF16)]),
        out_shape=jax.ShapeDtypeStruct(xs.shape, F32),
        compiler_params=_params("arbitrary"),
    )(tile_expert, n_valid, xs, w_gate, w_up, w_down)


def _combine_kernel(rdst_ref, rlen_ref, x1_ref, rinfo_ref, g2_ref, gpost_ref, ys_ref, o_ref,
                    gath_ref, sem):
    k = pl.program_id(0)
    nk = pl.num_programs(0)
    tm = x1_ref.shape[0]
    rows = 2 * tm
    slot = k % 2

    def gather_runs(tile, s):
        buf = gath_ref.at[s]

        def copy_piece(local, src, size):
            pltpu.make_async_copy(_token_rows(ys_ref, src, size), _token_rows(buf, local, size),
                                  sem.at[s]).start()

        _for_each_run_piece(rdst_ref, rlen_ref, tile, tm, copy_piece)

    @pl.when(k == 0)
    def _():
        gather_runs(0, 0)

    @pl.when(k + 1 < nk)
    def _():
        gather_runs(k + 1, 1 - slot)

    buf = gath_ref.at[slot]
    pltpu.make_async_copy(_token_rows(ys_ref, 0, rows), buf, sem.at[slot]).wait()
    g = jnp.concatenate(
        [buf[pl.ds(c, rows, stride=ROW_SUBLANES), :].astype(BF16) for c in range(ROW_SUBLANES)],
        axis=1)
    info = rinfo_ref[...]
    col = lax.broadcasted_iota(jnp.int32, (tm, rows), 1).astype(F32)
    pick0 = jnp.where(col == info[:, 2:3], 1.0, 0.0).astype(BF16)
    pick1 = jnp.where(col == info[:, 3:4], 1.0, 0.0).astype(BF16)
    fx = info[:, 4:5] * _dot(pick0, g) + info[:, 5:6] * _dot(pick1, g)
    o_ref[...] = x1_ref[...] + g2_ref[0] * (_rms(fx) * gpost_ref[...])


def _combine(run_dst, run_len, x1, rinfo, g2, gpost, ys, seq):
    t, d = x1.shape
    tm = MOVE_TM
    tpb = seq // tm
    return pl.pallas_call(
        _combine_kernel,
        grid_spec=pltpu.PrefetchScalarGridSpec(
            num_scalar_prefetch=2,
            grid=(t // tm,),
            in_specs=[pl.BlockSpec((tm, d), lambda i, *_: (i, 0)),
                      pl.BlockSpec((tm, LANES), lambda i, *_: (i, 0)),
                      pl.BlockSpec((1, 1, d), lambda i, *_: (i // tpb, 0, 0)),
                      pl.BlockSpec((1, d), lambda i, *_: (0, 0)),
                      pl.BlockSpec(memory_space=pl.ANY)],
            out_specs=pl.BlockSpec((tm, d), lambda i, *_: (i, 0)),
            scratch_shapes=[pltpu.VMEM((2, 2 * tm * ROW_SUBLANES, LANES), F32),
                            pltpu.SemaphoreType.DMA((2,))]),
        out_shape=jax.ShapeDtypeStruct((t, d), F32),
        compiler_params=_params("arbitrary"),
    )(run_dst, run_len, x1, rinfo, g2, gpost, ys)


def _rope_tables(seq):
    pos = jnp.arange(seq, dtype=jnp.int32)
    row = (pos // GRID_W).astype(F32)
    col = (pos % GRID_W).astype(F32)
    axis_dim = HEAD_DIM // 2
    inv_freq = ROPE_THETA ** (-jnp.arange(0, axis_dim, 2, dtype=F32) / axis_dim)
    ang = jnp.concatenate([row[:, None] * inv_freq, col[:, None] * inv_freq], axis=-1)
    pair = (jnp.arange(LANES) % HEAD_DIM) // 2
    cos = jnp.cos(ang)[:, pair]
    sin = jnp.sin(ang)[:, pair]
    even = (jnp.arange(LANES) % 2) == 0
    return cos, jnp.where(even, -sin, 0.0), jnp.where(even, 0.0, sin)


def _segment_ones(n):
    seg = jnp.arange(n) // HEAD_DIM
    return (seg[:, None] == seg[None, :]).astype(BF16)


def kernel(x, c, ctx, c_ctx, w_mod, b_mod, attn_pre_norm, attn_post_norm, w_in, a_sink,
           b_q_norm, b_k_norm, a_out_norm, b_out_norm, w_out, ffn_pre_norm, ffn_post_norm,
           w_group, b_group, w_router, b_router, w_gate, w_up, w_down):
    batch, seq, d = x.shape
    ctx_len = ctx.shape[1]
    assert w_mod.shape[0] == 1, "single-layer stack only (context stream is never updated)"
    assert seq % ATTN_A_TQ == 0 and seq >= ATTN_A_TQ + 2 * WINDOW
    assert seq % PROJ_TM == 0 and seq % ATTN_B_TQ == 0 and seq % OUT_TM == 0 and seq % MOVE_TM == 0
    t = batch * seq
    nq = d // 2
    nkv = nq // KV_GROUP
    assert nkv == LANES and w_in.shape[2] == 2 * nq + 4 * nkv

    cc = jnp.concatenate([c, c_ctx[None, :], jnp.zeros((16 - batch - 1, d), F32)], axis=0)
    mod = _modulation(cc, w_mod[0], b_mod[0])
    sh1, sc1, g1, sh2, sc2, g2 = (m.reshape(batch, 1, d) for m in jnp.split(mod[:batch], 6, axis=-1))
    csh1, csc1 = (m.reshape(1, d) for m in jnp.split(mod[batch], 6)[:2])

    x2 = x.reshape(t, d)
    c2 = ctx.reshape(batch * ctx_len, d)
    gpre = attn_pre_norm[0].reshape(1, d)
    w_in_bf = w_in[0].astype(BF16)
    kv_cols = jnp.concatenate([w_in_bf[:, nq:nq + 2 * nkv], w_in_bf[:, 2 * nq + 2 * nkv:]], axis=1)
    qn = jnp.tile(b_q_norm[0], nq // HEAD_DIM).reshape(1, nq)
    kn = jnp.tile(b_k_norm[0], nkv // HEAD_DIM).reshape(1, nkv)
    seg_q, seg_k = _segment_ones(nq), _segment_ones(nkv)
    qa, ka4, va4, qbt, kb, vbt = _project_latents(
        x2, sc1, sh1, gpre, w_in_bf, _rope_tables(seq), qn, kn, seg_q, seg_k, seq)
    kca4, vca4, kcb, vcbt = _project_context(c2, csc1, csh1, gpre, kv_cols, kn, seg_k, ctx_len)

    oa = _attention_a(a_sink[0], qa, ka4, va4, kca4, vca4, batch, seq, ctx_len)
    score_bound = (1.01 * HEAD_DIM ** 0.5 * LOG2_E
                   * jnp.max(jnp.abs(b_q_norm[0])) * jnp.max(jnp.abs(b_k_norm[0]))).reshape(1)
    ob = _attention_b(score_bound, qbt, kb, vbt, kcb, vcbt, batch, seq, ctx_len)

    w_out_bf = w_out[0].astype(BF16)
    w_r = jnp.zeros((d, LANES), F32)
    w_r = w_r.at[:, :N_EXPERTS].set(w_router[0]).at[:, N_EXPERTS:N_EXPERTS + N_GROUPS].set(w_group[0])
    w_r_hi = w_r.astype(BF16)
    w_r_lo = (w_r - w_r_hi.astype(F32)).astype(BF16)
    b_r = jnp.zeros((1, LANES), F32)
    b_r = b_r.at[0, :N_EXPERTS].set(b_router[0]).at[0, N_EXPERTS:N_EXPERTS + N_GROUPS].set(b_group[0])
    x1, h2, rinfo, rt, tcarry, tcnt, counts = _out_and_route(
        oa, ob, x2, g1, sc2, sh2, a_out_norm[0].reshape(1, nq), b_out_norm[0].reshape(1, nq),
        attn_post_norm[0].reshape(1, d), ffn_pre_norm[0].reshape(1, d),
        w_out_bf[:nq], w_out_bf[nq:], w_r_hi, w_r_lo, b_r, seq)

    te = EXPERT_TE
    n_tiles = -(-(2 * t + N_EXPERTS * (te - 1)) // te)
    n_rows = n_tiles * te
    cnt = counts[0, :N_EXPERTS].astype(jnp.int32)
    padded = ((cnt + te - 1) // te) * te
    ends = jnp.cumsum(padded)
    offs = ends - padded
    run_dst = (offs[None, :] + tcarry[:, 0, :N_EXPERTS].astype(jnp.int32)).reshape(-1)
    run_len = tcnt[:, 0, :N_EXPERTS].astype(jnp.int32).reshape(-1)
    n_valid = (ends[-1] // te).astype(jnp.int32).reshape(1)
    tile_start = jnp.arange(n_tiles, dtype=jnp.int32) * te
    tile_expert = jnp.sum(ends[None, :] <= tile_start[:, None], axis=1).astype(jnp.int32)
    last_expert = tile_expert[jnp.maximum(n_valid[0] - 1, 0)]
    tile_expert = jnp.where(tile_start < ends[-1], tile_expert, last_expert)

    xs = _dispatch(run_dst, run_len, ends.astype(jnp.int32), n_valid, h2, rt, n_rows)
    ys = _expert_mlp(tile_expert, n_valid, xs, w_gate[0], w_up[0], w_down[0])
    out = _combine(run_dst, run_len, x1, rinfo, g2, ffn_post_norm[0].reshape(1, d), ys, seq)
    return out.reshape(batch, seq, d)
```

```python
import functools

import jax
import jax.numpy as jnp
from jax import lax
from jax.experimental import pallas as pl
from jax.experimental.pallas import tpu as pltpu

F32 = jnp.float32
BF16 = jnp.bfloat16

GRID_W = 64
HEAD_DIM = 64
KV_GROUP = 4
WINDOW = 128
ROPE_THETA = 10000.0
N_GROUPS = 4
EXPERTS_PER_GROUP = 8
N_EXPERTS = N_GROUPS * EXPERTS_PER_GROUP
EPS = 1e-6
NEG_INF = -1e30
LOG2_E = 1.4426950408889634
SAFE_SOFTMAX_SHIFT = 40.0

LANES = 128
V7X_VMEM_LIMIT = 56 * 1024 * 1024

PROJ_TM = 512
ATTN_A_TQ = 512
ATTN_B_TQ = 256
OUT_TM = 512
EXPERT_TE = 512
MOVE_TM = OUT_TM


def _params(*sem):
    return pltpu.CompilerParams(dimension_semantics=sem, vmem_limit_bytes=V7X_VMEM_LIMIT)


def _dot(a, b):
    return jnp.dot(a, b, preferred_element_type=F32)


def _dot_nt(a, b):
    return lax.dot_general(a, b, (((1,), (1,)), ((), ())), preferred_element_type=F32)


def _rms(x):
    return x * lax.rsqrt(jnp.mean(x * x, axis=-1, keepdims=True) + EPS)


def _split_bf16(x):
    hi = x.astype(BF16)
    lo = (x - hi.astype(F32)).astype(BF16)
    return hi, lo


def _mod_kernel(c_ref, w_ref, b_ref, o_ref):
    cc = c_ref[...]
    s = cc * jax.nn.sigmoid(cc)
    s_hi, s_lo = _split_bf16(s)
    w_hi, w_lo = _split_bf16(w_ref[...])
    o_ref[...] = _dot(s_hi, w_hi) + _dot(s_lo, w_hi) + _dot(s_hi, w_lo) + b_ref[...]


def _modulation(cc, w_mod, b_mod):
    rows, d = cc.shape
    n = w_mod.shape[1]
    bn = 1024
    return pl.pallas_call(
        _mod_kernel,
        grid=(n // bn,),
        in_specs=[pl.BlockSpec((rows, d), lambda i: (0, 0)),
                  pl.BlockSpec((d, bn), lambda i: (0, i)),
                  pl.BlockSpec((1, bn), lambda i: (0, i))],
        out_specs=pl.BlockSpec((rows, bn), lambda i: (0, i)),
        out_shape=jax.ShapeDtypeStruct((rows, n), F32),
        compiler_params=_params("arbitrary"),
    )(cc, w_mod, b_mod.reshape(1, n))


def _rope(x, cos, sin_a, sin_b):
    return x * cos + pltpu.roll(x, LANES - 1, 1) * sin_a + pltpu.roll(x, 1, 1) * sin_b


def _head_norm(x, seg_ref, gain):
    ss = _dot((x * x).astype(BF16), seg_ref[...])
    return x * lax.rsqrt(ss * (1.0 / HEAD_DIM) + EPS) * gain


def _store_pair_variants(ref, t):
    lane = lax.broadcasted_iota(jnp.int32, t.shape, 1)
    lo = lane < HEAD_DIM
    sw = pltpu.roll(t, HEAD_DIM, 1)
    zero = jnp.zeros_like(t)
    ref[0] = jnp.where(lo, t, zero).astype(BF16)
    ref[1] = jnp.where(lo, zero, sw).astype(BF16)
    ref[2] = jnp.where(lo, sw, zero).astype(BF16)
    ref[3] = jnp.where(lo, zero, t).astype(BF16)


def _proj_kernel(x_ref, sc_ref, sh_ref, gpre_ref, w_ref, cos_ref, sa_ref, sb_ref,
                 qn_ref, kn_ref, seg_q_ref, seg_k_ref,
                 qa_ref, ka_ref, va_ref, qbt_ref, kb_ref, vbt_ref):
    h = _rms(x_ref[...]) * gpre_ref[...] * (1.0 + sc_ref[0]) + sh_ref[0]
    p = _dot(h.astype(BF16), w_ref[...])
    cos, sa, sb = cos_ref[...], sa_ref[...], sb_ref[...]
    nq = qa_ref.shape[1]
    q_scale = HEAD_DIM ** -0.5 * LOG2_E
    for c in range(nq // LANES):
        qa_ref[:, c * LANES:(c + 1) * LANES] = (
            _rope(p[:, c * LANES:(c + 1) * LANES], cos, sa, sb) * q_scale).astype(BF16)
    o = nq
    _store_pair_variants(ka_ref, _rope(p[:, o:o + LANES], cos, sa, sb))
    _store_pair_variants(va_ref, p[:, o + LANES:o + 2 * LANES])
    o += 2 * LANES
    qb = _head_norm(p[:, o:o + nq], seg_q_ref, qn_ref[...])
    for c in range(nq // LANES):
        qbt_ref[c * LANES:(c + 1) * LANES, :] = (
            _rope(qb[:, c * LANES:(c + 1) * LANES], cos, sa, sb) * q_scale).T.astype(BF16)
    o += nq
    kb = _head_norm(p[:, o:o + LANES], seg_k_ref, kn_ref[...])
    kb_ref[...] = _rope(kb, cos, sa, sb).astype(BF16)
    vbt_ref[...] = p[:, o + LANES:o + 2 * LANES].T.astype(BF16)


def _ctx_proj_kernel(x_ref, sc_ref, sh_ref, gpre_ref, w_ref, kn_ref, seg_k_ref,
                     ka_ref, va_ref, kb_ref, vbt_ref):
    h = _rms(x_ref[...]) * gpre_ref[...] * (1.0 + sc_ref[...]) + sh_ref[...]
    p = _dot(h.astype(BF16), w_ref[...])
    _store_pair_variants(ka_ref, p[:, 0:LANES])
    _store_pair_variants(va_ref, p[:, LANES:2 * LANES])
    kb_ref[...] = _head_norm(p[:, 2 * LANES:3 * LANES], seg_k_ref, kn_ref[...]).astype(BF16)
    vbt_ref[...] = p[:, 3 * LANES:4 * LANES].T.astype(BF16)


def _project_latents(x2, sc, sh, gpre, w_in, tables, qn, kn, seg_q, seg_k, seq):
    t, d = x2.shape
    tm = PROJ_TM
    tpb = seq // tm
    nq = seg_q.shape[0]
    const = lambda shape: pl.BlockSpec(shape, lambda i: (0,) * len(shape))
    per_batch = pl.BlockSpec((1, 1, d), lambda i: (i // tpb, 0, 0))
    table = pl.BlockSpec((tm, LANES), lambda i: (i % tpb, 0))
    k_spec = pl.BlockSpec((tm, LANES), lambda i: (i, 0))
    k_shape = jax.ShapeDtypeStruct((t, LANES), BF16)
    vt_spec = pl.BlockSpec((LANES, tm), lambda i: (0, i))
    vt_shape = jax.ShapeDtypeStruct((LANES, t), BF16)
    qt_spec = pl.BlockSpec((nq, tm), lambda i: (0, i))
    qt_shape = jax.ShapeDtypeStruct((nq, t), BF16)
    q_spec = pl.BlockSpec((tm, nq), lambda i: (i, 0))
    q_shape = jax.ShapeDtypeStruct((t, nq), BF16)
    kv4_spec = pl.BlockSpec((4, tm, LANES), lambda i: (0, i, 0))
    kv4_shape = jax.ShapeDtypeStruct((4, t, LANES), BF16)
    return pl.pallas_call(
        _proj_kernel,
        grid=(t // tm,),
        in_specs=[pl.BlockSpec((tm, d), lambda i: (i, 0)), per_batch, per_batch, const((1, d)),
                  const(w_in.shape), table, table, table,
                  const((1, nq)), const((1, LANES)), const(seg_q.shape), const(seg_k.shape)],
        out_specs=[q_spec, kv4_spec, kv4_spec, qt_spec, k_spec, vt_spec],
        out_shape=[q_shape, kv4_shape, kv4_shape, qt_shape, k_shape, vt_shape],
        compiler_params=_params("arbitrary"),
    )(x2, sc, sh, gpre, w_in, *tables, qn, kn, seg_q, seg_k)


def _project_context(c2, sc, sh, gpre, w_kv, kn, seg_k, ctx_len):
    t, d = c2.shape
    const = lambda shape: pl.BlockSpec(shape, lambda i: (0,) * len(shape))
    k_spec = pl.BlockSpec((ctx_len, LANES), lambda i: (i, 0))
    k_shape = jax.ShapeDtypeStruct((t, LANES), BF16)
    vt_spec = pl.BlockSpec((LANES, ctx_len), lambda i: (0, i))
    vt_shape = jax.ShapeDtypeStruct((LANES, t), BF16)
    kv4_spec = pl.BlockSpec((4, ctx_len, LANES), lambda i: (0, i, 0))
    kv4_shape = jax.ShapeDtypeStruct((4, t, LANES), BF16)
    return pl.pallas_call(
        _ctx_proj_kernel,
        grid=(t // ctx_len,),
        in_specs=[pl.BlockSpec((ctx_len, d), lambda i: (i, 0)), const((1, d)), const((1, d)),
                  const((1, d)), const(w_kv.shape), const((1, LANES)), const(seg_k.shape)],
        out_specs=[kv4_spec, kv4_spec, k_spec, vt_spec],
        out_shape=[kv4_shape, kv4_shape, k_shape, vt_shape],
        compiler_params=_params("arbitrary"),
    )(c2, sc, sh, gpre, w_kv, kn, seg_k)


def _attend_t(w, k, kc, vt, vct, shift=None):
    st = _dot(k, w)
    sct = _dot(kc, w)
    if shift is None:
        shift = jnp.maximum(jnp.max(st, axis=0, keepdims=True),
                            jnp.max(sct, axis=0, keepdims=True))
    pt = jnp.exp2(st - shift)
    pct = jnp.exp2(sct - shift)
    denom = jnp.sum(pt, axis=0, keepdims=True) + jnp.sum(pct, axis=0, keepdims=True)
    o2 = _dot(vt, pt.astype(BF16)) + _dot(vct, pct.astype(BF16))
    return o2, denom


def _all_heads_t(qt_ref, o_ref, attend):
    tq = qt_ref.shape[1]
    n_kv = LANES // HEAD_DIM
    zeros = jnp.zeros((HEAD_DIM, tq), BF16)
    outs = []
    for h in range(qt_ref.shape[0] // HEAD_DIM):
        g = h // KV_GROUP
        qh = qt_ref[h * HEAD_DIM:(h + 1) * HEAD_DIM, :]
        w = jnp.concatenate([zeros] * g + [qh] + [zeros] * (n_kv - 1 - g), axis=0)
        o2, denom = attend(h, w)
        outs.append(o2[g * HEAD_DIM:(g + 1) * HEAD_DIM, :] / denom)
    o_ref[...] = jnp.concatenate(outs, axis=0).T.astype(BF16)


def _all_heads_fused_t(qt_ref, o_ref, attend):
    tq = qt_ref.shape[1]
    n_heads = qt_ref.shape[0] // HEAD_DIM
    n_kv = LANES // HEAD_DIM
    rows = []
    for g in range(n_kv):
        heads = [qt_ref[h * HEAD_DIM:(h + 1) * HEAD_DIM, :] if h // KV_GROUP == g
                 else jnp.zeros((HEAD_DIM, tq), BF16) for h in range(n_heads)]
        rows.append(jnp.concatenate(heads, axis=1))
    w = jnp.concatenate(rows, axis=0)
    o2, denom = attend(0, w)
    o2 = o2 / denom
    outs = [o2[(h // KV_GROUP) * HEAD_DIM:(h // KV_GROUP + 1) * HEAD_DIM, h * tq:(h + 1) * tq]
            for h in range(n_heads)]
    o_ref[...] = jnp.concatenate(outs, axis=0).T.astype(BF16)


def _attn_b_kernel(bound_ref, qt_ref, k_ref, vt_ref, kc_ref, vct_ref, o_ref):
    k, kc, vt, vct = k_ref[...], kc_ref[...], vt_ref[...], vct_ref[...]
    bound = bound_ref[0]

    @pl.when(bound <= SAFE_SOFTMAX_SHIFT)
    def _():
        _all_heads_fused_t(qt_ref, o_ref, lambda h, w: _attend_t(w, k, kc, vt, vct, shift=bound))

    @pl.when(jnp.logical_not(bound <= SAFE_SOFTMAX_SHIFT))
    def _():
        _all_heads_t(qt_ref, o_ref, lambda h, w: _attend_t(w, k, kc, vt, vct))


def _attn_specs(nq, tq, seq, ctx_len):
    nqb = seq // tq
    return dict(
        qt=pl.BlockSpec((nq, tq), lambda b, i: (0, b * nqb + i)),
        k=pl.BlockSpec((seq, LANES), lambda b, i: (b, 0)),
        vt=pl.BlockSpec((LANES, seq), lambda b, i: (0, b)),
        kc=pl.BlockSpec((ctx_len, LANES), lambda b, i: (b, 0)),
        vct=pl.BlockSpec((LANES, ctx_len), lambda b, i: (0, b)),
        out=pl.BlockSpec((tq, nq), lambda b, i: (b * nqb + i, 0)))


def _attention_b(score_bound, qbt, kb, vbt, kcb, vcbt, batch, seq, ctx_len):
    nq, t = qbt.shape
    tq = ATTN_B_TQ
    sp = _attn_specs(nq, tq, seq, ctx_len)
    return pl.pallas_call(
        _attn_b_kernel,
        grid=(batch, seq // tq),
        in_specs=[pl.BlockSpec(memory_space=pltpu.SMEM),
                  sp["qt"], sp["k"], sp["vt"], sp["kc"], sp["vct"]],
        out_specs=sp["out"],
        out_shape=jax.ShapeDtypeStruct((t, nq), BF16),
        compiler_params=_params("arbitrary", "arbitrary"),
    )(score_bound, qbt, kb, vbt, kcb, vcbt)


def _attn_a_kernel(sink_ref, q_ref, k_ref, v_ref, kc_ref, vc_ref, o_ref, bias_ref, *, seq):
    i = pl.program_id(1)
    tq = q_ref.shape[0]
    win = tq + 2 * WINDOW
    start = pl.multiple_of(jnp.clip(i * tq - WINDOW, 0, seq - win), WINDOW)
    qpos = i * tq + lax.broadcasted_iota(jnp.int32, (tq, win), 0)
    kpos = start + lax.broadcasted_iota(jnp.int32, (tq, win), 1)
    bias_ref[...] = jnp.where(jnp.abs(kpos - qpos) <= WINDOW, 0.0, NEG_INF).astype(F32)
    n_pairs = q_ref.shape[1] // LANES
    pairs_per_kv = KV_GROUP // 2
    rowmax = lambda s: jnp.max(s, axis=-1, keepdims=True)
    rowsum = lambda s: jnp.sum(s, axis=-1, keepdims=True)
    for pair in range(n_pairs):
        g = pair // pairs_per_kv
        q2 = q_ref[:, pair * LANES:(pair + 1) * LANES]
        acc = None
        for half in range(2):
            kv = 2 * g + half
            sink = sink_ref[2 * pair + half] * LOG2_E
            s = _dot_nt(q2, k_ref[kv, pl.ds(start, win), :]) + bias_ref[...]
            sc = _dot_nt(q2, kc_ref[kv])
            m = jnp.maximum(jnp.maximum(rowmax(s), rowmax(sc)), sink)
            p = jnp.exp2(s - m)
            pc = jnp.exp2(sc - m)
            denom = rowsum(p) + rowsum(pc) + jnp.exp2(sink - m)
            o = (_dot(p.astype(BF16), v_ref[kv, pl.ds(start, win), :])
                 + _dot(pc.astype(BF16), vc_ref[kv])) / denom
            acc = o if acc is None else acc + o
        o_ref[:, pair * LANES:(pair + 1) * LANES] = acc.astype(BF16)


def _attention_a(sink, qa, ka4, va4, kca4, vca4, batch, seq, ctx_len):
    t, nq = qa.shape
    tq = ATTN_A_TQ
    nqb = seq // tq
    lat = pl.BlockSpec((4, seq, LANES), lambda b, i: (0, b, 0))
    cx = pl.BlockSpec((4, ctx_len, LANES), lambda b, i: (0, b, 0))
    qs = pl.BlockSpec((tq, nq), lambda b, i: (b * nqb + i, 0))
    return pl.pallas_call(
        functools.partial(_attn_a_kernel, seq=seq),
        grid=(batch, nqb),
        in_specs=[pl.BlockSpec(memory_space=pltpu.SMEM), qs, lat, lat, cx, cx],
        out_specs=qs,
        out_shape=jax.ShapeDtypeStruct((t, nq), BF16),
        scratch_shapes=[pltpu.VMEM((tq, tq + 2 * WINDOW), F32)],
        compiler_params=_params("arbitrary", "arbitrary"),
    )(sink, qa, ka4, va4, kca4, vca4)


def _out_kernel(oa_ref, ob_ref, x_ref, g1_ref, sc2_ref, sh2_ref, ga_ref, gb_ref, gpost_ref,
                gpre2_ref, woa_ref, wob_ref, wrh_ref, wrl_ref, br_ref,
                x1_ref, h2_ref, rinfo_ref, rt_ref, tcarry_ref, tcnt_ref, cnt_ref, carry_ref):
    step = pl.program_id(0)

    @pl.when(step == 0)
    def _():
        carry_ref[...] = jnp.zeros_like(carry_ref)

    na = _rms(oa_ref[...].astype(F32)) * ga_ref[...]
    nb = _rms(ob_ref[...].astype(F32)) * gb_ref[...]
    ox = _dot(na.astype(BF16), woa_ref[...]) + _dot(nb.astype(BF16), wob_ref[...])
    x1 = x_ref[...] + g1_ref[0] * (_rms(ox) * gpost_ref[...])
    x1_ref[...] = x1
    h2 = _rms(x1) * gpre2_ref[...] * (1.0 + sc2_ref[0]) + sh2_ref[0]
    h2_ref[...] = h2

    h_hi, h_lo = _split_bf16(h2)
    logits = (_dot(h_hi, wrh_ref[...]) + _dot(h_lo, wrh_ref[...]) + _dot(h_hi, wrl_ref[...])
              + br_ref[...])
    tm = logits.shape[0]
    lane = lax.broadcasted_iota(jnp.int32, logits.shape, 1)
    lanef = lane.astype(F32)
    big = jnp.float32(1e9)
    ninf = jnp.float32(-jnp.inf)
    rowmax = lambda v: jnp.max(v, axis=-1, keepdims=True)
    rowmin = lambda v: jnp.min(v, axis=-1, keepdims=True)
    rowsum = lambda v: jnp.sum(v, axis=-1, keepdims=True)

    gmask = (lane >= N_EXPERTS) & (lane < N_EXPERTS + N_GROUPS)
    lg = jnp.where(gmask, logits, ninf)
    gmax = rowmax(lg)
    gidx = rowmin(jnp.where(lg == gmax, lanef, big)) - N_EXPERTS
    g_w = 1.0 / rowsum(jnp.exp(lg - gmax))
    lane_group = (lane // EXPERTS_PER_GROUP).astype(F32)
    emask = (lane < N_EXPERTS) & (lane_group == gidx)
    le = jnp.where(emask, logits, ninf)
    m1 = rowmax(le)
    i1 = rowmin(jnp.where(le == m1, lanef, big))
    le2 = jnp.where(lanef == i1, ninf, le)
    m2 = rowmax(le2)
    i2 = rowmin(jnp.where(le2 == m2, lanef, big))
    e2 = jnp.exp(m2 - m1)
    w0 = g_w / (1.0 + e2)
    w1 = g_w * e2 / (1.0 + e2)

    hit1 = lanef == i1
    hit2 = lanef == i2
    onehot = jnp.where(hit1 | hit2, 1.0, 0.0).astype(F32)
    r = lax.broadcasted_iota(jnp.int32, (tm, tm), 0)
    c = lax.broadcasted_iota(jnp.int32, (tm, tm), 1)
    strict_lower = jnp.where(r > c, 1.0, 0.0).astype(BF16)
    within = _dot(strict_lower, onehot.astype(BF16))
    tile_cnt = jnp.sum(onehot, axis=0, keepdims=True)
    incl = jnp.broadcast_to(tile_cnt, (8, LANES))
    lane8 = lax.broadcasted_iota(jnp.int32, (8, LANES), 1)
    shift = 1
    while shift < LANES:
        incl = incl + jnp.where(lane8 >= shift, pltpu.roll(incl, shift, 1), 0.0)
        shift *= 2
    local = within + (incl[0:1] - tile_cnt)
    pos0 = rowsum(jnp.where(hit1, local, 0.0))
    pos1 = rowsum(jnp.where(hit2, local, 0.0))
    tcarry_ref[0] = carry_ref[...]
    tcnt_ref[0] = tile_cnt
    carry_ref[...] += tile_cnt
    cnt_ref[...] = carry_ref[...]

    info = jnp.zeros_like(logits)
    for k, val in enumerate((i1, i2, pos0, pos1, w0, w1)):
        info = jnp.where(lane == k, val, info)
    rinfo_ref[...] = info
    rt_ref[...] = info.T[0:8, :]


def _out_and_route(oa, ob, x2, g1, sc2, sh2, ga, gb, gpost, gpre2, woa, wob, wrh, wrl, br, seq):
    t, d = x2.shape
    tm = OUT_TM
    tpb = seq // tm
    nq = oa.shape[1]
    const = lambda shape: pl.BlockSpec(shape, lambda i: (0,) * len(shape))
    per_batch = pl.BlockSpec((1, 1, d), lambda i: (i // tpb, 0, 0))
    rows = lambda n: pl.BlockSpec((tm, n), lambda i: (i, 0))
    per_tile = pl.BlockSpec((1, 1, LANES), lambda i: (i, 0, 0))
    return pl.pallas_call(
        _out_kernel,
        grid=(t // tm,),
        in_specs=[rows(nq), rows(nq), rows(d), per_batch, per_batch, per_batch,
                  const((1, nq)), const((1, nq)), const((1, d)), const((1, d)),
                  const(woa.shape), const(wob.shape), const(wrh.shape), const(wrl.shape),
                  const((1, LANES))],
        out_specs=[rows(d), rows(d), rows(LANES), pl.BlockSpec((8, tm), lambda i: (0, i)),
                   per_tile, per_tile, const((1, LANES))],
        out_shape=[jax.ShapeDtypeStruct((t, d), F32), jax.ShapeDtypeStruct((t, d), F32),
                   jax.ShapeDtypeStruct((t, LANES), F32), jax.ShapeDtypeStruct((8, t), F32),
                   jax.ShapeDtypeStruct((t // tm, 1, LANES), F32),
                   jax.ShapeDtypeStruct((t // tm, 1, LANES), F32),
                   jax.ShapeDtypeStruct((1, LANES), F32)],
        scratch_shapes=[pltpu.VMEM((1, LANES), F32)],
        compiler_params=_params("arbitrary"),
    )(oa, ob, x2, g1, sc2, sh2, ga, gb, gpost, gpre2, woa, wob, wrh, wrl, br)


PACK_ROWS = 4
U32 = jnp.uint32
HIGH_HALF = 0xFFFF0000


def _pack_rows(ref, x):
    bits = lax.bitcast_convert_type(x.astype(BF16).astype(F32), U32)
    half = PACK_ROWS * LANES
    for c in range(PACK_ROWS):
        lo = bits[:, c * LANES:(c + 1) * LANES] >> 16
        hi = bits[:, half + c * LANES:half + (c + 1) * LANES] & U32(HIGH_HALF)
        ref[:, c, :] = lo | hi


def _unpack_rows(ref):
    words = [ref[:, c, :] for c in range(PACK_ROWS)]
    lo = [lax.bitcast_convert_type(w << 16, F32) for w in words]
    hi = [lax.bitcast_convert_type(w & U32(HIGH_HALF), F32) for w in words]
    return jnp.concatenate(lo + hi, axis=1).astype(BF16)


def _for_each_run_piece(rdst_ref, rlen_ref, tile, max_len, fn):
    n_bits = max_len.bit_length()

    def run(e, local):
        length = rlen_ref[tile * N_EXPERTS + e]
        dst = rdst_ref[tile * N_EXPERTS + e]
        for b in range(n_bits):
            size = 1 << b

            @pl.when(((length >> b) & 1) == 1)
            def _():
                done = length & (size - 1)
                fn(local + done, dst + done, size)
        return local + length

    lax.fori_loop(0, N_EXPERTS, run, 0)


def _token_rows(ref, row0, n_rows):
    return ref.at[pl.ds(row0, n_rows)]


def _dispatch_kernel(rdst_ref, rlen_ref, ends_ref, nv_ref, h_ref, rt_ref, xs_ref,
                     sorted_ref, zero_ref, sem, zsem, *, te, n_tiles):
    k = pl.program_id(0)
    nk = pl.num_programs(0)
    tm = h_ref.shape[0]
    rows = 2 * tm
    slot = k % 2

    def wait_slot(s):
        pltpu.make_async_copy(sorted_ref.at[s], _token_rows(xs_ref, 0, rows), sem.at[s]).wait()

    @pl.when(k == 0)
    def _():
        zero_ref[...] = jnp.zeros_like(zero_ref)

        def pad_copy(row0):
            return pltpu.make_async_copy(zero_ref, _token_rows(xs_ref, row0, te), zsem)

        def for_each_pad_tile(fn):
            def expert_pad(e, carry):
                end = ends_ref[e]
                prev = jnp.where(e > 0, ends_ref[jnp.maximum(e - 1, 0)], 0)

                @pl.when(end > prev)
                def _():
                    fn(pad_copy(end - te))
                return carry

            def tail_pad(j, carry):
                fn(pad_copy(j * te))
                return carry

            lax.fori_loop(0, N_EXPERTS, expert_pad, 0)
            lax.fori_loop(nv_ref[0], n_tiles, tail_pad, 0)

        for_each_pad_tile(lambda cp: cp.start())
        for_each_pad_tile(lambda cp: cp.wait())

    @pl.when(k >= 2)
    def _():
        wait_slot(slot)

    pos0 = rt_ref[2:3, :]
    pos1 = rt_ref[3:4, :]
    r = lax.broadcasted_iota(jnp.int32, (rows, tm), 0).astype(F32)
    perm = jnp.where((r == pos0) | (r == pos1), 1.0, 0.0).astype(BF16)
    srt = _dot(perm, h_ref[...].astype(BF16))
    buf = sorted_ref.at[slot]
    _pack_rows(buf, srt)

    def copy_piece(local, dst, size):
        pltpu.make_async_copy(_token_rows(buf, local, size), _token_rows(xs_ref, dst, size),
                              sem.at[slot]).start()

    _for_each_run_piece(rdst_ref, rlen_ref, k, tm, copy_piece)

    @pl.when(k == nk - 1)
    def _():
        wait_slot(slot)

        @pl.when(nk >= 2)
        def _():
            wait_slot(1 - slot)


def _dispatch(run_dst, run_len, ends, n_valid, h2, rt, n_rows):
    t, d = h2.shape
    assert d == 2 * PACK_ROWS * LANES
    tm = MOVE_TM
    te = EXPERT_TE
    return pl.pallas_call(
        functools.partial(_dispatch_kernel, te=te, n_tiles=n_rows // te),
        grid_spec=pltpu.PrefetchScalarGridSpec(
            num_scalar_prefetch=4,
            grid=(t // tm,),
            in_specs=[pl.BlockSpec((tm, d), lambda i, *_: (i, 0)),
                      pl.BlockSpec((8, tm), lambda i, *_: (0, i))],
            out_specs=pl.BlockSpec(memory_space=pl.ANY),
            scratch_shapes=[pltpu.VMEM((2, 2 * tm, PACK_ROWS, LANES), U32),
                            pltpu.VMEM((te, PACK_ROWS, LANES), U32),
                            pltpu.SemaphoreType.DMA((2,)), pltpu.SemaphoreType.DMA(())]),
        out_shape=jax.ShapeDtypeStruct((n_rows, PACK_ROWS, LANES), U32),
        compiler_params=_params("arbitrary"),
    )(run_dst, run_len, ends, n_valid, h2, rt)


def _expert_kernel(te_ref, nv_ref, xs_ref, wg_ref, wu_ref, wd_ref, ys_ref, wg_bf, wu_bf, wd_bf):
    j = pl.program_id(0)
    valid = j < nv_ref[0]
    changed = (j == 0) | (te_ref[j] != te_ref[jnp.maximum(j - 1, 0)])

    @pl.when(valid & changed)
    def _():
        wg_bf[...] = wg_ref[0].astype(BF16)
        wu_bf[...] = wu_ref[0].astype(BF16)
        wd_bf[...] = wd_ref[0].astype(BF16)

    @pl.when(valid)
    def _():
        xb = _unpack_rows(xs_ref)
        gate = _dot(xb, wg_bf[...])
        up = _dot(xb, wu_bf[...])
        act = gate * jax.nn.sigmoid(gate) * up
        _pack_rows(ys_ref, _dot(act.astype(BF16), wd_bf[...]))

    @pl.when(jnp.logical_not(valid))
    def _():
        ys_ref[...] = jnp.zeros_like(ys_ref)


def _expert_mlp(tile_expert, n_valid, xs, w_gate, w_up, w_down):
    te = EXPERT_TE
    d, ff = w_gate.shape[1:]
    n_rows = xs.shape[0]
    blk = (te, PACK_ROWS, LANES)
    tile = lambda j, e, nv: (jnp.minimum(j, nv[0] - 1), 0, 0)
    wsel = lambda j, e, nv: (e[j], 0, 0)
    return pl.pallas_call(
        _expert_kernel,
        grid_spec=pltpu.PrefetchScalarGridSpec(
            num_scalar_prefetch=2,
            grid=(n_rows // te,),
            in_specs=[pl.BlockSpec(blk, tile),
                      pl.BlockSpec((1, d, ff), wsel), pl.BlockSpec((1, d, ff), wsel),
                      pl.BlockSpec((1, ff, d), wsel)],
            out_specs=pl.BlockSpec(blk, lambda j, e, nv: (j, 0, 0)),
            scratch_shapes=[pltpu.VMEM((d, ff), BF16), pltpu.VMEM((d, ff), BF16),
                            pltpu.VMEM((ff, d), BF16)]),
        out_shape=jax.ShapeDtypeStruct(xs.shape, U32),
        compiler_params=_params("arbitrary"),
    )(tile_expert, n_valid, xs, w_gate, w_up, w_down)


def _combine_kernel(rdst_ref, rlen_ref, x1_ref, rinfo_ref, g2_ref, gpost_ref, ys_ref, o_ref,
                    gath_ref, sem):
    k = pl.program_id(0)
    nk = pl.num_programs(0)
    tm = x1_ref.shape[0]
    rows = 2 * tm
    slot = k % 2

    def gather_runs(tile, s):
        buf = gath_ref.at[s]

        def copy_piece(local, src, size):
            pltpu.make_async_copy(_token_rows(ys_ref, src, size), _token_rows(buf, local, size),
                                  sem.at[s]).start()

        _for_each_run_piece(rdst_ref, rlen_ref, tile, tm, copy_piece)

    @pl.when(k == 0)
    def _():
        gather_runs(0, 0)

    @pl.when(k + 1 < nk)
    def _():
        gather_runs(k + 1, 1 - slot)

    buf = gath_ref.at[slot]
    pltpu.make_async_copy(_token_rows(ys_ref, 0, rows), buf, sem.at[slot]).wait()
    g = _unpack_rows(buf)
    info = rinfo_ref[...]
    col = lax.broadcasted_iota(jnp.int32, (tm, rows), 1).astype(F32)
    pick0 = jnp.where(col == info[:, 2:3], 1.0, 0.0).astype(BF16)
    pick1 = jnp.where(col == info[:, 3:4], 1.0, 0.0).astype(BF16)
    fx = info[:, 4:5] * _dot(pick0, g) + info[:, 5:6] * _dot(pick1, g)
    o_ref[...] = x1_ref[...] + g2_ref[0] * (_rms(fx) * gpost_ref[...])


def _combine(run_dst, run_len, x1, rinfo, g2, gpost, ys, seq):
    t, d = x1.shape
    tm = MOVE_TM
    tpb = seq // tm
    return pl.pallas_call(
        _combine_kernel,
        grid_spec=pltpu.PrefetchScalarGridSpec(
            num_scalar_prefetch=2,
            grid=(t // tm,),
            in_specs=[pl.BlockSpec((tm, d), lambda i, *_: (i, 0)),
                      pl.BlockSpec((tm, LANES), lambda i, *_: (i, 0)),
                      pl.BlockSpec((1, 1, d), lambda i, *_: (i // tpb, 0, 0)),
                      pl.BlockSpec((1, d), lambda i, *_: (0, 0)),
                      pl.BlockSpec(memory_space=pl.ANY)],
            out_specs=pl.BlockSpec((tm, d), lambda i, *_: (i, 0)),
            scratch_shapes=[pltpu.VMEM((2, 2 * tm, PACK_ROWS, LANES), U32),
                            pltpu.SemaphoreType.DMA((2,))]),
        out_shape=jax.ShapeDtypeStruct((t, d), F32),
        compiler_params=_params("arbitrary"),
    )(run_dst, run_len, x1, rinfo, g2, gpost, ys)


def _rope_tables(seq):
    pos = jnp.arange(seq, dtype=jnp.int32)
    row = (pos // GRID_W).astype(F32)
    col = (pos % GRID_W).astype(F32)
    axis_dim = HEAD_DIM // 2
    inv_freq = ROPE_THETA ** (-jnp.arange(0, axis_dim, 2, dtype=F32) / axis_dim)
    ang = jnp.concatenate([row[:, None] * inv_freq, col[:, None] * inv_freq], axis=-1)
    pair = (jnp.arange(LANES) % HEAD_DIM) // 2
    cos = jnp.cos(ang)[:, pair]
    sin = jnp.sin(ang)[:, pair]
    even = (jnp.arange(LANES) % 2) == 0
    return cos, jnp.where(even, -sin, 0.0), jnp.where(even, 0.0, sin)


def _segment_ones(n):
    seg = jnp.arange(n) // HEAD_DIM
    return (seg[:, None] == seg[None, :]).astype(BF16)


def kernel(x, c, ctx, c_ctx, w_mod, b_mod, attn_pre_norm, attn_post_norm, w_in, a_sink,
           b_q_norm, b_k_norm, a_out_norm, b_out_norm, w_out, ffn_pre_norm, ffn_post_norm,
           w_group, b_group, w_router, b_router, w_gate, w_up, w_down):
    batch, seq, d = x.shape
    ctx_len = ctx.shape[1]
    assert w_mod.shape[0] == 1, "single-layer stack only (context stream is never updated)"
    assert seq % ATTN_A_TQ == 0 and seq >= ATTN_A_TQ + 2 * WINDOW
    assert seq % PROJ_TM == 0 and seq % ATTN_B_TQ == 0 and seq % OUT_TM == 0 and seq % MOVE_TM == 0
    t = batch * seq
    nq = d // 2
    nkv = nq // KV_GROUP
    assert nkv == LANES and w_in.shape[2] == 2 * nq + 4 * nkv

    cc = jnp.concatenate([c, c_ctx[None, :], jnp.zeros((16 - batch - 1, d), F32)], axis=0)
    mod = _modulation(cc, w_mod[0], b_mod[0])
    sh1, sc1, g1, sh2, sc2, g2 = (m.reshape(batch, 1, d) for m in jnp.split(mod[:batch], 6, axis=-1))
    csh1, csc1 = (m.reshape(1, d) for m in jnp.split(mod[batch], 6)[:2])

    x2 = x.reshape(t, d)
    c2 = ctx.reshape(batch * ctx_len, d)
    gpre = attn_pre_norm[0].reshape(1, d)
    w_in_bf = w_in[0].astype(BF16)
    kv_cols = jnp.concatenate([w_in_bf[:, nq:nq + 2 * nkv], w_in_bf[:, 2 * nq + 2 * nkv:]], axis=1)
    qn = jnp.tile(b_q_norm[0], nq // HEAD_DIM).reshape(1, nq)
    kn = jnp.tile(b_k_norm[0], nkv // HEAD_DIM).reshape(1, nkv)
    seg_q, seg_k = _segment_ones(nq), _segment_ones(nkv)
    qa, ka4, va4, qbt, kb, vbt = _project_latents(
        x2, sc1, sh1, gpre, w_in_bf, _rope_tables(seq), qn, kn, seg_q, seg_k, seq)
    kca4, vca4, kcb, vcbt = _project_context(c2, csc1, csh1, gpre, kv_cols, kn, seg_k, ctx_len)

    oa = _attention_a(a_sink[0], qa, ka4, va4, kca4, vca4, batch, seq, ctx_len)
    score_bound = (1.01 * HEAD_DIM ** 0.5 * LOG2_E
                   * jnp.max(jnp.abs(b_q_norm[0])) * jnp.max(jnp.abs(b_k_norm[0]))).reshape(1)
    ob = _attention_b(score_bound, qbt, kb, vbt, kcb, vcbt, batch, seq, ctx_len)

    w_out_bf = w_out[0].astype(BF16)
    w_r = jnp.zeros((d, LANES), F32)
    w_r = w_r.at[:, :N_EXPERTS].set(w_router[0]).at[:, N_EXPERTS:N_EXPERTS + N_GROUPS].set(w_group[0])
    w_r_hi = w_r.astype(BF16)
    w_r_lo = (w_r - w_r_hi.astype(F32)).astype(BF16)
    b_r = jnp.zeros((1, LANES), F32)
    b_r = b_r.at[0, :N_EXPERTS].set(b_router[0]).at[0, N_EXPERTS:N_EXPERTS + N_GROUPS].set(b_group[0])
    x1, h2, rinfo, rt, tcarry, tcnt, counts = _out_and_route(
        oa, ob, x2, g1, sc2, sh2, a_out_norm[0].reshape(1, nq), b_out_norm[0].reshape(1, nq),
        attn_post_norm[0].reshape(1, d), ffn_pre_norm[0].reshape(1, d),
        w_out_bf[:nq], w_out_bf[nq:], w_r_hi, w_r_lo, b_r, seq)

    te = EXPERT_TE
    n_tiles = -(-(2 * t + N_EXPERTS * (te - 1)) // te)
    n_rows = n_tiles * te
    cnt = counts[0, :N_EXPERTS].astype(jnp.int32)
    padded = ((cnt + te - 1) // te) * te
    ends = jnp.cumsum(padded)
    offs = ends - padded
    run_dst = (offs[None, :] + tcarry[:, 0, :N_EXPERTS].astype(jnp.int32)).reshape(-1)
    run_len = tcnt[:, 0, :N_EXPERTS].astype(jnp.int32).reshape(-1)
    n_valid = (ends[-1] // te).astype(jnp.int32).reshape(1)
    tile_start = jnp.arange(n_tiles, dtype=jnp.int32) * te
    tile_expert = jnp.sum(ends[None, :] <= tile_start[:, None], axis=1).astype(jnp.int32)
    last_expert = tile_expert[jnp.maximum(n_valid[0] - 1, 0)]
    tile_expert = jnp.where(tile_start < ends[-1], tile_expert, last_expert)

    xs = _dispatch(run_dst, run_len, ends.astype(jnp.int32), n_valid, h2, rt, n_rows)
    ys = _expert_mlp(tile_expert, n_valid, xs, w_gate[0], w_up[0], w_down[0])
    out = _combine(run_dst, run_len, x1, rinfo, g2, ffn_post_norm[0].reshape(1, d), ys, seq)
    return out.reshape(batch, seq, d)
```

```python
import functools

import jax
import jax.numpy as jnp
from jax import lax
from jax.experimental import pallas as pl
from jax.experimental.pallas import tpu as pltpu

F32 = jnp.float32
BF16 = jnp.bfloat16

GRID_W = 64
HEAD_DIM = 64
KV_GROUP = 4
WINDOW = 128
ROPE_THETA = 10000.0
N_GROUPS = 4
EXPERTS_PER_GROUP = 8
N_EXPERTS = N_GROUPS * EXPERTS_PER_GROUP
EPS = 1e-6
NEG_INF = -1e30
LOG2_E = 1.4426950408889634
SAFE_SOFTMAX_SHIFT = 40.0

LANES = 128
V7X_VMEM_LIMIT = 56 * 1024 * 1024

PROJ_TM = 512
ATTN_A_TQ = 512
ATTN_B_TQ = 256
OUT_TM = 512
EXPERT_TE = 512
MOVE_TM = OUT_TM


def _params(*sem):
    return pltpu.CompilerParams(dimension_semantics=sem, vmem_limit_bytes=V7X_VMEM_LIMIT)


def _dot(a, b):
    return jnp.dot(a, b, preferred_element_type=F32)


def _dot_nt(a, b):
    return lax.dot_general(a, b, (((1,), (1,)), ((), ())), preferred_element_type=F32)


def _rms(x):
    return x * lax.rsqrt(jnp.mean(x * x, axis=-1, keepdims=True) + EPS)


def _split_bf16(x):
    hi = x.astype(BF16)
    lo = (x - hi.astype(F32)).astype(BF16)
    return hi, lo


def _mod_kernel(c_ref, w_ref, b_ref, o_ref):
    cc = c_ref[...]
    s = cc * jax.nn.sigmoid(cc)
    s_hi, s_lo = _split_bf16(s)
    w_hi, w_lo = _split_bf16(w_ref[...])
    o_ref[...] = _dot(s_hi, w_hi) + _dot(s_lo, w_hi) + _dot(s_hi, w_lo) + b_ref[...]


def _modulation(cc, w_mod, b_mod):
    rows, d = cc.shape
    n = w_mod.shape[1]
    bn = 1024
    return pl.pallas_call(
        _mod_kernel,
        grid=(n // bn,),
        in_specs=[pl.BlockSpec((rows, d), lambda i: (0, 0)),
                  pl.BlockSpec((d, bn), lambda i: (0, i)),
                  pl.BlockSpec((1, bn), lambda i: (0, i))],
        out_specs=pl.BlockSpec((rows, bn), lambda i: (0, i)),
        out_shape=jax.ShapeDtypeStruct((rows, n), F32),
        compiler_params=_params("arbitrary"),
    )(cc, w_mod, b_mod.reshape(1, n))


def _rope(x, cos, sin_a, sin_b):
    return x * cos + pltpu.roll(x, LANES - 1, 1) * sin_a + pltpu.roll(x, 1, 1) * sin_b


def _head_norm(x, seg_ref, gain):
    ss = _dot((x * x).astype(BF16), seg_ref[...])
    return x * lax.rsqrt(ss * (1.0 / HEAD_DIM) + EPS) * gain


def _store_pair_variants(ref, t):
    lane = lax.broadcasted_iota(jnp.int32, t.shape, 1)
    lo = lane < HEAD_DIM
    sw = pltpu.roll(t, HEAD_DIM, 1)
    zero = jnp.zeros_like(t)
    ref[0] = jnp.where(lo, t, zero).astype(BF16)
    ref[1] = jnp.where(lo, zero, sw).astype(BF16)
    ref[2] = jnp.where(lo, sw, zero).astype(BF16)
    ref[3] = jnp.where(lo, zero, t).astype(BF16)


def _proj_kernel(x_ref, sc_ref, sh_ref, gpre_ref, w_ref, cos_ref, sa_ref, sb_ref,
                 qn_ref, kn_ref, seg_q_ref, seg_k_ref,
                 qa_ref, ka_ref, va_ref, qbt_ref, kb_ref, vbt_ref):
    h = _rms(x_ref[...]) * gpre_ref[...] * (1.0 + sc_ref[0]) + sh_ref[0]
    p = _dot(h.astype(BF16), w_ref[...])
    cos, sa, sb = cos_ref[...], sa_ref[...], sb_ref[...]
    nq = qa_ref.shape[1]
    q_scale = HEAD_DIM ** -0.5 * LOG2_E
    for c in range(nq // LANES):
        qa_ref[:, c * LANES:(c + 1) * LANES] = (
            _rope(p[:, c * LANES:(c + 1) * LANES], cos, sa, sb) * q_scale).astype(BF16)
    o = nq
    _store_pair_variants(ka_ref, _rope(p[:, o:o + LANES], cos, sa, sb))
    _store_pair_variants(va_ref, p[:, o + LANES:o + 2 * LANES])
    o += 2 * LANES
    qb = _head_norm(p[:, o:o + nq], seg_q_ref, qn_ref[...])
    for c in range(nq // LANES):
        qbt_ref[c * LANES:(c + 1) * LANES, :] = (
            _rope(qb[:, c * LANES:(c + 1) * LANES], cos, sa, sb) * q_scale).T.astype(BF16)
    o += nq
    kb = _head_norm(p[:, o:o + LANES], seg_k_ref, kn_ref[...])
    kb_ref[...] = _rope(kb, cos, sa, sb).astype(BF16)
    vbt_ref[...] = p[:, o + LANES:o + 2 * LANES].T.astype(BF16)


def _ctx_proj_kernel(x_ref, sc_ref, sh_ref, gpre_ref, w_ref, kn_ref, seg_k_ref,
                     ka_ref, va_ref, kb_ref, vbt_ref):
    h = _rms(x_ref[...]) * gpre_ref[...] * (1.0 + sc_ref[...]) + sh_ref[...]
    p = _dot(h.astype(BF16), w_ref[...])
    _store_pair_variants(ka_ref, p[:, 0:LANES])
    _store_pair_variants(va_ref, p[:, LANES:2 * LANES])
    kb_ref[...] = _head_norm(p[:, 2 * LANES:3 * LANES], seg_k_ref, kn_ref[...]).astype(BF16)
    vbt_ref[...] = p[:, 3 * LANES:4 * LANES].T.astype(BF16)


def _project_latents(x2, sc, sh, gpre, w_in, tables, qn, kn, seg_q, seg_k, seq):
    t, d = x2.shape
    tm = PROJ_TM
    tpb = seq // tm
    nq = seg_q.shape[0]
    const = lambda shape: pl.BlockSpec(shape, lambda i: (0,) * len(shape))
    per_batch = pl.BlockSpec((1, 1, d), lambda i: (i // tpb, 0, 0))
    table = pl.BlockSpec((tm, LANES), lambda i: (i % tpb, 0))
    k_spec = pl.BlockSpec((tm, LANES), lambda i: (i, 0))
    k_shape = jax.ShapeDtypeStruct((t, LANES), BF16)
    vt_spec = pl.BlockSpec((LANES, tm), lambda i: (0, i))
    vt_shape = jax.ShapeDtypeStruct((LANES, t), BF16)
    qt_spec = pl.BlockSpec((nq, tm), lambda i: (0, i))
    qt_shape = jax.ShapeDtypeStruct((nq, t), BF16)
    q_spec = pl.BlockSpec((tm, nq), lambda i: (i, 0))
    q_shape = jax.ShapeDtypeStruct((t, nq), BF16)
    kv4_spec = pl.BlockSpec((4, tm, LANES), lambda i: (0, i, 0))
    kv4_shape = jax.ShapeDtypeStruct((4, t, LANES), BF16)
    return pl.pallas_call(
        _proj_kernel,
        grid=(t // tm,),
        in_specs=[pl.BlockSpec((tm, d), lambda i: (i, 0)), per_batch, per_batch, const((1, d)),
                  const(w_in.shape), table, table, table,
                  const((1, nq)), const((1, LANES)), const(seg_q.shape), const(seg_k.shape)],
        out_specs=[q_spec, kv4_spec, kv4_spec, qt_spec, k_spec, vt_spec],
        out_shape=[q_shape, kv4_shape, kv4_shape, qt_shape, k_shape, vt_shape],
        compiler_params=_params("arbitrary"),
    )(x2, sc, sh, gpre, w_in, *tables, qn, kn, seg_q, seg_k)


def _project_context(c2, sc, sh, gpre, w_kv, kn, seg_k, ctx_len):
    t, d = c2.shape
    const = lambda shape: pl.BlockSpec(shape, lambda i: (0,) * len(shape))
    k_spec = pl.BlockSpec((ctx_len, LANES), lambda i: (i, 0))
    k_shape = jax.ShapeDtypeStruct((t, LANES), BF16)
    vt_spec = pl.BlockSpec((LANES, ctx_len), lambda i: (0, i))
    vt_shape = jax.ShapeDtypeStruct((LANES, t), BF16)
    kv4_spec = pl.BlockSpec((4, ctx_len, LANES), lambda i: (0, i, 0))
    kv4_shape = jax.ShapeDtypeStruct((4, t, LANES), BF16)
    return pl.pallas_call(
        _ctx_proj_kernel,
        grid=(t // ctx_len,),
        in_specs=[pl.BlockSpec((ctx_len, d), lambda i: (i, 0)), const((1, d)), const((1, d)),
                  const((1, d)), const(w_kv.shape), const((1, LANES)), const(seg_k.shape)],
        out_specs=[kv4_spec, kv4_spec, k_spec, vt_spec],
        out_shape=[kv4_shape, kv4_shape, k_shape, vt_shape],
        compiler_params=_params("arbitrary"),
    )(c2, sc, sh, gpre, w_kv, kn, seg_k)


def _attend_t(w, k, kc, vt, vct, shift=None):
    st = _dot(k, w)
    sct = _dot(kc, w)
    if shift is None:
        shift = jnp.maximum(jnp.max(st, axis=0, keepdims=True),
                            jnp.max(sct, axis=0, keepdims=True))
    pt = jnp.exp2(st - shift)
    pct = jnp.exp2(sct - shift)
    denom = jnp.sum(pt, axis=0, keepdims=True) + jnp.sum(pct, axis=0, keepdims=True)
    o2 = _dot(vt, pt.astype(BF16)) + _dot(vct, pct.astype(BF16))
    return o2, denom


def _all_heads_t(qt_ref, o_ref, attend):
    tq = qt_ref.shape[1]
    n_kv = LANES // HEAD_DIM
    zeros = jnp.zeros((HEAD_DIM, tq), BF16)
    outs = []
    for h in range(qt_ref.shape[0] // HEAD_DIM):
        g = h // KV_GROUP
        qh = qt_ref[h * HEAD_DIM:(h + 1) * HEAD_DIM, :]
        w = jnp.concatenate([zeros] * g + [qh] + [zeros] * (n_kv - 1 - g), axis=0)
        o2, denom = attend(h, w)
        outs.append(o2[g * HEAD_DIM:(g + 1) * HEAD_DIM, :] / denom)
    o_ref[...] = jnp.concatenate(outs, axis=0).T.astype(BF16)


def _all_heads_fused_t(qt_ref, o_ref, attend):
    tq = qt_ref.shape[1]
    n_heads = qt_ref.shape[0] // HEAD_DIM
    n_kv = LANES // HEAD_DIM
    rows = []
    for g in range(n_kv):
        heads = [qt_ref[h * HEAD_DIM:(h + 1) * HEAD_DIM, :] if h // KV_GROUP == g
                 else jnp.zeros((HEAD_DIM, tq), BF16) for h in range(n_heads)]
        rows.append(jnp.concatenate(heads, axis=1))
    w = jnp.concatenate(rows, axis=0)
    o2, denom = attend(0, w)
    o2 = o2 / denom
    outs = [o2[(h // KV_GROUP) * HEAD_DIM:(h // KV_GROUP + 1) * HEAD_DIM, h * tq:(h + 1) * tq]
            for h in range(n_heads)]
    o_ref[...] = jnp.concatenate(outs, axis=0).T.astype(BF16)


def _attn_b_kernel(bound_ref, qt_ref, k_ref, vt_ref, kc_ref, vct_ref, o_ref):
    k, kc, vt, vct = k_ref[...], kc_ref[...], vt_ref[...], vct_ref[...]
    bound = bound_ref[0]

    @pl.when(bound <= SAFE_SOFTMAX_SHIFT)
    def _():
        _all_heads_fused_t(qt_ref, o_ref, lambda h, w: _attend_t(w, k, kc, vt, vct, shift=bound))

    @pl.when(jnp.logical_not(bound <= SAFE_SOFTMAX_SHIFT))
    def _():
        _all_heads_t(qt_ref, o_ref, lambda h, w: _attend_t(w, k, kc, vt, vct))


def _attn_specs(nq, tq, seq, ctx_len):
    nqb = seq // tq
    return dict(
        qt=pl.BlockSpec((nq, tq), lambda b, i: (0, b * nqb + i)),
        k=pl.BlockSpec((seq, LANES), lambda b, i: (b, 0)),
        vt=pl.BlockSpec((LANES, seq), lambda b, i: (0, b)),
        kc=pl.BlockSpec((ctx_len, LANES), lambda b, i: (b, 0)),
        vct=pl.BlockSpec((LANES, ctx_len), lambda b, i: (0, b)),
        out=pl.BlockSpec((tq, nq), lambda b, i: (b * nqb + i, 0)))


def _attention_b(score_bound, qbt, kb, vbt, kcb, vcbt, batch, seq, ctx_len):
    nq, t = qbt.shape
    tq = ATTN_B_TQ
    sp = _attn_specs(nq, tq, seq, ctx_len)
    return pl.pallas_call(
        _attn_b_kernel,
        grid=(batch, seq // tq),
        in_specs=[pl.BlockSpec(memory_space=pltpu.SMEM),
                  sp["qt"], sp["k"], sp["vt"], sp["kc"], sp["vct"]],
        out_specs=sp["out"],
        out_shape=jax.ShapeDtypeStruct((t, nq), BF16),
        compiler_params=_params("arbitrary", "arbitrary"),
    )(score_bound, qbt, kb, vbt, kcb, vcbt)


def _attn_a_kernel(sink_ref, q_ref, k_ref, v_ref, kc_ref, vc_ref, o_ref, bias_ref, *, seq):
    i = pl.program_id(1)
    tq = q_ref.shape[0]
    win = tq + 2 * WINDOW
    start = pl.multiple_of(jnp.clip(i * tq - WINDOW, 0, seq - win), WINDOW)
    qpos = i * tq + lax.broadcasted_iota(jnp.int32, (tq, win), 0)
    kpos = start + lax.broadcasted_iota(jnp.int32, (tq, win), 1)
    bias_ref[...] = jnp.where(jnp.abs(kpos - qpos) <= WINDOW, 0.0, NEG_INF).astype(F32)
    n_pairs = q_ref.shape[1] // LANES
    pairs_per_kv = KV_GROUP // 2
    rowmax = lambda s: jnp.max(s, axis=-1, keepdims=True)
    rowsum = lambda s: jnp.sum(s, axis=-1, keepdims=True)
    for pair in range(n_pairs):
        g = pair // pairs_per_kv
        q2 = q_ref[:, pair * LANES:(pair + 1) * LANES]
        acc = None
        for half in range(2):
            kv = 2 * g + half
            sink = sink_ref[2 * pair + half] * LOG2_E
            s = _dot_nt(q2, k_ref[kv, pl.ds(start, win), :]) + bias_ref[...]
            sc = _dot_nt(q2, kc_ref[kv])
            m = jnp.maximum(jnp.maximum(rowmax(s), rowmax(sc)), sink)
            p = jnp.exp2(s - m)
            pc = jnp.exp2(sc - m)
            denom = rowsum(p) + rowsum(pc) + jnp.exp2(sink - m)
            o = (_dot(p.astype(BF16), v_ref[kv, pl.ds(start, win), :])
                 + _dot(pc.astype(BF16), vc_ref[kv])) / denom
            acc = o if acc is None else acc + o
        o_ref[:, pair * LANES:(pair + 1) * LANES] = acc.astype(BF16)


def _attention_a(sink, qa, ka4, va4, kca4, vca4, batch, seq, ctx_len):
    t, nq = qa.shape
    tq = ATTN_A_TQ
    nqb = seq // tq
    lat = pl.BlockSpec((4, seq, LANES), lambda b, i: (0, b, 0))
    cx = pl.BlockSpec((4, ctx_len, LANES), lambda b, i: (0, b, 0))
    qs = pl.BlockSpec((tq, nq), lambda b, i: (b * nqb + i, 0))
    return pl.pallas_call(
        functools.partial(_attn_a_kernel, seq=seq),
        grid=(batch, nqb),
        in_specs=[pl.BlockSpec(memory_space=pltpu.SMEM), qs, lat, lat, cx, cx],
        out_specs=qs,
        out_shape=jax.ShapeDtypeStruct((t, nq), BF16),
        scratch_shapes=[pltpu.VMEM((tq, tq + 2 * WINDOW), F32)],
        compiler_params=_params("arbitrary", "arbitrary"),
    )(sink, qa, ka4, va4, kca4, vca4)


def _out_kernel(oa_ref, ob_ref, x_ref, g1_ref, sc2_ref, sh2_ref, ga_ref, gb_ref, gpost_ref,
                gpre2_ref, woa_ref, wob_ref, wrh_ref, wrl_ref, br_ref,
                x1_ref, h2_ref, rinfo_ref, rt_ref, tcarry_ref, tcnt_ref, cnt_ref, carry_ref):
    step = pl.program_id(0)

    @pl.when(step == 0)
    def _():
        carry_ref[...] = jnp.zeros_like(carry_ref)

    na = _rms(oa_ref[...].astype(F32)) * ga_ref[...]
    nb = _rms(ob_ref[...].astype(F32)) * gb_ref[...]
    ox = _dot(na.astype(BF16), woa_ref[...]) + _dot(nb.astype(BF16), wob_ref[...])
    x1 = x_ref[...] + g1_ref[0] * (_rms(ox) * gpost_ref[...])
    x1_ref[...] = x1
    h2 = _rms(x1) * gpre2_ref[...] * (1.0 + sc2_ref[0]) + sh2_ref[0]
    h2_ref[...] = h2

    h_hi, h_lo = _split_bf16(h2)
    logits = (_dot(h_hi, wrh_ref[...]) + _dot(h_lo, wrh_ref[...]) + _dot(h_hi, wrl_ref[...])
              + br_ref[...])
    tm = logits.shape[0]
    lane = lax.broadcasted_iota(jnp.int32, logits.shape, 1)
    lanef = lane.astype(F32)
    big = jnp.float32(1e9)
    ninf = jnp.float32(-jnp.inf)
    rowmax = lambda v: jnp.max(v, axis=-1, keepdims=True)
    rowmin = lambda v: jnp.min(v, axis=-1, keepdims=True)
    rowsum = lambda v: jnp.sum(v, axis=-1, keepdims=True)

    gmask = (lane >= N_EXPERTS) & (lane < N_EXPERTS + N_GROUPS)
    lg = jnp.where(gmask, logits, ninf)
    gmax = rowmax(lg)
    gidx = rowmin(jnp.where(lg == gmax, lanef, big)) - N_EXPERTS
    g_w = 1.0 / rowsum(jnp.exp(lg - gmax))
    lane_group = (lane // EXPERTS_PER_GROUP).astype(F32)
    emask = (lane < N_EXPERTS) & (lane_group == gidx)
    le = jnp.where(emask, logits, ninf)
    m1 = rowmax(le)
    i1 = rowmin(jnp.where(le == m1, lanef, big))
    le2 = jnp.where(lanef == i1, ninf, le)
    m2 = rowmax(le2)
    i2 = rowmin(jnp.where(le2 == m2, lanef, big))
    e2 = jnp.exp(m2 - m1)
    w0 = g_w / (1.0 + e2)
    w1 = g_w * e2 / (1.0 + e2)

    hit1 = lanef == i1
    hit2 = lanef == i2
    onehot = jnp.where(hit1 | hit2, 1.0, 0.0).astype(F32)
    r = lax.broadcasted_iota(jnp.int32, (tm, tm), 0)
    c = lax.broadcasted_iota(jnp.int32, (tm, tm), 1)
    strict_lower = jnp.where(r > c, 1.0, 0.0).astype(BF16)
    within = _dot(strict_lower, onehot.astype(BF16))
    tile_cnt = jnp.sum(onehot, axis=0, keepdims=True)
    incl = jnp.broadcast_to(tile_cnt, (8, LANES))
    lane8 = lax.broadcasted_iota(jnp.int32, (8, LANES), 1)
    shift = 1
    while shift < LANES:
        incl = incl + jnp.where(lane8 >= shift, pltpu.roll(incl, shift, 1), 0.0)
        shift *= 2
    local = within + (incl[0:1] - tile_cnt)
    pos0 = rowsum(jnp.where(hit1, local, 0.0))
    pos1 = rowsum(jnp.where(hit2, local, 0.0))
    tcarry_ref[0] = carry_ref[...]
    tcnt_ref[0] = tile_cnt
    carry_ref[...] += tile_cnt
    cnt_ref[...] = carry_ref[...]

    info = jnp.zeros_like(logits)
    for k, val in enumerate((i1, i2, pos0, pos1, w0, w1)):
        info = jnp.where(lane == k, val, info)
    rinfo_ref[...] = info
    rt_ref[...] = info.T[0:8, :]


def _out_and_route(oa, ob, x2, g1, sc2, sh2, ga, gb, gpost, gpre2, woa, wob, wrh, wrl, br, seq):
    t, d = x2.shape
    tm = OUT_TM
    tpb = seq // tm
    nq = oa.shape[1]
    const = lambda shape: pl.BlockSpec(shape, lambda i: (0,) * len(shape))
    per_batch = pl.BlockSpec((1, 1, d), lambda i: (i // tpb, 0, 0))
    rows = lambda n: pl.BlockSpec((tm, n), lambda i: (i, 0))
    per_tile = pl.BlockSpec((1, 1, LANES), lambda i: (i, 0, 0))
    return pl.pallas_call(
        _out_kernel,
        grid=(t // tm,),
        in_specs=[rows(nq), rows(nq), rows(d), per_batch, per_batch, per_batch,
                  const((1, nq)), const((1, nq)), const((1, d)), const((1, d)),
                  const(woa.shape), const(wob.shape), const(wrh.shape), const(wrl.shape),
                  const((1, LANES))],
        out_specs=[rows(d), rows(d), rows(LANES), pl.BlockSpec((8, tm), lambda i: (0, i)),
                   per_tile, per_tile, const((1, LANES))],
        out_shape=[jax.ShapeDtypeStruct((t, d), F32), jax.ShapeDtypeStruct((t, d), F32),
                   jax.ShapeDtypeStruct((t, LANES), F32), jax.ShapeDtypeStruct((8, t), F32),
                   jax.ShapeDtypeStruct((t // tm, 1, LANES), F32),
                   jax.ShapeDtypeStruct((t // tm, 1, LANES), F32),
                   jax.ShapeDtypeStruct((1, LANES), F32)],
        scratch_shapes=[pltpu.VMEM((1, LANES), F32)],
        compiler_params=_params("arbitrary"),
    )(oa, ob, x2, g1, sc2, sh2, ga, gb, gpost, gpre2, woa, wob, wrh, wrl, br)


PACK_ROWS = 4
U32 = jnp.uint32
HIGH_HALF = 0xFFFF0000


def _pack_rows(ref, x):
    n = x.shape[0]
    bits = lax.bitcast_convert_type(x.astype(BF16).astype(F32), U32)
    half = PACK_ROWS * LANES
    for c in range(PACK_ROWS):
        lo = bits[:, c * LANES:(c + 1) * LANES] >> 16
        hi = bits[:, half + c * LANES:half + (c + 1) * LANES] & U32(HIGH_HALF)
        ref[pl.ds(c, n, stride=PACK_ROWS), :] = lo | hi


def _unpack_rows(ref):
    n = ref.shape[0] // PACK_ROWS
    words = [ref[pl.ds(c, n, stride=PACK_ROWS), :] for c in range(PACK_ROWS)]
    lo = [lax.bitcast_convert_type(w << 16, F32) for w in words]
    hi = [lax.bitcast_convert_type(w & U32(HIGH_HALF), F32) for w in words]
    return jnp.concatenate(lo + hi, axis=1).astype(BF16)


def _for_each_run_piece(rdst_ref, rlen_ref, tile, max_len, fn):
    n_bits = max_len.bit_length()

    def run(e, local):
        length = rlen_ref[tile * N_EXPERTS + e]
        dst = rdst_ref[tile * N_EXPERTS + e]
        for b in range(n_bits):
            size = 1 << b

            @pl.when(((length >> b) & 1) == 1)
            def _():
                done = length & (size - 1)
                fn(local + done, dst + done, size)
        return local + length

    lax.fori_loop(0, N_EXPERTS, run, 0)


def _token_rows(ref, row0, n_rows):
    start = row0 * PACK_ROWS
    if not isinstance(start, int):
        start = pl.multiple_of(start, PACK_ROWS)
    return ref.at[pl.ds(start, n_rows * PACK_ROWS)]


def _dispatch_kernel(rdst_ref, rlen_ref, ends_ref, nv_ref, h_ref, rt_ref, xs_ref,
                     sorted_ref, zero_ref, sem, zsem, *, te, n_tiles):
    k = pl.program_id(0)
    nk = pl.num_programs(0)
    tm = h_ref.shape[0]
    rows = 2 * tm
    slot = k % 2

    def wait_slot(s):
        pltpu.make_async_copy(sorted_ref.at[s], _token_rows(xs_ref, 0, rows), sem.at[s]).wait()

    @pl.when(k == 0)
    def _():
        zero_ref[...] = jnp.zeros_like(zero_ref)

        def pad_copy(row0):
            return pltpu.make_async_copy(zero_ref, _token_rows(xs_ref, row0, te), zsem)

        def for_each_pad_tile(fn):
            def expert_pad(e, carry):
                end = ends_ref[e]
                prev = jnp.where(e > 0, ends_ref[jnp.maximum(e - 1, 0)], 0)

                @pl.when(end > prev)
                def _():
                    fn(pad_copy(end - te))
                return carry

            def tail_pad(j, carry):
                fn(pad_copy(j * te))
                return carry

            lax.fori_loop(0, N_EXPERTS, expert_pad, 0)
            lax.fori_loop(nv_ref[0], n_tiles, tail_pad, 0)

        for_each_pad_tile(lambda cp: cp.start())
        for_each_pad_tile(lambda cp: cp.wait())

    @pl.when(k >= 2)
    def _():
        wait_slot(slot)

    pos0 = rt_ref[2:3, :]
    pos1 = rt_ref[3:4, :]
    r = lax.broadcasted_iota(jnp.int32, (rows, tm), 0).astype(F32)
    perm = jnp.where((r == pos0) | (r == pos1), 1.0, 0.0).astype(BF16)
    srt = _dot(perm, h_ref[...].astype(BF16))
    buf = sorted_ref.at[slot]
    _pack_rows(buf, srt)

    def copy_piece(local, dst, size):
        pltpu.make_async_copy(_token_rows(buf, local, size), _token_rows(xs_ref, dst, size),
                              sem.at[slot]).start()

    _for_each_run_piece(rdst_ref, rlen_ref, k, tm, copy_piece)

    @pl.when(k == nk - 1)
    def _():
        wait_slot(slot)

        @pl.when(nk >= 2)
        def _():
            wait_slot(1 - slot)


def _dispatch(run_dst, run_len, ends, n_valid, h2, rt, n_rows):
    t, d = h2.shape
    assert d == 2 * PACK_ROWS * LANES
    tm = MOVE_TM
    te = EXPERT_TE
    return pl.pallas_call(
        functools.partial(_dispatch_kernel, te=te, n_tiles=n_rows // te),
        grid_spec=pltpu.PrefetchScalarGridSpec(
            num_scalar_prefetch=4,
            grid=(t // tm,),
            in_specs=[pl.BlockSpec((tm, d), lambda i, *_: (i, 0)),
                      pl.BlockSpec((8, tm), lambda i, *_: (0, i))],
            out_specs=pl.BlockSpec(memory_space=pl.ANY),
            scratch_shapes=[pltpu.VMEM((2, 2 * tm * PACK_ROWS, LANES), U32),
                            pltpu.VMEM((te * PACK_ROWS, LANES), U32),
                            pltpu.SemaphoreType.DMA((2,)), pltpu.SemaphoreType.DMA(())]),
        out_shape=jax.ShapeDtypeStruct((n_rows * PACK_ROWS, LANES), U32),
        compiler_params=_params("arbitrary"),
    )(run_dst, run_len, ends, n_valid, h2, rt)


def _expert_kernel(te_ref, nv_ref, xs_ref, wg_ref, wu_ref, wd_ref, ys_ref, wg_bf, wu_bf, wd_bf):
    j = pl.program_id(0)
    valid = j < nv_ref[0]
    changed = (j == 0) | (te_ref[j] != te_ref[jnp.maximum(j - 1, 0)])

    @pl.when(valid & changed)
    def _():
        wg_bf[...] = wg_ref[0].astype(BF16)
        wu_bf[...] = wu_ref[0].astype(BF16)
        wd_bf[...] = wd_ref[0].astype(BF16)

    @pl.when(valid)
    def _():
        xb = _unpack_rows(xs_ref)
        gate = _dot(xb, wg_bf[...])
        up = _dot(xb, wu_bf[...])
        act = gate * jax.nn.sigmoid(gate) * up
        _pack_rows(ys_ref, _dot(act.astype(BF16), wd_bf[...]))

    @pl.when(jnp.logical_not(valid))
    def _():
        ys_ref[...] = jnp.zeros_like(ys_ref)


def _expert_mlp(tile_expert, n_valid, xs, w_gate, w_up, w_down):
    te = EXPERT_TE
    d, ff = w_gate.shape[1:]
    n_rows = xs.shape[0] // PACK_ROWS
    blk = (te * PACK_ROWS, LANES)
    tile = lambda j, e, nv: (jnp.minimum(j, nv[0] - 1), 0)
    wsel = lambda j, e, nv: (e[j], 0, 0)
    return pl.pallas_call(
        _expert_kernel,
        grid_spec=pltpu.PrefetchScalarGridSpec(
            num_scalar_prefetch=2,
            grid=(n_rows // te,),
            in_specs=[pl.BlockSpec(blk, tile),
                      pl.BlockSpec((1, d, ff), wsel), pl.BlockSpec((1, d, ff), wsel),
                      pl.BlockSpec((1, ff, d), wsel)],
            out_specs=pl.BlockSpec(blk, lambda j, e, nv: (j, 0)),
            scratch_shapes=[pltpu.VMEM((d, ff), BF16), pltpu.VMEM((d, ff), BF16),
                            pltpu.VMEM((ff, d), BF16)]),
        out_shape=jax.ShapeDtypeStruct(xs.shape, U32),
        compiler_params=_params("arbitrary"),
    )(tile_expert, n_valid, xs, w_gate, w_up, w_down)


def _combine_kernel(rdst_ref, rlen_ref, x1_ref, rinfo_ref, g2_ref, gpost_ref, ys_ref, o_ref,
                    gath_ref, sem):
    k = pl.program_id(0)
    nk = pl.num_programs(0)
    tm = x1_ref.shape[0]
    rows = 2 * tm
    slot = k % 2

    def gather_runs(tile, s):
        buf = gath_ref.at[s]

        def copy_piece(local, src, size):
            pltpu.make_async_copy(_token_rows(ys_ref, src, size), _token_rows(buf, local, size),
                                  sem.at[s]).start()

        _for_each_run_piece(rdst_ref, rlen_ref, tile, tm, copy_piece)

    @pl.when(k == 0)
    def _():
        gather_runs(0, 0)

    @pl.when(k + 1 < nk)
    def _():
        gather_runs(k + 1, 1 - slot)

    buf = gath_ref.at[slot]
    pltpu.make_async_copy(_token_rows(ys_ref, 0, rows), buf, sem.at[slot]).wait()
    g = _unpack_rows(buf)
    info = rinfo_ref[...]
    col = lax.broadcasted_iota(jnp.int32, (tm, rows), 1).astype(F32)
    pick0 = jnp.where(col == info[:, 2:3], 1.0, 0.0).astype(BF16)
    pick1 = jnp.where(col == info[:, 3:4], 1.0, 0.0).astype(BF16)
    fx = info[:, 4:5] * _dot(pick0, g) + info[:, 5:6] * _dot(pick1, g)
    o_ref[...] = x1_ref[...] + g2_ref[0] * (_rms(fx) * gpost_ref[...])


def _combine(run_dst, run_len, x1, rinfo, g2, gpost, ys, seq):
    t, d = x1.shape
    tm = MOVE_TM
    tpb = seq // tm
    return pl.pallas_call(
        _combine_kernel,
        grid_spec=pltpu.PrefetchScalarGridSpec(
            num_scalar_prefetch=2,
            grid=(t // tm,),
            in_specs=[pl.BlockSpec((tm, d), lambda i, *_: (i, 0)),
                      pl.BlockSpec((tm, LANES), lambda i, *_: (i, 0)),
                      pl.BlockSpec((1, 1, d), lambda i, *_: (i // tpb, 0, 0)),
                      pl.BlockSpec((1, d), lambda i, *_: (0, 0)),
                      pl.BlockSpec(memory_space=pl.ANY)],
            out_specs=pl.BlockSpec((tm, d), lambda i, *_: (i, 0)),
            scratch_shapes=[pltpu.VMEM((2, 2 * tm * PACK_ROWS, LANES), U32),
                            pltpu.SemaphoreType.DMA((2,))]),
        out_shape=jax.ShapeDtypeStruct((t, d), F32),
        compiler_params=_params("arbitrary"),
    )(run_dst, run_len, x1, rinfo, g2, gpost, ys)


def _rope_tables(seq):
    pos = jnp.arange(seq, dtype=jnp.int32)
    row = (pos // GRID_W).astype(F32)
    col = (pos % GRID_W).astype(F32)
    axis_dim = HEAD_DIM // 2
    inv_freq = ROPE_THETA ** (-jnp.arange(0, axis_dim, 2, dtype=F32) / axis_dim)
    ang = jnp.concatenate([row[:, None] * inv_freq, col[:, None] * inv_freq], axis=-1)
    pair = (jnp.arange(LANES) % HEAD_DIM) // 2
    cos = jnp.cos(ang)[:, pair]
    sin = jnp.sin(ang)[:, pair]
    even = (jnp.arange(LANES) % 2) == 0
    return cos, jnp.where(even, -sin, 0.0), jnp.where(even, 0.0, sin)


def _segment_ones(n):
    seg = jnp.arange(n) // HEAD_DIM
    return (seg[:, None] == seg[None, :]).astype(BF16)


def kernel(x, c, ctx, c_ctx, w_mod, b_mod, attn_pre_norm, attn_post_norm, w_in, a_sink,
           b_q_norm, b_k_norm, a_out_norm, b_out_norm, w_out, ffn_pre_norm, ffn_post_norm,
           w_group, b_group, w_router, b_router, w_gate, w_up, w_down):
    batch, seq, d = x.shape
    ctx_len = ctx.shape[1]
    assert w_mod.shape[0] == 1, "single-layer stack only (context stream is never updated)"
    assert seq % ATTN_A_TQ == 0 and seq >= ATTN_A_TQ + 2 * WINDOW
    assert seq % PROJ_TM == 0 and seq % ATTN_B_TQ == 0 and seq % OUT_TM == 0 and seq % MOVE_TM == 0
    t = batch * seq
    nq = d // 2
    nkv = nq // KV_GROUP
    assert nkv == LANES and w_in.shape[2] == 2 * nq + 4 * nkv

    cc = jnp.concatenate([c, c_ctx[None, :], jnp.zeros((16 - batch - 1, d), F32)], axis=0)
    mod = _modulation(cc, w_mod[0], b_mod[0])
    sh1, sc1, g1, sh2, sc2, g2 = (m.reshape(batch, 1, d) for m in jnp.split(mod[:batch], 6, axis=-1))
    csh1, csc1 = (m.reshape(1, d) for m in jnp.split(mod[batch], 6)[:2])

    x2 = x.reshape(t, d)
    c2 = ctx.reshape(batch * ctx_len, d)
    gpre = attn_pre_norm[0].reshape(1, d)
    w_in_bf = w_in[0].astype(BF16)
    kv_cols = jnp.concatenate([w_in_bf[:, nq:nq + 2 * nkv], w_in_bf[:, 2 * nq + 2 * nkv:]], axis=1)
    qn = jnp.tile(b_q_norm[0], nq // HEAD_DIM).reshape(1, nq)
    kn = jnp.tile(b_k_norm[0], nkv // HEAD_DIM).reshape(1, nkv)
    seg_q, seg_k = _segment_ones(nq), _segment_ones(nkv)
    qa, ka4, va4, qbt, kb, vbt = _project_latents(
        x2, sc1, sh1, gpre, w_in_bf, _rope_tables(seq), qn, kn, seg_q, seg_k, seq)
    kca4, vca4, kcb, vcbt = _project_context(c2, csc1, csh1, gpre, kv_cols, kn, seg_k, ctx_len)

    oa = _attention_a(a_sink[0], qa, ka4, va4, kca4, vca4, batch, seq, ctx_len)
    score_bound = (1.01 * HEAD_DIM ** 0.5 * LOG2_E
                   * jnp.max(jnp.abs(b_q_norm[0])) * jnp.max(jnp.abs(b_k_norm[0]))).reshape(1)
    ob = _attention_b(score_bound, qbt, kb, vbt, kcb, vcbt, batch, seq, ctx_len)

    w_out_bf = w_out[0].astype(BF16)
    w_r = jnp.zeros((d, LANES), F32)
    w_r = w_r.at[:, :N_EXPERTS].set(w_router[0]).at[:, N_EXPERTS:N_EXPERTS + N_GROUPS].set(w_group[0])
    w_r_hi = w_r.astype(BF16)
    w_r_lo = (w_r - w_r_hi.astype(F32)).astype(BF16)
    b_r = jnp.zeros((1, LANES), F32)
    b_r = b_r.at[0, :N_EXPERTS].set(b_router[0]).at[0, N_EXPERTS:N_EXPERTS + N_GROUPS].set(b_group[0])
    x1, h2, rinfo, rt, tcarry, tcnt, counts = _out_and_route(
        oa, ob, x2, g1, sc2, sh2, a_out_norm[0].reshape(1, nq), b_out_norm[0].reshape(1, nq),
        attn_post_norm[0].reshape(1, d), ffn_pre_norm[0].reshape(1, d),
        w_out_bf[:nq], w_out_bf[nq:], w_r_hi, w_r_lo, b_r, seq)

    te = EXPERT_TE
    n_tiles = -(-(2 * t + N_EXPERTS * (te - 1)) // te)
    n_rows = n_tiles * te
    cnt = counts[0, :N_EXPERTS].astype(jnp.int32)
    padded = ((cnt + te - 1) // te) * te
    ends = jnp.cumsum(padded)
    offs = ends - padded
    run_dst = (offs[None, :] + tcarry[:, 0, :N_EXPERTS].astype(jnp.int32)).reshape(-1)
    run_len = tcnt[:, 0, :N_EXPERTS].astype(jnp.int32).reshape(-1)
    n_valid = (ends[-1] // te).astype(jnp.int32).reshape(1)
    tile_start = jnp.arange(n_tiles, dtype=jnp.int32) * te
    tile_expert = jnp.sum(ends[None, :] <= tile_start[:, None], axis=1).astype(jnp.int32)
    last_expert = tile_expert[jnp.maximum(n_valid[0] - 1, 0)]
    tile_expert = jnp.where(tile_start < ends[-1], tile_expert, last_expert)

    xs = _dispatch(run_dst, run_len, ends.astype(jnp.int32), n_valid, h2, rt, n_rows)
    ys = _expert_mlp(tile_expert, n_valid, xs, w_gate[0], w_up[0], w_down[0])
    out = _combine(run_dst, run_len, x1, rinfo, g2, ffn_post_norm[0].reshape(1, d), ys, seq)
    return out.reshape(batch, seq, d)
```

```python
import functools

import jax
import jax.numpy as jnp
from jax import lax
from jax.experimental import pallas as pl
from jax.experimental.pallas import tpu as pltpu

F32 = jnp.float32
BF16 = jnp.bfloat16

GRID_W = 64
HEAD_DIM = 64
KV_GROUP = 4
WINDOW = 128
ROPE_THETA = 10000.0
N_GROUPS = 4
EXPERTS_PER_GROUP = 8
N_EXPERTS = N_GROUPS * EXPERTS_PER_GROUP
EPS = 1e-6
NEG_INF = -1e30
LOG2_E = 1.4426950408889634
SAFE_SOFTMAX_SHIFT = 40.0

LANES = 128
V7X_VMEM_LIMIT = 56 * 1024 * 1024

PROJ_TM = 512
ATTN_A_TQ = 512
ATTN_B_TQ = 256
OUT_TM = 512
EXPERT_TE = 512
MOVE_TM = OUT_TM


def _params(*sem):
    return pltpu.CompilerParams(dimension_semantics=sem, vmem_limit_bytes=V7X_VMEM_LIMIT)


def _dot(a, b):
    return jnp.dot(a, b, preferred_element_type=F32)


def _dot_nt(a, b):
    return lax.dot_general(a, b, (((1,), (1,)), ((), ())), preferred_element_type=F32)


def _rms(x):
    return x * lax.rsqrt(jnp.mean(x * x, axis=-1, keepdims=True) + EPS)


def _split_bf16(x):
    hi = x.astype(BF16)
    lo = (x - hi.astype(F32)).astype(BF16)
    return hi, lo


def _mod_kernel(c_ref, w_ref, b_ref, o_ref):
    cc = c_ref[...]
    s = cc * jax.nn.sigmoid(cc)
    s_hi, s_lo = _split_bf16(s)
    w_hi, w_lo = _split_bf16(w_ref[...])
    o_ref[...] = _dot(s_hi, w_hi) + _dot(s_lo, w_hi) + _dot(s_hi, w_lo) + b_ref[...]


def _modulation(cc, w_mod, b_mod):
    rows, d = cc.shape
    n = w_mod.shape[1]
    bn = 1024
    return pl.pallas_call(
        _mod_kernel,
        grid=(n // bn,),
        in_specs=[pl.BlockSpec((rows, d), lambda i: (0, 0)),
                  pl.BlockSpec((d, bn), lambda i: (0, i)),
                  pl.BlockSpec((1, bn), lambda i: (0, i))],
        out_specs=pl.BlockSpec((rows, bn), lambda i: (0, i)),
        out_shape=jax.ShapeDtypeStruct((rows, n), F32),
        compiler_params=_params("arbitrary"),
    )(cc, w_mod, b_mod.reshape(1, n))


def _rope(x, cos, sin_a, sin_b):
    return x * cos + pltpu.roll(x, LANES - 1, 1) * sin_a + pltpu.roll(x, 1, 1) * sin_b


def _head_norm(x, seg_ref, gain):
    ss = _dot((x * x).astype(BF16), seg_ref[...])
    return x * lax.rsqrt(ss * (1.0 / HEAD_DIM) + EPS) * gain


def _store_pair_variants(ref, t):
    lane = lax.broadcasted_iota(jnp.int32, t.shape, 1)
    lo = lane < HEAD_DIM
    sw = pltpu.roll(t, HEAD_DIM, 1)
    zero = jnp.zeros_like(t)
    ref[0] = jnp.where(lo, t, zero).astype(BF16)
    ref[1] = jnp.where(lo, zero, sw).astype(BF16)
    ref[2] = jnp.where(lo, sw, zero).astype(BF16)
    ref[3] = jnp.where(lo, zero, t).astype(BF16)


def _proj_kernel(x_ref, sc_ref, sh_ref, gpre_ref, w_ref, cos_ref, sa_ref, sb_ref,
                 qn_ref, kn_ref, seg_q_ref, seg_k_ref,
                 qa_ref, ka_ref, va_ref, qbt_ref, kb_ref, vbt_ref):
    h = _rms(x_ref[...]) * gpre_ref[...] * (1.0 + sc_ref[0]) + sh_ref[0]
    p = _dot(h.astype(BF16), w_ref[...])
    cos, sa, sb = cos_ref[...], sa_ref[...], sb_ref[...]
    nq = qa_ref.shape[1]
    q_scale = HEAD_DIM ** -0.5 * LOG2_E
    for c in range(nq // LANES):
        qa_ref[:, c * LANES:(c + 1) * LANES] = (
            _rope(p[:, c * LANES:(c + 1) * LANES], cos, sa, sb) * q_scale).astype(BF16)
    o = nq
    _store_pair_variants(ka_ref, _rope(p[:, o:o + LANES], cos, sa, sb))
    _store_pair_variants(va_ref, p[:, o + LANES:o + 2 * LANES])
    o += 2 * LANES
    qb = _head_norm(p[:, o:o + nq], seg_q_ref, qn_ref[...])
    for c in range(nq // LANES):
        qbt_ref[c * LANES:(c + 1) * LANES, :] = (
            _rope(qb[:, c * LANES:(c + 1) * LANES], cos, sa, sb) * q_scale).T.astype(BF16)
    o += nq
    kb = _head_norm(p[:, o:o + LANES], seg_k_ref, kn_ref[...])
    kb_ref[...] = _rope(kb, cos, sa, sb).astype(BF16)
    vbt_ref[...] = p[:, o + LANES:o + 2 * LANES].T.astype(BF16)


def _ctx_proj_kernel(x_ref, sc_ref, sh_ref, gpre_ref, w_ref, kn_ref, seg_k_ref,
                     ka_ref, va_ref, kb_ref, vbt_ref):
    h = _rms(x_ref[...]) * gpre_ref[...] * (1.0 + sc_ref[...]) + sh_ref[...]
    p = _dot(h.astype(BF16), w_ref[...])
    _store_pair_variants(ka_ref, p[:, 0:LANES])
    _store_pair_variants(va_ref, p[:, LANES:2 * LANES])
    kb_ref[...] = _head_norm(p[:, 2 * LANES:3 * LANES], seg_k_ref, kn_ref[...]).astype(BF16)
    vbt_ref[...] = p[:, 3 * LANES:4 * LANES].T.astype(BF16)


def _project_latents(x2, sc, sh, gpre, w_in, tables, qn, kn, seg_q, seg_k, seq):
    t, d = x2.shape
    tm = PROJ_TM
    tpb = seq // tm
    nq = seg_q.shape[0]
    const = lambda shape: pl.BlockSpec(shape, lambda i: (0,) * len(shape))
    per_batch = pl.BlockSpec((1, 1, d), lambda i: (i // tpb, 0, 0))
    table = pl.BlockSpec((tm, LANES), lambda i: (i % tpb, 0))
    k_spec = pl.BlockSpec((tm, LANES), lambda i: (i, 0))
    k_shape = jax.ShapeDtypeStruct((t, LANES), BF16)
    vt_spec = pl.BlockSpec((LANES, tm), lambda i: (0, i))
    vt_shape = jax.ShapeDtypeStruct((LANES, t), BF16)
    qt_spec = pl.BlockSpec((nq, tm), lambda i: (0, i))
    qt_shape = jax.ShapeDtypeStruct((nq, t), BF16)
    q_spec = pl.BlockSpec((tm, nq), lambda i: (i, 0))
    q_shape = jax.ShapeDtypeStruct((t, nq), BF16)
    kv4_spec = pl.BlockSpec((4, tm, LANES), lambda i: (0, i, 0))
    kv4_shape = jax.ShapeDtypeStruct((4, t, LANES), BF16)
    return pl.pallas_call(
        _proj_kernel,
        grid=(t // tm,),
        in_specs=[pl.BlockSpec((tm, d), lambda i: (i, 0)), per_batch, per_batch, const((1, d)),
                  const(w_in.shape), table, table, table,
                  const((1, nq)), const((1, LANES)), const(seg_q.shape), const(seg_k.shape)],
        out_specs=[q_spec, kv4_spec, kv4_spec, qt_spec, k_spec, vt_spec],
        out_shape=[q_shape, kv4_shape, kv4_shape, qt_shape, k_shape, vt_shape],
        compiler_params=_params("arbitrary"),
    )(x2, sc, sh, gpre, w_in, *tables, qn, kn, seg_q, seg_k)


def _project_context(c2, sc, sh, gpre, w_kv, kn, seg_k, ctx_len):
    t, d = c2.shape
    const = lambda shape: pl.BlockSpec(shape, lambda i: (0,) * len(shape))
    k_spec = pl.BlockSpec((ctx_len, LANES), lambda i: (i, 0))
    k_shape = jax.ShapeDtypeStruct((t, LANES), BF16)
    vt_spec = pl.BlockSpec((LANES, ctx_len), lambda i: (0, i))
    vt_shape = jax.ShapeDtypeStruct((LANES, t), BF16)
    kv4_spec = pl.BlockSpec((4, ctx_len, LANES), lambda i: (0, i, 0))
    kv4_shape = jax.ShapeDtypeStruct((4, t, LANES), BF16)
    return pl.pallas_call(
        _ctx_proj_kernel,
        grid=(t // ctx_len,),
        in_specs=[pl.BlockSpec((ctx_len, d), lambda i: (i, 0)), const((1, d)), const((1, d)),
                  const((1, d)), const(w_kv.shape), const((1, LANES)), const(seg_k.shape)],
        out_specs=[kv4_spec, kv4_spec, k_spec, vt_spec],
        out_shape=[kv4_shape, kv4_shape, k_shape, vt_shape],
        compiler_params=_params("arbitrary"),
    )(c2, sc, sh, gpre, w_kv, kn, seg_k)


def _attend_t(w, k, kc, vt, vct, shift=None):
    st = _dot(k, w)
    sct = _dot(kc, w)
    if shift is None:
        shift = jnp.maximum(jnp.max(st, axis=0, keepdims=True),
                            jnp.max(sct, axis=0, keepdims=True))
    pt = jnp.exp2(st - shift)
    pct = jnp.exp2(sct - shift)
    denom = jnp.sum(pt, axis=0, keepdims=True) + jnp.sum(pct, axis=0, keepdims=True)
    o2 = _dot(vt, pt.astype(BF16)) + _dot(vct, pct.astype(BF16))
    return o2, denom


def _all_heads_t(qt_ref, o_ref, attend):
    tq = qt_ref.shape[1]
    n_kv = LANES // HEAD_DIM
    zeros = jnp.zeros((HEAD_DIM, tq), BF16)
    outs = []
    for h in range(qt_ref.shape[0] // HEAD_DIM):
        g = h // KV_GROUP
        qh = qt_ref[h * HEAD_DIM:(h + 1) * HEAD_DIM, :]
        w = jnp.concatenate([zeros] * g + [qh] + [zeros] * (n_kv - 1 - g), axis=0)
        o2, denom = attend(h, w)
        outs.append(o2[g * HEAD_DIM:(g + 1) * HEAD_DIM, :] / denom)
    o_ref[...] = jnp.concatenate(outs, axis=0).T.astype(BF16)


def _all_heads_fused_t(qt_ref, o_ref, attend):
    tq = qt_ref.shape[1]
    n_heads = qt_ref.shape[0] // HEAD_DIM
    n_kv = LANES // HEAD_DIM
    rows = []
    for g in range(n_kv):
        heads = [qt_ref[h * HEAD_DIM:(h + 1) * HEAD_DIM, :] if h // KV_GROUP == g
                 else jnp.zeros((HEAD_DIM, tq), BF16) for h in range(n_heads)]
        rows.append(jnp.concatenate(heads, axis=1))
    w = jnp.concatenate(rows, axis=0)
    o2, denom = attend(0, w)
    o2 = o2 / denom
    outs = [o2[(h // KV_GROUP) * HEAD_DIM:(h // KV_GROUP + 1) * HEAD_DIM, h * tq:(h + 1) * tq]
            for h in range(n_heads)]
    o_ref[...] = jnp.concatenate(outs, axis=0).T.astype(BF16)


def _attn_b_kernel(bound_ref, qt_ref, k_ref, vt_ref, kc_ref, vct_ref, o_ref):
    k, kc, vt, vct = k_ref[...], kc_ref[...], vt_ref[...], vct_ref[...]
    bound = bound_ref[0]

    @pl.when(bound <= SAFE_SOFTMAX_SHIFT)
    def _():
        _all_heads_fused_t(qt_ref, o_ref, lambda h, w: _attend_t(w, k, kc, vt, vct, shift=bound))

    @pl.when(jnp.logical_not(bound <= SAFE_SOFTMAX_SHIFT))
    def _():
        _all_heads_t(qt_ref, o_ref, lambda h, w: _attend_t(w, k, kc, vt, vct))


def _attn_specs(nq, tq, seq, ctx_len):
    nqb = seq // tq
    return dict(
        qt=pl.BlockSpec((nq, tq), lambda b, i: (0, b * nqb + i)),
        k=pl.BlockSpec((seq, LANES), lambda b, i: (b, 0)),
        vt=pl.BlockSpec((LANES, seq), lambda b, i: (0, b)),
        kc=pl.BlockSpec((ctx_len, LANES), lambda b, i: (b, 0)),
        vct=pl.BlockSpec((LANES, ctx_len), lambda b, i: (0, b)),
        out=pl.BlockSpec((tq, nq), lambda b, i: (b * nqb + i, 0)))


def _attention_b(score_bound, qbt, kb, vbt, kcb, vcbt, batch, seq, ctx_len):
    nq, t = qbt.shape
    tq = ATTN_B_TQ
    sp = _attn_specs(nq, tq, seq, ctx_len)
    return pl.pallas_call(
        _attn_b_kernel,
        grid=(batch, seq // tq),
        in_specs=[pl.BlockSpec(memory_space=pltpu.SMEM),
                  sp["qt"], sp["k"], sp["vt"], sp["kc"], sp["vct"]],
        out_specs=sp["out"],
        out_shape=jax.ShapeDtypeStruct((t, nq), BF16),
        compiler_params=_params("arbitrary", "arbitrary"),
    )(score_bound, qbt, kb, vbt, kcb, vcbt)


def _attn_a_kernel(sink_ref, q_ref, k_ref, v_ref, kc_ref, vc_ref, o_ref, bias_ref, *, seq):
    i = pl.program_id(1)
    tq = q_ref.shape[0]
    win = tq + 2 * WINDOW
    start = pl.multiple_of(jnp.clip(i * tq - WINDOW, 0, seq - win), WINDOW)
    qpos = i * tq + lax.broadcasted_iota(jnp.int32, (tq, win), 0)
    kpos = start + lax.broadcasted_iota(jnp.int32, (tq, win), 1)
    bias_ref[...] = jnp.where(jnp.abs(kpos - qpos) <= WINDOW, 0.0, NEG_INF).astype(F32)
    n_pairs = q_ref.shape[1] // LANES
    pairs_per_kv = KV_GROUP // 2
    rowmax = lambda s: jnp.max(s, axis=-1, keepdims=True)
    rowsum = lambda s: jnp.sum(s, axis=-1, keepdims=True)
    for pair in range(n_pairs):
        g = pair // pairs_per_kv
        q2 = q_ref[:, pair * LANES:(pair + 1) * LANES]
        acc = None
        for half in range(2):
            kv = 2 * g + half
            sink = sink_ref[2 * pair + half] * LOG2_E
            s = _dot_nt(q2, k_ref[kv, pl.ds(start, win), :]) + bias_ref[...]
            sc = _dot_nt(q2, kc_ref[kv])
            m = jnp.maximum(jnp.maximum(rowmax(s), rowmax(sc)), sink)
            p = jnp.exp2(s - m)
            pc = jnp.exp2(sc - m)
            denom = rowsum(p) + rowsum(pc) + jnp.exp2(sink - m)
            o = (_dot(p.astype(BF16), v_ref[kv, pl.ds(start, win), :])
                 + _dot(pc.astype(BF16), vc_ref[kv])) / denom
            acc = o if acc is None else acc + o
        o_ref[:, pair * LANES:(pair + 1) * LANES] = acc.astype(BF16)


def _attention_a(sink, qa, ka4, va4, kca4, vca4, batch, seq, ctx_len):
    t, nq = qa.shape
    tq = ATTN_A_TQ
    nqb = seq // tq
    lat = pl.BlockSpec((4, seq, LANES), lambda b, i: (0, b, 0))
    cx = pl.BlockSpec((4, ctx_len, LANES), lambda b, i: (0, b, 0))
    qs = pl.BlockSpec((tq, nq), lambda b, i: (b * nqb + i, 0))
    return pl.pallas_call(
        functools.partial(_attn_a_kernel, seq=seq),
        grid=(batch, nqb),
        in_specs=[pl.BlockSpec(memory_space=pltpu.SMEM), qs, lat, lat, cx, cx],
        out_specs=qs,
        out_shape=jax.ShapeDtypeStruct((t, nq), BF16),
        scratch_shapes=[pltpu.VMEM((tq, tq + 2 * WINDOW), F32)],
        compiler_params=_params("arbitrary", "arbitrary"),
    )(sink, qa, ka4, va4, kca4, vca4)


def _out_kernel(oa_ref, ob_ref, x_ref, g1_ref, sc2_ref, sh2_ref, ga_ref, gb_ref, gpost_ref,
                gpre2_ref, woa_ref, wob_ref, wrh_ref, wrl_ref, br_ref,
                x1_ref, h2_ref, rinfo_ref, rt_ref, tcarry_ref, tcnt_ref, cnt_ref, carry_ref):
    step = pl.program_id(0)

    @pl.when(step == 0)
    def _():
        carry_ref[...] = jnp.zeros_like(carry_ref)

    na = _rms(oa_ref[...].astype(F32)) * ga_ref[...]
    nb = _rms(ob_ref[...].astype(F32)) * gb_ref[...]
    ox = _dot(na.astype(BF16), woa_ref[...]) + _dot(nb.astype(BF16), wob_ref[...])
    x1 = x_ref[...] + g1_ref[0] * (_rms(ox) * gpost_ref[...])
    x1_ref[...] = x1
    h2 = _rms(x1) * gpre2_ref[...] * (1.0 + sc2_ref[0]) + sh2_ref[0]
    h_hi, h_lo = _split_bf16(h2)
    h2_ref[...] = h_hi

    logits = (_dot(h_hi, wrh_ref[...]) + _dot(h_lo, wrh_ref[...]) + _dot(h_hi, wrl_ref[...])
              + br_ref[...])
    tm = logits.shape[0]
    lane = lax.broadcasted_iota(jnp.int32, logits.shape, 1)
    lanef = lane.astype(F32)
    big = jnp.float32(1e9)
    ninf = jnp.float32(-jnp.inf)
    rowmax = lambda v: jnp.max(v, axis=-1, keepdims=True)
    rowmin = lambda v: jnp.min(v, axis=-1, keepdims=True)
    rowsum = lambda v: jnp.sum(v, axis=-1, keepdims=True)

    gmask = (lane >= N_EXPERTS) & (lane < N_EXPERTS + N_GROUPS)
    lg = jnp.where(gmask, logits, ninf)
    gmax = rowmax(lg)
    gidx = rowmin(jnp.where(lg == gmax, lanef, big)) - N_EXPERTS
    g_w = 1.0 / rowsum(jnp.exp(lg - gmax))
    lane_group = (lane // EXPERTS_PER_GROUP).astype(F32)
    emask = (lane < N_EXPERTS) & (lane_group == gidx)
    le = jnp.where(emask, logits, ninf)
    m1 = rowmax(le)
    i1 = rowmin(jnp.where(le == m1, lanef, big))
    le2 = jnp.where(lanef == i1, ninf, le)
    m2 = rowmax(le2)
    i2 = rowmin(jnp.where(le2 == m2, lanef, big))
    e2 = jnp.exp(m2 - m1)
    w0 = g_w / (1.0 + e2)
    w1 = g_w * e2 / (1.0 + e2)

    hit1 = lanef == i1
    hit2 = lanef == i2
    onehot = jnp.where(hit1 | hit2, 1.0, 0.0).astype(F32)
    r = lax.broadcasted_iota(jnp.int32, (tm, tm), 0)
    c = lax.broadcasted_iota(jnp.int32, (tm, tm), 1)
    strict_lower = jnp.where(r > c, 1.0, 0.0).astype(BF16)
    within = _dot(strict_lower, onehot.astype(BF16))
    tile_cnt = jnp.sum(onehot, axis=0, keepdims=True)
    incl = jnp.broadcast_to(tile_cnt, (8, LANES))
    lane8 = lax.broadcasted_iota(jnp.int32, (8, LANES), 1)
    shift = 1
    while shift < LANES:
        incl = incl + jnp.where(lane8 >= shift, pltpu.roll(incl, shift, 1), 0.0)
        shift *= 2
    local = within + (incl[0:1] - tile_cnt)
    pos0 = rowsum(jnp.where(hit1, local, 0.0))
    pos1 = rowsum(jnp.where(hit2, local, 0.0))
    tcarry_ref[0] = carry_ref[...]
    tcnt_ref[0] = tile_cnt
    carry_ref[...] += tile_cnt
    cnt_ref[...] = carry_ref[...]

    info = jnp.zeros_like(logits)
    for k, val in enumerate((i1, i2, pos0, pos1, w0, w1)):
        info = jnp.where(lane == k, val, info)
    rinfo_ref[...] = info
    rt_ref[...] = info.T[0:8, :]


def _out_and_route(oa, ob, x2, g1, sc2, sh2, ga, gb, gpost, gpre2, woa, wob, wrh, wrl, br, seq):
    t, d = x2.shape
    tm = OUT_TM
    tpb = seq // tm
    nq = oa.shape[1]
    const = lambda shape: pl.BlockSpec(shape, lambda i: (0,) * len(shape))
    per_batch = pl.BlockSpec((1, 1, d), lambda i: (i // tpb, 0, 0))
    rows = lambda n: pl.BlockSpec((tm, n), lambda i: (i, 0))
    per_tile = pl.BlockSpec((1, 1, LANES), lambda i: (i, 0, 0))
    return pl.pallas_call(
        _out_kernel,
        grid=(t // tm,),
        in_specs=[rows(nq), rows(nq), rows(d), per_batch, per_batch, per_batch,
                  const((1, nq)), const((1, nq)), const((1, d)), const((1, d)),
                  const(woa.shape), const(wob.shape), const(wrh.shape), const(wrl.shape),
                  const((1, LANES))],
        out_specs=[rows(d), rows(d), rows(LANES), pl.BlockSpec((8, tm), lambda i: (0, i)),
                   per_tile, per_tile, const((1, LANES))],
        out_shape=[jax.ShapeDtypeStruct((t, d), F32), jax.ShapeDtypeStruct((t, d), BF16),
                   jax.ShapeDtypeStruct((t, LANES), F32), jax.ShapeDtypeStruct((8, t), F32),
                   jax.ShapeDtypeStruct((t // tm, 1, LANES), F32),
                   jax.ShapeDtypeStruct((t // tm, 1, LANES), F32),
                   jax.ShapeDtypeStruct((1, LANES), F32)],
        scratch_shapes=[pltpu.VMEM((1, LANES), F32)],
        compiler_params=_params("arbitrary"),
    )(oa, ob, x2, g1, sc2, sh2, ga, gb, gpost, gpre2, woa, wob, wrh, wrl, br)


PACK_ROWS = 8
ROW_DTYPE = F32


def _pack_rows(ref, x):
    n = x.shape[0]
    for c in range(PACK_ROWS):
        ref[pl.ds(c, n, stride=PACK_ROWS), :] = x[:, c * LANES:(c + 1) * LANES]


def _unpack_rows(ref):
    n = ref.shape[0] // PACK_ROWS
    return jnp.concatenate(
        [ref[pl.ds(c, n, stride=PACK_ROWS), :].astype(BF16) for c in range(PACK_ROWS)], axis=1)


def _for_each_run_piece(rdst_ref, rlen_ref, tile, max_len, fn):
    n_bits = max_len.bit_length()

    def run(e, local):
        length = rlen_ref[tile * N_EXPERTS + e]
        dst = rdst_ref[tile * N_EXPERTS + e]
        for b in range(n_bits):
            size = 1 << b

            @pl.when(((length >> b) & 1) == 1)
            def _():
                done = length & (size - 1)
                fn(local + done, dst + done, size)
        return local + length

    lax.fori_loop(0, N_EXPERTS, run, 0)


def _token_rows(ref, row0, n_rows):
    start = row0 * PACK_ROWS
    if not isinstance(start, int):
        start = pl.multiple_of(start, PACK_ROWS)
    return ref.at[pl.ds(start, n_rows * PACK_ROWS)]


def _dispatch_kernel(rdst_ref, rlen_ref, ends_ref, nv_ref, h_ref, rt_ref, xs_ref,
                     sorted_ref, zero_ref, sem, zsem, *, te, n_tiles):
    k = pl.program_id(0)
    nk = pl.num_programs(0)
    tm = h_ref.shape[0]
    rows = 2 * tm
    slot = k % 2

    def wait_slot(s):
        pltpu.make_async_copy(sorted_ref.at[s], _token_rows(xs_ref, 0, rows), sem.at[s]).wait()

    @pl.when(k == 0)
    def _():
        zero_ref[...] = jnp.zeros_like(zero_ref)

        def pad_copy(row0):
            return pltpu.make_async_copy(zero_ref, _token_rows(xs_ref, row0, te), zsem)

        def for_each_pad_tile(fn):
            def expert_pad(e, carry):
                end = ends_ref[e]
                prev = jnp.where(e > 0, ends_ref[jnp.maximum(e - 1, 0)], 0)

                @pl.when(end > prev)
                def _():
                    fn(pad_copy(end - te))
                return carry

            def tail_pad(j, carry):
                fn(pad_copy(j * te))
                return carry

            lax.fori_loop(0, N_EXPERTS, expert_pad, 0)
            lax.fori_loop(nv_ref[0], n_tiles, tail_pad, 0)

        for_each_pad_tile(lambda cp: cp.start())
        for_each_pad_tile(lambda cp: cp.wait())

    @pl.when(k >= 2)
    def _():
        wait_slot(slot)

    pos0 = rt_ref[2:3, :]
    pos1 = rt_ref[3:4, :]
    r = lax.broadcasted_iota(jnp.int32, (rows, tm), 0).astype(F32)
    perm = jnp.where((r == pos0) | (r == pos1), 1.0, 0.0).astype(BF16)
    srt = _dot(perm, h_ref[...].astype(BF16))
    buf = sorted_ref.at[slot]
    _pack_rows(buf, srt)

    def copy_piece(local, dst, size):
        pltpu.make_async_copy(_token_rows(buf, local, size), _token_rows(xs_ref, dst, size),
                              sem.at[slot]).start()

    _for_each_run_piece(rdst_ref, rlen_ref, k, tm, copy_piece)

    @pl.when(k == nk - 1)
    def _():
        wait_slot(slot)

        @pl.when(nk >= 2)
        def _():
            wait_slot(1 - slot)


def _dispatch(run_dst, run_len, ends, n_valid, h2, rt, n_rows):
    t, d = h2.shape
    assert d == PACK_ROWS * LANES
    tm = MOVE_TM
    te = EXPERT_TE
    return pl.pallas_call(
        functools.partial(_dispatch_kernel, te=te, n_tiles=n_rows // te),
        grid_spec=pltpu.PrefetchScalarGridSpec(
            num_scalar_prefetch=4,
            grid=(t // tm,),
            in_specs=[pl.BlockSpec((tm, d), lambda i, *_: (i, 0)),
                      pl.BlockSpec((8, tm), lambda i, *_: (0, i))],
            out_specs=pl.BlockSpec(memory_space=pl.ANY),
            scratch_shapes=[pltpu.VMEM((2, 2 * tm * PACK_ROWS, LANES), ROW_DTYPE),
                            pltpu.VMEM((te * PACK_ROWS, LANES), ROW_DTYPE),
                            pltpu.SemaphoreType.DMA((2,)), pltpu.SemaphoreType.DMA(())]),
        out_shape=jax.ShapeDtypeStruct((n_rows * PACK_ROWS, LANES), ROW_DTYPE),
        compiler_params=_params("arbitrary"),
    )(run_dst, run_len, ends, n_valid, h2, rt)


def _expert_kernel(te_ref, nv_ref, xs_ref, wg_ref, wu_ref, wd_ref, ys_ref, wg_bf, wu_bf, wd_bf):
    j = pl.program_id(0)
    valid = j < nv_ref[0]
    changed = (j == 0) | (te_ref[j] != te_ref[jnp.maximum(j - 1, 0)])

    @pl.when(valid & changed)
    def _():
        wg_bf[...] = wg_ref[0].astype(BF16)
        wu_bf[...] = wu_ref[0].astype(BF16)
        wd_bf[...] = wd_ref[0].astype(BF16)

    @pl.when(valid)
    def _():
        xb = _unpack_rows(xs_ref)
        gate = _dot(xb, wg_bf[...])
        up = _dot(xb, wu_bf[...])
        act = gate * jax.nn.sigmoid(gate) * up
        _pack_rows(ys_ref, _dot(act.astype(BF16), wd_bf[...]))

    @pl.when(jnp.logical_not(valid))
    def _():
        ys_ref[...] = jnp.zeros_like(ys_ref)


def _expert_mlp(tile_expert, n_valid, xs, w_gate, w_up, w_down):
    te = EXPERT_TE
    d, ff = w_gate.shape[1:]
    n_rows = xs.shape[0] // PACK_ROWS
    blk = (te * PACK_ROWS, LANES)
    tile = lambda j, e, nv: (jnp.minimum(j, nv[0] - 1), 0)
    wsel = lambda j, e, nv: (e[j], 0, 0)
    return pl.pallas_call(
        _expert_kernel,
        grid_spec=pltpu.PrefetchScalarGridSpec(
            num_scalar_prefetch=2,
            grid=(n_rows // te,),
            in_specs=[pl.BlockSpec(blk, tile),
                      pl.BlockSpec((1, d, ff), wsel), pl.BlockSpec((1, d, ff), wsel),
                      pl.BlockSpec((1, ff, d), wsel)],
            out_specs=pl.BlockSpec(blk, lambda j, e, nv: (j, 0)),
            scratch_shapes=[pltpu.VMEM((d, ff), BF16), pltpu.VMEM((d, ff), BF16),
                            pltpu.VMEM((ff, d), BF16)]),
        out_shape=jax.ShapeDtypeStruct(xs.shape, ROW_DTYPE),
        compiler_params=_params("arbitrary"),
    )(tile_expert, n_valid, xs, w_gate, w_up, w_down)


def _combine_kernel(rdst_ref, rlen_ref, x1_ref, rinfo_ref, g2_ref, gpost_ref, ys_ref, o_ref,
                    gath_ref, sem):
    k = pl.program_id(0)
    nk = pl.num_programs(0)
    tm = x1_ref.shape[0]
    rows = 2 * tm
    slot = k % 2

    def gather_runs(tile, s):
        buf = gath_ref.at[s]

        def copy_piece(local, src, size):
            pltpu.make_async_copy(_token_rows(ys_ref, src, size), _token_rows(buf, local, size),
                                  sem.at[s]).start()

        _for_each_run_piece(rdst_ref, rlen_ref, tile, tm, copy_piece)

    @pl.when(k == 0)
    def _():
        gather_runs(0, 0)

    @pl.when(k + 1 < nk)
    def _():
        gather_runs(k + 1, 1 - slot)

    buf = gath_ref.at[slot]
    pltpu.make_async_copy(_token_rows(ys_ref, 0, rows), buf, sem.at[slot]).wait()
    g = _unpack_rows(buf)
    info = rinfo_ref[...]
    col = lax.broadcasted_iota(jnp.int32, (tm, rows), 1).astype(F32)
    pick0 = jnp.where(col == info[:, 2:3], 1.0, 0.0).astype(BF16)
    pick1 = jnp.where(col == info[:, 3:4], 1.0, 0.0).astype(BF16)
    fx = info[:, 4:5] * _dot(pick0, g) + info[:, 5:6] * _dot(pick1, g)
    o_ref[...] = x1_ref[...] + g2_ref[0] * (_rms(fx) * gpost_ref[...])


def _combine(run_dst, run_len, x1, rinfo, g2, gpost, ys, seq):
    t, d = x1.shape
    tm = MOVE_TM
    tpb = seq // tm
    return pl.pallas_call(
        _combine_kernel,
        grid_spec=pltpu.PrefetchScalarGridSpec(
            num_scalar_prefetch=2,
            grid=(t // tm,),
            in_specs=[pl.BlockSpec((tm, d), lambda i, *_: (i, 0)),
                      pl.BlockSpec((tm, LANES), lambda i, *_: (i, 0)),
                      pl.BlockSpec((1, 1, d), lambda i, *_: (i // tpb, 0, 0)),
                      pl.BlockSpec((1, d), lambda i, *_: (0, 0)),
                      pl.BlockSpec(memory_space=pl.ANY)],
            out_specs=pl.BlockSpec((tm, d), lambda i, *_: (i, 0)),
            scratch_shapes=[pltpu.VMEM((2, 2 * tm * PACK_ROWS, LANES), ROW_DTYPE),
                            pltpu.SemaphoreType.DMA((2,))]),
        out_shape=jax.ShapeDtypeStruct((t, d), F32),
        compiler_params=_params("arbitrary"),
    )(run_dst, run_len, x1, rinfo, g2, gpost, ys)


def _rope_tables(seq):
    pos = jnp.arange(seq, dtype=jnp.int32)
    row = (pos // GRID_W).astype(F32)
    col = (pos % GRID_W).astype(F32)
    axis_dim = HEAD_DIM // 2
    inv_freq = ROPE_THETA ** (-jnp.arange(0, axis_dim, 2, dtype=F32) / axis_dim)
    ang = jnp.concatenate([row[:, None] * inv_freq, col[:, None] * inv_freq], axis=-1)
    pair = (jnp.arange(LANES) % HEAD_DIM) // 2
    cos = jnp.cos(ang)[:, pair]
    sin = jnp.sin(ang)[:, pair]
    even = (jnp.arange(LANES) % 2) == 0
    return cos, jnp.where(even, -sin, 0.0), jnp.where(even, 0.0, sin)


def _segment_ones(n):
    seg = jnp.arange(n) // HEAD_DIM
    return (seg[:, None] == seg[None, :]).astype(BF16)


def kernel(x, c, ctx, c_ctx, w_mod, b_mod, attn_pre_norm, attn_post_norm, w_in, a_sink,
           b_q_norm, b_k_norm, a_out_norm, b_out_norm, w_out, ffn_pre_norm, ffn_post_norm,
           w_group, b_group, w_router, b_router, w_gate, w_up, w_down):
    batch, seq, d = x.shape
    ctx_len = ctx.shape[1]
    assert w_mod.shape[0] == 1, "single-layer stack only (context stream is never updated)"
    assert seq % ATTN_A_TQ == 0 and seq >= ATTN_A_TQ + 2 * WINDOW
    assert seq % PROJ_TM == 0 and seq % ATTN_B_TQ == 0 and seq % OUT_TM == 0 and seq % MOVE_TM == 0
    t = batch * seq
    nq = d // 2
    nkv = nq // KV_GROUP
    assert nkv == LANES and w_in.shape[2] == 2 * nq + 4 * nkv

    cc = jnp.concatenate([c, c_ctx[None, :], jnp.zeros((16 - batch - 1, d), F32)], axis=0)
    mod = _modulation(cc, w_mod[0], b_mod[0])
    sh1, sc1, g1, sh2, sc2, g2 = (m.reshape(batch, 1, d) for m in jnp.split(mod[:batch], 6, axis=-1))
    csh1, csc1 = (m.reshape(1, d) for m in jnp.split(mod[batch], 6)[:2])

    x2 = x.reshape(t, d)
    c2 = ctx.reshape(batch * ctx_len, d)
    gpre = attn_pre_norm[0].reshape(1, d)
    w_in_bf = w_in[0].astype(BF16)
    kv_cols = jnp.concatenate([w_in_bf[:, nq:nq + 2 * nkv], w_in_bf[:, 2 * nq + 2 * nkv:]], axis=1)
    qn = jnp.tile(b_q_norm[0], nq // HEAD_DIM).reshape(1, nq)
    kn = jnp.tile(b_k_norm[0], nkv // HEAD_DIM).reshape(1, nkv)
    seg_q, seg_k = _segment_ones(nq), _segment_ones(nkv)
    qa, ka4, va4, qbt, kb, vbt = _project_latents(
        x2, sc1, sh1, gpre, w_in_bf, _rope_tables(seq), qn, kn, seg_q, seg_k, seq)
    kca4, vca4, kcb, vcbt = _project_context(c2, csc1, csh1, gpre, kv_cols, kn, seg_k, ctx_len)

    oa = _attention_a(a_sink[0], qa, ka4, va4, kca4, vca4, batch, seq, ctx_len)
    score_bound = (1.01 * HEAD_DIM ** 0.5 * LOG2_E
                   * jnp.max(jnp.abs(b_q_norm[0])) * jnp.max(jnp.abs(b_k_norm[0]))).reshape(1)
    ob = _attention_b(score_bound, qbt, kb, vbt, kcb, vcbt, batch, seq, ctx_len)

    w_out_bf = w_out[0].astype(BF16)
    w_r = jnp.zeros((d, LANES), F32)
    w_r = w_r.at[:, :N_EXPERTS].set(w_router[0]).at[:, N_EXPERTS:N_EXPERTS + N_GROUPS].set(w_group[0])
    w_r_hi = w_r.astype(BF16)
    w_r_lo = (w_r - w_r_hi.astype(F32)).astype(BF16)
    b_r = jnp.zeros((1, LANES), F32)
    b_r = b_r.at[0, :N_EXPERTS].set(b_router[0]).at[0, N_EXPERTS:N_EXPERTS + N_GROUPS].set(b_group[0])
    x1, h2, rinfo, rt, tcarry, tcnt, counts = _out_and_route(
        oa, ob, x2, g1, sc2, sh2, a_out_norm[0].reshape(1, nq), b_out_norm[0].reshape(1, nq),
        attn_post_norm[0].reshape(1, d), ffn_pre_norm[0].reshape(1, d),
        w_out_bf[:nq], w_out_bf[nq:], w_r_hi, w_r_lo, b_r, seq)

    te = EXPERT_TE
    n_tiles = -(-(2 * t + N_EXPERTS * (te - 1)) // te)
    n_rows = n_tiles * te
    cnt = counts[0, :N_EXPERTS].astype(jnp.int32)
    padded = ((cnt + te - 1) // te) * te
    ends = jnp.cumsum(padded)
    offs = ends - padded
    run_dst = (offs[None, :] + tcarry[:, 0, :N_EXPERTS].astype(jnp.int32)).reshape(-1)
    run_len = tcnt[:, 0, :N_EXPERTS].astype(jnp.int32).reshape(-1)
    n_valid = (ends[-1] // te).astype(jnp.int32).reshape(1)
    tile_start = jnp.arange(n_tiles, dtype=jnp.int32) * te
    tile_expert = jnp.sum(ends[None, :] <= tile_start[:, None], axis=1).astype(jnp.int32)
    last_expert = tile_expert[jnp.maximum(n_valid[0] - 1, 0)]
    tile_expert = jnp.where(tile_start < ends[-1], tile_expert, last_expert)

    xs = _dispatch(run_dst, run_len, ends.astype(jnp.int32), n_valid, h2, rt, n_rows)
    ys = _expert_mlp(tile_expert, n_valid, xs, w_gate[0], w_up[0], w_down[0])
    out = _combine(run_dst, run_len, x1, rinfo, g2, ffn_post_norm[0].reshape(1, d), ys, seq)
    return out.reshape(batch, seq, d)
```

```python
import functools

import jax
import jax.numpy as jnp
from jax import lax
from jax.experimental import pallas as pl
from jax.experimental.pallas import tpu as pltpu

F32 = jnp.float32
BF16 = jnp.bfloat16

GRID_W = 64
HEAD_DIM = 64
KV_GROUP = 4
WINDOW = 128
ROPE_THETA = 10000.0
N_GROUPS = 4
EXPERTS_PER_GROUP = 8
N_EXPERTS = N_GROUPS * EXPERTS_PER_GROUP
EPS = 1e-6
NEG_INF = -1e30
LOG2_E = 1.4426950408889634
SAFE_SOFTMAX_SHIFT = 40.0

LANES = 128
V7X_VMEM_LIMIT = 56 * 1024 * 1024

PROJ_TM = 512
ATTN_A_TQ = 256
ATTN_B_TQ = 256
OUT_TM = 512
EXPERT_TE = 512
MOVE_TM = OUT_TM


def _params(*sem):
    return pltpu.CompilerParams(dimension_semantics=sem, vmem_limit_bytes=V7X_VMEM_LIMIT)


def _dot(a, b):
    return jnp.dot(a, b, preferred_element_type=F32)


def _dot_nt(a, b):
    return lax.dot_general(a, b, (((1,), (1,)), ((), ())), preferred_element_type=F32)


def _rms(x):
    return x * lax.rsqrt(jnp.mean(x * x, axis=-1, keepdims=True) + EPS)


def _split_bf16(x):
    hi = x.astype(BF16)
    lo = (x - hi.astype(F32)).astype(BF16)
    return hi, lo


def _mod_kernel(c_ref, w_ref, b_ref, o_ref):
    cc = c_ref[...]
    s = cc * jax.nn.sigmoid(cc)
    s_hi, s_lo = _split_bf16(s)
    w_hi, w_lo = _split_bf16(w_ref[...])
    o_ref[...] = _dot(s_hi, w_hi) + _dot(s_lo, w_hi) + _dot(s_hi, w_lo) + b_ref[...]


def _modulation(cc, w_mod, b_mod):
    rows, d = cc.shape
    n = w_mod.shape[1]
    bn = 1024
    return pl.pallas_call(
        _mod_kernel,
        grid=(n // bn,),
        in_specs=[pl.BlockSpec((rows, d), lambda i: (0, 0)),
                  pl.BlockSpec((d, bn), lambda i: (0, i)),
                  pl.BlockSpec((1, bn), lambda i: (0, i))],
        out_specs=pl.BlockSpec((rows, bn), lambda i: (0, i)),
        out_shape=jax.ShapeDtypeStruct((rows, n), F32),
        compiler_params=_params("arbitrary"),
    )(cc, w_mod, b_mod.reshape(1, n))


def _rope(x, cos, sin_a, sin_b):
    return x * cos + pltpu.roll(x, LANES - 1, 1) * sin_a + pltpu.roll(x, 1, 1) * sin_b


def _head_norm(x, seg_ref, gain):
    ss = _dot((x * x).astype(BF16), seg_ref[...])
    return x * lax.rsqrt(ss * (1.0 / HEAD_DIM) + EPS) * gain


def _proj_kernel(x_ref, sc_ref, sh_ref, gpre_ref, w_ref, cos_ref, sa_ref, sb_ref,
                 qn_ref, kn_ref, seg_q_ref, seg_k_ref,
                 qat_ref, ka_ref, vat_ref, qbt_ref, kb_ref, vbt_ref):
    h = _rms(x_ref[...]) * gpre_ref[...] * (1.0 + sc_ref[0]) + sh_ref[0]
    p = _dot(h.astype(BF16), w_ref[...])
    cos, sa, sb = cos_ref[...], sa_ref[...], sb_ref[...]
    nq = qat_ref.shape[0]
    q_scale = HEAD_DIM ** -0.5 * LOG2_E
    for c in range(nq // LANES):
        qat_ref[c * LANES:(c + 1) * LANES, :] = (
            _rope(p[:, c * LANES:(c + 1) * LANES], cos, sa, sb) * q_scale).T.astype(BF16)
    o = nq
    ka_ref[...] = _rope(p[:, o:o + LANES], cos, sa, sb).astype(BF16)
    vat_ref[...] = p[:, o + LANES:o + 2 * LANES].T.astype(BF16)
    o += 2 * LANES
    qb = _head_norm(p[:, o:o + nq], seg_q_ref, qn_ref[...])
    for c in range(nq // LANES):
        qbt_ref[c * LANES:(c + 1) * LANES, :] = (
            _rope(qb[:, c * LANES:(c + 1) * LANES], cos, sa, sb) * q_scale).T.astype(BF16)
    o += nq
    kb = _head_norm(p[:, o:o + LANES], seg_k_ref, kn_ref[...])
    kb_ref[...] = _rope(kb, cos, sa, sb).astype(BF16)
    vbt_ref[...] = p[:, o + LANES:o + 2 * LANES].T.astype(BF16)


def _ctx_proj_kernel(x_ref, sc_ref, sh_ref, gpre_ref, w_ref, kn_ref, seg_k_ref,
                     ka_ref, vat_ref, kb_ref, vbt_ref):
    h = _rms(x_ref[...]) * gpre_ref[...] * (1.0 + sc_ref[...]) + sh_ref[...]
    p = _dot(h.astype(BF16), w_ref[...])
    ka_ref[...] = p[:, 0:LANES].astype(BF16)
    vat_ref[...] = p[:, LANES:2 * LANES].T.astype(BF16)
    kb_ref[...] = _head_norm(p[:, 2 * LANES:3 * LANES], seg_k_ref, kn_ref[...]).astype(BF16)
    vbt_ref[...] = p[:, 3 * LANES:4 * LANES].T.astype(BF16)


def _project_latents(x2, sc, sh, gpre, w_in, tables, qn, kn, seg_q, seg_k, seq):
    t, d = x2.shape
    tm = PROJ_TM
    tpb = seq // tm
    nq = seg_q.shape[0]
    const = lambda shape: pl.BlockSpec(shape, lambda i: (0,) * len(shape))
    per_batch = pl.BlockSpec((1, 1, d), lambda i: (i // tpb, 0, 0))
    table = pl.BlockSpec((tm, LANES), lambda i: (i % tpb, 0))
    k_spec = pl.BlockSpec((tm, LANES), lambda i: (i, 0))
    k_shape = jax.ShapeDtypeStruct((t, LANES), BF16)
    vt_spec = pl.BlockSpec((LANES, tm), lambda i: (0, i))
    vt_shape = jax.ShapeDtypeStruct((LANES, t), BF16)
    qt_spec = pl.BlockSpec((nq, tm), lambda i: (0, i))
    qt_shape = jax.ShapeDtypeStruct((nq, t), BF16)
    return pl.pallas_call(
        _proj_kernel,
        grid=(t // tm,),
        in_specs=[pl.BlockSpec((tm, d), lambda i: (i, 0)), per_batch, per_batch, const((1, d)),
                  const(w_in.shape), table, table, table,
                  const((1, nq)), const((1, LANES)), const(seg_q.shape), const(seg_k.shape)],
        out_specs=[qt_spec, k_spec, vt_spec, qt_spec, k_spec, vt_spec],
        out_shape=[qt_shape, k_shape, vt_shape, qt_shape, k_shape, vt_shape],
        compiler_params=_params("arbitrary"),
    )(x2, sc, sh, gpre, w_in, *tables, qn, kn, seg_q, seg_k)


def _project_context(c2, sc, sh, gpre, w_kv, kn, seg_k, ctx_len):
    t, d = c2.shape
    const = lambda shape: pl.BlockSpec(shape, lambda i: (0,) * len(shape))
    k_spec = pl.BlockSpec((ctx_len, LANES), lambda i: (i, 0))
    k_shape = jax.ShapeDtypeStruct((t, LANES), BF16)
    vt_spec = pl.BlockSpec((LANES, ctx_len), lambda i: (0, i))
    vt_shape = jax.ShapeDtypeStruct((LANES, t), BF16)
    return pl.pallas_call(
        _ctx_proj_kernel,
        grid=(t // ctx_len,),
        in_specs=[pl.BlockSpec((ctx_len, d), lambda i: (i, 0)), const((1, d)), const((1, d)),
                  const((1, d)), const(w_kv.shape), const((1, LANES)), const(seg_k.shape)],
        out_specs=[k_spec, vt_spec, k_spec, vt_spec],
        out_shape=[k_shape, vt_shape, k_shape, vt_shape],
        compiler_params=_params("arbitrary"),
    )(c2, sc, sh, gpre, w_kv, kn, seg_k)


def _attend_t(w, k, kc, vt, vct, shift=None, bias=None, sink=None):
    st = _dot(k, w)
    sct = _dot(kc, w)
    if bias is not None:
        tq = bias.shape[1]
        st = jnp.concatenate([st[:, c * tq:(c + 1) * tq] + bias
                              for c in range(st.shape[1] // tq)], axis=1)
    if shift is None:
        shift = jnp.maximum(jnp.max(st, axis=0, keepdims=True),
                            jnp.max(sct, axis=0, keepdims=True))
        if sink is not None:
            shift = jnp.maximum(shift, sink)
    pt = jnp.exp2(st - shift)
    pct = jnp.exp2(sct - shift)
    denom = jnp.sum(pt, axis=0, keepdims=True) + jnp.sum(pct, axis=0, keepdims=True)
    if sink is not None:
        denom = denom + jnp.exp2(sink - shift)
    o2 = _dot(vt, pt.astype(BF16)) + _dot(vct, pct.astype(BF16))
    return o2, denom


def _all_heads_t(qt_ref, o_ref, attend):
    tq = qt_ref.shape[1]
    n_kv = LANES // HEAD_DIM
    zeros = jnp.zeros((HEAD_DIM, tq), BF16)
    outs = []
    for h in range(qt_ref.shape[0] // HEAD_DIM):
        g = h // KV_GROUP
        qh = qt_ref[h * HEAD_DIM:(h + 1) * HEAD_DIM, :]
        w = jnp.concatenate([zeros] * g + [qh] + [zeros] * (n_kv - 1 - g), axis=0)
        o2, denom = attend(h, w)
        outs.append(o2[g * HEAD_DIM:(g + 1) * HEAD_DIM, :] / denom)
    o_ref[...] = jnp.concatenate(outs, axis=0).T.astype(BF16)


def _all_heads_fused_t(qt_ref, o_ref, attend):
    tq = qt_ref.shape[1]
    n_heads = qt_ref.shape[0] // HEAD_DIM
    n_kv = LANES // HEAD_DIM
    rows = []
    for g in range(n_kv):
        heads = [qt_ref[h * HEAD_DIM:(h + 1) * HEAD_DIM, :] if h // KV_GROUP == g
                 else jnp.zeros((HEAD_DIM, tq), BF16) for h in range(n_heads)]
        rows.append(jnp.concatenate(heads, axis=1))
    w = jnp.concatenate(rows, axis=0)
    o2, denom = attend(0, w)
    o2 = o2 / denom
    outs = [o2[(h // KV_GROUP) * HEAD_DIM:(h // KV_GROUP + 1) * HEAD_DIM, h * tq:(h + 1) * tq]
            for h in range(n_heads)]
    o_ref[...] = jnp.concatenate(outs, axis=0).T.astype(BF16)


def _attn_b_kernel(bound_ref, qt_ref, k_ref, vt_ref, kc_ref, vct_ref, o_ref):
    k, kc, vt, vct = k_ref[...], kc_ref[...], vt_ref[...], vct_ref[...]
    bound = bound_ref[0]

    @pl.when(bound <= SAFE_SOFTMAX_SHIFT)
    def _():
        _all_heads_fused_t(qt_ref, o_ref, lambda h, w: _attend_t(w, k, kc, vt, vct, shift=bound))

    @pl.when(jnp.logical_not(bound <= SAFE_SOFTMAX_SHIFT))
    def _():
        _all_heads_t(qt_ref, o_ref, lambda h, w: _attend_t(w, k, kc, vt, vct))


def _attn_specs(nq, tq, seq, ctx_len):
    nqb = seq // tq
    return dict(
        qt=pl.BlockSpec((nq, tq), lambda b, i: (0, b * nqb + i)),
        k=pl.BlockSpec((seq, LANES), lambda b, i: (b, 0)),
        vt=pl.BlockSpec((LANES, seq), lambda b, i: (0, b)),
        kc=pl.BlockSpec((ctx_len, LANES), lambda b, i: (b, 0)),
        vct=pl.BlockSpec((LANES, ctx_len), lambda b, i: (0, b)),
        out=pl.BlockSpec((tq, nq), lambda b, i: (b * nqb + i, 0)))


def _attention_b(score_bound, qbt, kb, vbt, kcb, vcbt, batch, seq, ctx_len):
    nq, t = qbt.shape
    tq = ATTN_B_TQ
    sp = _attn_specs(nq, tq, seq, ctx_len)
    return pl.pallas_call(
        _attn_b_kernel,
        grid=(batch, seq // tq),
        in_specs=[pl.BlockSpec(memory_space=pltpu.SMEM),
                  sp["qt"], sp["k"], sp["vt"], sp["kc"], sp["vct"]],
        out_specs=sp["out"],
        out_shape=jax.ShapeDtypeStruct((t, nq), BF16),
        compiler_params=_params("arbitrary", "arbitrary"),
    )(score_bound, qbt, kb, vbt, kcb, vcbt)


def _max_head_sq_norm_rows(x):
    sq = x.astype(F32)
    sq = sq * sq
    lo = lax.broadcasted_iota(jnp.int32, sq.shape, 1) < HEAD_DIM
    a = jnp.sum(jnp.where(lo, sq, 0.0), axis=1, keepdims=True)
    b = jnp.sum(jnp.where(lo, 0.0, sq), axis=1, keepdims=True)
    return jnp.max(jnp.maximum(a, b), axis=0, keepdims=True)


def _attn_a_kernel(sink_ref, qt_ref, k_ref, vt_ref, kc_ref, vct_ref, o_ref, bias_ref, *, seq):
    i = pl.program_id(1)
    tq = qt_ref.shape[1]
    n_heads = qt_ref.shape[0] // HEAD_DIM
    win = tq + 2 * WINDOW
    start = pl.multiple_of(jnp.clip(i * tq - WINDOW, 0, seq - win), WINDOW)
    kpos = start + lax.broadcasted_iota(jnp.int32, (win, tq), 0)
    qpos = i * tq + lax.broadcasted_iota(jnp.int32, (win, tq), 1)
    bias_ref[...] = jnp.where(jnp.abs(kpos - qpos) <= WINDOW, 0.0, NEG_INF).astype(F32)
    k = k_ref[pl.ds(start, win), :]
    vt = vt_ref[:, pl.ds(start, win)]
    kc, vct = kc_ref[...], vct_ref[...]
    sinks = [sink_ref[h] * LOG2_E for h in range(n_heads)]

    qf = qt_ref[...].astype(F32)
    q_sq = jnp.sum((qf * qf).reshape(n_heads, HEAD_DIM, tq), axis=1)
    q_sq = jnp.max(jnp.max(q_sq, axis=1, keepdims=True), axis=0, keepdims=True)
    k_sq = jnp.maximum(_max_head_sq_norm_rows(k), _max_head_sq_norm_rows(kc))
    bound = 1.01 * jnp.sqrt(q_sq * k_sq)
    for s in sinks:
        bound = jnp.maximum(bound, s)
    small = bound[0, 0] <= SAFE_SOFTMAX_SHIFT

    @pl.when(small)
    def _():
        sink_row = jnp.concatenate([jnp.full((1, tq), s, F32) for s in sinks], axis=1)
        _all_heads_fused_t(qt_ref, o_ref, lambda h, w: _attend_t(
            w, k, kc, vt, vct, shift=bound, bias=bias_ref[...], sink=sink_row))

    @pl.when(jnp.logical_not(small))
    def _():
        _all_heads_t(qt_ref, o_ref, lambda h, w: _attend_t(
            w, k, kc, vt, vct, bias=bias_ref[...], sink=sinks[h]))


def _attention_a(sink, qat, ka, vat, kca, vcat, batch, seq, ctx_len):
    nq, t = qat.shape
    tq = ATTN_A_TQ
    sp = _attn_specs(nq, tq, seq, ctx_len)
    return pl.pallas_call(
        functools.partial(_attn_a_kernel, seq=seq),
        grid=(batch, seq // tq),
        in_specs=[pl.BlockSpec(memory_space=pltpu.SMEM),
                  sp["qt"], sp["k"], sp["vt"], sp["kc"], sp["vct"]],
        out_specs=sp["out"],
        out_shape=jax.ShapeDtypeStruct((t, nq), BF16),
        scratch_shapes=[pltpu.VMEM((tq + 2 * WINDOW, tq), F32)],
        compiler_params=_params("arbitrary", "arbitrary"),
    )(sink, qat, ka, vat, kca, vcat)


def _out_kernel(oa_ref, ob_ref, x_ref, g1_ref, sc2_ref, sh2_ref, ga_ref, gb_ref, gpost_ref,
                gpre2_ref, woa_ref, wob_ref, wrh_ref, wrl_ref, br_ref,
                x1_ref, h2_ref, rinfo_ref, rt_ref, tcarry_ref, tcnt_ref, cnt_ref, carry_ref):
    step = pl.program_id(0)

    @pl.when(step == 0)
    def _():
        carry_ref[...] = jnp.zeros_like(carry_ref)

    na = _rms(oa_ref[...].astype(F32)) * ga_ref[...]
    nb = _rms(ob_ref[...].astype(F32)) * gb_ref[...]
    ox = _dot(na.astype(BF16), woa_ref[...]) + _dot(nb.astype(BF16), wob_ref[...])
    x1 = x_ref[...] + g1_ref[0] * (_rms(ox) * gpost_ref[...])
    x1_ref[...] = x1
    h2 = _rms(x1) * gpre2_ref[...] * (1.0 + sc2_ref[0]) + sh2_ref[0]
    h_hi, h_lo = _split_bf16(h2)
    h2_ref[...] = h_hi

    logits = (_dot(h_hi, wrh_ref[...]) + _dot(h_lo, wrh_ref[...]) + _dot(h_hi, wrl_ref[...])
              + br_ref[...])
    tm = logits.shape[0]
    lane = lax.broadcasted_iota(jnp.int32, logits.shape, 1)
    lanef = lane.astype(F32)
    big = jnp.float32(1e9)
    ninf = jnp.float32(-jnp.inf)
    rowmax = lambda v: jnp.max(v, axis=-1, keepdims=True)
    rowmin = lambda v: jnp.min(v, axis=-1, keepdims=True)
    rowsum = lambda v: jnp.sum(v, axis=-1, keepdims=True)

    gmask = (lane >= N_EXPERTS) & (lane < N_EXPERTS + N_GROUPS)
    lg = jnp.where(gmask, logits, ninf)
    gmax = rowmax(lg)
    gidx = rowmin(jnp.where(lg == gmax, lanef, big)) - N_EXPERTS
    g_w = 1.0 / rowsum(jnp.exp(lg - gmax))
    lane_group = (lane // EXPERTS_PER_GROUP).astype(F32)
    emask = (lane < N_EXPERTS) & (lane_group == gidx)
    le = jnp.where(emask, logits, ninf)
    m1 = rowmax(le)
    i1 = rowmin(jnp.where(le == m1, lanef, big))
    le2 = jnp.where(lanef == i1, ninf, le)
    m2 = rowmax(le2)
    i2 = rowmin(jnp.where(le2 == m2, lanef, big))
    e2 = jnp.exp(m2 - m1)
    w0 = g_w / (1.0 + e2)
    w1 = g_w * e2 / (1.0 + e2)

    hit1 = lanef == i1
    hit2 = lanef == i2
    onehot = jnp.where(hit1 | hit2, 1.0, 0.0).astype(F32)
    r = lax.broadcasted_iota(jnp.int32, (tm, tm), 0)
    c = lax.broadcasted_iota(jnp.int32, (tm, tm), 1)
    strict_lower = jnp.where(r > c, 1.0, 0.0).astype(BF16)
    within = _dot(strict_lower, onehot.astype(BF16))
    tile_cnt = jnp.sum(onehot, axis=0, keepdims=True)
    incl = jnp.broadcast_to(tile_cnt, (8, LANES))
    lane8 = lax.broadcasted_iota(jnp.int32, (8, LANES), 1)
    shift = 1
    while shift < LANES:
        incl = incl + jnp.where(lane8 >= shift, pltpu.roll(incl, shift, 1), 0.0)
        shift *= 2
    local = within + (incl[0:1] - tile_cnt)
    pos0 = rowsum(jnp.where(hit1, local, 0.0))
    pos1 = rowsum(jnp.where(hit2, local, 0.0))
    tcarry_ref[0] = carry_ref[...]
    tcnt_ref[0] = tile_cnt
    carry_ref[...] += tile_cnt
    cnt_ref[...] = carry_ref[...]

    info = jnp.zeros_like(logits)
    for k, val in enumerate((i1, i2, pos0, pos1, w0, w1)):
        info = jnp.where(lane == k, val, info)
    rinfo_ref[...] = info
    rt_ref[...] = info.T[0:8, :]


def _out_and_route(oa, ob, x2, g1, sc2, sh2, ga, gb, gpost, gpre2, woa, wob, wrh, wrl, br, seq):
    t, d = x2.shape
    tm = OUT_TM
    tpb = seq // tm
    nq = oa.shape[1]
    const = lambda shape: pl.BlockSpec(shape, lambda i: (0,) * len(shape))
    per_batch = pl.BlockSpec((1, 1, d), lambda i: (i // tpb, 0, 0))
    rows = lambda n: pl.BlockSpec((tm, n), lambda i: (i, 0))
    per_tile = pl.BlockSpec((1, 1, LANES), lambda i: (i, 0, 0))
    return pl.pallas_call(
        _out_kernel,
        grid=(t // tm,),
        in_specs=[rows(nq), rows(nq), rows(d), per_batch, per_batch, per_batch,
                  const((1, nq)), const((1, nq)), const((1, d)), const((1, d)),
                  const(woa.shape), const(wob.shape), const(wrh.shape), const(wrl.shape),
                  const((1, LANES))],
        out_specs=[rows(d), rows(d), rows(LANES), pl.BlockSpec((8, tm), lambda i: (0, i)),
                   per_tile, per_tile, const((1, LANES))],
        out_shape=[jax.ShapeDtypeStruct((t, d), F32), jax.ShapeDtypeStruct((t, d), BF16),
                   jax.ShapeDtypeStruct((t, LANES), F32), jax.ShapeDtypeStruct((8, t), F32),
                   jax.ShapeDtypeStruct((t // tm, 1, LANES), F32),
                   jax.ShapeDtypeStruct((t // tm, 1, LANES), F32),
                   jax.ShapeDtypeStruct((1, LANES), F32)],
        scratch_shapes=[pltpu.VMEM((1, LANES), F32)],
        compiler_params=_params("arbitrary"),
    )(oa, ob, x2, g1, sc2, sh2, ga, gb, gpost, gpre2, woa, wob, wrh, wrl, br)


PACK_ROWS = 8
ROW_DTYPE = F32


def _pack_rows(ref, x):
    n = x.shape[0]
    for c in range(PACK_ROWS):
        ref[pl.ds(c, n, stride=PACK_ROWS), :] = x[:, c * LANES:(c + 1) * LANES]


def _unpack_rows(ref):
    n = ref.shape[0] // PACK_ROWS
    return jnp.concatenate(
        [ref[pl.ds(c, n, stride=PACK_ROWS), :].astype(BF16) for c in range(PACK_ROWS)], axis=1)


def _for_each_run_piece(rdst_ref, rlen_ref, tile, max_len, fn):
    n_bits = max_len.bit_length()

    def run(e, local):
        length = rlen_ref[tile * N_EXPERTS + e]
        dst = rdst_ref[tile * N_EXPERTS + e]
        for b in range(n_bits):
            size = 1 << b

            @pl.when(((length >> b) & 1) == 1)
            def _():
                done = length & (size - 1)
                fn(local + done, dst + done, size)
        return local + length

    lax.fori_loop(0, N_EXPERTS, run, 0)


def _token_rows(ref, row0, n_rows):
    start = row0 * PACK_ROWS
    if not isinstance(start, int):
        start = pl.multiple_of(start, PACK_ROWS)
    return ref.at[pl.ds(start, n_rows * PACK_ROWS)]


def _dispatch_kernel(rdst_ref, rlen_ref, ends_ref, nv_ref, h_ref, rt_ref, xs_ref,
                     sorted_ref, zero_ref, sem, zsem, *, te, n_tiles):
    k = pl.program_id(0)
    nk = pl.num_programs(0)
    tm = h_ref.shape[0]
    rows = 2 * tm
    slot = k % 2

    def wait_slot(s):
        pltpu.make_async_copy(sorted_ref.at[s], _token_rows(xs_ref, 0, rows), sem.at[s]).wait()

    @pl.when(k == 0)
    def _():
        zero_ref[...] = jnp.zeros_like(zero_ref)

        def pad_copy(row0):
            return pltpu.make_async_copy(zero_ref, _token_rows(xs_ref, row0, te), zsem)

        def for_each_pad_tile(fn):
            def expert_pad(e, carry):
                end = ends_ref[e]
                prev = jnp.where(e > 0, ends_ref[jnp.maximum(e - 1, 0)], 0)

                @pl.when(end > prev)
                def _():
                    fn(pad_copy(end - te))
                return carry

            def tail_pad(j, carry):
                fn(pad_copy(j * te))
                return carry

            lax.fori_loop(0, N_EXPERTS, expert_pad, 0)
            lax.fori_loop(nv_ref[0], n_tiles, tail_pad, 0)

        for_each_pad_tile(lambda cp: cp.start())
        for_each_pad_tile(lambda cp: cp.wait())

    @pl.when(k >= 2)
    def _():
        wait_slot(slot)

    pos0 = rt_ref[2:3, :]
    pos1 = rt_ref[3:4, :]
    r = lax.broadcasted_iota(jnp.int32, (rows, tm), 0).astype(F32)
    perm = jnp.where((r == pos0) | (r == pos1), 1.0, 0.0).astype(BF16)
    srt = _dot(perm, h_ref[...].astype(BF16))
    buf = sorted_ref.at[slot]
    _pack_rows(buf, srt)

    def copy_piece(local, dst, size):
        pltpu.make_async_copy(_token_rows(buf, local, size), _token_rows(xs_ref, dst, size),
                              sem.at[slot]).start()

    _for_each_run_piece(rdst_ref, rlen_ref, k, tm, copy_piece)

    @pl.when(k == nk - 1)
    def _():
        wait_slot(slot)

        @pl.when(nk >= 2)
        def _():
            wait_slot(1 - slot)


def _dispatch(run_dst, run_len, ends, n_valid, h2, rt, n_rows):
    t, d = h2.shape
    assert d == PACK_ROWS * LANES
    tm = MOVE_TM
    te = EXPERT_TE
    return pl.pallas_call(
        functools.partial(_dispatch_kernel, te=te, n_tiles=n_rows // te),
        grid_spec=pltpu.PrefetchScalarGridSpec(
            num_scalar_prefetch=4,
            grid=(t // tm,),
            in_specs=[pl.BlockSpec((tm, d), lambda i, *_: (i, 0)),
                      pl.BlockSpec((8, tm), lambda i, *_: (0, i))],
            out_specs=pl.BlockSpec(memory_space=pl.ANY),
            scratch_shapes=[pltpu.VMEM((2, 2 * tm * PACK_ROWS, LANES), ROW_DTYPE),
                            pltpu.VMEM((te * PACK_ROWS, LANES), ROW_DTYPE),
                            pltpu.SemaphoreType.DMA((2,)), pltpu.SemaphoreType.DMA(())]),
        out_shape=jax.ShapeDtypeStruct((n_rows * PACK_ROWS, LANES), ROW_DTYPE),
        compiler_params=_params("arbitrary"),
    )(run_dst, run_len, ends, n_valid, h2, rt)


def _expert_kernel(te_ref, nv_ref, xs_ref, wg_ref, wu_ref, wd_ref, ys_ref, wg_bf, wu_bf, wd_bf):
    j = pl.program_id(0)
    valid = j < nv_ref[0]
    changed = (j == 0) | (te_ref[j] != te_ref[jnp.maximum(j - 1, 0)])

    @pl.when(valid & changed)
    def _():
        wg_bf[...] = wg_ref[0].astype(BF16)
        wu_bf[...] = wu_ref[0].astype(BF16)
        wd_bf[...] = wd_ref[0].astype(BF16)

    @pl.when(valid)
    def _():
        xb = _unpack_rows(xs_ref)
        gate = _dot(xb, wg_bf[...])
        up = _dot(xb, wu_bf[...])
        act = gate * jax.nn.sigmoid(gate) * up
        _pack_rows(ys_ref, _dot(act.astype(BF16), wd_bf[...]))

    @pl.when(jnp.logical_not(valid))
    def _():
        ys_ref[...] = jnp.zeros_like(ys_ref)


def _expert_mlp(tile_expert, n_valid, xs, w_gate, w_up, w_down):
    te = EXPERT_TE
    d, ff = w_gate.shape[1:]
    n_rows = xs.shape[0] // PACK_ROWS
    blk = (te * PACK_ROWS, LANES)
    tile = lambda j, e, nv: (jnp.minimum(j, nv[0] - 1), 0)
    wsel = lambda j, e, nv: (e[j], 0, 0)
    return pl.pallas_call(
        _expert_kernel,
        grid_spec=pltpu.PrefetchScalarGridSpec(
            num_scalar_prefetch=2,
            grid=(n_rows // te,),
            in_specs=[pl.BlockSpec(blk, tile),
                      pl.BlockSpec((1, d, ff), wsel), pl.BlockSpec((1, d, ff), wsel),
                      pl.BlockSpec((1, ff, d), wsel)],
            out_specs=pl.BlockSpec(blk, lambda j, e, nv: (j, 0)),
            scratch_shapes=[pltpu.VMEM((d, ff), BF16), pltpu.VMEM((d, ff), BF16),
                            pltpu.VMEM((ff, d), BF16)]),
        out_shape=jax.ShapeDtypeStruct(xs.shape, ROW_DTYPE),
        compiler_params=_params("arbitrary"),
    )(tile_expert, n_valid, xs, w_gate, w_up, w_down)


def _combine_kernel(rdst_ref, rlen_ref, x1_ref, rinfo_ref, g2_ref, gpost_ref, ys_ref, o_ref,
                    gath_ref, sem):
    k = pl.program_id(0)
    nk = pl.num_programs(0)
    tm = x1_ref.shape[0]
    rows = 2 * tm
    slot = k % 2

    def gather_runs(tile, s):
        buf = gath_ref.at[s]

        def copy_piece(local, src, size):
            pltpu.make_async_copy(_token_rows(ys_ref, src, size), _token_rows(buf, local, size),
                                  sem.at[s]).start()

        _for_each_run_piece(rdst_ref, rlen_ref, tile, tm, copy_piece)

    @pl.when(k == 0)
    def _():
        gather_runs(0, 0)

    @pl.when(k + 1 < nk)
    def _():
        gather_runs(k + 1, 1 - slot)

    buf = gath_ref.at[slot]
    pltpu.make_async_copy(_token_rows(ys_ref, 0, rows), buf, sem.at[slot]).wait()
    g = _unpack_rows(buf)
    info = rinfo_ref[...]
    col = lax.broadcasted_iota(jnp.int32, (tm, rows), 1).astype(F32)
    pick0 = jnp.where(col == info[:, 2:3], 1.0, 0.0).astype(BF16)
    pick1 = jnp.where(col == info[:, 3:4], 1.0, 0.0).astype(BF16)
    fx = info[:, 4:5] * _dot(pick0, g) + info[:, 5:6] * _dot(pick1, g)
    o_ref[...] = x1_ref[...] + g2_ref[0] * (_rms(fx) * gpost_ref[...])


def _combine(run_dst, run_len, x1, rinfo, g2, gpost, ys, seq):
    t, d = x1.shape
    tm = MOVE_TM
    tpb = seq // tm
    return pl.pallas_call(
        _combine_kernel,
        grid_spec=pltpu.PrefetchScalarGridSpec(
            num_scalar_prefetch=2,
            grid=(t // tm,),
            in_specs=[pl.BlockSpec((tm, d), lambda i, *_: (i, 0)),
                      pl.BlockSpec((tm, LANES), lambda i, *_: (i, 0)),
                      pl.BlockSpec((1, 1, d), lambda i, *_: (i // tpb, 0, 0)),
                      pl.BlockSpec((1, d), lambda i, *_: (0, 0)),
                      pl.BlockSpec(memory_space=pl.ANY)],
            out_specs=pl.BlockSpec((tm, d), lambda i, *_: (i, 0)),
            scratch_shapes=[pltpu.VMEM((2, 2 * tm * PACK_ROWS, LANES), ROW_DTYPE),
                            pltpu.SemaphoreType.DMA((2,))]),
        out_shape=jax.ShapeDtypeStruct((t, d), F32),
        compiler_params=_params("arbitrary"),
    )(run_dst, run_len, x1, rinfo, g2, gpost, ys)


def _rope_tables(seq):
    pos = jnp.arange(seq, dtype=jnp.int32)
    row = (pos // GRID_W).astype(F32)
    col = (pos % GRID_W).astype(F32)
    axis_dim = HEAD_DIM // 2
    inv_freq = ROPE_THETA ** (-jnp.arange(0, axis_dim, 2, dtype=F32) / axis_dim)
    ang = jnp.concatenate([row[:, None] * inv_freq, col[:, None] * inv_freq], axis=-1)
    pair = (jnp.arange(LANES) % HEAD_DIM) // 2
    cos = jnp.cos(ang)[:, pair]
    sin = jnp.sin(ang)[:, pair]
    even = (jnp.arange(LANES) % 2) == 0
    return cos, jnp.where(even, -sin, 0.0), jnp.where(even, 0.0, sin)


def _segment_ones(n):
    seg = jnp.arange(n) // HEAD_DIM
    return (seg[:, None] == seg[None, :]).astype(BF16)


def kernel(x, c, ctx, c_ctx, w_mod, b_mod, attn_pre_norm, attn_post_norm, w_in, a_sink,
           b_q_norm, b_k_norm, a_out_norm, b_out_norm, w_out, ffn_pre_norm, ffn_post_norm,
           w_group, b_group, w_router, b_router, w_gate, w_up, w_down):
    batch, seq, d = x.shape
    ctx_len = ctx.shape[1]
    assert w_mod.shape[0] == 1, "single-layer stack only (context stream is never updated)"
    assert seq % ATTN_A_TQ == 0 and seq >= ATTN_A_TQ + 2 * WINDOW
    assert seq % PROJ_TM == 0 and seq % ATTN_B_TQ == 0 and seq % OUT_TM == 0 and seq % MOVE_TM == 0
    t = batch * seq
    nq = d // 2
    nkv = nq // KV_GROUP
    assert nkv == LANES and w_in.shape[2] == 2 * nq + 4 * nkv

    cc = jnp.concatenate([c, c_ctx[None, :], jnp.zeros((16 - batch - 1, d), F32)], axis=0)
    mod = _modulation(cc, w_mod[0], b_mod[0])
    sh1, sc1, g1, sh2, sc2, g2 = (m.reshape(batch, 1, d) for m in jnp.split(mod[:batch], 6, axis=-1))
    csh1, csc1 = (m.reshape(1, d) for m in jnp.split(mod[batch], 6)[:2])

    x2 = x.reshape(t, d)
    c2 = ctx.reshape(batch * ctx_len, d)
    gpre = attn_pre_norm[0].reshape(1, d)
    w_in_bf = w_in[0].astype(BF16)
    kv_cols = jnp.concatenate([w_in_bf[:, nq:nq + 2 * nkv], w_in_bf[:, 2 * nq + 2 * nkv:]], axis=1)
    qn = jnp.tile(b_q_norm[0], nq // HEAD_DIM).reshape(1, nq)
    kn = jnp.tile(b_k_norm[0], nkv // HEAD_DIM).reshape(1, nkv)
    seg_q, seg_k = _segment_ones(nq), _segment_ones(nkv)
    qat, ka, vat, qbt, kb, vbt = _project_latents(
        x2, sc1, sh1, gpre, w_in_bf, _rope_tables(seq), qn, kn, seg_q, seg_k, seq)
    kca, vcat, kcb, vcbt = _project_context(c2, csc1, csh1, gpre, kv_cols, kn, seg_k, ctx_len)

    oa = _attention_a(a_sink[0], qat, ka, vat, kca, vcat, batch, seq, ctx_len)
    score_bound = (1.01 * HEAD_DIM ** 0.5 * LOG2_E
                   * jnp.max(jnp.abs(b_q_norm[0])) * jnp.max(jnp.abs(b_k_norm[0]))).reshape(1)
    ob = _attention_b(score_bound, qbt, kb, vbt, kcb, vcbt, batch, seq, ctx_len)

    w_out_bf = w_out[0].astype(BF16)
    w_r = jnp.zeros((d, LANES), F32)
    w_r = w_r.at[:, :N_EXPERTS].set(w_router[0]).at[:, N_EXPERTS:N_EXPERTS + N_GROUPS].set(w_group[0])
    w_r_hi = w_r.astype(BF16)
    w_r_lo = (w_r - w_r_hi.astype(F32)).astype(BF16)
    b_r = jnp.zeros((1, LANES), F32)
    b_r = b_r.at[0, :N_EXPERTS].set(b_router[0]).at[0, N_EXPERTS:N_EXPERTS + N_GROUPS].set(b_group[0])
    x1, h2, rinfo, rt, tcarry, tcnt, counts = _out_and_route(
        oa, ob, x2, g1, sc2, sh2, a_out_norm[0].reshape(1, nq), b_out_norm[0].reshape(1, nq),
        attn_post_norm[0].reshape(1, d), ffn_pre_norm[0].reshape(1, d),
        w_out_bf[:nq], w_out_bf[nq:], w_r_hi, w_r_lo, b_r, seq)

    te = EXPERT_TE
    n_tiles = -(-(2 * t + N_EXPERTS * (te - 1)) // te)
    n_rows = n_tiles * te
    cnt = counts[0, :N_EXPERTS].astype(jnp.int32)
    padded = ((cnt + te - 1) // te) * te
    ends = jnp.cumsum(padded)
    offs = ends - padded
    run_dst = (offs[None, :] + tcarry[:, 0, :N_EXPERTS].astype(jnp.int32)).reshape(-1)
    run_len = tcnt[:, 0, :N_EXPERTS].astype(jnp.int32).reshape(-1)
    n_valid = (ends[-1] // te).astype(jnp.int32).reshape(1)
    tile_start = jnp.arange(n_tiles, dtype=jnp.int32) * te
    tile_expert = jnp.sum(ends[None, :] <= tile_start[:, None], axis=1).astype(jnp.int32)
    last_expert = tile_expert[jnp.maximum(n_valid[0] - 1, 0)]
    tile_expert = jnp.where(tile_start < ends[-1], tile_expert, last_expert)

    xs = _dispatch(run_dst, run_len, ends.astype(jnp.int32), n_valid, h2, rt, n_rows)
    ys = _expert_mlp(tile_expert, n_valid, xs, w_gate[0], w_up[0], w_down[0])
    out = _combine(run_dst, run_len, x1, rinfo, g2, ffn_post_norm[0].reshape(1, d), ys, seq)
    return out.reshape(batch, seq, d)
```

```python
import functools

import jax
import jax.numpy as jnp
from jax import lax
from jax.experimental import pallas as pl
from jax.experimental.pallas import tpu as pltpu

F32 = jnp.float32
BF16 = jnp.bfloat16

GRID_W = 64
HEAD_DIM = 64
KV_GROUP = 4
WINDOW = 128
ROPE_THETA = 10000.0
N_GROUPS = 4
EXPERTS_PER_GROUP = 8
N_EXPERTS = N_GROUPS * EXPERTS_PER_GROUP
EPS = 1e-6
NEG_INF = -1e30
LOG2_E = 1.4426950408889634
SAFE_SOFTMAX_SHIFT = 40.0

LANES = 128
V7X_VMEM_LIMIT = 56 * 1024 * 1024

PROJ_TM = 512
ATTN_A_TQ = 256
ATTN_B_TQ = 256
OUT_TM = 512
EXPERT_TE = 512
MOVE_TM = OUT_TM


def _params(*sem):
    return pltpu.CompilerParams(dimension_semantics=sem, vmem_limit_bytes=V7X_VMEM_LIMIT)


def _dot(a, b):
    return jnp.dot(a, b, preferred_element_type=F32)


def _dot_nt(a, b):
    return lax.dot_general(a, b, (((1,), (1,)), ((), ())), preferred_element_type=F32)


def _rms(x):
    return x * lax.rsqrt(jnp.mean(x * x, axis=-1, keepdims=True) + EPS)


def _split_bf16(x):
    hi = x.astype(BF16)
    lo = (x - hi.astype(F32)).astype(BF16)
    return hi, lo


def _mod_kernel(c_ref, w_ref, b_ref, o_ref):
    cc = c_ref[...]
    s = cc * jax.nn.sigmoid(cc)
    s_hi, s_lo = _split_bf16(s)
    w_hi, w_lo = _split_bf16(w_ref[...])
    o_ref[...] = _dot(s_hi, w_hi) + _dot(s_lo, w_hi) + _dot(s_hi, w_lo) + b_ref[...]


def _modulation(cc, w_mod, b_mod):
    rows, d = cc.shape
    n = w_mod.shape[1]
    bn = 1024
    return pl.pallas_call(
        _mod_kernel,
        grid=(n // bn,),
        in_specs=[pl.BlockSpec((rows, d), lambda i: (0, 0)),
                  pl.BlockSpec((d, bn), lambda i: (0, i)),
                  pl.BlockSpec((1, bn), lambda i: (0, i))],
        out_specs=pl.BlockSpec((rows, bn), lambda i: (0, i)),
        out_shape=jax.ShapeDtypeStruct((rows, n), F32),
        compiler_params=_params("arbitrary"),
    )(cc, w_mod, b_mod.reshape(1, n))


def _rope(x, cos, sin_a, sin_b):
    return x * cos + pltpu.roll(x, LANES - 1, 1) * sin_a + pltpu.roll(x, 1, 1) * sin_b


def _head_norm(x, seg_ref, gain):
    ss = _dot((x * x).astype(BF16), seg_ref[...])
    return x * lax.rsqrt(ss * (1.0 / HEAD_DIM) + EPS) * gain


def _proj_kernel(x_ref, sc_ref, sh_ref, gpre_ref, w_ref, cos_ref, sa_ref, sb_ref,
                 qn_ref, kn_ref, seg_q_ref, seg_k_ref,
                 qat_ref, ka_ref, vat_ref, qbt_ref, kb_ref, vbt_ref):
    h = _rms(x_ref[...]) * gpre_ref[...] * (1.0 + sc_ref[0]) + sh_ref[0]
    p = _dot(h.astype(BF16), w_ref[...])
    cos, sa, sb = cos_ref[...], sa_ref[...], sb_ref[...]
    nq = qat_ref.shape[0]
    q_scale = HEAD_DIM ** -0.5 * LOG2_E
    for c in range(nq // LANES):
        qat_ref[c * LANES:(c + 1) * LANES, :] = (
            _rope(p[:, c * LANES:(c + 1) * LANES], cos, sa, sb) * q_scale).T.astype(BF16)
    o = nq
    ka_ref[...] = _rope(p[:, o:o + LANES], cos, sa, sb).astype(BF16)
    vat_ref[...] = p[:, o + LANES:o + 2 * LANES].T.astype(BF16)
    o += 2 * LANES
    qb = _head_norm(p[:, o:o + nq], seg_q_ref, qn_ref[...])
    for c in range(nq // LANES):
        qbt_ref[c * LANES:(c + 1) * LANES, :] = (
            _rope(qb[:, c * LANES:(c + 1) * LANES], cos, sa, sb) * q_scale).T.astype(BF16)
    o += nq
    kb = _head_norm(p[:, o:o + LANES], seg_k_ref, kn_ref[...])
    kb_ref[...] = _rope(kb, cos, sa, sb).astype(BF16)
    vbt_ref[...] = p[:, o + LANES:o + 2 * LANES].T.astype(BF16)


def _ctx_proj_kernel(x_ref, sc_ref, sh_ref, gpre_ref, w_ref, kn_ref, seg_k_ref,
                     ka_ref, vat_ref, kb_ref, vbt_ref):
    h = _rms(x_ref[...]) * gpre_ref[...] * (1.0 + sc_ref[...]) + sh_ref[...]
    p = _dot(h.astype(BF16), w_ref[...])
    ka_ref[...] = p[:, 0:LANES].astype(BF16)
    vat_ref[...] = p[:, LANES:2 * LANES].T.astype(BF16)
    kb_ref[...] = _head_norm(p[:, 2 * LANES:3 * LANES], seg_k_ref, kn_ref[...]).astype(BF16)
    vbt_ref[...] = p[:, 3 * LANES:4 * LANES].T.astype(BF16)


def _project_latents(x2, sc, sh, gpre, w_in, tables, qn, kn, seg_q, seg_k, seq):
    t, d = x2.shape
    tm = PROJ_TM
    tpb = seq // tm
    nq = seg_q.shape[0]
    const = lambda shape: pl.BlockSpec(shape, lambda i: (0,) * len(shape))
    per_batch = pl.BlockSpec((1, 1, d), lambda i: (i // tpb, 0, 0))
    table = pl.BlockSpec((tm, LANES), lambda i: (i % tpb, 0))
    k_spec = pl.BlockSpec((tm, LANES), lambda i: (i, 0))
    k_shape = jax.ShapeDtypeStruct((t, LANES), BF16)
    vt_spec = pl.BlockSpec((LANES, tm), lambda i: (0, i))
    vt_shape = jax.ShapeDtypeStruct((LANES, t), BF16)
    qt_spec = pl.BlockSpec((nq, tm), lambda i: (0, i))
    qt_shape = jax.ShapeDtypeStruct((nq, t), BF16)
    return pl.pallas_call(
        _proj_kernel,
        grid=(t // tm,),
        in_specs=[pl.BlockSpec((tm, d), lambda i: (i, 0)), per_batch, per_batch, const((1, d)),
                  const(w_in.shape), table, table, table,
                  const((1, nq)), const((1, LANES)), const(seg_q.shape), const(seg_k.shape)],
        out_specs=[qt_spec, k_spec, vt_spec, qt_spec, k_spec, vt_spec],
        out_shape=[qt_shape, k_shape, vt_shape, qt_shape, k_shape, vt_shape],
        compiler_params=_params("arbitrary"),
    )(x2, sc, sh, gpre, w_in, *tables, qn, kn, seg_q, seg_k)


def _project_context(c2, sc, sh, gpre, w_kv, kn, seg_k, ctx_len):
    t, d = c2.shape
    const = lambda shape: pl.BlockSpec(shape, lambda i: (0,) * len(shape))
    k_spec = pl.BlockSpec((ctx_len, LANES), lambda i: (i, 0))
    k_shape = jax.ShapeDtypeStruct((t, LANES), BF16)
    vt_spec = pl.BlockSpec((LANES, ctx_len), lambda i: (0, i))
    vt_shape = jax.ShapeDtypeStruct((LANES, t), BF16)
    return pl.pallas_call(
        _ctx_proj_kernel,
        grid=(t // ctx_len,),
        in_specs=[pl.BlockSpec((ctx_len, d), lambda i: (i, 0)), const((1, d)), const((1, d)),
                  const((1, d)), const(w_kv.shape), const((1, LANES)), const(seg_k.shape)],
        out_specs=[k_spec, vt_spec, k_spec, vt_spec],
        out_shape=[k_shape, vt_shape, k_shape, vt_shape],
        compiler_params=_params("arbitrary"),
    )(c2, sc, sh, gpre, w_kv, kn, seg_k)


def _attend_t(w, k, kc, vt, vct, shift=None, bias=None, sink=None):
    st = _dot(k, w)
    sct = _dot(kc, w)
    if bias is not None:
        tq = bias.shape[1]
        st = jnp.concatenate([st[:, c * tq:(c + 1) * tq] + bias
                              for c in range(st.shape[1] // tq)], axis=1)
    if shift is None:
        shift = jnp.maximum(jnp.max(st, axis=0, keepdims=True),
                            jnp.max(sct, axis=0, keepdims=True))
        if sink is not None:
            shift = jnp.maximum(shift, sink)
    pt = jnp.exp2(st - shift)
    pct = jnp.exp2(sct - shift)
    denom = jnp.sum(pt, axis=0, keepdims=True) + jnp.sum(pct, axis=0, keepdims=True)
    if sink is not None:
        denom = denom + jnp.exp2(sink - shift)
    o2 = _dot(vt, pt.astype(BF16)) + _dot(vct, pct.astype(BF16))
    return o2, denom


def _all_heads_t(qt_ref, o_ref, attend):
    tq = qt_ref.shape[1]
    n_kv = LANES // HEAD_DIM
    zeros = jnp.zeros((HEAD_DIM, tq), BF16)
    outs = []
    for h in range(qt_ref.shape[0] // HEAD_DIM):
        g = h // KV_GROUP
        qh = qt_ref[h * HEAD_DIM:(h + 1) * HEAD_DIM, :]
        w = jnp.concatenate([zeros] * g + [qh] + [zeros] * (n_kv - 1 - g), axis=0)
        o2, denom = attend(h, w)
        outs.append(o2[g * HEAD_DIM:(g + 1) * HEAD_DIM, :] / denom)
    o_ref[...] = jnp.concatenate(outs, axis=0).T.astype(BF16)


def _all_heads_fused_t(qt_ref, o_ref, attend):
    tq = qt_ref.shape[1]
    n_heads = qt_ref.shape[0] // HEAD_DIM
    n_kv = LANES // HEAD_DIM
    rows = []
    for g in range(n_kv):
        heads = [qt_ref[h * HEAD_DIM:(h + 1) * HEAD_DIM, :] if h // KV_GROUP == g
                 else jnp.zeros((HEAD_DIM, tq), BF16) for h in range(n_heads)]
        rows.append(jnp.concatenate(heads, axis=1))
    w = jnp.concatenate(rows, axis=0)
    o2, denom = attend(0, w)
    o2 = o2 / denom
    outs = [o2[(h // KV_GROUP) * HEAD_DIM:(h // KV_GROUP + 1) * HEAD_DIM, h * tq:(h + 1) * tq]
            for h in range(n_heads)]
    o_ref[...] = jnp.concatenate(outs, axis=0).T.astype(BF16)


def _attn_b_kernel(bound_ref, qt_ref, k_ref, vt_ref, kc_ref, vct_ref, o_ref):
    k, kc, vt, vct = k_ref[...], kc_ref[...], vt_ref[...], vct_ref[...]
    bound = bound_ref[0]

    @pl.when(bound <= SAFE_SOFTMAX_SHIFT)
    def _():
        _all_heads_fused_t(qt_ref, o_ref, lambda h, w: _attend_t(w, k, kc, vt, vct, shift=bound))

    @pl.when(jnp.logical_not(bound <= SAFE_SOFTMAX_SHIFT))
    def _():
        _all_heads_t(qt_ref, o_ref, lambda h, w: _attend_t(w, k, kc, vt, vct))


def _attn_specs(nq, tq, seq, ctx_len):
    nqb = seq // tq
    return dict(
        qt=pl.BlockSpec((nq, tq), lambda b, i: (0, b * nqb + i)),
        k=pl.BlockSpec((seq, LANES), lambda b, i: (b, 0)),
        vt=pl.BlockSpec((LANES, seq), lambda b, i: (0, b)),
        kc=pl.BlockSpec((ctx_len, LANES), lambda b, i: (b, 0)),
        vct=pl.BlockSpec((LANES, ctx_len), lambda b, i: (0, b)),
        out=pl.BlockSpec((tq, nq), lambda b, i: (b * nqb + i, 0)))


def _attention_b(score_bound, qbt, kb, vbt, kcb, vcbt, batch, seq, ctx_len):
    nq, t = qbt.shape
    tq = ATTN_B_TQ
    sp = _attn_specs(nq, tq, seq, ctx_len)
    return pl.pallas_call(
        _attn_b_kernel,
        grid=(batch, seq // tq),
        in_specs=[pl.BlockSpec(memory_space=pltpu.SMEM),
                  sp["qt"], sp["k"], sp["vt"], sp["kc"], sp["vct"]],
        out_specs=sp["out"],
        out_shape=jax.ShapeDtypeStruct((t, nq), BF16),
        compiler_params=_params("arbitrary", "arbitrary"),
    )(score_bound, qbt, kb, vbt, kcb, vcbt)


def _max_head_sq_norm_rows(x):
    sq = x.astype(F32)
    sq = sq * sq
    lo = lax.broadcasted_iota(jnp.int32, sq.shape, 1) < HEAD_DIM
    a = jnp.sum(jnp.where(lo, sq, 0.0), axis=1, keepdims=True)
    b = jnp.sum(jnp.where(lo, 0.0, sq), axis=1, keepdims=True)
    return jnp.max(jnp.maximum(a, b), axis=0, keepdims=True)


def _attn_a_kernel(sink_ref, qt_ref, k_ref, vt_ref, kc_ref, vct_ref, bias_ref, o_ref, *, seq):
    i = pl.program_id(1)
    tq = qt_ref.shape[1]
    n_heads = qt_ref.shape[0] // HEAD_DIM
    win = tq + 2 * WINDOW
    start = pl.multiple_of(jnp.clip(i * tq - WINDOW, 0, seq - win), WINDOW)
    k = k_ref[pl.ds(start, win), :]
    vt = vt_ref[:, pl.ds(start, win)]
    kc, vct = kc_ref[...], vct_ref[...]
    sinks = [sink_ref[h] * LOG2_E for h in range(n_heads)]

    qf = qt_ref[...].astype(F32)
    q_sq = jnp.sum((qf * qf).reshape(n_heads, HEAD_DIM, tq), axis=1)
    q_sq = jnp.max(jnp.max(q_sq, axis=1, keepdims=True), axis=0, keepdims=True)
    k_sq = jnp.maximum(_max_head_sq_norm_rows(k), _max_head_sq_norm_rows(kc))
    bound = 1.01 * jnp.sqrt(q_sq * k_sq)
    for s in sinks:
        bound = jnp.maximum(bound, s)
    small = bound[0, 0] <= SAFE_SOFTMAX_SHIFT

    @pl.when(small)
    def _():
        sink_row = jnp.concatenate([jnp.full((1, tq), s, F32) for s in sinks], axis=1)
        _all_heads_fused_t(qt_ref, o_ref, lambda h, w: _attend_t(
            w, k, kc, vt, vct, shift=bound, bias=bias_ref[0], sink=sink_row))

    @pl.when(jnp.logical_not(small))
    def _():
        _all_heads_t(qt_ref, o_ref, lambda h, w: _attend_t(
            w, k, kc, vt, vct, bias=bias_ref[0], sink=sinks[h]))


def _band_bias(tq):
    win = tq + 2 * WINDOW
    r = jnp.arange(win, dtype=jnp.int32)[:, None]
    j = jnp.arange(tq, dtype=jnp.int32)[None, :]
    tables = [jnp.where(jnp.abs(off + r - j) <= WINDOW, 0.0, NEG_INF)
              for off in (0, -WINDOW, -2 * WINDOW)]
    return jnp.stack(tables).astype(F32)


def _attention_a(sink, qat, ka, vat, kca, vcat, batch, seq, ctx_len):
    nq, t = qat.shape
    tq = ATTN_A_TQ
    nqb = seq // tq
    win = tq + 2 * WINDOW
    assert nqb >= 2 and tq >= WINDOW
    sp = _attn_specs(nq, tq, seq, ctx_len)
    which = lambda b, i: (jnp.where(i == 0, 0, jnp.where(i == nqb - 1, 2, 1)), 0, 0)
    return pl.pallas_call(
        functools.partial(_attn_a_kernel, seq=seq),
        grid=(batch, nqb),
        in_specs=[pl.BlockSpec(memory_space=pltpu.SMEM),
                  sp["qt"], sp["k"], sp["vt"], sp["kc"], sp["vct"],
                  pl.BlockSpec((1, win, tq), which)],
        out_specs=sp["out"],
        out_shape=jax.ShapeDtypeStruct((t, nq), BF16),
        compiler_params=_params("arbitrary", "arbitrary"),
    )(sink, qat, ka, vat, kca, vcat, _band_bias(tq))


def _out_kernel(oa_ref, ob_ref, x_ref, g1_ref, sc2_ref, sh2_ref, ga_ref, gb_ref, gpost_ref,
                gpre2_ref, woa_ref, wob_ref, wrh_ref, wrl_ref, br_ref,
                x1_ref, h2_ref, rinfo_ref, rt_ref, tcarry_ref, tcnt_ref, cnt_ref, carry_ref):
    step = pl.program_id(0)

    @pl.when(step == 0)
    def _():
        carry_ref[...] = jnp.zeros_like(carry_ref)

    na = _rms(oa_ref[...].astype(F32)) * ga_ref[...]
    nb = _rms(ob_ref[...].astype(F32)) * gb_ref[...]
    ox = _dot(na.astype(BF16), woa_ref[...]) + _dot(nb.astype(BF16), wob_ref[...])
    x1 = x_ref[...] + g1_ref[0] * (_rms(ox) * gpost_ref[...])
    x1_ref[...] = x1
    h2 = _rms(x1) * gpre2_ref[...] * (1.0 + sc2_ref[0]) + sh2_ref[0]
    h_hi, h_lo = _split_bf16(h2)
    h2_ref[...] = h_hi

    logits = (_dot(h_hi, wrh_ref[...]) + _dot(h_lo, wrh_ref[...]) + _dot(h_hi, wrl_ref[...])
              + br_ref[...])
    tm = logits.shape[0]
    lane = lax.broadcasted_iota(jnp.int32, logits.shape, 1)
    lanef = lane.astype(F32)
    big = jnp.float32(1e9)
    ninf = jnp.float32(-jnp.inf)
    rowmax = lambda v: jnp.max(v, axis=-1, keepdims=True)
    rowmin = lambda v: jnp.min(v, axis=-1, keepdims=True)
    rowsum = lambda v: jnp.sum(v, axis=-1, keepdims=True)

    gmask = (lane >= N_EXPERTS) & (lane < N_EXPERTS + N_GROUPS)
    lg = jnp.where(gmask, logits, ninf)
    gmax = rowmax(lg)
    gidx = rowmin(jnp.where(lg == gmax, lanef, big)) - N_EXPERTS
    g_w = 1.0 / rowsum(jnp.exp(lg - gmax))
    lane_group = (lane // EXPERTS_PER_GROUP).astype(F32)
    emask = (lane < N_EXPERTS) & (lane_group == gidx)
    le = jnp.where(emask, logits, ninf)
    m1 = rowmax(le)
    i1 = rowmin(jnp.where(le == m1, lanef, big))
    le2 = jnp.where(lanef == i1, ninf, le)
    m2 = rowmax(le2)
    i2 = rowmin(jnp.where(le2 == m2, lanef, big))
    e2 = jnp.exp(m2 - m1)
    w0 = g_w / (1.0 + e2)
    w1 = g_w * e2 / (1.0 + e2)

    hit1 = lanef == i1
    hit2 = lanef == i2
    onehot = jnp.where(hit1 | hit2, 1.0, 0.0).astype(F32)
    r = lax.broadcasted_iota(jnp.int32, (tm, tm), 0)
    c = lax.broadcasted_iota(jnp.int32, (tm, tm), 1)
    strict_lower = jnp.where(r > c, 1.0, 0.0).astype(BF16)
    within = _dot(strict_lower, onehot.astype(BF16))
    tile_cnt = jnp.sum(onehot, axis=0, keepdims=True)
    incl = jnp.broadcast_to(tile_cnt, (8, LANES))
    lane8 = lax.broadcasted_iota(jnp.int32, (8, LANES), 1)
    shift = 1
    while shift < LANES:
        incl = incl + jnp.where(lane8 >= shift, pltpu.roll(incl, shift, 1), 0.0)
        shift *= 2
    local = within + (incl[0:1] - tile_cnt)
    pos0 = rowsum(jnp.where(hit1, local, 0.0))
    pos1 = rowsum(jnp.where(hit2, local, 0.0))
    tcarry_ref[0] = carry_ref[...]
    tcnt_ref[0] = tile_cnt
    carry_ref[...] += tile_cnt
    cnt_ref[...] = carry_ref[...]

    info = jnp.zeros_like(logits)
    for k, val in enumerate((i1, i2, pos0, pos1, w0, w1)):
        info = jnp.where(lane == k, val, info)
    rinfo_ref[...] = info
    rt_ref[...] = info.T[0:8, :]


def _out_and_route(oa, ob, x2, g1, sc2, sh2, ga, gb, gpost, gpre2, woa, wob, wrh, wrl, br, seq):
    t, d = x2.shape
    tm = OUT_TM
    tpb = seq // tm
    nq = oa.shape[1]
    const = lambda shape: pl.BlockSpec(shape, lambda i: (0,) * len(shape))
    per_batch = pl.BlockSpec((1, 1, d), lambda i: (i // tpb, 0, 0))
    rows = lambda n: pl.BlockSpec((tm, n), lambda i: (i, 0))
    per_tile = pl.BlockSpec((1, 1, LANES), lambda i: (i, 0, 0))
    return pl.pallas_call(
        _out_kernel,
        grid=(t // tm,),
        in_specs=[rows(nq), rows(nq), rows(d), per_batch, per_batch, per_batch,
                  const((1, nq)), const((1, nq)), const((1, d)), const((1, d)),
                  const(woa.shape), const(wob.shape), const(wrh.shape), const(wrl.shape),
                  const((1, LANES))],
        out_specs=[rows(d), rows(d), rows(LANES), pl.BlockSpec((8, tm), lambda i: (0, i)),
                   per_tile, per_tile, const((1, LANES))],
        out_shape=[jax.ShapeDtypeStruct((t, d), F32), jax.ShapeDtypeStruct((t, d), BF16),
                   jax.ShapeDtypeStruct((t, LANES), F32), jax.ShapeDtypeStruct((8, t), F32),
                   jax.ShapeDtypeStruct((t // tm, 1, LANES), F32),
                   jax.ShapeDtypeStruct((t // tm, 1, LANES), F32),
                   jax.ShapeDtypeStruct((1, LANES), F32)],
        scratch_shapes=[pltpu.VMEM((1, LANES), F32)],
        compiler_params=_params("arbitrary"),
    )(oa, ob, x2, g1, sc2, sh2, ga, gb, gpost, gpre2, woa, wob, wrh, wrl, br)


PACK_ROWS = 8
ROW_DTYPE = F32


def _pack_rows(ref, x):
    n = x.shape[0]
    for c in range(PACK_ROWS):
        ref[pl.ds(c, n, stride=PACK_ROWS), :] = x[:, c * LANES:(c + 1) * LANES]


def _unpack_rows(ref):
    n = ref.shape[0] // PACK_ROWS
    return jnp.concatenate(
        [ref[pl.ds(c, n, stride=PACK_ROWS), :].astype(BF16) for c in range(PACK_ROWS)], axis=1)


def _for_each_run_piece(rdst_ref, rlen_ref, tile, max_len, fn):
    n_bits = max_len.bit_length()

    def run(e, local):
        length = rlen_ref[tile * N_EXPERTS + e]
        dst = rdst_ref[tile * N_EXPERTS + e]
        for b in range(n_bits):
            size = 1 << b

            @pl.when(((length >> b) & 1) == 1)
            def _():
                done = length & (size - 1)
                fn(local + done, dst + done, size)
        return local + length

    lax.fori_loop(0, N_EXPERTS, run, 0)


def _token_rows(ref, row0, n_rows):
    start = row0 * PACK_ROWS
    if not isinstance(start, int):
        start = pl.multiple_of(start, PACK_ROWS)
    return ref.at[pl.ds(start, n_rows * PACK_ROWS)]


def _dispatch_kernel(rdst_ref, rlen_ref, ends_ref, nv_ref, h_ref, rt_ref, xs_ref,
                     sorted_ref, zero_ref, sem, zsem, *, te, n_tiles):
    k = pl.program_id(0)
    nk = pl.num_programs(0)
    tm = h_ref.shape[0]
    rows = 2 * tm
    slot = k % 2

    def wait_slot(s):
        pltpu.make_async_copy(sorted_ref.at[s], _token_rows(xs_ref, 0, rows), sem.at[s]).wait()

    @pl.when(k == 0)
    def _():
        zero_ref[...] = jnp.zeros_like(zero_ref)

        def pad_copy(row0):
            return pltpu.make_async_copy(zero_ref, _token_rows(xs_ref, row0, te), zsem)

        def for_each_pad_tile(fn):
            def expert_pad(e, carry):
                end = ends_ref[e]
                prev = jnp.where(e > 0, ends_ref[jnp.maximum(e - 1, 0)], 0)

                @pl.when(end > prev)
                def _():
                    fn(pad_copy(end - te))
                return carry

            def tail_pad(j, carry):
                fn(pad_copy(j * te))
                return carry

            lax.fori_loop(0, N_EXPERTS, expert_pad, 0)
            lax.fori_loop(nv_ref[0], n_tiles, tail_pad, 0)

        for_each_pad_tile(lambda cp: cp.start())
        for_each_pad_tile(lambda cp: cp.wait())

    @pl.when(k >= 2)
    def _():
        wait_slot(slot)

    pos0 = rt_ref[2:3, :]
    pos1 = rt_ref[3:4, :]
    r = lax.broadcasted_iota(jnp.int32, (rows, tm), 0).astype(F32)
    perm = jnp.where((r == pos0) | (r == pos1), 1.0, 0.0).astype(BF16)
    srt = _dot(perm, h_ref[...].astype(BF16))
    buf = sorted_ref.at[slot]
    _pack_rows(buf, srt)

    def copy_piece(local, dst, size):
        pltpu.make_async_copy(_token_rows(buf, local, size), _token_rows(xs_ref, dst, size),
                              sem.at[slot]).start()

    _for_each_run_piece(rdst_ref, rlen_ref, k, tm, copy_piece)

    @pl.when(k == nk - 1)
    def _():
        wait_slot(slot)

        @pl.when(nk >= 2)
        def _():
            wait_slot(1 - slot)


def _dispatch(run_dst, run_len, ends, n_valid, h2, rt, n_rows):
    t, d = h2.shape
    assert d == PACK_ROWS * LANES
    tm = MOVE_TM
    te = EXPERT_TE
    return pl.pallas_call(
        functools.partial(_dispatch_kernel, te=te, n_tiles=n_rows // te),
        grid_spec=pltpu.PrefetchScalarGridSpec(
            num_scalar_prefetch=4,
            grid=(t // tm,),
            in_specs=[pl.BlockSpec((tm, d), lambda i, *_: (i, 0)),
                      pl.BlockSpec((8, tm), lambda i, *_: (0, i))],
            out_specs=pl.BlockSpec(memory_space=pl.ANY),
            scratch_shapes=[pltpu.VMEM((2, 2 * tm * PACK_ROWS, LANES), ROW_DTYPE),
                            pltpu.VMEM((te * PACK_ROWS, LANES), ROW_DTYPE),
                            pltpu.SemaphoreType.DMA((2,)), pltpu.SemaphoreType.DMA(())]),
        out_shape=jax.ShapeDtypeStruct((n_rows * PACK_ROWS, LANES), ROW_DTYPE),
        compiler_params=_params("arbitrary"),
    )(run_dst, run_len, ends, n_valid, h2, rt)


def _expert_kernel(te_ref, nv_ref, xs_ref, wg_ref, wu_ref, wd_ref, ys_ref, wg_bf, wu_bf, wd_bf):
    j = pl.program_id(0)
    valid = j < nv_ref[0]
    changed = (j == 0) | (te_ref[j] != te_ref[jnp.maximum(j - 1, 0)])

    @pl.when(valid & changed)
    def _():
        wg_bf[...] = wg_ref[0].astype(BF16)
        wu_bf[...] = wu_ref[0].astype(BF16)
        wd_bf[...] = wd_ref[0].astype(BF16)

    @pl.when(valid)
    def _():
        xb = _unpack_rows(xs_ref)
        gate = _dot(xb, wg_bf[...])
        up = _dot(xb, wu_bf[...])
        act = gate * jax.nn.sigmoid(gate) * up
        _pack_rows(ys_ref, _dot(act.astype(BF16), wd_bf[...]))

    @pl.when(jnp.logical_not(valid))
    def _():
        ys_ref[...] = jnp.zeros_like(ys_ref)


def _expert_mlp(tile_expert, n_valid, xs, w_gate, w_up, w_down):
    te = EXPERT_TE
    d, ff = w_gate.shape[1:]
    n_rows = xs.shape[0] // PACK_ROWS
    blk = (te * PACK_ROWS, LANES)
    tile = lambda j, e, nv: (jnp.minimum(j, nv[0] - 1), 0)
    wsel = lambda j, e, nv: (e[j], 0, 0)
    return pl.pallas_call(
        _expert_kernel,
        grid_spec=pltpu.PrefetchScalarGridSpec(
            num_scalar_prefetch=2,
            grid=(n_rows // te,),
            in_specs=[pl.BlockSpec(blk, tile),
                      pl.BlockSpec((1, d, ff), wsel), pl.BlockSpec((1, d, ff), wsel),
                      pl.BlockSpec((1, ff, d), wsel)],
            out_specs=pl.BlockSpec(blk, lambda j, e, nv: (j, 0)),
            scratch_shapes=[pltpu.VMEM((d, ff), BF16), pltpu.VMEM((d, ff), BF16),
                            pltpu.VMEM((ff, d), BF16)]),
        out_shape=jax.ShapeDtypeStruct(xs.shape, ROW_DTYPE),
        compiler_params=_params("arbitrary"),
    )(tile_expert, n_valid, xs, w_gate, w_up, w_down)


def _combine_kernel(rdst_ref, rlen_ref, x1_ref, rinfo_ref, g2_ref, gpost_ref, ys_ref, o_ref,
                    gath_ref, sem):
    k = pl.program_id(0)
    nk = pl.num_programs(0)
    tm = x1_ref.shape[0]
    rows = 2 * tm
    slot = k % 2

    def gather_runs(tile, s):
        buf = gath_ref.at[s]

        def copy_piece(local, src, size):
            pltpu.make_async_copy(_token_rows(ys_ref, src, size), _token_rows(buf, local, size),
                                  sem.at[s]).start()

        _for_each_run_piece(rdst_ref, rlen_ref, tile, tm, copy_piece)

    @pl.when(k == 0)
    def _():
        gather_runs(0, 0)

    @pl.when(k + 1 < nk)
    def _():
        gather_runs(k + 1, 1 - slot)

    buf = gath_ref.at[slot]
    pltpu.make_async_copy(_token_rows(ys_ref, 0, rows), buf, sem.at[slot]).wait()
    g = _unpack_rows(buf)
    info = rinfo_ref[...]
    col = lax.broadcasted_iota(jnp.int32, (tm, rows), 1).astype(F32)
    pick0 = jnp.where(col == info[:, 2:3], 1.0, 0.0).astype(BF16)
    pick1 = jnp.where(col == info[:, 3:4], 1.0, 0.0).astype(BF16)
    fx = info[:, 4:5] * _dot(pick0, g) + info[:, 5:6] * _dot(pick1, g)
    o_ref[...] = x1_ref[...] + g2_ref[0] * (_rms(fx) * gpost_ref[...])


def _combine(run_dst, run_len, x1, rinfo, g2, gpost, ys, seq):
    t, d = x1.shape
    tm = MOVE_TM
    tpb = seq // tm
    return pl.pallas_call(
        _combine_kernel,
        grid_spec=pltpu.PrefetchScalarGridSpec(
            num_scalar_prefetch=2,
            grid=(t // tm,),
            in_specs=[pl.BlockSpec((tm, d), lambda i, *_: (i, 0)),
                      pl.BlockSpec((tm, LANES), lambda i, *_: (i, 0)),
                      pl.BlockSpec((1, 1, d), lambda i, *_: (i // tpb, 0, 0)),
                      pl.BlockSpec((1, d), lambda i, *_: (0, 0)),
                      pl.BlockSpec(memory_space=pl.ANY)],
            out_specs=pl.BlockSpec((tm, d), lambda i, *_: (i, 0)),
            scratch_shapes=[pltpu.VMEM((2, 2 * tm * PACK_ROWS, LANES), ROW_DTYPE),
                            pltpu.SemaphoreType.DMA((2,))]),
        out_shape=jax.ShapeDtypeStruct((t, d), F32),
        compiler_params=_params("arbitrary"),
    )(run_dst, run_len, x1, rinfo, g2, gpost, ys)


def _rope_tables(seq):
    pos = jnp.arange(seq, dtype=jnp.int32)
    row = (pos // GRID_W).astype(F32)
    col = (pos % GRID_W).astype(F32)
    axis_dim = HEAD_DIM // 2
    inv_freq = ROPE_THETA ** (-jnp.arange(0, axis_dim, 2, dtype=F32) / axis_dim)
    ang = jnp.concatenate([row[:, None] * inv_freq, col[:, None] * inv_freq], axis=-1)
    pair = (jnp.arange(LANES) % HEAD_DIM) // 2
    cos = jnp.cos(ang)[:, pair]
    sin = jnp.sin(ang)[:, pair]
    even = (jnp.arange(LANES) % 2) == 0
    return cos, jnp.where(even, -sin, 0.0), jnp.where(even, 0.0, sin)


def _segment_ones(n):
    seg = jnp.arange(n) // HEAD_DIM
    return (seg[:, None] == seg[None, :]).astype(BF16)


def kernel(x, c, ctx, c_ctx, w_mod, b_mod, attn_pre_norm, attn_post_norm, w_in, a_sink,
           b_q_norm, b_k_norm, a_out_norm, b_out_norm, w_out, ffn_pre_norm, ffn_post_norm,
           w_group, b_group, w_router, b_router, w_gate, w_up, w_down):
    batch, seq, d = x.shape
    ctx_len = ctx.shape[1]
    assert w_mod.shape[0] == 1, "single-layer stack only (context stream is never updated)"
    assert seq % ATTN_A_TQ == 0 and seq >= ATTN_A_TQ + 2 * WINDOW
    assert seq % PROJ_TM == 0 and seq % ATTN_B_TQ == 0 and seq % OUT_TM == 0 and seq % MOVE_TM == 0
    t = batch * seq
    nq = d // 2
    nkv = nq // KV_GROUP
    assert nkv == LANES and w_in.shape[2] == 2 * nq + 4 * nkv

    cc = jnp.concatenate([c, c_ctx[None, :], jnp.zeros((16 - batch - 1, d), F32)], axis=0)
    mod = _modulation(cc, w_mod[0], b_mod[0])
    sh1, sc1, g1, sh2, sc2, g2 = (m.reshape(batch, 1, d) for m in jnp.split(mod[:batch], 6, axis=-1))
    csh1, csc1 = (m.reshape(1, d) for m in jnp.split(mod[batch], 6)[:2])

    x2 = x.reshape(t, d)
    c2 = ctx.reshape(batch * ctx_len, d)
    gpre = attn_pre_norm[0].reshape(1, d)
    w_in_bf = w_in[0].astype(BF16)
    kv_cols = jnp.concatenate([w_in_bf[:, nq:nq + 2 * nkv], w_in_bf[:, 2 * nq + 2 * nkv:]], axis=1)
    qn = jnp.tile(b_q_norm[0], nq // HEAD_DIM).reshape(1, nq)
    kn = jnp.tile(b_k_norm[0], nkv // HEAD_DIM).reshape(1, nkv)
    seg_q, seg_k = _segment_ones(nq), _segment_ones(nkv)
    qat, ka, vat, qbt, kb, vbt = _project_latents(
        x2, sc1, sh1, gpre, w_in_bf, _rope_tables(seq), qn, kn, seg_q, seg_k, seq)
    kca, vcat, kcb, vcbt = _project_context(c2, csc1, csh1, gpre, kv_cols, kn, seg_k, ctx_len)

    oa = _attention_a(a_sink[0], qat, ka, vat, kca, vcat, batch, seq, ctx_len)
    score_bound = (1.01 * HEAD_DIM ** 0.5 * LOG2_E
                   * jnp.max(jnp.abs(b_q_norm[0])) * jnp.max(jnp.abs(b_k_norm[0]))).reshape(1)
    ob = _attention_b(score_bound, qbt, kb, vbt, kcb, vcbt, batch, seq, ctx_len)

    w_out_bf = w_out[0].astype(BF16)
    w_r = jnp.zeros((d, LANES), F32)
    w_r = w_r.at[:, :N_EXPERTS].set(w_router[0]).at[:, N_EXPERTS:N_EXPERTS + N_GROUPS].set(w_group[0])
    w_r_hi = w_r.astype(BF16)
    w_r_lo = (w_r - w_r_hi.astype(F32)).astype(BF16)
    b_r = jnp.zeros((1, LANES), F32)
    b_r = b_r.at[0, :N_EXPERTS].set(b_router[0]).at[0, N_EXPERTS:N_EXPERTS + N_GROUPS].set(b_group[0])
    x1, h2, rinfo, rt, tcarry, tcnt, counts = _out_and_route(
        oa, ob, x2, g1, sc2, sh2, a_out_norm[0].reshape(1, nq), b_out_norm[0].reshape(1, nq),
        attn_post_norm[0].reshape(1, d), ffn_pre_norm[0].reshape(1, d),
        w_out_bf[:nq], w_out_bf[nq:], w_r_hi, w_r_lo, b_r, seq)

    te = EXPERT_TE
    n_tiles = -(-(2 * t + N_EXPERTS * (te - 1)) // te)
    n_rows = n_tiles * te
    cnt = counts[0, :N_EXPERTS].astype(jnp.int32)
    padded = ((cnt + te - 1) // te) * te
    ends = jnp.cumsum(padded)
    offs = ends - padded
    run_dst = (offs[None, :] + tcarry[:, 0, :N_EXPERTS].astype(jnp.int32)).reshape(-1)
    run_len = tcnt[:, 0, :N_EXPERTS].astype(jnp.int32).reshape(-1)
    n_valid = (ends[-1] // te).astype(jnp.int32).reshape(1)
    tile_start = jnp.arange(n_tiles, dtype=jnp.int32) * te
    tile_expert = jnp.sum(ends[None, :] <= tile_start[:, None], axis=1).astype(jnp.int32)
    last_expert = tile_expert[jnp.maximum(n_valid[0] - 1, 0)]
    tile_expert = jnp.where(tile_start < ends[-1], tile_expert, last_expert)

    xs = _dispatch(run_dst, run_len, ends.astype(jnp.int32), n_valid, h2, rt, n_rows)
    ys = _expert_mlp(tile_expert, n_valid, xs, w_gate[0], w_up[0], w_down[0])
    out = _combine(run_dst, run_len, x1, rinfo, g2, ffn_post_norm[0].reshape(1, d), ys, seq)
    return out.reshape(batch, seq, d)
```

```python
import functools

import jax
import jax.numpy as jnp
from jax import lax
from jax.experimental import pallas as pl
from jax.experimental.pallas import tpu as pltpu

F32 = jnp.float32
BF16 = jnp.bfloat16

GRID_W = 64
HEAD_DIM = 64
KV_GROUP = 4
WINDOW = 128
ROPE_THETA = 10000.0
N_GROUPS = 4
EXPERTS_PER_GROUP = 8
N_EXPERTS = N_GROUPS * EXPERTS_PER_GROUP
EPS = 1e-6
NEG_INF = -1e30
LOG2_E = 1.4426950408889634
SAFE_SOFTMAX_SHIFT = 40.0

LANES = 128
V7X_VMEM_LIMIT = 56 * 1024 * 1024

PROJ_TM = 512
ATTN_A_TQ = 256
ATTN_B_TQ = 256
OUT_TM = 512
EXPERT_TE = 512
MOVE_TM = OUT_TM


def _params(*sem):
    return pltpu.CompilerParams(dimension_semantics=sem, vmem_limit_bytes=V7X_VMEM_LIMIT)


def _dot(a, b):
    return jnp.dot(a, b, preferred_element_type=F32)


def _dot_nt(a, b):
    return lax.dot_general(a, b, (((1,), (1,)), ((), ())), preferred_element_type=F32)


def _rms(x):
    return x * lax.rsqrt(jnp.mean(x * x, axis=-1, keepdims=True) + EPS)


def _split_bf16(x):
    hi = x.astype(BF16)
    lo = (x - hi.astype(F32)).astype(BF16)
    return hi, lo


def _mod_kernel(c_ref, w_ref, b_ref, o_ref):
    cc = c_ref[...]
    s = cc * jax.nn.sigmoid(cc)
    s_hi, s_lo = _split_bf16(s)
    w_hi, w_lo = _split_bf16(w_ref[...])
    o_ref[...] = _dot(s_hi, w_hi) + _dot(s_lo, w_hi) + _dot(s_hi, w_lo) + b_ref[...]


def _modulation(cc, w_mod, b_mod):
    rows, d = cc.shape
    n = w_mod.shape[1]
    bn = 1024
    return pl.pallas_call(
        _mod_kernel,
        grid=(n // bn,),
        in_specs=[pl.BlockSpec((rows, d), lambda i: (0, 0)),
                  pl.BlockSpec((d, bn), lambda i: (0, i)),
                  pl.BlockSpec((1, bn), lambda i: (0, i))],
        out_specs=pl.BlockSpec((rows, bn), lambda i: (0, i)),
        out_shape=jax.ShapeDtypeStruct((rows, n), F32),
        compiler_params=_params("arbitrary"),
    )(cc, w_mod, b_mod.reshape(1, n))


def _rope(x, cos, sin_a, sin_b):
    return x * cos + pltpu.roll(x, LANES - 1, 1) * sin_a + pltpu.roll(x, 1, 1) * sin_b


def _head_norm(x, seg_ref, gain):
    ss = _dot((x * x).astype(BF16), seg_ref[...])
    return x * lax.rsqrt(ss * (1.0 / HEAD_DIM) + EPS) * gain


def _max_head_sq_norm(x, seg_ref):
    ss = _dot((x * x).astype(BF16), seg_ref[...])
    return jnp.max(jnp.max(ss, axis=1, keepdims=True), axis=0, keepdims=True)


def _norm_stats(q_sq, k_sq):
    lane = lax.broadcasted_iota(jnp.int32, (1, LANES), 1)
    zero = jnp.zeros((1, LANES), F32)
    return jnp.where(lane == 0, q_sq, zero) + jnp.where(lane == 1, k_sq, zero)


def _proj_kernel(x_ref, sc_ref, sh_ref, gpre_ref, w_ref, cos_ref, sa_ref, sb_ref,
                 qn_ref, kn_ref, seg_q_ref, seg_k_ref,
                 qat_ref, ka_ref, vat_ref, qbt_ref, kb_ref, vbt_ref, stats_ref):
    h = _rms(x_ref[...]) * gpre_ref[...] * (1.0 + sc_ref[0]) + sh_ref[0]
    p = _dot(h.astype(BF16), w_ref[...])
    cos, sa, sb = cos_ref[...], sa_ref[...], sb_ref[...]
    nq = qat_ref.shape[0]
    q_scale = HEAD_DIM ** -0.5 * LOG2_E
    for c in range(nq // LANES):
        qat_ref[c * LANES:(c + 1) * LANES, :] = (
            _rope(p[:, c * LANES:(c + 1) * LANES], cos, sa, sb) * q_scale).T.astype(BF16)
    o = nq
    ka_ref[...] = _rope(p[:, o:o + LANES], cos, sa, sb).astype(BF16)
    vat_ref[...] = p[:, o + LANES:o + 2 * LANES].T.astype(BF16)
    stats_ref[0] = _norm_stats(_max_head_sq_norm(p[:, 0:nq], seg_q_ref) * (q_scale * q_scale),
                               _max_head_sq_norm(p[:, o:o + LANES], seg_k_ref))
    o += 2 * LANES
    qb = _head_norm(p[:, o:o + nq], seg_q_ref, qn_ref[...])
    for c in range(nq // LANES):
        qbt_ref[c * LANES:(c + 1) * LANES, :] = (
            _rope(qb[:, c * LANES:(c + 1) * LANES], cos, sa, sb) * q_scale).T.astype(BF16)
    o += nq
    kb = _head_norm(p[:, o:o + LANES], seg_k_ref, kn_ref[...])
    kb_ref[...] = _rope(kb, cos, sa, sb).astype(BF16)
    vbt_ref[...] = p[:, o + LANES:o + 2 * LANES].T.astype(BF16)


def _ctx_proj_kernel(x_ref, sc_ref, sh_ref, gpre_ref, w_ref, kn_ref, seg_k_ref,
                     ka_ref, vat_ref, kb_ref, vbt_ref, stats_ref):
    h = _rms(x_ref[...]) * gpre_ref[...] * (1.0 + sc_ref[...]) + sh_ref[...]
    p = _dot(h.astype(BF16), w_ref[...])
    stats_ref[0] = _norm_stats(0.0, _max_head_sq_norm(p[:, 0:LANES], seg_k_ref))
    ka_ref[...] = p[:, 0:LANES].astype(BF16)
    vat_ref[...] = p[:, LANES:2 * LANES].T.astype(BF16)
    kb_ref[...] = _head_norm(p[:, 2 * LANES:3 * LANES], seg_k_ref, kn_ref[...]).astype(BF16)
    vbt_ref[...] = p[:, 3 * LANES:4 * LANES].T.astype(BF16)


def _project_latents(x2, sc, sh, gpre, w_in, tables, qn, kn, seg_q, seg_k, seq):
    t, d = x2.shape
    tm = PROJ_TM
    tpb = seq // tm
    nq = seg_q.shape[0]
    const = lambda shape: pl.BlockSpec(shape, lambda i: (0,) * len(shape))
    per_batch = pl.BlockSpec((1, 1, d), lambda i: (i // tpb, 0, 0))
    table = pl.BlockSpec((tm, LANES), lambda i: (i % tpb, 0))
    k_spec = pl.BlockSpec((tm, LANES), lambda i: (i, 0))
    k_shape = jax.ShapeDtypeStruct((t, LANES), BF16)
    vt_spec = pl.BlockSpec((LANES, tm), lambda i: (0, i))
    vt_shape = jax.ShapeDtypeStruct((LANES, t), BF16)
    qt_spec = pl.BlockSpec((nq, tm), lambda i: (0, i))
    qt_shape = jax.ShapeDtypeStruct((nq, t), BF16)
    return pl.pallas_call(
        _proj_kernel,
        grid=(t // tm,),
        in_specs=[pl.BlockSpec((tm, d), lambda i: (i, 0)), per_batch, per_batch, const((1, d)),
                  const(w_in.shape), table, table, table,
                  const((1, nq)), const((1, LANES)), const(seg_q.shape), const(seg_k.shape)],
        out_specs=[qt_spec, k_spec, vt_spec, qt_spec, k_spec, vt_spec,
                   pl.BlockSpec((1, 1, LANES), lambda i: (i, 0, 0))],
        out_shape=[qt_shape, k_shape, vt_shape, qt_shape, k_shape, vt_shape,
                   jax.ShapeDtypeStruct((t // tm, 1, LANES), F32)],
        compiler_params=_params("arbitrary"),
    )(x2, sc, sh, gpre, w_in, *tables, qn, kn, seg_q, seg_k)


def _project_context(c2, sc, sh, gpre, w_kv, kn, seg_k, ctx_len):
    t, d = c2.shape
    const = lambda shape: pl.BlockSpec(shape, lambda i: (0,) * len(shape))
    k_spec = pl.BlockSpec((ctx_len, LANES), lambda i: (i, 0))
    k_shape = jax.ShapeDtypeStruct((t, LANES), BF16)
    vt_spec = pl.BlockSpec((LANES, ctx_len), lambda i: (0, i))
    vt_shape = jax.ShapeDtypeStruct((LANES, t), BF16)
    return pl.pallas_call(
        _ctx_proj_kernel,
        grid=(t // ctx_len,),
        in_specs=[pl.BlockSpec((ctx_len, d), lambda i: (i, 0)), const((1, d)), const((1, d)),
                  const((1, d)), const(w_kv.shape), const((1, LANES)), const(seg_k.shape)],
        out_specs=[k_spec, vt_spec, k_spec, vt_spec,
                   pl.BlockSpec((1, 1, LANES), lambda i: (i, 0, 0))],
        out_shape=[k_shape, vt_shape, k_shape, vt_shape,
                   jax.ShapeDtypeStruct((t // ctx_len, 1, LANES), F32)],
        compiler_params=_params("arbitrary"),
    )(c2, sc, sh, gpre, w_kv, kn, seg_k)


def _attend_t(w, k, kc, vt, vct, shift=None, bias=None, sink=None):
    st = _dot(k, w)
    sct = _dot(kc, w)
    if bias is not None:
        tq = bias.shape[1]
        st = jnp.concatenate([st[:, c * tq:(c + 1) * tq] + bias
                              for c in range(st.shape[1] // tq)], axis=1)
    if shift is None:
        shift = jnp.maximum(jnp.max(st, axis=0, keepdims=True),
                            jnp.max(sct, axis=0, keepdims=True))
        if sink is not None:
            shift = jnp.maximum(shift, sink)
    pt = jnp.exp2(st - shift)
    pct = jnp.exp2(sct - shift)
    denom = jnp.sum(pt, axis=0, keepdims=True) + jnp.sum(pct, axis=0, keepdims=True)
    if sink is not None:
        denom = denom + jnp.exp2(sink - shift)
    o2 = _dot(vt, pt.astype(BF16)) + _dot(vct, pct.astype(BF16))
    return o2, denom


def _all_heads_t(qt_ref, o_ref, attend):
    tq = qt_ref.shape[1]
    n_kv = LANES // HEAD_DIM
    zeros = jnp.zeros((HEAD_DIM, tq), BF16)
    outs = []
    for h in range(qt_ref.shape[0] // HEAD_DIM):
        g = h // KV_GROUP
        qh = qt_ref[h * HEAD_DIM:(h + 1) * HEAD_DIM, :]
        w = jnp.concatenate([zeros] * g + [qh] + [zeros] * (n_kv - 1 - g), axis=0)
        o2, denom = attend(h, w)
        outs.append(o2[g * HEAD_DIM:(g + 1) * HEAD_DIM, :] / denom)
    o_ref[...] = jnp.concatenate(outs, axis=0).T.astype(BF16)


def _all_heads_fused_t(qt_ref, o_ref, attend):
    tq = qt_ref.shape[1]
    n_heads = qt_ref.shape[0] // HEAD_DIM
    n_kv = LANES // HEAD_DIM
    rows = []
    for g in range(n_kv):
        heads = [qt_ref[h * HEAD_DIM:(h + 1) * HEAD_DIM, :] if h // KV_GROUP == g
                 else jnp.zeros((HEAD_DIM, tq), BF16) for h in range(n_heads)]
        rows.append(jnp.concatenate(heads, axis=1))
    w = jnp.concatenate(rows, axis=0)
    o2, denom = attend(0, w)
    o2 = o2 / denom
    outs = [o2[(h // KV_GROUP) * HEAD_DIM:(h // KV_GROUP + 1) * HEAD_DIM, h * tq:(h + 1) * tq]
            for h in range(n_heads)]
    o_ref[...] = jnp.concatenate(outs, axis=0).T.astype(BF16)


def _attn_b_kernel(bound_ref, qt_ref, k_ref, vt_ref, kc_ref, vct_ref, o_ref):
    k, kc, vt, vct = k_ref[...], kc_ref[...], vt_ref[...], vct_ref[...]
    bound = bound_ref[0]

    @pl.when(bound <= SAFE_SOFTMAX_SHIFT)
    def _():
        _all_heads_fused_t(qt_ref, o_ref, lambda h, w: _attend_t(w, k, kc, vt, vct, shift=bound))

    @pl.when(jnp.logical_not(bound <= SAFE_SOFTMAX_SHIFT))
    def _():
        _all_heads_t(qt_ref, o_ref, lambda h, w: _attend_t(w, k, kc, vt, vct))


def _attn_specs(nq, tq, seq, ctx_len):
    nqb = seq // tq
    return dict(
        qt=pl.BlockSpec((nq, tq), lambda b, i: (0, b * nqb + i)),
        k=pl.BlockSpec((seq, LANES), lambda b, i: (b, 0)),
        vt=pl.BlockSpec((LANES, seq), lambda b, i: (0, b)),
        kc=pl.BlockSpec((ctx_len, LANES), lambda b, i: (b, 0)),
        vct=pl.BlockSpec((LANES, ctx_len), lambda b, i: (0, b)),
        out=pl.BlockSpec((tq, nq), lambda b, i: (b * nqb + i, 0)))


def _attention_b(score_bound, qbt, kb, vbt, kcb, vcbt, batch, seq, ctx_len):
    nq, t = qbt.shape
    tq = ATTN_B_TQ
    sp = _attn_specs(nq, tq, seq, ctx_len)
    return pl.pallas_call(
        _attn_b_kernel,
        grid=(batch, seq // tq),
        in_specs=[pl.BlockSpec(memory_space=pltpu.SMEM),
                  sp["qt"], sp["k"], sp["vt"], sp["kc"], sp["vct"]],
        out_specs=sp["out"],
        out_shape=jax.ShapeDtypeStruct((t, nq), BF16),
        compiler_params=_params("arbitrary", "arbitrary"),
    )(score_bound, qbt, kb, vbt, kcb, vcbt)


def _attn_a_kernel(bound_ref, sink_ref, qt_ref, k_ref, vt_ref, kc_ref, vct_ref, bias_ref, o_ref,
                   *, seq):
    i = pl.program_id(1)
    tq = qt_ref.shape[1]
    n_heads = qt_ref.shape[0] // HEAD_DIM
    win = tq + 2 * WINDOW
    start = pl.multiple_of(jnp.clip(i * tq - WINDOW, 0, seq - win), WINDOW)
    k = k_ref[pl.ds(start, win), :]
    vt = vt_ref[:, pl.ds(start, win)]
    kc, vct = kc_ref[...], vct_ref[...]
    sinks = [sink_ref[h] * LOG2_E for h in range(n_heads)]

    bound = bound_ref[0]
    small = bound <= SAFE_SOFTMAX_SHIFT

    @pl.when(small)
    def _():
        sink_row = jnp.concatenate([jnp.full((1, tq), s, F32) for s in sinks], axis=1)
        _all_heads_fused_t(qt_ref, o_ref, lambda h, w: _attend_t(
            w, k, kc, vt, vct, shift=bound, bias=bias_ref[0], sink=sink_row))

    @pl.when(jnp.logical_not(small))
    def _():
        _all_heads_t(qt_ref, o_ref, lambda h, w: _attend_t(
            w, k, kc, vt, vct, bias=bias_ref[0], sink=sinks[h]))


def _band_bias(tq):
    win = tq + 2 * WINDOW
    r = jnp.arange(win, dtype=jnp.int32)[:, None]
    j = jnp.arange(tq, dtype=jnp.int32)[None, :]
    tables = [jnp.where(jnp.abs(off + r - j) <= WINDOW, 0.0, NEG_INF)
              for off in (0, -WINDOW, -2 * WINDOW)]
    return jnp.stack(tables).astype(F32)


def _attention_a(score_bound, sink, qat, ka, vat, kca, vcat, batch, seq, ctx_len):
    nq, t = qat.shape
    tq = ATTN_A_TQ
    nqb = seq // tq
    win = tq + 2 * WINDOW
    assert nqb >= 2 and tq >= WINDOW
    sp = _attn_specs(nq, tq, seq, ctx_len)
    which = lambda b, i: (jnp.where(i == 0, 0, jnp.where(i == nqb - 1, 2, 1)), 0, 0)
    return pl.pallas_call(
        functools.partial(_attn_a_kernel, seq=seq),
        grid=(batch, nqb),
        in_specs=[pl.BlockSpec(memory_space=pltpu.SMEM), pl.BlockSpec(memory_space=pltpu.SMEM),
                  sp["qt"], sp["k"], sp["vt"], sp["kc"], sp["vct"],
                  pl.BlockSpec((1, win, tq), which)],
        out_specs=sp["out"],
        out_shape=jax.ShapeDtypeStruct((t, nq), BF16),
        compiler_params=_params("arbitrary", "arbitrary"),
    )(score_bound, sink, qat, ka, vat, kca, vcat, _band_bias(tq))


def _out_kernel(oa_ref, ob_ref, x_ref, g1_ref, sc2_ref, sh2_ref, ga_ref, gb_ref, gpost_ref,
                gpre2_ref, woa_ref, wob_ref, wrh_ref, wrl_ref, br_ref,
                x1_ref, h2_ref, rinfo_ref, rt_ref, tcarry_ref, tcnt_ref, cnt_ref, carry_ref):
    step = pl.program_id(0)

    @pl.when(step == 0)
    def _():
        carry_ref[...] = jnp.zeros_like(carry_ref)

    na = _rms(oa_ref[...].astype(F32)) * ga_ref[...]
    nb = _rms(ob_ref[...].astype(F32)) * gb_ref[...]
    ox = _dot(na.astype(BF16), woa_ref[...]) + _dot(nb.astype(BF16), wob_ref[...])
    x1 = x_ref[...] + g1_ref[0] * (_rms(ox) * gpost_ref[...])
    x1_ref[...] = x1
    h2 = _rms(x1) * gpre2_ref[...] * (1.0 + sc2_ref[0]) + sh2_ref[0]
    h_hi, h_lo = _split_bf16(h2)
    h2_ref[...] = h_hi

    logits = (_dot(h_hi, wrh_ref[...]) + _dot(h_lo, wrh_ref[...]) + _dot(h_hi, wrl_ref[...])
              + br_ref[...])
    tm = logits.shape[0]
    lane = lax.broadcasted_iota(jnp.int32, logits.shape, 1)
    lanef = lane.astype(F32)
    big = jnp.float32(1e9)
    ninf = jnp.float32(-jnp.inf)
    rowmax = lambda v: jnp.max(v, axis=-1, keepdims=True)
    rowmin = lambda v: jnp.min(v, axis=-1, keepdims=True)
    rowsum = lambda v: jnp.sum(v, axis=-1, keepdims=True)

    gmask = (lane >= N_EXPERTS) & (lane < N_EXPERTS + N_GROUPS)
    lg = jnp.where(gmask, logits, ninf)
    gmax = rowmax(lg)
    gidx = rowmin(jnp.where(lg == gmax, lanef, big)) - N_EXPERTS
    g_w = 1.0 / rowsum(jnp.exp(lg - gmax))
    lane_group = (lane // EXPERTS_PER_GROUP).astype(F32)
    emask = (lane < N_EXPERTS) & (lane_group == gidx)
    le = jnp.where(emask, logits, ninf)
    m1 = rowmax(le)
    i1 = rowmin(jnp.where(le == m1, lanef, big))
    le2 = jnp.where(lanef == i1, ninf, le)
    m2 = rowmax(le2)
    i2 = rowmin(jnp.where(le2 == m2, lanef, big))
    e2 = jnp.exp(m2 - m1)
    w0 = g_w / (1.0 + e2)
    w1 = g_w * e2 / (1.0 + e2)

    hit1 = lanef == i1
    hit2 = lanef == i2
    onehot = jnp.where(hit1 | hit2, 1.0, 0.0).astype(F32)
    r = lax.broadcasted_iota(jnp.int32, (tm, tm), 0)
    c = lax.broadcasted_iota(jnp.int32, (tm, tm), 1)
    strict_lower = jnp.where(r > c, 1.0, 0.0).astype(BF16)
    within = _dot(strict_lower, onehot.astype(BF16))
    tile_cnt = jnp.sum(onehot, axis=0, keepdims=True)
    incl = jnp.broadcast_to(tile_cnt, (8, LANES))
    lane8 = lax.broadcasted_iota(jnp.int32, (8, LANES), 1)
    shift = 1
    while shift < LANES:
        incl = incl + jnp.where(lane8 >= shift, pltpu.roll(incl, shift, 1), 0.0)
        shift *= 2
    local = within + (incl[0:1] - tile_cnt)
    pos0 = rowsum(jnp.where(hit1, local, 0.0))
    pos1 = rowsum(jnp.where(hit2, local, 0.0))
    tcarry_ref[0] = carry_ref[...]
    tcnt_ref[0] = tile_cnt
    carry_ref[...] += tile_cnt
    cnt_ref[...] = carry_ref[...]

    info = jnp.zeros_like(logits)
    for k, val in enumerate((i1, i2, pos0, pos1, w0, w1)):
        info = jnp.where(lane == k, val, info)
    rinfo_ref[...] = info
    rt_ref[...] = info.T[0:8, :]


def _out_and_route(oa, ob, x2, g1, sc2, sh2, ga, gb, gpost, gpre2, woa, wob, wrh, wrl, br, seq):
    t, d = x2.shape
    tm = OUT_TM
    tpb = seq // tm
    nq = oa.shape[1]
    const = lambda shape: pl.BlockSpec(shape, lambda i: (0,) * len(shape))
    per_batch = pl.BlockSpec((1, 1, d), lambda i: (i // tpb, 0, 0))
    rows = lambda n: pl.BlockSpec((tm, n), lambda i: (i, 0))
    per_tile = pl.BlockSpec((1, 1, LANES), lambda i: (i, 0, 0))
    return pl.pallas_call(
        _out_kernel,
        grid=(t // tm,),
        in_specs=[rows(nq), rows(nq), rows(d), per_batch, per_batch, per_batch,
                  const((1, nq)), const((1, nq)), const((1, d)), const((1, d)),
                  const(woa.shape), const(wob.shape), const(wrh.shape), const(wrl.shape),
                  const((1, LANES))],
        out_specs=[rows(d), rows(d), rows(LANES), pl.BlockSpec((8, tm), lambda i: (0, i)),
                   per_tile, per_tile, const((1, LANES))],
        out_shape=[jax.ShapeDtypeStruct((t, d), F32), jax.ShapeDtypeStruct((t, d), BF16),
                   jax.ShapeDtypeStruct((t, LANES), F32), jax.ShapeDtypeStruct((8, t), F32),
                   jax.ShapeDtypeStruct((t // tm, 1, LANES), F32),
                   jax.ShapeDtypeStruct((t // tm, 1, LANES), F32),
                   jax.ShapeDtypeStruct((1, LANES), F32)],
        scratch_shapes=[pltpu.VMEM((1, LANES), F32)],
        compiler_params=_params("arbitrary"),
    )(oa, ob, x2, g1, sc2, sh2, ga, gb, gpost, gpre2, woa, wob, wrh, wrl, br)


PACK_ROWS = 8
ROW_DTYPE = F32


def _pack_rows(ref, x):
    n = x.shape[0]
    for c in range(PACK_ROWS):
        ref[pl.ds(c, n, stride=PACK_ROWS), :] = x[:, c * LANES:(c + 1) * LANES]


def _unpack_rows(ref):
    n = ref.shape[0] // PACK_ROWS
    return jnp.concatenate(
        [ref[pl.ds(c, n, stride=PACK_ROWS), :].astype(BF16) for c in range(PACK_ROWS)], axis=1)


def _for_each_run_piece(rdst_ref, rlen_ref, tile, max_len, fn):
    n_bits = max_len.bit_length()

    def run(e, local):
        length = rlen_ref[tile * N_EXPERTS + e]
        dst = rdst_ref[tile * N_EXPERTS + e]
        for b in range(n_bits):
            size = 1 << b

            @pl.when(((length >> b) & 1) == 1)
            def _():
                done = length & (size - 1)
                fn(local + done, dst + done, size)
        return local + length

    lax.fori_loop(0, N_EXPERTS, run, 0)


def _token_rows(ref, row0, n_rows):
    start = row0 * PACK_ROWS
    if not isinstance(start, int):
        start = pl.multiple_of(start, PACK_ROWS)
    return ref.at[pl.ds(start, n_rows * PACK_ROWS)]


def _dispatch_kernel(rdst_ref, rlen_ref, ends_ref, nv_ref, h_ref, rt_ref, xs_ref,
                     sorted_ref, zero_ref, sem, zsem, *, te, n_tiles):
    k = pl.program_id(0)
    nk = pl.num_programs(0)
    tm = h_ref.shape[0]
    rows = 2 * tm
    slot = k % 2

    def wait_slot(s):
        pltpu.make_async_copy(sorted_ref.at[s], _token_rows(xs_ref, 0, rows), sem.at[s]).wait()

    @pl.when(k == 0)
    def _():
        zero_ref[...] = jnp.zeros_like(zero_ref)

        def pad_copy(row0):
            return pltpu.make_async_copy(zero_ref, _token_rows(xs_ref, row0, te), zsem)

        def for_each_pad_tile(fn):
            def expert_pad(e, carry):
                end = ends_ref[e]
                prev = jnp.where(e > 0, ends_ref[jnp.maximum(e - 1, 0)], 0)

                @pl.when(end > prev)
                def _():
                    fn(pad_copy(end - te))
                return carry

            def tail_pad(j, carry):
                fn(pad_copy(j * te))
                return carry

            lax.fori_loop(0, N_EXPERTS, expert_pad, 0)
            lax.fori_loop(nv_ref[0], n_tiles, tail_pad, 0)

        for_each_pad_tile(lambda cp: cp.start())
        for_each_pad_tile(lambda cp: cp.wait())

    @pl.when(k >= 2)
    def _():
        wait_slot(slot)

    pos0 = rt_ref[2:3, :]
    pos1 = rt_ref[3:4, :]
    r = lax.broadcasted_iota(jnp.int32, (rows, tm), 0).astype(F32)
    perm = jnp.where((r == pos0) | (r == pos1), 1.0, 0.0).astype(BF16)
    srt = _dot(perm, h_ref[...].astype(BF16))
    buf = sorted_ref.at[slot]
    _pack_rows(buf, srt)

    def copy_piece(local, dst, size):
        pltpu.make_async_copy(_token_rows(buf, local, size), _token_rows(xs_ref, dst, size),
                              sem.at[slot]).start()

    _for_each_run_piece(rdst_ref, rlen_ref, k, tm, copy_piece)

    @pl.when(k == nk - 1)
    def _():
        wait_slot(slot)

        @pl.when(nk >= 2)
        def _():
            wait_slot(1 - slot)


def _dispatch(run_dst, run_len, ends, n_valid, h2, rt, n_rows):
    t, d = h2.shape
    assert d == PACK_ROWS * LANES
    tm = MOVE_TM
    te = EXPERT_TE
    return pl.pallas_call(
        functools.partial(_dispatch_kernel, te=te, n_tiles=n_rows // te),
        grid_spec=pltpu.PrefetchScalarGridSpec(
            num_scalar_prefetch=4,
            grid=(t // tm,),
            in_specs=[pl.BlockSpec((tm, d), lambda i, *_: (i, 0)),
                      pl.BlockSpec((8, tm), lambda i, *_: (0, i))],
            out_specs=pl.BlockSpec(memory_space=pl.ANY),
            scratch_shapes=[pltpu.VMEM((2, 2 * tm * PACK_ROWS, LANES), ROW_DTYPE),
                            pltpu.VMEM((te * PACK_ROWS, LANES), ROW_DTYPE),
                            pltpu.SemaphoreType.DMA((2,)), pltpu.SemaphoreType.DMA(())]),
        out_shape=jax.ShapeDtypeStruct((n_rows * PACK_ROWS, LANES), ROW_DTYPE),
        compiler_params=_params("arbitrary"),
    )(run_dst, run_len, ends, n_valid, h2, rt)


def _expert_kernel(te_ref, nv_ref, xs_ref, wg_ref, wu_ref, wd_ref, ys_ref, wg_bf, wu_bf, wd_bf):
    j = pl.program_id(0)
    valid = j < nv_ref[0]
    changed = (j == 0) | (te_ref[j] != te_ref[jnp.maximum(j - 1, 0)])

    @pl.when(valid & changed)
    def _():
        wg_bf[...] = wg_ref[0].astype(BF16)
        wu_bf[...] = wu_ref[0].astype(BF16)
        wd_bf[...] = wd_ref[0].astype(BF16)

    @pl.when(valid)
    def _():
        xb = _unpack_rows(xs_ref)
        gate = _dot(xb, wg_bf[...])
        up = _dot(xb, wu_bf[...])
        act = gate * jax.nn.sigmoid(gate) * up
        _pack_rows(ys_ref, _dot(act.astype(BF16), wd_bf[...]))

    @pl.when(jnp.logical_not(valid))
    def _():
        ys_ref[...] = jnp.zeros_like(ys_ref)


def _expert_mlp(tile_expert, n_valid, xs, w_gate, w_up, w_down):
    te = EXPERT_TE
    d, ff = w_gate.shape[1:]
    n_rows = xs.shape[0] // PACK_ROWS
    blk = (te * PACK_ROWS, LANES)
    tile = lambda j, e, nv: (jnp.minimum(j, nv[0] - 1), 0)
    wsel = lambda j, e, nv: (e[j], 0, 0)
    return pl.pallas_call(
        _expert_kernel,
        grid_spec=pltpu.PrefetchScalarGridSpec(
            num_scalar_prefetch=2,
            grid=(n_rows // te,),
            in_specs=[pl.BlockSpec(blk, tile),
                      pl.BlockSpec((1, d, ff), wsel), pl.BlockSpec((1, d, ff), wsel),
                      pl.BlockSpec((1, ff, d), wsel)],
            out_specs=pl.BlockSpec(blk, lambda j, e, nv: (j, 0)),
            scratch_shapes=[pltpu.VMEM((d, ff), BF16), pltpu.VMEM((d, ff), BF16),
                            pltpu.VMEM((ff, d), BF16)]),
        out_shape=jax.ShapeDtypeStruct(xs.shape, ROW_DTYPE),
        compiler_params=_params("arbitrary"),
    )(tile_expert, n_valid, xs, w_gate, w_up, w_down)


def _combine_kernel(rdst_ref, rlen_ref, x1_ref, rinfo_ref, g2_ref, gpost_ref, ys_ref, o_ref,
                    gath_ref, sem):
    k = pl.program_id(0)
    nk = pl.num_programs(0)
    tm = x1_ref.shape[0]
    rows = 2 * tm
    slot = k % 2

    def gather_runs(tile, s):
        buf = gath_ref.at[s]

        def copy_piece(local, src, size):
            pltpu.make_async_copy(_token_rows(ys_ref, src, size), _token_rows(buf, local, size),
                                  sem.at[s]).start()

        _for_each_run_piece(rdst_ref, rlen_ref, tile, tm, copy_piece)

    @pl.when(k == 0)
    def _():
        gather_runs(0, 0)

    @pl.when(k + 1 < nk)
    def _():
        gather_runs(k + 1, 1 - slot)

    buf = gath_ref.at[slot]
    pltpu.make_async_copy(_token_rows(ys_ref, 0, rows), buf, sem.at[slot]).wait()
    g = _unpack_rows(buf)
    info = rinfo_ref[...]
    col = lax.broadcasted_iota(jnp.int32, (tm, rows), 1).astype(F32)
    pick0 = jnp.where(col == info[:, 2:3], 1.0, 0.0).astype(BF16)
    pick1 = jnp.where(col == info[:, 3:4], 1.0, 0.0).astype(BF16)
    fx = info[:, 4:5] * _dot(pick0, g) + info[:, 5:6] * _dot(pick1, g)
    o_ref[...] = x1_ref[...] + g2_ref[0] * (_rms(fx) * gpost_ref[...])


def _combine(run_dst, run_len, x1, rinfo, g2, gpost, ys, seq):
    t, d = x1.shape
    tm = MOVE_TM
    tpb = seq // tm
    return pl.pallas_call(
        _combine_kernel,
        grid_spec=pltpu.PrefetchScalarGridSpec(
            num_scalar_prefetch=2,
            grid=(t // tm,),
            in_specs=[pl.BlockSpec((tm, d), lambda i, *_: (i, 0)),
                      pl.BlockSpec((tm, LANES), lambda i, *_: (i, 0)),
                      pl.BlockSpec((1, 1, d), lambda i, *_: (i // tpb, 0, 0)),
                      pl.BlockSpec((1, d), lambda i, *_: (0, 0)),
                      pl.BlockSpec(memory_space=pl.ANY)],
            out_specs=pl.BlockSpec((tm, d), lambda i, *_: (i, 0)),
            scratch_shapes=[pltpu.VMEM((2, 2 * tm * PACK_ROWS, LANES), ROW_DTYPE),
                            pltpu.SemaphoreType.DMA((2,))]),
        out_shape=jax.ShapeDtypeStruct((t, d), F32),
        compiler_params=_params("arbitrary"),
    )(run_dst, run_len, x1, rinfo, g2, gpost, ys)


def _rope_tables(seq):
    pos = jnp.arange(seq, dtype=jnp.int32)
    row = (pos // GRID_W).astype(F32)
    col = (pos % GRID_W).astype(F32)
    axis_dim = HEAD_DIM // 2
    inv_freq = ROPE_THETA ** (-jnp.arange(0, axis_dim, 2, dtype=F32) / axis_dim)
    ang = jnp.concatenate([row[:, None] * inv_freq, col[:, None] * inv_freq], axis=-1)
    pair = (jnp.arange(LANES) % HEAD_DIM) // 2
    cos = jnp.cos(ang)[:, pair]
    sin = jnp.sin(ang)[:, pair]
    even = (jnp.arange(LANES) % 2) == 0
    return cos, jnp.where(even, -sin, 0.0), jnp.where(even, 0.0, sin)


def _segment_ones(n):
    seg = jnp.arange(n) // HEAD_DIM
    return (seg[:, None] == seg[None, :]).astype(BF16)


def kernel(x, c, ctx, c_ctx, w_mod, b_mod, attn_pre_norm, attn_post_norm, w_in, a_sink,
           b_q_norm, b_k_norm, a_out_norm, b_out_norm, w_out, ffn_pre_norm, ffn_post_norm,
           w_group, b_group, w_router, b_router, w_gate, w_up, w_down):
    batch, seq, d = x.shape
    ctx_len = ctx.shape[1]
    assert w_mod.shape[0] == 1, "single-layer stack only (context stream is never updated)"
    assert seq % ATTN_A_TQ == 0 and seq >= ATTN_A_TQ + 2 * WINDOW
    assert seq % PROJ_TM == 0 and seq % ATTN_B_TQ == 0 and seq % OUT_TM == 0 and seq % MOVE_TM == 0
    t = batch * seq
    nq = d // 2
    nkv = nq // KV_GROUP
    assert nkv == LANES and w_in.shape[2] == 2 * nq + 4 * nkv

    cc = jnp.concatenate([c, c_ctx[None, :], jnp.zeros((16 - batch - 1, d), F32)], axis=0)
    mod = _modulation(cc, w_mod[0], b_mod[0])
    sh1, sc1, g1, sh2, sc2, g2 = (m.reshape(batch, 1, d) for m in jnp.split(mod[:batch], 6, axis=-1))
    csh1, csc1 = (m.reshape(1, d) for m in jnp.split(mod[batch], 6)[:2])

    x2 = x.reshape(t, d)
    c2 = ctx.reshape(batch * ctx_len, d)
    gpre = attn_pre_norm[0].reshape(1, d)
    w_in_bf = w_in[0].astype(BF16)
    kv_cols = jnp.concatenate([w_in_bf[:, nq:nq + 2 * nkv], w_in_bf[:, 2 * nq + 2 * nkv:]], axis=1)
    qn = jnp.tile(b_q_norm[0], nq // HEAD_DIM).reshape(1, nq)
    kn = jnp.tile(b_k_norm[0], nkv // HEAD_DIM).reshape(1, nkv)
    seg_q, seg_k = _segment_ones(nq), _segment_ones(nkv)
    qat, ka, vat, qbt, kb, vbt, stats = _project_latents(
        x2, sc1, sh1, gpre, w_in_bf, _rope_tables(seq), qn, kn, seg_q, seg_k, seq)
    kca, vcat, kcb, vcbt, ctx_stats = _project_context(
        c2, csc1, csh1, gpre, kv_cols, kn, seg_k, ctx_len)

    q_sq = jnp.max(stats[:, 0, 0])
    k_sq = jnp.maximum(jnp.max(stats[:, 0, 1]), jnp.max(ctx_stats[:, 0, 1]))
    bound_a = jnp.maximum(1.01 * jnp.sqrt(q_sq * k_sq), jnp.max(a_sink[0]) * LOG2_E).reshape(1)
    oa = _attention_a(bound_a, a_sink[0], qat, ka, vat, kca, vcat, batch, seq, ctx_len)
    score_bound = (1.01 * HEAD_DIM ** 0.5 * LOG2_E
                   * jnp.max(jnp.abs(b_q_norm[0])) * jnp.max(jnp.abs(b_k_norm[0]))).reshape(1)
    ob = _attention_b(score_bound, qbt, kb, vbt, kcb, vcbt, batch, seq, ctx_len)

    w_out_bf = w_out[0].astype(BF16)
    w_r = jnp.zeros((d, LANES), F32)
    w_r = w_r.at[:, :N_EXPERTS].set(w_router[0]).at[:, N_EXPERTS:N_EXPERTS + N_GROUPS].set(w_group[0])
    w_r_hi = w_r.astype(BF16)
    w_r_lo = (w_r - w_r_hi.astype(F32)).astype(BF16)
    b_r = jnp.zeros((1, LANES), F32)
    b_r = b_r.at[0, :N_EXPERTS].set(b_router[0]).at[0, N_EXPERTS:N_EXPERTS + N_GROUPS].set(b_group[0])
    x1, h2, rinfo, rt, tcarry, tcnt, counts = _out_and_route(
        oa, ob, x2, g1, sc2, sh2, a_out_norm[0].reshape(1, nq), b_out_norm[0].reshape(1, nq),
        attn_post_norm[0].reshape(1, d), ffn_pre_norm[0].reshape(1, d),
        w_out_bf[:nq], w_out_bf[nq:], w_r_hi, w_r_lo, b_r, seq)

    te = EXPERT_TE
    n_tiles = -(-(2 * t + N_EXPERTS * (te - 1)) // te)
    n_rows = n_tiles * te
    cnt = counts[0, :N_EXPERTS].astype(jnp.int32)
    padded = ((cnt + te - 1) // te) * te
    ends = jnp.cumsum(padded)
    offs = ends - padded
    run_dst = (offs[None, :] + tcarry[:, 0, :N_EXPERTS].astype(jnp.int32)).reshape(-1)
    run_len = tcnt[:, 0, :N_EXPERTS].astype(jnp.int32).reshape(-1)
    n_valid = (ends[-1] // te).astype(jnp.int32).reshape(1)
    tile_start = jnp.arange(n_tiles, dtype=jnp.int32) * te
    tile_expert = jnp.sum(ends[None, :] <= tile_start[:, None], axis=1).astype(jnp.int32)
    last_expert = tile_expert[jnp.maximum(n_valid[0] - 1, 0)]
    tile_expert = jnp.where(tile_start < ends[-1], tile_expert, last_expert)

    xs = _dispatch(run_dst, run_len, ends.astype(jnp.int32), n_valid, h2, rt, n_rows)
    ys = _expert_mlp(tile_expert, n_valid, xs, w_gate[0], w_up[0], w_down[0])
    out = _combine(run_dst, run_len, x1, rinfo, g2, ffn_post_norm[0].reshape(1, d), ys, seq)
    return out.reshape(batch, seq, d)
```

```python
import functools

import jax
import jax.numpy as jnp
from jax import lax
from jax.experimental import pallas as pl
from jax.experimental.pallas import tpu as pltpu

F32 = jnp.float32
BF16 = jnp.bfloat16

GRID_W = 64
HEAD_DIM = 64
KV_GROUP = 4
WINDOW = 128
ROPE_THETA = 10000.0
N_GROUPS = 4
EXPERTS_PER_GROUP = 8
N_EXPERTS = N_GROUPS * EXPERTS_PER_GROUP
EPS = 1e-6
NEG_INF = -1e30
LOG2_E = 1.4426950408889634
SAFE_SOFTMAX_SHIFT = 40.0

LANES = 128
V7X_VMEM_LIMIT = 56 * 1024 * 1024

PROJ_TM = 512
ATTN_A_TQ = 256
ATTN_B_TQ = 256
OUT_TM = 512
EXPERT_TE = 512
EXPERT_CHUNKS = 2
MOVE_TM = OUT_TM


def _params(*sem):
    return pltpu.CompilerParams(dimension_semantics=sem, vmem_limit_bytes=V7X_VMEM_LIMIT)


def _dot(a, b):
    return jnp.dot(a, b, preferred_element_type=F32)


def _dot_nt(a, b):
    return lax.dot_general(a, b, (((1,), (1,)), ((), ())), preferred_element_type=F32)


def _rms(x):
    return x * lax.rsqrt(jnp.mean(x * x, axis=-1, keepdims=True) + EPS)


def _split_bf16(x):
    hi = x.astype(BF16)
    lo = (x - hi.astype(F32)).astype(BF16)
    return hi, lo


def _mod_kernel(c_ref, w_ref, b_ref, o_ref):
    cc = c_ref[...]
    s = cc * jax.nn.sigmoid(cc)
    s_hi, s_lo = _split_bf16(s)
    w_hi, w_lo = _split_bf16(w_ref[...])
    o_ref[...] = _dot(s_hi, w_hi) + _dot(s_lo, w_hi) + _dot(s_hi, w_lo) + b_ref[...]


def _modulation(cc, w_mod, b_mod):
    rows, d = cc.shape
    n = w_mod.shape[1]
    bn = 1024
    return pl.pallas_call(
        _mod_kernel,
        grid=(n // bn,),
        in_specs=[pl.BlockSpec((rows, d), lambda i: (0, 0)),
                  pl.BlockSpec((d, bn), lambda i: (0, i)),
                  pl.BlockSpec((1, bn), lambda i: (0, i))],
        out_specs=pl.BlockSpec((rows, bn), lambda i: (0, i)),
        out_shape=jax.ShapeDtypeStruct((rows, n), F32),
        compiler_params=_params("arbitrary"),
    )(cc, w_mod, b_mod.reshape(1, n))


def _rope(x, cos, sin_a, sin_b):
    return x * cos + pltpu.roll(x, LANES - 1, 1) * sin_a + pltpu.roll(x, 1, 1) * sin_b


def _head_norm(x, seg_ref, gain):
    ss = _dot((x * x).astype(BF16), seg_ref[...])
    return x * lax.rsqrt(ss * (1.0 / HEAD_DIM) + EPS) * gain


def _max_head_sq_norm(x, seg_ref):
    ss = _dot((x * x).astype(BF16), seg_ref[...])
    return jnp.max(jnp.max(ss, axis=1, keepdims=True), axis=0, keepdims=True)


def _norm_stats(q_sq, k_sq):
    lane = lax.broadcasted_iota(jnp.int32, (1, LANES), 1)
    zero = jnp.zeros((1, LANES), F32)
    return jnp.where(lane == 0, q_sq, zero) + jnp.where(lane == 1, k_sq, zero)


def _proj_kernel(x_ref, sc_ref, sh_ref, gpre_ref, w_ref, cos_ref, sa_ref, sb_ref,
                 qn_ref, kn_ref, seg_q_ref, seg_k_ref,
                 qat_ref, ka_ref, vat_ref, qbt_ref, kb_ref, vbt_ref, stats_ref):
    h = _rms(x_ref[...]) * gpre_ref[...] * (1.0 + sc_ref[0]) + sh_ref[0]
    p = _dot(h.astype(BF16), w_ref[...])
    cos, sa, sb = cos_ref[...], sa_ref[...], sb_ref[...]
    nq = qat_ref.shape[0]
    q_scale = HEAD_DIM ** -0.5 * LOG2_E
    for c in range(nq // LANES):
        qat_ref[c * LANES:(c + 1) * LANES, :] = (
            _rope(p[:, c * LANES:(c + 1) * LANES], cos, sa, sb) * q_scale).T.astype(BF16)
    o = nq
    ka_ref[...] = _rope(p[:, o:o + LANES], cos, sa, sb).astype(BF16)
    vat_ref[...] = p[:, o + LANES:o + 2 * LANES].T.astype(BF16)
    stats_ref[0] = _norm_stats(_max_head_sq_norm(p[:, 0:nq], seg_q_ref) * (q_scale * q_scale),
                               _max_head_sq_norm(p[:, o:o + LANES], seg_k_ref))
    o += 2 * LANES
    qb = _head_norm(p[:, o:o + nq], seg_q_ref, qn_ref[...])
    for c in range(nq // LANES):
        qbt_ref[c * LANES:(c + 1) * LANES, :] = (
            _rope(qb[:, c * LANES:(c + 1) * LANES], cos, sa, sb) * q_scale).T.astype(BF16)
    o += nq
    kb = _head_norm(p[:, o:o + LANES], seg_k_ref, kn_ref[...])
    kb_ref[...] = _rope(kb, cos, sa, sb).astype(BF16)
    vbt_ref[...] = p[:, o + LANES:o + 2 * LANES].T.astype(BF16)


def _ctx_proj_kernel(x_ref, sc_ref, sh_ref, gpre_ref, w_ref, kn_ref, seg_k_ref,
                     ka_ref, vat_ref, kb_ref, vbt_ref, stats_ref):
    h = _rms(x_ref[...]) * gpre_ref[...] * (1.0 + sc_ref[...]) + sh_ref[...]
    p = _dot(h.astype(BF16), w_ref[...])
    stats_ref[0] = _norm_stats(0.0, _max_head_sq_norm(p[:, 0:LANES], seg_k_ref))
    ka_ref[...] = p[:, 0:LANES].astype(BF16)
    vat_ref[...] = p[:, LANES:2 * LANES].T.astype(BF16)
    kb_ref[...] = _head_norm(p[:, 2 * LANES:3 * LANES], seg_k_ref, kn_ref[...]).astype(BF16)
    vbt_ref[...] = p[:, 3 * LANES:4 * LANES].T.astype(BF16)


def _project_latents(x2, sc, sh, gpre, w_in, tables, qn, kn, seg_q, seg_k, seq):
    t, d = x2.shape
    tm = PROJ_TM
    tpb = seq // tm
    nq = seg_q.shape[0]
    const = lambda shape: pl.BlockSpec(shape, lambda i: (0,) * len(shape))
    per_batch = pl.BlockSpec((1, 1, d), lambda i: (i // tpb, 0, 0))
    table = pl.BlockSpec((tm, LANES), lambda i: (i % tpb, 0))
    k_spec = pl.BlockSpec((tm, LANES), lambda i: (i, 0))
    k_shape = jax.ShapeDtypeStruct((t, LANES), BF16)
    vt_spec = pl.BlockSpec((LANES, tm), lambda i: (0, i))
    vt_shape = jax.ShapeDtypeStruct((LANES, t), BF16)
    qt_spec = pl.BlockSpec((nq, tm), lambda i: (0, i))
    qt_shape = jax.ShapeDtypeStruct((nq, t), BF16)
    return pl.pallas_call(
        _proj_kernel,
        grid=(t // tm,),
        in_specs=[pl.BlockSpec((tm, d), lambda i: (i, 0)), per_batch, per_batch, const((1, d)),
                  const(w_in.shape), table, table, table,
                  const((1, nq)), const((1, LANES)), const(seg_q.shape), const(seg_k.shape)],
        out_specs=[qt_spec, k_spec, vt_spec, qt_spec, k_spec, vt_spec,
                   pl.BlockSpec((1, 1, LANES), lambda i: (i, 0, 0))],
        out_shape=[qt_shape, k_shape, vt_shape, qt_shape, k_shape, vt_shape,
                   jax.ShapeDtypeStruct((t // tm, 1, LANES), F32)],
        compiler_params=_params("arbitrary"),
    )(x2, sc, sh, gpre, w_in, *tables, qn, kn, seg_q, seg_k)


def _project_context(c2, sc, sh, gpre, w_kv, kn, seg_k, ctx_len):
    t, d = c2.shape
    const = lambda shape: pl.BlockSpec(shape, lambda i: (0,) * len(shape))
    k_spec = pl.BlockSpec((ctx_len, LANES), lambda i: (i, 0))
    k_shape = jax.ShapeDtypeStruct((t, LANES), BF16)
    vt_spec = pl.BlockSpec((LANES, ctx_len), lambda i: (0, i))
    vt_shape = jax.ShapeDtypeStruct((LANES, t), BF16)
    return pl.pallas_call(
        _ctx_proj_kernel,
        grid=(t // ctx_len,),
        in_specs=[pl.BlockSpec((ctx_len, d), lambda i: (i, 0)), const((1, d)), const((1, d)),
                  const((1, d)), const(w_kv.shape), const((1, LANES)), const(seg_k.shape)],
        out_specs=[k_spec, vt_spec, k_spec, vt_spec,
                   pl.BlockSpec((1, 1, LANES), lambda i: (i, 0, 0))],
        out_shape=[k_shape, vt_shape, k_shape, vt_shape,
                   jax.ShapeDtypeStruct((t // ctx_len, 1, LANES), F32)],
        compiler_params=_params("arbitrary"),
    )(c2, sc, sh, gpre, w_kv, kn, seg_k)


def _attend_t(w, k, kc, vt, vct, shift=None, bias=None, sink=None):
    st = _dot(k, w)
    sct = _dot(kc, w)
    if bias is not None:
        tq = bias.shape[1]
        st = jnp.concatenate([st[:, c * tq:(c + 1) * tq] + bias
                              for c in range(st.shape[1] // tq)], axis=1)
    if shift is None:
        shift = jnp.maximum(jnp.max(st, axis=0, keepdims=True),
                            jnp.max(sct, axis=0, keepdims=True))
        if sink is not None:
            shift = jnp.maximum(shift, sink)
    pt = jnp.exp2(st - shift)
    pct = jnp.exp2(sct - shift)
    denom = jnp.sum(pt, axis=0, keepdims=True) + jnp.sum(pct, axis=0, keepdims=True)
    if sink is not None:
        denom = denom + jnp.exp2(sink - shift)
    o2 = _dot(vt, pt.astype(BF16)) + _dot(vct, pct.astype(BF16))
    return o2, denom


def _all_heads_t(qt_ref, o_ref, attend):
    tq = qt_ref.shape[1]
    n_kv = LANES // HEAD_DIM
    zeros = jnp.zeros((HEAD_DIM, tq), BF16)
    outs = []
    for h in range(qt_ref.shape[0] // HEAD_DIM):
        g = h // KV_GROUP
        qh = qt_ref[h * HEAD_DIM:(h + 1) * HEAD_DIM, :]
        w = jnp.concatenate([zeros] * g + [qh] + [zeros] * (n_kv - 1 - g), axis=0)
        o2, denom = attend(h, w)
        outs.append(o2[g * HEAD_DIM:(g + 1) * HEAD_DIM, :] / denom)
    o_ref[...] = jnp.concatenate(outs, axis=0).T.astype(BF16)


def _all_heads_fused_t(qt_ref, o_ref, attend):
    tq = qt_ref.shape[1]
    n_heads = qt_ref.shape[0] // HEAD_DIM
    n_kv = LANES // HEAD_DIM
    rows = []
    for g in range(n_kv):
        heads = [qt_ref[h * HEAD_DIM:(h + 1) * HEAD_DIM, :] if h // KV_GROUP == g
                 else jnp.zeros((HEAD_DIM, tq), BF16) for h in range(n_heads)]
        rows.append(jnp.concatenate(heads, axis=1))
    w = jnp.concatenate(rows, axis=0)
    o2, denom = attend(0, w)
    o2 = o2 / denom
    outs = [o2[(h // KV_GROUP) * HEAD_DIM:(h // KV_GROUP + 1) * HEAD_DIM, h * tq:(h + 1) * tq]
            for h in range(n_heads)]
    o_ref[...] = jnp.concatenate(outs, axis=0).T.astype(BF16)


def _attn_b_kernel(bound_ref, qt_ref, k_ref, vt_ref, kc_ref, vct_ref, o_ref):
    k, kc, vt, vct = k_ref[...], kc_ref[...], vt_ref[...], vct_ref[...]
    bound = bound_ref[0]

    @pl.when(bound <= SAFE_SOFTMAX_SHIFT)
    def _():
        _all_heads_fused_t(qt_ref, o_ref, lambda h, w: _attend_t(w, k, kc, vt, vct, shift=bound))

    @pl.when(jnp.logical_not(bound <= SAFE_SOFTMAX_SHIFT))
    def _():
        _all_heads_t(qt_ref, o_ref, lambda h, w: _attend_t(w, k, kc, vt, vct))


def _attn_specs(nq, tq, seq, ctx_len):
    nqb = seq // tq
    return dict(
        qt=pl.BlockSpec((nq, tq), lambda b, i: (0, b * nqb + i)),
        k=pl.BlockSpec((seq, LANES), lambda b, i: (b, 0)),
        vt=pl.BlockSpec((LANES, seq), lambda b, i: (0, b)),
        kc=pl.BlockSpec((ctx_len, LANES), lambda b, i: (b, 0)),
        vct=pl.BlockSpec((LANES, ctx_len), lambda b, i: (0, b)),
        out=pl.BlockSpec((tq, nq), lambda b, i: (b * nqb + i, 0)))


def _attention_b(score_bound, qbt, kb, vbt, kcb, vcbt, batch, seq, ctx_len):
    nq, t = qbt.shape
    tq = ATTN_B_TQ
    sp = _attn_specs(nq, tq, seq, ctx_len)
    return pl.pallas_call(
        _attn_b_kernel,
        grid=(batch, seq // tq),
        in_specs=[pl.BlockSpec(memory_space=pltpu.SMEM),
                  sp["qt"], sp["k"], sp["vt"], sp["kc"], sp["vct"]],
        out_specs=sp["out"],
        out_shape=jax.ShapeDtypeStruct((t, nq), BF16),
        compiler_params=_params("arbitrary", "arbitrary"),
    )(score_bound, qbt, kb, vbt, kcb, vcbt)


def _attn_a_kernel(bound_ref, sink_ref, qt_ref, k_ref, vt_ref, kc_ref, vct_ref, bias_ref, o_ref,
                   *, seq):
    i = pl.program_id(1)
    tq = qt_ref.shape[1]
    n_heads = qt_ref.shape[0] // HEAD_DIM
    win = tq + 2 * WINDOW
    start = pl.multiple_of(jnp.clip(i * tq - WINDOW, 0, seq - win), WINDOW)
    k = k_ref[pl.ds(start, win), :]
    vt = vt_ref[:, pl.ds(start, win)]
    kc, vct = kc_ref[...], vct_ref[...]
    sinks = [sink_ref[h] * LOG2_E for h in range(n_heads)]

    bound = bound_ref[0]
    small = bound <= SAFE_SOFTMAX_SHIFT

    @pl.when(small)
    def _():
        sink_row = jnp.concatenate([jnp.full((1, tq), s, F32) for s in sinks], axis=1)
        _all_heads_fused_t(qt_ref, o_ref, lambda h, w: _attend_t(
            w, k, kc, vt, vct, shift=bound, bias=bias_ref[0], sink=sink_row))

    @pl.when(jnp.logical_not(small))
    def _():
        _all_heads_t(qt_ref, o_ref, lambda h, w: _attend_t(
            w, k, kc, vt, vct, bias=bias_ref[0], sink=sinks[h]))


def _band_bias(tq):
    win = tq + 2 * WINDOW
    r = jnp.arange(win, dtype=jnp.int32)[:, None]
    j = jnp.arange(tq, dtype=jnp.int32)[None, :]
    tables = [jnp.where(jnp.abs(off + r - j) <= WINDOW, 0.0, NEG_INF)
              for off in (0, -WINDOW, -2 * WINDOW)]
    return jnp.stack(tables).astype(F32)


def _attention_a(score_bound, sink, qat, ka, vat, kca, vcat, batch, seq, ctx_len):
    nq, t = qat.shape
    tq = ATTN_A_TQ
    nqb = seq // tq
    win = tq + 2 * WINDOW
    assert nqb >= 2 and tq >= WINDOW
    sp = _attn_specs(nq, tq, seq, ctx_len)
    which = lambda b, i: (jnp.where(i == 0, 0, jnp.where(i == nqb - 1, 2, 1)), 0, 0)
    return pl.pallas_call(
        functools.partial(_attn_a_kernel, seq=seq),
        grid=(batch, nqb),
        in_specs=[pl.BlockSpec(memory_space=pltpu.SMEM), pl.BlockSpec(memory_space=pltpu.SMEM),
                  sp["qt"], sp["k"], sp["vt"], sp["kc"], sp["vct"],
                  pl.BlockSpec((1, win, tq), which)],
        out_specs=sp["out"],
        out_shape=jax.ShapeDtypeStruct((t, nq), BF16),
        compiler_params=_params("arbitrary", "arbitrary"),
    )(score_bound, sink, qat, ka, vat, kca, vcat, _band_bias(tq))


def _out_kernel(oa_ref, ob_ref, x_ref, g1_ref, sc2_ref, sh2_ref, ga_ref, gb_ref, gpost_ref,
                gpre2_ref, woa_ref, wob_ref, wrh_ref, wrl_ref, br_ref,
                x1_ref, h2_ref, rinfo_ref, rt_ref, tcarry_ref, tcnt_ref, cnt_ref, carry_ref):
    step = pl.program_id(0)

    @pl.when(step == 0)
    def _():
        carry_ref[...] = jnp.zeros_like(carry_ref)

    na = _rms(oa_ref[...].astype(F32)) * ga_ref[...]
    nb = _rms(ob_ref[...].astype(F32)) * gb_ref[...]
    ox = _dot(na.astype(BF16), woa_ref[...]) + _dot(nb.astype(BF16), wob_ref[...])
    x1 = x_ref[...] + g1_ref[0] * (_rms(ox) * gpost_ref[...])
    x1_ref[...] = x1
    h2 = _rms(x1) * gpre2_ref[...] * (1.0 + sc2_ref[0]) + sh2_ref[0]
    h_hi, h_lo = _split_bf16(h2)
    h2_ref[...] = h_hi

    logits = (_dot(h_hi, wrh_ref[...]) + _dot(h_lo, wrh_ref[...]) + _dot(h_hi, wrl_ref[...])
              + br_ref[...])
    tm = logits.shape[0]
    lane = lax.broadcasted_iota(jnp.int32, logits.shape, 1)
    lanef = lane.astype(F32)
    big = jnp.float32(1e9)
    ninf = jnp.float32(-jnp.inf)
    rowmax = lambda v: jnp.max(v, axis=-1, keepdims=True)
    rowmin = lambda v: jnp.min(v, axis=-1, keepdims=True)
    rowsum = lambda v: jnp.sum(v, axis=-1, keepdims=True)

    gmask = (lane >= N_EXPERTS) & (lane < N_EXPERTS + N_GROUPS)
    lg = jnp.where(gmask, logits, ninf)
    gmax = rowmax(lg)
    gidx = rowmin(jnp.where(lg == gmax, lanef, big)) - N_EXPERTS
    g_w = 1.0 / rowsum(jnp.exp(lg - gmax))
    lane_group = (lane // EXPERTS_PER_GROUP).astype(F32)
    emask = (lane < N_EXPERTS) & (lane_group == gidx)
    le = jnp.where(emask, logits, ninf)
    m1 = rowmax(le)
    i1 = rowmin(jnp.where(le == m1, lanef, big))
    le2 = jnp.where(lanef == i1, ninf, le)
    m2 = rowmax(le2)
    i2 = rowmin(jnp.where(le2 == m2, lanef, big))
    e2 = jnp.exp(m2 - m1)
    w0 = g_w / (1.0 + e2)
    w1 = g_w * e2 / (1.0 + e2)

    hit1 = lanef == i1
    hit2 = lanef == i2
    onehot = jnp.where(hit1 | hit2, 1.0, 0.0).astype(F32)
    r = lax.broadcasted_iota(jnp.int32, (tm, tm), 0)
    c = lax.broadcasted_iota(jnp.int32, (tm, tm), 1)
    strict_lower = jnp.where(r > c, 1.0, 0.0).astype(BF16)
    within = _dot(strict_lower, onehot.astype(BF16))
    tile_cnt = jnp.sum(onehot, axis=0, keepdims=True)
    incl = jnp.broadcast_to(tile_cnt, (8, LANES))
    lane8 = lax.broadcasted_iota(jnp.int32, (8, LANES), 1)
    shift = 1
    while shift < LANES:
        incl = incl + jnp.where(lane8 >= shift, pltpu.roll(incl, shift, 1), 0.0)
        shift *= 2
    local = within + (incl[0:1] - tile_cnt)
    pos0 = rowsum(jnp.where(hit1, local, 0.0))
    pos1 = rowsum(jnp.where(hit2, local, 0.0))
    tcarry_ref[0] = carry_ref[...]
    tcnt_ref[0] = tile_cnt
    carry_ref[...] += tile_cnt
    cnt_ref[...] = carry_ref[...]

    info = jnp.zeros_like(logits)
    for k, val in enumerate((i1, i2, pos0, pos1, w0, w1)):
        info = jnp.where(lane == k, val, info)
    rinfo_ref[...] = info
    rt_ref[...] = info.T[0:8, :]


def _out_and_route(oa, ob, x2, g1, sc2, sh2, ga, gb, gpost, gpre2, woa, wob, wrh, wrl, br, seq):
    t, d = x2.shape
    tm = OUT_TM
    tpb = seq // tm
    nq = oa.shape[1]
    const = lambda shape: pl.BlockSpec(shape, lambda i: (0,) * len(shape))
    per_batch = pl.BlockSpec((1, 1, d), lambda i: (i // tpb, 0, 0))
    rows = lambda n: pl.BlockSpec((tm, n), lambda i: (i, 0))
    per_tile = pl.BlockSpec((1, 1, LANES), lambda i: (i, 0, 0))
    return pl.pallas_call(
        _out_kernel,
        grid=(t // tm,),
        in_specs=[rows(nq), rows(nq), rows(d), per_batch, per_batch, per_batch,
                  const((1, nq)), const((1, nq)), const((1, d)), const((1, d)),
                  const(woa.shape), const(wob.shape), const(wrh.shape), const(wrl.shape),
                  const((1, LANES))],
        out_specs=[rows(d), rows(d), rows(LANES), pl.BlockSpec((8, tm), lambda i: (0, i)),
                   per_tile, per_tile, const((1, LANES))],
        out_shape=[jax.ShapeDtypeStruct((t, d), F32), jax.ShapeDtypeStruct((t, d), BF16),
                   jax.ShapeDtypeStruct((t, LANES), F32), jax.ShapeDtypeStruct((8, t), F32),
                   jax.ShapeDtypeStruct((t // tm, 1, LANES), F32),
                   jax.ShapeDtypeStruct((t // tm, 1, LANES), F32),
                   jax.ShapeDtypeStruct((1, LANES), F32)],
        scratch_shapes=[pltpu.VMEM((1, LANES), F32)],
        compiler_params=_params("arbitrary"),
    )(oa, ob, x2, g1, sc2, sh2, ga, gb, gpost, gpre2, woa, wob, wrh, wrl, br)


TOKEN_ROWS = 8
PACK_ROWS = 4
ROW_DTYPE = jnp.uint32


def _pack_rows(ref, x, tmp_ref):
    n = x.shape[0]
    for c in range(TOKEN_ROWS):
        tmp_ref[pl.ds(c, n, stride=TOKEN_ROWS), :] = x[:, c * LANES:(c + 1) * LANES]
    ref[...] = pltpu.bitcast(tmp_ref[...].astype(BF16), ROW_DTYPE)


def _unpack_rows(ref, tmp_ref):
    n = ref.shape[0] // PACK_ROWS
    tmp_ref[...] = pltpu.bitcast(ref[...], BF16).astype(F32)
    return jnp.concatenate(
        [tmp_ref[pl.ds(c, n, stride=TOKEN_ROWS), :] for c in range(TOKEN_ROWS)],
        axis=1).astype(BF16)


def _for_each_run_piece(rdst_ref, rlen_ref, tile, max_len, fn):
    n_bits = max_len.bit_length()

    def run(e, local):
        length = rlen_ref[tile * N_EXPERTS + e]
        dst = rdst_ref[tile * N_EXPERTS + e]
        for b in range(n_bits):
            size = 1 << b

            @pl.when(((length >> b) & 1) == 1)
            def _():
                done = length & (size - 1)
                fn(local + done, dst + done, size)
        return local + length

    lax.fori_loop(0, N_EXPERTS, run, 0)


def _token_rows(ref, row0, n_rows):
    start = row0 * PACK_ROWS
    if not isinstance(start, int):
        start = pl.multiple_of(start, PACK_ROWS)
    return ref.at[pl.ds(start, n_rows * PACK_ROWS)]


def _dispatch_kernel(rdst_ref, rlen_ref, ends_ref, nv_ref, h_ref, rt_ref, xs_ref,
                     sorted_ref, zero_ref, tmp_ref, sem, zsem, *, te, n_tiles):
    k = pl.program_id(0)
    nk = pl.num_programs(0)
    tm = h_ref.shape[0]
    rows = 2 * tm
    slot = k % 2

    def wait_slot(s):
        pltpu.make_async_copy(sorted_ref.at[s], _token_rows(xs_ref, 0, rows), sem.at[s]).wait()

    @pl.when(k == 0)
    def _():
        zero_ref[...] = jnp.zeros_like(zero_ref)

        def pad_copy(row0):
            return pltpu.make_async_copy(zero_ref, _token_rows(xs_ref, row0, te), zsem)

        def for_each_pad_tile(fn):
            def expert_pad(e, carry):
                end = ends_ref[e]
                prev = jnp.where(e > 0, ends_ref[jnp.maximum(e - 1, 0)], 0)

                @pl.when(end > prev)
                def _():
                    fn(pad_copy(end - te))
                return carry

            def tail_pad(j, carry):
                fn(pad_copy(j * te))
                return carry

            lax.fori_loop(0, N_EXPERTS, expert_pad, 0)
            lax.fori_loop(nv_ref[0], n_tiles, tail_pad, 0)

        for_each_pad_tile(lambda cp: cp.start())
        for_each_pad_tile(lambda cp: cp.wait())

    @pl.when(k >= 2)
    def _():
        wait_slot(slot)

    pos0 = rt_ref[2:3, :]
    pos1 = rt_ref[3:4, :]
    r = lax.broadcasted_iota(jnp.int32, (rows, tm), 0).astype(F32)
    perm = jnp.where((r == pos0) | (r == pos1), 1.0, 0.0).astype(BF16)
    srt = _dot(perm, h_ref[...].astype(BF16))
    buf = sorted_ref.at[slot]
    _pack_rows(buf, srt, tmp_ref)

    def copy_piece(local, dst, size):
        pltpu.make_async_copy(_token_rows(buf, local, size), _token_rows(xs_ref, dst, size),
                              sem.at[slot]).start()

    _for_each_run_piece(rdst_ref, rlen_ref, k, tm, copy_piece)

    @pl.when(k == nk - 1)
    def _():
        wait_slot(slot)

        @pl.when(nk >= 2)
        def _():
            wait_slot(1 - slot)


def _dispatch(run_dst, run_len, ends, n_valid, h2, rt, n_rows):
    t, d = h2.shape
    assert d == TOKEN_ROWS * LANES
    tm = MOVE_TM
    te = EXPERT_TE
    return pl.pallas_call(
        functools.partial(_dispatch_kernel, te=te, n_tiles=n_rows // te),
        grid_spec=pltpu.PrefetchScalarGridSpec(
            num_scalar_prefetch=4,
            grid=(t // tm,),
            in_specs=[pl.BlockSpec((tm, d), lambda i, *_: (i, 0)),
                      pl.BlockSpec((8, tm), lambda i, *_: (0, i))],
            out_specs=pl.BlockSpec(memory_space=pl.ANY),
            scratch_shapes=[pltpu.VMEM((2, 2 * tm * PACK_ROWS, LANES), ROW_DTYPE),
                            pltpu.VMEM((te * PACK_ROWS, LANES), ROW_DTYPE),
                            pltpu.VMEM((2 * tm * TOKEN_ROWS, LANES), F32),
                            pltpu.SemaphoreType.DMA((2,)), pltpu.SemaphoreType.DMA(())]),
        out_shape=jax.ShapeDtypeStruct((n_rows * PACK_ROWS, LANES), ROW_DTYPE),
        compiler_params=_params("arbitrary"),
    )(run_dst, run_len, ends, n_valid, h2, rt)


def _expert_kernel(te_ref, nv_ref, xs_ref, wg_ref, wu_ref, wd_ref, ys_ref,
                   wg_bf, wu_bf, wd_bf, tmp_in, tmp_out):
    j = pl.program_id(0)
    valid = j < nv_ref[0]
    changed = (j == 0) | (te_ref[j] != te_ref[jnp.maximum(j - 1, 0)])

    @pl.when(valid & changed)
    def _():
        wg_bf[...] = wg_ref[0].astype(BF16)
        wu_bf[...] = wu_ref[0].astype(BF16)
        wd_bf[...] = wd_ref[0].astype(BF16)

    @pl.when(valid)
    def _():
        chunk = xs_ref.shape[0] // EXPERT_CHUNKS
        for r in range(EXPERT_CHUNKS):
            rows = pl.ds(r * chunk, chunk)
            xb = _unpack_rows(xs_ref.at[rows], tmp_in.at[r])
            gate = _dot(xb, wg_bf[...])
            up = _dot(xb, wu_bf[...])
            act = gate * jax.nn.sigmoid(gate) * up
            _pack_rows(ys_ref.at[rows], _dot(act.astype(BF16), wd_bf[...]), tmp_out.at[r])

    @pl.when(jnp.logical_not(valid))
    def _():
        ys_ref[...] = jnp.zeros_like(ys_ref)


def _expert_mlp(tile_expert, n_valid, xs, w_gate, w_up, w_down):
    te = EXPERT_TE
    d, ff = w_gate.shape[1:]
    n_rows = xs.shape[0] // PACK_ROWS
    blk = (te * PACK_ROWS, LANES)
    chunk_rows = te // EXPERT_CHUNKS * TOKEN_ROWS
    tile = lambda j, e, nv: (jnp.minimum(j, nv[0] - 1), 0)
    wsel = lambda j, e, nv: (e[j], 0, 0)
    return pl.pallas_call(
        _expert_kernel,
        grid_spec=pltpu.PrefetchScalarGridSpec(
            num_scalar_prefetch=2,
            grid=(n_rows // te,),
            in_specs=[pl.BlockSpec(blk, tile),
                      pl.BlockSpec((1, d, ff), wsel), pl.BlockSpec((1, d, ff), wsel),
                      pl.BlockSpec((1, ff, d), wsel)],
            out_specs=pl.BlockSpec(blk, lambda j, e, nv: (j, 0)),
            scratch_shapes=[pltpu.VMEM((d, ff), BF16), pltpu.VMEM((d, ff), BF16),
                            pltpu.VMEM((ff, d), BF16),
                            pltpu.VMEM((EXPERT_CHUNKS, chunk_rows, LANES), F32),
                            pltpu.VMEM((EXPERT_CHUNKS, chunk_rows, LANES), F32)]),
        out_shape=jax.ShapeDtypeStruct(xs.shape, ROW_DTYPE),
        compiler_params=_params("arbitrary"),
    )(tile_expert, n_valid, xs, w_gate, w_up, w_down)


def _combine_kernel(rdst_ref, rlen_ref, x1_ref, rinfo_ref, g2_ref, gpost_ref, ys_ref, o_ref,
                    gath_ref, tmp_ref, sem):
    k = pl.program_id(0)
    nk = pl.num_programs(0)
    tm = x1_ref.shape[0]
    rows = 2 * tm
    slot = k % 2

    def gather_runs(tile, s):
        buf = gath_ref.at[s]

        def copy_piece(local, src, size):
            pltpu.make_async_copy(_token_rows(ys_ref, src, size), _token_rows(buf, local, size),
                                  sem.at[s]).start()

        _for_each_run_piece(rdst_ref, rlen_ref, tile, tm, copy_piece)

    @pl.when(k == 0)
    def _():
        gather_runs(0, 0)

    @pl.when(k + 1 < nk)
    def _():
        gather_runs(k + 1, 1 - slot)

    buf = gath_ref.at[slot]
    pltpu.make_async_copy(_token_rows(ys_ref, 0, rows), buf, sem.at[slot]).wait()
    g = _unpack_rows(buf, tmp_ref)
    info = rinfo_ref[...]
    col = lax.broadcasted_iota(jnp.int32, (tm, rows), 1).astype(F32)
    pick0 = jnp.where(col == info[:, 2:3], 1.0, 0.0).astype(BF16)
    pick1 = jnp.where(col == info[:, 3:4], 1.0, 0.0).astype(BF16)
    fx = info[:, 4:5] * _dot(pick0, g) + info[:, 5:6] * _dot(pick1, g)
    o_ref[...] = x1_ref[...] + g2_ref[0] * (_rms(fx) * gpost_ref[...])


def _combine(run_dst, run_len, x1, rinfo, g2, gpost, ys, seq):
    t, d = x1.shape
    tm = MOVE_TM
    tpb = seq // tm
    return pl.pallas_call(
        _combine_kernel,
        grid_spec=pltpu.PrefetchScalarGridSpec(
            num_scalar_prefetch=2,
            grid=(t // tm,),
            in_specs=[pl.BlockSpec((tm, d), lambda i, *_: (i, 0)),
                      pl.BlockSpec((tm, LANES), lambda i, *_: (i, 0)),
                      pl.BlockSpec((1, 1, d), lambda i, *_: (i // tpb, 0, 0)),
                      pl.BlockSpec((1, d), lambda i, *_: (0, 0)),
                      pl.BlockSpec(memory_space=pl.ANY)],
            out_specs=pl.BlockSpec((tm, d), lambda i, *_: (i, 0)),
            scratch_shapes=[pltpu.VMEM((2, 2 * tm * PACK_ROWS, LANES), ROW_DTYPE),
                            pltpu.VMEM((2 * tm * TOKEN_ROWS, LANES), F32),
                            pltpu.SemaphoreType.DMA((2,))]),
        out_shape=jax.ShapeDtypeStruct((t, d), F32),
        compiler_params=_params("arbitrary"),
    )(run_dst, run_len, x1, rinfo, g2, gpost, ys)


def _rope_tables(seq):
    pos = jnp.arange(seq, dtype=jnp.int32)
    row = (pos // GRID_W).astype(F32)
    col = (pos % GRID_W).astype(F32)
    axis_dim = HEAD_DIM // 2
    inv_freq = ROPE_THETA ** (-jnp.arange(0, axis_dim, 2, dtype=F32) / axis_dim)
    ang = jnp.concatenate([row[:, None] * inv_freq, col[:, None] * inv_freq], axis=-1)
    pair = (jnp.arange(LANES) % HEAD_DIM) // 2
    cos = jnp.cos(ang)[:, pair]
    sin = jnp.sin(ang)[:, pair]
    even = (jnp.arange(LANES) % 2) == 0
    return cos, jnp.where(even, -sin, 0.0), jnp.where(even, 0.0, sin)


def _segment_ones(n):
    seg = jnp.arange(n) // HEAD_DIM
    return (seg[:, None] == seg[None, :]).astype(BF16)


def kernel(x, c, ctx, c_ctx, w_mod, b_mod, attn_pre_norm, attn_post_norm, w_in, a_sink,
           b_q_norm, b_k_norm, a_out_norm, b_out_norm, w_out, ffn_pre_norm, ffn_post_norm,
           w_group, b_group, w_router, b_router, w_gate, w_up, w_down):
    batch, seq, d = x.shape
    ctx_len = ctx.shape[1]
    assert w_mod.shape[0] == 1, "single-layer stack only (context stream is never updated)"
    assert seq % ATTN_A_TQ == 0 and seq >= ATTN_A_TQ + 2 * WINDOW
    assert seq % PROJ_TM == 0 and seq % ATTN_B_TQ == 0 and seq % OUT_TM == 0 and seq % MOVE_TM == 0
    t = batch * seq
    nq = d // 2
    nkv = nq // KV_GROUP
    assert nkv == LANES and w_in.shape[2] == 2 * nq + 4 * nkv

    cc = jnp.concatenate([c, c_ctx[None, :], jnp.zeros((16 - batch - 1, d), F32)], axis=0)
    mod = _modulation(cc, w_mod[0], b_mod[0])
    sh1, sc1, g1, sh2, sc2, g2 = (m.reshape(batch, 1, d) for m in jnp.split(mod[:batch], 6, axis=-1))
    csh1, csc1 = (m.reshape(1, d) for m in jnp.split(mod[batch], 6)[:2])

    x2 = x.reshape(t, d)
    c2 = ctx.reshape(batch * ctx_len, d)
    gpre = attn_pre_norm[0].reshape(1, d)
    w_in_bf = w_in[0].astype(BF16)
    kv_cols = jnp.concatenate([w_in_bf[:, nq:nq + 2 * nkv], w_in_bf[:, 2 * nq + 2 * nkv:]], axis=1)
    qn = jnp.tile(b_q_norm[0], nq // HEAD_DIM).reshape(1, nq)
    kn = jnp.tile(b_k_norm[0], nkv // HEAD_DIM).reshape(1, nkv)
    seg_q, seg_k = _segment_ones(nq), _segment_ones(nkv)
    qat, ka, vat, qbt, kb, vbt, stats = _project_latents(
        x2, sc1, sh1, gpre, w_in_bf, _rope_tables(seq), qn, kn, seg_q, seg_k, seq)
    kca, vcat, kcb, vcbt, ctx_stats = _project_context(
        c2, csc1, csh1, gpre, kv_cols, kn, seg_k, ctx_len)

    q_sq = jnp.max(stats[:, 0, 0])
    k_sq = jnp.maximum(jnp.max(stats[:, 0, 1]), jnp.max(ctx_stats[:, 0, 1]))
    bound_a = jnp.maximum(1.01 * jnp.sqrt(q_sq * k_sq), jnp.max(a_sink[0]) * LOG2_E).reshape(1)
    oa = _attention_a(bound_a, a_sink[0], qat, ka, vat, kca, vcat, batch, seq, ctx_len)
    score_bound = (1.01 * HEAD_DIM ** 0.5 * LOG2_E
                   * jnp.max(jnp.abs(b_q_norm[0])) * jnp.max(jnp.abs(b_k_norm[0]))).reshape(1)
    ob = _attention_b(score_bound, qbt, kb, vbt, kcb, vcbt, batch, seq, ctx_len)

    w_out_bf = w_out[0].astype(BF16)
    w_r = jnp.zeros((d, LANES), F32)
    w_r = w_r.at[:, :N_EXPERTS].set(w_router[0]).at[:, N_EXPERTS:N_EXPERTS + N_GROUPS].set(w_group[0])
    w_r_hi = w_r.astype(BF16)
    w_r_lo = (w_r - w_r_hi.astype(F32)).astype(BF16)
    b_r = jnp.zeros((1, LANES), F32)
    b_r = b_r.at[0, :N_EXPERTS].set(b_router[0]).at[0, N_EXPERTS:N_EXPERTS + N_GROUPS].set(b_group[0])
    x1, h2, rinfo, rt, tcarry, tcnt, counts = _out_and_route(
        oa, ob, x2, g1, sc2, sh2, a_out_norm[0].reshape(1, nq), b_out_norm[0].reshape(1, nq),
        attn_post_norm[0].reshape(1, d), ffn_pre_norm[0].reshape(1, d),
        w_out_bf[:nq], w_out_bf[nq:], w_r_hi, w_r_lo, b_r, seq)

    te = EXPERT_TE
    n_tiles = -(-(2 * t + N_EXPERTS * (te - 1)) // te)
    n_rows = n_tiles * te
    cnt = counts[0, :N_EXPERTS].astype(jnp.int32)
    padded = ((cnt + te - 1) // te) * te
    ends = jnp.cumsum(padded)
    offs = ends - padded
    run_dst = (offs[None, :] + tcarry[:, 0, :N_EXPERTS].astype(jnp.int32)).reshape(-1)
    run_len = tcnt[:, 0, :N_EXPERTS].astype(jnp.int32).reshape(-1)
    n_valid = (ends[-1] // te).astype(jnp.int32).reshape(1)
    tile_start = jnp.arange(n_tiles, dtype=jnp.int32) * te
    tile_expert = jnp.sum(ends[None, :] <= tile_start[:, None], axis=1).astype(jnp.int32)
    last_expert = tile_expert[jnp.maximum(n_valid[0] - 1, 0)]
    tile_expert = jnp.where(tile_start < ends[-1], tile_expert, last_expert)

    xs = _dispatch(run_dst, run_len, ends.astype(jnp.int32), n_valid, h2, rt, n_rows)
    ys = _expert_mlp(tile_expert, n_valid, xs, w_gate[0], w_up[0], w_down[0])
    out = _combine(run_dst, run_len, x1, rinfo, g2, ffn_post_norm[0].reshape(1, d), ys, seq)
    return out.reshape(batch, seq, d)
```

```python
import functools

import jax
import jax.numpy as jnp
from jax import lax
from jax.experimental import pallas as pl
from jax.experimental.pallas import tpu as pltpu

F32 = jnp.float32
BF16 = jnp.bfloat16

GRID_W = 64
HEAD_DIM = 64
KV_GROUP = 4
WINDOW = 128
ROPE_THETA = 10000.0
N_GROUPS = 4
EXPERTS_PER_GROUP = 8
N_EXPERTS = N_GROUPS * EXPERTS_PER_GROUP
EPS = 1e-6
NEG_INF = -1e30
LOG2_E = 1.4426950408889634
SAFE_SOFTMAX_SHIFT = 40.0

LANES = 128
V7X_VMEM_LIMIT = 56 * 1024 * 1024

PROJ_TM = 512
ATTN_A_TQ = 256
ATTN_B_TQ = 256
OUT_TM = 512
EXPERT_TE = 512
EXPERT_CHUNKS = 2
MOVE_TM = OUT_TM


def _params(*sem):
    return pltpu.CompilerParams(dimension_semantics=sem, vmem_limit_bytes=V7X_VMEM_LIMIT)


def _dot(a, b):
    return jnp.dot(a, b, preferred_element_type=F32)


def _dot_nt(a, b):
    return lax.dot_general(a, b, (((1,), (1,)), ((), ())), preferred_element_type=F32)


def _rms(x):
    return x * lax.rsqrt(jnp.mean(x * x, axis=-1, keepdims=True) + EPS)


def _split_bf16(x):
    hi = x.astype(BF16)
    lo = (x - hi.astype(F32)).astype(BF16)
    return hi, lo


def _mod_kernel(c_ref, w_ref, b_ref, o_ref):
    cc = c_ref[...]
    s = cc * jax.nn.sigmoid(cc)
    s_hi, s_lo = _split_bf16(s)
    w_hi, w_lo = _split_bf16(w_ref[...])
    o_ref[...] = _dot(s_hi, w_hi) + _dot(s_lo, w_hi) + _dot(s_hi, w_lo) + b_ref[...]


def _modulation(cc, w_mod, b_mod):
    rows, d = cc.shape
    n = w_mod.shape[1]
    bn = 1024
    return pl.pallas_call(
        _mod_kernel,
        grid=(n // bn,),
        in_specs=[pl.BlockSpec((rows, d), lambda i: (0, 0)),
                  pl.BlockSpec((d, bn), lambda i: (0, i)),
                  pl.BlockSpec((1, bn), lambda i: (0, i))],
        out_specs=pl.BlockSpec((rows, bn), lambda i: (0, i)),
        out_shape=jax.ShapeDtypeStruct((rows, n), F32),
        compiler_params=_params("arbitrary"),
    )(cc, w_mod, b_mod.reshape(1, n))


def _rope(x, cos, sin_a, sin_b):
    return x * cos + pltpu.roll(x, LANES - 1, 1) * sin_a + pltpu.roll(x, 1, 1) * sin_b


def _head_norm(x, seg_ref, gain):
    ss = _dot((x * x).astype(BF16), seg_ref[...])
    return x * lax.rsqrt(ss * (1.0 / HEAD_DIM) + EPS) * gain


def _max_head_sq_norm(x, seg_ref):
    ss = _dot((x * x).astype(BF16), seg_ref[...])
    return jnp.max(jnp.max(ss, axis=1, keepdims=True), axis=0, keepdims=True)


def _norm_stats(q_sq, k_sq):
    lane = lax.broadcasted_iota(jnp.int32, (1, LANES), 1)
    zero = jnp.zeros((1, LANES), F32)
    return jnp.where(lane == 0, q_sq, zero) + jnp.where(lane == 1, k_sq, zero)


def _proj_kernel(x_ref, sc_ref, sh_ref, gpre_ref, w_ref, cos_ref, sa_ref, sb_ref,
                 qn_ref, kn_ref, seg_q_ref, seg_k_ref,
                 qat_ref, ka_ref, vat_ref, qbt_ref, kb_ref, vbt_ref, stats_ref):
    h = _rms(x_ref[...]) * gpre_ref[...] * (1.0 + sc_ref[0]) + sh_ref[0]
    p = _dot(h.astype(BF16), w_ref[...])
    cos, sa, sb = cos_ref[...], sa_ref[...], sb_ref[...]
    nq = qat_ref.shape[0]
    q_scale = HEAD_DIM ** -0.5 * LOG2_E
    for c in range(nq // LANES):
        qat_ref[c * LANES:(c + 1) * LANES, :] = (
            _rope(p[:, c * LANES:(c + 1) * LANES], cos, sa, sb) * q_scale).T.astype(BF16)
    o = nq
    ka_ref[...] = _rope(p[:, o:o + LANES], cos, sa, sb).astype(BF16)
    vat_ref[...] = p[:, o + LANES:o + 2 * LANES].T.astype(BF16)
    stats_ref[0] = _norm_stats(_max_head_sq_norm(p[:, 0:nq], seg_q_ref) * (q_scale * q_scale),
                               _max_head_sq_norm(p[:, o:o + LANES], seg_k_ref))
    o += 2 * LANES
    qb = _head_norm(p[:, o:o + nq], seg_q_ref, qn_ref[...])
    for c in range(nq // LANES):
        qbt_ref[c * LANES:(c + 1) * LANES, :] = (
            _rope(qb[:, c * LANES:(c + 1) * LANES], cos, sa, sb) * q_scale).T.astype(BF16)
    o += nq
    kb = _head_norm(p[:, o:o + LANES], seg_k_ref, kn_ref[...])
    kb_ref[...] = _rope(kb, cos, sa, sb).astype(BF16)
    vbt_ref[...] = p[:, o + LANES:o + 2 * LANES].T.astype(BF16)


def _ctx_proj_kernel(x_ref, sc_ref, sh_ref, gpre_ref, w_ref, kn_ref, seg_k_ref,
                     ka_ref, vat_ref, kb_ref, vbt_ref, stats_ref):
    h = _rms(x_ref[...]) * gpre_ref[...] * (1.0 + sc_ref[...]) + sh_ref[...]
    p = _dot(h.astype(BF16), w_ref[...])
    stats_ref[0] = _norm_stats(0.0, _max_head_sq_norm(p[:, 0:LANES], seg_k_ref))
    ka_ref[...] = p[:, 0:LANES].astype(BF16)
    vat_ref[...] = p[:, LANES:2 * LANES].T.astype(BF16)
    kb_ref[...] = _head_norm(p[:, 2 * LANES:3 * LANES], seg_k_ref, kn_ref[...]).astype(BF16)
    vbt_ref[...] = p[:, 3 * LANES:4 * LANES].T.astype(BF16)


def _project_latents(x2, sc, sh, gpre, w_in, tables, qn, kn, seg_q, seg_k, seq):
    t, d = x2.shape
    tm = PROJ_TM
    tpb = seq // tm
    nq = seg_q.shape[0]
    const = lambda shape: pl.BlockSpec(shape, lambda i: (0,) * len(shape))
    per_batch = pl.BlockSpec((1, 1, d), lambda i: (i // tpb, 0, 0))
    table = pl.BlockSpec((tm, LANES), lambda i: (i % tpb, 0))
    k_spec = pl.BlockSpec((tm, LANES), lambda i: (i, 0))
    k_shape = jax.ShapeDtypeStruct((t, LANES), BF16)
    vt_spec = pl.BlockSpec((LANES, tm), lambda i: (0, i))
    vt_shape = jax.ShapeDtypeStruct((LANES, t), BF16)
    qt_spec = pl.BlockSpec((nq, tm), lambda i: (0, i))
    qt_shape = jax.ShapeDtypeStruct((nq, t), BF16)
    return pl.pallas_call(
        _proj_kernel,
        grid=(t // tm,),
        in_specs=[pl.BlockSpec((tm, d), lambda i: (i, 0)), per_batch, per_batch, const((1, d)),
                  const(w_in.shape), table, table, table,
                  const((1, nq)), const((1, LANES)), const(seg_q.shape), const(seg_k.shape)],
        out_specs=[qt_spec, k_spec, vt_spec, qt_spec, k_spec, vt_spec,
                   pl.BlockSpec((1, 1, LANES), lambda i: (i, 0, 0))],
        out_shape=[qt_shape, k_shape, vt_shape, qt_shape, k_shape, vt_shape,
                   jax.ShapeDtypeStruct((t // tm, 1, LANES), F32)],
        compiler_params=_params("arbitrary"),
    )(x2, sc, sh, gpre, w_in, *tables, qn, kn, seg_q, seg_k)


def _project_context(c2, sc, sh, gpre, w_kv, kn, seg_k, ctx_len):
    t, d = c2.shape
    const = lambda shape: pl.BlockSpec(shape, lambda i: (0,) * len(shape))
    k_spec = pl.BlockSpec((ctx_len, LANES), lambda i: (i, 0))
    k_shape = jax.ShapeDtypeStruct((t, LANES), BF16)
    vt_spec = pl.BlockSpec((LANES, ctx_len), lambda i: (0, i))
    vt_shape = jax.ShapeDtypeStruct((LANES, t), BF16)
    return pl.pallas_call(
        _ctx_proj_kernel,
        grid=(t // ctx_len,),
        in_specs=[pl.BlockSpec((ctx_len, d), lambda i: (i, 0)), const((1, d)), const((1, d)),
                  const((1, d)), const(w_kv.shape), const((1, LANES)), const(seg_k.shape)],
        out_specs=[k_spec, vt_spec, k_spec, vt_spec,
                   pl.BlockSpec((1, 1, LANES), lambda i: (i, 0, 0))],
        out_shape=[k_shape, vt_shape, k_shape, vt_shape,
                   jax.ShapeDtypeStruct((t // ctx_len, 1, LANES), F32)],
        compiler_params=_params("arbitrary"),
    )(c2, sc, sh, gpre, w_kv, kn, seg_k)


def _attend_t(w, k, kc, vt, vct, shift=None, bias=None, sink=None):
    st = _dot(k, w)
    sct = _dot(kc, w)
    if bias is not None:
        tq = bias.shape[1]
        st = jnp.concatenate([st[:, c * tq:(c + 1) * tq] + bias
                              for c in range(st.shape[1] // tq)], axis=1)
    if shift is None:
        shift = jnp.maximum(jnp.max(st, axis=0, keepdims=True),
                            jnp.max(sct, axis=0, keepdims=True))
        if sink is not None:
            shift = jnp.maximum(shift, sink)
    pt = jnp.exp2(st - shift)
    pct = jnp.exp2(sct - shift)
    denom = jnp.sum(pt, axis=0, keepdims=True) + jnp.sum(pct, axis=0, keepdims=True)
    if sink is not None:
        denom = denom + jnp.exp2(sink - shift)
    o2 = _dot(vt, pt.astype(BF16)) + _dot(vct, pct.astype(BF16))
    return o2, denom


def _all_heads_t(qt_ref, o_ref, attend):
    tq = qt_ref.shape[1]
    n_kv = LANES // HEAD_DIM
    zeros = jnp.zeros((HEAD_DIM, tq), BF16)
    outs = []
    for h in range(qt_ref.shape[0] // HEAD_DIM):
        g = h // KV_GROUP
        qh = qt_ref[h * HEAD_DIM:(h + 1) * HEAD_DIM, :]
        w = jnp.concatenate([zeros] * g + [qh] + [zeros] * (n_kv - 1 - g), axis=0)
        o2, denom = attend(h, w)
        outs.append(o2[g * HEAD_DIM:(g + 1) * HEAD_DIM, :] / denom)
    o_ref[...] = jnp.concatenate(outs, axis=0).T.astype(BF16)


def _all_heads_fused_t(qt_ref, o_ref, attend):
    tq = qt_ref.shape[1]
    n_heads = qt_ref.shape[0] // HEAD_DIM
    n_kv = LANES // HEAD_DIM
    rows = []
    for g in range(n_kv):
        heads = [qt_ref[h * HEAD_DIM:(h + 1) * HEAD_DIM, :] if h // KV_GROUP == g
                 else jnp.zeros((HEAD_DIM, tq), BF16) for h in range(n_heads)]
        rows.append(jnp.concatenate(heads, axis=1))
    w = jnp.concatenate(rows, axis=0)
    o2, denom = attend(0, w)
    o2 = o2 / denom
    outs = [o2[(h // KV_GROUP) * HEAD_DIM:(h // KV_GROUP + 1) * HEAD_DIM, h * tq:(h + 1) * tq]
            for h in range(n_heads)]
    o_ref[...] = jnp.concatenate(outs, axis=0).T.astype(BF16)


def _attn_b_kernel(bound_ref, qt_ref, k_ref, vt_ref, kc_ref, vct_ref, o_ref):
    k, kc, vt, vct = k_ref[...], kc_ref[...], vt_ref[...], vct_ref[...]
    bound = bound_ref[0]

    @pl.when(bound <= SAFE_SOFTMAX_SHIFT)
    def _():
        _all_heads_fused_t(qt_ref, o_ref, lambda h, w: _attend_t(w, k, kc, vt, vct, shift=bound))

    @pl.when(jnp.logical_not(bound <= SAFE_SOFTMAX_SHIFT))
    def _():
        _all_heads_t(qt_ref, o_ref, lambda h, w: _attend_t(w, k, kc, vt, vct))


def _attn_specs(nq, tq, seq, ctx_len):
    nqb = seq // tq
    return dict(
        qt=pl.BlockSpec((nq, tq), lambda b, i: (0, b * nqb + i)),
        k=pl.BlockSpec((seq, LANES), lambda b, i: (b, 0)),
        vt=pl.BlockSpec((LANES, seq), lambda b, i: (0, b)),
        kc=pl.BlockSpec((ctx_len, LANES), lambda b, i: (b, 0)),
        vct=pl.BlockSpec((LANES, ctx_len), lambda b, i: (0, b)),
        out=pl.BlockSpec((tq, nq), lambda b, i: (b * nqb + i, 0)))


def _attention_b(score_bound, qbt, kb, vbt, kcb, vcbt, batch, seq, ctx_len):
    nq, t = qbt.shape
    tq = ATTN_B_TQ
    sp = _attn_specs(nq, tq, seq, ctx_len)
    return pl.pallas_call(
        _attn_b_kernel,
        grid=(batch, seq // tq),
        in_specs=[pl.BlockSpec(memory_space=pltpu.SMEM),
                  sp["qt"], sp["k"], sp["vt"], sp["kc"], sp["vct"]],
        out_specs=sp["out"],
        out_shape=jax.ShapeDtypeStruct((t, nq), BF16),
        compiler_params=_params("arbitrary", "arbitrary"),
    )(score_bound, qbt, kb, vbt, kcb, vcbt)


def _attn_a_kernel(bound_ref, sink_ref, qt_ref, k_ref, vt_ref, kc_ref, vct_ref, bias_ref, o_ref,
                   *, seq):
    i = pl.program_id(1)
    tq = qt_ref.shape[1]
    n_heads = qt_ref.shape[0] // HEAD_DIM
    win = tq + 2 * WINDOW
    start = pl.multiple_of(jnp.clip(i * tq - WINDOW, 0, seq - win), WINDOW)
    k = k_ref[pl.ds(start, win), :]
    vt = vt_ref[:, pl.ds(start, win)]
    kc, vct = kc_ref[...], vct_ref[...]
    sinks = [sink_ref[h] * LOG2_E for h in range(n_heads)]

    bound = bound_ref[0]
    small = bound <= SAFE_SOFTMAX_SHIFT

    @pl.when(small)
    def _():
        sink_row = jnp.concatenate([jnp.full((1, tq), s, F32) for s in sinks], axis=1)
        _all_heads_fused_t(qt_ref, o_ref, lambda h, w: _attend_t(
            w, k, kc, vt, vct, shift=bound, bias=bias_ref[0], sink=sink_row))

    @pl.when(jnp.logical_not(small))
    def _():
        _all_heads_t(qt_ref, o_ref, lambda h, w: _attend_t(
            w, k, kc, vt, vct, bias=bias_ref[0], sink=sinks[h]))


def _band_bias(tq):
    win = tq + 2 * WINDOW
    r = jnp.arange(win, dtype=jnp.int32)[:, None]
    j = jnp.arange(tq, dtype=jnp.int32)[None, :]
    tables = [jnp.where(jnp.abs(off + r - j) <= WINDOW, 0.0, NEG_INF)
              for off in (0, -WINDOW, -2 * WINDOW)]
    return jnp.stack(tables).astype(F32)


def _attention_a(score_bound, sink, qat, ka, vat, kca, vcat, batch, seq, ctx_len):
    nq, t = qat.shape
    tq = ATTN_A_TQ
    nqb = seq // tq
    win = tq + 2 * WINDOW
    assert nqb >= 2 and tq >= WINDOW
    sp = _attn_specs(nq, tq, seq, ctx_len)
    which = lambda b, i: (jnp.where(i == 0, 0, jnp.where(i == nqb - 1, 2, 1)), 0, 0)
    return pl.pallas_call(
        functools.partial(_attn_a_kernel, seq=seq),
        grid=(batch, nqb),
        in_specs=[pl.BlockSpec(memory_space=pltpu.SMEM), pl.BlockSpec(memory_space=pltpu.SMEM),
                  sp["qt"], sp["k"], sp["vt"], sp["kc"], sp["vct"],
                  pl.BlockSpec((1, win, tq), which)],
        out_specs=sp["out"],
        out_shape=jax.ShapeDtypeStruct((t, nq), BF16),
        compiler_params=_params("arbitrary", "arbitrary"),
    )(score_bound, sink, qat, ka, vat, kca, vcat, _band_bias(tq))


def _out_kernel(oa_ref, ob_ref, x_ref, g1_ref, sc2_ref, sh2_ref, ga_ref, gb_ref, gpost_ref,
                gpre2_ref, woa_ref, wob_ref, wrh_ref, wrl_ref, br_ref,
                x1_ref, h2_ref, rinfo_ref, rt_ref, tcarry_ref, tcnt_ref, cnt_ref, carry_ref):
    step = pl.program_id(0)

    @pl.when(step == 0)
    def _():
        carry_ref[...] = jnp.zeros_like(carry_ref)

    na = _rms(oa_ref[...].astype(F32)) * ga_ref[...]
    nb = _rms(ob_ref[...].astype(F32)) * gb_ref[...]
    ox = _dot(na.astype(BF16), woa_ref[...]) + _dot(nb.astype(BF16), wob_ref[...])
    x1 = x_ref[...] + g1_ref[0] * (_rms(ox) * gpost_ref[...])
    x1_ref[...] = x1
    h2 = _rms(x1) * gpre2_ref[...] * (1.0 + sc2_ref[0]) + sh2_ref[0]
    h_hi, h_lo = _split_bf16(h2)
    h2_ref[...] = h_hi

    logits = (_dot(h_hi, wrh_ref[...]) + _dot(h_lo, wrh_ref[...]) + _dot(h_hi, wrl_ref[...])
              + br_ref[...])
    tm = logits.shape[0]
    lt = logits.T
    row = lax.broadcasted_iota(jnp.int32, lt.shape, 0)
    rowf = row.astype(F32)
    big = jnp.float32(1e9)
    ninf = jnp.float32(-jnp.inf)
    colmax = lambda v: jnp.max(v, axis=0, keepdims=True)
    colmin = lambda v: jnp.min(v, axis=0, keepdims=True)
    colsum = lambda v: jnp.sum(v, axis=0, keepdims=True)

    gmask = (row >= N_EXPERTS) & (row < N_EXPERTS + N_GROUPS)
    lg = jnp.where(gmask, lt, ninf)
    gmax = colmax(lg)
    gidx = colmin(jnp.where(lg == gmax, rowf, big)) - N_EXPERTS
    g_w = 1.0 / colsum(jnp.exp(lg - gmax))
    row_group = (row // EXPERTS_PER_GROUP).astype(F32)
    emask = (row < N_EXPERTS) & (row_group == gidx)
    le = jnp.where(emask, lt, ninf)
    m1 = colmax(le)
    i1 = colmin(jnp.where(le == m1, rowf, big))
    le2 = jnp.where(rowf == i1, ninf, le)
    m2 = colmax(le2)
    i2 = colmin(jnp.where(le2 == m2, rowf, big))
    e2 = jnp.exp(m2 - m1)
    w0 = g_w / (1.0 + e2)
    w1 = g_w * e2 / (1.0 + e2)

    hit1 = rowf == i1
    hit2 = rowf == i2
    onehot = jnp.where(hit1, 1.0, jnp.where(hit2, 1.0, 0.0)).astype(F32)
    r = lax.broadcasted_iota(jnp.int32, (tm, tm), 0)
    c = lax.broadcasted_iota(jnp.int32, (tm, tm), 1)
    earlier = jnp.where(r < c, 1.0, 0.0).astype(BF16)
    within = _dot(onehot.astype(BF16), earlier)
    tile_cnt = jnp.broadcast_to(jnp.sum(onehot, axis=1, keepdims=True), (LANES, LANES))
    er = lax.broadcasted_iota(jnp.int32, (LANES, LANES), 0)
    ec = lax.broadcasted_iota(jnp.int32, (LANES, LANES), 1)
    below = jnp.where(er > ec, 1.0, 0.0).astype(BF16)
    cnt_hi = jnp.floor(tile_cnt * (1.0 / 32.0))
    cnt_lo = tile_cnt - 32.0 * cnt_hi
    run_start = 32.0 * _dot(below, cnt_hi.astype(BF16)) + _dot(below, cnt_lo.astype(BF16))
    local = within + run_start[:, 0:1]
    pos0 = colsum(jnp.where(hit1, local, 0.0))
    pos1 = colsum(jnp.where(hit2, local, 0.0))
    cnt_row = tile_cnt.T[0:1, :]
    tcarry_ref[0] = carry_ref[...]
    tcnt_ref[0] = cnt_row
    carry_ref[...] += cnt_row
    cnt_ref[...] = carry_ref[...]

    fields = jnp.concatenate([i1, i2, pos0, pos1, w0, w1, jnp.zeros((2, tm), F32)], axis=0)
    rt_ref[...] = fields
    rinfo_ref[...] = jnp.concatenate(
        [fields, jnp.zeros((LANES - 8, tm), F32)], axis=0).T


def _out_and_route(oa, ob, x2, g1, sc2, sh2, ga, gb, gpost, gpre2, woa, wob, wrh, wrl, br, seq):
    t, d = x2.shape
    tm = OUT_TM
    tpb = seq // tm
    nq = oa.shape[1]
    const = lambda shape: pl.BlockSpec(shape, lambda i: (0,) * len(shape))
    per_batch = pl.BlockSpec((1, 1, d), lambda i: (i // tpb, 0, 0))
    rows = lambda n: pl.BlockSpec((tm, n), lambda i: (i, 0))
    per_tile = pl.BlockSpec((1, 1, LANES), lambda i: (i, 0, 0))
    return pl.pallas_call(
        _out_kernel,
        grid=(t // tm,),
        in_specs=[rows(nq), rows(nq), rows(d), per_batch, per_batch, per_batch,
                  const((1, nq)), const((1, nq)), const((1, d)), const((1, d)),
                  const(woa.shape), const(wob.shape), const(wrh.shape), const(wrl.shape),
                  const((1, LANES))],
        out_specs=[rows(d), rows(d), rows(LANES), pl.BlockSpec((8, tm), lambda i: (0, i)),
                   per_tile, per_tile, const((1, LANES))],
        out_shape=[jax.ShapeDtypeStruct((t, d), F32), jax.ShapeDtypeStruct((t, d), BF16),
                   jax.ShapeDtypeStruct((t, LANES), F32), jax.ShapeDtypeStruct((8, t), F32),
                   jax.ShapeDtypeStruct((t // tm, 1, LANES), F32),
                   jax.ShapeDtypeStruct((t // tm, 1, LANES), F32),
                   jax.ShapeDtypeStruct((1, LANES), F32)],
        scratch_shapes=[pltpu.VMEM((1, LANES), F32)],
        compiler_params=_params("arbitrary"),
    )(oa, ob, x2, g1, sc2, sh2, ga, gb, gpost, gpre2, woa, wob, wrh, wrl, br)


PACK_ROWS = 8
ROW_DTYPE = F32


def _pack_rows(ref, x):
    n = x.shape[0]
    for c in range(PACK_ROWS):
        ref[pl.ds(c, n, stride=PACK_ROWS), :] = x[:, c * LANES:(c + 1) * LANES]


def _unpack_rows(ref):
    n = ref.shape[0] // PACK_ROWS
    return jnp.concatenate(
        [ref[pl.ds(c, n, stride=PACK_ROWS), :].astype(BF16) for c in range(PACK_ROWS)], axis=1)


def _for_each_run_piece(rdst_ref, rlen_ref, tile, max_len, fn):
    n_bits = max_len.bit_length()

    def run(e, local):
        length = rlen_ref[tile * N_EXPERTS + e]
        dst = rdst_ref[tile * N_EXPERTS + e]
        for b in range(n_bits):
            size = 1 << b

            @pl.when(((length >> b) & 1) == 1)
            def _():
                done = length & (size - 1)
                fn(local + done, dst + done, size)
        return local + length

    lax.fori_loop(0, N_EXPERTS, run, 0)


def _token_rows(ref, row0, n_rows):
    start = row0 * PACK_ROWS
    if not isinstance(start, int):
        start = pl.multiple_of(start, PACK_ROWS)
    return ref.at[pl.ds(start, n_rows * PACK_ROWS)]


def _dispatch_kernel(rdst_ref, rlen_ref, used_ref, ends_ref, nv_ref, h_ref, rt_ref, xs_ref,
                     sorted_ref, zero_ref, sem, zsem, *, te, n_tiles):
    k = pl.program_id(0)
    nk = pl.num_programs(0)
    tm = h_ref.shape[0]
    rows = 2 * tm
    slot = k % 2

    def wait_slot(s):
        pltpu.make_async_copy(sorted_ref.at[s], _token_rows(xs_ref, 0, rows), sem.at[s]).wait()

    @pl.when(k == 0)
    def _():
        zero_ref[...] = jnp.zeros_like(zero_ref)

        def pad_copy(row0, size):
            return pltpu.make_async_copy(_token_rows(zero_ref, 0, size),
                                         _token_rows(xs_ref, row0, size), zsem)

        def for_each_pad_tile(fn):
            def expert_pad(e, carry):
                start = used_ref[e]
                length = ends_ref[e] - start
                for b in range((te - 1).bit_length()):
                    size = 1 << b

                    @pl.when(((length >> b) & 1) == 1)
                    def _():
                        fn(pad_copy(start + (length & (size - 1)), size))
                return carry

            def tail_pad(j, carry):
                fn(pad_copy(j * te, te))
                return carry

            lax.fori_loop(0, N_EXPERTS, expert_pad, 0)
            lax.fori_loop(nv_ref[0], n_tiles, tail_pad, 0)

        for_each_pad_tile(lambda cp: cp.start())
        for_each_pad_tile(lambda cp: cp.wait())

    @pl.when(k >= 2)
    def _():
        wait_slot(slot)

    pos0 = rt_ref[2:3, :]
    pos1 = rt_ref[3:4, :]
    r = lax.broadcasted_iota(jnp.int32, (rows, tm), 0).astype(F32)
    perm = jnp.where((r == pos0) | (r == pos1), 1.0, 0.0).astype(BF16)
    srt = _dot(perm, h_ref[...].astype(BF16))
    buf = sorted_ref.at[slot]
    _pack_rows(buf, srt)

    def copy_piece(local, dst, size):
        pltpu.make_async_copy(_token_rows(buf, local, size), _token_rows(xs_ref, dst, size),
                              sem.at[slot]).start()

    _for_each_run_piece(rdst_ref, rlen_ref, k, tm, copy_piece)

    @pl.when(k == nk - 1)
    def _():
        wait_slot(slot)

        @pl.when(nk >= 2)
        def _():
            wait_slot(1 - slot)


def _dispatch(run_dst, run_len, used, ends, n_valid, h2, rt, n_rows):
    t, d = h2.shape
    assert d == PACK_ROWS * LANES
    tm = MOVE_TM
    te = EXPERT_TE
    return pl.pallas_call(
        functools.partial(_dispatch_kernel, te=te, n_tiles=n_rows // te),
        grid_spec=pltpu.PrefetchScalarGridSpec(
            num_scalar_prefetch=5,
            grid=(t // tm,),
            in_specs=[pl.BlockSpec((tm, d), lambda i, *_: (i, 0)),
                      pl.BlockSpec((8, tm), lambda i, *_: (0, i))],
            out_specs=pl.BlockSpec(memory_space=pl.ANY),
            scratch_shapes=[pltpu.VMEM((2, 2 * tm * PACK_ROWS, LANES), ROW_DTYPE),
                            pltpu.VMEM((te * PACK_ROWS, LANES), ROW_DTYPE),
                            pltpu.SemaphoreType.DMA((2,)), pltpu.SemaphoreType.DMA(())]),
        out_shape=jax.ShapeDtypeStruct((n_rows * PACK_ROWS, LANES), ROW_DTYPE),
        compiler_params=_params("arbitrary"),
    )(run_dst, run_len, used, ends, n_valid, h2, rt)


def _expert_kernel(te_ref, nv_ref, xs_ref, wg_ref, wu_ref, wd_ref, ys_ref, wg_bf, wu_bf, wd_bf):
    j = pl.program_id(0)
    valid = j < nv_ref[0]
    changed = (j == 0) | (te_ref[j] != te_ref[jnp.maximum(j - 1, 0)])

    @pl.when(valid & changed)
    def _():
        wg_bf[...] = wg_ref[0].astype(BF16)
        wu_bf[...] = wu_ref[0].astype(BF16)
        wd_bf[...] = wd_ref[0].astype(BF16)

    @pl.when(valid)
    def _():
        chunk = xs_ref.shape[0] // EXPERT_CHUNKS
        for r in range(EXPERT_CHUNKS):
            rows = pl.ds(r * chunk, chunk)
            xb = _unpack_rows(xs_ref.at[rows])
            gate = _dot(xb, wg_bf[...])
            up = _dot(xb, wu_bf[...])
            act = gate * jax.nn.sigmoid(gate) * up
            _pack_rows(ys_ref.at[rows], _dot(act.astype(BF16), wd_bf[...]))

    @pl.when(jnp.logical_not(valid))
    def _():
        ys_ref[...] = jnp.zeros_like(ys_ref)


def _expert_mlp(tile_expert, n_valid, xs, w_gate, w_up, w_down):
    te = EXPERT_TE
    d, ff = w_gate.shape[1:]
    n_rows = xs.shape[0] // PACK_ROWS
    blk = (te * PACK_ROWS, LANES)
    tile = lambda j, e, nv: (jnp.minimum(j, nv[0] - 1), 0)
    wsel = lambda j, e, nv: (e[j], 0, 0)
    return pl.pallas_call(
        _expert_kernel,
        grid_spec=pltpu.PrefetchScalarGridSpec(
            num_scalar_prefetch=2,
            grid=(n_rows // te,),
            in_specs=[pl.BlockSpec(blk, tile),
                      pl.BlockSpec((1, d, ff), wsel), pl.BlockSpec((1, d, ff), wsel),
                      pl.BlockSpec((1, ff, d), wsel)],
            out_specs=pl.BlockSpec(blk, lambda j, e, nv: (j, 0)),
            scratch_shapes=[pltpu.VMEM((d, ff), BF16), pltpu.VMEM((d, ff), BF16),
                            pltpu.VMEM((ff, d), BF16)]),
        out_shape=jax.ShapeDtypeStruct(xs.shape, ROW_DTYPE),
        compiler_params=_params("arbitrary"),
    )(tile_expert, n_valid, xs, w_gate, w_up, w_down)


def _combine_kernel(rdst_ref, rlen_ref, x1_ref, rinfo_ref, g2_ref, gpost_ref, ys_ref, o_ref,
                    gath_ref, sem):
    k = pl.program_id(0)
    nk = pl.num_programs(0)
    tm = x1_ref.shape[0]
    rows = 2 * tm
    slot = k % 2

    def gather_runs(tile, s):
        buf = gath_ref.at[s]

        def copy_piece(local, src, size):
            pltpu.make_async_copy(_token_rows(ys_ref, src, size), _token_rows(buf, local, size),
                                  sem.at[s]).start()

        _for_each_run_piece(rdst_ref, rlen_ref, tile, tm, copy_piece)

    @pl.when(k == 0)
    def _():
        gather_runs(0, 0)

    @pl.when(k + 1 < nk)
    def _():
        gather_runs(k + 1, 1 - slot)

    buf = gath_ref.at[slot]
    pltpu.make_async_copy(_token_rows(ys_ref, 0, rows), buf, sem.at[slot]).wait()
    g = _unpack_rows(buf)
    info = rinfo_ref[...]
    col = lax.broadcasted_iota(jnp.int32, (tm, rows), 1).astype(F32)
    pick0 = jnp.where(col == info[:, 2:3], 1.0, 0.0).astype(BF16)
    pick1 = jnp.where(col == info[:, 3:4], 1.0, 0.0).astype(BF16)
    fx = info[:, 4:5] * _dot(pick0, g) + info[:, 5:6] * _dot(pick1, g)
    o_ref[...] = x1_ref[...] + g2_ref[0] * (_rms(fx) * gpost_ref[...])


def _combine(run_dst, run_len, x1, rinfo, g2, gpost, ys, seq):
    t, d = x1.shape
    tm = MOVE_TM
    tpb = seq // tm
    return pl.pallas_call(
        _combine_kernel,
        grid_spec=pltpu.PrefetchScalarGridSpec(
            num_scalar_prefetch=2,
            grid=(t // tm,),
            in_specs=[pl.BlockSpec((tm, d), lambda i, *_: (i, 0)),
                      pl.BlockSpec((tm, LANES), lambda i, *_: (i, 0)),
                      pl.BlockSpec((1, 1, d), lambda i, *_: (i // tpb, 0, 0)),
                      pl.BlockSpec((1, d), lambda i, *_: (0, 0)),
                      pl.BlockSpec(memory_space=pl.ANY)],
            out_specs=pl.BlockSpec((tm, d), lambda i, *_: (i, 0)),
            scratch_shapes=[pltpu.VMEM((2, 2 * tm * PACK_ROWS, LANES), ROW_DTYPE),
                            pltpu.SemaphoreType.DMA((2,))]),
        out_shape=jax.ShapeDtypeStruct((t, d), F32),
        compiler_params=_params("arbitrary"),
    )(run_dst, run_len, x1, rinfo, g2, gpost, ys)


def _rope_tables(seq):
    pos = jnp.arange(seq, dtype=jnp.int32)
    row = (pos // GRID_W).astype(F32)
    col = (pos % GRID_W).astype(F32)
    axis_dim = HEAD_DIM // 2
    inv_freq = ROPE_THETA ** (-jnp.arange(0, axis_dim, 2, dtype=F32) / axis_dim)
    ang = jnp.concatenate([row[:, None] * inv_freq, col[:, None] * inv_freq], axis=-1)
    pair = (jnp.arange(LANES) % HEAD_DIM) // 2
    cos = jnp.cos(ang)[:, pair]
    sin = jnp.sin(ang)[:, pair]
    even = (jnp.arange(LANES) % 2) == 0
    return cos, jnp.where(even, -sin, 0.0), jnp.where(even, 0.0, sin)


def _segment_ones(n):
    seg = jnp.arange(n) // HEAD_DIM
    return (seg[:, None] == seg[None, :]).astype(BF16)


def kernel(x, c, ctx, c_ctx, w_mod, b_mod, attn_pre_norm, attn_post_norm, w_in, a_sink,
           b_q_norm, b_k_norm, a_out_norm, b_out_norm, w_out, ffn_pre_norm, ffn_post_norm,
           w_group, b_group, w_router, b_router, w_gate, w_up, w_down):
    batch, seq, d = x.shape
    ctx_len = ctx.shape[1]
    assert w_mod.shape[0] == 1, "single-layer stack only (context stream is never updated)"
    assert seq % ATTN_A_TQ == 0 and seq >= ATTN_A_TQ + 2 * WINDOW
    assert seq % PROJ_TM == 0 and seq % ATTN_B_TQ == 0 and seq % OUT_TM == 0 and seq % MOVE_TM == 0
    t = batch * seq
    nq = d // 2
    nkv = nq // KV_GROUP
    assert nkv == LANES and w_in.shape[2] == 2 * nq + 4 * nkv

    cc = jnp.concatenate([c, c_ctx[None, :], jnp.zeros((16 - batch - 1, d), F32)], axis=0)
    mod = _modulation(cc, w_mod[0], b_mod[0])
    sh1, sc1, g1, sh2, sc2, g2 = (m.reshape(batch, 1, d) for m in jnp.split(mod[:batch], 6, axis=-1))
    csh1, csc1 = (m.reshape(1, d) for m in jnp.split(mod[batch], 6)[:2])

    x2 = x.reshape(t, d)
    c2 = ctx.reshape(batch * ctx_len, d)
    gpre = attn_pre_norm[0].reshape(1, d)
    w_in_bf = w_in[0].astype(BF16)
    kv_cols = jnp.concatenate([w_in_bf[:, nq:nq + 2 * nkv], w_in_bf[:, 2 * nq + 2 * nkv:]], axis=1)
    qn = jnp.tile(b_q_norm[0], nq // HEAD_DIM).reshape(1, nq)
    kn = jnp.tile(b_k_norm[0], nkv // HEAD_DIM).reshape(1, nkv)
    seg_q, seg_k = _segment_ones(nq), _segment_ones(nkv)
    qat, ka, vat, qbt, kb, vbt, stats = _project_latents(
        x2, sc1, sh1, gpre, w_in_bf, _rope_tables(seq), qn, kn, seg_q, seg_k, seq)
    kca, vcat, kcb, vcbt, ctx_stats = _project_context(
        c2, csc1, csh1, gpre, kv_cols, kn, seg_k, ctx_len)

    q_sq = jnp.max(stats[:, 0, 0])
    k_sq = jnp.maximum(jnp.max(stats[:, 0, 1]), jnp.max(ctx_stats[:, 0, 1]))
    bound_a = jnp.maximum(1.01 * jnp.sqrt(q_sq * k_sq), jnp.max(a_sink[0]) * LOG2_E).reshape(1)
    oa = _attention_a(bound_a, a_sink[0], qat, ka, vat, kca, vcat, batch, seq, ctx_len)
    score_bound = (1.01 * HEAD_DIM ** 0.5 * LOG2_E
                   * jnp.max(jnp.abs(b_q_norm[0])) * jnp.max(jnp.abs(b_k_norm[0]))).reshape(1)
    ob = _attention_b(score_bound, qbt, kb, vbt, kcb, vcbt, batch, seq, ctx_len)

    w_out_bf = w_out[0].astype(BF16)
    w_r = jnp.zeros((d, LANES), F32)
    w_r = w_r.at[:, :N_EXPERTS].set(w_router[0]).at[:, N_EXPERTS:N_EXPERTS + N_GROUPS].set(w_group[0])
    w_r_hi = w_r.astype(BF16)
    w_r_lo = (w_r - w_r_hi.astype(F32)).astype(BF16)
    b_r = jnp.zeros((1, LANES), F32)
    b_r = b_r.at[0, :N_EXPERTS].set(b_router[0]).at[0, N_EXPERTS:N_EXPERTS + N_GROUPS].set(b_group[0])
    x1, h2, rinfo, rt, tcarry, tcnt, counts = _out_and_route(
        oa, ob, x2, g1, sc2, sh2, a_out_norm[0].reshape(1, nq), b_out_norm[0].reshape(1, nq),
        attn_post_norm[0].reshape(1, d), ffn_pre_norm[0].reshape(1, d),
        w_out_bf[:nq], w_out_bf[nq:], w_r_hi, w_r_lo, b_r, seq)

    te = EXPERT_TE
    n_tiles = -(-(2 * t + N_EXPERTS * (te - 1)) // te)
    n_rows = n_tiles * te
    cnt = counts[0, :N_EXPERTS].astype(jnp.int32)
    padded = ((cnt + te - 1) // te) * te
    ends = jnp.cumsum(padded)
    offs = ends - padded
    run_dst = (offs[None, :] + tcarry[:, 0, :N_EXPERTS].astype(jnp.int32)).reshape(-1)
    run_len = tcnt[:, 0, :N_EXPERTS].astype(jnp.int32).reshape(-1)
    n_valid = (ends[-1] // te).astype(jnp.int32).reshape(1)
    tile_start = jnp.arange(n_tiles, dtype=jnp.int32) * te
    tile_expert = jnp.sum(ends[None, :] <= tile_start[:, None], axis=1).astype(jnp.int32)
    last_expert = tile_expert[jnp.maximum(n_valid[0] - 1, 0)]
    tile_expert = jnp.where(tile_start < ends[-1], tile_expert, last_expert)

    xs = _dispatch(run_dst, run_len, (offs + cnt).astype(jnp.int32), ends.astype(jnp.int32),
                   n_valid, h2, rt, n_rows)
    ys = _expert_mlp(tile_expert, n_valid, xs, w_gate[0], w_up[0], w_down[0])
    out = _combine(run_dst, run_len, x1, rinfo, g2, ffn_post_norm[0].reshape(1, d), ys, seq)
    return out.reshape(batch, seq, d)
```

```python
import functools

import jax
import jax.numpy as jnp
from jax import lax
from jax.experimental import pallas as pl
from jax.experimental.pallas import tpu as pltpu

F32 = jnp.float32
BF16 = jnp.bfloat16

GRID_W = 64
HEAD_DIM = 64
KV_GROUP = 4
WINDOW = 128
ROPE_THETA = 10000.0
N_GROUPS = 4
EXPERTS_PER_GROUP = 8
N_EXPERTS = N_GROUPS * EXPERTS_PER_GROUP
EPS = 1e-6
NEG_INF = -1e30
LOG2_E = 1.4426950408889634
SAFE_SOFTMAX_SHIFT = 40.0

LANES = 128
V7X_VMEM_LIMIT = 56 * 1024 * 1024

PROJ_TM = 512
ATTN_A_TQ = 256
ATTN_B_TQ = 256
OUT_TM = 512
EXPERT_TE = 512
MOVE_TM = OUT_TM


def _params(*sem):
    return pltpu.CompilerParams(dimension_semantics=sem, vmem_limit_bytes=V7X_VMEM_LIMIT)


def _dot(a, b):
    return jnp.dot(a, b, preferred_element_type=F32)


def _dot_nt(a, b):
    return lax.dot_general(a, b, (((1,), (1,)), ((), ())), preferred_element_type=F32)


def _rms(x):
    return x * lax.rsqrt(jnp.mean(x * x, axis=-1, keepdims=True) + EPS)


def _split_bf16(x):
    hi = x.astype(BF16)
    lo = (x - hi.astype(F32)).astype(BF16)
    return hi, lo


def _mod_kernel(c_ref, w_ref, b_ref, o_ref):
    cc = c_ref[...]
    s = cc * jax.nn.sigmoid(cc)
    s_hi, s_lo = _split_bf16(s)
    w_hi, w_lo = _split_bf16(w_ref[...])
    o_ref[...] = _dot(s_hi, w_hi) + _dot(s_lo, w_hi) + _dot(s_hi, w_lo) + b_ref[...]


def _modulation(cc, w_mod, b_mod):
    rows, d = cc.shape
    n = w_mod.shape[1]
    bn = 1024
    return pl.pallas_call(
        _mod_kernel,
        grid=(n // bn,),
        in_specs=[pl.BlockSpec((rows, d), lambda i: (0, 0)),
                  pl.BlockSpec((d, bn), lambda i: (0, i)),
                  pl.BlockSpec((1, bn), lambda i: (0, i))],
        out_specs=pl.BlockSpec((rows, bn), lambda i: (0, i)),
        out_shape=jax.ShapeDtypeStruct((rows, n), F32),
        compiler_params=_params("arbitrary"),
    )(cc, w_mod, b_mod.reshape(1, n))


def _rope(x, cos, sin_a, sin_b):
    return x * cos + pltpu.roll(x, LANES - 1, 1) * sin_a + pltpu.roll(x, 1, 1) * sin_b


def _head_norm(x, seg_ref, gain):
    ss = _dot((x * x).astype(BF16), seg_ref[...])
    return x * lax.rsqrt(ss * (1.0 / HEAD_DIM) + EPS) * gain


def _max_head_sq_norm(x, seg_ref):
    ss = _dot((x * x).astype(BF16), seg_ref[...])
    return jnp.max(jnp.max(ss, axis=1, keepdims=True), axis=0, keepdims=True)


def _norm_stats(q_sq, k_sq):
    lane = lax.broadcasted_iota(jnp.int32, (1, LANES), 1)
    zero = jnp.zeros((1, LANES), F32)
    return jnp.where(lane == 0, q_sq, zero) + jnp.where(lane == 1, k_sq, zero)


def _rope_t(xt, cos_t, sin_a_t, sin_b_t):
    return (xt * cos_t + pltpu.roll(xt, LANES - 1, 0) * sin_a_t
            + pltpu.roll(xt, 1, 0) * sin_b_t)


def _proj_kernel(x_ref, sc_ref, sh_ref, gpre_ref, w_ref, cos_ref, sa_ref, sb_ref,
                 cos_t_ref, sa_t_ref, sb_t_ref, qn_ref, kn_ref, seg_q_ref, seg_k_ref,
                 qat_ref, ka_ref, vat_ref, qbt_ref, kb_ref, vbt_ref, stats_ref):
    h = _rms(x_ref[...]) * gpre_ref[...] * (1.0 + sc_ref[0]) + sh_ref[0]
    p = _dot(h.astype(BF16), w_ref[...])
    cos, sa, sb = cos_ref[...], sa_ref[...], sb_ref[...]
    q_scale = HEAD_DIM ** -0.5 * LOG2_E
    cos_t, sa_t, sb_t = cos_t_ref[...] * q_scale, sa_t_ref[...] * q_scale, sb_t_ref[...] * q_scale
    nq = qat_ref.shape[0]
    for c in range(nq // LANES):
        qat_ref[c * LANES:(c + 1) * LANES, :] = _rope_t(
            p[:, c * LANES:(c + 1) * LANES].T, cos_t, sa_t, sb_t).astype(BF16)
    o = nq
    ka_ref[...] = _rope(p[:, o:o + LANES], cos, sa, sb).astype(BF16)
    vat_ref[...] = p[:, o + LANES:o + 2 * LANES].T.astype(BF16)
    stats_ref[0] = _norm_stats(_max_head_sq_norm(p[:, 0:nq], seg_q_ref) * (q_scale * q_scale),
                               _max_head_sq_norm(p[:, o:o + LANES], seg_k_ref))
    o += 2 * LANES
    qb = _head_norm(p[:, o:o + nq], seg_q_ref, qn_ref[...])
    for c in range(nq // LANES):
        qbt_ref[c * LANES:(c + 1) * LANES, :] = _rope_t(
            qb[:, c * LANES:(c + 1) * LANES].T, cos_t, sa_t, sb_t).astype(BF16)
    o += nq
    kb = _head_norm(p[:, o:o + LANES], seg_k_ref, kn_ref[...])
    kb_ref[...] = _rope(kb, cos, sa, sb).astype(BF16)
    vbt_ref[...] = p[:, o + LANES:o + 2 * LANES].T.astype(BF16)


def _ctx_proj_kernel(x_ref, sc_ref, sh_ref, gpre_ref, w_ref, kn_ref, seg_k_ref,
                     ka_ref, vat_ref, kb_ref, vbt_ref, stats_ref):
    h = _rms(x_ref[...]) * gpre_ref[...] * (1.0 + sc_ref[...]) + sh_ref[...]
    p = _dot(h.astype(BF16), w_ref[...])
    stats_ref[0] = _norm_stats(0.0, _max_head_sq_norm(p[:, 0:LANES], seg_k_ref))
    ka_ref[...] = p[:, 0:LANES].astype(BF16)
    vat_ref[...] = p[:, LANES:2 * LANES].T.astype(BF16)
    kb_ref[...] = _head_norm(p[:, 2 * LANES:3 * LANES], seg_k_ref, kn_ref[...]).astype(BF16)
    vbt_ref[...] = p[:, 3 * LANES:4 * LANES].T.astype(BF16)


def _project_latents(x2, sc, sh, gpre, w_in, tables, qn, kn, seg_q, seg_k, seq):
    t, d = x2.shape
    tm = PROJ_TM
    tpb = seq // tm
    nq = seg_q.shape[0]
    const = lambda shape: pl.BlockSpec(shape, lambda i: (0,) * len(shape))
    per_batch = pl.BlockSpec((1, 1, d), lambda i: (i // tpb, 0, 0))
    table = pl.BlockSpec((tm, LANES), lambda i: (i % tpb, 0))
    table_t = pl.BlockSpec((LANES, tm), lambda i: (0, i % tpb))
    k_spec = pl.BlockSpec((tm, LANES), lambda i: (i, 0))
    k_shape = jax.ShapeDtypeStruct((t, LANES), BF16)
    vt_spec = pl.BlockSpec((LANES, tm), lambda i: (0, i))
    vt_shape = jax.ShapeDtypeStruct((LANES, t), BF16)
    qt_spec = pl.BlockSpec((nq, tm), lambda i: (0, i))
    qt_shape = jax.ShapeDtypeStruct((nq, t), BF16)
    return pl.pallas_call(
        _proj_kernel,
        grid=(t // tm,),
        in_specs=[pl.BlockSpec((tm, d), lambda i: (i, 0)), per_batch, per_batch, const((1, d)),
                  const(w_in.shape), table, table, table, table_t, table_t, table_t,
                  const((1, nq)), const((1, LANES)), const(seg_q.shape), const(seg_k.shape)],
        out_specs=[qt_spec, k_spec, vt_spec, qt_spec, k_spec, vt_spec,
                   pl.BlockSpec((1, 1, LANES), lambda i: (i, 0, 0))],
        out_shape=[qt_shape, k_shape, vt_shape, qt_shape, k_shape, vt_shape,
                   jax.ShapeDtypeStruct((t // tm, 1, LANES), F32)],
        compiler_params=_params("arbitrary"),
    )(x2, sc, sh, gpre, w_in, *tables, qn, kn, seg_q, seg_k)


def _project_context(c2, sc, sh, gpre, w_kv, kn, seg_k, ctx_len):
    t, d = c2.shape
    const = lambda shape: pl.BlockSpec(shape, lambda i: (0,) * len(shape))
    k_spec = pl.BlockSpec((ctx_len, LANES), lambda i: (i, 0))
    k_shape = jax.ShapeDtypeStruct((t, LANES), BF16)
    vt_spec = pl.BlockSpec((LANES, ctx_len), lambda i: (0, i))
    vt_shape = jax.ShapeDtypeStruct((LANES, t), BF16)
    return pl.pallas_call(
        _ctx_proj_kernel,
        grid=(t // ctx_len,),
        in_specs=[pl.BlockSpec((ctx_len, d), lambda i: (i, 0)), const((1, d)), const((1, d)),
                  const((1, d)), const(w_kv.shape), const((1, LANES)), const(seg_k.shape)],
        out_specs=[k_spec, vt_spec, k_spec, vt_spec,
                   pl.BlockSpec((1, 1, LANES), lambda i: (i, 0, 0))],
        out_shape=[k_shape, vt_shape, k_shape, vt_shape,
                   jax.ShapeDtypeStruct((t // ctx_len, 1, LANES), F32)],
        compiler_params=_params("arbitrary"),
    )(c2, sc, sh, gpre, w_kv, kn, seg_k)


def _attend_t(w, k, kc, vt, vct, shift=None, bias=None, sink=None):
    st = _dot(k, w)
    sct = _dot(kc, w)
    if bias is not None:
        tq = bias.shape[1]
        st = jnp.concatenate([st[:, c * tq:(c + 1) * tq] + bias
                              for c in range(st.shape[1] // tq)], axis=1)
    if shift is None:
        shift = jnp.maximum(jnp.max(st, axis=0, keepdims=True),
                            jnp.max(sct, axis=0, keepdims=True))
        if sink is not None:
            shift = jnp.maximum(shift, sink)
    pt = jnp.exp2(st - shift)
    pct = jnp.exp2(sct - shift)
    denom = jnp.sum(pt, axis=0, keepdims=True) + jnp.sum(pct, axis=0, keepdims=True)
    if sink is not None:
        denom = denom + jnp.exp2(sink - shift)
    o2 = _dot(vt, pt.astype(BF16)) + _dot(vct, pct.astype(BF16))
    return o2, denom


def _all_heads_t(qt_ref, o_ref, attend):
    tq = qt_ref.shape[1]
    n_kv = LANES // HEAD_DIM
    zeros = jnp.zeros((HEAD_DIM, tq), BF16)
    outs = []
    for h in range(qt_ref.shape[0] // HEAD_DIM):
        g = h // KV_GROUP
        qh = qt_ref[h * HEAD_DIM:(h + 1) * HEAD_DIM, :]
        w = jnp.concatenate([zeros] * g + [qh] + [zeros] * (n_kv - 1 - g), axis=0)
        o2, denom = attend(h, w)
        outs.append(o2[g * HEAD_DIM:(g + 1) * HEAD_DIM, :] / denom)
    o_ref[...] = jnp.concatenate(outs, axis=0).T.astype(BF16)


def _all_heads_fused_t(qt_ref, o_ref, attend):
    tq = qt_ref.shape[1]
    n_heads = qt_ref.shape[0] // HEAD_DIM
    n_kv = LANES // HEAD_DIM
    rows = []
    for g in range(n_kv):
        heads = [qt_ref[h * HEAD_DIM:(h + 1) * HEAD_DIM, :] if h // KV_GROUP == g
                 else jnp.zeros((HEAD_DIM, tq), BF16) for h in range(n_heads)]
        rows.append(jnp.concatenate(heads, axis=1))
    w = jnp.concatenate(rows, axis=0)
    o2, denom = attend(0, w)
    o2 = o2 / denom
    outs = [o2[(h // KV_GROUP) * HEAD_DIM:(h // KV_GROUP + 1) * HEAD_DIM, h * tq:(h + 1) * tq]
            for h in range(n_heads)]
    o_ref[...] = jnp.concatenate(outs, axis=0).T.astype(BF16)


def _attn_b_kernel(bound_ref, qt_ref, k_ref, vt_ref, kc_ref, vct_ref, o_ref):
    k, kc, vt, vct = k_ref[...], kc_ref[...], vt_ref[...], vct_ref[...]
    bound = bound_ref[0]

    @pl.when(bound <= SAFE_SOFTMAX_SHIFT)
    def _():
        _all_heads_fused_t(qt_ref, o_ref, lambda h, w: _attend_t(w, k, kc, vt, vct, shift=bound))

    @pl.when(jnp.logical_not(bound <= SAFE_SOFTMAX_SHIFT))
    def _():
        _all_heads_t(qt_ref, o_ref, lambda h, w: _attend_t(w, k, kc, vt, vct))


def _attn_specs(nq, tq, seq, ctx_len):
    nqb = seq // tq
    return dict(
        qt=pl.BlockSpec((nq, tq), lambda b, i: (0, b * nqb + i)),
        k=pl.BlockSpec((seq, LANES), lambda b, i: (b, 0)),
        vt=pl.BlockSpec((LANES, seq), lambda b, i: (0, b)),
        kc=pl.BlockSpec((ctx_len, LANES), lambda b, i: (b, 0)),
        vct=pl.BlockSpec((LANES, ctx_len), lambda b, i: (0, b)),
        out=pl.BlockSpec((tq, nq), lambda b, i: (b * nqb + i, 0)))


def _attention_b(score_bound, qbt, kb, vbt, kcb, vcbt, batch, seq, ctx_len):
    nq, t = qbt.shape
    tq = ATTN_B_TQ
    sp = _attn_specs(nq, tq, seq, ctx_len)
    return pl.pallas_call(
        _attn_b_kernel,
        grid=(batch, seq // tq),
        in_specs=[pl.BlockSpec(memory_space=pltpu.SMEM),
                  sp["qt"], sp["k"], sp["vt"], sp["kc"], sp["vct"]],
        out_specs=sp["out"],
        out_shape=jax.ShapeDtypeStruct((t, nq), BF16),
        compiler_params=_params("arbitrary", "arbitrary"),
    )(score_bound, qbt, kb, vbt, kcb, vcbt)


def _attn_a_kernel(bound_ref, sink_ref, qt_ref, k_ref, vt_ref, kc_ref, vct_ref, bias_ref, o_ref,
                   *, seq):
    i = pl.program_id(1)
    tq = qt_ref.shape[1]
    n_heads = qt_ref.shape[0] // HEAD_DIM
    win = tq + 2 * WINDOW
    start = pl.multiple_of(jnp.clip(i * tq - WINDOW, 0, seq - win), WINDOW)
    k = k_ref[pl.ds(start, win), :]
    vt = vt_ref[:, pl.ds(start, win)]
    kc, vct = kc_ref[...], vct_ref[...]
    sinks = [sink_ref[h] * LOG2_E for h in range(n_heads)]

    bound = bound_ref[0]
    small = bound <= SAFE_SOFTMAX_SHIFT

    @pl.when(small)
    def _():
        sink_row = jnp.concatenate([jnp.full((1, tq), s, F32) for s in sinks], axis=1)
        _all_heads_fused_t(qt_ref, o_ref, lambda h, w: _attend_t(
            w, k, kc, vt, vct, shift=bound, bias=bias_ref[0], sink=sink_row))

    @pl.when(jnp.logical_not(small))
    def _():
        _all_heads_t(qt_ref, o_ref, lambda h, w: _attend_t(
            w, k, kc, vt, vct, bias=bias_ref[0], sink=sinks[h]))


def _band_bias(tq):
    win = tq + 2 * WINDOW
    r = jnp.arange(win, dtype=jnp.int32)[:, None]
    j = jnp.arange(tq, dtype=jnp.int32)[None, :]
    tables = [jnp.where(jnp.abs(off + r - j) <= WINDOW, 0.0, NEG_INF)
              for off in (0, -WINDOW, -2 * WINDOW)]
    return jnp.stack(tables).astype(F32)


def _attention_a(score_bound, sink, qat, ka, vat, kca, vcat, batch, seq, ctx_len):
    nq, t = qat.shape
    tq = ATTN_A_TQ
    nqb = seq // tq
    win = tq + 2 * WINDOW
    assert nqb >= 2 and tq >= WINDOW
    sp = _attn_specs(nq, tq, seq, ctx_len)
    which = lambda b, i: (jnp.where(i == 0, 0, jnp.where(i == nqb - 1, 2, 1)), 0, 0)
    return pl.pallas_call(
        functools.partial(_attn_a_kernel, seq=seq),
        grid=(batch, nqb),
        in_specs=[pl.BlockSpec(memory_space=pltpu.SMEM), pl.BlockSpec(memory_space=pltpu.SMEM),
                  sp["qt"], sp["k"], sp["vt"], sp["kc"], sp["vct"],
                  pl.BlockSpec((1, win, tq), which)],
        out_specs=sp["out"],
        out_shape=jax.ShapeDtypeStruct((t, nq), BF16),
        compiler_params=_params("arbitrary", "arbitrary"),
    )(score_bound, sink, qat, ka, vat, kca, vcat, _band_bias(tq))


def _out_kernel(oa_ref, ob_ref, x_ref, g1_ref, sc2_ref, sh2_ref, ga_ref, gb_ref, gpost_ref,
                gpre2_ref, woa_ref, wob_ref, wrh_ref, wrl_ref, br_ref,
                x1_ref, h2_ref, rinfo_ref, rt_ref, tcarry_ref, tcnt_ref, cnt_ref, carry_ref):
    step = pl.program_id(0)

    @pl.when(step == 0)
    def _():
        carry_ref[...] = jnp.zeros_like(carry_ref)

    na = _rms(oa_ref[...].astype(F32)) * ga_ref[...]
    nb = _rms(ob_ref[...].astype(F32)) * gb_ref[...]
    ox = _dot(na.astype(BF16), woa_ref[...]) + _dot(nb.astype(BF16), wob_ref[...])
    x1 = x_ref[...] + g1_ref[0] * (_rms(ox) * gpost_ref[...])
    x1_ref[...] = x1
    h2 = _rms(x1) * gpre2_ref[...] * (1.0 + sc2_ref[0]) + sh2_ref[0]
    h_hi, h_lo = _split_bf16(h2)
    h2_ref[...] = h_hi

    logits = (_dot(h_hi, wrh_ref[...]) + _dot(h_lo, wrh_ref[...]) + _dot(h_hi, wrl_ref[...])
              + br_ref[...])
    tm = logits.shape[0]
    lt = logits.T
    row = lax.broadcasted_iota(jnp.int32, lt.shape, 0)
    rowf = row.astype(F32)
    big = jnp.float32(1e9)
    ninf = jnp.float32(-jnp.inf)
    colmax = lambda v: jnp.max(v, axis=0, keepdims=True)
    colmin = lambda v: jnp.min(v, axis=0, keepdims=True)
    colsum = lambda v: jnp.sum(v, axis=0, keepdims=True)

    gmask = (row >= N_EXPERTS) & (row < N_EXPERTS + N_GROUPS)
    lg = jnp.where(gmask, lt, ninf)
    gmax = colmax(lg)
    gidx = colmin(jnp.where(lg == gmax, rowf, big)) - N_EXPERTS
    g_w = 1.0 / colsum(jnp.exp(lg - gmax))
    row_group = (row // EXPERTS_PER_GROUP).astype(F32)
    emask = (row < N_EXPERTS) & (row_group == gidx)
    le = jnp.where(emask, lt, ninf)
    m1 = colmax(le)
    i1 = colmin(jnp.where(le == m1, rowf, big))
    le2 = jnp.where(rowf == i1, ninf, le)
    m2 = colmax(le2)
    i2 = colmin(jnp.where(le2 == m2, rowf, big))
    e2 = jnp.exp(m2 - m1)
    w0 = g_w / (1.0 + e2)
    w1 = g_w * e2 / (1.0 + e2)

    hit1 = rowf == i1
    hit2 = rowf == i2
    onehot = jnp.where(hit1, 1.0, jnp.where(hit2, 1.0, 0.0)).astype(F32)
    r = lax.broadcasted_iota(jnp.int32, (tm, tm), 0)
    c = lax.broadcasted_iota(jnp.int32, (tm, tm), 1)
    earlier = jnp.where(r < c, 1.0, 0.0).astype(BF16)
    within = _dot(onehot.astype(BF16), earlier)
    tile_cnt = jnp.broadcast_to(jnp.sum(onehot, axis=1, keepdims=True), (LANES, LANES))
    er = lax.broadcasted_iota(jnp.int32, (LANES, LANES), 0)
    ec = lax.broadcasted_iota(jnp.int32, (LANES, LANES), 1)
    below = jnp.where(er > ec, 1.0, 0.0).astype(BF16)
    cnt_hi = jnp.floor(tile_cnt * (1.0 / 32.0))
    cnt_lo = tile_cnt - 32.0 * cnt_hi
    run_start = 32.0 * _dot(below, cnt_hi.astype(BF16)) + _dot(below, cnt_lo.astype(BF16))
    local = within + run_start[:, 0:1]
    pos0 = colsum(jnp.where(hit1, local, 0.0))
    pos1 = colsum(jnp.where(hit2, local, 0.0))
    cnt_row = tile_cnt.T[0:1, :]
    tcarry_ref[0] = carry_ref[...]
    tcnt_ref[0] = cnt_row
    carry_ref[...] += cnt_row
    cnt_ref[...] = carry_ref[...]

    fields = jnp.concatenate([i1, i2, pos0, pos1, w0, w1, jnp.zeros((2, tm), F32)], axis=0)
    rt_ref[...] = fields
    rinfo_ref[...] = jnp.concatenate(
        [fields, jnp.zeros((LANES - 8, tm), F32)], axis=0).T


def _out_and_route(oa, ob, x2, g1, sc2, sh2, ga, gb, gpost, gpre2, woa, wob, wrh, wrl, br, seq):
    t, d = x2.shape
    tm = OUT_TM
    tpb = seq // tm
    nq = oa.shape[1]
    const = lambda shape: pl.BlockSpec(shape, lambda i: (0,) * len(shape))
    per_batch = pl.BlockSpec((1, 1, d), lambda i: (i // tpb, 0, 0))
    rows = lambda n: pl.BlockSpec((tm, n), lambda i: (i, 0))
    per_tile = pl.BlockSpec((1, 1, LANES), lambda i: (i, 0, 0))
    return pl.pallas_call(
        _out_kernel,
        grid=(t // tm,),
        in_specs=[rows(nq), rows(nq), rows(d), per_batch, per_batch, per_batch,
                  const((1, nq)), const((1, nq)), const((1, d)), const((1, d)),
                  const(woa.shape), const(wob.shape), const(wrh.shape), const(wrl.shape),
                  const((1, LANES))],
        out_specs=[rows(d), rows(d), rows(LANES), pl.BlockSpec((8, tm), lambda i: (0, i)),
                   per_tile, per_tile, const((1, LANES))],
        out_shape=[jax.ShapeDtypeStruct((t, d), F32), jax.ShapeDtypeStruct((t, d), BF16),
                   jax.ShapeDtypeStruct((t, LANES), F32), jax.ShapeDtypeStruct((8, t), F32),
                   jax.ShapeDtypeStruct((t // tm, 1, LANES), F32),
                   jax.ShapeDtypeStruct((t // tm, 1, LANES), F32),
                   jax.ShapeDtypeStruct((1, LANES), F32)],
        scratch_shapes=[pltpu.VMEM((1, LANES), F32)],
        compiler_params=_params("arbitrary"),
    )(oa, ob, x2, g1, sc2, sh2, ga, gb, gpost, gpre2, woa, wob, wrh, wrl, br)


PACK_ROWS = 8
ROW_DTYPE = F32


def _pack_rows(ref, x):
    n = x.shape[0]
    for c in range(PACK_ROWS):
        ref[pl.ds(c, n, stride=PACK_ROWS), :] = x[:, c * LANES:(c + 1) * LANES]


def _unpack_rows(ref):
    n = ref.shape[0] // PACK_ROWS
    return jnp.concatenate(
        [ref[pl.ds(c, n, stride=PACK_ROWS), :].astype(BF16) for c in range(PACK_ROWS)], axis=1)


def _for_each_run_piece(rdst_ref, rlen_ref, tile, max_len, fn):
    n_bits = max_len.bit_length()

    def run(e, local):
        length = rlen_ref[tile * N_EXPERTS + e]
        dst = rdst_ref[tile * N_EXPERTS + e]
        for b in range(n_bits):
            size = 1 << b

            @pl.when(((length >> b) & 1) == 1)
            def _():
                done = length & (size - 1)
                fn(local + done, dst + done, size)
        return local + length

    lax.fori_loop(0, N_EXPERTS, run, 0)


def _token_rows(ref, row0, n_rows):
    start = row0 * PACK_ROWS
    if not isinstance(start, int):
        start = pl.multiple_of(start, PACK_ROWS)
    return ref.at[pl.ds(start, n_rows * PACK_ROWS)]


def _dispatch_kernel(rdst_ref, rlen_ref, used_ref, ends_ref, nv_ref, h_ref, rt_ref, xs_ref,
                     sorted_ref, zero_ref, sem, zsem, *, te, n_tiles):
    k = pl.program_id(0)
    nk = pl.num_programs(0)
    tm = h_ref.shape[0]
    rows = 2 * tm
    slot = k % 2

    def wait_slot(s):
        pltpu.make_async_copy(sorted_ref.at[s], _token_rows(xs_ref, 0, rows), sem.at[s]).wait()

    @pl.when(k == 0)
    def _():
        zero_ref[...] = jnp.zeros_like(zero_ref)

        def pad_copy(row0, size):
            return pltpu.make_async_copy(_token_rows(zero_ref, 0, size),
                                         _token_rows(xs_ref, row0, size), zsem)

        def for_each_pad_tile(fn):
            def expert_pad(e, carry):
                start = used_ref[e]
                length = ends_ref[e] - start
                for b in range((te - 1).bit_length()):
                    size = 1 << b

                    @pl.when(((length >> b) & 1) == 1)
                    def _():
                        fn(pad_copy(start + (length & (size - 1)), size))
                return carry

            def tail_pad(j, carry):
                fn(pad_copy(j * te, te))
                return carry

            lax.fori_loop(0, N_EXPERTS, expert_pad, 0)
            lax.fori_loop(nv_ref[0], n_tiles, tail_pad, 0)

        for_each_pad_tile(lambda cp: cp.start())
        for_each_pad_tile(lambda cp: cp.wait())

    @pl.when(k >= 2)
    def _():
        wait_slot(slot)

    pos0 = rt_ref[2:3, :]
    pos1 = rt_ref[3:4, :]
    r = lax.broadcasted_iota(jnp.int32, (rows, tm), 0).astype(F32)
    perm = jnp.where((r == pos0) | (r == pos1), 1.0, 0.0).astype(BF16)
    srt = _dot(perm, h_ref[...].astype(BF16))
    buf = sorted_ref.at[slot]
    _pack_rows(buf, srt)

    def copy_piece(local, dst, size):
        pltpu.make_async_copy(_token_rows(buf, local, size), _token_rows(xs_ref, dst, size),
                              sem.at[slot]).start()

    _for_each_run_piece(rdst_ref, rlen_ref, k, tm, copy_piece)

    @pl.when(k == nk - 1)
    def _():
        wait_slot(slot)

        @pl.when(nk >= 2)
        def _():
            wait_slot(1 - slot)


def _dispatch(run_dst, run_len, used, ends, n_valid, h2, rt, n_rows):
    t, d = h2.shape
    assert d == PACK_ROWS * LANES
    tm = MOVE_TM
    te = EXPERT_TE
    return pl.pallas_call(
        functools.partial(_dispatch_kernel, te=te, n_tiles=n_rows // te),
        grid_spec=pltpu.PrefetchScalarGridSpec(
            num_scalar_prefetch=5,
            grid=(t // tm,),
            in_specs=[pl.BlockSpec((tm, d), lambda i, *_: (i, 0)),
                      pl.BlockSpec((8, tm), lambda i, *_: (0, i))],
            out_specs=pl.BlockSpec(memory_space=pl.ANY),
            scratch_shapes=[pltpu.VMEM((2, 2 * tm * PACK_ROWS, LANES), ROW_DTYPE),
                            pltpu.VMEM((te * PACK_ROWS, LANES), ROW_DTYPE),
                            pltpu.SemaphoreType.DMA((2,)), pltpu.SemaphoreType.DMA(())]),
        out_shape=jax.ShapeDtypeStruct((n_rows * PACK_ROWS, LANES), ROW_DTYPE),
        compiler_params=_params("arbitrary"),
    )(run_dst, run_len, used, ends, n_valid, h2, rt)


def _expert_kernel(te_ref, nv_ref, xs_ref, wg_ref, wu_ref, wd_ref, ys_ref, wg_bf, wu_bf, wd_bf):
    j = pl.program_id(0)
    valid = j < nv_ref[0]
    changed = (j == 0) | (te_ref[j] != te_ref[jnp.maximum(j - 1, 0)])

    @pl.when(valid & changed)
    def _():
        wg_bf[...] = wg_ref[0].astype(BF16)
        wu_bf[...] = wu_ref[0].astype(BF16)
        wd_bf[...] = wd_ref[0].astype(BF16)

    @pl.when(valid)
    def _():
        xb = _unpack_rows(xs_ref)
        gate = _dot(xb, wg_bf[...])
        up = _dot(xb, wu_bf[...])
        act = gate * jax.nn.sigmoid(gate) * up
        _pack_rows(ys_ref, _dot(act.astype(BF16), wd_bf[...]))

    @pl.when(jnp.logical_not(valid))
    def _():
        ys_ref[...] = jnp.zeros_like(ys_ref)


def _expert_mlp(tile_expert, n_valid, xs, w_gate, w_up, w_down):
    te = EXPERT_TE
    d, ff = w_gate.shape[1:]
    n_rows = xs.shape[0] // PACK_ROWS
    blk = (te * PACK_ROWS, LANES)
    tile = lambda j, e, nv: (jnp.minimum(j, nv[0] - 1), 0)
    wsel = lambda j, e, nv: (e[j], 0, 0)
    return pl.pallas_call(
        _expert_kernel,
        grid_spec=pltpu.PrefetchScalarGridSpec(
            num_scalar_prefetch=2,
            grid=(n_rows // te,),
            in_specs=[pl.BlockSpec(blk, tile),
                      pl.BlockSpec((1, d, ff), wsel), pl.BlockSpec((1, d, ff), wsel),
                      pl.BlockSpec((1, ff, d), wsel)],
            out_specs=pl.BlockSpec(blk, lambda j, e, nv: (j, 0)),
            scratch_shapes=[pltpu.VMEM((d, ff), BF16), pltpu.VMEM((d, ff), BF16),
                            pltpu.VMEM((ff, d), BF16)]),
        out_shape=jax.ShapeDtypeStruct(xs.shape, ROW_DTYPE),
        compiler_params=_params("arbitrary"),
    )(tile_expert, n_valid, xs, w_gate, w_up, w_down)


def _combine_kernel(rdst_ref, rlen_ref, x1_ref, rinfo_ref, g2_ref, gpost_ref, ys_ref, o_ref,
                    gath_ref, sem):
    k = pl.program_id(0)
    nk = pl.num_programs(0)
    tm = x1_ref.shape[0]
    rows = 2 * tm
    slot = k % 2

    def gather_runs(tile, s):
        buf = gath_ref.at[s]

        def copy_piece(local, src, size):
            pltpu.make_async_copy(_token_rows(ys_ref, src, size), _token_rows(buf, local, size),
                                  sem.at[s]).start()

        _for_each_run_piece(rdst_ref, rlen_ref, tile, tm, copy_piece)

    @pl.when(k == 0)
    def _():
        gather_runs(0, 0)

    @pl.when(k + 1 < nk)
    def _():
        gather_runs(k + 1, 1 - slot)

    buf = gath_ref.at[slot]
    pltpu.make_async_copy(_token_rows(ys_ref, 0, rows), buf, sem.at[slot]).wait()
    g = _unpack_rows(buf)
    info = rinfo_ref[...]
    col = lax.broadcasted_iota(jnp.int32, (tm, rows), 1).astype(F32)
    pick0 = jnp.where(col == info[:, 2:3], 1.0, 0.0).astype(BF16)
    pick1 = jnp.where(col == info[:, 3:4], 1.0, 0.0).astype(BF16)
    fx = info[:, 4:5] * _dot(pick0, g) + info[:, 5:6] * _dot(pick1, g)
    o_ref[...] = x1_ref[...] + g2_ref[0] * (_rms(fx) * gpost_ref[...])


def _combine(run_dst, run_len, x1, rinfo, g2, gpost, ys, seq):
    t, d = x1.shape
    tm = MOVE_TM
    tpb = seq // tm
    return pl.pallas_call(
        _combine_kernel,
        grid_spec=pltpu.PrefetchScalarGridSpec(
            num_scalar_prefetch=2,
            grid=(t // tm,),
            in_specs=[pl.BlockSpec((tm, d), lambda i, *_: (i, 0)),
                      pl.BlockSpec((tm, LANES), lambda i, *_: (i, 0)),
                      pl.BlockSpec((1, 1, d), lambda i, *_: (i // tpb, 0, 0)),
                      pl.BlockSpec((1, d), lambda i, *_: (0, 0)),
                      pl.BlockSpec(memory_space=pl.ANY)],
            out_specs=pl.BlockSpec((tm, d), lambda i, *_: (i, 0)),
            scratch_shapes=[pltpu.VMEM((2, 2 * tm * PACK_ROWS, LANES), ROW_DTYPE),
                            pltpu.SemaphoreType.DMA((2,))]),
        out_shape=jax.ShapeDtypeStruct((t, d), F32),
        compiler_params=_params("arbitrary"),
    )(run_dst, run_len, x1, rinfo, g2, gpost, ys)


def _rope_tables(seq):
    pos = jnp.arange(seq, dtype=jnp.int32)
    row = (pos // GRID_W).astype(F32)
    col = (pos % GRID_W).astype(F32)
    axis_dim = HEAD_DIM // 2
    inv_freq = ROPE_THETA ** (-jnp.arange(0, axis_dim, 2, dtype=F32) / axis_dim)
    ang = jnp.concatenate([row[:, None] * inv_freq, col[:, None] * inv_freq], axis=-1)
    pair = (jnp.arange(LANES) % HEAD_DIM) // 2
    cos = jnp.cos(ang)[:, pair]
    sin = jnp.sin(ang)[:, pair]
    even = (jnp.arange(LANES) % 2) == 0
    tables = (cos, jnp.where(even, -sin, 0.0), jnp.where(even, 0.0, sin))
    return tables + tuple(tb.T for tb in tables)


def _segment_ones(n):
    seg = jnp.arange(n) // HEAD_DIM
    return (seg[:, None] == seg[None, :]).astype(BF16)


def kernel(x, c, ctx, c_ctx, w_mod, b_mod, attn_pre_norm, attn_post_norm, w_in, a_sink,
           b_q_norm, b_k_norm, a_out_norm, b_out_norm, w_out, ffn_pre_norm, ffn_post_norm,
           w_group, b_group, w_router, b_router, w_gate, w_up, w_down):
    batch, seq, d = x.shape
    ctx_len = ctx.shape[1]
    assert w_mod.shape[0] == 1, "single-layer stack only (context stream is never updated)"
    assert seq % ATTN_A_TQ == 0 and seq >= ATTN_A_TQ + 2 * WINDOW
    assert seq % PROJ_TM == 0 and seq % ATTN_B_TQ == 0 and seq % OUT_TM == 0 and seq % MOVE_TM == 0
    t = batch * seq
    nq = d // 2
    nkv = nq // KV_GROUP
    assert nkv == LANES and w_in.shape[2] == 2 * nq + 4 * nkv

    cc = jnp.concatenate([c, c_ctx[None, :], jnp.zeros((16 - batch - 1, d), F32)], axis=0)
    mod = _modulation(cc, w_mod[0], b_mod[0])
    sh1, sc1, g1, sh2, sc2, g2 = (m.reshape(batch, 1, d) for m in jnp.split(mod[:batch], 6, axis=-1))
    csh1, csc1 = (m.reshape(1, d) for m in jnp.split(mod[batch], 6)[:2])

    x2 = x.reshape(t, d)
    c2 = ctx.reshape(batch * ctx_len, d)
    gpre = attn_pre_norm[0].reshape(1, d)
    w_in_bf = w_in[0].astype(BF16)
    kv_cols = jnp.concatenate([w_in_bf[:, nq:nq + 2 * nkv], w_in_bf[:, 2 * nq + 2 * nkv:]], axis=1)
    qn = jnp.tile(b_q_norm[0], nq // HEAD_DIM).reshape(1, nq)
    kn = jnp.tile(b_k_norm[0], nkv // HEAD_DIM).reshape(1, nkv)
    seg_q, seg_k = _segment_ones(nq), _segment_ones(nkv)
    qat, ka, vat, qbt, kb, vbt, stats = _project_latents(
        x2, sc1, sh1, gpre, w_in_bf, _rope_tables(seq), qn, kn, seg_q, seg_k, seq)
    kca, vcat, kcb, vcbt, ctx_stats = _project_context(
        c2, csc1, csh1, gpre, kv_cols, kn, seg_k, ctx_len)

    q_sq = jnp.max(stats[:, 0, 0])
    k_sq = jnp.maximum(jnp.max(stats[:, 0, 1]), jnp.max(ctx_stats[:, 0, 1]))
    bound_a = jnp.maximum(1.01 * jnp.sqrt(q_sq * k_sq), jnp.max(a_sink[0]) * LOG2_E).reshape(1)
    oa = _attention_a(bound_a, a_sink[0], qat, ka, vat, kca, vcat, batch, seq, ctx_len)
    score_bound = (1.01 * HEAD_DIM ** 0.5 * LOG2_E
                   * jnp.max(jnp.abs(b_q_norm[0])) * jnp.max(jnp.abs(b_k_norm[0]))).reshape(1)
    ob = _attention_b(score_bound, qbt, kb, vbt, kcb, vcbt, batch, seq, ctx_len)

    w_out_bf = w_out[0].astype(BF16)
    w_r = jnp.zeros((d, LANES), F32)
    w_r = w_r.at[:, :N_EXPERTS].set(w_router[0]).at[:, N_EXPERTS:N_EXPERTS + N_GROUPS].set(w_group[0])
    w_r_hi = w_r.astype(BF16)
    w_r_lo = (w_r - w_r_hi.astype(F32)).astype(BF16)
    b_r = jnp.zeros((1, LANES), F32)
    b_r = b_r.at[0, :N_EXPERTS].set(b_router[0]).at[0, N_EXPERTS:N_EXPERTS + N_GROUPS].set(b_group[0])
    x1, h2, rinfo, rt, tcarry, tcnt, counts = _out_and_route(
        oa, ob, x2, g1, sc2, sh2, a_out_norm[0].reshape(1, nq), b_out_norm[0].reshape(1, nq),
        attn_post_norm[0].reshape(1, d), ffn_pre_norm[0].reshape(1, d),
        w_out_bf[:nq], w_out_bf[nq:], w_r_hi, w_r_lo, b_r, seq)

    te = EXPERT_TE
    n_tiles = -(-(2 * t + N_EXPERTS * (te - 1)) // te)
    n_rows = n_tiles * te
    cnt = counts[0, :N_EXPERTS].astype(jnp.int32)
    padded = ((cnt + te - 1) // te) * te
    ends = jnp.cumsum(padded)
    offs = ends - padded
    run_dst = (offs[None, :] + tcarry[:, 0, :N_EXPERTS].astype(jnp.int32)).reshape(-1)
    run_len = tcnt[:, 0, :N_EXPERTS].astype(jnp.int32).reshape(-1)
    n_valid = (ends[-1] // te).astype(jnp.int32).reshape(1)
    tile_start = jnp.arange(n_tiles, dtype=jnp.int32) * te
    tile_expert = jnp.sum(ends[None, :] <= tile_start[:, None], axis=1).astype(jnp.int32)
    last_expert = tile_expert[jnp.maximum(n_valid[0] - 1, 0)]
    tile_expert = jnp.where(tile_start < ends[-1], tile_expert, last_expert)

    xs = _dispatch(run_dst, run_len, (offs + cnt).astype(jnp.int32), ends.astype(jnp.int32),
                   n_valid, h2, rt, n_rows)
    ys = _expert_mlp(tile_expert, n_valid, xs, w_gate[0], w_up[0], w_down[0])
    out = _combine(run_dst, run_len, x1, rinfo, g2, ffn_post_norm[0].reshape(1, d), ys, seq)
    return out.reshape(batch, seq, d)
```

```python
import functools

import jax
import jax.numpy as jnp
from jax import lax
from jax.experimental import pallas as pl
from jax.experimental.pallas import tpu as pltpu

F32 = jnp.float32
BF16 = jnp.bfloat16

GRID_W = 64
HEAD_DIM = 64
KV_GROUP = 4
WINDOW = 128
ROPE_THETA = 10000.0
N_GROUPS = 4
EXPERTS_PER_GROUP = 8
N_EXPERTS = N_GROUPS * EXPERTS_PER_GROUP
EPS = 1e-6
NEG_INF = -1e30
LOG2_E = 1.4426950408889634
SAFE_SOFTMAX_SHIFT = 40.0

LANES = 128
V7X_VMEM_LIMIT = 56 * 1024 * 1024

PROJ_TM = 512
ATTN_A_TQ = 256
ATTN_B_TQ = 256
OUT_TM = 512
EXPERT_TE = 512
MOVE_TM = OUT_TM


def _params(*sem):
    return pltpu.CompilerParams(dimension_semantics=sem, vmem_limit_bytes=V7X_VMEM_LIMIT)


def _dot(a, b):
    return jnp.dot(a, b, preferred_element_type=F32)


def _dot_nt(a, b):
    return lax.dot_general(a, b, (((1,), (1,)), ((), ())), preferred_element_type=F32)


def _rms(x):
    return x * lax.rsqrt(jnp.mean(x * x, axis=-1, keepdims=True) + EPS)


def _split_bf16(x):
    hi = x.astype(BF16)
    lo = (x - hi.astype(F32)).astype(BF16)
    return hi, lo


def _mod_kernel(c_ref, w_ref, b_ref, o_ref):
    cc = c_ref[...]
    s = cc * jax.nn.sigmoid(cc)
    s_hi, s_lo = _split_bf16(s)
    w_hi, w_lo = _split_bf16(w_ref[...])
    o_ref[...] = _dot(s_hi, w_hi) + _dot(s_lo, w_hi) + _dot(s_hi, w_lo) + b_ref[...]


def _modulation(cc, w_mod, b_mod):
    rows, d = cc.shape
    n = w_mod.shape[1]
    bn = 1024
    return pl.pallas_call(
        _mod_kernel,
        grid=(n // bn,),
        in_specs=[pl.BlockSpec((rows, d), lambda i: (0, 0)),
                  pl.BlockSpec((d, bn), lambda i: (0, i)),
                  pl.BlockSpec((1, bn), lambda i: (0, i))],
        out_specs=pl.BlockSpec((rows, bn), lambda i: (0, i)),
        out_shape=jax.ShapeDtypeStruct((rows, n), F32),
        compiler_params=_params("arbitrary"),
    )(cc, w_mod, b_mod.reshape(1, n))


def _rope(x, cos, sin_a, sin_b):
    return x * cos + pltpu.roll(x, LANES - 1, 1) * sin_a + pltpu.roll(x, 1, 1) * sin_b


def _head_norm(x, seg_ref, gain):
    ss = _dot((x * x).astype(BF16), seg_ref[...])
    return x * lax.rsqrt(ss * (1.0 / HEAD_DIM) + EPS) * gain


def _max_head_sq_norm(x, seg_ref):
    ss = _dot((x * x).astype(BF16), seg_ref[...])
    return jnp.max(jnp.max(ss, axis=1, keepdims=True), axis=0, keepdims=True)


def _norm_stats(q_sq, k_sq):
    lane = lax.broadcasted_iota(jnp.int32, (1, LANES), 1)
    zero = jnp.zeros((1, LANES), F32)
    return jnp.where(lane == 0, q_sq, zero) + jnp.where(lane == 1, k_sq, zero)


def _rope_t(xt, cos_t, sin_a_t, sin_b_t):
    return (xt * cos_t + pltpu.roll(xt, LANES - 1, 0) * sin_a_t
            + pltpu.roll(xt, 1, 0) * sin_b_t)


def _proj_kernel(x_ref, sc_ref, sh_ref, gpre_ref, w_ref, cos_ref, sa_ref, sb_ref,
                 cos_t_ref, sa_t_ref, sb_t_ref, qn_ref, kn_ref, seg_q_ref, seg_k_ref,
                 qat_ref, ka_ref, vat_ref, qbt_ref, kb_ref, vbt_ref, stats_ref):
    h = _rms(x_ref[...]) * gpre_ref[...] * (1.0 + sc_ref[0]) + sh_ref[0]
    p = _dot(h.astype(BF16), w_ref[...])
    cos, sa, sb = cos_ref[...], sa_ref[...], sb_ref[...]
    q_scale = HEAD_DIM ** -0.5 * LOG2_E
    cos_t, sa_t, sb_t = cos_t_ref[...] * q_scale, sa_t_ref[...] * q_scale, sb_t_ref[...] * q_scale
    nq = qat_ref.shape[0]
    for c in range(nq // LANES):
        qat_ref[c * LANES:(c + 1) * LANES, :] = _rope_t(
            p[:, c * LANES:(c + 1) * LANES].T, cos_t, sa_t, sb_t).astype(BF16)
    o = nq
    ka_ref[...] = _rope(p[:, o:o + LANES], cos, sa, sb).astype(BF16)
    vat_ref[...] = p[:, o + LANES:o + 2 * LANES].T.astype(BF16)
    stats_ref[0] = _norm_stats(_max_head_sq_norm(p[:, 0:nq], seg_q_ref) * (q_scale * q_scale),
                               _max_head_sq_norm(p[:, o:o + LANES], seg_k_ref))
    o += 2 * LANES
    qb = _head_norm(p[:, o:o + nq], seg_q_ref, qn_ref[...])
    for c in range(nq // LANES):
        qbt_ref[c * LANES:(c + 1) * LANES, :] = _rope_t(
            qb[:, c * LANES:(c + 1) * LANES].T, cos_t, sa_t, sb_t).astype(BF16)
    o += nq
    kb = _head_norm(p[:, o:o + LANES], seg_k_ref, kn_ref[...])
    kb_ref[...] = _rope(kb, cos, sa, sb).astype(BF16)
    vbt_ref[...] = p[:, o + LANES:o + 2 * LANES].T.astype(BF16)


def _ctx_proj_kernel(x_ref, sc_ref, sh_ref, gpre_ref, w_ref, kn_ref, seg_k_ref,
                     ka_ref, vat_ref, kb_ref, vbt_ref, stats_ref):
    h = _rms(x_ref[...]) * gpre_ref[...] * (1.0 + sc_ref[...]) + sh_ref[...]
    p = _dot(h.astype(BF16), w_ref[...])
    stats_ref[0] = _norm_stats(0.0, _max_head_sq_norm(p[:, 0:LANES], seg_k_ref))
    ka_ref[...] = p[:, 0:LANES].astype(BF16)
    vat_ref[...] = p[:, LANES:2 * LANES].T.astype(BF16)
    kb_ref[...] = _head_norm(p[:, 2 * LANES:3 * LANES], seg_k_ref, kn_ref[...]).astype(BF16)
    vbt_ref[...] = p[:, 3 * LANES:4 * LANES].T.astype(BF16)


def _project_latents(x2, sc, sh, gpre, w_in, tables, qn, kn, seg_q, seg_k, seq):
    t, d = x2.shape
    tm = PROJ_TM
    tpb = seq // tm
    nq = seg_q.shape[0]
    const = lambda shape: pl.BlockSpec(shape, lambda i: (0,) * len(shape))
    per_batch = pl.BlockSpec((1, 1, d), lambda i: (i // tpb, 0, 0))
    table = pl.BlockSpec((tm, LANES), lambda i: (i % tpb, 0))
    table_t = pl.BlockSpec((LANES, tm), lambda i: (0, i % tpb))
    k_spec = pl.BlockSpec((tm, LANES), lambda i: (i, 0))
    k_shape = jax.ShapeDtypeStruct((t, LANES), BF16)
    vt_spec = pl.BlockSpec((LANES, tm), lambda i: (0, i))
    vt_shape = jax.ShapeDtypeStruct((LANES, t), BF16)
    qt_spec = pl.BlockSpec((nq, tm), lambda i: (0, i))
    qt_shape = jax.ShapeDtypeStruct((nq, t), BF16)
    return pl.pallas_call(
        _proj_kernel,
        grid=(t // tm,),
        in_specs=[pl.BlockSpec((tm, d), lambda i: (i, 0)), per_batch, per_batch, const((1, d)),
                  const(w_in.shape), table, table, table, table_t, table_t, table_t,
                  const((1, nq)), const((1, LANES)), const(seg_q.shape), const(seg_k.shape)],
        out_specs=[qt_spec, k_spec, vt_spec, qt_spec, k_spec, vt_spec,
                   pl.BlockSpec((1, 1, LANES), lambda i: (i, 0, 0))],
        out_shape=[qt_shape, k_shape, vt_shape, qt_shape, k_shape, vt_shape,
                   jax.ShapeDtypeStruct((t // tm, 1, LANES), F32)],
        compiler_params=_params("arbitrary"),
    )(x2, sc, sh, gpre, w_in, *tables, qn, kn, seg_q, seg_k)


def _project_context(c2, sc, sh, gpre, w_kv, kn, seg_k, ctx_len):
    t, d = c2.shape
    const = lambda shape: pl.BlockSpec(shape, lambda i: (0,) * len(shape))
    k_spec = pl.BlockSpec((ctx_len, LANES), lambda i: (i, 0))
    k_shape = jax.ShapeDtypeStruct((t, LANES), BF16)
    vt_spec = pl.BlockSpec((LANES, ctx_len), lambda i: (0, i))
    vt_shape = jax.ShapeDtypeStruct((LANES, t), BF16)
    return pl.pallas_call(
        _ctx_proj_kernel,
        grid=(t // ctx_len,),
        in_specs=[pl.BlockSpec((ctx_len, d), lambda i: (i, 0)), const((1, d)), const((1, d)),
                  const((1, d)), const(w_kv.shape), const((1, LANES)), const(seg_k.shape)],
        out_specs=[k_spec, vt_spec, k_spec, vt_spec,
                   pl.BlockSpec((1, 1, LANES), lambda i: (i, 0, 0))],
        out_shape=[k_shape, vt_shape, k_shape, vt_shape,
                   jax.ShapeDtypeStruct((t // ctx_len, 1, LANES), F32)],
        compiler_params=_params("arbitrary"),
    )(c2, sc, sh, gpre, w_kv, kn, seg_k)


def _attend_t(w, k, kc, vt, vct, shift=None, bias=None, sink=None):
    st = _dot(k, w)
    sct = _dot(kc, w)
    if bias is not None:
        tq = bias.shape[1]
        st = jnp.concatenate([st[:, c * tq:(c + 1) * tq] + bias
                              for c in range(st.shape[1] // tq)], axis=1)
    if shift is None:
        shift = jnp.maximum(jnp.max(st, axis=0, keepdims=True),
                            jnp.max(sct, axis=0, keepdims=True))
        if sink is not None:
            shift = jnp.maximum(shift, sink)
    pt = jnp.exp2(st - shift)
    pct = jnp.exp2(sct - shift)
    denom = jnp.sum(pt, axis=0, keepdims=True) + jnp.sum(pct, axis=0, keepdims=True)
    if sink is not None:
        denom = denom + jnp.exp2(sink - shift)
    o2 = _dot(vt, pt.astype(BF16)) + _dot(vct, pct.astype(BF16))
    return o2, denom


def _all_heads_t(qt_ref, o_ref, attend):
    tq = qt_ref.shape[1]
    n_kv = LANES // HEAD_DIM
    zeros = jnp.zeros((HEAD_DIM, tq), BF16)
    outs = []
    for h in range(qt_ref.shape[0] // HEAD_DIM):
        g = h // KV_GROUP
        qh = qt_ref[h * HEAD_DIM:(h + 1) * HEAD_DIM, :]
        w = jnp.concatenate([zeros] * g + [qh] + [zeros] * (n_kv - 1 - g), axis=0)
        o2, denom = attend(h, w)
        outs.append(o2[g * HEAD_DIM:(g + 1) * HEAD_DIM, :] / denom)
    o_ref[...] = jnp.concatenate(outs, axis=0).T.astype(BF16)


def _all_heads_fused_t(qt_ref, o_ref, attend):
    tq = qt_ref.shape[1]
    n_heads = qt_ref.shape[0] // HEAD_DIM
    n_kv = LANES // HEAD_DIM
    rows = []
    for g in range(n_kv):
        heads = [qt_ref[h * HEAD_DIM:(h + 1) * HEAD_DIM, :] if h // KV_GROUP == g
                 else jnp.zeros((HEAD_DIM, tq), BF16) for h in range(n_heads)]
        rows.append(jnp.concatenate(heads, axis=1))
    w = jnp.concatenate(rows, axis=0)
    o2, denom = attend(0, w)
    o2 = o2 / denom
    outs = [o2[(h // KV_GROUP) * HEAD_DIM:(h // KV_GROUP + 1) * HEAD_DIM, h * tq:(h + 1) * tq]
            for h in range(n_heads)]
    o_ref[...] = jnp.concatenate(outs, axis=0).T.astype(BF16)


def _attn_b_kernel(bound_ref, qt_ref, k_ref, vt_ref, kc_ref, vct_ref, o_ref):
    k, kc, vt, vct = k_ref[...], kc_ref[...], vt_ref[...], vct_ref[...]
    bound = bound_ref[0]

    @pl.when(bound <= SAFE_SOFTMAX_SHIFT)
    def _():
        _all_heads_fused_t(qt_ref, o_ref, lambda h, w: _attend_t(w, k, kc, vt, vct, shift=bound))

    @pl.when(jnp.logical_not(bound <= SAFE_SOFTMAX_SHIFT))
    def _():
        _all_heads_t(qt_ref, o_ref, lambda h, w: _attend_t(w, k, kc, vt, vct))


def _attn_specs(nq, tq, seq, ctx_len):
    nqb = seq // tq
    return dict(
        qt=pl.BlockSpec((nq, tq), lambda b, i: (0, b * nqb + i)),
        k=pl.BlockSpec((seq, LANES), lambda b, i: (b, 0)),
        vt=pl.BlockSpec((LANES, seq), lambda b, i: (0, b)),
        kc=pl.BlockSpec((ctx_len, LANES), lambda b, i: (b, 0)),
        vct=pl.BlockSpec((LANES, ctx_len), lambda b, i: (0, b)),
        out=pl.BlockSpec((tq, nq), lambda b, i: (b * nqb + i, 0)))


def _attention_b(score_bound, qbt, kb, vbt, kcb, vcbt, batch, seq, ctx_len):
    nq, t = qbt.shape
    tq = ATTN_B_TQ
    sp = _attn_specs(nq, tq, seq, ctx_len)
    return pl.pallas_call(
        _attn_b_kernel,
        grid=(batch, seq // tq),
        in_specs=[pl.BlockSpec(memory_space=pltpu.SMEM),
                  sp["qt"], sp["k"], sp["vt"], sp["kc"], sp["vct"]],
        out_specs=sp["out"],
        out_shape=jax.ShapeDtypeStruct((t, nq), BF16),
        compiler_params=_params("arbitrary", "arbitrary"),
    )(score_bound, qbt, kb, vbt, kcb, vcbt)


def _attn_a_kernel(bound_ref, sink_ref, qt_ref, k_ref, vt_ref, kc_ref, vct_ref, bias_ref, o_ref,
                   *, seq):
    i = pl.program_id(1)
    tq = qt_ref.shape[1]
    n_heads = qt_ref.shape[0] // HEAD_DIM
    win = tq + 2 * WINDOW
    start = pl.multiple_of(jnp.clip(i * tq - WINDOW, 0, seq - win), WINDOW)
    k = k_ref[pl.ds(start, win), :]
    vt = vt_ref[:, pl.ds(start, win)]
    kc, vct = kc_ref[...], vct_ref[...]
    sinks = [sink_ref[h] * LOG2_E for h in range(n_heads)]

    bound = bound_ref[0]
    small = bound <= SAFE_SOFTMAX_SHIFT

    @pl.when(small)
    def _():
        sink_row = jnp.concatenate([jnp.full((1, tq), s, F32) for s in sinks], axis=1)
        _all_heads_fused_t(qt_ref, o_ref, lambda h, w: _attend_t(
            w, k, kc, vt, vct, shift=bound, bias=bias_ref[0], sink=sink_row))

    @pl.when(jnp.logical_not(small))
    def _():
        _all_heads_t(qt_ref, o_ref, lambda h, w: _attend_t(
            w, k, kc, vt, vct, bias=bias_ref[0], sink=sinks[h]))


def _band_bias(tq):
    win = tq + 2 * WINDOW
    r = jnp.arange(win, dtype=jnp.int32)[:, None]
    j = jnp.arange(tq, dtype=jnp.int32)[None, :]
    tables = [jnp.where(jnp.abs(off + r - j) <= WINDOW, 0.0, NEG_INF)
              for off in (0, -WINDOW, -2 * WINDOW)]
    return jnp.stack(tables).astype(F32)


def _attention_a(score_bound, sink, qat, ka, vat, kca, vcat, batch, seq, ctx_len):
    nq, t = qat.shape
    tq = ATTN_A_TQ
    nqb = seq // tq
    win = tq + 2 * WINDOW
    assert nqb >= 2 and tq >= WINDOW
    sp = _attn_specs(nq, tq, seq, ctx_len)
    which = lambda b, i: (jnp.where(i == 0, 0, jnp.where(i == nqb - 1, 2, 1)), 0, 0)
    return pl.pallas_call(
        functools.partial(_attn_a_kernel, seq=seq),
        grid=(batch, nqb),
        in_specs=[pl.BlockSpec(memory_space=pltpu.SMEM), pl.BlockSpec(memory_space=pltpu.SMEM),
                  sp["qt"], sp["k"], sp["vt"], sp["kc"], sp["vct"],
                  pl.BlockSpec((1, win, tq), which)],
        out_specs=sp["out"],
        out_shape=jax.ShapeDtypeStruct((t, nq), BF16),
        compiler_params=_params("arbitrary", "arbitrary"),
    )(score_bound, sink, qat, ka, vat, kca, vcat, _band_bias(tq))


def _out_kernel(oa_ref, ob_ref, x_ref, g1_ref, sc2_ref, sh2_ref, ga_ref, gb_ref, gpost_ref,
                gpre2_ref, woa_ref, wob_ref, wrh_ref, wrl_ref, br_ref,
                x1_ref, h2_ref, rinfo_ref, rt_ref, tcarry_ref, tcnt_ref, cnt_ref, carry_ref):
    step = pl.program_id(0)

    @pl.when(step == 0)
    def _():
        carry_ref[...] = jnp.zeros_like(carry_ref)

    na = _rms(oa_ref[...].astype(F32)) * ga_ref[...]
    nb = _rms(ob_ref[...].astype(F32)) * gb_ref[...]
    ox = _dot(na.astype(BF16), woa_ref[...]) + _dot(nb.astype(BF16), wob_ref[...])
    x1 = x_ref[...] + g1_ref[0] * (_rms(ox) * gpost_ref[...])
    x1_ref[...] = x1
    h2 = _rms(x1) * gpre2_ref[...] * (1.0 + sc2_ref[0]) + sh2_ref[0]
    h_hi, h_lo = _split_bf16(h2)
    h2_ref[...] = h_hi

    logits = (_dot(h_hi, wrh_ref[...]) + _dot(h_lo, wrh_ref[...]) + _dot(h_hi, wrl_ref[...])
              + br_ref[...])
    tm = logits.shape[0]
    lt = logits.T
    row = lax.broadcasted_iota(jnp.int32, lt.shape, 0)
    rowf = row.astype(F32)
    big = jnp.float32(1e9)
    ninf = jnp.float32(-jnp.inf)
    colmax = lambda v: jnp.max(v, axis=0, keepdims=True)
    colmin = lambda v: jnp.min(v, axis=0, keepdims=True)
    colsum = lambda v: jnp.sum(v, axis=0, keepdims=True)

    gmask = (row >= N_EXPERTS) & (row < N_EXPERTS + N_GROUPS)
    lg = jnp.where(gmask, lt, ninf)
    gmax = colmax(lg)
    gidx = colmin(jnp.where(lg == gmax, rowf, big)) - N_EXPERTS
    g_w = 1.0 / colsum(jnp.exp(lg - gmax))
    row_group = (row // EXPERTS_PER_GROUP).astype(F32)
    emask = (row < N_EXPERTS) & (row_group == gidx)
    le = jnp.where(emask, lt, ninf)
    m1 = colmax(le)
    i1 = colmin(jnp.where(le == m1, rowf, big))
    le2 = jnp.where(rowf == i1, ninf, le)
    m2 = colmax(le2)
    i2 = colmin(jnp.where(le2 == m2, rowf, big))
    e2 = jnp.exp(m2 - m1)
    w0 = g_w / (1.0 + e2)
    w1 = g_w * e2 / (1.0 + e2)

    hit1 = rowf == i1
    hit2 = rowf == i2
    onehot = jnp.where(hit1, 1.0, jnp.where(hit2, 1.0, 0.0)).astype(F32)
    r = lax.broadcasted_iota(jnp.int32, (tm, tm), 0)
    c = lax.broadcasted_iota(jnp.int32, (tm, tm), 1)
    earlier = jnp.where(r < c, 1.0, 0.0).astype(BF16)
    within = _dot(onehot.astype(BF16), earlier)
    tile_cnt = jnp.broadcast_to(jnp.sum(onehot, axis=1, keepdims=True), (LANES, LANES))
    er = lax.broadcasted_iota(jnp.int32, (LANES, LANES), 0)
    ec = lax.broadcasted_iota(jnp.int32, (LANES, LANES), 1)
    below = jnp.where(er > ec, 1.0, 0.0).astype(BF16)
    cnt_hi = jnp.floor(tile_cnt * (1.0 / 32.0))
    cnt_lo = tile_cnt - 32.0 * cnt_hi
    run_start = 32.0 * _dot(below, cnt_hi.astype(BF16)) + _dot(below, cnt_lo.astype(BF16))
    local = within + run_start[:, 0:1]
    pos0 = colsum(jnp.where(hit1, local, 0.0))
    pos1 = colsum(jnp.where(hit2, local, 0.0))
    cnt_row = tile_cnt.T[0:1, :]
    tcarry_ref[0] = carry_ref[...]
    tcnt_ref[0] = cnt_row
    carry_ref[...] += cnt_row
    cnt_ref[...] = carry_ref[...]

    fields = jnp.concatenate([i1, i2, pos0, pos1, w0, w1, jnp.zeros((2, tm), F32)], axis=0)
    rt_ref[...] = fields
    rinfo_ref[...] = jnp.concatenate(
        [fields, jnp.zeros((LANES - 8, tm), F32)], axis=0).T


def _out_and_route(oa, ob, x2, g1, sc2, sh2, ga, gb, gpost, gpre2, woa, wob, wrh, wrl, br, seq):
    t, d = x2.shape
    tm = OUT_TM
    tpb = seq // tm
    nq = oa.shape[1]
    const = lambda shape: pl.BlockSpec(shape, lambda i: (0,) * len(shape))
    per_batch = pl.BlockSpec((1, 1, d), lambda i: (i // tpb, 0, 0))
    rows = lambda n: pl.BlockSpec((tm, n), lambda i: (i, 0))
    per_tile = pl.BlockSpec((1, 1, LANES), lambda i: (i, 0, 0))
    return pl.pallas_call(
        _out_kernel,
        grid=(t // tm,),
        in_specs=[rows(nq), rows(nq), rows(d), per_batch, per_batch, per_batch,
                  const((1, nq)), const((1, nq)), const((1, d)), const((1, d)),
                  const(woa.shape), const(wob.shape), const(wrh.shape), const(wrl.shape),
                  const((1, LANES))],
        out_specs=[rows(d), rows(d), rows(LANES), pl.BlockSpec((8, tm), lambda i: (0, i)),
                   per_tile, per_tile, const((1, LANES))],
        out_shape=[jax.ShapeDtypeStruct((t, d), F32), jax.ShapeDtypeStruct((t, d), BF16),
                   jax.ShapeDtypeStruct((t, LANES), F32), jax.ShapeDtypeStruct((8, t), F32),
                   jax.ShapeDtypeStruct((t // tm, 1, LANES), F32),
                   jax.ShapeDtypeStruct((t // tm, 1, LANES), F32),
                   jax.ShapeDtypeStruct((1, LANES), F32)],
        scratch_shapes=[pltpu.VMEM((1, LANES), F32)],
        compiler_params=_params("arbitrary"),
    )(oa, ob, x2, g1, sc2, sh2, ga, gb, gpost, gpre2, woa, wob, wrh, wrl, br)


PACK_ROWS = 8
ROW_DTYPE = F32


def _pack_rows(ref, x):
    n = x.shape[0]
    for c in range(PACK_ROWS):
        ref[pl.ds(c, n, stride=PACK_ROWS), :] = x[:, c * LANES:(c + 1) * LANES]


def _unpack_rows(ref):
    n = ref.shape[0] // PACK_ROWS
    return jnp.concatenate(
        [ref[pl.ds(c, n, stride=PACK_ROWS), :].astype(BF16) for c in range(PACK_ROWS)], axis=1)


def _for_each_run_piece(rdst_ref, rlen_ref, rmax_ref, tile, max_len, fn):
    def copy_runs(n_bits):
        def run(e, local):
            length = rlen_ref[tile * N_EXPERTS + e]
            dst = rdst_ref[tile * N_EXPERTS + e]
            for b in range(n_bits):
                size = 1 << b

                @pl.when(((length >> b) & 1) == 1)
                def _():
                    done = length & (size - 1)
                    fn(local + done, dst + done, size)
            return local + length

        lax.fori_loop(0, N_EXPERTS, run, 0)

    all_bits = max_len.bit_length()
    low_bits = min(all_bits, (2 * max_len // N_EXPERTS).bit_length())
    short = rmax_ref[tile] < (1 << low_bits)

    @pl.when(short)
    def _():
        copy_runs(low_bits)

    @pl.when(jnp.logical_not(short))
    def _():
        copy_runs(all_bits)


def _token_rows(ref, row0, n_rows):
    start = row0 * PACK_ROWS
    if not isinstance(start, int):
        start = pl.multiple_of(start, PACK_ROWS)
    return ref.at[pl.ds(start, n_rows * PACK_ROWS)]


def _dispatch_kernel(rdst_ref, rlen_ref, rmax_ref, used_ref, ends_ref, nv_ref, h_ref, rt_ref, xs_ref,
                     sorted_ref, zero_ref, sem, zsem, *, te, n_tiles):
    k = pl.program_id(0)
    nk = pl.num_programs(0)
    tm = h_ref.shape[0]
    rows = 2 * tm
    slot = k % 2

    def wait_slot(s):
        pltpu.make_async_copy(sorted_ref.at[s], _token_rows(xs_ref, 0, rows), sem.at[s]).wait()

    @pl.when(k == 0)
    def _():
        zero_ref[...] = jnp.zeros_like(zero_ref)

        def pad_copy(row0, size):
            return pltpu.make_async_copy(_token_rows(zero_ref, 0, size),
                                         _token_rows(xs_ref, row0, size), zsem)

        def for_each_pad_tile(fn):
            def expert_pad(e, carry):
                start = used_ref[e]
                length = ends_ref[e] - start
                for b in range((te - 1).bit_length()):
                    size = 1 << b

                    @pl.when(((length >> b) & 1) == 1)
                    def _():
                        fn(pad_copy(start + (length & (size - 1)), size))
                return carry

            def tail_pad(j, carry):
                fn(pad_copy(j * te, te))
                return carry

            lax.fori_loop(0, N_EXPERTS, expert_pad, 0)
            lax.fori_loop(nv_ref[0], n_tiles, tail_pad, 0)

        for_each_pad_tile(lambda cp: cp.start())
        for_each_pad_tile(lambda cp: cp.wait())

    @pl.when(k >= 2)
    def _():
        wait_slot(slot)

    pos0 = rt_ref[2:3, :]
    pos1 = rt_ref[3:4, :]
    r = lax.broadcasted_iota(jnp.int32, (rows, tm), 0).astype(F32)
    perm = jnp.where((r == pos0) | (r == pos1), 1.0, 0.0).astype(BF16)
    srt = _dot(perm, h_ref[...].astype(BF16))
    buf = sorted_ref.at[slot]
    _pack_rows(buf, srt)

    def copy_piece(local, dst, size):
        pltpu.make_async_copy(_token_rows(buf, local, size), _token_rows(xs_ref, dst, size),
                              sem.at[slot]).start()

    _for_each_run_piece(rdst_ref, rlen_ref, rmax_ref, k, tm, copy_piece)

    @pl.when(k == nk - 1)
    def _():
        wait_slot(slot)

        @pl.when(nk >= 2)
        def _():
            wait_slot(1 - slot)


def _dispatch(run_dst, run_len, run_max, used, ends, n_valid, h2, rt, n_rows):
    t, d = h2.shape
    assert d == PACK_ROWS * LANES
    tm = MOVE_TM
    te = EXPERT_TE
    return pl.pallas_call(
        functools.partial(_dispatch_kernel, te=te, n_tiles=n_rows // te),
        grid_spec=pltpu.PrefetchScalarGridSpec(
            num_scalar_prefetch=6,
            grid=(t // tm,),
            in_specs=[pl.BlockSpec((tm, d), lambda i, *_: (i, 0)),
                      pl.BlockSpec((8, tm), lambda i, *_: (0, i))],
            out_specs=pl.BlockSpec(memory_space=pl.ANY),
            scratch_shapes=[pltpu.VMEM((2, 2 * tm * PACK_ROWS, LANES), ROW_DTYPE),
                            pltpu.VMEM((te * PACK_ROWS, LANES), ROW_DTYPE),
                            pltpu.SemaphoreType.DMA((2,)), pltpu.SemaphoreType.DMA(())]),
        out_shape=jax.ShapeDtypeStruct((n_rows * PACK_ROWS, LANES), ROW_DTYPE),
        compiler_params=_params("arbitrary"),
    )(run_dst, run_len, run_max, used, ends, n_valid, h2, rt)


def _expert_kernel(te_ref, nv_ref, xs_ref, wg_ref, wu_ref, wd_ref, ys_ref, wg_bf, wu_bf, wd_bf):
    j = pl.program_id(0)
    valid = j < nv_ref[0]
    changed = (j == 0) | (te_ref[j] != te_ref[jnp.maximum(j - 1, 0)])

    @pl.when(valid & changed)
    def _():
        wg_bf[...] = wg_ref[0].astype(BF16)
        wu_bf[...] = wu_ref[0].astype(BF16)
        wd_bf[...] = wd_ref[0].astype(BF16)

    @pl.when(valid)
    def _():
        xb = _unpack_rows(xs_ref)
        gate = _dot(xb, wg_bf[...])
        up = _dot(xb, wu_bf[...])
        act = gate * jax.nn.sigmoid(gate) * up
        _pack_rows(ys_ref, _dot(act.astype(BF16), wd_bf[...]))

    @pl.when(jnp.logical_not(valid))
    def _():
        ys_ref[...] = jnp.zeros_like(ys_ref)


def _expert_mlp(tile_expert, n_valid, xs, w_gate, w_up, w_down):
    te = EXPERT_TE
    d, ff = w_gate.shape[1:]
    n_rows = xs.shape[0] // PACK_ROWS
    blk = (te * PACK_ROWS, LANES)
    tile = lambda j, e, nv: (jnp.minimum(j, nv[0] - 1), 0)
    wsel = lambda j, e, nv: (e[j], 0, 0)
    return pl.pallas_call(
        _expert_kernel,
        grid_spec=pltpu.PrefetchScalarGridSpec(
            num_scalar_prefetch=2,
            grid=(n_rows // te,),
            in_specs=[pl.BlockSpec(blk, tile),
                      pl.BlockSpec((1, d, ff), wsel), pl.BlockSpec((1, d, ff), wsel),
                      pl.BlockSpec((1, ff, d), wsel)],
            out_specs=pl.BlockSpec(blk, lambda j, e, nv: (j, 0)),
            scratch_shapes=[pltpu.VMEM((d, ff), BF16), pltpu.VMEM((d, ff), BF16),
                            pltpu.VMEM((ff, d), BF16)]),
        out_shape=jax.ShapeDtypeStruct(xs.shape, ROW_DTYPE),
        compiler_params=_params("arbitrary"),
    )(tile_expert, n_valid, xs, w_gate, w_up, w_down)


def _combine_kernel(rdst_ref, rlen_ref, rmax_ref, x1_ref, rinfo_ref, g2_ref, gpost_ref, ys_ref, o_ref,
                    gath_ref, sem):
    k = pl.program_id(0)
    nk = pl.num_programs(0)
    tm = x1_ref.shape[0]
    rows = 2 * tm
    slot = k % 2

    def gather_runs(tile, s):
        buf = gath_ref.at[s]

        def copy_piece(local, src, size):
            pltpu.make_async_copy(_token_rows(ys_ref, src, size), _token_rows(buf, local, size),
                                  sem.at[s]).start()

        _for_each_run_piece(rdst_ref, rlen_ref, rmax_ref, tile, tm, copy_piece)

    @pl.when(k == 0)
    def _():
        gather_runs(0, 0)

    @pl.when(k + 1 < nk)
    def _():
        gather_runs(k + 1, 1 - slot)

    buf = gath_ref.at[slot]
    pltpu.make_async_copy(_token_rows(ys_ref, 0, rows), buf, sem.at[slot]).wait()
    g = _unpack_rows(buf)
    info = rinfo_ref[...]
    col = lax.broadcasted_iota(jnp.int32, (tm, rows), 1).astype(F32)
    pick0 = jnp.where(col == info[:, 2:3], 1.0, 0.0).astype(BF16)
    pick1 = jnp.where(col == info[:, 3:4], 1.0, 0.0).astype(BF16)
    fx = info[:, 4:5] * _dot(pick0, g) + info[:, 5:6] * _dot(pick1, g)
    o_ref[...] = x1_ref[...] + g2_ref[0] * (_rms(fx) * gpost_ref[...])


def _combine(run_dst, run_len, run_max, x1, rinfo, g2, gpost, ys, seq):
    t, d = x1.shape
    tm = MOVE_TM
    tpb = seq // tm
    return pl.pallas_call(
        _combine_kernel,
        grid_spec=pltpu.PrefetchScalarGridSpec(
            num_scalar_prefetch=3,
            grid=(t // tm,),
            in_specs=[pl.BlockSpec((tm, d), lambda i, *_: (i, 0)),
                      pl.BlockSpec((tm, LANES), lambda i, *_: (i, 0)),
                      pl.BlockSpec((1, 1, d), lambda i, *_: (i // tpb, 0, 0)),
                      pl.BlockSpec((1, d), lambda i, *_: (0, 0)),
                      pl.BlockSpec(memory_space=pl.ANY)],
            out_specs=pl.BlockSpec((tm, d), lambda i, *_: (i, 0)),
            scratch_shapes=[pltpu.VMEM((2, 2 * tm * PACK_ROWS, LANES), ROW_DTYPE),
                            pltpu.SemaphoreType.DMA((2,))]),
        out_shape=jax.ShapeDtypeStruct((t, d), F32),
        compiler_params=_params("arbitrary"),
    )(run_dst, run_len, run_max, x1, rinfo, g2, gpost, ys)


def _rope_tables(seq):
    pos = jnp.arange(seq, dtype=jnp.int32)
    row = (pos // GRID_W).astype(F32)
    col = (pos % GRID_W).astype(F32)
    axis_dim = HEAD_DIM // 2
    inv_freq = ROPE_THETA ** (-jnp.arange(0, axis_dim, 2, dtype=F32) / axis_dim)
    ang = jnp.concatenate([row[:, None] * inv_freq, col[:, None] * inv_freq], axis=-1)
    pair = (jnp.arange(LANES) % HEAD_DIM) // 2
    cos = jnp.cos(ang)[:, pair]
    sin = jnp.sin(ang)[:, pair]
    even = (jnp.arange(LANES) % 2) == 0
    tables = (cos, jnp.where(even, -sin, 0.0), jnp.where(even, 0.0, sin))
    return tables + tuple(tb.T for tb in tables)


def _segment_ones(n):
    seg = jnp.arange(n) // HEAD_DIM
    return (seg[:, None] == seg[None, :]).astype(BF16)


def kernel(x, c, ctx, c_ctx, w_mod, b_mod, attn_pre_norm, attn_post_norm, w_in, a_sink,
           b_q_norm, b_k_norm, a_out_norm, b_out_norm, w_out, ffn_pre_norm, ffn_post_norm,
           w_group, b_group, w_router, b_router, w_gate, w_up, w_down):
    batch, seq, d = x.shape
    ctx_len = ctx.shape[1]
    assert w_mod.shape[0] == 1, "single-layer stack only (context stream is never updated)"
    assert seq % ATTN_A_TQ == 0 and seq >= ATTN_A_TQ + 2 * WINDOW
    assert seq % PROJ_TM == 0 and seq % ATTN_B_TQ == 0 and seq % OUT_TM == 0 and seq % MOVE_TM == 0
    t = batch * seq
    nq = d // 2
    nkv = nq // KV_GROUP
    assert nkv == LANES and w_in.shape[2] == 2 * nq + 4 * nkv

    cc = jnp.concatenate([c, c_ctx[None, :], jnp.zeros((16 - batch - 1, d), F32)], axis=0)
    mod = _modulation(cc, w_mod[0], b_mod[0])
    sh1, sc1, g1, sh2, sc2, g2 = (m.reshape(batch, 1, d) for m in jnp.split(mod[:batch], 6, axis=-1))
    csh1, csc1 = (m.reshape(1, d) for m in jnp.split(mod[batch], 6)[:2])

    x2 = x.reshape(t, d)
    c2 = ctx.reshape(batch * ctx_len, d)
    gpre = attn_pre_norm[0].reshape(1, d)
    w_in_bf = w_in[0].astype(BF16)
    kv_cols = jnp.concatenate([w_in_bf[:, nq:nq + 2 * nkv], w_in_bf[:, 2 * nq + 2 * nkv:]], axis=1)
    qn = jnp.tile(b_q_norm[0], nq // HEAD_DIM).reshape(1, nq)
    kn = jnp.tile(b_k_norm[0], nkv // HEAD_DIM).reshape(1, nkv)
    seg_q, seg_k = _segment_ones(nq), _segment_ones(nkv)
    qat, ka, vat, qbt, kb, vbt, stats = _project_latents(
        x2, sc1, sh1, gpre, w_in_bf, _rope_tables(seq), qn, kn, seg_q, seg_k, seq)
    kca, vcat, kcb, vcbt, ctx_stats = _project_context(
        c2, csc1, csh1, gpre, kv_cols, kn, seg_k, ctx_len)

    q_sq = jnp.max(stats[:, 0, 0])
    k_sq = jnp.maximum(jnp.max(stats[:, 0, 1]), jnp.max(ctx_stats[:, 0, 1]))
    bound_a = jnp.maximum(1.01 * jnp.sqrt(q_sq * k_sq), jnp.max(a_sink[0]) * LOG2_E).reshape(1)
    oa = _attention_a(bound_a, a_sink[0], qat, ka, vat, kca, vcat, batch, seq, ctx_len)
    score_bound = (1.01 * HEAD_DIM ** 0.5 * LOG2_E
                   * jnp.max(jnp.abs(b_q_norm[0])) * jnp.max(jnp.abs(b_k_norm[0]))).reshape(1)
    ob = _attention_b(score_bound, qbt, kb, vbt, kcb, vcbt, batch, seq, ctx_len)

    w_out_bf = w_out[0].astype(BF16)
    w_r = jnp.zeros((d, LANES), F32)
    w_r = w_r.at[:, :N_EXPERTS].set(w_router[0]).at[:, N_EXPERTS:N_EXPERTS + N_GROUPS].set(w_group[0])
    w_r_hi = w_r.astype(BF16)
    w_r_lo = (w_r - w_r_hi.astype(F32)).astype(BF16)
    b_r = jnp.zeros((1, LANES), F32)
    b_r = b_r.at[0, :N_EXPERTS].set(b_router[0]).at[0, N_EXPERTS:N_EXPERTS + N_GROUPS].set(b_group[0])
    x1, h2, rinfo, rt, tcarry, tcnt, counts = _out_and_route(
        oa, ob, x2, g1, sc2, sh2, a_out_norm[0].reshape(1, nq), b_out_norm[0].reshape(1, nq),
        attn_post_norm[0].reshape(1, d), ffn_pre_norm[0].reshape(1, d),
        w_out_bf[:nq], w_out_bf[nq:], w_r_hi, w_r_lo, b_r, seq)

    te = EXPERT_TE
    n_tiles = -(-(2 * t + N_EXPERTS * (te - 1)) // te)
    n_rows = n_tiles * te
    cnt = counts[0, :N_EXPERTS].astype(jnp.int32)
    padded = ((cnt + te - 1) // te) * te
    ends = jnp.cumsum(padded)
    offs = ends - padded
    run_dst = (offs[None, :] + tcarry[:, 0, :N_EXPERTS].astype(jnp.int32)).reshape(-1)
    tile_runs = tcnt[:, 0, :N_EXPERTS].astype(jnp.int32)
    run_len = tile_runs.reshape(-1)
    run_max = jnp.max(tile_runs, axis=1)
    n_valid = (ends[-1] // te).astype(jnp.int32).reshape(1)
    tile_start = jnp.arange(n_tiles, dtype=jnp.int32) * te
    tile_expert = jnp.sum(ends[None, :] <= tile_start[:, None], axis=1).astype(jnp.int32)
    last_expert = tile_expert[jnp.maximum(n_valid[0] - 1, 0)]
    tile_expert = jnp.where(tile_start < ends[-1], tile_expert, last_expert)

    xs = _dispatch(run_dst, run_len, run_max, (offs + cnt).astype(jnp.int32),
                   ends.astype(jnp.int32), n_valid, h2, rt, n_rows)
    ys = _expert_mlp(tile_expert, n_valid, xs, w_gate[0], w_up[0], w_down[0])
    out = _combine(run_dst, run_len, run_max, x1, rinfo, g2, ffn_post_norm[0].reshape(1, d), ys,
                   seq)
    return out.reshape(batch, seq, d)
```

```python
import functools

import jax
import jax.numpy as jnp
import numpy as np
from jax import lax
from jax.experimental import pallas as pl
from jax.experimental.pallas import tpu as pltpu

F32 = jnp.float32
BF16 = jnp.bfloat16

GRID_W = 64
HEAD_DIM = 64
KV_GROUP = 4
WINDOW = 128
ROPE_THETA = 10000.0
N_GROUPS = 4
EXPERTS_PER_GROUP = 8
N_EXPERTS = N_GROUPS * EXPERTS_PER_GROUP
EPS = 1e-6
NEG_INF = -1e30
LOG2_E = 1.4426950408889634
SAFE_SOFTMAX_SHIFT = 40.0

LANES = 128
V7X_VMEM_LIMIT = 56 * 1024 * 1024

PROJ_TM = 512
ATTN_A_TQ = 256
ATTN_B_TQ = 256
OUT_TM = 512
EXPERT_TE = 512
MOVE_TM = OUT_TM


def _params(*sem):
    return pltpu.CompilerParams(dimension_semantics=sem, vmem_limit_bytes=V7X_VMEM_LIMIT)


def _dot(a, b):
    return jnp.dot(a, b, preferred_element_type=F32)


def _dot_nt(a, b):
    return lax.dot_general(a, b, (((1,), (1,)), ((), ())), preferred_element_type=F32)


def _rms(x):
    return x * lax.rsqrt(jnp.mean(x * x, axis=-1, keepdims=True) + EPS)


def _split_bf16(x):
    hi = x.astype(BF16)
    lo = (x - hi.astype(F32)).astype(BF16)
    return hi, lo


def _mod_kernel(c_ref, w_ref, b_ref, o_ref):
    cc = c_ref[...]
    s = cc * jax.nn.sigmoid(cc)
    s_hi, s_lo = _split_bf16(s)
    w_hi, w_lo = _split_bf16(w_ref[...])
    o_ref[...] = _dot(s_hi, w_hi) + _dot(s_lo, w_hi) + _dot(s_hi, w_lo) + b_ref[...]


def _modulation(cc, w_mod, b_mod):
    rows, d = cc.shape
    n = w_mod.shape[1]
    bn = 1024
    return pl.pallas_call(
        _mod_kernel,
        grid=(n // bn,),
        in_specs=[pl.BlockSpec((rows, d), lambda i: (0, 0)),
                  pl.BlockSpec((d, bn), lambda i: (0, i)),
                  pl.BlockSpec((1, bn), lambda i: (0, i))],
        out_specs=pl.BlockSpec((rows, bn), lambda i: (0, i)),
        out_shape=jax.ShapeDtypeStruct((rows, n), F32),
        compiler_params=_params("arbitrary"),
    )(cc, w_mod, b_mod.reshape(1, n))


def _rope(x, cos, sin_a, sin_b):
    return x * cos + pltpu.roll(x, LANES - 1, 1) * sin_a + pltpu.roll(x, 1, 1) * sin_b


def _head_norm(x, seg_ref, gain):
    ss = _dot((x * x).astype(BF16), seg_ref[...])
    return x * lax.rsqrt(ss * (1.0 / HEAD_DIM) + EPS) * gain


def _max_head_sq_norm(x, seg_ref):
    ss = _dot((x * x).astype(BF16), seg_ref[...])
    return jnp.max(jnp.max(ss, axis=1, keepdims=True), axis=0, keepdims=True)


def _norm_stats(q_sq, k_sq):
    lane = lax.broadcasted_iota(jnp.int32, (1, LANES), 1)
    zero = jnp.zeros((1, LANES), F32)
    return jnp.where(lane == 0, q_sq, zero) + jnp.where(lane == 1, k_sq, zero)


def _rope_t(xt, cos_t, sin_a_t, sin_b_t):
    return (xt * cos_t + pltpu.roll(xt, LANES - 1, 0) * sin_a_t
            + pltpu.roll(xt, 1, 0) * sin_b_t)


def _proj_kernel(x_ref, sc_ref, sh_ref, gpre_ref, w_ref, cos_ref, sa_ref, sb_ref,
                 cos_t_ref, sa_t_ref, sb_t_ref, qn_ref, kn_ref, seg_q_ref, seg_k_ref,
                 qat_ref, ka_ref, vat_ref, qbt_ref, kb_ref, vbt_ref, stats_ref):
    h = _rms(x_ref[...]) * gpre_ref[...] * (1.0 + sc_ref[0, 0]) + sh_ref[0, 0]
    p = _dot(h.astype(BF16), w_ref[...])
    cos, sa, sb = cos_ref[...], sa_ref[...], sb_ref[...]
    q_scale = HEAD_DIM ** -0.5 * LOG2_E
    cos_t, sa_t, sb_t = cos_t_ref[...] * q_scale, sa_t_ref[...] * q_scale, sb_t_ref[...] * q_scale
    nq = qat_ref.shape[0]
    for c in range(nq // LANES):
        qat_ref[c * LANES:(c + 1) * LANES, :] = _rope_t(
            p[:, c * LANES:(c + 1) * LANES].T, cos_t, sa_t, sb_t).astype(BF16)
    o = nq
    ka_ref[...] = _rope(p[:, o:o + LANES], cos, sa, sb).astype(BF16)
    vat_ref[...] = p[:, o + LANES:o + 2 * LANES].T.astype(BF16)
    stats_ref[0] = _norm_stats(_max_head_sq_norm(p[:, 0:nq], seg_q_ref) * (q_scale * q_scale),
                               _max_head_sq_norm(p[:, o:o + LANES], seg_k_ref))
    o += 2 * LANES
    qb = _head_norm(p[:, o:o + nq], seg_q_ref, qn_ref[...])
    for c in range(nq // LANES):
        qbt_ref[c * LANES:(c + 1) * LANES, :] = _rope_t(
            qb[:, c * LANES:(c + 1) * LANES].T, cos_t, sa_t, sb_t).astype(BF16)
    o += nq
    kb = _head_norm(p[:, o:o + LANES], seg_k_ref, kn_ref[...])
    kb_ref[...] = _rope(kb, cos, sa, sb).astype(BF16)
    vbt_ref[...] = p[:, o + LANES:o + 2 * LANES].T.astype(BF16)


def _ctx_proj_kernel(x_ref, sc_ref, sh_ref, gpre_ref, wa_ref, wb_ref, kn_ref, seg_k_ref,
                     ka_ref, vat_ref, kb_ref, vbt_ref, stats_ref):
    h = (_rms(x_ref[...]) * gpre_ref[...] * (1.0 + sc_ref[0, 0]) + sh_ref[0, 0]).astype(BF16)
    pa = _dot(h, wa_ref[...])
    pb = _dot(h, wb_ref[...])
    stats_ref[0] = _norm_stats(0.0, _max_head_sq_norm(pa[:, 0:LANES], seg_k_ref))
    ka_ref[...] = pa[:, 0:LANES].astype(BF16)
    vat_ref[...] = pa[:, LANES:2 * LANES].T.astype(BF16)
    kb_ref[...] = _head_norm(pb[:, 0:LANES], seg_k_ref, kn_ref[...]).astype(BF16)
    vbt_ref[...] = pb[:, LANES:2 * LANES].T.astype(BF16)


def _mod_spec(chunk, row_of_step):
    return lambda d: pl.BlockSpec((1, 1, 1, d), lambda i, *_: (row_of_step(i), chunk, 0, 0))


MOD_SH1, MOD_SC1, MOD_G1, MOD_SH2, MOD_SC2, MOD_G2 = range(6)


def _project_latents(x2, mod, gpre, w_in, tables, qn, kn, seg_q, seg_k, seq):
    t, d = x2.shape
    tm = PROJ_TM
    tpb = seq // tm
    nq = seg_q.shape[0]
    const = lambda shape: pl.BlockSpec(shape, lambda i: (0,) * len(shape))
    batch_of = lambda i: i // tpb
    table = pl.BlockSpec((tm, LANES), lambda i: (i % tpb, 0))
    table_t = pl.BlockSpec((LANES, tm), lambda i: (0, i % tpb))
    k_spec = pl.BlockSpec((tm, LANES), lambda i: (i, 0))
    k_shape = jax.ShapeDtypeStruct((t, LANES), BF16)
    vt_spec = pl.BlockSpec((LANES, tm), lambda i: (0, i))
    vt_shape = jax.ShapeDtypeStruct((LANES, t), BF16)
    qt_spec = pl.BlockSpec((nq, tm), lambda i: (0, i))
    qt_shape = jax.ShapeDtypeStruct((nq, t), BF16)
    return pl.pallas_call(
        _proj_kernel,
        grid=(t // tm,),
        in_specs=[pl.BlockSpec((tm, d), lambda i: (i, 0)),
                  _mod_spec(MOD_SC1, batch_of)(d), _mod_spec(MOD_SH1, batch_of)(d), const((1, d)),
                  const(w_in.shape), table, table, table, table_t, table_t, table_t,
                  const((1, nq)), const((1, LANES)), const(seg_q.shape), const(seg_k.shape)],
        out_specs=[qt_spec, k_spec, vt_spec, qt_spec, k_spec, vt_spec,
                   pl.BlockSpec((1, 1, LANES), lambda i: (i, 0, 0))],
        out_shape=[qt_shape, k_shape, vt_shape, qt_shape, k_shape, vt_shape,
                   jax.ShapeDtypeStruct((t // tm, 1, LANES), F32)],
        compiler_params=_params("arbitrary"),
    )(x2, mod, mod, gpre, w_in, *tables, qn, kn, seg_q, seg_k)


def _project_context(c2, mod, ctx_row, gpre, w_in, kn, seg_k, ctx_len):
    t, d = c2.shape
    ctx_mod = lambda i: ctx_row
    nq = (w_in.shape[1] - 4 * LANES) // 2
    kv = 2 * LANES
    assert nq % kv == 0
    group_kv = lambda g: pl.BlockSpec((d, kv), lambda i: (0, (g * (nq + kv) + nq) // kv))
    const = lambda shape: pl.BlockSpec(shape, lambda i: (0,) * len(shape))
    k_spec = pl.BlockSpec((ctx_len, LANES), lambda i: (i, 0))
    k_shape = jax.ShapeDtypeStruct((t, LANES), BF16)
    vt_spec = pl.BlockSpec((LANES, ctx_len), lambda i: (0, i))
    vt_shape = jax.ShapeDtypeStruct((LANES, t), BF16)
    return pl.pallas_call(
        _ctx_proj_kernel,
        grid=(t // ctx_len,),
        in_specs=[pl.BlockSpec((ctx_len, d), lambda i: (i, 0)),
                  _mod_spec(MOD_SC1, ctx_mod)(d), _mod_spec(MOD_SH1, ctx_mod)(d),
                  const((1, d)), group_kv(0), group_kv(1), const((1, LANES)), const(seg_k.shape)],
        out_specs=[k_spec, vt_spec, k_spec, vt_spec,
                   pl.BlockSpec((1, 1, LANES), lambda i: (i, 0, 0))],
        out_shape=[k_shape, vt_shape, k_shape, vt_shape,
                   jax.ShapeDtypeStruct((t // ctx_len, 1, LANES), F32)],
        compiler_params=_params("arbitrary"),
    )(c2, mod, mod, gpre, w_in, w_in, kn, seg_k)


def _attend_t(w, k, kc, vt, vct, shift=None, bias=None, sink=None):
    st = _dot(k, w)
    sct = _dot(kc, w)
    if bias is not None:
        tq = bias.shape[1]
        st = jnp.concatenate([st[:, c * tq:(c + 1) * tq] + bias
                              for c in range(st.shape[1] // tq)], axis=1)
    if shift is None:
        shift = jnp.maximum(jnp.max(st, axis=0, keepdims=True),
                            jnp.max(sct, axis=0, keepdims=True))
        if sink is not None:
            shift = jnp.maximum(shift, sink)
    pt = jnp.exp2(st - shift)
    pct = jnp.exp2(sct - shift)
    denom = jnp.sum(pt, axis=0, keepdims=True) + jnp.sum(pct, axis=0, keepdims=True)
    if sink is not None:
        denom = denom + jnp.exp2(sink - shift)
    o2 = _dot(vt, pt.astype(BF16)) + _dot(vct, pct.astype(BF16))
    return o2, denom


def _all_heads_t(qt_ref, o_ref, attend):
    tq = qt_ref.shape[1]
    n_kv = LANES // HEAD_DIM
    zeros = jnp.zeros((HEAD_DIM, tq), BF16)
    outs = []
    for h in range(qt_ref.shape[0] // HEAD_DIM):
        g = h // KV_GROUP
        qh = qt_ref[h * HEAD_DIM:(h + 1) * HEAD_DIM, :]
        w = jnp.concatenate([zeros] * g + [qh] + [zeros] * (n_kv - 1 - g), axis=0)
        o2, denom = attend(h, w)
        outs.append(o2[g * HEAD_DIM:(g + 1) * HEAD_DIM, :] / denom)
    o_ref[...] = jnp.concatenate(outs, axis=0).T.astype(BF16)


def _all_heads_fused_t(qt_ref, o_ref, attend):
    tq = qt_ref.shape[1]
    n_heads = qt_ref.shape[0] // HEAD_DIM
    n_kv = LANES // HEAD_DIM
    rows = []
    for g in range(n_kv):
        heads = [qt_ref[h * HEAD_DIM:(h + 1) * HEAD_DIM, :] if h // KV_GROUP == g
                 else jnp.zeros((HEAD_DIM, tq), BF16) for h in range(n_heads)]
        rows.append(jnp.concatenate(heads, axis=1))
    w = jnp.concatenate(rows, axis=0)
    o2, denom = attend(0, w)
    o2 = o2 / denom
    outs = [o2[(h // KV_GROUP) * HEAD_DIM:(h // KV_GROUP + 1) * HEAD_DIM, h * tq:(h + 1) * tq]
            for h in range(n_heads)]
    o_ref[...] = jnp.concatenate(outs, axis=0).T.astype(BF16)


def _attn_b_kernel(bound_ref, qt_ref, k_ref, vt_ref, kc_ref, vct_ref, o_ref):
    k, kc, vt, vct = k_ref[...], kc_ref[...], vt_ref[...], vct_ref[...]
    bound = bound_ref[0]

    @pl.when(bound <= SAFE_SOFTMAX_SHIFT)
    def _():
        _all_heads_fused_t(qt_ref, o_ref, lambda h, w: _attend_t(w, k, kc, vt, vct, shift=bound))

    @pl.when(jnp.logical_not(bound <= SAFE_SOFTMAX_SHIFT))
    def _():
        _all_heads_t(qt_ref, o_ref, lambda h, w: _attend_t(w, k, kc, vt, vct))


def _attn_specs(nq, tq, seq, ctx_len):
    nqb = seq // tq
    return dict(
        qt=pl.BlockSpec((nq, tq), lambda b, i: (0, b * nqb + i)),
        k=pl.BlockSpec((seq, LANES), lambda b, i: (b, 0)),
        vt=pl.BlockSpec((LANES, seq), lambda b, i: (0, b)),
        kc=pl.BlockSpec((ctx_len, LANES), lambda b, i: (b, 0)),
        vct=pl.BlockSpec((LANES, ctx_len), lambda b, i: (0, b)),
        out=pl.BlockSpec((tq, nq), lambda b, i: (b * nqb + i, 0)))


def _attention_b(score_bound, qbt, kb, vbt, kcb, vcbt, batch, seq, ctx_len):
    nq, t = qbt.shape
    tq = ATTN_B_TQ
    sp = _attn_specs(nq, tq, seq, ctx_len)
    return pl.pallas_call(
        _attn_b_kernel,
        grid=(batch, seq // tq),
        in_specs=[pl.BlockSpec(memory_space=pltpu.SMEM),
                  sp["qt"], sp["k"], sp["vt"], sp["kc"], sp["vct"]],
        out_specs=sp["out"],
        out_shape=jax.ShapeDtypeStruct((t, nq), BF16),
        compiler_params=_params("arbitrary", "arbitrary"),
    )(score_bound, qbt, kb, vbt, kcb, vcbt)


def _attn_a_kernel(bound_ref, sink_ref, qt_ref, k_ref, vt_ref, kc_ref, vct_ref, bias_ref, o_ref,
                   *, seq):
    i = pl.program_id(1)
    tq = qt_ref.shape[1]
    n_heads = qt_ref.shape[0] // HEAD_DIM
    win = tq + 2 * WINDOW
    start = pl.multiple_of(jnp.clip(i * tq - WINDOW, 0, seq - win), WINDOW)
    k = k_ref[pl.ds(start, win), :]
    vt = vt_ref[:, pl.ds(start, win)]
    kc, vct = kc_ref[...], vct_ref[...]
    sinks = [sink_ref[h] * LOG2_E for h in range(n_heads)]

    bound = bound_ref[0]
    small = bound <= SAFE_SOFTMAX_SHIFT

    @pl.when(small)
    def _():
        sink_row = jnp.concatenate([jnp.full((1, tq), s, F32) for s in sinks], axis=1)
        _all_heads_fused_t(qt_ref, o_ref, lambda h, w: _attend_t(
            w, k, kc, vt, vct, shift=bound, bias=bias_ref[0], sink=sink_row))

    @pl.when(jnp.logical_not(small))
    def _():
        _all_heads_t(qt_ref, o_ref, lambda h, w: _attend_t(
            w, k, kc, vt, vct, bias=bias_ref[0], sink=sinks[h]))


def _band_bias(tq):
    win = tq + 2 * WINDOW
    r = np.arange(win)[:, None]
    j = np.arange(tq)[None, :]
    tables = [np.where(np.abs(off + r - j) <= WINDOW, 0.0, NEG_INF)
              for off in (0, -WINDOW, -2 * WINDOW)]
    return jnp.asarray(np.stack(tables), F32)


def _attention_a(score_bound, sink, qat, ka, vat, kca, vcat, batch, seq, ctx_len):
    nq, t = qat.shape
    tq = ATTN_A_TQ
    nqb = seq // tq
    win = tq + 2 * WINDOW
    assert nqb >= 2 and tq >= WINDOW
    sp = _attn_specs(nq, tq, seq, ctx_len)
    which = lambda b, i: (jnp.where(i == 0, 0, jnp.where(i == nqb - 1, 2, 1)), 0, 0)
    return pl.pallas_call(
        functools.partial(_attn_a_kernel, seq=seq),
        grid=(batch, nqb),
        in_specs=[pl.BlockSpec(memory_space=pltpu.SMEM), pl.BlockSpec(memory_space=pltpu.SMEM),
                  sp["qt"], sp["k"], sp["vt"], sp["kc"], sp["vct"],
                  pl.BlockSpec((1, win, tq), which)],
        out_specs=sp["out"],
        out_shape=jax.ShapeDtypeStruct((t, nq), BF16),
        compiler_params=_params("arbitrary", "arbitrary"),
    )(score_bound, sink, qat, ka, vat, kca, vcat, _band_bias(tq))


def _out_kernel(oa_ref, ob_ref, x_ref, g1_ref, sc2_ref, sh2_ref, ga_ref, gb_ref, gpost_ref,
                gpre2_ref, woa_ref, wob_ref, wrh_ref, wrl_ref, br_ref,
                x1_ref, h2_ref, rinfo_ref, rt_ref, tcarry_ref, tcnt_ref, cnt_ref, carry_ref):
    step = pl.program_id(0)

    @pl.when(step == 0)
    def _():
        carry_ref[...] = jnp.zeros_like(carry_ref)

    na = _rms(oa_ref[...].astype(F32)) * ga_ref[...]
    nb = _rms(ob_ref[...].astype(F32)) * gb_ref[...]
    ox = _dot(na.astype(BF16), woa_ref[...]) + _dot(nb.astype(BF16), wob_ref[...])
    x1 = x_ref[...] + g1_ref[0, 0] * (_rms(ox) * gpost_ref[...])
    x1_ref[...] = x1
    h2 = _rms(x1) * gpre2_ref[...] * (1.0 + sc2_ref[0, 0]) + sh2_ref[0, 0]
    h_hi, h_lo = _split_bf16(h2)
    h2_ref[...] = h_hi

    logits = (_dot(h_hi, wrh_ref[...]) + _dot(h_lo, wrh_ref[...]) + _dot(h_hi, wrl_ref[...])
              + br_ref[...])
    tm = logits.shape[0]
    lt = logits.T
    row = lax.broadcasted_iota(jnp.int32, lt.shape, 0)
    rowf = row.astype(F32)
    big = jnp.float32(1e9)
    ninf = jnp.float32(-jnp.inf)
    colmax = lambda v: jnp.max(v, axis=0, keepdims=True)
    colmin = lambda v: jnp.min(v, axis=0, keepdims=True)
    colsum = lambda v: jnp.sum(v, axis=0, keepdims=True)

    gmask = (row >= N_EXPERTS) & (row < N_EXPERTS + N_GROUPS)
    lg = jnp.where(gmask, lt, ninf)
    gmax = colmax(lg)
    gidx = colmin(jnp.where(lg == gmax, rowf, big)) - N_EXPERTS
    g_w = 1.0 / colsum(jnp.exp(lg - gmax))
    row_group = (row // EXPERTS_PER_GROUP).astype(F32)
    emask = (row < N_EXPERTS) & (row_group == gidx)
    le = jnp.where(emask, lt, ninf)
    m1 = colmax(le)
    i1 = colmin(jnp.where(le == m1, rowf, big))
    le2 = jnp.where(rowf == i1, ninf, le)
    m2 = colmax(le2)
    i2 = colmin(jnp.where(le2 == m2, rowf, big))
    e2 = jnp.exp(m2 - m1)
    w0 = g_w / (1.0 + e2)
    w1 = g_w * e2 / (1.0 + e2)

    hit1 = rowf == i1
    hit2 = rowf == i2
    onehot = jnp.where(hit1, 1.0, jnp.where(hit2, 1.0, 0.0)).astype(F32)
    r = lax.broadcasted_iota(jnp.int32, (tm, tm), 0)
    c = lax.broadcasted_iota(jnp.int32, (tm, tm), 1)
    earlier = jnp.where(r < c, 1.0, 0.0).astype(BF16)
    within = _dot(onehot.astype(BF16), earlier)
    tile_cnt = jnp.broadcast_to(jnp.sum(onehot, axis=1, keepdims=True), (LANES, LANES))
    er = lax.broadcasted_iota(jnp.int32, (LANES, LANES), 0)
    ec = lax.broadcasted_iota(jnp.int32, (LANES, LANES), 1)
    below = jnp.where(er > ec, 1.0, 0.0).astype(BF16)
    cnt_hi = jnp.floor(tile_cnt * (1.0 / 32.0))
    cnt_lo = tile_cnt - 32.0 * cnt_hi
    run_start = 32.0 * _dot(below, cnt_hi.astype(BF16)) + _dot(below, cnt_lo.astype(BF16))
    local = within + run_start[:, 0:1]
    pos0 = colsum(jnp.where(hit1, local, 0.0))
    pos1 = colsum(jnp.where(hit2, local, 0.0))
    cnt_row = tile_cnt.T[0:1, :]
    tcarry_ref[0] = carry_ref[...]
    tcnt_ref[0] = cnt_row
    carry_ref[...] += cnt_row
    cnt_ref[...] = carry_ref[...]

    fields = jnp.concatenate([i1, i2, pos0, pos1, w0, w1, jnp.zeros((2, tm), F32)], axis=0)
    rt_ref[...] = fields
    rinfo_ref[...] = jnp.concatenate(
        [fields, jnp.zeros((LANES - 8, tm), F32)], axis=0).T


def _out_and_route(oa, ob, x2, mod, ga, gb, gpost, gpre2, w_out, wrh, wrl, br, seq):
    t, d = x2.shape
    tm = OUT_TM
    tpb = seq // tm
    nq = oa.shape[1]
    const = lambda shape: pl.BlockSpec(shape, lambda i: (0,) * len(shape))
    batch_of = lambda i: i // tpb
    rows = lambda n: pl.BlockSpec((tm, n), lambda i: (i, 0))
    per_tile = pl.BlockSpec((1, 1, LANES), lambda i: (i, 0, 0))
    w_half = lambda g: pl.BlockSpec((nq, d), lambda i: (g, 0))
    return pl.pallas_call(
        _out_kernel,
        grid=(t // tm,),
        in_specs=[rows(nq), rows(nq), rows(d),
                  _mod_spec(MOD_G1, batch_of)(d), _mod_spec(MOD_SC2, batch_of)(d),
                  _mod_spec(MOD_SH2, batch_of)(d),
                  const((1, nq)), const((1, nq)), const((1, d)), const((1, d)),
                  w_half(0), w_half(1), const(wrh.shape), const(wrl.shape),
                  const((1, LANES))],
        out_specs=[rows(d), rows(d), rows(LANES), pl.BlockSpec((8, tm), lambda i: (0, i)),
                   per_tile, per_tile, const((1, LANES))],
        out_shape=[jax.ShapeDtypeStruct((t, d), F32), jax.ShapeDtypeStruct((t, d), BF16),
                   jax.ShapeDtypeStruct((t, LANES), F32), jax.ShapeDtypeStruct((8, t), F32),
                   jax.ShapeDtypeStruct((t // tm, 1, LANES), F32),
                   jax.ShapeDtypeStruct((t // tm, 1, LANES), F32),
                   jax.ShapeDtypeStruct((1, LANES), F32)],
        scratch_shapes=[pltpu.VMEM((1, LANES), F32)],
        compiler_params=_params("arbitrary"),
    )(oa, ob, x2, mod, mod, mod, ga, gb, gpost, gpre2, w_out, w_out, wrh, wrl, br)


PACK_ROWS = 8
ROW_DTYPE = F32


def _pack_rows(ref, x):
    n = x.shape[0]
    for c in range(PACK_ROWS):
        ref[pl.ds(c, n, stride=PACK_ROWS), :] = x[:, c * LANES:(c + 1) * LANES]


def _unpack_rows(ref):
    n = ref.shape[0] // PACK_ROWS
    return jnp.concatenate(
        [ref[pl.ds(c, n, stride=PACK_ROWS), :].astype(BF16) for c in range(PACK_ROWS)], axis=1)


def _for_each_run_piece(rdst_ref, rlen_ref, tile, max_len, fn):
    n_bits = max_len.bit_length()

    def run(e, local):
        length = rlen_ref[tile * N_EXPERTS + e]
        dst = rdst_ref[tile * N_EXPERTS + e]
        for b in range(n_bits):
            size = 1 << b

            @pl.when(((length >> b) & 1) == 1)
            def _():
                done = length & (size - 1)
                fn(local + done, dst + done, size)
        return local + length

    lax.fori_loop(0, N_EXPERTS, run, 0)


def _token_rows(ref, row0, n_rows):
    start = row0 * PACK_ROWS
    if not isinstance(start, int):
        start = pl.multiple_of(start, PACK_ROWS)
    return ref.at[pl.ds(start, n_rows * PACK_ROWS)]


def _dispatch_kernel(rdst_ref, rlen_ref, used_ref, ends_ref, nv_ref, h_ref, rt_ref, xs_ref,
                     sorted_ref, zero_ref, sem, zsem, *, te, n_tiles):
    k = pl.program_id(0)
    nk = pl.num_programs(0)
    tm = h_ref.shape[0]
    rows = 2 * tm
    slot = k % 2

    def wait_slot(s):
        pltpu.make_async_copy(sorted_ref.at[s], _token_rows(xs_ref, 0, rows), sem.at[s]).wait()

    @pl.when(k == 0)
    def _():
        zero_ref[...] = jnp.zeros_like(zero_ref)

        def pad_copy(row0, size):
            return pltpu.make_async_copy(_token_rows(zero_ref, 0, size),
                                         _token_rows(xs_ref, row0, size), zsem)

        def for_each_pad_tile(fn):
            def expert_pad(e, carry):
                start = used_ref[e]
                length = ends_ref[e] - start
                for b in range((te - 1).bit_length()):
                    size = 1 << b

                    @pl.when(((length >> b) & 1) == 1)
                    def _():
                        fn(pad_copy(start + (length & (size - 1)), size))
                return carry

            def tail_pad(j, carry):
                fn(pad_copy(j * te, te))
                return carry

            lax.fori_loop(0, N_EXPERTS, expert_pad, 0)
            lax.fori_loop(nv_ref[0], n_tiles, tail_pad, 0)

        for_each_pad_tile(lambda cp: cp.start())
        for_each_pad_tile(lambda cp: cp.wait())

    @pl.when(k >= 2)
    def _():
        wait_slot(slot)

    pos0 = rt_ref[2:3, :]
    pos1 = rt_ref[3:4, :]
    r = lax.broadcasted_iota(jnp.int32, (rows, tm), 0).astype(F32)
    perm = jnp.where((r == pos0) | (r == pos1), 1.0, 0.0).astype(BF16)
    srt = _dot(perm, h_ref[...].astype(BF16))
    buf = sorted_ref.at[slot]
    _pack_rows(buf, srt)

    def copy_piece(local, dst, size):
        pltpu.make_async_copy(_token_rows(buf, local, size), _token_rows(xs_ref, dst, size),
                              sem.at[slot]).start()

    _for_each_run_piece(rdst_ref, rlen_ref, k, tm, copy_piece)

    @pl.when(k == nk - 1)
    def _():
        wait_slot(slot)

        @pl.when(nk >= 2)
        def _():
            wait_slot(1 - slot)


def _dispatch(run_dst, run_len, used, ends, n_valid, h2, rt, n_rows):
    t, d = h2.shape
    assert d == PACK_ROWS * LANES
    tm = MOVE_TM
    te = EXPERT_TE
    return pl.pallas_call(
        functools.partial(_dispatch_kernel, te=te, n_tiles=n_rows // te),
        grid_spec=pltpu.PrefetchScalarGridSpec(
            num_scalar_prefetch=5,
            grid=(t // tm,),
            in_specs=[pl.BlockSpec((tm, d), lambda i, *_: (i, 0)),
                      pl.BlockSpec((8, tm), lambda i, *_: (0, i))],
            out_specs=pl.BlockSpec(memory_space=pl.ANY),
            scratch_shapes=[pltpu.VMEM((2, 2 * tm * PACK_ROWS, LANES), ROW_DTYPE),
                            pltpu.VMEM((te * PACK_ROWS, LANES), ROW_DTYPE),
                            pltpu.SemaphoreType.DMA((2,)), pltpu.SemaphoreType.DMA(())]),
        out_shape=jax.ShapeDtypeStruct((n_rows * PACK_ROWS, LANES), ROW_DTYPE),
        compiler_params=_params("arbitrary"),
    )(run_dst, run_len, used, ends, n_valid, h2, rt)


def _expert_kernel(te_ref, nv_ref, xs_ref, wg_ref, wu_ref, wd_ref, ys_ref, wg_bf, wu_bf, wd_bf):
    j = pl.program_id(0)
    valid = j < nv_ref[0]
    changed = (j == 0) | (te_ref[j] != te_ref[jnp.maximum(j - 1, 0)])

    @pl.when(valid & changed)
    def _():
        wg_bf[...] = wg_ref[0].astype(BF16)
        wu_bf[...] = wu_ref[0].astype(BF16)
        wd_bf[...] = wd_ref[0].astype(BF16)

    @pl.when(valid)
    def _():
        xb = _unpack_rows(xs_ref)
        gate = _dot(xb, wg_bf[...])
        up = _dot(xb, wu_bf[...])
        act = gate * jax.nn.sigmoid(gate) * up
        _pack_rows(ys_ref, _dot(act.astype(BF16), wd_bf[...]))

    @pl.when(jnp.logical_not(valid))
    def _():
        ys_ref[...] = jnp.zeros_like(ys_ref)


def _expert_mlp(tile_expert, n_valid, xs, w_gate, w_up, w_down):
    te = EXPERT_TE
    d, ff = w_gate.shape[1:]
    n_rows = xs.shape[0] // PACK_ROWS
    blk = (te * PACK_ROWS, LANES)
    tile = lambda j, e, nv: (jnp.minimum(j, nv[0] - 1), 0)
    wsel = lambda j, e, nv: (e[j], 0, 0)
    return pl.pallas_call(
        _expert_kernel,
        grid_spec=pltpu.PrefetchScalarGridSpec(
            num_scalar_prefetch=2,
            grid=(n_rows // te,),
            in_specs=[pl.BlockSpec(blk, tile),
                      pl.BlockSpec((1, d, ff), wsel), pl.BlockSpec((1, d, ff), wsel),
                      pl.BlockSpec((1, ff, d), wsel)],
            out_specs=pl.BlockSpec(blk, lambda j, e, nv: (j, 0)),
            scratch_shapes=[pltpu.VMEM((d, ff), BF16), pltpu.VMEM((d, ff), BF16),
                            pltpu.VMEM((ff, d), BF16)]),
        out_shape=jax.ShapeDtypeStruct(xs.shape, ROW_DTYPE),
        compiler_params=_params("arbitrary"),
    )(tile_expert, n_valid, xs, w_gate, w_up, w_down)


def _combine_kernel(rdst_ref, rlen_ref, x1_ref, rinfo_ref, g2_ref, gpost_ref, ys_ref, o_ref,
                    gath_ref, sem):
    k = pl.program_id(0)
    nk = pl.num_programs(0)
    tm = x1_ref.shape[0]
    rows = 2 * tm
    slot = k % 2

    def gather_runs(tile, s):
        buf = gath_ref.at[s]

        def copy_piece(local, src, size):
            pltpu.make_async_copy(_token_rows(ys_ref, src, size), _token_rows(buf, local, size),
                                  sem.at[s]).start()

        _for_each_run_piece(rdst_ref, rlen_ref, tile, tm, copy_piece)

    @pl.when(k == 0)
    def _():
        gather_runs(0, 0)

    @pl.when(k + 1 < nk)
    def _():
        gather_runs(k + 1, 1 - slot)

    buf = gath_ref.at[slot]
    pltpu.make_async_copy(_token_rows(ys_ref, 0, rows), buf, sem.at[slot]).wait()
    g = _unpack_rows(buf)
    info = rinfo_ref[...]
    col = lax.broadcasted_iota(jnp.int32, (tm, rows), 1).astype(F32)
    pick0 = jnp.where(col == info[:, 2:3], 1.0, 0.0).astype(BF16)
    pick1 = jnp.where(col == info[:, 3:4], 1.0, 0.0).astype(BF16)
    fx = info[:, 4:5] * _dot(pick0, g) + info[:, 5:6] * _dot(pick1, g)
    o_ref[...] = x1_ref[...] + g2_ref[0, 0] * (_rms(fx) * gpost_ref[...])


def _combine(run_dst, run_len, x1, rinfo, mod, gpost, ys, seq):
    t, d = x1.shape
    tm = MOVE_TM
    tpb = seq // tm
    batch_of = lambda i: i // tpb
    return pl.pallas_call(
        _combine_kernel,
        grid_spec=pltpu.PrefetchScalarGridSpec(
            num_scalar_prefetch=2,
            grid=(t // tm,),
            in_specs=[pl.BlockSpec((tm, d), lambda i, *_: (i, 0)),
                      pl.BlockSpec((tm, LANES), lambda i, *_: (i, 0)),
                      _mod_spec(MOD_G2, batch_of)(d),
                      pl.BlockSpec((1, d), lambda i, *_: (0, 0)),
                      pl.BlockSpec(memory_space=pl.ANY)],
            out_specs=pl.BlockSpec((tm, d), lambda i, *_: (i, 0)),
            scratch_shapes=[pltpu.VMEM((2, 2 * tm * PACK_ROWS, LANES), ROW_DTYPE),
                            pltpu.SemaphoreType.DMA((2,))]),
        out_shape=jax.ShapeDtypeStruct((t, d), F32),
        compiler_params=_params("arbitrary"),
    )(run_dst, run_len, x1, rinfo, mod, gpost, ys)


def _rope_tables(seq):
    pos = np.arange(seq)
    row = (pos // GRID_W).astype(np.float32)
    col = (pos % GRID_W).astype(np.float32)
    axis_dim = HEAD_DIM // 2
    inv_freq = (ROPE_THETA ** (-np.arange(0, axis_dim, 2, dtype=np.float32) / axis_dim)).astype(
        np.float32)
    ang = np.concatenate([row[:, None] * inv_freq, col[:, None] * inv_freq], axis=-1)
    pair = (np.arange(LANES) % HEAD_DIM) // 2
    cos = np.cos(ang)[:, pair]
    sin = np.sin(ang)[:, pair]
    even = (np.arange(LANES) % 2) == 0
    tables = (cos, np.where(even, -sin, 0.0), np.where(even, 0.0, sin))
    tables = tables + tuple(tb.T for tb in tables)
    return tuple(jnp.asarray(tb, F32) for tb in tables)


def _segment_ones(n):
    seg = np.arange(n) // HEAD_DIM
    return jnp.asarray(seg[:, None] == seg[None, :], BF16)


def kernel(x, c, ctx, c_ctx, w_mod, b_mod, attn_pre_norm, attn_post_norm, w_in, a_sink,
           b_q_norm, b_k_norm, a_out_norm, b_out_norm, w_out, ffn_pre_norm, ffn_post_norm,
           w_group, b_group, w_router, b_router, w_gate, w_up, w_down):
    batch, seq, d = x.shape
    ctx_len = ctx.shape[1]
    assert w_mod.shape[0] == 1, "single-layer stack only (context stream is never updated)"
    assert seq % ATTN_A_TQ == 0 and seq >= ATTN_A_TQ + 2 * WINDOW
    assert seq % PROJ_TM == 0 and seq % ATTN_B_TQ == 0 and seq % OUT_TM == 0 and seq % MOVE_TM == 0
    t = batch * seq
    nq = d // 2
    nkv = nq // KV_GROUP
    assert nkv == LANES and w_in.shape[2] == 2 * nq + 4 * nkv

    cc = jnp.concatenate([c, c_ctx[None, :], jnp.zeros((16 - batch - 1, d), F32)], axis=0)
    mod = _modulation(cc, w_mod[0], b_mod[0]).reshape(cc.shape[0], 6, 1, d)

    x2 = x.reshape(t, d)
    c2 = ctx.reshape(batch * ctx_len, d)
    gpre = attn_pre_norm[0].reshape(1, d)
    w_in_bf = w_in[0].astype(BF16)
    qn = jnp.tile(b_q_norm[0], nq // HEAD_DIM).reshape(1, nq)
    kn = jnp.tile(b_k_norm[0], nkv // HEAD_DIM).reshape(1, nkv)
    seg_q, seg_k = _segment_ones(nq), _segment_ones(nkv)
    qat, ka, vat, qbt, kb, vbt, stats = _project_latents(
        x2, mod, gpre, w_in_bf, _rope_tables(seq), qn, kn, seg_q, seg_k, seq)
    kca, vcat, kcb, vcbt, ctx_stats = _project_context(
        c2, mod, batch, gpre, w_in_bf, kn, seg_k, ctx_len)

    q_sq = jnp.max(stats[:, 0, 0])
    k_sq = jnp.maximum(jnp.max(stats[:, 0, 1]), jnp.max(ctx_stats[:, 0, 1]))
    bound_a = jnp.maximum(1.01 * jnp.sqrt(q_sq * k_sq), jnp.max(a_sink[0]) * LOG2_E).reshape(1)
    oa = _attention_a(bound_a, a_sink[0], qat, ka, vat, kca, vcat, batch, seq, ctx_len)
    score_bound = (1.01 * HEAD_DIM ** 0.5 * LOG2_E
                   * jnp.max(jnp.abs(b_q_norm[0])) * jnp.max(jnp.abs(b_k_norm[0]))).reshape(1)
    ob = _attention_b(score_bound, qbt, kb, vbt, kcb, vcbt, batch, seq, ctx_len)

    w_out_bf = w_out[0].astype(BF16)
    lane_pad = LANES - N_EXPERTS - N_GROUPS
    w_r = jnp.pad(jnp.concatenate([w_router[0], w_group[0]], axis=1), ((0, 0), (0, lane_pad)))
    w_r_hi = w_r.astype(BF16)
    w_r_lo = (w_r - w_r_hi.astype(F32)).astype(BF16)
    b_r = jnp.pad(jnp.concatenate([b_router[0], b_group[0]]), (0, lane_pad)).reshape(1, LANES)
    x1, h2, rinfo, rt, tcarry, tcnt, counts = _out_and_route(
        oa, ob, x2, mod, a_out_norm[0].reshape(1, nq), b_out_norm[0].reshape(1, nq),
        attn_post_norm[0].reshape(1, d), ffn_pre_norm[0].reshape(1, d),
        w_out_bf, w_r_hi, w_r_lo, b_r, seq)

    te = EXPERT_TE
    n_tiles = -(-(2 * t + N_EXPERTS * (te - 1)) // te)
    n_rows = n_tiles * te
    cnt = counts[0, :N_EXPERTS].astype(jnp.int32)
    padded = ((cnt + te - 1) // te) * te
    ends = jnp.cumsum(padded)
    offs = ends - padded
    run_dst = (offs[None, :] + tcarry[:, 0, :N_EXPERTS].astype(jnp.int32)).reshape(-1)
    run_len = tcnt[:, 0, :N_EXPERTS].astype(jnp.int32).reshape(-1)
    n_valid = (ends[-1] // te).astype(jnp.int32).reshape(1)
    tile_start = jnp.arange(n_tiles, dtype=jnp.int32) * te
    tile_expert = jnp.sum(ends[None, :] <= tile_start[:, None], axis=1).astype(jnp.int32)
    last_expert = tile_expert[jnp.maximum(n_valid[0] - 1, 0)]
    tile_expert = jnp.where(tile_start < ends[-1], tile_expert, last_expert)

    xs = _dispatch(run_dst, run_len, (offs + cnt).astype(jnp.int32), ends.astype(jnp.int32),
                   n_valid, h2, rt, n_rows)
    ys = _expert_mlp(tile_expert, n_valid, xs, w_gate[0], w_up[0], w_down[0])
    out = _combine(run_dst, run_len, x1, rinfo, mod, ffn_post_norm[0].reshape(1, d), ys, seq)
    return out.reshape(batch, seq, d)
```

```python
import functools

import jax
import jax.numpy as jnp
import numpy as np
from jax import lax
from jax.experimental import pallas as pl
from jax.experimental.pallas import tpu as pltpu

F32 = jnp.float32
BF16 = jnp.bfloat16

GRID_W = 64
HEAD_DIM = 64
KV_GROUP = 4
WINDOW = 128
ROPE_THETA = 10000.0
N_GROUPS = 4
EXPERTS_PER_GROUP = 8
N_EXPERTS = N_GROUPS * EXPERTS_PER_GROUP
EPS = 1e-6
NEG_INF = -1e30
LOG2_E = 1.4426950408889634
SAFE_SOFTMAX_SHIFT = 40.0

LANES = 128
V7X_VMEM_LIMIT = 56 * 1024 * 1024

PROJ_TM = 512
ATTN_A_TQ = 256
ATTN_B_TQ = 256
OUT_TM = 512
EXPERT_TE = 512
MOVE_TM = OUT_TM


def _params(*sem):
    return pltpu.CompilerParams(dimension_semantics=sem, vmem_limit_bytes=V7X_VMEM_LIMIT)


def _dot(a, b):
    return jnp.dot(a, b, preferred_element_type=F32)


def _dot_nt(a, b):
    return lax.dot_general(a, b, (((1,), (1,)), ((), ())), preferred_element_type=F32)


def _rms(x):
    return x * lax.rsqrt(jnp.mean(x * x, axis=-1, keepdims=True) + EPS)


def _split_bf16(x):
    hi = x.astype(BF16)
    lo = (x - hi.astype(F32)).astype(BF16)
    return hi, lo


def _mod_kernel(c_ref, w_ref, b_ref, o_ref):
    cc = c_ref[...]
    s = cc * jax.nn.sigmoid(cc)
    s_hi, s_lo = _split_bf16(s)
    w_hi, w_lo = _split_bf16(w_ref[...])
    o_ref[...] = _dot(s_hi, w_hi) + _dot(s_lo, w_hi) + _dot(s_hi, w_lo) + b_ref[...]


def _modulation(cc, w_mod, b_mod):
    rows, d = cc.shape
    n = w_mod.shape[1]
    bn = 1024
    return pl.pallas_call(
        _mod_kernel,
        grid=(n // bn,),
        in_specs=[pl.BlockSpec((rows, d), lambda i: (0, 0)),
                  pl.BlockSpec((d, bn), lambda i: (0, i)),
                  pl.BlockSpec((1, bn), lambda i: (0, i))],
        out_specs=pl.BlockSpec((rows, bn), lambda i: (0, i)),
        out_shape=jax.ShapeDtypeStruct((rows, n), F32),
        compiler_params=_params("arbitrary"),
    )(cc, w_mod, b_mod.reshape(1, n))


def _rope(x, cos, sin_a, sin_b):
    return x * cos + pltpu.roll(x, LANES - 1, 1) * sin_a + pltpu.roll(x, 1, 1) * sin_b


def _head_norm(x, seg_ref, gain):
    ss = _dot((x * x).astype(BF16), seg_ref[...])
    return x * lax.rsqrt(ss * (1.0 / HEAD_DIM) + EPS) * gain


def _max_head_sq_norm(x, seg_ref):
    ss = _dot((x * x).astype(BF16), seg_ref[...])
    return jnp.max(jnp.max(ss, axis=1, keepdims=True), axis=0, keepdims=True)


def _norm_stats(q_sq, k_sq):
    lane = lax.broadcasted_iota(jnp.int32, (1, LANES), 1)
    zero = jnp.zeros((1, LANES), F32)
    return jnp.where(lane == 0, q_sq, zero) + jnp.where(lane == 1, k_sq, zero)


def _rope_t(xt, cos_t, sin_a_t, sin_b_t):
    return (xt * cos_t + pltpu.roll(xt, LANES - 1, 0) * sin_a_t
            + pltpu.roll(xt, 1, 0) * sin_b_t)


def _proj_kernel(x_ref, sc_ref, sh_ref, gpre_ref, w_ref, cos_ref, sa_ref, sb_ref,
                 cos_t_ref, sa_t_ref, sb_t_ref, qn_ref, kn_ref, seg_q_ref, seg_k_ref,
                 qat_ref, ka_ref, vat_ref, qbt_ref, kb_ref, vbt_ref, stats_ref):
    h = _rms(x_ref[...]) * gpre_ref[...] * (1.0 + sc_ref[0, 0]) + sh_ref[0, 0]
    p = _dot(h.astype(BF16), w_ref[...])
    cos, sa, sb = cos_ref[...], sa_ref[...], sb_ref[...]
    q_scale = HEAD_DIM ** -0.5 * LOG2_E
    cos_t, sa_t, sb_t = cos_t_ref[...] * q_scale, sa_t_ref[...] * q_scale, sb_t_ref[...] * q_scale
    nq = qat_ref.shape[0]
    for c in range(nq // LANES):
        qat_ref[c * LANES:(c + 1) * LANES, :] = _rope_t(
            p[:, c * LANES:(c + 1) * LANES].T, cos_t, sa_t, sb_t).astype(BF16)
    o = nq
    ka_ref[...] = _rope(p[:, o:o + LANES], cos, sa, sb).astype(BF16)
    vat_ref[...] = p[:, o + LANES:o + 2 * LANES].T.astype(BF16)
    stats_ref[0] = _norm_stats(_max_head_sq_norm(p[:, 0:nq], seg_q_ref) * (q_scale * q_scale),
                               _max_head_sq_norm(p[:, o:o + LANES], seg_k_ref))
    o += 2 * LANES
    qb = _head_norm(p[:, o:o + nq], seg_q_ref, qn_ref[...])
    for c in range(nq // LANES):
        qbt_ref[c * LANES:(c + 1) * LANES, :] = _rope_t(
            qb[:, c * LANES:(c + 1) * LANES].T, cos_t, sa_t, sb_t).astype(BF16)
    o += nq
    kb = _head_norm(p[:, o:o + LANES], seg_k_ref, kn_ref[...])
    kb_ref[...] = _rope(kb, cos, sa, sb).astype(BF16)
    vbt_ref[...] = p[:, o + LANES:o + 2 * LANES].T.astype(BF16)


def _ctx_proj_kernel(x_ref, sc_ref, sh_ref, gpre_ref, wa_ref, wb_ref, kn_ref, seg_k_ref,
                     ka_ref, vat_ref, kb_ref, vbt_ref, stats_ref):
    h = (_rms(x_ref[...]) * gpre_ref[...] * (1.0 + sc_ref[0, 0]) + sh_ref[0, 0]).astype(BF16)
    pa = _dot(h, wa_ref[...])
    pb = _dot(h, wb_ref[...])
    stats_ref[0] = _norm_stats(0.0, _max_head_sq_norm(pa[:, 0:LANES], seg_k_ref))
    ka_ref[...] = pa[:, 0:LANES].astype(BF16)
    vat_ref[...] = pa[:, LANES:2 * LANES].T.astype(BF16)
    kb_ref[...] = _head_norm(pb[:, 0:LANES], seg_k_ref, kn_ref[...]).astype(BF16)
    vbt_ref[...] = pb[:, LANES:2 * LANES].T.astype(BF16)


def _mod_spec(chunk, row_of_step):
    return lambda d: pl.BlockSpec((1, 1, 1, d), lambda i, *_: (row_of_step(i), chunk, 0, 0))


MOD_SH1, MOD_SC1, MOD_G1, MOD_SH2, MOD_SC2, MOD_G2 = range(6)


def _project_latents(x2, mod, gpre, w_in, tables, qn, kn, seg_q, seg_k, seq):
    t, d = x2.shape
    tm = PROJ_TM
    tpb = seq // tm
    nq = seg_q.shape[0]
    const = lambda shape: pl.BlockSpec(shape, lambda i: (0,) * len(shape))
    batch_of = lambda i: i // tpb
    table = pl.BlockSpec((tm, LANES), lambda i: (i % tpb, 0))
    table_t = pl.BlockSpec((LANES, tm), lambda i: (0, i % tpb))
    k_spec = pl.BlockSpec((tm, LANES), lambda i: (i, 0))
    k_shape = jax.ShapeDtypeStruct((t, LANES), BF16)
    vt_spec = pl.BlockSpec((LANES, tm), lambda i: (0, i))
    vt_shape = jax.ShapeDtypeStruct((LANES, t), BF16)
    qt_spec = pl.BlockSpec((nq, tm), lambda i: (0, i))
    qt_shape = jax.ShapeDtypeStruct((nq, t), BF16)
    return pl.pallas_call(
        _proj_kernel,
        grid=(t // tm,),
        in_specs=[pl.BlockSpec((tm, d), lambda i: (i, 0)),
                  _mod_spec(MOD_SC1, batch_of)(d), _mod_spec(MOD_SH1, batch_of)(d), const((1, d)),
                  const(w_in.shape), table, table, table, table_t, table_t, table_t,
                  const((1, nq)), const((1, LANES)), const(seg_q.shape), const(seg_k.shape)],
        out_specs=[qt_spec, k_spec, vt_spec, qt_spec, k_spec, vt_spec,
                   pl.BlockSpec((1, 1, LANES), lambda i: (i, 0, 0))],
        out_shape=[qt_shape, k_shape, vt_shape, qt_shape, k_shape, vt_shape,
                   jax.ShapeDtypeStruct((t // tm, 1, LANES), F32)],
        compiler_params=_params("arbitrary"),
    )(x2, mod, mod, gpre, w_in, *tables, qn, kn, seg_q, seg_k)


def _project_context(c2, mod, ctx_row, gpre, w_in, kn, seg_k, ctx_len):
    t, d = c2.shape
    ctx_mod = lambda i: ctx_row
    nq = (w_in.shape[1] - 4 * LANES) // 2
    kv = 2 * LANES
    assert nq % kv == 0
    group_kv = lambda g: pl.BlockSpec((d, kv), lambda i: (0, (g * (nq + kv) + nq) // kv))
    const = lambda shape: pl.BlockSpec(shape, lambda i: (0,) * len(shape))
    k_spec = pl.BlockSpec((ctx_len, LANES), lambda i: (i, 0))
    k_shape = jax.ShapeDtypeStruct((t, LANES), BF16)
    vt_spec = pl.BlockSpec((LANES, ctx_len), lambda i: (0, i))
    vt_shape = jax.ShapeDtypeStruct((LANES, t), BF16)
    return pl.pallas_call(
        _ctx_proj_kernel,
        grid=(t // ctx_len,),
        in_specs=[pl.BlockSpec((ctx_len, d), lambda i: (i, 0)),
                  _mod_spec(MOD_SC1, ctx_mod)(d), _mod_spec(MOD_SH1, ctx_mod)(d),
                  const((1, d)), group_kv(0), group_kv(1), const((1, LANES)), const(seg_k.shape)],
        out_specs=[k_spec, vt_spec, k_spec, vt_spec,
                   pl.BlockSpec((1, 1, LANES), lambda i: (i, 0, 0))],
        out_shape=[k_shape, vt_shape, k_shape, vt_shape,
                   jax.ShapeDtypeStruct((t // ctx_len, 1, LANES), F32)],
        compiler_params=_params("arbitrary"),
    )(c2, mod, mod, gpre, w_in, w_in, kn, seg_k)


def _attend_t(w, k, kc, vt, vct, shift=None, bias=None, sink=None):
    st = _dot(k, w)
    sct = _dot(kc, w)
    if bias is not None:
        tq = bias.shape[1]
        st = jnp.concatenate([st[:, c * tq:(c + 1) * tq] + bias
                              for c in range(st.shape[1] // tq)], axis=1)
    if shift is None:
        shift = jnp.maximum(jnp.max(st, axis=0, keepdims=True),
                            jnp.max(sct, axis=0, keepdims=True))
        if sink is not None:
            shift = jnp.maximum(shift, sink)
    pt = jnp.exp2(st - shift)
    pct = jnp.exp2(sct - shift)
    denom = jnp.sum(pt, axis=0, keepdims=True) + jnp.sum(pct, axis=0, keepdims=True)
    if sink is not None:
        denom = denom + jnp.exp2(sink - shift)
    o2 = _dot(vt, pt.astype(BF16)) + _dot(vct, pct.astype(BF16))
    return o2, denom


def _all_heads_t(qt_ref, o_ref, attend):
    tq = qt_ref.shape[1]
    n_kv = LANES // HEAD_DIM
    zeros = jnp.zeros((HEAD_DIM, tq), BF16)
    outs = []
    for h in range(qt_ref.shape[0] // HEAD_DIM):
        g = h // KV_GROUP
        qh = qt_ref[h * HEAD_DIM:(h + 1) * HEAD_DIM, :]
        w = jnp.concatenate([zeros] * g + [qh] + [zeros] * (n_kv - 1 - g), axis=0)
        o2, denom = attend(h, w)
        outs.append(o2[g * HEAD_DIM:(g + 1) * HEAD_DIM, :] / denom)
    o_ref[...] = jnp.concatenate(outs, axis=0).T.astype(BF16)


def _all_heads_fused_t(qt_ref, o_ref, attend):
    tq = qt_ref.shape[1]
    n_heads = qt_ref.shape[0] // HEAD_DIM
    n_kv = LANES // HEAD_DIM
    rows = []
    for g in range(n_kv):
        heads = [qt_ref[h * HEAD_DIM:(h + 1) * HEAD_DIM, :] if h // KV_GROUP == g
                 else jnp.zeros((HEAD_DIM, tq), BF16) for h in range(n_heads)]
        rows.append(jnp.concatenate(heads, axis=1))
    w = jnp.concatenate(rows, axis=0)
    o2, denom = attend(0, w)
    o2 = o2 / denom
    outs = [o2[(h // KV_GROUP) * HEAD_DIM:(h // KV_GROUP + 1) * HEAD_DIM, h * tq:(h + 1) * tq]
            for h in range(n_heads)]
    o_ref[...] = jnp.concatenate(outs, axis=0).T.astype(BF16)


def _attn_b_kernel(bound_ref, qt_ref, k_ref, vt_ref, kc_ref, vct_ref, o_ref):
    k, kc, vt, vct = k_ref[...], kc_ref[...], vt_ref[...], vct_ref[...]
    bound = bound_ref[0]

    @pl.when(bound <= SAFE_SOFTMAX_SHIFT)
    def _():
        _all_heads_fused_t(qt_ref, o_ref, lambda h, w: _attend_t(w, k, kc, vt, vct, shift=bound))

    @pl.when(jnp.logical_not(bound <= SAFE_SOFTMAX_SHIFT))
    def _():
        _all_heads_t(qt_ref, o_ref, lambda h, w: _attend_t(w, k, kc, vt, vct))


def _attn_specs(nq, tq, seq, ctx_len):
    nqb = seq // tq
    return dict(
        qt=pl.BlockSpec((nq, tq), lambda b, i: (0, b * nqb + i)),
        k=pl.BlockSpec((seq, LANES), lambda b, i: (b, 0)),
        vt=pl.BlockSpec((LANES, seq), lambda b, i: (0, b)),
        kc=pl.BlockSpec((ctx_len, LANES), lambda b, i: (b, 0)),
        vct=pl.BlockSpec((LANES, ctx_len), lambda b, i: (0, b)),
        out=pl.BlockSpec((tq, nq), lambda b, i: (b * nqb + i, 0)))


def _attention_b(score_bound, qbt, kb, vbt, kcb, vcbt, batch, seq, ctx_len):
    nq, t = qbt.shape
    tq = ATTN_B_TQ
    sp = _attn_specs(nq, tq, seq, ctx_len)
    return pl.pallas_call(
        _attn_b_kernel,
        grid=(batch, seq // tq),
        in_specs=[pl.BlockSpec(memory_space=pltpu.SMEM),
                  sp["qt"], sp["k"], sp["vt"], sp["kc"], sp["vct"]],
        out_specs=sp["out"],
        out_shape=jax.ShapeDtypeStruct((t, nq), BF16),
        compiler_params=_params("arbitrary", "arbitrary"),
    )(score_bound, qbt, kb, vbt, kcb, vcbt)


def _attn_a_kernel(bound_ref, sink_ref, qt_ref, k_ref, vt_ref, kc_ref, vct_ref, bias_ref, o_ref,
                   *, seq):
    i = pl.program_id(1)
    tq = qt_ref.shape[1]
    n_heads = qt_ref.shape[0] // HEAD_DIM
    win = tq + 2 * WINDOW
    start = pl.multiple_of(jnp.clip(i * tq - WINDOW, 0, seq - win), WINDOW)
    k = k_ref[pl.ds(start, win), :]
    vt = vt_ref[:, pl.ds(start, win)]
    kc, vct = kc_ref[...], vct_ref[...]
    sinks = [sink_ref[h] * LOG2_E for h in range(n_heads)]

    bound = bound_ref[0]
    small = bound <= SAFE_SOFTMAX_SHIFT

    @pl.when(small)
    def _():
        sink_row = jnp.concatenate([jnp.full((1, tq), s, F32) for s in sinks], axis=1)
        _all_heads_fused_t(qt_ref, o_ref, lambda h, w: _attend_t(
            w, k, kc, vt, vct, shift=bound, bias=bias_ref[0], sink=sink_row))

    @pl.when(jnp.logical_not(small))
    def _():
        _all_heads_t(qt_ref, o_ref, lambda h, w: _attend_t(
            w, k, kc, vt, vct, bias=bias_ref[0], sink=sinks[h]))


def _band_bias(tq):
    win = tq + 2 * WINDOW
    r = np.arange(win)[:, None]
    j = np.arange(tq)[None, :]
    tables = [np.where(np.abs(off + r - j) <= WINDOW, 0.0, NEG_INF)
              for off in (0, -WINDOW, -2 * WINDOW)]
    return jnp.asarray(np.stack(tables), F32)


def _attention_a(score_bound, sink, qat, ka, vat, kca, vcat, batch, seq, ctx_len):
    nq, t = qat.shape
    tq = ATTN_A_TQ
    nqb = seq // tq
    win = tq + 2 * WINDOW
    assert nqb >= 2 and tq >= WINDOW
    sp = _attn_specs(nq, tq, seq, ctx_len)
    which = lambda b, i: (jnp.where(i == 0, 0, jnp.where(i == nqb - 1, 2, 1)), 0, 0)
    return pl.pallas_call(
        functools.partial(_attn_a_kernel, seq=seq),
        grid=(batch, nqb),
        in_specs=[pl.BlockSpec(memory_space=pltpu.SMEM), pl.BlockSpec(memory_space=pltpu.SMEM),
                  sp["qt"], sp["k"], sp["vt"], sp["kc"], sp["vct"],
                  pl.BlockSpec((1, win, tq), which)],
        out_specs=sp["out"],
        out_shape=jax.ShapeDtypeStruct((t, nq), BF16),
        compiler_params=_params("arbitrary", "arbitrary"),
    )(score_bound, sink, qat, ka, vat, kca, vcat, _band_bias(tq))


def _out_kernel(oa_ref, ob_ref, x_ref, g1_ref, sc2_ref, sh2_ref, ga_ref, gb_ref, gpost_ref,
                gpre2_ref, woa_ref, wob_ref, wrh_ref, wrl_ref, br_ref,
                x1_ref, h2_ref, rinfo_ref, rt_ref, tcarry_ref, tcnt_ref, cnt_ref, carry_ref):
    step = pl.program_id(0)

    @pl.when(step == 0)
    def _():
        carry_ref[...] = jnp.zeros_like(carry_ref)

    na = _rms(oa_ref[...].astype(F32)) * ga_ref[...]
    nb = _rms(ob_ref[...].astype(F32)) * gb_ref[...]
    ox = _dot(na.astype(BF16), woa_ref[...]) + _dot(nb.astype(BF16), wob_ref[...])
    x1 = x_ref[...] + g1_ref[0, 0] * (_rms(ox) * gpost_ref[...])
    x1_ref[...] = x1
    h2 = _rms(x1) * gpre2_ref[...] * (1.0 + sc2_ref[0, 0]) + sh2_ref[0, 0]
    h_hi, h_lo = _split_bf16(h2)
    h2_ref[...] = h_hi

    logits = (_dot(h_hi, wrh_ref[...]) + _dot(h_lo, wrh_ref[...]) + _dot(h_hi, wrl_ref[...])
              + br_ref[...])
    tm = logits.shape[0]
    lt = logits.T
    row = lax.broadcasted_iota(jnp.int32, lt.shape, 0)
    rowf = row.astype(F32)
    big = jnp.float32(1e9)
    ninf = jnp.float32(-jnp.inf)
    colmax = lambda v: jnp.max(v, axis=0, keepdims=True)
    colmin = lambda v: jnp.min(v, axis=0, keepdims=True)
    colsum = lambda v: jnp.sum(v, axis=0, keepdims=True)

    gmask = (row >= N_EXPERTS) & (row < N_EXPERTS + N_GROUPS)
    lg = jnp.where(gmask, lt, ninf)
    gmax = colmax(lg)
    gidx = colmin(jnp.where(lg == gmax, rowf, big)) - N_EXPERTS
    g_w = 1.0 / colsum(jnp.exp(lg - gmax))
    row_group = (row // EXPERTS_PER_GROUP).astype(F32)
    emask = (row < N_EXPERTS) & (row_group == gidx)
    le = jnp.where(emask, lt, ninf)
    m1 = colmax(le)
    i1 = colmin(jnp.where(le == m1, rowf, big))
    le2 = jnp.where(rowf == i1, ninf, le)
    m2 = colmax(le2)
    i2 = colmin(jnp.where(le2 == m2, rowf, big))
    e2 = jnp.exp(m2 - m1)
    w0 = g_w / (1.0 + e2)
    w1 = g_w * e2 / (1.0 + e2)

    hit1 = rowf == i1
    hit2 = rowf == i2
    onehot = jnp.where(hit1, 1.0, jnp.where(hit2, 1.0, 0.0)).astype(F32)
    r = lax.broadcasted_iota(jnp.int32, (tm, tm), 0)
    c = lax.broadcasted_iota(jnp.int32, (tm, tm), 1)
    earlier = jnp.where(r < c, 1.0, 0.0).astype(BF16)
    within = _dot(onehot.astype(BF16), earlier)
    tile_cnt = jnp.broadcast_to(jnp.sum(onehot, axis=1, keepdims=True), (LANES, LANES))
    er = lax.broadcasted_iota(jnp.int32, (LANES, LANES), 0)
    ec = lax.broadcasted_iota(jnp.int32, (LANES, LANES), 1)
    below = jnp.where(er > ec, 1.0, 0.0).astype(BF16)
    cnt_hi = jnp.floor(tile_cnt * (1.0 / 32.0))
    cnt_lo = tile_cnt - 32.0 * cnt_hi
    run_start = 32.0 * _dot(below, cnt_hi.astype(BF16)) + _dot(below, cnt_lo.astype(BF16))
    local = within + run_start[:, 0:1]
    pos0 = colsum(jnp.where(hit1, local, 0.0))
    pos1 = colsum(jnp.where(hit2, local, 0.0))
    cnt_row = tile_cnt.T[0:1, :]
    tcarry_ref[0] = carry_ref[...]
    tcnt_ref[0] = cnt_row
    carry_ref[...] += cnt_row
    cnt_ref[...] = carry_ref[...]

    fields = jnp.concatenate([i1, i2, pos0, pos1, w0, w1, jnp.zeros((2, tm), F32)], axis=0)
    rt_ref[...] = fields
    rinfo_ref[...] = jnp.concatenate(
        [fields, jnp.zeros((LANES - 8, tm), F32)], axis=0).T


def _out_and_route(oa, ob, x2, mod, ga, gb, gpost, gpre2, w_out, wrh, wrl, br, seq):
    t, d = x2.shape
    tm = OUT_TM
    tpb = seq // tm
    nq = oa.shape[1]
    const = lambda shape: pl.BlockSpec(shape, lambda i: (0,) * len(shape))
    batch_of = lambda i: i // tpb
    rows = lambda n: pl.BlockSpec((tm, n), lambda i: (i, 0))
    per_tile = pl.BlockSpec((1, 1, LANES), lambda i: (i, 0, 0))
    w_half = lambda g: pl.BlockSpec((nq, d), lambda i: (g, 0))
    return pl.pallas_call(
        _out_kernel,
        grid=(t // tm,),
        in_specs=[rows(nq), rows(nq), rows(d),
                  _mod_spec(MOD_G1, batch_of)(d), _mod_spec(MOD_SC2, batch_of)(d),
                  _mod_spec(MOD_SH2, batch_of)(d),
                  const((1, nq)), const((1, nq)), const((1, d)), const((1, d)),
                  w_half(0), w_half(1), const(wrh.shape), const(wrl.shape),
                  const((1, LANES))],
        out_specs=[rows(d), rows(d), rows(LANES), pl.BlockSpec((8, tm), lambda i: (0, i)),
                   per_tile, per_tile, const((1, LANES))],
        out_shape=[jax.ShapeDtypeStruct((t, d), F32), jax.ShapeDtypeStruct((t, d), BF16),
                   jax.ShapeDtypeStruct((t, LANES), F32), jax.ShapeDtypeStruct((8, t), F32),
                   jax.ShapeDtypeStruct((t // tm, 1, LANES), F32),
                   jax.ShapeDtypeStruct((t // tm, 1, LANES), F32),
                   jax.ShapeDtypeStruct((1, LANES), F32)],
        scratch_shapes=[pltpu.VMEM((1, LANES), F32)],
        compiler_params=_params("arbitrary"),
    )(oa, ob, x2, mod, mod, mod, ga, gb, gpost, gpre2, w_out, w_out, wrh, wrl, br)


PACK_ROWS = 8
ROW_DTYPE = F32


def _pack_rows(ref, x):
    n = x.shape[0]
    for c in range(PACK_ROWS):
        ref[pl.ds(c, n, stride=PACK_ROWS), :] = x[:, c * LANES:(c + 1) * LANES]


def _unpack_rows(ref):
    n = ref.shape[0] // PACK_ROWS
    return jnp.concatenate(
        [ref[pl.ds(c, n, stride=PACK_ROWS), :].astype(BF16) for c in range(PACK_ROWS)], axis=1)


def _for_each_run_piece(rdst_ref, rlen_ref, tile, max_len, fn):
    n_bits = max_len.bit_length()

    def run(e, local):
        length = rlen_ref[tile * N_EXPERTS + e]
        dst = rdst_ref[tile * N_EXPERTS + e]
        for b in range(n_bits):
            size = 1 << b

            @pl.when(((length >> b) & 1) == 1)
            def _():
                done = length & (size - 1)
                fn(local + done, dst + done, size)
        return local + length

    lax.fori_loop(0, N_EXPERTS, run, 0)


def _token_rows(ref, row0, n_rows):
    start = row0 * PACK_ROWS
    if not isinstance(start, int):
        start = pl.multiple_of(start, PACK_ROWS)
    return ref.at[pl.ds(start, n_rows * PACK_ROWS)]


def _dispatch_kernel(rdst_ref, rlen_ref, h_ref, rt_ref, xs_ref, sorted_ref, sem):
    k = pl.program_id(0)
    nk = pl.num_programs(0)
    tm = h_ref.shape[0]
    rows = 2 * tm
    slot = k % 2

    def wait_slot(s):
        pltpu.make_async_copy(sorted_ref.at[s], _token_rows(xs_ref, 0, rows), sem.at[s]).wait()

    @pl.when(k >= 2)
    def _():
        wait_slot(slot)

    pos0 = rt_ref[2:3, :]
    pos1 = rt_ref[3:4, :]
    r = lax.broadcasted_iota(jnp.int32, (rows, tm), 0).astype(F32)
    perm = jnp.where((r == pos0) | (r == pos1), 1.0, 0.0).astype(BF16)
    srt = _dot(perm, h_ref[...].astype(BF16))
    buf = sorted_ref.at[slot]
    _pack_rows(buf, srt)

    def copy_piece(local, dst, size):
        pltpu.make_async_copy(_token_rows(buf, local, size), _token_rows(xs_ref, dst, size),
                              sem.at[slot]).start()

    _for_each_run_piece(rdst_ref, rlen_ref, k, tm, copy_piece)

    @pl.when(k == nk - 1)
    def _():
        wait_slot(slot)

        @pl.when(nk >= 2)
        def _():
            wait_slot(1 - slot)


def _dispatch(run_dst, run_len, h2, rt):
    t, d = h2.shape
    assert d == PACK_ROWS * LANES
    tm = MOVE_TM
    return pl.pallas_call(
        _dispatch_kernel,
        grid_spec=pltpu.PrefetchScalarGridSpec(
            num_scalar_prefetch=2,
            grid=(t // tm,),
            in_specs=[pl.BlockSpec((tm, d), lambda i, *_: (i, 0)),
                      pl.BlockSpec((8, tm), lambda i, *_: (0, i))],
            out_specs=pl.BlockSpec(memory_space=pl.ANY),
            scratch_shapes=[pltpu.VMEM((2, 2 * tm * PACK_ROWS, LANES), ROW_DTYPE),
                            pltpu.SemaphoreType.DMA((2,))]),
        out_shape=jax.ShapeDtypeStruct((2 * t * PACK_ROWS, LANES), ROW_DTYPE),
        compiler_params=_params("arbitrary"),
    )(run_dst, run_len, h2, rt)


def _expert_kernel(vt_ref, ve_ref, va_ref, vb_ref, nv_ref, xs_ref, wg_ref, wu_ref, wd_ref, ys_ref,
                   wg_bf, wu_bf, wd_bf):
    v = pl.program_id(0)
    valid = v < nv_ref[0]
    prev = jnp.maximum(v - 1, 0)
    new_expert = (v == 0) | (ve_ref[v] != ve_ref[prev])
    new_tile = (v == 0) | (vt_ref[v] != vt_ref[prev])

    @pl.when(valid & new_expert)
    def _():
        wg_bf[...] = wg_ref[0].astype(BF16)
        wu_bf[...] = wu_ref[0].astype(BF16)
        wd_bf[...] = wd_ref[0].astype(BF16)

    def expert_rows():
        xb = _unpack_rows(xs_ref)
        gate = _dot(xb, wg_bf[...])
        up = _dot(xb, wu_bf[...])
        act = gate * jax.nn.sigmoid(gate) * up
        return _dot(act.astype(BF16), wd_bf[...])

    @pl.when(valid & new_tile)
    def _():
        _pack_rows(ys_ref, expert_rows())

    @pl.when(valid & jnp.logical_not(new_tile))
    def _():
        y = expert_rows()
        te = y.shape[0]
        row = lax.broadcasted_iota(jnp.int32, (te, 1), 0)
        mine = (row >= va_ref[v]) & (row < vb_ref[v])
        for c in range(PACK_ROWS):
            rows = pl.ds(c, te, stride=PACK_ROWS)
            ys_ref[rows, :] = jnp.where(mine, y[:, c * LANES:(c + 1) * LANES], ys_ref[rows, :])


def _expert_mlp(visit_tile, visit_expert, visit_lo, visit_hi, n_visits, xs, w_gate, w_up, w_down):
    te = EXPERT_TE
    d, ff = w_gate.shape[1:]
    blk = (te * PACK_ROWS, LANES)
    tile = lambda v, vt, *_: (vt[v], 0)
    wsel = lambda v, vt, ve, *_: (ve[v], 0, 0)
    return pl.pallas_call(
        _expert_kernel,
        grid_spec=pltpu.PrefetchScalarGridSpec(
            num_scalar_prefetch=5,
            grid=(visit_tile.shape[0],),
            in_specs=[pl.BlockSpec(blk, tile),
                      pl.BlockSpec((1, d, ff), wsel), pl.BlockSpec((1, d, ff), wsel),
                      pl.BlockSpec((1, ff, d), wsel)],
            out_specs=pl.BlockSpec(blk, tile),
            scratch_shapes=[pltpu.VMEM((d, ff), BF16), pltpu.VMEM((d, ff), BF16),
                            pltpu.VMEM((ff, d), BF16)]),
        out_shape=jax.ShapeDtypeStruct(xs.shape, ROW_DTYPE),
        compiler_params=_params("arbitrary"),
    )(visit_tile, visit_expert, visit_lo, visit_hi, n_visits, xs, w_gate, w_up, w_down)


def _combine_kernel(rdst_ref, rlen_ref, x1_ref, rinfo_ref, g2_ref, gpost_ref, ys_ref, o_ref,
                    gath_ref, sem):
    k = pl.program_id(0)
    nk = pl.num_programs(0)
    tm = x1_ref.shape[0]
    rows = 2 * tm
    slot = k % 2

    def gather_runs(tile, s):
        buf = gath_ref.at[s]

        def copy_piece(local, src, size):
            pltpu.make_async_copy(_token_rows(ys_ref, src, size), _token_rows(buf, local, size),
                                  sem.at[s]).start()

        _for_each_run_piece(rdst_ref, rlen_ref, tile, tm, copy_piece)

    @pl.when(k == 0)
    def _():
        gather_runs(0, 0)

    @pl.when(k + 1 < nk)
    def _():
        gather_runs(k + 1, 1 - slot)

    buf = gath_ref.at[slot]
    pltpu.make_async_copy(_token_rows(ys_ref, 0, rows), buf, sem.at[slot]).wait()
    g = _unpack_rows(buf)
    info = rinfo_ref[...]
    col = lax.broadcasted_iota(jnp.int32, (tm, rows), 1).astype(F32)
    pick0 = jnp.where(col == info[:, 2:3], 1.0, 0.0).astype(BF16)
    pick1 = jnp.where(col == info[:, 3:4], 1.0, 0.0).astype(BF16)
    fx = info[:, 4:5] * _dot(pick0, g) + info[:, 5:6] * _dot(pick1, g)
    o_ref[...] = x1_ref[...] + g2_ref[0, 0] * (_rms(fx) * gpost_ref[...])


def _combine(run_dst, run_len, x1, rinfo, mod, gpost, ys, seq):
    t, d = x1.shape
    tm = MOVE_TM
    tpb = seq // tm
    batch_of = lambda i: i // tpb
    return pl.pallas_call(
        _combine_kernel,
        grid_spec=pltpu.PrefetchScalarGridSpec(
            num_scalar_prefetch=2,
            grid=(t // tm,),
            in_specs=[pl.BlockSpec((tm, d), lambda i, *_: (i, 0)),
                      pl.BlockSpec((tm, LANES), lambda i, *_: (i, 0)),
                      _mod_spec(MOD_G2, batch_of)(d),
                      pl.BlockSpec((1, d), lambda i, *_: (0, 0)),
                      pl.BlockSpec(memory_space=pl.ANY)],
            out_specs=pl.BlockSpec((tm, d), lambda i, *_: (i, 0)),
            scratch_shapes=[pltpu.VMEM((2, 2 * tm * PACK_ROWS, LANES), ROW_DTYPE),
                            pltpu.SemaphoreType.DMA((2,))]),
        out_shape=jax.ShapeDtypeStruct((t, d), F32),
        compiler_params=_params("arbitrary"),
    )(run_dst, run_len, x1, rinfo, mod, gpost, ys)


def _rope_tables(seq):
    pos = np.arange(seq)
    row = (pos // GRID_W).astype(np.float32)
    col = (pos % GRID_W).astype(np.float32)
    axis_dim = HEAD_DIM // 2
    inv_freq = (ROPE_THETA ** (-np.arange(0, axis_dim, 2, dtype=np.float32) / axis_dim)).astype(
        np.float32)
    ang = np.concatenate([row[:, None] * inv_freq, col[:, None] * inv_freq], axis=-1)
    pair = (np.arange(LANES) % HEAD_DIM) // 2
    cos = np.cos(ang)[:, pair]
    sin = np.sin(ang)[:, pair]
    even = (np.arange(LANES) % 2) == 0
    tables = (cos, np.where(even, -sin, 0.0), np.where(even, 0.0, sin))
    tables = tables + tuple(tb.T for tb in tables)
    return tuple(jnp.asarray(tb, F32) for tb in tables)


def _segment_ones(n):
    seg = np.arange(n) // HEAD_DIM
    return jnp.asarray(seg[:, None] == seg[None, :], BF16)


def kernel(x, c, ctx, c_ctx, w_mod, b_mod, attn_pre_norm, attn_post_norm, w_in, a_sink,
           b_q_norm, b_k_norm, a_out_norm, b_out_norm, w_out, ffn_pre_norm, ffn_post_norm,
           w_group, b_group, w_router, b_router, w_gate, w_up, w_down):
    batch, seq, d = x.shape
    ctx_len = ctx.shape[1]
    assert w_mod.shape[0] == 1, "single-layer stack only (context stream is never updated)"
    assert seq % ATTN_A_TQ == 0 and seq >= ATTN_A_TQ + 2 * WINDOW
    assert seq % PROJ_TM == 0 and seq % ATTN_B_TQ == 0 and seq % OUT_TM == 0 and seq % MOVE_TM == 0
    t = batch * seq
    nq = d // 2
    nkv = nq // KV_GROUP
    assert nkv == LANES and w_in.shape[2] == 2 * nq + 4 * nkv

    cc = jnp.concatenate([c, c_ctx[None, :], jnp.zeros((16 - batch - 1, d), F32)], axis=0)
    mod = _modulation(cc, w_mod[0], b_mod[0]).reshape(cc.shape[0], 6, 1, d)

    x2 = x.reshape(t, d)
    c2 = ctx.reshape(batch * ctx_len, d)
    gpre = attn_pre_norm[0].reshape(1, d)
    w_in_bf = w_in[0].astype(BF16)
    qn = jnp.tile(b_q_norm[0], nq // HEAD_DIM).reshape(1, nq)
    kn = jnp.tile(b_k_norm[0], nkv // HEAD_DIM).reshape(1, nkv)
    seg_q, seg_k = _segment_ones(nq), _segment_ones(nkv)
    qat, ka, vat, qbt, kb, vbt, stats = _project_latents(
        x2, mod, gpre, w_in_bf, _rope_tables(seq), qn, kn, seg_q, seg_k, seq)
    kca, vcat, kcb, vcbt, ctx_stats = _project_context(
        c2, mod, batch, gpre, w_in_bf, kn, seg_k, ctx_len)

    q_sq = jnp.max(stats[:, 0, 0])
    k_sq = jnp.maximum(jnp.max(stats[:, 0, 1]), jnp.max(ctx_stats[:, 0, 1]))
    bound_a = jnp.maximum(1.01 * jnp.sqrt(q_sq * k_sq), jnp.max(a_sink[0]) * LOG2_E).reshape(1)
    oa = _attention_a(bound_a, a_sink[0], qat, ka, vat, kca, vcat, batch, seq, ctx_len)
    score_bound = (1.01 * HEAD_DIM ** 0.5 * LOG2_E
                   * jnp.max(jnp.abs(b_q_norm[0])) * jnp.max(jnp.abs(b_k_norm[0]))).reshape(1)
    ob = _attention_b(score_bound, qbt, kb, vbt, kcb, vcbt, batch, seq, ctx_len)

    w_out_bf = w_out[0].astype(BF16)
    lane_pad = LANES - N_EXPERTS - N_GROUPS
    w_r = jnp.pad(jnp.concatenate([w_router[0], w_group[0]], axis=1), ((0, 0), (0, lane_pad)))
    w_r_hi = w_r.astype(BF16)
    w_r_lo = (w_r - w_r_hi.astype(F32)).astype(BF16)
    b_r = jnp.pad(jnp.concatenate([b_router[0], b_group[0]]), (0, lane_pad)).reshape(1, LANES)
    x1, h2, rinfo, rt, tcarry, tcnt, counts = _out_and_route(
        oa, ob, x2, mod, a_out_norm[0].reshape(1, nq), b_out_norm[0].reshape(1, nq),
        attn_post_norm[0].reshape(1, d), ffn_pre_norm[0].reshape(1, d),
        w_out_bf, w_r_hi, w_r_lo, b_r, seq)

    te = EXPERT_TE
    assert (2 * t) % te == 0
    cnt = counts[0, :N_EXPERTS].astype(jnp.int32)
    ends = jnp.cumsum(cnt)
    starts = ends - cnt
    run_dst = (starts[None, :] + tcarry[:, 0, :N_EXPERTS].astype(jnp.int32)).reshape(-1)
    run_len = tcnt[:, 0, :N_EXPERTS].astype(jnp.int32).reshape(-1)
    first_tile = starts // te
    n_vis = jnp.where(cnt > 0, (ends - 1) // te - first_tile + 1, 0)
    vis_end = jnp.cumsum(n_vis)
    n_visits = vis_end[-1]
    v = jnp.minimum(jnp.arange(2 * t // te + N_EXPERTS, dtype=jnp.int32), n_visits - 1)
    v_expert = jnp.sum(vis_end[None, :] <= v[:, None], axis=1).astype(jnp.int32)
    pick = (v_expert[:, None] == jnp.arange(N_EXPERTS)[None, :]).astype(jnp.int32)
    of_expert = lambda table: jnp.sum(pick * table[None, :], axis=1)
    v_tile = of_expert(first_tile) + v - of_expert(vis_end - n_vis)
    v_lo = jnp.maximum(of_expert(starts) - v_tile * te, 0)
    v_hi = jnp.minimum(of_expert(ends) - v_tile * te, te)

    xs = _dispatch(run_dst, run_len, h2, rt)
    ys = _expert_mlp(v_tile, v_expert, v_lo, v_hi, n_visits.reshape(1), xs,
                     w_gate[0], w_up[0], w_down[0])
    out = _combine(run_dst, run_len, x1, rinfo, mod, ffn_post_norm[0].reshape(1, d), ys, seq)
    return out.reshape(batch, seq, d)
```

```python
import functools

import jax
import jax.numpy as jnp
import numpy as np
from jax import lax
from jax.experimental import pallas as pl
from jax.experimental.pallas import tpu as pltpu

F32 = jnp.float32
BF16 = jnp.bfloat16

GRID_W = 64
HEAD_DIM = 64
KV_GROUP = 4
WINDOW = 128
ROPE_THETA = 10000.0
N_GROUPS = 4
EXPERTS_PER_GROUP = 8
N_EXPERTS = N_GROUPS * EXPERTS_PER_GROUP
EPS = 1e-6
NEG_INF = -1e30
LOG2_E = 1.4426950408889634
SAFE_SOFTMAX_SHIFT = 40.0

LANES = 128
V7X_VMEM_LIMIT = 56 * 1024 * 1024

PROJ_TM = 512
ATTN_A_TQ = 256
ATTN_B_TQ = 256
OUT_TM = 512
EXPERT_TE = 256
MOVE_TM = OUT_TM


def _params(*sem):
    return pltpu.CompilerParams(dimension_semantics=sem, vmem_limit_bytes=V7X_VMEM_LIMIT)


def _dot(a, b):
    return jnp.dot(a, b, preferred_element_type=F32)


def _dot_nt(a, b):
    return lax.dot_general(a, b, (((1,), (1,)), ((), ())), preferred_element_type=F32)


def _rms(x):
    return x * lax.rsqrt(jnp.mean(x * x, axis=-1, keepdims=True) + EPS)


def _split_bf16(x):
    hi = x.astype(BF16)
    lo = (x - hi.astype(F32)).astype(BF16)
    return hi, lo


def _mod_kernel(c_ref, w_ref, b_ref, o_ref):
    cc = c_ref[...]
    s = cc * jax.nn.sigmoid(cc)
    s_hi, s_lo = _split_bf16(s)
    w_hi, w_lo = _split_bf16(w_ref[...])
    o_ref[...] = _dot(s_hi, w_hi) + _dot(s_lo, w_hi) + _dot(s_hi, w_lo) + b_ref[...]


def _modulation(cc, w_mod, b_mod):
    rows, d = cc.shape
    n = w_mod.shape[1]
    bn = 1024
    return pl.pallas_call(
        _mod_kernel,
        grid=(n // bn,),
        in_specs=[pl.BlockSpec((rows, d), lambda i: (0, 0)),
                  pl.BlockSpec((d, bn), lambda i: (0, i)),
                  pl.BlockSpec((1, bn), lambda i: (0, i))],
        out_specs=pl.BlockSpec((rows, bn), lambda i: (0, i)),
        out_shape=jax.ShapeDtypeStruct((rows, n), F32),
        compiler_params=_params("arbitrary"),
    )(cc, w_mod, b_mod.reshape(1, n))


def _rope(x, cos, sin_a, sin_b):
    return x * cos + pltpu.roll(x, LANES - 1, 1) * sin_a + pltpu.roll(x, 1, 1) * sin_b


def _head_norm(x, seg_ref, gain):
    ss = _dot((x * x).astype(BF16), seg_ref[...])
    return x * lax.rsqrt(ss * (1.0 / HEAD_DIM) + EPS) * gain


def _max_head_sq_norm(x, seg_ref):
    ss = _dot((x * x).astype(BF16), seg_ref[...])
    return jnp.max(jnp.max(ss, axis=1, keepdims=True), axis=0, keepdims=True)


def _norm_stats(q_sq, k_sq):
    lane = lax.broadcasted_iota(jnp.int32, (1, LANES), 1)
    zero = jnp.zeros((1, LANES), F32)
    return jnp.where(lane == 0, q_sq, zero) + jnp.where(lane == 1, k_sq, zero)


def _rope_t(xt, cos_t, sin_a_t, sin_b_t):
    return (xt * cos_t + pltpu.roll(xt, LANES - 1, 0) * sin_a_t
            + pltpu.roll(xt, 1, 0) * sin_b_t)


def _proj_kernel(x_ref, sc_ref, sh_ref, gpre_ref, w_ref, cos_ref, sa_ref, sb_ref,
                 cos_t_ref, sa_t_ref, sb_t_ref, qn_ref, kn_ref, seg_q_ref, seg_k_ref,
                 qat_ref, ka_ref, vat_ref, qbt_ref, kb_ref, vbt_ref, stats_ref):
    h = _rms(x_ref[...]) * gpre_ref[...] * (1.0 + sc_ref[0, 0]) + sh_ref[0, 0]
    p = _dot(h.astype(BF16), w_ref[...])
    cos, sa, sb = cos_ref[...], sa_ref[...], sb_ref[...]
    q_scale = HEAD_DIM ** -0.5 * LOG2_E
    cos_t, sa_t, sb_t = cos_t_ref[...] * q_scale, sa_t_ref[...] * q_scale, sb_t_ref[...] * q_scale
    nq = qat_ref.shape[0]
    for c in range(nq // LANES):
        qat_ref[c * LANES:(c + 1) * LANES, :] = _rope_t(
            p[:, c * LANES:(c + 1) * LANES].T, cos_t, sa_t, sb_t).astype(BF16)
    o = nq
    ka_ref[...] = _rope(p[:, o:o + LANES], cos, sa, sb).astype(BF16)
    vat_ref[...] = p[:, o + LANES:o + 2 * LANES].T.astype(BF16)
    stats_ref[0] = _norm_stats(_max_head_sq_norm(p[:, 0:nq], seg_q_ref) * (q_scale * q_scale),
                               _max_head_sq_norm(p[:, o:o + LANES], seg_k_ref))
    o += 2 * LANES
    qb = _head_norm(p[:, o:o + nq], seg_q_ref, qn_ref[...])
    for c in range(nq // LANES):
        qbt_ref[c * LANES:(c + 1) * LANES, :] = _rope_t(
            qb[:, c * LANES:(c + 1) * LANES].T, cos_t, sa_t, sb_t).astype(BF16)
    o += nq
    kb = _head_norm(p[:, o:o + LANES], seg_k_ref, kn_ref[...])
    kb_ref[...] = _rope(kb, cos, sa, sb).astype(BF16)
    vbt_ref[...] = p[:, o + LANES:o + 2 * LANES].T.astype(BF16)


def _ctx_proj_kernel(x_ref, sc_ref, sh_ref, gpre_ref, wa_ref, wb_ref, kn_ref, seg_k_ref,
                     ka_ref, vat_ref, kb_ref, vbt_ref, stats_ref):
    h = (_rms(x_ref[...]) * gpre_ref[...] * (1.0 + sc_ref[0, 0]) + sh_ref[0, 0]).astype(BF16)
    pa = _dot(h, wa_ref[...])
    pb = _dot(h, wb_ref[...])
    stats_ref[0] = _norm_stats(0.0, _max_head_sq_norm(pa[:, 0:LANES], seg_k_ref))
    ka_ref[...] = pa[:, 0:LANES].astype(BF16)
    vat_ref[...] = pa[:, LANES:2 * LANES].T.astype(BF16)
    kb_ref[...] = _head_norm(pb[:, 0:LANES], seg_k_ref, kn_ref[...]).astype(BF16)
    vbt_ref[...] = pb[:, LANES:2 * LANES].T.astype(BF16)


def _mod_spec(chunk, row_of_step):
    return lambda d: pl.BlockSpec((1, 1, 1, d), lambda i, *_: (row_of_step(i), chunk, 0, 0))


MOD_SH1, MOD_SC1, MOD_G1, MOD_SH2, MOD_SC2, MOD_G2 = range(6)


def _project_latents(x2, mod, gpre, w_in, tables, qn, kn, seg_q, seg_k, seq):
    t, d = x2.shape
    tm = PROJ_TM
    tpb = seq // tm
    nq = seg_q.shape[0]
    const = lambda shape: pl.BlockSpec(shape, lambda i: (0,) * len(shape))
    batch_of = lambda i: i // tpb
    table = pl.BlockSpec((tm, LANES), lambda i: (i % tpb, 0))
    table_t = pl.BlockSpec((LANES, tm), lambda i: (0, i % tpb))
    k_spec = pl.BlockSpec((tm, LANES), lambda i: (i, 0))
    k_shape = jax.ShapeDtypeStruct((t, LANES), BF16)
    vt_spec = pl.BlockSpec((LANES, tm), lambda i: (0, i))
    vt_shape = jax.ShapeDtypeStruct((LANES, t), BF16)
    qt_spec = pl.BlockSpec((nq, tm), lambda i: (0, i))
    qt_shape = jax.ShapeDtypeStruct((nq, t), BF16)
    return pl.pallas_call(
        _proj_kernel,
        grid=(t // tm,),
        in_specs=[pl.BlockSpec((tm, d), lambda i: (i, 0)),
                  _mod_spec(MOD_SC1, batch_of)(d), _mod_spec(MOD_SH1, batch_of)(d), const((1, d)),
                  const(w_in.shape), table, table, table, table_t, table_t, table_t,
                  const((1, nq)), const((1, LANES)), const(seg_q.shape), const(seg_k.shape)],
        out_specs=[qt_spec, k_spec, vt_spec, qt_spec, k_spec, vt_spec,
                   pl.BlockSpec((1, 1, LANES), lambda i: (i, 0, 0))],
        out_shape=[qt_shape, k_shape, vt_shape, qt_shape, k_shape, vt_shape,
                   jax.ShapeDtypeStruct((t // tm, 1, LANES), F32)],
        compiler_params=_params("arbitrary"),
    )(x2, mod, mod, gpre, w_in, *tables, qn, kn, seg_q, seg_k)


def _project_context(c2, mod, ctx_row, gpre, w_in, kn, seg_k, ctx_len):
    t, d = c2.shape
    ctx_mod = lambda i: ctx_row
    nq = (w_in.shape[1] - 4 * LANES) // 2
    kv = 2 * LANES
    assert nq % kv == 0
    group_kv = lambda g: pl.BlockSpec((d, kv), lambda i: (0, (g * (nq + kv) + nq) // kv))
    const = lambda shape: pl.BlockSpec(shape, lambda i: (0,) * len(shape))
    k_spec = pl.BlockSpec((ctx_len, LANES), lambda i: (i, 0))
    k_shape = jax.ShapeDtypeStruct((t, LANES), BF16)
    vt_spec = pl.BlockSpec((LANES, ctx_len), lambda i: (0, i))
    vt_shape = jax.ShapeDtypeStruct((LANES, t), BF16)
    return pl.pallas_call(
        _ctx_proj_kernel,
        grid=(t // ctx_len,),
        in_specs=[pl.BlockSpec((ctx_len, d), lambda i: (i, 0)),
                  _mod_spec(MOD_SC1, ctx_mod)(d), _mod_spec(MOD_SH1, ctx_mod)(d),
                  const((1, d)), group_kv(0), group_kv(1), const((1, LANES)), const(seg_k.shape)],
        out_specs=[k_spec, vt_spec, k_spec, vt_spec,
                   pl.BlockSpec((1, 1, LANES), lambda i: (i, 0, 0))],
        out_shape=[k_shape, vt_shape, k_shape, vt_shape,
                   jax.ShapeDtypeStruct((t // ctx_len, 1, LANES), F32)],
        compiler_params=_params("arbitrary"),
    )(c2, mod, mod, gpre, w_in, w_in, kn, seg_k)


def _attend_t(w, k, kc, vt, vct, shift=None, bias=None, sink=None):
    st = _dot(k, w)
    sct = _dot(kc, w)
    if bias is not None:
        tq = bias.shape[1]
        st = jnp.concatenate([st[:, c * tq:(c + 1) * tq] + bias
                              for c in range(st.shape[1] // tq)], axis=1)
    if shift is None:
        shift = jnp.maximum(jnp.max(st, axis=0, keepdims=True),
                            jnp.max(sct, axis=0, keepdims=True))
        if sink is not None:
            shift = jnp.maximum(shift, sink)
    pt = jnp.exp2(st - shift)
    pct = jnp.exp2(sct - shift)
    denom = jnp.sum(pt, axis=0, keepdims=True) + jnp.sum(pct, axis=0, keepdims=True)
    if sink is not None:
        denom = denom + jnp.exp2(sink - shift)
    o2 = _dot(vt, pt.astype(BF16)) + _dot(vct, pct.astype(BF16))
    return o2, denom


def _all_heads_t(qt_ref, o_ref, attend):
    tq = qt_ref.shape[1]
    n_kv = LANES // HEAD_DIM
    zeros = jnp.zeros((HEAD_DIM, tq), BF16)
    outs = []
    for h in range(qt_ref.shape[0] // HEAD_DIM):
        g = h // KV_GROUP
        qh = qt_ref[h * HEAD_DIM:(h + 1) * HEAD_DIM, :]
        w = jnp.concatenate([zeros] * g + [qh] + [zeros] * (n_kv - 1 - g), axis=0)
        o2, denom = attend(h, w)
        outs.append(o2[g * HEAD_DIM:(g + 1) * HEAD_DIM, :] / denom)
    o_ref[...] = jnp.concatenate(outs, axis=0).T.astype(BF16)


def _all_heads_fused_t(qt_ref, o_ref, attend):
    tq = qt_ref.shape[1]
    n_heads = qt_ref.shape[0] // HEAD_DIM
    n_kv = LANES // HEAD_DIM
    rows = []
    for g in range(n_kv):
        heads = [qt_ref[h * HEAD_DIM:(h + 1) * HEAD_DIM, :] if h // KV_GROUP == g
                 else jnp.zeros((HEAD_DIM, tq), BF16) for h in range(n_heads)]
        rows.append(jnp.concatenate(heads, axis=1))
    w = jnp.concatenate(rows, axis=0)
    o2, denom = attend(0, w)
    o2 = o2 / denom
    outs = [o2[(h // KV_GROUP) * HEAD_DIM:(h // KV_GROUP + 1) * HEAD_DIM, h * tq:(h + 1) * tq]
            for h in range(n_heads)]
    o_ref[...] = jnp.concatenate(outs, axis=0).T.astype(BF16)


def _attn_b_kernel(bound_ref, qt_ref, k_ref, vt_ref, kc_ref, vct_ref, o_ref):
    k, kc, vt, vct = k_ref[...], kc_ref[...], vt_ref[...], vct_ref[...]
    bound = bound_ref[0]

    @pl.when(bound <= SAFE_SOFTMAX_SHIFT)
    def _():
        _all_heads_fused_t(qt_ref, o_ref, lambda h, w: _attend_t(w, k, kc, vt, vct, shift=bound))

    @pl.when(jnp.logical_not(bound <= SAFE_SOFTMAX_SHIFT))
    def _():
        _all_heads_t(qt_ref, o_ref, lambda h, w: _attend_t(w, k, kc, vt, vct))


def _attn_specs(nq, tq, seq, ctx_len):
    nqb = seq // tq
    return dict(
        qt=pl.BlockSpec((nq, tq), lambda b, i: (0, b * nqb + i)),
        k=pl.BlockSpec((seq, LANES), lambda b, i: (b, 0)),
        vt=pl.BlockSpec((LANES, seq), lambda b, i: (0, b)),
        kc=pl.BlockSpec((ctx_len, LANES), lambda b, i: (b, 0)),
        vct=pl.BlockSpec((LANES, ctx_len), lambda b, i: (0, b)),
        out=pl.BlockSpec((tq, nq), lambda b, i: (b * nqb + i, 0)))


def _attention_b(score_bound, qbt, kb, vbt, kcb, vcbt, batch, seq, ctx_len):
    nq, t = qbt.shape
    tq = ATTN_B_TQ
    sp = _attn_specs(nq, tq, seq, ctx_len)
    return pl.pallas_call(
        _attn_b_kernel,
        grid=(batch, seq // tq),
        in_specs=[pl.BlockSpec(memory_space=pltpu.SMEM),
                  sp["qt"], sp["k"], sp["vt"], sp["kc"], sp["vct"]],
        out_specs=sp["out"],
        out_shape=jax.ShapeDtypeStruct((t, nq), BF16),
        compiler_params=_params("arbitrary", "arbitrary"),
    )(score_bound, qbt, kb, vbt, kcb, vcbt)


def _attn_a_kernel(bound_ref, sink_ref, qt_ref, k_ref, vt_ref, kc_ref, vct_ref, bias_ref, o_ref,
                   *, seq):
    i = pl.program_id(1)
    tq = qt_ref.shape[1]
    n_heads = qt_ref.shape[0] // HEAD_DIM
    win = tq + 2 * WINDOW
    start = pl.multiple_of(jnp.clip(i * tq - WINDOW, 0, seq - win), WINDOW)
    k = k_ref[pl.ds(start, win), :]
    vt = vt_ref[:, pl.ds(start, win)]
    kc, vct = kc_ref[...], vct_ref[...]
    sinks = [sink_ref[h] * LOG2_E for h in range(n_heads)]

    bound = bound_ref[0]
    small = bound <= SAFE_SOFTMAX_SHIFT

    @pl.when(small)
    def _():
        sink_row = jnp.concatenate([jnp.full((1, tq), s, F32) for s in sinks], axis=1)
        _all_heads_fused_t(qt_ref, o_ref, lambda h, w: _attend_t(
            w, k, kc, vt, vct, shift=bound, bias=bias_ref[0], sink=sink_row))

    @pl.when(jnp.logical_not(small))
    def _():
        _all_heads_t(qt_ref, o_ref, lambda h, w: _attend_t(
            w, k, kc, vt, vct, bias=bias_ref[0], sink=sinks[h]))


def _band_bias(tq):
    win = tq + 2 * WINDOW
    r = np.arange(win)[:, None]
    j = np.arange(tq)[None, :]
    tables = [np.where(np.abs(off + r - j) <= WINDOW, 0.0, NEG_INF)
              for off in (0, -WINDOW, -2 * WINDOW)]
    return jnp.asarray(np.stack(tables), F32)


def _attention_a(score_bound, sink, qat, ka, vat, kca, vcat, batch, seq, ctx_len):
    nq, t = qat.shape
    tq = ATTN_A_TQ
    nqb = seq // tq
    win = tq + 2 * WINDOW
    assert nqb >= 2 and tq >= WINDOW
    sp = _attn_specs(nq, tq, seq, ctx_len)
    which = lambda b, i: (jnp.where(i == 0, 0, jnp.where(i == nqb - 1, 2, 1)), 0, 0)
    return pl.pallas_call(
        functools.partial(_attn_a_kernel, seq=seq),
        grid=(batch, nqb),
        in_specs=[pl.BlockSpec(memory_space=pltpu.SMEM), pl.BlockSpec(memory_space=pltpu.SMEM),
                  sp["qt"], sp["k"], sp["vt"], sp["kc"], sp["vct"],
                  pl.BlockSpec((1, win, tq), which)],
        out_specs=sp["out"],
        out_shape=jax.ShapeDtypeStruct((t, nq), BF16),
        compiler_params=_params("arbitrary", "arbitrary"),
    )(score_bound, sink, qat, ka, vat, kca, vcat, _band_bias(tq))


def _out_kernel(oa_ref, ob_ref, x_ref, g1_ref, sc2_ref, sh2_ref, ga_ref, gb_ref, gpost_ref,
                gpre2_ref, woa_ref, wob_ref, wrh_ref, wrl_ref, br_ref,
                x1_ref, h2_ref, rinfo_ref, rt_ref, tcarry_ref, tcnt_ref, cnt_ref, carry_ref):
    step = pl.program_id(0)

    @pl.when(step == 0)
    def _():
        carry_ref[...] = jnp.zeros_like(carry_ref)

    na = _rms(oa_ref[...].astype(F32)) * ga_ref[...]
    nb = _rms(ob_ref[...].astype(F32)) * gb_ref[...]
    ox = _dot(na.astype(BF16), woa_ref[...]) + _dot(nb.astype(BF16), wob_ref[...])
    x1 = x_ref[...] + g1_ref[0, 0] * (_rms(ox) * gpost_ref[...])
    x1_ref[...] = x1
    h2 = _rms(x1) * gpre2_ref[...] * (1.0 + sc2_ref[0, 0]) + sh2_ref[0, 0]
    h_hi, h_lo = _split_bf16(h2)
    h2_ref[...] = h_hi

    logits = (_dot(h_hi, wrh_ref[...]) + _dot(h_lo, wrh_ref[...]) + _dot(h_hi, wrl_ref[...])
              + br_ref[...])
    tm = logits.shape[0]
    lt = logits.T
    row = lax.broadcasted_iota(jnp.int32, lt.shape, 0)
    rowf = row.astype(F32)
    big = jnp.float32(1e9)
    ninf = jnp.float32(-jnp.inf)
    colmax = lambda v: jnp.max(v, axis=0, keepdims=True)
    colmin = lambda v: jnp.min(v, axis=0, keepdims=True)
    colsum = lambda v: jnp.sum(v, axis=0, keepdims=True)

    gmask = (row >= N_EXPERTS) & (row < N_EXPERTS + N_GROUPS)
    lg = jnp.where(gmask, lt, ninf)
    gmax = colmax(lg)
    gidx = colmin(jnp.where(lg == gmax, rowf, big)) - N_EXPERTS
    g_w = 1.0 / colsum(jnp.exp(lg - gmax))
    row_group = (row // EXPERTS_PER_GROUP).astype(F32)
    emask = (row < N_EXPERTS) & (row_group == gidx)
    le = jnp.where(emask, lt, ninf)
    m1 = colmax(le)
    i1 = colmin(jnp.where(le == m1, rowf, big))
    le2 = jnp.where(rowf == i1, ninf, le)
    m2 = colmax(le2)
    i2 = colmin(jnp.where(le2 == m2, rowf, big))
    e2 = jnp.exp(m2 - m1)
    w0 = g_w / (1.0 + e2)
    w1 = g_w * e2 / (1.0 + e2)

    hit1 = rowf == i1
    hit2 = rowf == i2
    onehot = jnp.where(hit1, 1.0, jnp.where(hit2, 1.0, 0.0)).astype(F32)
    r = lax.broadcasted_iota(jnp.int32, (tm, tm), 0)
    c = lax.broadcasted_iota(jnp.int32, (tm, tm), 1)
    earlier = jnp.where(r < c, 1.0, 0.0).astype(BF16)
    within = _dot(onehot.astype(BF16), earlier)
    tile_cnt = jnp.broadcast_to(jnp.sum(onehot, axis=1, keepdims=True), (LANES, LANES))
    er = lax.broadcasted_iota(jnp.int32, (LANES, LANES), 0)
    ec = lax.broadcasted_iota(jnp.int32, (LANES, LANES), 1)
    below = jnp.where(er > ec, 1.0, 0.0).astype(BF16)
    cnt_hi = jnp.floor(tile_cnt * (1.0 / 32.0))
    cnt_lo = tile_cnt - 32.0 * cnt_hi
    run_start = 32.0 * _dot(below, cnt_hi.astype(BF16)) + _dot(below, cnt_lo.astype(BF16))
    local = within + run_start[:, 0:1]
    pos0 = colsum(jnp.where(hit1, local, 0.0))
    pos1 = colsum(jnp.where(hit2, local, 0.0))
    cnt_row = tile_cnt.T[0:1, :]
    tcarry_ref[0] = carry_ref[...]
    tcnt_ref[0] = cnt_row
    carry_ref[...] += cnt_row
    cnt_ref[...] = carry_ref[...]

    fields = jnp.concatenate([i1, i2, pos0, pos1, w0, w1, jnp.zeros((2, tm), F32)], axis=0)
    rt_ref[...] = fields
    rinfo_ref[...] = jnp.concatenate(
        [fields, jnp.zeros((LANES - 8, tm), F32)], axis=0).T


def _out_and_route(oa, ob, x2, mod, ga, gb, gpost, gpre2, w_out, wrh, wrl, br, seq):
    t, d = x2.shape
    tm = OUT_TM
    tpb = seq // tm
    nq = oa.shape[1]
    const = lambda shape: pl.BlockSpec(shape, lambda i: (0,) * len(shape))
    batch_of = lambda i: i // tpb
    rows = lambda n: pl.BlockSpec((tm, n), lambda i: (i, 0))
    per_tile = pl.BlockSpec((1, 1, LANES), lambda i: (i, 0, 0))
    w_half = lambda g: pl.BlockSpec((nq, d), lambda i: (g, 0))
    return pl.pallas_call(
        _out_kernel,
        grid=(t // tm,),
        in_specs=[rows(nq), rows(nq), rows(d),
                  _mod_spec(MOD_G1, batch_of)(d), _mod_spec(MOD_SC2, batch_of)(d),
                  _mod_spec(MOD_SH2, batch_of)(d),
                  const((1, nq)), const((1, nq)), const((1, d)), const((1, d)),
                  w_half(0), w_half(1), const(wrh.shape), const(wrl.shape),
                  const((1, LANES))],
        out_specs=[rows(d), rows(d), rows(LANES), pl.BlockSpec((8, tm), lambda i: (0, i)),
                   per_tile, per_tile, const((1, LANES))],
        out_shape=[jax.ShapeDtypeStruct((t, d), F32), jax.ShapeDtypeStruct((t, d), BF16),
                   jax.ShapeDtypeStruct((t, LANES), F32), jax.ShapeDtypeStruct((8, t), F32),
                   jax.ShapeDtypeStruct((t // tm, 1, LANES), F32),
                   jax.ShapeDtypeStruct((t // tm, 1, LANES), F32),
                   jax.ShapeDtypeStruct((1, LANES), F32)],
        scratch_shapes=[pltpu.VMEM((1, LANES), F32)],
        compiler_params=_params("arbitrary"),
    )(oa, ob, x2, mod, mod, mod, ga, gb, gpost, gpre2, w_out, w_out, wrh, wrl, br)


PACK_ROWS = 8
ROW_DTYPE = F32


def _pack_rows(ref, x):
    n = x.shape[0]
    for c in range(PACK_ROWS):
        ref[pl.ds(c, n, stride=PACK_ROWS), :] = x[:, c * LANES:(c + 1) * LANES]


def _unpack_rows(ref):
    n = ref.shape[0] // PACK_ROWS
    return jnp.concatenate(
        [ref[pl.ds(c, n, stride=PACK_ROWS), :].astype(BF16) for c in range(PACK_ROWS)], axis=1)


def _for_each_run_piece(rdst_ref, rlen_ref, tile, max_len, fn):
    n_bits = max_len.bit_length()

    def run(e, local):
        length = rlen_ref[tile * N_EXPERTS + e]
        dst = rdst_ref[tile * N_EXPERTS + e]
        for b in range(n_bits):
            size = 1 << b

            @pl.when(((length >> b) & 1) == 1)
            def _():
                done = length & (size - 1)
                fn(local + done, dst + done, size)
        return local + length

    lax.fori_loop(0, N_EXPERTS, run, 0)


def _token_rows(ref, row0, n_rows):
    start = row0 * PACK_ROWS
    if not isinstance(start, int):
        start = pl.multiple_of(start, PACK_ROWS)
    return ref.at[pl.ds(start, n_rows * PACK_ROWS)]


def _dispatch_kernel(rdst_ref, rlen_ref, h_ref, rt_ref, xs_ref, sorted_ref, sem):
    k = pl.program_id(0)
    nk = pl.num_programs(0)
    tm = h_ref.shape[0]
    rows = 2 * tm
    slot = k % 2

    def wait_slot(s):
        pltpu.make_async_copy(sorted_ref.at[s], _token_rows(xs_ref, 0, rows), sem.at[s]).wait()

    @pl.when(k >= 2)
    def _():
        wait_slot(slot)

    pos0 = rt_ref[2:3, :]
    pos1 = rt_ref[3:4, :]
    r = lax.broadcasted_iota(jnp.int32, (rows, tm), 0).astype(F32)
    perm = jnp.where((r == pos0) | (r == pos1), 1.0, 0.0).astype(BF16)
    srt = _dot(perm, h_ref[...].astype(BF16))
    buf = sorted_ref.at[slot]
    _pack_rows(buf, srt)

    def copy_piece(local, dst, size):
        pltpu.make_async_copy(_token_rows(buf, local, size), _token_rows(xs_ref, dst, size),
                              sem.at[slot]).start()

    _for_each_run_piece(rdst_ref, rlen_ref, k, tm, copy_piece)

    @pl.when(k == nk - 1)
    def _():
        wait_slot(slot)

        @pl.when(nk >= 2)
        def _():
            wait_slot(1 - slot)


def _dispatch(run_dst, run_len, h2, rt):
    t, d = h2.shape
    assert d == PACK_ROWS * LANES
    tm = MOVE_TM
    return pl.pallas_call(
        _dispatch_kernel,
        grid_spec=pltpu.PrefetchScalarGridSpec(
            num_scalar_prefetch=2,
            grid=(t // tm,),
            in_specs=[pl.BlockSpec((tm, d), lambda i, *_: (i, 0)),
                      pl.BlockSpec((8, tm), lambda i, *_: (0, i))],
            out_specs=pl.BlockSpec(memory_space=pl.ANY),
            scratch_shapes=[pltpu.VMEM((2, 2 * tm * PACK_ROWS, LANES), ROW_DTYPE),
                            pltpu.SemaphoreType.DMA((2,))]),
        out_shape=jax.ShapeDtypeStruct((2 * t * PACK_ROWS, LANES), ROW_DTYPE),
        compiler_params=_params("arbitrary"),
    )(run_dst, run_len, h2, rt)


def _expert_kernel(vt_ref, ve_ref, va_ref, vb_ref, nv_ref, xs_ref, wg_ref, wu_ref, wd_ref, ys_ref,
                   wg_bf, wu_bf, wd_bf):
    v = pl.program_id(0)
    valid = v < nv_ref[0]
    prev = jnp.maximum(v - 1, 0)
    new_expert = (v == 0) | (ve_ref[v] != ve_ref[prev])
    new_tile = (v == 0) | (vt_ref[v] != vt_ref[prev])

    @pl.when(valid & new_expert)
    def _():
        wg_bf[...] = wg_ref[0].astype(BF16)
        wu_bf[...] = wu_ref[0].astype(BF16)
        wd_bf[...] = wd_ref[0].astype(BF16)

    def expert_rows():
        xb = _unpack_rows(xs_ref)
        gate = _dot(xb, wg_bf[...])
        up = _dot(xb, wu_bf[...])
        act = gate * jax.nn.sigmoid(gate) * up
        return _dot(act.astype(BF16), wd_bf[...])

    @pl.when(valid & new_tile)
    def _():
        _pack_rows(ys_ref, expert_rows())

    @pl.when(valid & jnp.logical_not(new_tile))
    def _():
        y = expert_rows()
        te = y.shape[0]
        row = lax.broadcasted_iota(jnp.int32, (te, 1), 0)
        mine = (row >= va_ref[v]) & (row < vb_ref[v])
        for c in range(PACK_ROWS):
            rows = pl.ds(c, te, stride=PACK_ROWS)
            ys_ref[rows, :] = jnp.where(mine, y[:, c * LANES:(c + 1) * LANES], ys_ref[rows, :])


def _expert_mlp(visit_tile, visit_expert, visit_lo, visit_hi, n_visits, xs, w_gate, w_up, w_down):
    te = EXPERT_TE
    d, ff = w_gate.shape[1:]
    blk = (te * PACK_ROWS, LANES)
    tile = lambda v, vt, *_: (vt[v], 0)
    wsel = lambda v, vt, ve, *_: (ve[v], 0, 0)
    return pl.pallas_call(
        _expert_kernel,
        grid_spec=pltpu.PrefetchScalarGridSpec(
            num_scalar_prefetch=5,
            grid=(visit_tile.shape[0],),
            in_specs=[pl.BlockSpec(blk, tile),
                      pl.BlockSpec((1, d, ff), wsel), pl.BlockSpec((1, d, ff), wsel),
                      pl.BlockSpec((1, ff, d), wsel)],
            out_specs=pl.BlockSpec(blk, tile),
            scratch_shapes=[pltpu.VMEM((d, ff), BF16), pltpu.VMEM((d, ff), BF16),
                            pltpu.VMEM((ff, d), BF16)]),
        out_shape=jax.ShapeDtypeStruct(xs.shape, ROW_DTYPE),
        compiler_params=_params("arbitrary"),
    )(visit_tile, visit_expert, visit_lo, visit_hi, n_visits, xs, w_gate, w_up, w_down)


def _combine_kernel(rdst_ref, rlen_ref, x1_ref, rinfo_ref, g2_ref, gpost_ref, ys_ref, o_ref,
                    gath_ref, sem):
    k = pl.program_id(0)
    nk = pl.num_programs(0)
    tm = x1_ref.shape[0]
    rows = 2 * tm
    slot = k % 2

    def gather_runs(tile, s):
        buf = gath_ref.at[s]

        def copy_piece(local, src, size):
            pltpu.make_async_copy(_token_rows(ys_ref, src, size), _token_rows(buf, local, size),
                                  sem.at[s]).start()

        _for_each_run_piece(rdst_ref, rlen_ref, tile, tm, copy_piece)

    @pl.when(k == 0)
    def _():
        gather_runs(0, 0)

    @pl.when(k + 1 < nk)
    def _():
        gather_runs(k + 1, 1 - slot)

    buf = gath_ref.at[slot]
    pltpu.make_async_copy(_token_rows(ys_ref, 0, rows), buf, sem.at[slot]).wait()
    g = _unpack_rows(buf)
    info = rinfo_ref[...]
    col = lax.broadcasted_iota(jnp.int32, (tm, rows), 1).astype(F32)
    pick0 = jnp.where(col == info[:, 2:3], 1.0, 0.0).astype(BF16)
    pick1 = jnp.where(col == info[:, 3:4], 1.0, 0.0).astype(BF16)
    fx = info[:, 4:5] * _dot(pick0, g) + info[:, 5:6] * _dot(pick1, g)
    o_ref[...] = x1_ref[...] + g2_ref[0, 0] * (_rms(fx) * gpost_ref[...])


def _combine(run_dst, run_len, x1, rinfo, mod, gpost, ys, seq):
    t, d = x1.shape
    tm = MOVE_TM
    tpb = seq // tm
    batch_of = lambda i: i // tpb
    return pl.pallas_call(
        _combine_kernel,
        grid_spec=pltpu.PrefetchScalarGridSpec(
            num_scalar_prefetch=2,
            grid=(t // tm,),
            in_specs=[pl.BlockSpec((tm, d), lambda i, *_: (i, 0)),
                      pl.BlockSpec((tm, LANES), lambda i, *_: (i, 0)),
                      _mod_spec(MOD_G2, batch_of)(d),
                      pl.BlockSpec((1, d), lambda i, *_: (0, 0)),
                      pl.BlockSpec(memory_space=pl.ANY)],
            out_specs=pl.BlockSpec((tm, d), lambda i, *_: (i, 0)),
            scratch_shapes=[pltpu.VMEM((2, 2 * tm * PACK_ROWS, LANES), ROW_DTYPE),
                            pltpu.SemaphoreType.DMA((2,))]),
        out_shape=jax.ShapeDtypeStruct((t, d), F32),
        compiler_params=_params("arbitrary"),
    )(run_dst, run_len, x1, rinfo, mod, gpost, ys)


def _rope_tables(seq):
    pos = np.arange(seq)
    row = (pos // GRID_W).astype(np.float32)
    col = (pos % GRID_W).astype(np.float32)
    axis_dim = HEAD_DIM // 2
    inv_freq = (ROPE_THETA ** (-np.arange(0, axis_dim, 2, dtype=np.float32) / axis_dim)).astype(
        np.float32)
    ang = np.concatenate([row[:, None] * inv_freq, col[:, None] * inv_freq], axis=-1)
    pair = (np.arange(LANES) % HEAD_DIM) // 2
    cos = np.cos(ang)[:, pair]
    sin = np.sin(ang)[:, pair]
    even = (np.arange(LANES) % 2) == 0
    tables = (cos, np.where(even, -sin, 0.0), np.where(even, 0.0, sin))
    tables = tables + tuple(tb.T for tb in tables)
    return tuple(jnp.asarray(tb, F32) for tb in tables)


def _segment_ones(n):
    seg = np.arange(n) // HEAD_DIM
    return jnp.asarray(seg[:, None] == seg[None, :], BF16)


def kernel(x, c, ctx, c_ctx, w_mod, b_mod, attn_pre_norm, attn_post_norm, w_in, a_sink,
           b_q_norm, b_k_norm, a_out_norm, b_out_norm, w_out, ffn_pre_norm, ffn_post_norm,
           w_group, b_group, w_router, b_router, w_gate, w_up, w_down):
    batch, seq, d = x.shape
    ctx_len = ctx.shape[1]
    assert w_mod.shape[0] == 1, "single-layer stack only (context stream is never updated)"
    assert seq % ATTN_A_TQ == 0 and seq >= ATTN_A_TQ + 2 * WINDOW
    assert seq % PROJ_TM == 0 and seq % ATTN_B_TQ == 0 and seq % OUT_TM == 0 and seq % MOVE_TM == 0
    t = batch * seq
    nq = d // 2
    nkv = nq // KV_GROUP
    assert nkv == LANES and w_in.shape[2] == 2 * nq + 4 * nkv

    cc = jnp.concatenate([c, c_ctx[None, :], jnp.zeros((16 - batch - 1, d), F32)], axis=0)
    mod = _modulation(cc, w_mod[0], b_mod[0]).reshape(cc.shape[0], 6, 1, d)

    x2 = x.reshape(t, d)
    c2 = ctx.reshape(batch * ctx_len, d)
    gpre = attn_pre_norm[0].reshape(1, d)
    w_in_bf = w_in[0].astype(BF16)
    qn = jnp.tile(b_q_norm[0], nq // HEAD_DIM).reshape(1, nq)
    kn = jnp.tile(b_k_norm[0], nkv // HEAD_DIM).reshape(1, nkv)
    seg_q, seg_k = _segment_ones(nq), _segment_ones(nkv)
    qat, ka, vat, qbt, kb, vbt, stats = _project_latents(
        x2, mod, gpre, w_in_bf, _rope_tables(seq), qn, kn, seg_q, seg_k, seq)
    kca, vcat, kcb, vcbt, ctx_stats = _project_context(
        c2, mod, batch, gpre, w_in_bf, kn, seg_k, ctx_len)

    q_sq = jnp.max(stats[:, 0, 0])
    k_sq = jnp.maximum(jnp.max(stats[:, 0, 1]), jnp.max(ctx_stats[:, 0, 1]))
    bound_a = jnp.maximum(1.01 * jnp.sqrt(q_sq * k_sq), jnp.max(a_sink[0]) * LOG2_E).reshape(1)
    oa = _attention_a(bound_a, a_sink[0], qat, ka, vat, kca, vcat, batch, seq, ctx_len)
    score_bound = (1.01 * HEAD_DIM ** 0.5 * LOG2_E
                   * jnp.max(jnp.abs(b_q_norm[0])) * jnp.max(jnp.abs(b_k_norm[0]))).reshape(1)
    ob = _attention_b(score_bound, qbt, kb, vbt, kcb, vcbt, batch, seq, ctx_len)

    w_out_bf = w_out[0].astype(BF16)
    lane_pad = LANES - N_EXPERTS - N_GROUPS
    w_r = jnp.pad(jnp.concatenate([w_router[0], w_group[0]], axis=1), ((0, 0), (0, lane_pad)))
    w_r_hi = w_r.astype(BF16)
    w_r_lo = (w_r - w_r_hi.astype(F32)).astype(BF16)
    b_r = jnp.pad(jnp.concatenate([b_router[0], b_group[0]]), (0, lane_pad)).reshape(1, LANES)
    x1, h2, rinfo, rt, tcarry, tcnt, counts = _out_and_route(
        oa, ob, x2, mod, a_out_norm[0].reshape(1, nq), b_out_norm[0].reshape(1, nq),
        attn_post_norm[0].reshape(1, d), ffn_pre_norm[0].reshape(1, d),
        w_out_bf, w_r_hi, w_r_lo, b_r, seq)

    te = EXPERT_TE
    assert (2 * t) % te == 0
    cnt = counts[0, :N_EXPERTS].astype(jnp.int32)
    ends = jnp.cumsum(cnt)
    starts = ends - cnt
    run_dst = (starts[None, :] + tcarry[:, 0, :N_EXPERTS].astype(jnp.int32)).reshape(-1)
    run_len = tcnt[:, 0, :N_EXPERTS].astype(jnp.int32).reshape(-1)
    first_tile = starts // te
    n_vis = jnp.where(cnt > 0, (ends - 1) // te - first_tile + 1, 0)
    vis_end = jnp.cumsum(n_vis)
    n_visits = vis_end[-1]
    v = jnp.minimum(jnp.arange(2 * t // te + N_EXPERTS, dtype=jnp.int32), n_visits - 1)
    v_expert = jnp.sum(vis_end[None, :] <= v[:, None], axis=1).astype(jnp.int32)
    pick = (v_expert[:, None] == jnp.arange(N_EXPERTS)[None, :]).astype(jnp.int32)
    of_expert = lambda table: jnp.sum(pick * table[None, :], axis=1)
    v_tile = of_expert(first_tile) + v - of_expert(vis_end - n_vis)
    v_lo = jnp.maximum(of_expert(starts) - v_tile * te, 0)
    v_hi = jnp.minimum(of_expert(ends) - v_tile * te, te)

    xs = _dispatch(run_dst, run_len, h2, rt)
    ys = _expert_mlp(v_tile, v_expert, v_lo, v_hi, n_visits.reshape(1), xs,
                     w_gate[0], w_up[0], w_down[0])
    out = _combine(run_dst, run_len, x1, rinfo, mod, ffn_post_norm[0].reshape(1, d), ys, seq)
    return out.reshape(batch, seq, d)
```

```python
import functools

import jax
import jax.numpy as jnp
import numpy as np
from jax import lax
from jax.experimental import pallas as pl
from jax.experimental.pallas import tpu as pltpu

F32 = jnp.float32
BF16 = jnp.bfloat16

GRID_W = 64
HEAD_DIM = 64
KV_GROUP = 4
WINDOW = 128
ROPE_THETA = 10000.0
N_GROUPS = 4
EXPERTS_PER_GROUP = 8
N_EXPERTS = N_GROUPS * EXPERTS_PER_GROUP
EPS = 1e-6
NEG_INF = -1e30
LOG2_E = 1.4426950408889634
SAFE_SOFTMAX_SHIFT = 40.0

LANES = 128
V7X_VMEM_LIMIT = 56 * 1024 * 1024

PROJ_TM = 512
ATTN_A_TQ = 256
ATTN_B_TQ = 256
OUT_TM = 512
EXPERT_TE = 512
MOVE_TM = OUT_TM


def _params(*sem):
    return pltpu.CompilerParams(dimension_semantics=sem, vmem_limit_bytes=V7X_VMEM_LIMIT)


def _dot(a, b):
    return jnp.dot(a, b, preferred_element_type=F32)


def _dot_nt(a, b):
    return lax.dot_general(a, b, (((1,), (1,)), ((), ())), preferred_element_type=F32)


def _rms(x):
    return x * lax.rsqrt(jnp.mean(x * x, axis=-1, keepdims=True) + EPS)


def _split_bf16(x):
    hi = x.astype(BF16)
    lo = (x - hi.astype(F32)).astype(BF16)
    return hi, lo


def _mod_kernel(c_ref, w_ref, b_ref, o_ref):
    cc = c_ref[...]
    s = cc * jax.nn.sigmoid(cc)
    s_hi, s_lo = _split_bf16(s)
    w_hi, w_lo = _split_bf16(w_ref[...])
    o_ref[...] = _dot(s_hi, w_hi) + _dot(s_lo, w_hi) + _dot(s_hi, w_lo) + b_ref[...]


def _modulation(cc, w_mod, b_mod):
    rows, d = cc.shape
    n = w_mod.shape[1]
    bn = 1024
    return pl.pallas_call(
        _mod_kernel,
        grid=(n // bn,),
        in_specs=[pl.BlockSpec((rows, d), lambda i: (0, 0)),
                  pl.BlockSpec((d, bn), lambda i: (0, i)),
                  pl.BlockSpec((1, bn), lambda i: (0, i))],
        out_specs=pl.BlockSpec((rows, bn), lambda i: (0, i)),
        out_shape=jax.ShapeDtypeStruct((rows, n), F32),
        compiler_params=_params("arbitrary"),
    )(cc, w_mod, b_mod.reshape(1, n))


def _rope(x, cos, sin_a, sin_b):
    return x * cos + pltpu.roll(x, LANES - 1, 1) * sin_a + pltpu.roll(x, 1, 1) * sin_b


def _head_norm(x, seg_ref, gain):
    ss = _dot((x * x).astype(BF16), seg_ref[...])
    return x * lax.rsqrt(ss * (1.0 / HEAD_DIM) + EPS) * gain


def _max_head_sq_norm(x, seg_ref):
    ss = _dot((x * x).astype(BF16), seg_ref[...])
    return jnp.max(jnp.max(ss, axis=1, keepdims=True), axis=0, keepdims=True)


def _norm_stats(q_sq, k_sq):
    lane = lax.broadcasted_iota(jnp.int32, (1, LANES), 1)
    zero = jnp.zeros((1, LANES), F32)
    return jnp.where(lane == 0, q_sq, zero) + jnp.where(lane == 1, k_sq, zero)


def _rope_t(xt, cos_t, sin_a_t, sin_b_t):
    return (xt * cos_t + pltpu.roll(xt, LANES - 1, 0) * sin_a_t
            + pltpu.roll(xt, 1, 0) * sin_b_t)


def _proj_kernel(x_ref, sc_ref, sh_ref, gpre_ref, w_ref, cos_ref, sa_ref, sb_ref,
                 cos_t_ref, sa_t_ref, sb_t_ref, qn_ref, kn_ref, seg_q_ref, seg_k_ref,
                 qat_ref, ka_ref, vat_ref, qbt_ref, kb_ref, vbt_ref, stats_ref, w_bf):
    h = _rms(x_ref[...]) * gpre_ref[...] * (1.0 + sc_ref[0, 0]) + sh_ref[0, 0]
    @pl.when(pl.program_id(0) == 0)
    def _():
        w_bf[...] = w_ref[...].astype(BF16)

    p = _dot(h.astype(BF16), w_bf[...])
    cos, sa, sb = cos_ref[...], sa_ref[...], sb_ref[...]
    q_scale = HEAD_DIM ** -0.5 * LOG2_E
    cos_t, sa_t, sb_t = cos_t_ref[...] * q_scale, sa_t_ref[...] * q_scale, sb_t_ref[...] * q_scale
    nq = qat_ref.shape[0]
    for c in range(nq // LANES):
        qat_ref[c * LANES:(c + 1) * LANES, :] = _rope_t(
            p[:, c * LANES:(c + 1) * LANES].T, cos_t, sa_t, sb_t).astype(BF16)
    o = nq
    ka_ref[...] = _rope(p[:, o:o + LANES], cos, sa, sb).astype(BF16)
    vat_ref[...] = p[:, o + LANES:o + 2 * LANES].T.astype(BF16)
    stats_ref[0] = _norm_stats(_max_head_sq_norm(p[:, 0:nq], seg_q_ref) * (q_scale * q_scale),
                               _max_head_sq_norm(p[:, o:o + LANES], seg_k_ref))
    o += 2 * LANES
    qb = _head_norm(p[:, o:o + nq], seg_q_ref, qn_ref[...])
    for c in range(nq // LANES):
        qbt_ref[c * LANES:(c + 1) * LANES, :] = _rope_t(
            qb[:, c * LANES:(c + 1) * LANES].T, cos_t, sa_t, sb_t).astype(BF16)
    o += nq
    kb = _head_norm(p[:, o:o + LANES], seg_k_ref, kn_ref[...])
    kb_ref[...] = _rope(kb, cos, sa, sb).astype(BF16)
    vbt_ref[...] = p[:, o + LANES:o + 2 * LANES].T.astype(BF16)


def _ctx_proj_kernel(x_ref, sc_ref, sh_ref, gpre_ref, wa_ref, wb_ref, kn_ref, seg_k_ref,
                     ka_ref, vat_ref, kb_ref, vbt_ref, stats_ref):
    h = (_rms(x_ref[...]) * gpre_ref[...] * (1.0 + sc_ref[0, 0]) + sh_ref[0, 0]).astype(BF16)
    pa = _dot(h, wa_ref[...].astype(BF16))
    pb = _dot(h, wb_ref[...].astype(BF16))
    stats_ref[0] = _norm_stats(0.0, _max_head_sq_norm(pa[:, 0:LANES], seg_k_ref))
    ka_ref[...] = pa[:, 0:LANES].astype(BF16)
    vat_ref[...] = pa[:, LANES:2 * LANES].T.astype(BF16)
    kb_ref[...] = _head_norm(pb[:, 0:LANES], seg_k_ref, kn_ref[...]).astype(BF16)
    vbt_ref[...] = pb[:, LANES:2 * LANES].T.astype(BF16)


def _mod_spec(chunk, row_of_step):
    return lambda d: pl.BlockSpec((1, 1, 1, d), lambda i, *_: (row_of_step(i), chunk, 0, 0))


MOD_SH1, MOD_SC1, MOD_G1, MOD_SH2, MOD_SC2, MOD_G2 = range(6)


def _project_latents(x2, mod, gpre, w_in, tables, qn, kn, seg_q, seg_k, seq):
    t, d = x2.shape
    tm = PROJ_TM
    tpb = seq // tm
    nq = seg_q.shape[0]
    const = lambda shape: pl.BlockSpec(shape, lambda i: (0,) * len(shape))
    batch_of = lambda i: i // tpb
    table = pl.BlockSpec((tm, LANES), lambda i: (i % tpb, 0))
    table_t = pl.BlockSpec((LANES, tm), lambda i: (0, i % tpb))
    k_spec = pl.BlockSpec((tm, LANES), lambda i: (i, 0))
    k_shape = jax.ShapeDtypeStruct((t, LANES), BF16)
    vt_spec = pl.BlockSpec((LANES, tm), lambda i: (0, i))
    vt_shape = jax.ShapeDtypeStruct((LANES, t), BF16)
    qt_spec = pl.BlockSpec((nq, tm), lambda i: (0, i))
    qt_shape = jax.ShapeDtypeStruct((nq, t), BF16)
    return pl.pallas_call(
        _proj_kernel,
        grid=(t // tm,),
        in_specs=[pl.BlockSpec((tm, d), lambda i: (i, 0)),
                  _mod_spec(MOD_SC1, batch_of)(d), _mod_spec(MOD_SH1, batch_of)(d), const((1, d)),
                  const(w_in.shape), table, table, table, table_t, table_t, table_t,
                  const((1, nq)), const((1, LANES)), const(seg_q.shape), const(seg_k.shape)],
        out_specs=[qt_spec, k_spec, vt_spec, qt_spec, k_spec, vt_spec,
                   pl.BlockSpec((1, 1, LANES), lambda i: (i, 0, 0))],
        out_shape=[qt_shape, k_shape, vt_shape, qt_shape, k_shape, vt_shape,
                   jax.ShapeDtypeStruct((t // tm, 1, LANES), F32)],
        scratch_shapes=[pltpu.VMEM(w_in.shape, BF16)],
        compiler_params=_params("arbitrary"),
    )(x2, mod, mod, gpre, w_in, *tables, qn, kn, seg_q, seg_k)


def _project_context(c2, mod, ctx_row, gpre, w_in, kn, seg_k, ctx_len):
    t, d = c2.shape
    ctx_mod = lambda i: ctx_row
    nq = (w_in.shape[1] - 4 * LANES) // 2
    kv = 2 * LANES
    assert nq % kv == 0
    group_kv = lambda g: pl.BlockSpec((d, kv), lambda i: (0, (g * (nq + kv) + nq) // kv))
    const = lambda shape: pl.BlockSpec(shape, lambda i: (0,) * len(shape))
    k_spec = pl.BlockSpec((ctx_len, LANES), lambda i: (i, 0))
    k_shape = jax.ShapeDtypeStruct((t, LANES), BF16)
    vt_spec = pl.BlockSpec((LANES, ctx_len), lambda i: (0, i))
    vt_shape = jax.ShapeDtypeStruct((LANES, t), BF16)
    return pl.pallas_call(
        _ctx_proj_kernel,
        grid=(t // ctx_len,),
        in_specs=[pl.BlockSpec((ctx_len, d), lambda i: (i, 0)),
                  _mod_spec(MOD_SC1, ctx_mod)(d), _mod_spec(MOD_SH1, ctx_mod)(d),
                  const((1, d)), group_kv(0), group_kv(1), const((1, LANES)), const(seg_k.shape)],
        out_specs=[k_spec, vt_spec, k_spec, vt_spec,
                   pl.BlockSpec((1, 1, LANES), lambda i: (i, 0, 0))],
        out_shape=[k_shape, vt_shape, k_shape, vt_shape,
                   jax.ShapeDtypeStruct((t // ctx_len, 1, LANES), F32)],
        compiler_params=_params("arbitrary"),
    )(c2, mod, mod, gpre, w_in, w_in, kn, seg_k)


def _attend_t(w, k, kc, vt, vct, shift=None, bias=None, sink=None):
    st = _dot(k, w)
    sct = _dot(kc, w)
    if bias is not None:
        tq = bias.shape[1]
        st = jnp.concatenate([st[:, c * tq:(c + 1) * tq] + bias
                              for c in range(st.shape[1] // tq)], axis=1)
    if shift is None:
        shift = jnp.maximum(jnp.max(st, axis=0, keepdims=True),
                            jnp.max(sct, axis=0, keepdims=True))
        if sink is not None:
            shift = jnp.maximum(shift, sink)
    pt = jnp.exp2(st - shift)
    pct = jnp.exp2(sct - shift)
    denom = jnp.sum(pt, axis=0, keepdims=True) + jnp.sum(pct, axis=0, keepdims=True)
    if sink is not None:
        denom = denom + jnp.exp2(sink - shift)
    o2 = _dot(vt, pt.astype(BF16)) + _dot(vct, pct.astype(BF16))
    return o2, denom


def _all_heads_t(qt_ref, o_ref, attend):
    tq = qt_ref.shape[1]
    n_kv = LANES // HEAD_DIM
    zeros = jnp.zeros((HEAD_DIM, tq), BF16)
    outs = []
    for h in range(qt_ref.shape[0] // HEAD_DIM):
        g = h // KV_GROUP
        qh = qt_ref[h * HEAD_DIM:(h + 1) * HEAD_DIM, :]
        w = jnp.concatenate([zeros] * g + [qh] + [zeros] * (n_kv - 1 - g), axis=0)
        o2, denom = attend(h, w)
        outs.append(o2[g * HEAD_DIM:(g + 1) * HEAD_DIM, :] / denom)
    o_ref[...] = jnp.concatenate(outs, axis=0).T.astype(BF16)


def _all_heads_fused_t(qt_ref, o_ref, attend):
    tq = qt_ref.shape[1]
    n_heads = qt_ref.shape[0] // HEAD_DIM
    n_kv = LANES // HEAD_DIM
    rows = []
    for g in range(n_kv):
        heads = [qt_ref[h * HEAD_DIM:(h + 1) * HEAD_DIM, :] if h // KV_GROUP == g
                 else jnp.zeros((HEAD_DIM, tq), BF16) for h in range(n_heads)]
        rows.append(jnp.concatenate(heads, axis=1))
    w = jnp.concatenate(rows, axis=0)
    o2, denom = attend(0, w)
    o2 = o2 / denom
    outs = [o2[(h // KV_GROUP) * HEAD_DIM:(h // KV_GROUP + 1) * HEAD_DIM, h * tq:(h + 1) * tq]
            for h in range(n_heads)]
    o_ref[...] = jnp.concatenate(outs, axis=0).T.astype(BF16)


def _attn_b_kernel(bound_ref, qt_ref, k_ref, vt_ref, kc_ref, vct_ref, o_ref):
    k, kc, vt, vct = k_ref[...], kc_ref[...], vt_ref[...], vct_ref[...]
    bound = bound_ref[0]

    @pl.when(bound <= SAFE_SOFTMAX_SHIFT)
    def _():
        _all_heads_fused_t(qt_ref, o_ref, lambda h, w: _attend_t(w, k, kc, vt, vct, shift=bound))

    @pl.when(jnp.logical_not(bound <= SAFE_SOFTMAX_SHIFT))
    def _():
        _all_heads_t(qt_ref, o_ref, lambda h, w: _attend_t(w, k, kc, vt, vct))


def _attn_specs(nq, tq, seq, ctx_len):
    nqb = seq // tq
    return dict(
        qt=pl.BlockSpec((nq, tq), lambda b, i: (0, b * nqb + i)),
        k=pl.BlockSpec((seq, LANES), lambda b, i: (b, 0)),
        vt=pl.BlockSpec((LANES, seq), lambda b, i: (0, b)),
        kc=pl.BlockSpec((ctx_len, LANES), lambda b, i: (b, 0)),
        vct=pl.BlockSpec((LANES, ctx_len), lambda b, i: (0, b)),
        out=pl.BlockSpec((tq, nq), lambda b, i: (b * nqb + i, 0)))


def _attention_b(score_bound, qbt, kb, vbt, kcb, vcbt, batch, seq, ctx_len):
    nq, t = qbt.shape
    tq = ATTN_B_TQ
    sp = _attn_specs(nq, tq, seq, ctx_len)
    return pl.pallas_call(
        _attn_b_kernel,
        grid=(batch, seq // tq),
        in_specs=[pl.BlockSpec(memory_space=pltpu.SMEM),
                  sp["qt"], sp["k"], sp["vt"], sp["kc"], sp["vct"]],
        out_specs=sp["out"],
        out_shape=jax.ShapeDtypeStruct((t, nq), BF16),
        compiler_params=_params("arbitrary", "arbitrary"),
    )(score_bound, qbt, kb, vbt, kcb, vcbt)


def _attn_a_kernel(bound_ref, sink_ref, qt_ref, k_ref, vt_ref, kc_ref, vct_ref, bias_ref, o_ref,
                   *, seq):
    i = pl.program_id(1)
    tq = qt_ref.shape[1]
    n_heads = qt_ref.shape[0] // HEAD_DIM
    win = tq + 2 * WINDOW
    start = pl.multiple_of(jnp.clip(i * tq - WINDOW, 0, seq - win), WINDOW)
    k = k_ref[pl.ds(start, win), :]
    vt = vt_ref[:, pl.ds(start, win)]
    kc, vct = kc_ref[...], vct_ref[...]
    sinks = [sink_ref[h] * LOG2_E for h in range(n_heads)]

    bound = bound_ref[0]
    small = bound <= SAFE_SOFTMAX_SHIFT

    @pl.when(small)
    def _():
        sink_row = jnp.concatenate([jnp.full((1, tq), s, F32) for s in sinks], axis=1)
        _all_heads_fused_t(qt_ref, o_ref, lambda h, w: _attend_t(
            w, k, kc, vt, vct, shift=bound, bias=bias_ref[0], sink=sink_row))

    @pl.when(jnp.logical_not(small))
    def _():
        _all_heads_t(qt_ref, o_ref, lambda h, w: _attend_t(
            w, k, kc, vt, vct, bias=bias_ref[0], sink=sinks[h]))


def _band_bias(tq):
    win = tq + 2 * WINDOW
    r = np.arange(win)[:, None]
    j = np.arange(tq)[None, :]
    tables = [np.where(np.abs(off + r - j) <= WINDOW, 0.0, NEG_INF)
              for off in (0, -WINDOW, -2 * WINDOW)]
    return jnp.asarray(np.stack(tables), F32)


def _attention_a(score_bound, sink, qat, ka, vat, kca, vcat, batch, seq, ctx_len):
    nq, t = qat.shape
    tq = ATTN_A_TQ
    nqb = seq // tq
    win = tq + 2 * WINDOW
    assert nqb >= 2 and tq >= WINDOW
    sp = _attn_specs(nq, tq, seq, ctx_len)
    which = lambda b, i: (jnp.where(i == 0, 0, jnp.where(i == nqb - 1, 2, 1)), 0, 0)
    return pl.pallas_call(
        functools.partial(_attn_a_kernel, seq=seq),
        grid=(batch, nqb),
        in_specs=[pl.BlockSpec(memory_space=pltpu.SMEM), pl.BlockSpec(memory_space=pltpu.SMEM),
                  sp["qt"], sp["k"], sp["vt"], sp["kc"], sp["vct"],
                  pl.BlockSpec((1, win, tq), which)],
        out_specs=sp["out"],
        out_shape=jax.ShapeDtypeStruct((t, nq), BF16),
        compiler_params=_params("arbitrary", "arbitrary"),
    )(score_bound, sink, qat, ka, vat, kca, vcat, _band_bias(tq))


def _out_kernel(oa_ref, ob_ref, x_ref, g1_ref, sc2_ref, sh2_ref, ga_ref, gb_ref, gpost_ref,
                gpre2_ref, woa_ref, wob_ref, wrh_ref, wrl_ref, br_ref,
                x1_ref, h2_ref, rinfo_ref, rt_ref, tcarry_ref, tcnt_ref, cnt_ref,
                carry_ref, woa_bf, wob_bf):
    step = pl.program_id(0)

    @pl.when(step == 0)
    def _():
        carry_ref[...] = jnp.zeros_like(carry_ref)
        woa_bf[...] = woa_ref[...].astype(BF16)
        wob_bf[...] = wob_ref[...].astype(BF16)

    na = _rms(oa_ref[...].astype(F32)) * ga_ref[...]
    nb = _rms(ob_ref[...].astype(F32)) * gb_ref[...]
    ox = _dot(na.astype(BF16), woa_bf[...]) + _dot(nb.astype(BF16), wob_bf[...])
    x1 = x_ref[...] + g1_ref[0, 0] * (_rms(ox) * gpost_ref[...])
    x1_ref[...] = x1
    h2 = _rms(x1) * gpre2_ref[...] * (1.0 + sc2_ref[0, 0]) + sh2_ref[0, 0]
    h_hi, h_lo = _split_bf16(h2)
    h2_ref[...] = h_hi

    logits = (_dot(h_hi, wrh_ref[...]) + _dot(h_lo, wrh_ref[...]) + _dot(h_hi, wrl_ref[...])
              + br_ref[...])
    tm = logits.shape[0]
    lt = logits.T
    row = lax.broadcasted_iota(jnp.int32, lt.shape, 0)
    rowf = row.astype(F32)
    big = jnp.float32(1e9)
    ninf = jnp.float32(-jnp.inf)
    colmax = lambda v: jnp.max(v, axis=0, keepdims=True)
    colmin = lambda v: jnp.min(v, axis=0, keepdims=True)
    colsum = lambda v: jnp.sum(v, axis=0, keepdims=True)

    gmask = (row >= N_EXPERTS) & (row < N_EXPERTS + N_GROUPS)
    lg = jnp.where(gmask, lt, ninf)
    gmax = colmax(lg)
    gidx = colmin(jnp.where(lg == gmax, rowf, big)) - N_EXPERTS
    g_w = 1.0 / colsum(jnp.exp(lg - gmax))
    row_group = (row // EXPERTS_PER_GROUP).astype(F32)
    emask = (row < N_EXPERTS) & (row_group == gidx)
    le = jnp.where(emask, lt, ninf)
    m1 = colmax(le)
    i1 = colmin(jnp.where(le == m1, rowf, big))
    le2 = jnp.where(rowf == i1, ninf, le)
    m2 = colmax(le2)
    i2 = colmin(jnp.where(le2 == m2, rowf, big))
    e2 = jnp.exp(m2 - m1)
    w0 = g_w / (1.0 + e2)
    w1 = g_w * e2 / (1.0 + e2)

    hit1 = rowf == i1
    hit2 = rowf == i2
    onehot = jnp.where(hit1, 1.0, jnp.where(hit2, 1.0, 0.0)).astype(F32)
    r = lax.broadcasted_iota(jnp.int32, (tm, tm), 0)
    c = lax.broadcasted_iota(jnp.int32, (tm, tm), 1)
    earlier = jnp.where(r < c, 1.0, 0.0).astype(BF16)
    within = _dot(onehot.astype(BF16), earlier)
    tile_cnt = jnp.broadcast_to(jnp.sum(onehot, axis=1, keepdims=True), (LANES, LANES))
    er = lax.broadcasted_iota(jnp.int32, (LANES, LANES), 0)
    ec = lax.broadcasted_iota(jnp.int32, (LANES, LANES), 1)
    below = jnp.where(er > ec, 1.0, 0.0).astype(BF16)
    cnt_hi = jnp.floor(tile_cnt * (1.0 / 32.0))
    cnt_lo = tile_cnt - 32.0 * cnt_hi
    run_start = 32.0 * _dot(below, cnt_hi.astype(BF16)) + _dot(below, cnt_lo.astype(BF16))
    local = within + run_start[:, 0:1]
    pos0 = colsum(jnp.where(hit1, local, 0.0))
    pos1 = colsum(jnp.where(hit2, local, 0.0))
    cnt_row = tile_cnt.T[0:1, :]
    tcarry_ref[0] = carry_ref[...]
    tcnt_ref[0] = cnt_row
    carry_ref[...] += cnt_row
    cnt_ref[...] = carry_ref[...]

    fields = jnp.concatenate([i1, i2, pos0, pos1, w0, w1, jnp.zeros((2, tm), F32)], axis=0)
    rt_ref[...] = fields
    rinfo_ref[...] = jnp.concatenate(
        [fields, jnp.zeros((LANES - 8, tm), F32)], axis=0).T


def _out_and_route(oa, ob, x2, mod, ga, gb, gpost, gpre2, w_out, wrh, wrl, br, seq):
    t, d = x2.shape
    tm = OUT_TM
    tpb = seq // tm
    nq = oa.shape[1]
    const = lambda shape: pl.BlockSpec(shape, lambda i: (0,) * len(shape))
    batch_of = lambda i: i // tpb
    rows = lambda n: pl.BlockSpec((tm, n), lambda i: (i, 0))
    per_tile = pl.BlockSpec((1, 1, LANES), lambda i: (i, 0, 0))
    w_half = lambda g: pl.BlockSpec((nq, d), lambda i: (g, 0))
    return pl.pallas_call(
        _out_kernel,
        grid=(t // tm,),
        in_specs=[rows(nq), rows(nq), rows(d),
                  _mod_spec(MOD_G1, batch_of)(d), _mod_spec(MOD_SC2, batch_of)(d),
                  _mod_spec(MOD_SH2, batch_of)(d),
                  const((1, nq)), const((1, nq)), const((1, d)), const((1, d)),
                  w_half(0), w_half(1), const(wrh.shape), const(wrl.shape),
                  const((1, LANES))],
        out_specs=[rows(d), rows(d), rows(LANES), pl.BlockSpec((8, tm), lambda i: (0, i)),
                   per_tile, per_tile, const((1, LANES))],
        out_shape=[jax.ShapeDtypeStruct((t, d), F32), jax.ShapeDtypeStruct((t, d), BF16),
                   jax.ShapeDtypeStruct((t, LANES), F32), jax.ShapeDtypeStruct((8, t), F32),
                   jax.ShapeDtypeStruct((t // tm, 1, LANES), F32),
                   jax.ShapeDtypeStruct((t // tm, 1, LANES), F32),
                   jax.ShapeDtypeStruct((1, LANES), F32)],
        scratch_shapes=[pltpu.VMEM((1, LANES), F32), pltpu.VMEM((nq, d), BF16),
                        pltpu.VMEM((nq, d), BF16)],
        compiler_params=_params("arbitrary"),
    )(oa, ob, x2, mod, mod, mod, ga, gb, gpost, gpre2, w_out, w_out, wrh, wrl, br)


PACK_ROWS = 8
ROW_DTYPE = F32


def _pack_rows(ref, x):
    n = x.shape[0]
    for c in range(PACK_ROWS):
        ref[pl.ds(c, n, stride=PACK_ROWS), :] = x[:, c * LANES:(c + 1) * LANES]


def _unpack_rows(ref):
    n = ref.shape[0] // PACK_ROWS
    return jnp.concatenate(
        [ref[pl.ds(c, n, stride=PACK_ROWS), :].astype(BF16) for c in range(PACK_ROWS)], axis=1)


def _for_each_run_piece(rdst_ref, rlen_ref, tile, max_len, fn):
    n_bits = max_len.bit_length()

    def run(e, local):
        length = rlen_ref[tile * N_EXPERTS + e]
        dst = rdst_ref[tile * N_EXPERTS + e]
        for b in range(n_bits):
            size = 1 << b

            @pl.when(((length >> b) & 1) == 1)
            def _():
                done = length & (size - 1)
                fn(local + done, dst + done, size)
        return local + length

    lax.fori_loop(0, N_EXPERTS, run, 0)


def _token_rows(ref, row0, n_rows):
    start = row0 * PACK_ROWS
    if not isinstance(start, int):
        start = pl.multiple_of(start, PACK_ROWS)
    return ref.at[pl.ds(start, n_rows * PACK_ROWS)]


def _dispatch_kernel(rdst_ref, rlen_ref, h_ref, rt_ref, xs_ref, sorted_ref, sem):
    k = pl.program_id(0)
    nk = pl.num_programs(0)
    tm = h_ref.shape[0]
    rows = 2 * tm
    slot = k % 2

    def wait_slot(s):
        pltpu.make_async_copy(sorted_ref.at[s], _token_rows(xs_ref, 0, rows), sem.at[s]).wait()

    @pl.when(k >= 2)
    def _():
        wait_slot(slot)

    pos0 = rt_ref[2:3, :]
    pos1 = rt_ref[3:4, :]
    r = lax.broadcasted_iota(jnp.int32, (rows, tm), 0).astype(F32)
    perm = jnp.where((r == pos0) | (r == pos1), 1.0, 0.0).astype(BF16)
    srt = _dot(perm, h_ref[...].astype(BF16))
    buf = sorted_ref.at[slot]
    _pack_rows(buf, srt)

    def copy_piece(local, dst, size):
        pltpu.make_async_copy(_token_rows(buf, local, size), _token_rows(xs_ref, dst, size),
                              sem.at[slot]).start()

    _for_each_run_piece(rdst_ref, rlen_ref, k, tm, copy_piece)

    @pl.when(k == nk - 1)
    def _():
        wait_slot(slot)

        @pl.when(nk >= 2)
        def _():
            wait_slot(1 - slot)


def _dispatch(run_dst, run_len, h2, rt):
    t, d = h2.shape
    assert d == PACK_ROWS * LANES
    tm = MOVE_TM
    return pl.pallas_call(
        _dispatch_kernel,
        grid_spec=pltpu.PrefetchScalarGridSpec(
            num_scalar_prefetch=2,
            grid=(t // tm,),
            in_specs=[pl.BlockSpec((tm, d), lambda i, *_: (i, 0)),
                      pl.BlockSpec((8, tm), lambda i, *_: (0, i))],
            out_specs=pl.BlockSpec(memory_space=pl.ANY),
            scratch_shapes=[pltpu.VMEM((2, 2 * tm * PACK_ROWS, LANES), ROW_DTYPE),
                            pltpu.SemaphoreType.DMA((2,))]),
        out_shape=jax.ShapeDtypeStruct((2 * t * PACK_ROWS, LANES), ROW_DTYPE),
        compiler_params=_params("arbitrary"),
    )(run_dst, run_len, h2, rt)


def _expert_kernel(vt_ref, ve_ref, va_ref, vb_ref, nv_ref, xs_ref, wg_ref, wu_ref, wd_ref, ys_ref,
                   wg_bf, wu_bf, wd_bf):
    v = pl.program_id(0)
    valid = v < nv_ref[0]
    prev = jnp.maximum(v - 1, 0)
    new_expert = (v == 0) | (ve_ref[v] != ve_ref[prev])
    new_tile = (v == 0) | (vt_ref[v] != vt_ref[prev])

    @pl.when(valid & new_expert)
    def _():
        wg_bf[...] = wg_ref[0].astype(BF16)
        wu_bf[...] = wu_ref[0].astype(BF16)
        wd_bf[...] = wd_ref[0].astype(BF16)

    def expert_rows():
        xb = _unpack_rows(xs_ref)
        gate = _dot(xb, wg_bf[...])
        up = _dot(xb, wu_bf[...])
        act = gate * jax.nn.sigmoid(gate) * up
        return _dot(act.astype(BF16), wd_bf[...])

    @pl.when(valid & new_tile)
    def _():
        _pack_rows(ys_ref, expert_rows())

    @pl.when(valid & jnp.logical_not(new_tile))
    def _():
        y = expert_rows()
        te = y.shape[0]
        row = lax.broadcasted_iota(jnp.int32, (te, 1), 0)
        mine = (row >= va_ref[v]) & (row < vb_ref[v])
        for c in range(PACK_ROWS):
            rows = pl.ds(c, te, stride=PACK_ROWS)
            ys_ref[rows, :] = jnp.where(mine, y[:, c * LANES:(c + 1) * LANES], ys_ref[rows, :])


def _expert_mlp(visit_tile, visit_expert, visit_lo, visit_hi, n_visits, xs, w_gate, w_up, w_down):
    te = EXPERT_TE
    d, ff = w_gate.shape[1:]
    blk = (te * PACK_ROWS, LANES)
    tile = lambda v, vt, *_: (vt[v], 0)
    wsel = lambda v, vt, ve, *_: (ve[v], 0, 0)
    return pl.pallas_call(
        _expert_kernel,
        grid_spec=pltpu.PrefetchScalarGridSpec(
            num_scalar_prefetch=5,
            grid=(visit_tile.shape[0],),
            in_specs=[pl.BlockSpec(blk, tile),
                      pl.BlockSpec((1, d, ff), wsel), pl.BlockSpec((1, d, ff), wsel),
                      pl.BlockSpec((1, ff, d), wsel)],
            out_specs=pl.BlockSpec(blk, tile),
            scratch_shapes=[pltpu.VMEM((d, ff), BF16), pltpu.VMEM((d, ff), BF16),
                            pltpu.VMEM((ff, d), BF16)]),
        out_shape=jax.ShapeDtypeStruct(xs.shape, ROW_DTYPE),
        compiler_params=_params("arbitrary"),
    )(visit_tile, visit_expert, visit_lo, visit_hi, n_visits, xs, w_gate, w_up, w_down)


def _combine_kernel(rdst_ref, rlen_ref, x1_ref, rinfo_ref, g2_ref, gpost_ref, ys_ref, o_ref,
                    gath_ref, sem):
    k = pl.program_id(0)
    nk = pl.num_programs(0)
    tm = x1_ref.shape[0]
    rows = 2 * tm
    slot = k % 2

    def gather_runs(tile, s):
        buf = gath_ref.at[s]

        def copy_piece(local, src, size):
            pltpu.make_async_copy(_token_rows(ys_ref, src, size), _token_rows(buf, local, size),
                                  sem.at[s]).start()

        _for_each_run_piece(rdst_ref, rlen_ref, tile, tm, copy_piece)

    @pl.when(k == 0)
    def _():
        gather_runs(0, 0)

    @pl.when(k + 1 < nk)
    def _():
        gather_runs(k + 1, 1 - slot)

    buf = gath_ref.at[slot]
    pltpu.make_async_copy(_token_rows(ys_ref, 0, rows), buf, sem.at[slot]).wait()
    g = _unpack_rows(buf)
    info = rinfo_ref[...]
    col = lax.broadcasted_iota(jnp.int32, (tm, rows), 1).astype(F32)
    pick0 = jnp.where(col == info[:, 2:3], 1.0, 0.0).astype(BF16)
    pick1 = jnp.where(col == info[:, 3:4], 1.0, 0.0).astype(BF16)
    fx = info[:, 4:5] * _dot(pick0, g) + info[:, 5:6] * _dot(pick1, g)
    o_ref[...] = x1_ref[...] + g2_ref[0, 0] * (_rms(fx) * gpost_ref[...])


def _combine(run_dst, run_len, x1, rinfo, mod, gpost, ys, seq):
    t, d = x1.shape
    tm = MOVE_TM
    tpb = seq // tm
    batch_of = lambda i: i // tpb
    return pl.pallas_call(
        _combine_kernel,
        grid_spec=pltpu.PrefetchScalarGridSpec(
            num_scalar_prefetch=2,
            grid=(t // tm,),
            in_specs=[pl.BlockSpec((tm, d), lambda i, *_: (i, 0)),
                      pl.BlockSpec((tm, LANES), lambda i, *_: (i, 0)),
                      _mod_spec(MOD_G2, batch_of)(d),
                      pl.BlockSpec((1, d), lambda i, *_: (0, 0)),
                      pl.BlockSpec(memory_space=pl.ANY)],
            out_specs=pl.BlockSpec((tm, d), lambda i, *_: (i, 0)),
            scratch_shapes=[pltpu.VMEM((2, 2 * tm * PACK_ROWS, LANES), ROW_DTYPE),
                            pltpu.SemaphoreType.DMA((2,))]),
        out_shape=jax.ShapeDtypeStruct((t, d), F32),
        compiler_params=_params("arbitrary"),
    )(run_dst, run_len, x1, rinfo, mod, gpost, ys)


def _rope_tables(seq):
    pos = np.arange(seq)
    row = (pos // GRID_W).astype(np.float32)
    col = (pos % GRID_W).astype(np.float32)
    axis_dim = HEAD_DIM // 2
    inv_freq = (ROPE_THETA ** (-np.arange(0, axis_dim, 2, dtype=np.float32) / axis_dim)).astype(
        np.float32)
    ang = np.concatenate([row[:, None] * inv_freq, col[:, None] * inv_freq], axis=-1)
    pair = (np.arange(LANES) % HEAD_DIM) // 2
    cos = np.cos(ang)[:, pair]
    sin = np.sin(ang)[:, pair]
    even = (np.arange(LANES) % 2) == 0
    tables = (cos, np.where(even, -sin, 0.0), np.where(even, 0.0, sin))
    tables = tables + tuple(tb.T for tb in tables)
    return tuple(jnp.asarray(tb, F32) for tb in tables)


def _segment_ones(n):
    seg = np.arange(n) // HEAD_DIM
    return jnp.asarray(seg[:, None] == seg[None, :], BF16)


def kernel(x, c, ctx, c_ctx, w_mod, b_mod, attn_pre_norm, attn_post_norm, w_in, a_sink,
           b_q_norm, b_k_norm, a_out_norm, b_out_norm, w_out, ffn_pre_norm, ffn_post_norm,
           w_group, b_group, w_router, b_router, w_gate, w_up, w_down):
    batch, seq, d = x.shape
    ctx_len = ctx.shape[1]
    assert w_mod.shape[0] == 1, "single-layer stack only (context stream is never updated)"
    assert seq % ATTN_A_TQ == 0 and seq >= ATTN_A_TQ + 2 * WINDOW
    assert seq % PROJ_TM == 0 and seq % ATTN_B_TQ == 0 and seq % OUT_TM == 0 and seq % MOVE_TM == 0
    t = batch * seq
    nq = d // 2
    nkv = nq // KV_GROUP
    assert nkv == LANES and w_in.shape[2] == 2 * nq + 4 * nkv

    cc = jnp.concatenate([c, c_ctx[None, :], jnp.zeros((16 - batch - 1, d), F32)], axis=0)
    mod = _modulation(cc, w_mod[0], b_mod[0]).reshape(cc.shape[0], 6, 1, d)

    x2 = x.reshape(t, d)
    c2 = ctx.reshape(batch * ctx_len, d)
    gpre = attn_pre_norm[0].reshape(1, d)
    qn =jnp.tile(b_q_norm[0], nq // HEAD_DIM).reshape(1, nq)
    kn = jnp.tile(b_k_norm[0], nkv // HEAD_DIM).reshape(1, nkv)
    seg_q, seg_k = _segment_ones(nq), _segment_ones(nkv)
    qat, ka, vat, qbt, kb, vbt, stats = _project_latents(
        x2, mod, gpre, w_in[0], _rope_tables(seq), qn, kn, seg_q, seg_k, seq)
    kca, vcat, kcb, vcbt, ctx_stats = _project_context(
        c2, mod, batch, gpre, w_in[0], kn, seg_k, ctx_len)

    sq = jnp.max(jnp.concatenate([stats, ctx_stats], axis=0), axis=(0, 1))
    q_sq, k_sq = sq[0], sq[1]
    bound_a = jnp.maximum(1.01 * jnp.sqrt(q_sq * k_sq), jnp.max(a_sink[0]) * LOG2_E).reshape(1)
    oa = _attention_a(bound_a, a_sink[0], qat, ka, vat, kca, vcat, batch, seq, ctx_len)
    score_bound = (1.01 * HEAD_DIM ** 0.5 * LOG2_E
                   * jnp.max(jnp.abs(b_q_norm[0])) * jnp.max(jnp.abs(b_k_norm[0]))).reshape(1)
    ob = _attention_b(score_bound, qbt, kb, vbt, kcb, vcbt, batch, seq, ctx_len)

    lane_pad = LANES - N_EXPERTS - N_GROUPS
    w_r = jnp.pad(jnp.concatenate([w_router[0], w_group[0]], axis=1), ((0, 0), (0, lane_pad)))
    w_r_hi = w_r.astype(BF16)
    w_r_lo = (w_r - w_r_hi.astype(F32)).astype(BF16)
    b_r = jnp.pad(jnp.concatenate([b_router[0], b_group[0]]), (0, lane_pad)).reshape(1, LANES)
    x1, h2, rinfo, rt, tcarry, tcnt, counts = _out_and_route(
        oa, ob, x2, mod, a_out_norm[0].reshape(1, nq), b_out_norm[0].reshape(1, nq),
        attn_post_norm[0].reshape(1, d), ffn_pre_norm[0].reshape(1, d),
        w_out[0], w_r_hi, w_r_lo, b_r, seq)

    te = EXPERT_TE
    assert (2 * t) % te == 0
    cnt = counts[0, :N_EXPERTS].astype(jnp.int32)
    ends = jnp.cumsum(cnt)
    starts = ends - cnt
    run_dst = (starts[None, :] + tcarry[:, 0, :N_EXPERTS].astype(jnp.int32)).reshape(-1)
    run_len = tcnt[:, 0, :N_EXPERTS].astype(jnp.int32).reshape(-1)
    first_tile = starts // te
    n_vis = jnp.where(cnt > 0, (ends - 1) // te - first_tile + 1, 0)
    vis_end = jnp.cumsum(n_vis)
    n_visits = vis_end[-1]
    v = jnp.minimum(jnp.arange(2 * t // te + N_EXPERTS, dtype=jnp.int32), n_visits - 1)
    v_expert = jnp.sum(vis_end[None, :] <= v[:, None], axis=1).astype(jnp.int32)
    pick = (v_expert[:, None] == jnp.arange(N_EXPERTS)[None, :]).astype(jnp.int32)
    of_expert = lambda table: jnp.sum(pick * table[None, :], axis=1)
    v_tile = of_expert(first_tile) + v - of_expert(vis_end - n_vis)
    v_lo = jnp.maximum(of_expert(starts) - v_tile * te, 0)
    v_hi = jnp.minimum(of_expert(ends) - v_tile * te, te)

    xs = _dispatch(run_dst, run_len, h2, rt)
    ys = _expert_mlp(v_tile, v_expert, v_lo, v_hi, n_visits.reshape(1), xs,
                     w_gate[0], w_up[0], w_down[0])
    out = _combine(run_dst, run_len, x1, rinfo, mod, ffn_post_norm[0].reshape(1, d), ys, seq)
    return out.reshape(batch, seq, d)
```

```python
import functools

import jax
import jax.numpy as jnp
import numpy as np
from jax import lax
from jax.experimental import pallas as pl
from jax.experimental.pallas import tpu as pltpu

F32 = jnp.float32
BF16 = jnp.bfloat16

GRID_W = 64
HEAD_DIM = 64
KV_GROUP = 4
WINDOW = 128
ROPE_THETA = 10000.0
N_GROUPS = 4
EXPERTS_PER_GROUP = 8
N_EXPERTS = N_GROUPS * EXPERTS_PER_GROUP
EPS = 1e-6
NEG_INF = -1e30
LOG2_E = 1.4426950408889634
SAFE_SOFTMAX_SHIFT = 40.0

LANES = 128
V7X_VMEM_LIMIT = 56 * 1024 * 1024

PROJ_TM = 512
ATTN_A_TQ = 256
ATTN_B_TQ = 256
OUT_TM = 512
EXPERT_TE = 512
MOVE_TM = OUT_TM


def _params(*sem):
    return pltpu.CompilerParams(dimension_semantics=sem, vmem_limit_bytes=V7X_VMEM_LIMIT)


def _dot(a, b):
    return jnp.dot(a, b, preferred_element_type=F32)


def _dot_nt(a, b):
    return lax.dot_general(a, b, (((1,), (1,)), ((), ())), preferred_element_type=F32)


def _rms(x):
    return x * lax.rsqrt(jnp.mean(x * x, axis=-1, keepdims=True) + EPS)


def _split_bf16(x):
    hi = x.astype(BF16)
    lo = (x - hi.astype(F32)).astype(BF16)
    return hi, lo


def _mod_kernel(c_ref, w_ref, b_ref, o_ref):
    cc = c_ref[...]
    s = cc * jax.nn.sigmoid(cc)
    s_hi, s_lo = _split_bf16(s)
    w_hi, w_lo = _split_bf16(w_ref[...])
    o_ref[...] = _dot(s_hi, w_hi) + _dot(s_lo, w_hi) + _dot(s_hi, w_lo) + b_ref[...]


def _modulation(cc, w_mod, b_mod):
    rows, d = cc.shape
    n = w_mod.shape[1]
    bn = 1024
    return pl.pallas_call(
        _mod_kernel,
        grid=(n // bn,),
        in_specs=[pl.BlockSpec((rows, d), lambda i: (0, 0)),
                  pl.BlockSpec((d, bn), lambda i: (0, i)),
                  pl.BlockSpec((1, bn), lambda i: (0, i))],
        out_specs=pl.BlockSpec((rows, bn), lambda i: (0, i)),
        out_shape=jax.ShapeDtypeStruct((rows, n), F32),
        compiler_params=_params("arbitrary"),
    )(cc, w_mod, b_mod.reshape(1, n))


def _rope(x, cos, sin_a, sin_b):
    return x * cos + pltpu.roll(x, LANES - 1, 1) * sin_a + pltpu.roll(x, 1, 1) * sin_b


def _head_norm(x, seg_ref, gain):
    ss = _dot((x * x).astype(BF16), seg_ref[...])
    return x * lax.rsqrt(ss * (1.0 / HEAD_DIM) + EPS) * gain


def _max_head_sq_norm(x, seg_ref):
    ss = _dot((x * x).astype(BF16), seg_ref[...])
    return jnp.max(jnp.max(ss, axis=1, keepdims=True), axis=0, keepdims=True)


def _norm_stats(q_sq, k_sq):
    lane = lax.broadcasted_iota(jnp.int32, (1, LANES), 1)
    zero = jnp.zeros((1, LANES), F32)
    return jnp.where(lane == 0, q_sq, zero) + jnp.where(lane == 1, k_sq, zero)


def _rope_t(xt, cos_t, sin_a_t, sin_b_t):
    return (xt * cos_t + pltpu.roll(xt, LANES - 1, 0) * sin_a_t
            + pltpu.roll(xt, 1, 0) * sin_b_t)


def _proj_kernel(x_ref, sc_ref, sh_ref, gpre_ref, w_ref, cos_ref, sa_ref, sb_ref,
                 cos_t_ref, sa_t_ref, sb_t_ref, qn_ref, kn_ref, seg_q_ref, seg_k_ref,
                 qat_ref, ka_ref, vat_ref, qbt_ref, kb_ref, vbt_ref, stats_ref):
    h = _rms(x_ref[...]) * gpre_ref[...] * (1.0 + sc_ref[0, 0]) + sh_ref[0, 0]
    p = _dot(h.astype(BF16), w_ref[...])
    cos, sa, sb = cos_ref[...], sa_ref[...], sb_ref[...]
    q_scale = HEAD_DIM ** -0.5 * LOG2_E
    cos_t, sa_t, sb_t = cos_t_ref[...] * q_scale, sa_t_ref[...] * q_scale, sb_t_ref[...] * q_scale
    nq = qat_ref.shape[0]
    for c in range(nq // LANES):
        qat_ref[c * LANES:(c + 1) * LANES, :] = _rope_t(
            p[:, c * LANES:(c + 1) * LANES].T, cos_t, sa_t, sb_t).astype(BF16)
    o = nq
    ka_ref[...] = _rope(p[:, o:o + LANES], cos, sa, sb).astype(BF16)
    vat_ref[...] = p[:, o + LANES:o + 2 * LANES].T.astype(BF16)
    stats_ref[0] = _norm_stats(_max_head_sq_norm(p[:, 0:nq], seg_q_ref) * (q_scale * q_scale),
                               _max_head_sq_norm(p[:, o:o + LANES], seg_k_ref))
    o += 2 * LANES
    qb = _head_norm(p[:, o:o + nq], seg_q_ref, qn_ref[...])
    for c in range(nq // LANES):
        qbt_ref[c * LANES:(c + 1) * LANES, :] = _rope_t(
            qb[:, c * LANES:(c + 1) * LANES].T, cos_t, sa_t, sb_t).astype(BF16)
    o += nq
    kb = _head_norm(p[:, o:o + LANES], seg_k_ref, kn_ref[...])
    kb_ref[...] = _rope(kb, cos, sa, sb).astype(BF16)
    vbt_ref[...] = p[:, o + LANES:o + 2 * LANES].T.astype(BF16)


def _ctx_proj_kernel(x_ref, sc_ref, sh_ref, gpre_ref, wa_ref, wb_ref, kn_ref, seg_k_ref,
                     ka_ref, vat_ref, kb_ref, vbt_ref, stats_ref):
    h = (_rms(x_ref[...]) * gpre_ref[...] * (1.0 + sc_ref[0, 0]) + sh_ref[0, 0]).astype(BF16)
    pa = _dot(h, wa_ref[...])
    pb = _dot(h, wb_ref[...])
    stats_ref[0] = _norm_stats(0.0, _max_head_sq_norm(pa[:, 0:LANES], seg_k_ref))
    ka_ref[...] = pa[:, 0:LANES].astype(BF16)
    vat_ref[...] = pa[:, LANES:2 * LANES].T.astype(BF16)
    kb_ref[...] = _head_norm(pb[:, 0:LANES], seg_k_ref, kn_ref[...]).astype(BF16)
    vbt_ref[...] = pb[:, LANES:2 * LANES].T.astype(BF16)


def _mod_spec(chunk, row_of_step):
    return lambda d: pl.BlockSpec((1, 1, 1, d), lambda i, *_: (row_of_step(i), chunk, 0, 0))


MOD_SH1, MOD_SC1, MOD_G1, MOD_SH2, MOD_SC2, MOD_G2 = range(6)


def _project_latents(x2, mod, gpre, w_in, tables, qn, kn, seg_q, seg_k, seq):
    t, d = x2.shape
    tm = PROJ_TM
    tpb = seq // tm
    nq = seg_q.shape[0]
    const = lambda shape: pl.BlockSpec(shape, lambda i: (0,) * len(shape))
    batch_of = lambda i: i // tpb
    table = pl.BlockSpec((tm, LANES), lambda i: (i % tpb, 0))
    table_t = pl.BlockSpec((LANES, tm), lambda i: (0, i % tpb))
    k_spec = pl.BlockSpec((tm, LANES), lambda i: (i, 0))
    k_shape = jax.ShapeDtypeStruct((t, LANES), BF16)
    vt_spec = pl.BlockSpec((LANES, tm), lambda i: (0, i))
    vt_shape = jax.ShapeDtypeStruct((LANES, t), BF16)
    qt_spec = pl.BlockSpec((nq, tm), lambda i: (0, i))
    qt_shape = jax.ShapeDtypeStruct((nq, t), BF16)
    return pl.pallas_call(
        _proj_kernel,
        grid=(t // tm,),
        in_specs=[pl.BlockSpec((tm, d), lambda i: (i, 0)),
                  _mod_spec(MOD_SC1, batch_of)(d), _mod_spec(MOD_SH1, batch_of)(d), const((1, d)),
                  const(w_in.shape), table, table, table, table_t, table_t, table_t,
                  const((1, nq)), const((1, LANES)), const(seg_q.shape), const(seg_k.shape)],
        out_specs=[qt_spec, k_spec, vt_spec, qt_spec, k_spec, vt_spec,
                   pl.BlockSpec((1, 1, LANES), lambda i: (i, 0, 0))],
        out_shape=[qt_shape, k_shape, vt_shape, qt_shape, k_shape, vt_shape,
                   jax.ShapeDtypeStruct((t // tm, 1, LANES), F32)],
        compiler_params=_params("arbitrary"),
    )(x2, mod, mod, gpre, w_in, *tables, qn, kn, seg_q, seg_k)


def _project_context(c2, mod, ctx_row, gpre, w_in, kn, seg_k, ctx_len):
    t, d = c2.shape
    ctx_mod = lambda i: ctx_row
    nq = (w_in.shape[1] - 4 * LANES) // 2
    kv = 2 * LANES
    assert nq % kv == 0
    group_kv = lambda g: pl.BlockSpec((d, kv), lambda i: (0, (g * (nq + kv) + nq) // kv))
    const = lambda shape: pl.BlockSpec(shape, lambda i: (0,) * len(shape))
    k_spec = pl.BlockSpec((ctx_len, LANES), lambda i: (i, 0))
    k_shape = jax.ShapeDtypeStruct((t, LANES), BF16)
    vt_spec = pl.BlockSpec((LANES, ctx_len), lambda i: (0, i))
    vt_shape = jax.ShapeDtypeStruct((LANES, t), BF16)
    return pl.pallas_call(
        _ctx_proj_kernel,
        grid=(t // ctx_len,),
        in_specs=[pl.BlockSpec((ctx_len, d), lambda i: (i, 0)),
                  _mod_spec(MOD_SC1, ctx_mod)(d), _mod_spec(MOD_SH1, ctx_mod)(d),
                  const((1, d)), group_kv(0), group_kv(1), const((1, LANES)), const(seg_k.shape)],
        out_specs=[k_spec, vt_spec, k_spec, vt_spec,
                   pl.BlockSpec((1, 1, LANES), lambda i: (i, 0, 0))],
        out_shape=[k_shape, vt_shape, k_shape, vt_shape,
                   jax.ShapeDtypeStruct((t // ctx_len, 1, LANES), F32)],
        compiler_params=_params("arbitrary"),
    )(c2, mod, mod, gpre, w_in, w_in, kn, seg_k)


def _attend_t(w, k, kc, vt, vct, shift=None, bias=None, sink=None):
    st = _dot(k, w)
    sct = _dot(kc, w)
    if bias is not None:
        tq = bias.shape[1]
        st = jnp.concatenate([st[:, c * tq:(c + 1) * tq] + bias
                              for c in range(st.shape[1] // tq)], axis=1)
    if shift is None:
        shift = jnp.maximum(jnp.max(st, axis=0, keepdims=True),
                            jnp.max(sct, axis=0, keepdims=True))
        if sink is not None:
            shift = jnp.maximum(shift, sink)
    pt = jnp.exp2(st - shift)
    pct = jnp.exp2(sct - shift)
    denom = jnp.sum(pt, axis=0, keepdims=True) + jnp.sum(pct, axis=0, keepdims=True)
    if sink is not None:
        denom = denom + jnp.exp2(sink - shift)
    o2 = _dot(vt, pt.astype(BF16)) + _dot(vct, pct.astype(BF16))
    return o2, denom


def _all_heads_t(qt_ref, o_ref, attend):
    tq = qt_ref.shape[1]
    n_kv = LANES // HEAD_DIM
    zeros = jnp.zeros((HEAD_DIM, tq), BF16)
    outs = []
    for h in range(qt_ref.shape[0] // HEAD_DIM):
        g = h // KV_GROUP
        qh = qt_ref[h * HEAD_DIM:(h + 1) * HEAD_DIM, :]
        w = jnp.concatenate([zeros] * g + [qh] + [zeros] * (n_kv - 1 - g), axis=0)
        o2, denom = attend(h, w)
        outs.append(o2[g * HEAD_DIM:(g + 1) * HEAD_DIM, :] / denom)
    o_ref[...] = jnp.concatenate(outs, axis=0).T.astype(BF16)


def _all_heads_fused_t(qt_ref, o_ref, attend):
    tq = qt_ref.shape[1]
    n_heads = qt_ref.shape[0] // HEAD_DIM
    n_kv = LANES // HEAD_DIM
    rows = []
    for g in range(n_kv):
        heads = [qt_ref[h * HEAD_DIM:(h + 1) * HEAD_DIM, :] if h // KV_GROUP == g
                 else jnp.zeros((HEAD_DIM, tq), BF16) for h in range(n_heads)]
        rows.append(jnp.concatenate(heads, axis=1))
    w = jnp.concatenate(rows, axis=0)
    o2, denom = attend(0, w)
    o2 = o2 / denom
    outs = [o2[(h // KV_GROUP) * HEAD_DIM:(h // KV_GROUP + 1) * HEAD_DIM, h * tq:(h + 1) * tq]
            for h in range(n_heads)]
    o_ref[...] = jnp.concatenate(outs, axis=0).T.astype(BF16)


def _attn_b_kernel(bound_ref, qt_ref, k_ref, vt_ref, kc_ref, vct_ref, o_ref):
    k, kc, vt, vct = k_ref[...], kc_ref[...], vt_ref[...], vct_ref[...]
    bound = bound_ref[0]

    @pl.when(bound <= SAFE_SOFTMAX_SHIFT)
    def _():
        _all_heads_fused_t(qt_ref, o_ref, lambda h, w: _attend_t(w, k, kc, vt, vct, shift=bound))

    @pl.when(jnp.logical_not(bound <= SAFE_SOFTMAX_SHIFT))
    def _():
        _all_heads_t(qt_ref, o_ref, lambda h, w: _attend_t(w, k, kc, vt, vct))


def _attn_specs(nq, tq, seq, ctx_len):
    nqb = seq // tq
    return dict(
        qt=pl.BlockSpec((nq, tq), lambda b, i: (0, b * nqb + i)),
        k=pl.BlockSpec((seq, LANES), lambda b, i: (b, 0)),
        vt=pl.BlockSpec((LANES, seq), lambda b, i: (0, b)),
        kc=pl.BlockSpec((ctx_len, LANES), lambda b, i: (b, 0)),
        vct=pl.BlockSpec((LANES, ctx_len), lambda b, i: (0, b)),
        out=pl.BlockSpec((tq, nq), lambda b, i: (b * nqb + i, 0)))


def _attention_b(score_bound, qbt, kb, vbt, kcb, vcbt, batch, seq, ctx_len):
    nq, t = qbt.shape
    tq = ATTN_B_TQ
    sp = _attn_specs(nq, tq, seq, ctx_len)
    return pl.pallas_call(
        _attn_b_kernel,
        grid=(batch, seq // tq),
        in_specs=[pl.BlockSpec(memory_space=pltpu.SMEM),
                  sp["qt"], sp["k"], sp["vt"], sp["kc"], sp["vct"]],
        out_specs=sp["out"],
        out_shape=jax.ShapeDtypeStruct((t, nq), BF16),
        compiler_params=_params("arbitrary", "arbitrary"),
    )(score_bound, qbt, kb, vbt, kcb, vcbt)


def _attn_a_kernel(bound_ref, sink_ref, qt_ref, k_ref, vt_ref, kc_ref, vct_ref, bias_ref, o_ref,
                   *, seq):
    i = pl.program_id(1)
    tq = qt_ref.shape[1]
    n_heads = qt_ref.shape[0] // HEAD_DIM
    win = tq + 2 * WINDOW
    start = pl.multiple_of(jnp.clip(i * tq - WINDOW, 0, seq - win), WINDOW)
    k = k_ref[pl.ds(start, win), :]
    vt = vt_ref[:, pl.ds(start, win)]
    kc, vct = kc_ref[...], vct_ref[...]
    sinks = [sink_ref[h] * LOG2_E for h in range(n_heads)]

    bound = bound_ref[0]
    small = bound <= SAFE_SOFTMAX_SHIFT

    @pl.when(small)
    def _():
        sink_row = jnp.concatenate([jnp.full((1, tq), s, F32) for s in sinks], axis=1)
        _all_heads_fused_t(qt_ref, o_ref, lambda h, w: _attend_t(
            w, k, kc, vt, vct, shift=bound, bias=bias_ref[0], sink=sink_row))

    @pl.when(jnp.logical_not(small))
    def _():
        _all_heads_t(qt_ref, o_ref, lambda h, w: _attend_t(
            w, k, kc, vt, vct, bias=bias_ref[0], sink=sinks[h]))


def _band_bias(tq):
    win = tq + 2 * WINDOW
    r = np.arange(win)[:, None]
    j = np.arange(tq)[None, :]
    tables = [np.where(np.abs(off + r - j) <= WINDOW, 0.0, NEG_INF)
              for off in (0, -WINDOW, -2 * WINDOW)]
    return jnp.asarray(np.stack(tables), F32)


def _attention_a(score_bound, sink, qat, ka, vat, kca, vcat, batch, seq, ctx_len):
    nq, t = qat.shape
    tq = ATTN_A_TQ
    nqb = seq // tq
    win = tq + 2 * WINDOW
    assert nqb >= 2 and tq >= WINDOW
    sp = _attn_specs(nq, tq, seq, ctx_len)
    which = lambda b, i: (jnp.where(i == 0, 0, jnp.where(i == nqb - 1, 2, 1)), 0, 0)
    return pl.pallas_call(
        functools.partial(_attn_a_kernel, seq=seq),
        grid=(batch, nqb),
        in_specs=[pl.BlockSpec(memory_space=pltpu.SMEM), pl.BlockSpec(memory_space=pltpu.SMEM),
                  sp["qt"], sp["k"], sp["vt"], sp["kc"], sp["vct"],
                  pl.BlockSpec((1, win, tq), which)],
        out_specs=sp["out"],
        out_shape=jax.ShapeDtypeStruct((t, nq), BF16),
        compiler_params=_params("arbitrary", "arbitrary"),
    )(score_bound, sink, qat, ka, vat, kca, vcat, _band_bias(tq))


def _out_kernel(oa_ref, ob_ref, x_ref, g1_ref, sc2_ref, sh2_ref, ga_ref, gb_ref, gpost_ref,
                gpre2_ref, woa_ref, wob_ref, wrh_ref, wrl_ref, br_ref,
                x1_ref, h2_ref, rinfo_ref, rt_ref, tcarry_ref, tcnt_ref, cnt_ref, carry_ref):
    step = pl.program_id(0)

    @pl.when(step == 0)
    def _():
        carry_ref[...] = jnp.zeros_like(carry_ref)

    na = _rms(oa_ref[...].astype(F32)) * ga_ref[...]
    nb = _rms(ob_ref[...].astype(F32)) * gb_ref[...]
    ox = _dot(na.astype(BF16), woa_ref[...]) + _dot(nb.astype(BF16), wob_ref[...])
    x1 = x_ref[...] + g1_ref[0, 0] * (_rms(ox) * gpost_ref[...])
    x1_ref[...] = x1
    h2 = _rms(x1) * gpre2_ref[...] * (1.0 + sc2_ref[0, 0]) + sh2_ref[0, 0]
    h_hi, h_lo = _split_bf16(h2)
    h2_ref[...] = h_hi

    logits = (_dot(h_hi, wrh_ref[...]) + _dot(h_lo, wrh_ref[...]) + _dot(h_hi, wrl_ref[...])
              + br_ref[...])
    tm = logits.shape[0]
    lt = logits.T
    row = lax.broadcasted_iota(jnp.int32, lt.shape, 0)
    rowf = row.astype(F32)
    big = jnp.float32(1e9)
    ninf = jnp.float32(-jnp.inf)
    colmax = lambda v: jnp.max(v, axis=0, keepdims=True)
    colmin = lambda v: jnp.min(v, axis=0, keepdims=True)
    colsum = lambda v: jnp.sum(v, axis=0, keepdims=True)

    gmask = (row >= N_EXPERTS) & (row < N_EXPERTS + N_GROUPS)
    lg = jnp.where(gmask, lt, ninf)
    gmax = colmax(lg)
    gidx = colmin(jnp.where(lg == gmax, rowf, big)) - N_EXPERTS
    g_w = 1.0 / colsum(jnp.exp(lg - gmax))
    row_group = (row // EXPERTS_PER_GROUP).astype(F32)
    emask = (row < N_EXPERTS) & (row_group == gidx)
    le = jnp.where(emask, lt, ninf)
    m1 = colmax(le)
    i1 = colmin(jnp.where(le == m1, rowf, big))
    le2 = jnp.where(rowf == i1, ninf, le)
    m2 = colmax(le2)
    i2 = colmin(jnp.where(le2 == m2, rowf, big))
    e2 = jnp.exp(m2 - m1)
    w0 = g_w / (1.0 + e2)
    w1 = g_w * e2 / (1.0 + e2)

    hit1 = rowf == i1
    hit2 = rowf == i2
    onehot = jnp.where(hit1, 1.0, jnp.where(hit2, 1.0, 0.0)).astype(F32)
    r = lax.broadcasted_iota(jnp.int32, (tm, tm), 0)
    c = lax.broadcasted_iota(jnp.int32, (tm, tm), 1)
    earlier = jnp.where(r < c, 1.0, 0.0).astype(BF16)
    within = _dot(onehot.astype(BF16), earlier)
    tile_cnt = jnp.broadcast_to(jnp.sum(onehot, axis=1, keepdims=True), (LANES, LANES))
    er = lax.broadcasted_iota(jnp.int32, (LANES, LANES), 0)
    ec = lax.broadcasted_iota(jnp.int32, (LANES, LANES), 1)
    below = jnp.where(er > ec, 1.0, 0.0).astype(BF16)
    cnt_hi = jnp.floor(tile_cnt * (1.0 / 32.0))
    cnt_lo = tile_cnt - 32.0 * cnt_hi
    run_start = 32.0 * _dot(below, cnt_hi.astype(BF16)) + _dot(below, cnt_lo.astype(BF16))
    local = within + run_start[:, 0:1]
    pos0 = colsum(jnp.where(hit1, local, 0.0))
    pos1 = colsum(jnp.where(hit2, local, 0.0))
    cnt_row = tile_cnt.T[0:1, :]
    tcarry_ref[0] = carry_ref[...]
    tcnt_ref[0] = cnt_row
    carry_ref[...] += cnt_row
    cnt_ref[...] = carry_ref[...]

    fields = jnp.concatenate([i1, i2, pos0, pos1, w0, w1, jnp.zeros((2, tm), F32)], axis=0)
    rt_ref[...] = fields
    rinfo_ref[...] = jnp.concatenate(
        [fields, jnp.zeros((LANES - 8, tm), F32)], axis=0).T


def _out_and_route(oa, ob, x2, mod, ga, gb, gpost, gpre2, w_out, wrh, wrl, br, seq):
    t, d = x2.shape
    tm = OUT_TM
    tpb = seq // tm
    nq = oa.shape[1]
    const = lambda shape: pl.BlockSpec(shape, lambda i: (0,) * len(shape))
    batch_of = lambda i: i // tpb
    rows = lambda n: pl.BlockSpec((tm, n), lambda i: (i, 0))
    per_tile = pl.BlockSpec((1, 1, LANES), lambda i: (i, 0, 0))
    w_half = lambda g: pl.BlockSpec((nq, d), lambda i: (g, 0))
    return pl.pallas_call(
        _out_kernel,
        grid=(t // tm,),
        in_specs=[rows(nq), rows(nq), rows(d),
                  _mod_spec(MOD_G1, batch_of)(d), _mod_spec(MOD_SC2, batch_of)(d),
                  _mod_spec(MOD_SH2, batch_of)(d),
                  const((1, nq)), const((1, nq)), const((1, d)), const((1, d)),
                  w_half(0), w_half(1), const(wrh.shape), const(wrl.shape),
                  const((1, LANES))],
        out_specs=[rows(d), rows(d), rows(LANES), pl.BlockSpec((8, tm), lambda i: (0, i)),
                   per_tile, per_tile, const((1, LANES))],
        out_shape=[jax.ShapeDtypeStruct((t, d), F32), jax.ShapeDtypeStruct((t, d), BF16),
                   jax.ShapeDtypeStruct((t, LANES), F32), jax.ShapeDtypeStruct((8, t), F32),
                   jax.ShapeDtypeStruct((t // tm, 1, LANES), F32),
                   jax.ShapeDtypeStruct((t // tm, 1, LANES), F32),
                   jax.ShapeDtypeStruct((1, LANES), F32)],
        scratch_shapes=[pltpu.VMEM((1, LANES), F32)],
        compiler_params=_params("arbitrary"),
    )(oa, ob, x2, mod, mod, mod, ga, gb, gpost, gpre2, w_out, w_out, wrh, wrl, br)


PACK_ROWS = 8
ROW_DTYPE = F32


def _pack_rows(ref, x):
    n = x.shape[0]
    for c in range(PACK_ROWS):
        ref[pl.ds(c, n, stride=PACK_ROWS), :] = x[:, c * LANES:(c + 1) * LANES]


def _unpack_rows(ref):
    n = ref.shape[0] // PACK_ROWS
    return jnp.concatenate(
        [ref[pl.ds(c, n, stride=PACK_ROWS), :].astype(BF16) for c in range(PACK_ROWS)], axis=1)


def _for_each_run_piece(rdst_ref, rlen_ref, tile, max_len, fn):
    n_bits = max_len.bit_length()

    def run(e, local):
        length = rlen_ref[tile * N_EXPERTS + e]
        dst = rdst_ref[tile * N_EXPERTS + e]
        for b in range(n_bits):
            size = 1 << b

            @pl.when(((length >> b) & 1) == 1)
            def _():
                done = length & (size - 1)
                fn(local + done, dst + done, size)
        return local + length

    lax.fori_loop(0, N_EXPERTS, run, 0)


def _token_rows(ref, row0, n_rows):
    start = row0 * PACK_ROWS
    if not isinstance(start, int):
        start = pl.multiple_of(start, PACK_ROWS)
    return ref.at[pl.ds(start, n_rows * PACK_ROWS)]


def _dispatch_kernel(rdst_ref, rlen_ref, h_ref, rt_ref, xs_ref, sorted_ref, sem):
    k = pl.program_id(0)
    nk = pl.num_programs(0)
    tm = h_ref.shape[0]
    rows = 2 * tm
    slot = k % 2

    def wait_slot(s):
        pltpu.make_async_copy(sorted_ref.at[s], _token_rows(xs_ref, 0, rows), sem.at[s]).wait()

    @pl.when(k >= 2)
    def _():
        wait_slot(slot)

    pos0 = rt_ref[2:3, :]
    pos1 = rt_ref[3:4, :]
    r = lax.broadcasted_iota(jnp.int32, (rows, tm), 0).astype(F32)
    perm = jnp.where((r == pos0) | (r == pos1), 1.0, 0.0).astype(BF16)
    srt = _dot(perm, h_ref[...].astype(BF16))
    buf = sorted_ref.at[slot]
    _pack_rows(buf, srt)

    def copy_piece(local, dst, size):
        pltpu.make_async_copy(_token_rows(buf, local, size), _token_rows(xs_ref, dst, size),
                              sem.at[slot]).start()

    _for_each_run_piece(rdst_ref, rlen_ref, k, tm, copy_piece)

    @pl.when(k == nk - 1)
    def _():
        wait_slot(slot)

        @pl.when(nk >= 2)
        def _():
            wait_slot(1 - slot)


def _dispatch(run_dst, run_len, h2, rt):
    t, d = h2.shape
    assert d == PACK_ROWS * LANES
    tm = MOVE_TM
    return pl.pallas_call(
        _dispatch_kernel,
        grid_spec=pltpu.PrefetchScalarGridSpec(
            num_scalar_prefetch=2,
            grid=(t // tm,),
            in_specs=[pl.BlockSpec((tm, d), lambda i, *_: (i, 0)),
                      pl.BlockSpec((8, tm), lambda i, *_: (0, i))],
            out_specs=pl.BlockSpec(memory_space=pl.ANY),
            scratch_shapes=[pltpu.VMEM((2, 2 * tm * PACK_ROWS, LANES), ROW_DTYPE),
                            pltpu.SemaphoreType.DMA((2,))]),
        out_shape=jax.ShapeDtypeStruct((2 * t * PACK_ROWS, LANES), ROW_DTYPE),
        compiler_params=_params("arbitrary"),
    )(run_dst, run_len, h2, rt)


def _expert_kernel(vt_ref, ve_ref, va_ref, vb_ref, nv_ref, xs_ref, wg_ref, wu_ref, wd_ref, ys_ref,
                   wg_bf, wu_bf, wd_bf):
    v = pl.program_id(0)
    valid = v < nv_ref[0]
    prev = jnp.maximum(v - 1, 0)
    new_expert = (v == 0) | (ve_ref[v] != ve_ref[prev])
    new_tile = (v == 0) | (vt_ref[v] != vt_ref[prev])

    @pl.when(valid & new_expert)
    def _():
        wg_bf[...] = wg_ref[0].astype(BF16)
        wu_bf[...] = wu_ref[0].astype(BF16)
        wd_bf[...] = wd_ref[0].astype(BF16)

    def expert_rows():
        xb = _unpack_rows(xs_ref)
        gate = _dot(xb, wg_bf[...])
        up = _dot(xb, wu_bf[...])
        act = gate * jax.nn.sigmoid(gate) * up
        return _dot(act.astype(BF16), wd_bf[...])

    @pl.when(valid & new_tile)
    def _():
        _pack_rows(ys_ref, expert_rows())

    @pl.when(valid & jnp.logical_not(new_tile))
    def _():
        y = expert_rows()
        te = y.shape[0]
        row = lax.broadcasted_iota(jnp.int32, (te, 1), 0)
        mine = (row >= va_ref[v]) & (row < vb_ref[v])
        for c in range(PACK_ROWS):
            rows = pl.ds(c, te, stride=PACK_ROWS)
            ys_ref[rows, :] = jnp.where(mine, y[:, c * LANES:(c + 1) * LANES], ys_ref[rows, :])


def _expert_mlp(visit_tile, visit_expert, visit_lo, visit_hi, n_visits, xs, w_gate, w_up, w_down):
    te = EXPERT_TE
    d, ff = w_gate.shape[1:]
    blk = (te * PACK_ROWS, LANES)
    tile = lambda v, vt, *_: (vt[v], 0)
    wsel = lambda v, vt, ve, *_: (ve[v], 0, 0)
    return pl.pallas_call(
        _expert_kernel,
        grid_spec=pltpu.PrefetchScalarGridSpec(
            num_scalar_prefetch=5,
            grid=(visit_tile.shape[0],),
            in_specs=[pl.BlockSpec(blk, tile),
                      pl.BlockSpec((1, d, ff), wsel), pl.BlockSpec((1, d, ff), wsel),
                      pl.BlockSpec((1, ff, d), wsel)],
            out_specs=pl.BlockSpec(blk, tile),
            scratch_shapes=[pltpu.VMEM((d, ff), BF16), pltpu.VMEM((d, ff), BF16),
                            pltpu.VMEM((ff, d), BF16)]),
        out_shape=jax.ShapeDtypeStruct(xs.shape, ROW_DTYPE),
        compiler_params=_params("arbitrary"),
    )(visit_tile, visit_expert, visit_lo, visit_hi, n_visits, xs, w_gate, w_up, w_down)


def _combine_kernel(rdst_ref, rlen_ref, x1_ref, rinfo_ref, g2_ref, gpost_ref, ys_ref, o_ref,
                    gath_ref, sem):
    k = pl.program_id(0)
    nk = pl.num_programs(0)
    tm = x1_ref.shape[0]
    rows = 2 * tm
    slot = k % 2

    def gather_runs(tile, s):
        buf = gath_ref.at[s]

        def copy_piece(local, src, size):
            pltpu.make_async_copy(_token_rows(ys_ref, src, size), _token_rows(buf, local, size),
                                  sem.at[s]).start()

        _for_each_run_piece(rdst_ref, rlen_ref, tile, tm, copy_piece)

    @pl.when(k == 0)
    def _():
        gather_runs(0, 0)

    @pl.when(k + 1 < nk)
    def _():
        gather_runs(k + 1, 1 - slot)

    buf = gath_ref.at[slot]
    pltpu.make_async_copy(_token_rows(ys_ref, 0, rows), buf, sem.at[slot]).wait()
    g = _unpack_rows(buf)
    info = rinfo_ref[...]
    col = lax.broadcasted_iota(jnp.int32, (tm, rows), 1).astype(F32)
    pick = jnp.where(col == info[:, 2:3], info[:, 4:5],
                     jnp.where(col == info[:, 3:4], info[:, 5:6], 0.0)).astype(BF16)
    fx = _dot(pick, g)
    o_ref[...] = x1_ref[...] + g2_ref[0, 0] * (_rms(fx) * gpost_ref[...])


def _combine(run_dst, run_len, x1, rinfo, mod, gpost, ys, seq):
    t, d = x1.shape
    tm = MOVE_TM
    tpb = seq // tm
    batch_of = lambda i: i // tpb
    return pl.pallas_call(
        _combine_kernel,
        grid_spec=pltpu.PrefetchScalarGridSpec(
            num_scalar_prefetch=2,
            grid=(t // tm,),
            in_specs=[pl.BlockSpec((tm, d), lambda i, *_: (i, 0)),
                      pl.BlockSpec((tm, LANES), lambda i, *_: (i, 0)),
                      _mod_spec(MOD_G2, batch_of)(d),
                      pl.BlockSpec((1, d), lambda i, *_: (0, 0)),
                      pl.BlockSpec(memory_space=pl.ANY)],
            out_specs=pl.BlockSpec((tm, d), lambda i, *_: (i, 0)),
            scratch_shapes=[pltpu.VMEM((2, 2 * tm * PACK_ROWS, LANES), ROW_DTYPE),
                            pltpu.SemaphoreType.DMA((2,))]),
        out_shape=jax.ShapeDtypeStruct((t, d), F32),
        compiler_params=_params("arbitrary"),
    )(run_dst, run_len, x1, rinfo, mod, gpost, ys)


def _rope_tables(seq):
    pos = np.arange(seq)
    row = (pos // GRID_W).astype(np.float32)
    col = (pos % GRID_W).astype(np.float32)
    axis_dim = HEAD_DIM // 2
    inv_freq = (ROPE_THETA ** (-np.arange(0, axis_dim, 2, dtype=np.float32) / axis_dim)).astype(
        np.float32)
    ang = np.concatenate([row[:, None] * inv_freq, col[:, None] * inv_freq], axis=-1)
    pair = (np.arange(LANES) % HEAD_DIM) // 2
    cos = np.cos(ang)[:, pair]
    sin = np.sin(ang)[:, pair]
    even = (np.arange(LANES) % 2) == 0
    tables = (cos, np.where(even, -sin, 0.0), np.where(even, 0.0, sin))
    tables = tables + tuple(tb.T for tb in tables)
    return tuple(jnp.asarray(tb, F32) for tb in tables)


def _segment_ones(n):
    seg = np.arange(n) // HEAD_DIM
    return jnp.asarray(seg[:, None] == seg[None, :], BF16)


def kernel(x, c, ctx, c_ctx, w_mod, b_mod, attn_pre_norm, attn_post_norm, w_in, a_sink,
           b_q_norm, b_k_norm, a_out_norm, b_out_norm, w_out, ffn_pre_norm, ffn_post_norm,
           w_group, b_group, w_router, b_router, w_gate, w_up, w_down):
    batch, seq, d = x.shape
    ctx_len = ctx.shape[1]
    assert w_mod.shape[0] == 1, "single-layer stack only (context stream is never updated)"
    assert seq % ATTN_A_TQ == 0 and seq >= ATTN_A_TQ + 2 * WINDOW
    assert seq % PROJ_TM == 0 and seq % ATTN_B_TQ == 0 and seq % OUT_TM == 0 and seq % MOVE_TM == 0
    t = batch * seq
    nq = d // 2
    nkv = nq // KV_GROUP
    assert nkv == LANES and w_in.shape[2] == 2 * nq + 4 * nkv

    cc = jnp.concatenate([c, c_ctx[None, :], jnp.zeros((16 - batch - 1, d), F32)], axis=0)
    mod = _modulation(cc, w_mod[0], b_mod[0]).reshape(cc.shape[0], 6, 1, d)

    x2 = x.reshape(t, d)
    c2 = ctx.reshape(batch * ctx_len, d)
    gpre = attn_pre_norm[0].reshape(1, d)
    w_in_bf = w_in[0].astype(BF16)
    qn = jnp.tile(b_q_norm[0], nq // HEAD_DIM).reshape(1, nq)
    kn = jnp.tile(b_k_norm[0], nkv // HEAD_DIM).reshape(1, nkv)
    seg_q, seg_k = _segment_ones(nq), _segment_ones(nkv)
    qat, ka, vat, qbt, kb, vbt, stats = _project_latents(
        x2, mod, gpre, w_in_bf, _rope_tables(seq), qn, kn, seg_q, seg_k, seq)
    kca, vcat, kcb, vcbt, ctx_stats = _project_context(
        c2, mod, batch, gpre, w_in_bf, kn, seg_k, ctx_len)

    q_sq = jnp.max(stats[:, 0, 0])
    k_sq = jnp.maximum(jnp.max(stats[:, 0, 1]), jnp.max(ctx_stats[:, 0, 1]))
    bound_a = jnp.maximum(1.01 * jnp.sqrt(q_sq * k_sq), jnp.max(a_sink[0]) * LOG2_E).reshape(1)
    oa = _attention_a(bound_a, a_sink[0], qat, ka, vat, kca, vcat, batch, seq, ctx_len)
    score_bound = (1.01 * HEAD_DIM ** 0.5 * LOG2_E
                   * jnp.max(jnp.abs(b_q_norm[0])) * jnp.max(jnp.abs(b_k_norm[0]))).reshape(1)
    ob = _attention_b(score_bound, qbt, kb, vbt, kcb, vcbt, batch, seq, ctx_len)

    w_out_bf = w_out[0].astype(BF16)
    lane_pad = LANES - N_EXPERTS - N_GROUPS
    w_r = jnp.pad(jnp.concatenate([w_router[0], w_group[0]], axis=1), ((0, 0), (0, lane_pad)))
    w_r_hi = w_r.astype(BF16)
    w_r_lo = (w_r - w_r_hi.astype(F32)).astype(BF16)
    b_r = jnp.pad(jnp.concatenate([b_router[0], b_group[0]]), (0, lane_pad)).reshape(1, LANES)
    x1, h2, rinfo, rt, tcarry, tcnt, counts = _out_and_route(
        oa, ob, x2, mod, a_out_norm[0].reshape(1, nq), b_out_norm[0].reshape(1, nq),
        attn_post_norm[0].reshape(1, d), ffn_pre_norm[0].reshape(1, d),
        w_out_bf, w_r_hi, w_r_lo, b_r, seq)

    te = EXPERT_TE
    assert (2 * t) % te == 0
    cnt = counts[0, :N_EXPERTS].astype(jnp.int32)
    ends = jnp.cumsum(cnt)
    starts = ends - cnt
    run_dst = (starts[None, :] + tcarry[:, 0, :N_EXPERTS].astype(jnp.int32)).reshape(-1)
    run_len = tcnt[:, 0, :N_EXPERTS].astype(jnp.int32).reshape(-1)
    first_tile = starts // te
    n_vis = jnp.where(cnt > 0, (ends - 1) // te - first_tile + 1, 0)
    vis_end = jnp.cumsum(n_vis)
    n_visits = vis_end[-1]
    v = jnp.minimum(jnp.arange(2 * t // te + N_EXPERTS, dtype=jnp.int32), n_visits - 1)
    v_expert = jnp.sum(vis_end[None, :] <= v[:, None], axis=1).astype(jnp.int32)
    pick = (v_expert[:, None] == jnp.arange(N_EXPERTS)[None, :]).astype(jnp.int32)
    of_expert = lambda table: jnp.sum(pick * table[None, :], axis=1)
    v_tile = of_expert(first_tile) + v - of_expert(vis_end - n_vis)
    v_lo = jnp.maximum(of_expert(starts) - v_tile * te, 0)
    v_hi = jnp.minimum(of_expert(ends) - v_tile * te, te)

    xs = _dispatch(run_dst, run_len, h2, rt)
    ys = _expert_mlp(v_tile, v_expert, v_lo, v_hi, n_visits.reshape(1), xs,
                     w_gate[0], w_up[0], w_down[0])
    out = _combine(run_dst, run_len, x1, rinfo, mod, ffn_post_norm[0].reshape(1, d), ys, seq)
    return out.reshape(batch, seq, d)
```

```python
import functools

import jax
import jax.numpy as jnp
import numpy as np
from jax import lax
from jax.experimental import pallas as pl
from jax.experimental.pallas import tpu as pltpu

F32 = jnp.float32
BF16 = jnp.bfloat16

GRID_W = 64
HEAD_DIM = 64
KV_GROUP = 4
WINDOW = 128
ROPE_THETA = 10000.0
N_GROUPS = 4
EXPERTS_PER_GROUP = 8
N_EXPERTS = N_GROUPS * EXPERTS_PER_GROUP
EPS = 1e-6
NEG_INF = -1e30
LOG2_E = 1.4426950408889634
SAFE_SOFTMAX_SHIFT = 40.0

LANES = 128
V7X_VMEM_LIMIT = 56 * 1024 * 1024

PROJ_TM = 512
ATTN_A_TQ = 256
ATTN_B_TQ = 256
OUT_TM = 512
EXPERT_TE = 512
MOVE_TM = OUT_TM


def _params(*sem):
    return pltpu.CompilerParams(dimension_semantics=sem, vmem_limit_bytes=V7X_VMEM_LIMIT)


def _dot(a, b):
    return jnp.dot(a, b, preferred_element_type=F32)


def _dot_nt(a, b):
    return lax.dot_general(a, b, (((1,), (1,)), ((), ())), preferred_element_type=F32)


def _rms(x):
    return x * lax.rsqrt(jnp.mean(x * x, axis=-1, keepdims=True) + EPS)


def _split_bf16(x):
    hi = x.astype(BF16)
    lo = (x - hi.astype(F32)).astype(BF16)
    return hi, lo


def _mod_kernel(c_ref, w_ref, b_ref, o_ref):
    cc = c_ref[...]
    s = cc * jax.nn.sigmoid(cc)
    s_hi, s_lo = _split_bf16(s)
    w_hi, w_lo = _split_bf16(w_ref[...])
    o_ref[...] = _dot(s_hi, w_hi) + _dot(s_lo, w_hi) + _dot(s_hi, w_lo) + b_ref[...]


def _modulation(cc, w_mod, b_mod):
    rows, d = cc.shape
    n = w_mod.shape[1]
    bn = 1024
    return pl.pallas_call(
        _mod_kernel,
        grid=(n // bn,),
        in_specs=[pl.BlockSpec((rows, d), lambda i: (0, 0)),
                  pl.BlockSpec((d, bn), lambda i: (0, i)),
                  pl.BlockSpec((1, bn), lambda i: (0, i))],
        out_specs=pl.BlockSpec((rows, bn), lambda i: (0, i)),
        out_shape=jax.ShapeDtypeStruct((rows, n), F32),
        compiler_params=_params("arbitrary"),
    )(cc, w_mod, b_mod.reshape(1, n))


def _rope(x, cos, sin_a, sin_b):
    return x * cos + pltpu.roll(x, LANES - 1, 1) * sin_a + pltpu.roll(x, 1, 1) * sin_b


def _head_norm(x, seg_ref, gain):
    ss = _dot((x * x).astype(BF16), seg_ref[...])
    return x * lax.rsqrt(ss * (1.0 / HEAD_DIM) + EPS) * gain


def _max_head_sq_norm(x, seg_ref):
    ss = _dot((x * x).astype(BF16), seg_ref[...])
    return jnp.max(jnp.max(ss, axis=1, keepdims=True), axis=0, keepdims=True)


def _norm_stats(q_sq, k_sq):
    lane = lax.broadcasted_iota(jnp.int32, (1, LANES), 1)
    zero = jnp.zeros((1, LANES), F32)
    return jnp.where(lane == 0, q_sq, zero) + jnp.where(lane == 1, k_sq, zero)


def _rope_t(xt, cos_t, sin_a_t, sin_b_t):
    return (xt * cos_t + pltpu.roll(xt, LANES - 1, 0) * sin_a_t
            + pltpu.roll(xt, 1, 0) * sin_b_t)


def _proj_kernel(x_ref, sc_ref, sh_ref, gpre_ref, w_ref, cos_ref, sa_ref, sb_ref,
                 cos_t_ref, sa_t_ref, sb_t_ref, qn_ref, kn_ref, seg_q_ref, seg_k_ref,
                 qat_ref, ka_ref, vat_ref, qbt_ref, kb_ref, vbt_ref, stats_ref):
    h = _rms(x_ref[...]) * gpre_ref[...] * (1.0 + sc_ref[0, 0]) + sh_ref[0, 0]
    p = _dot(h.astype(BF16), w_ref[...])
    cos, sa, sb = cos_ref[...], sa_ref[...], sb_ref[...]
    q_scale = HEAD_DIM ** -0.5 * LOG2_E
    cos_t, sa_t, sb_t = cos_t_ref[...] * q_scale, sa_t_ref[...] * q_scale, sb_t_ref[...] * q_scale
    nq = qat_ref.shape[0]
    for c in range(nq // LANES):
        qat_ref[c * LANES:(c + 1) * LANES, :] = _rope_t(
            p[:, c * LANES:(c + 1) * LANES].T, cos_t, sa_t, sb_t).astype(BF16)
    o = nq
    ka_ref[...] = _rope(p[:, o:o + LANES], cos, sa, sb).astype(BF16)
    vat_ref[...] = p[:, o + LANES:o + 2 * LANES].T.astype(BF16)
    stats_ref[0] = _norm_stats(_max_head_sq_norm(p[:, 0:nq], seg_q_ref) * (q_scale * q_scale),
                               _max_head_sq_norm(p[:, o:o + LANES], seg_k_ref))
    o += 2 * LANES
    qb = _head_norm(p[:, o:o + nq], seg_q_ref, qn_ref[...])
    for c in range(nq // LANES):
        qbt_ref[c * LANES:(c + 1) * LANES, :] = _rope_t(
            qb[:, c * LANES:(c + 1) * LANES].T, cos_t, sa_t, sb_t).astype(BF16)
    o += nq
    kb = _head_norm(p[:, o:o + LANES], seg_k_ref, kn_ref[...])
    kb_ref[...] = _rope(kb, cos, sa, sb).astype(BF16)
    vbt_ref[...] = p[:, o + LANES:o + 2 * LANES].T.astype(BF16)


def _ctx_proj_kernel(x_ref, sc_ref, sh_ref, gpre_ref, wa_ref, wb_ref, kn_ref, seg_k_ref,
                     ka_ref, vat_ref, kb_ref, vbt_ref, stats_ref):
    h = (_rms(x_ref[...]) * gpre_ref[...] * (1.0 + sc_ref[0, 0]) + sh_ref[0, 0]).astype(BF16)
    pa = _dot(h, wa_ref[...])
    pb = _dot(h, wb_ref[...])
    stats_ref[0] = _norm_stats(0.0, _max_head_sq_norm(pa[:, 0:LANES], seg_k_ref))
    ka_ref[...] = pa[:, 0:LANES].astype(BF16)
    vat_ref[...] = pa[:, LANES:2 * LANES].T.astype(BF16)
    kb_ref[...] = _head_norm(pb[:, 0:LANES], seg_k_ref, kn_ref[...]).astype(BF16)
    vbt_ref[...] = pb[:, LANES:2 * LANES].T.astype(BF16)


def _mod_spec(chunk, row_of_step):
    return lambda d: pl.BlockSpec((1, 1, 1, d), lambda i, *_: (row_of_step(i), chunk, 0, 0))


MOD_SH1, MOD_SC1, MOD_G1, MOD_SH2, MOD_SC2, MOD_G2 = range(6)


def _project_latents(x2, mod, gpre, w_in, tables, qn, kn, seg_q, seg_k, seq):
    t, d = x2.shape
    tm = PROJ_TM
    tpb = seq // tm
    nq = seg_q.shape[0]
    const = lambda shape: pl.BlockSpec(shape, lambda i: (0,) * len(shape))
    batch_of = lambda i: i // tpb
    table = pl.BlockSpec((tm, LANES), lambda i: (i % tpb, 0))
    table_t = pl.BlockSpec((LANES, tm), lambda i: (0, i % tpb))
    k_spec = pl.BlockSpec((tm, LANES), lambda i: (i, 0))
    k_shape = jax.ShapeDtypeStruct((t, LANES), BF16)
    vt_spec = pl.BlockSpec((LANES, tm), lambda i: (0, i))
    vt_shape = jax.ShapeDtypeStruct((LANES, t), BF16)
    qt_spec = pl.BlockSpec((nq, tm), lambda i: (0, i))
    qt_shape = jax.ShapeDtypeStruct((nq, t), BF16)
    return pl.pallas_call(
        _proj_kernel,
        grid=(t // tm,),
        in_specs=[pl.BlockSpec((tm, d), lambda i: (i, 0)),
                  _mod_spec(MOD_SC1, batch_of)(d), _mod_spec(MOD_SH1, batch_of)(d), const((1, d)),
                  const(w_in.shape), table, table, table, table_t, table_t, table_t,
                  const((1, nq)), const((1, LANES)), const(seg_q.shape), const(seg_k.shape)],
        out_specs=[qt_spec, k_spec, vt_spec, qt_spec, k_spec, vt_spec,
                   pl.BlockSpec((1, 1, LANES), lambda i: (i, 0, 0))],
        out_shape=[qt_shape, k_shape, vt_shape, qt_shape, k_shape, vt_shape,
                   jax.ShapeDtypeStruct((t // tm, 1, LANES), F32)],
        compiler_params=_params("arbitrary"),
    )(x2, mod, mod, gpre, w_in, *tables, qn, kn, seg_q, seg_k)


def _project_context(c2, mod, ctx_row, gpre, w_in, kn, seg_k, ctx_len):
    t, d = c2.shape
    ctx_mod = lambda i: ctx_row
    nq = (w_in.shape[1] - 4 * LANES) // 2
    kv = 2 * LANES
    assert nq % kv == 0
    group_kv = lambda g: pl.BlockSpec((d, kv), lambda i: (0, (g * (nq + kv) + nq) // kv))
    const = lambda shape: pl.BlockSpec(shape, lambda i: (0,) * len(shape))
    k_spec = pl.BlockSpec((ctx_len, LANES), lambda i: (i, 0))
    k_shape = jax.ShapeDtypeStruct((t, LANES), BF16)
    vt_spec = pl.BlockSpec((LANES, ctx_len), lambda i: (0, i))
    vt_shape = jax.ShapeDtypeStruct((LANES, t), BF16)
    return pl.pallas_call(
        _ctx_proj_kernel,
        grid=(t // ctx_len,),
        in_specs=[pl.BlockSpec((ctx_len, d), lambda i: (i, 0)),
                  _mod_spec(MOD_SC1, ctx_mod)(d), _mod_spec(MOD_SH1, ctx_mod)(d),
                  const((1, d)), group_kv(0), group_kv(1), const((1, LANES)), const(seg_k.shape)],
        out_specs=[k_spec, vt_spec, k_spec, vt_spec,
                   pl.BlockSpec((1, 1, LANES), lambda i: (i, 0, 0))],
        out_shape=[k_shape, vt_shape, k_shape, vt_shape,
                   jax.ShapeDtypeStruct((t // ctx_len, 1, LANES), F32)],
        compiler_params=_params("arbitrary"),
    )(c2, mod, mod, gpre, w_in, w_in, kn, seg_k)


def _attend_t(w, k, kc, vt, vct, kv_of_cols, shift=None, bias=None, sink=None):
    st = _dot(k, w)
    sct = _dot(kc, w)
    if bias is not None:
        tq = bias.shape[1]
        st = jnp.concatenate([st[:, c * tq:(c + 1) * tq] + bias
                              for c in range(st.shape[1] // tq)], axis=1)
    if shift is None:
        shift = jnp.maximum(jnp.max(st, axis=0, keepdims=True),
                            jnp.max(sct, axis=0, keepdims=True))
        if sink is not None:
            shift = jnp.maximum(shift, sink)
    pt = jnp.exp2(st - shift)
    pct = jnp.exp2(sct - shift)
    denom = jnp.sum(pt, axis=0, keepdims=True) + jnp.sum(pct, axis=0, keepdims=True)
    if sink is not None:
        denom = denom + jnp.exp2(sink - shift)
    pt, pct = pt.astype(BF16), pct.astype(BF16)
    seg = pt.shape[1] // len(kv_of_cols)
    outs = []
    for s, g in enumerate(kv_of_cols):
        rows = slice(g * HEAD_DIM, (g + 1) * HEAD_DIM)
        cols = slice(s * seg, (s + 1) * seg)
        outs.append(_dot(vt[rows, :], pt[:, cols]) + _dot(vct[rows, :], pct[:, cols]))
    return jnp.concatenate(outs, axis=1), denom


def _all_heads_t(qt_ref, o_ref, attend):
    tq = qt_ref.shape[1]
    n_kv = LANES // HEAD_DIM
    zeros = jnp.zeros((HEAD_DIM, tq), BF16)
    outs = []
    for h in range(qt_ref.shape[0] // HEAD_DIM):
        g = h // KV_GROUP
        qh = qt_ref[h * HEAD_DIM:(h + 1) * HEAD_DIM, :]
        w = jnp.concatenate([zeros] * g + [qh] + [zeros] * (n_kv - 1 - g), axis=0)
        o, denom = attend(h, w, (g,))
        outs.append(o / denom)
    o_ref[...] = jnp.concatenate(outs, axis=0).T.astype(BF16)


def _all_heads_fused_t(qt_ref, o_ref, attend):
    tq = qt_ref.shape[1]
    n_heads = qt_ref.shape[0] // HEAD_DIM
    n_kv = LANES // HEAD_DIM
    rows = []
    for g in range(n_kv):
        heads = [qt_ref[h * HEAD_DIM:(h + 1) * HEAD_DIM, :] if h // KV_GROUP == g
                 else jnp.zeros((HEAD_DIM, tq), BF16) for h in range(n_heads)]
        rows.append(jnp.concatenate(heads, axis=1))
    w = jnp.concatenate(rows, axis=0)
    o, denom = attend(0, w, tuple(range(n_kv)))
    o = o / denom
    outs = [o[:, h * tq:(h + 1) * tq] for h in range(n_heads)]
    o_ref[...] = jnp.concatenate(outs, axis=0).T.astype(BF16)


def _attn_b_kernel(bound_ref, qt_ref, k_ref, vt_ref, kc_ref, vct_ref, o_ref):
    k, kc, vt, vct = k_ref[...], kc_ref[...], vt_ref[...], vct_ref[...]
    bound = bound_ref[0]

    @pl.when(bound <= SAFE_SOFTMAX_SHIFT)
    def _():
        _all_heads_fused_t(qt_ref, o_ref,
                           lambda h, w, kv: _attend_t(w, k, kc, vt, vct, kv, shift=bound))

    @pl.when(jnp.logical_not(bound <= SAFE_SOFTMAX_SHIFT))
    def _():
        _all_heads_t(qt_ref, o_ref, lambda h, w, kv: _attend_t(w, k, kc, vt, vct, kv))


def _attn_specs(nq, tq, seq, ctx_len):
    nqb = seq // tq
    return dict(
        qt=pl.BlockSpec((nq, tq), lambda b, i: (0, b * nqb + i)),
        k=pl.BlockSpec((seq, LANES), lambda b, i: (b, 0)),
        vt=pl.BlockSpec((LANES, seq), lambda b, i: (0, b)),
        kc=pl.BlockSpec((ctx_len, LANES), lambda b, i: (b, 0)),
        vct=pl.BlockSpec((LANES, ctx_len), lambda b, i: (0, b)),
        out=pl.BlockSpec((tq, nq), lambda b, i: (b * nqb + i, 0)))


def _attention_b(score_bound, qbt, kb, vbt, kcb, vcbt, batch, seq, ctx_len):
    nq, t = qbt.shape
    tq = ATTN_B_TQ
    sp = _attn_specs(nq, tq, seq, ctx_len)
    return pl.pallas_call(
        _attn_b_kernel,
        grid=(batch, seq // tq),
        in_specs=[pl.BlockSpec(memory_space=pltpu.SMEM),
                  sp["qt"], sp["k"], sp["vt"], sp["kc"], sp["vct"]],
        out_specs=sp["out"],
        out_shape=jax.ShapeDtypeStruct((t, nq), BF16),
        compiler_params=_params("arbitrary", "arbitrary"),
    )(score_bound, qbt, kb, vbt, kcb, vcbt)


def _attn_a_kernel(bound_ref, sink_ref, qt_ref, k_ref, vt_ref, kc_ref, vct_ref, bias_ref, o_ref,
                   *, seq):
    i = pl.program_id(1)
    tq = qt_ref.shape[1]
    n_heads = qt_ref.shape[0] // HEAD_DIM
    win = tq + 2 * WINDOW
    start = pl.multiple_of(jnp.clip(i * tq - WINDOW, 0, seq - win), WINDOW)
    k = k_ref[pl.ds(start, win), :]
    vt = vt_ref[:, pl.ds(start, win)]
    kc, vct = kc_ref[...], vct_ref[...]
    sinks = [sink_ref[h] * LOG2_E for h in range(n_heads)]

    bound = bound_ref[0]
    small = bound <= SAFE_SOFTMAX_SHIFT

    @pl.when(small)
    def _():
        sink_row = jnp.concatenate([jnp.full((1, tq), s, F32) for s in sinks], axis=1)
        _all_heads_fused_t(qt_ref, o_ref, lambda h, w, kv: _attend_t(
            w, k, kc, vt, vct, kv, shift=bound, bias=bias_ref[0], sink=sink_row))

    @pl.when(jnp.logical_not(small))
    def _():
        _all_heads_t(qt_ref, o_ref, lambda h, w, kv: _attend_t(
            w, k, kc, vt, vct, kv, bias=bias_ref[0], sink=sinks[h]))


def _band_bias(tq):
    win = tq + 2 * WINDOW
    r = np.arange(win)[:, None]
    j = np.arange(tq)[None, :]
    tables = [np.where(np.abs(off + r - j) <= WINDOW, 0.0, NEG_INF)
              for off in (0, -WINDOW, -2 * WINDOW)]
    return jnp.asarray(np.stack(tables), F32)


def _attention_a(score_bound, sink, qat, ka, vat, kca, vcat, batch, seq, ctx_len):
    nq, t = qat.shape
    tq = ATTN_A_TQ
    nqb = seq // tq
    win = tq + 2 * WINDOW
    assert nqb >= 2 and tq >= WINDOW
    sp = _attn_specs(nq, tq, seq, ctx_len)
    which = lambda b, i: (jnp.where(i == 0, 0, jnp.where(i == nqb - 1, 2, 1)), 0, 0)
    return pl.pallas_call(
        functools.partial(_attn_a_kernel, seq=seq),
        grid=(batch, nqb),
        in_specs=[pl.BlockSpec(memory_space=pltpu.SMEM), pl.BlockSpec(memory_space=pltpu.SMEM),
                  sp["qt"], sp["k"], sp["vt"], sp["kc"], sp["vct"],
                  pl.BlockSpec((1, win, tq), which)],
        out_specs=sp["out"],
        out_shape=jax.ShapeDtypeStruct((t, nq), BF16),
        compiler_params=_params("arbitrary", "arbitrary"),
    )(score_bound, sink, qat, ka, vat, kca, vcat, _band_bias(tq))


def _out_kernel(oa_ref, ob_ref, x_ref, g1_ref, sc2_ref, sh2_ref, ga_ref, gb_ref, gpost_ref,
                gpre2_ref, woa_ref, wob_ref, wrh_ref, wrl_ref, br_ref,
                x1_ref, h2_ref, rinfo_ref, rt_ref, tcarry_ref, tcnt_ref, cnt_ref, carry_ref):
    step = pl.program_id(0)

    @pl.when(step == 0)
    def _():
        carry_ref[...] = jnp.zeros_like(carry_ref)

    na = _rms(oa_ref[...].astype(F32)) * ga_ref[...]
    nb = _rms(ob_ref[...].astype(F32)) * gb_ref[...]
    ox = _dot(na.astype(BF16), woa_ref[...]) + _dot(nb.astype(BF16), wob_ref[...])
    x1 = x_ref[...] + g1_ref[0, 0] * (_rms(ox) * gpost_ref[...])
    x1_ref[...] = x1
    h2 = _rms(x1) * gpre2_ref[...] * (1.0 + sc2_ref[0, 0]) + sh2_ref[0, 0]
    h_hi, h_lo = _split_bf16(h2)
    h2_ref[...] = h_hi

    logits = (_dot(h_hi, wrh_ref[...]) + _dot(h_lo, wrh_ref[...]) + _dot(h_hi, wrl_ref[...])
              + br_ref[...])
    tm = logits.shape[0]
    lt = logits.T
    row = lax.broadcasted_iota(jnp.int32, lt.shape, 0)
    rowf = row.astype(F32)
    big = jnp.float32(1e9)
    ninf = jnp.float32(-jnp.inf)
    colmax = lambda v: jnp.max(v, axis=0, keepdims=True)
    colmin = lambda v: jnp.min(v, axis=0, keepdims=True)
    colsum = lambda v: jnp.sum(v, axis=0, keepdims=True)

    gmask = (row >= N_EXPERTS) & (row < N_EXPERTS + N_GROUPS)
    lg = jnp.where(gmask, lt, ninf)
    gmax = colmax(lg)
    gidx = colmin(jnp.where(lg == gmax, rowf, big)) - N_EXPERTS
    g_w = 1.0 / colsum(jnp.exp(lg - gmax))
    row_group = (row // EXPERTS_PER_GROUP).astype(F32)
    emask = (row < N_EXPERTS) & (row_group == gidx)
    le = jnp.where(emask, lt, ninf)
    m1 = colmax(le)
    i1 = colmin(jnp.where(le == m1, rowf, big))
    le2 = jnp.where(rowf == i1, ninf, le)
    m2 = colmax(le2)
    i2 = colmin(jnp.where(le2 == m2, rowf, big))
    e2 = jnp.exp(m2 - m1)
    w0 = g_w / (1.0 + e2)
    w1 = g_w * e2 / (1.0 + e2)

    hit1 = rowf == i1
    hit2 = rowf == i2
    onehot = jnp.where(hit1, 1.0, jnp.where(hit2, 1.0, 0.0)).astype(F32)
    r = lax.broadcasted_iota(jnp.int32, (tm, tm), 0)
    c = lax.broadcasted_iota(jnp.int32, (tm, tm), 1)
    earlier = jnp.where(r < c, 1.0, 0.0).astype(BF16)
    within = _dot(onehot.astype(BF16), earlier)
    tile_cnt = jnp.broadcast_to(jnp.sum(onehot, axis=1, keepdims=True), (LANES, LANES))
    er = lax.broadcasted_iota(jnp.int32, (LANES, LANES), 0)
    ec = lax.broadcasted_iota(jnp.int32, (LANES, LANES), 1)
    below = jnp.where(er > ec, 1.0, 0.0).astype(BF16)
    cnt_hi = jnp.floor(tile_cnt * (1.0 / 32.0))
    cnt_lo = tile_cnt - 32.0 * cnt_hi
    run_start = 32.0 * _dot(below, cnt_hi.astype(BF16)) + _dot(below, cnt_lo.astype(BF16))
    local = within + run_start[:, 0:1]
    pos0 = colsum(jnp.where(hit1, local, 0.0))
    pos1 = colsum(jnp.where(hit2, local, 0.0))
    cnt_row = tile_cnt.T[0:1, :]
    tcarry_ref[0] = carry_ref[...]
    tcnt_ref[0] = cnt_row
    carry_ref[...] += cnt_row
    cnt_ref[...] = carry_ref[...]

    fields = jnp.concatenate([i1, i2, pos0, pos1, w0, w1, jnp.zeros((2, tm), F32)], axis=0)
    rt_ref[...] = fields
    rinfo_ref[...] = jnp.concatenate(
        [fields, jnp.zeros((LANES - 8, tm), F32)], axis=0).T


def _out_and_route(oa, ob, x2, mod, ga, gb, gpost, gpre2, w_out, wrh, wrl, br, seq):
    t, d = x2.shape
    tm = OUT_TM
    tpb = seq // tm
    nq = oa.shape[1]
    const = lambda shape: pl.BlockSpec(shape, lambda i: (0,) * len(shape))
    batch_of = lambda i: i // tpb
    rows = lambda n: pl.BlockSpec((tm, n), lambda i: (i, 0))
    per_tile = pl.BlockSpec((1, 1, LANES), lambda i: (i, 0, 0))
    w_half = lambda g: pl.BlockSpec((nq, d), lambda i: (g, 0))
    return pl.pallas_call(
        _out_kernel,
        grid=(t // tm,),
        in_specs=[rows(nq), rows(nq), rows(d),
                  _mod_spec(MOD_G1, batch_of)(d), _mod_spec(MOD_SC2, batch_of)(d),
                  _mod_spec(MOD_SH2, batch_of)(d),
                  const((1, nq)), const((1, nq)), const((1, d)), const((1, d)),
                  w_half(0), w_half(1), const(wrh.shape), const(wrl.shape),
                  const((1, LANES))],
        out_specs=[rows(d), rows(d), rows(LANES), pl.BlockSpec((8, tm), lambda i: (0, i)),
                   per_tile, per_tile, const((1, LANES))],
        out_shape=[jax.ShapeDtypeStruct((t, d), F32), jax.ShapeDtypeStruct((t, d), BF16),
                   jax.ShapeDtypeStruct((t, LANES), F32), jax.ShapeDtypeStruct((8, t), F32),
                   jax.ShapeDtypeStruct((t // tm, 1, LANES), F32),
                   jax.ShapeDtypeStruct((t // tm, 1, LANES), F32),
                   jax.ShapeDtypeStruct((1, LANES), F32)],
        scratch_shapes=[pltpu.VMEM((1, LANES), F32)],
        compiler_params=_params("arbitrary"),
    )(oa, ob, x2, mod, mod, mod, ga, gb, gpost, gpre2, w_out, w_out, wrh, wrl, br)


PACK_ROWS = 8
ROW_DTYPE = F32


def _pack_rows(ref, x):
    n = x.shape[0]
    for c in range(PACK_ROWS):
        ref[pl.ds(c, n, stride=PACK_ROWS), :] = x[:, c * LANES:(c + 1) * LANES]


def _unpack_rows(ref):
    n = ref.shape[0] // PACK_ROWS
    return jnp.concatenate(
        [ref[pl.ds(c, n, stride=PACK_ROWS), :].astype(BF16) for c in range(PACK_ROWS)], axis=1)


def _for_each_run_piece(rdst_ref, rlen_ref, tile, max_len, fn):
    n_bits = max_len.bit_length()

    def run(e, local):
        length = rlen_ref[tile * N_EXPERTS + e]
        dst = rdst_ref[tile * N_EXPERTS + e]
        for b in range(n_bits):
            size = 1 << b

            @pl.when(((length >> b) & 1) == 1)
            def _():
                done = length & (size - 1)
                fn(local + done, dst + done, size)
        return local + length

    lax.fori_loop(0, N_EXPERTS, run, 0)


def _token_rows(ref, row0, n_rows):
    start = row0 * PACK_ROWS
    if not isinstance(start, int):
        start = pl.multiple_of(start, PACK_ROWS)
    return ref.at[pl.ds(start, n_rows * PACK_ROWS)]


def _dispatch_kernel(rdst_ref, rlen_ref, h_ref, rt_ref, xs_ref, sorted_ref, sem):
    k = pl.program_id(0)
    nk = pl.num_programs(0)
    tm = h_ref.shape[0]
    rows = 2 * tm
    slot = k % 2

    def wait_slot(s):
        pltpu.make_async_copy(sorted_ref.at[s], _token_rows(xs_ref, 0, rows), sem.at[s]).wait()

    @pl.when(k >= 2)
    def _():
        wait_slot(slot)

    pos0 = rt_ref[2:3, :]
    pos1 = rt_ref[3:4, :]
    r = lax.broadcasted_iota(jnp.int32, (rows, tm), 0).astype(F32)
    perm = jnp.where((r == pos0) | (r == pos1), 1.0, 0.0).astype(BF16)
    srt = _dot(perm, h_ref[...].astype(BF16))
    buf = sorted_ref.at[slot]
    _pack_rows(buf, srt)

    def copy_piece(local, dst, size):
        pltpu.make_async_copy(_token_rows(buf, local, size), _token_rows(xs_ref, dst, size),
                              sem.at[slot]).start()

    _for_each_run_piece(rdst_ref, rlen_ref, k, tm, copy_piece)

    @pl.when(k == nk - 1)
    def _():
        wait_slot(slot)

        @pl.when(nk >= 2)
        def _():
            wait_slot(1 - slot)


def _dispatch(run_dst, run_len, h2, rt):
    t, d = h2.shape
    assert d == PACK_ROWS * LANES
    tm = MOVE_TM
    return pl.pallas_call(
        _dispatch_kernel,
        grid_spec=pltpu.PrefetchScalarGridSpec(
            num_scalar_prefetch=2,
            grid=(t // tm,),
            in_specs=[pl.BlockSpec((tm, d), lambda i, *_: (i, 0)),
                      pl.BlockSpec((8, tm), lambda i, *_: (0, i))],
            out_specs=pl.BlockSpec(memory_space=pl.ANY),
            scratch_shapes=[pltpu.VMEM((2, 2 * tm * PACK_ROWS, LANES), ROW_DTYPE),
                            pltpu.SemaphoreType.DMA((2,))]),
        out_shape=jax.ShapeDtypeStruct((2 * t * PACK_ROWS, LANES), ROW_DTYPE),
        compiler_params=_params("arbitrary"),
    )(run_dst, run_len, h2, rt)


def _expert_kernel(vt_ref, ve_ref, va_ref, vb_ref, nv_ref, xs_ref, wg_ref, wu_ref, wd_ref, ys_ref,
                   wg_bf, wu_bf, wd_bf):
    v = pl.program_id(0)
    valid = v < nv_ref[0]
    prev = jnp.maximum(v - 1, 0)
    new_expert = (v == 0) | (ve_ref[v] != ve_ref[prev])
    new_tile = (v == 0) | (vt_ref[v] != vt_ref[prev])

    @pl.when(valid & new_expert)
    def _():
        wg_bf[...] = wg_ref[0].astype(BF16)
        wu_bf[...] = wu_ref[0].astype(BF16)
        wd_bf[...] = wd_ref[0].astype(BF16)

    def expert_rows():
        xb = _unpack_rows(xs_ref)
        gate = _dot(xb, wg_bf[...])
        up = _dot(xb, wu_bf[...])
        act = gate * jax.nn.sigmoid(gate) * up
        return _dot(act.astype(BF16), wd_bf[...])

    @pl.when(valid & new_tile)
    def _():
        _pack_rows(ys_ref, expert_rows())

    @pl.when(valid & jnp.logical_not(new_tile))
    def _():
        y = expert_rows()
        te = y.shape[0]
        row = lax.broadcasted_iota(jnp.int32, (te, 1), 0)
        mine = (row >= va_ref[v]) & (row < vb_ref[v])
        for c in range(PACK_ROWS):
            rows = pl.ds(c, te, stride=PACK_ROWS)
            ys_ref[rows, :] = jnp.where(mine, y[:, c * LANES:(c + 1) * LANES], ys_ref[rows, :])


def _expert_mlp(visit_tile, visit_expert, visit_lo, visit_hi, n_visits, xs, w_gate, w_up, w_down):
    te = EXPERT_TE
    d, ff = w_gate.shape[1:]
    blk = (te * PACK_ROWS, LANES)
    tile = lambda v, vt, *_: (vt[v], 0)
    wsel = lambda v, vt, ve, *_: (ve[v], 0, 0)
    return pl.pallas_call(
        _expert_kernel,
        grid_spec=pltpu.PrefetchScalarGridSpec(
            num_scalar_prefetch=5,
            grid=(visit_tile.shape[0],),
            in_specs=[pl.BlockSpec(blk, tile),
                      pl.BlockSpec((1, d, ff), wsel), pl.BlockSpec((1, d, ff), wsel),
                      pl.BlockSpec((1, ff, d), wsel)],
            out_specs=pl.BlockSpec(blk, tile),
            scratch_shapes=[pltpu.VMEM((d, ff), BF16), pltpu.VMEM((d, ff), BF16),
                            pltpu.VMEM((ff, d), BF16)]),
        out_shape=jax.ShapeDtypeStruct(xs.shape, ROW_DTYPE),
        compiler_params=_params("arbitrary"),
    )(visit_tile, visit_expert, visit_lo, visit_hi, n_visits, xs, w_gate, w_up, w_down)


def _combine_kernel(rdst_ref, rlen_ref, x1_ref, rinfo_ref, g2_ref, gpost_ref, ys_ref, o_ref,
                    gath_ref, sem):
    k = pl.program_id(0)
    nk = pl.num_programs(0)
    tm = x1_ref.shape[0]
    rows = 2 * tm
    slot = k % 2

    def gather_runs(tile, s):
        buf = gath_ref.at[s]

        def copy_piece(local, src, size):
            pltpu.make_async_copy(_token_rows(ys_ref, src, size), _token_rows(buf, local, size),
                                  sem.at[s]).start()

        _for_each_run_piece(rdst_ref, rlen_ref, tile, tm, copy_piece)

    @pl.when(k == 0)
    def _():
        gather_runs(0, 0)

    @pl.when(k + 1 < nk)
    def _():
        gather_runs(k + 1, 1 - slot)

    buf = gath_ref.at[slot]
    pltpu.make_async_copy(_token_rows(ys_ref, 0, rows), buf, sem.at[slot]).wait()
    g = _unpack_rows(buf)
    info = rinfo_ref[...]
    col = lax.broadcasted_iota(jnp.int32, (tm, rows), 1).astype(F32)
    pick = jnp.where(col == info[:, 2:3], info[:, 4:5],
                     jnp.where(col == info[:, 3:4], info[:, 5:6], 0.0)).astype(BF16)
    fx = _dot(pick, g)
    o_ref[...] = x1_ref[...] + g2_ref[0, 0] * (_rms(fx) * gpost_ref[...])


def _combine(run_dst, run_len, x1, rinfo, mod, gpost, ys, seq):
    t, d = x1.shape
    tm = MOVE_TM
    tpb = seq // tm
    batch_of = lambda i: i // tpb
    return pl.pallas_call(
        _combine_kernel,
        grid_spec=pltpu.PrefetchScalarGridSpec(
            num_scalar_prefetch=2,
            grid=(t // tm,),
            in_specs=[pl.BlockSpec((tm, d), lambda i, *_: (i, 0)),
                      pl.BlockSpec((tm, LANES), lambda i, *_: (i, 0)),
                      _mod_spec(MOD_G2, batch_of)(d),
                      pl.BlockSpec((1, d), lambda i, *_: (0, 0)),
                      pl.BlockSpec(memory_space=pl.ANY)],
            out_specs=pl.BlockSpec((tm, d), lambda i, *_: (i, 0)),
            scratch_shapes=[pltpu.VMEM((2, 2 * tm * PACK_ROWS, LANES), ROW_DTYPE),
                            pltpu.SemaphoreType.DMA((2,))]),
        out_shape=jax.ShapeDtypeStruct((t, d), F32),
        compiler_params=_params("arbitrary"),
    )(run_dst, run_len, x1, rinfo, mod, gpost, ys)


def _rope_tables(seq):
    pos = np.arange(seq)
    row = (pos // GRID_W).astype(np.float32)
    col = (pos % GRID_W).astype(np.float32)
    axis_dim = HEAD_DIM // 2
    inv_freq = (ROPE_THETA ** (-np.arange(0, axis_dim, 2, dtype=np.float32) / axis_dim)).astype(
        np.float32)
    ang = np.concatenate([row[:, None] * inv_freq, col[:, None] * inv_freq], axis=-1)
    pair = (np.arange(LANES) % HEAD_DIM) // 2
    cos = np.cos(ang)[:, pair]
    sin = np.sin(ang)[:, pair]
    even = (np.arange(LANES) % 2) == 0
    tables = (cos, np.where(even, -sin, 0.0), np.where(even, 0.0, sin))
    tables = tables + tuple(tb.T for tb in tables)
    return tuple(jnp.asarray(tb, F32) for tb in tables)


def _segment_ones(n):
    seg = np.arange(n) // HEAD_DIM
    return jnp.asarray(seg[:, None] == seg[None, :], BF16)


def kernel(x, c, ctx, c_ctx, w_mod, b_mod, attn_pre_norm, attn_post_norm, w_in, a_sink,
           b_q_norm, b_k_norm, a_out_norm, b_out_norm, w_out, ffn_pre_norm, ffn_post_norm,
           w_group, b_group, w_router, b_router, w_gate, w_up, w_down):
    batch, seq, d = x.shape
    ctx_len = ctx.shape[1]
    assert w_mod.shape[0] == 1, "single-layer stack only (context stream is never updated)"
    assert seq % ATTN_A_TQ == 0 and seq >= ATTN_A_TQ + 2 * WINDOW
    assert seq % PROJ_TM == 0 and seq % ATTN_B_TQ == 0 and seq % OUT_TM == 0 and seq % MOVE_TM == 0
    t = batch * seq
    nq = d // 2
    nkv = nq // KV_GROUP
    assert nkv == LANES and w_in.shape[2] == 2 * nq + 4 * nkv

    cc = jnp.concatenate([c, c_ctx[None, :], jnp.zeros((16 - batch - 1, d), F32)], axis=0)
    mod = _modulation(cc, w_mod[0], b_mod[0]).reshape(cc.shape[0], 6, 1, d)

    x2 = x.reshape(t, d)
    c2 = ctx.reshape(batch * ctx_len, d)
    gpre = attn_pre_norm[0].reshape(1, d)
    w_in_bf = w_in[0].astype(BF16)
    qn = jnp.tile(b_q_norm[0], nq // HEAD_DIM).reshape(1, nq)
    kn = jnp.tile(b_k_norm[0], nkv // HEAD_DIM).reshape(1, nkv)
    seg_q, seg_k = _segment_ones(nq), _segment_ones(nkv)
    qat, ka, vat, qbt, kb, vbt, stats = _project_latents(
        x2, mod, gpre, w_in_bf, _rope_tables(seq), qn, kn, seg_q, seg_k, seq)
    kca, vcat, kcb, vcbt, ctx_stats = _project_context(
        c2, mod, batch, gpre, w_in_bf, kn, seg_k, ctx_len)

    q_sq = jnp.max(stats[:, 0, 0])
    k_sq = jnp.maximum(jnp.max(stats[:, 0, 1]), jnp.max(ctx_stats[:, 0, 1]))
    bound_a = jnp.maximum(1.01 * jnp.sqrt(q_sq * k_sq), jnp.max(a_sink[0]) * LOG2_E).reshape(1)
    oa = _attention_a(bound_a, a_sink[0], qat, ka, vat, kca, vcat, batch, seq, ctx_len)
    score_bound = (1.01 * HEAD_DIM ** 0.5 * LOG2_E
                   * jnp.max(jnp.abs(b_q_norm[0])) * jnp.max(jnp.abs(b_k_norm[0]))).reshape(1)
    ob = _attention_b(score_bound, qbt, kb, vbt, kcb, vcbt, batch, seq, ctx_len)

    w_out_bf = w_out[0].astype(BF16)
    lane_pad = LANES - N_EXPERTS - N_GROUPS
    w_r = jnp.pad(jnp.concatenate([w_router[0], w_group[0]], axis=1), ((0, 0), (0, lane_pad)))
    w_r_hi = w_r.astype(BF16)
    w_r_lo = (w_r - w_r_hi.astype(F32)).astype(BF16)
    b_r = jnp.pad(jnp.concatenate([b_router[0], b_group[0]]), (0, lane_pad)).reshape(1, LANES)
    x1, h2, rinfo, rt, tcarry, tcnt, counts = _out_and_route(
        oa, ob, x2, mod, a_out_norm[0].reshape(1, nq), b_out_norm[0].reshape(1, nq),
        attn_post_norm[0].reshape(1, d), ffn_pre_norm[0].reshape(1, d),
        w_out_bf, w_r_hi, w_r_lo, b_r, seq)

    te = EXPERT_TE
    assert (2 * t) % te == 0
    cnt = counts[0, :N_EXPERTS].astype(jnp.int32)
    ends = jnp.cumsum(cnt)
    starts = ends - cnt
    run_dst = (starts[None, :] + tcarry[:, 0, :N_EXPERTS].astype(jnp.int32)).reshape(-1)
    run_len = tcnt[:, 0, :N_EXPERTS].astype(jnp.int32).reshape(-1)
    first_tile = starts // te
    n_vis = jnp.where(cnt > 0, (ends - 1) // te - first_tile + 1, 0)
    vis_end = jnp.cumsum(n_vis)
    n_visits = vis_end[-1]
    v = jnp.minimum(jnp.arange(2 * t // te + N_EXPERTS, dtype=jnp.int32), n_visits - 1)
    v_expert = jnp.sum(vis_end[None, :] <= v[:, None], axis=1).astype(jnp.int32)
    pick = (v_expert[:, None] == jnp.arange(N_EXPERTS)[None, :]).astype(jnp.int32)
    of_expert = lambda table: jnp.sum(pick * table[None, :], axis=1)
    v_tile = of_expert(first_tile) + v - of_expert(vis_end - n_vis)
    v_lo = jnp.maximum(of_expert(starts) - v_tile * te, 0)
    v_hi = jnp.minimum(of_expert(ends) - v_tile * te, te)

    xs = _dispatch(run_dst, run_len, h2, rt)
    ys = _expert_mlp(v_tile, v_expert, v_lo, v_hi, n_visits.reshape(1), xs,
                     w_gate[0], w_up[0], w_down[0])
    out = _combine(run_dst, run_len, x1, rinfo, mod, ffn_post_norm[0].reshape(1, d), ys, seq)
    return out.reshape(batch, seq, d)
```

```python
import functools

import jax
import jax.numpy as jnp
import numpy as np
from jax import lax
from jax.experimental import pallas as pl
from jax.experimental.pallas import tpu as pltpu

F32 = jnp.float32
BF16 = jnp.bfloat16

GRID_W = 64
HEAD_DIM = 64
KV_GROUP = 4
WINDOW = 128
ROPE_THETA = 10000.0
N_GROUPS = 4
EXPERTS_PER_GROUP = 8
N_EXPERTS = N_GROUPS * EXPERTS_PER_GROUP
EPS = 1e-6
NEG_INF = -1e30
LOG2_E = 1.4426950408889634
SAFE_SOFTMAX_SHIFT = 40.0

LANES = 128
V7X_VMEM_LIMIT = 56 * 1024 * 1024

PROJ_TM = 512
ATTN_A_TQ = 256
ATTN_B_TQ = 256
OUT_TM = 512
EXPERT_TE = 512
MOVE_TM = OUT_TM


def _params(*sem):
    return pltpu.CompilerParams(dimension_semantics=sem, vmem_limit_bytes=V7X_VMEM_LIMIT)


def _dot(a, b):
    return jnp.dot(a, b, preferred_element_type=F32)


def _rms(x):
    return x * lax.rsqrt(jnp.mean(x * x, axis=-1, keepdims=True) + EPS)


def _split_bf16(x):
    hi = x.astype(BF16)
    lo = (x - hi.astype(F32)).astype(BF16)
    return hi, lo


def _mod_kernel(c_ref, w_ref, b_ref, o_ref):
    cc = c_ref[...]
    s = cc * jax.nn.sigmoid(cc)
    s_hi, s_lo = _split_bf16(s)
    w_hi, w_lo = _split_bf16(w_ref[...])
    o_ref[...] = _dot(s_hi, w_hi) + _dot(s_lo, w_hi) + _dot(s_hi, w_lo) + b_ref[...]


def _modulation(cc, w_mod, b_mod):
    rows, d = cc.shape
    n = w_mod.shape[1]
    bn = 1024
    return pl.pallas_call(
        _mod_kernel,
        grid=(n // bn,),
        in_specs=[pl.BlockSpec((rows, d), lambda i: (0, 0)),
                  pl.BlockSpec((d, bn), lambda i: (0, i)),
                  pl.BlockSpec((1, bn), lambda i: (0, i))],
        out_specs=pl.BlockSpec((rows, bn), lambda i: (0, i)),
        out_shape=jax.ShapeDtypeStruct((rows, n), F32),
        compiler_params=_params("arbitrary"),
    )(cc, w_mod, b_mod.reshape(1, n))


def _rope(x, cos, sin_a, sin_b):
    return x * cos + pltpu.roll(x, LANES - 1, 1) * sin_a + pltpu.roll(x, 1, 1) * sin_b


def _head_norm(x, seg_ref, gain):
    ss = _dot((x * x).astype(BF16), seg_ref[...])
    return x * lax.rsqrt(ss * (1.0 / HEAD_DIM) + EPS) * gain


def _max_head_sq_norm(x, seg_ref):
    ss = _dot((x * x).astype(BF16), seg_ref[...])
    return jnp.max(jnp.max(ss, axis=1, keepdims=True), axis=0, keepdims=True)


def _norm_stats(q_sq, k_sq):
    lane = lax.broadcasted_iota(jnp.int32, (1, LANES), 1)
    zero = jnp.zeros((1, LANES), F32)
    return jnp.where(lane == 0, q_sq, zero) + jnp.where(lane == 1, k_sq, zero)


def _rope_t(xt, cos_t, sin_a_t, sin_b_t):
    return (xt * cos_t + pltpu.roll(xt, LANES - 1, 0) * sin_a_t
            + pltpu.roll(xt, 1, 0) * sin_b_t)


def _proj_kernel(x_ref, sc_ref, sh_ref, gpre_ref, w_ref, cos_ref, sa_ref, sb_ref,
                 cos_t_ref, sa_t_ref, sb_t_ref, qn_ref, kn_ref, seg_k_ref,
                 qat_ref, ka_ref, vat_ref, qbt_ref, kb_ref, vbt_ref, stats_ref):
    h = _rms(x_ref[...]) * gpre_ref[...] * (1.0 + sc_ref[0, 0]) + sh_ref[0, 0]
    p = _dot(h.astype(BF16), w_ref[...])
    cos, sa, sb = cos_ref[...], sa_ref[...], sb_ref[...]
    q_scale = HEAD_DIM ** -0.5 * LOG2_E
    cos_t, sa_t, sb_t = cos_t_ref[...] * q_scale, sa_t_ref[...] * q_scale, sb_t_ref[...] * q_scale
    nq = qat_ref.shape[0]
    q_sq = None
    for c in range(nq // LANES):
        xt = p[:, c * LANES:(c + 1) * LANES].T
        qat_ref[c * LANES:(c + 1) * LANES, :] = _rope_t(xt, cos_t, sa_t, sb_t).astype(BF16)
        sq = xt * xt
        for head in range(LANES // HEAD_DIM):
            norm = jnp.sum(sq[head * HEAD_DIM:(head + 1) * HEAD_DIM, :], axis=0, keepdims=True)
            q_sq = norm if q_sq is None else jnp.maximum(q_sq, norm)
    q_sq = jnp.max(q_sq, axis=1, keepdims=True) * (q_scale * q_scale)
    o = nq
    ka_ref[...] = _rope(p[:, o:o + LANES], cos, sa, sb).astype(BF16)
    vat_ref[...] = p[:, o + LANES:o + 2 * LANES].T.astype(BF16)
    stats_ref[0] = _norm_stats(q_sq, _max_head_sq_norm(p[:, o:o + LANES], seg_k_ref))
    o += 2 * LANES
    for c in range(nq // LANES):
        xt = p[:, o + c * LANES:o + (c + 1) * LANES].T
        sq = xt * xt
        halves = []
        for head in range(LANES // HEAD_DIM):
            rows = slice(head * HEAD_DIM, (head + 1) * HEAD_DIM)
            ms = jnp.sum(sq[rows, :], axis=0, keepdims=True) * (1.0 / HEAD_DIM)
            halves.append(xt[rows, :] * lax.rsqrt(ms + EPS))
        qn = jnp.concatenate(halves, axis=0) * qn_ref[c * LANES:(c + 1) * LANES, :]
        qbt_ref[c * LANES:(c + 1) * LANES, :] = _rope_t(qn, cos_t, sa_t, sb_t).astype(BF16)
    o += nq
    kb = _head_norm(p[:, o:o + LANES], seg_k_ref, kn_ref[...])
    kb_ref[...] = _rope(kb, cos, sa, sb).astype(BF16)
    vbt_ref[...] = p[:, o + LANES:o + 2 * LANES].T.astype(BF16)


def _ctx_proj_kernel(x_ref, sc_ref, sh_ref, gpre_ref, wa_ref, wb_ref, kn_ref, seg_k_ref,
                     ka_ref, vat_ref, kb_ref, vbt_ref, stats_ref):
    h = (_rms(x_ref[...]) * gpre_ref[...] * (1.0 + sc_ref[0, 0]) + sh_ref[0, 0]).astype(BF16)
    pa = _dot(h, wa_ref[...])
    pb = _dot(h, wb_ref[...])
    stats_ref[0] = _norm_stats(0.0, _max_head_sq_norm(pa[:, 0:LANES], seg_k_ref))
    ka_ref[...] = pa[:, 0:LANES].astype(BF16)
    vat_ref[...] = pa[:, LANES:2 * LANES].T.astype(BF16)
    kb_ref[...] = _head_norm(pb[:, 0:LANES], seg_k_ref, kn_ref[...]).astype(BF16)
    vbt_ref[...] = pb[:, LANES:2 * LANES].T.astype(BF16)


def _mod_spec(chunk, row_of_step):
    return lambda d: pl.BlockSpec((1, 1, 1, d), lambda i, *_: (row_of_step(i), chunk, 0, 0))


MOD_SH1, MOD_SC1, MOD_G1, MOD_SH2, MOD_SC2, MOD_G2 = range(6)


def _project_latents(x2, mod, gpre, w_in, tables, qn, kn, seg_k, seq):
    t, d = x2.shape
    tm = PROJ_TM
    tpb = seq // tm
    nq = qn.shape[0]
    const = lambda shape: pl.BlockSpec(shape, lambda i: (0,) * len(shape))
    batch_of = lambda i: i // tpb
    table = pl.BlockSpec((tm, LANES), lambda i: (i % tpb, 0))
    table_t = pl.BlockSpec((LANES, tm), lambda i: (0, i % tpb))
    k_spec = pl.BlockSpec((tm, LANES), lambda i: (i, 0))
    k_shape = jax.ShapeDtypeStruct((t, LANES), BF16)
    vt_spec = pl.BlockSpec((LANES, tm), lambda i: (0, i))
    vt_shape = jax.ShapeDtypeStruct((LANES, t), BF16)
    qt_spec = pl.BlockSpec((nq, tm), lambda i: (0, i))
    qt_shape = jax.ShapeDtypeStruct((nq, t), BF16)
    return pl.pallas_call(
        _proj_kernel,
        grid=(t // tm,),
        in_specs=[pl.BlockSpec((tm, d), lambda i: (i, 0)),
                  _mod_spec(MOD_SC1, batch_of)(d), _mod_spec(MOD_SH1, batch_of)(d), const((1, d)),
                  const(w_in.shape), table, table, table, table_t, table_t, table_t,
                  const((nq, tm)), const((1, LANES)), const(seg_k.shape)],
        out_specs=[qt_spec, k_spec, vt_spec, qt_spec, k_spec, vt_spec,
                   pl.BlockSpec((1, 1, LANES), lambda i: (i, 0, 0))],
        out_shape=[qt_shape, k_shape, vt_shape, qt_shape, k_shape, vt_shape,
                   jax.ShapeDtypeStruct((t // tm, 1, LANES), F32)],
        compiler_params=_params("arbitrary"),
    )(x2, mod, mod, gpre, w_in, *tables, qn, kn, seg_k)


def _project_context(c2, mod, ctx_row, gpre, w_in, kn, seg_k, ctx_len):
    t, d = c2.shape
    ctx_mod = lambda i: ctx_row
    nq = (w_in.shape[1] - 4 * LANES) // 2
    kv = 2 * LANES
    assert nq % kv == 0
    group_kv = lambda g: pl.BlockSpec((d, kv), lambda i: (0, (g * (nq + kv) + nq) // kv))
    const = lambda shape: pl.BlockSpec(shape, lambda i: (0,) * len(shape))
    k_spec = pl.BlockSpec((ctx_len, LANES), lambda i: (i, 0))
    k_shape = jax.ShapeDtypeStruct((t, LANES), BF16)
    vt_spec = pl.BlockSpec((LANES, ctx_len), lambda i: (0, i))
    vt_shape = jax.ShapeDtypeStruct((LANES, t), BF16)
    return pl.pallas_call(
        _ctx_proj_kernel,
        grid=(t // ctx_len,),
        in_specs=[pl.BlockSpec((ctx_len, d), lambda i: (i, 0)),
                  _mod_spec(MOD_SC1, ctx_mod)(d), _mod_spec(MOD_SH1, ctx_mod)(d),
                  const((1, d)), group_kv(0), group_kv(1), const((1, LANES)), const(seg_k.shape)],
        out_specs=[k_spec, vt_spec, k_spec, vt_spec,
                   pl.BlockSpec((1, 1, LANES), lambda i: (i, 0, 0))],
        out_shape=[k_shape, vt_shape, k_shape, vt_shape,
                   jax.ShapeDtypeStruct((t // ctx_len, 1, LANES), F32)],
        compiler_params=_params("arbitrary"),
    )(c2, mod, mod, gpre, w_in, w_in, kn, seg_k)


def _attend_t(w, k, kc, vt, vct, shift=None, bias=None, sink=None):
    st = _dot(k, w)
    sct = _dot(kc, w)
    if bias is not None:
        tq = bias.shape[1]
        st = jnp.concatenate([st[:, c * tq:(c + 1) * tq] + bias
                              for c in range(st.shape[1] // tq)], axis=1)
    if shift is None:
        shift = jnp.maximum(jnp.max(st, axis=0, keepdims=True),
                            jnp.max(sct, axis=0, keepdims=True))
        if sink is not None:
            shift = jnp.maximum(shift, sink)
    pt = jnp.exp2(st - shift)
    pct = jnp.exp2(sct - shift)
    denom = jnp.sum(pt, axis=0, keepdims=True) + jnp.sum(pct, axis=0, keepdims=True)
    if sink is not None:
        denom = denom + jnp.exp2(sink - shift)
    o2 = _dot(vt, pt.astype(BF16)) + _dot(vct, pct.astype(BF16))
    return o2, denom


def _all_heads_t(qt_ref, o_ref, attend):
    tq = qt_ref.shape[1]
    n_kv = LANES // HEAD_DIM
    zeros = jnp.zeros((HEAD_DIM, tq), BF16)
    outs = []
    for h in range(qt_ref.shape[0] // HEAD_DIM):
        g = h // KV_GROUP
        qh = qt_ref[h * HEAD_DIM:(h + 1) * HEAD_DIM, :]
        w = jnp.concatenate([zeros] * g + [qh] + [zeros] * (n_kv - 1 - g), axis=0)
        o2, denom = attend(h, w)
        outs.append(o2[g * HEAD_DIM:(g + 1) * HEAD_DIM, :] / denom)
    o_ref[...] = jnp.concatenate(outs, axis=0).T.astype(BF16)


def _all_heads_fused_t(qt_ref, o_ref, attend):
    tq = qt_ref.shape[1]
    n_heads = qt_ref.shape[0] // HEAD_DIM
    n_kv = LANES // HEAD_DIM
    rows = []
    for g in range(n_kv):
        heads = [qt_ref[h * HEAD_DIM:(h + 1) * HEAD_DIM, :] if h // KV_GROUP == g
                 else jnp.zeros((HEAD_DIM, tq), BF16) for h in range(n_heads)]
        rows.append(jnp.concatenate(heads, axis=1))
    w = jnp.concatenate(rows, axis=0)
    o2, denom = attend(0, w)
    o2 = o2 / denom
    outs = [o2[(h // KV_GROUP) * HEAD_DIM:(h // KV_GROUP + 1) * HEAD_DIM, h * tq:(h + 1) * tq]
            for h in range(n_heads)]
    o_ref[...] = jnp.concatenate(outs, axis=0).T.astype(BF16)


def _attn_b_kernel(bound_ref, qt_ref, k_ref, vt_ref, kc_ref, vct_ref, o_ref):
    k, kc, vt, vct = k_ref[...], kc_ref[...], vt_ref[...], vct_ref[...]
    bound = bound_ref[0]

    @pl.when(bound <= SAFE_SOFTMAX_SHIFT)
    def _():
        _all_heads_fused_t(qt_ref, o_ref, lambda h, w: _attend_t(w, k, kc, vt, vct, shift=bound))

    @pl.when(jnp.logical_not(bound <= SAFE_SOFTMAX_SHIFT))
    def _():
        _all_heads_t(qt_ref, o_ref, lambda h, w: _attend_t(w, k, kc, vt, vct))


def _attn_specs(nq, tq, seq, ctx_len):
    nqb = seq // tq
    return dict(
        qt=pl.BlockSpec((nq, tq), lambda b, i: (0, b * nqb + i)),
        k=pl.BlockSpec((seq, LANES), lambda b, i: (b, 0)),
        vt=pl.BlockSpec((LANES, seq), lambda b, i: (0, b)),
        kc=pl.BlockSpec((ctx_len, LANES), lambda b, i: (b, 0)),
        vct=pl.BlockSpec((LANES, ctx_len), lambda b, i: (0, b)),
        out=pl.BlockSpec((tq, nq), lambda b, i: (b * nqb + i, 0)))


def _attention_b(score_bound, qbt, kb, vbt, kcb, vcbt, batch, seq, ctx_len):
    nq, t = qbt.shape
    tq = ATTN_B_TQ
    sp = _attn_specs(nq, tq, seq, ctx_len)
    return pl.pallas_call(
        _attn_b_kernel,
        grid=(batch, seq // tq),
        in_specs=[pl.BlockSpec(memory_space=pltpu.SMEM),
                  sp["qt"], sp["k"], sp["vt"], sp["kc"], sp["vct"]],
        out_specs=sp["out"],
        out_shape=jax.ShapeDtypeStruct((t, nq), BF16),
        compiler_params=_params("arbitrary", "arbitrary"),
    )(score_bound, qbt, kb, vbt, kcb, vcbt)


def _attn_a_kernel(bound_ref, sink_ref, qt_ref, k_ref, vt_ref, kc_ref, vct_ref, bias_ref, o_ref,
                   *, seq):
    i = pl.program_id(1)
    tq = qt_ref.shape[1]
    n_heads = qt_ref.shape[0] // HEAD_DIM
    win = tq + 2 * WINDOW
    start = pl.multiple_of(jnp.clip(i * tq - WINDOW, 0, seq - win), WINDOW)
    k = k_ref[pl.ds(start, win), :]
    vt = vt_ref[:, pl.ds(start, win)]
    kc, vct = kc_ref[...], vct_ref[...]
    sinks = [sink_ref[h] * LOG2_E for h in range(n_heads)]

    bound = bound_ref[0]
    small = bound <= SAFE_SOFTMAX_SHIFT

    @pl.when(small)
    def _():
        sink_row = jnp.concatenate([jnp.full((1, tq), s, F32) for s in sinks], axis=1)
        _all_heads_fused_t(qt_ref, o_ref, lambda h, w: _attend_t(
            w, k, kc, vt, vct, shift=bound, bias=bias_ref[0], sink=sink_row))

    @pl.when(jnp.logical_not(small))
    def _():
        _all_heads_t(qt_ref, o_ref, lambda h, w: _attend_t(
            w, k, kc, vt, vct, bias=bias_ref[0], sink=sinks[h]))


def _band_bias(tq):
    win = tq + 2 * WINDOW
    r = np.arange(win)[:, None]
    j = np.arange(tq)[None, :]
    tables = [np.where(np.abs(off + r - j) <= WINDOW, 0.0, NEG_INF)
              for off in (0, -WINDOW, -2 * WINDOW)]
    return jnp.asarray(np.stack(tables), F32)


def _attention_a(score_bound, sink, qat, ka, vat, kca, vcat, batch, seq, ctx_len):
    nq, t = qat.shape
    tq = ATTN_A_TQ
    nqb = seq // tq
    win = tq + 2 * WINDOW
    assert nqb >= 2 and tq >= WINDOW
    sp = _attn_specs(nq, tq, seq, ctx_len)
    which = lambda b, i: (jnp.where(i == 0, 0, jnp.where(i == nqb - 1, 2, 1)), 0, 0)
    return pl.pallas_call(
        functools.partial(_attn_a_kernel, seq=seq),
        grid=(batch, nqb),
        in_specs=[pl.BlockSpec(memory_space=pltpu.SMEM), pl.BlockSpec(memory_space=pltpu.SMEM),
                  sp["qt"], sp["k"], sp["vt"], sp["kc"], sp["vct"],
                  pl.BlockSpec((1, win, tq), which)],
        out_specs=sp["out"],
        out_shape=jax.ShapeDtypeStruct((t, nq), BF16),
        compiler_params=_params("arbitrary", "arbitrary"),
    )(score_bound, sink, qat, ka, vat, kca, vcat, _band_bias(tq))


def _out_kernel(oa_ref, ob_ref, x_ref, g1_ref, sc2_ref, sh2_ref, ga_ref, gb_ref, gpost_ref,
                gpre2_ref, woa_ref, wob_ref, wr_ref, br_ref,
                x1_ref, h2_ref, rinfo_ref, rt_ref, tcarry_ref, tcnt_ref, cnt_ref, carry_ref):
    step = pl.program_id(0)

    @pl.when(step == 0)
    def _():
        carry_ref[...] = jnp.zeros_like(carry_ref)

    na = _rms(oa_ref[...].astype(F32)) * ga_ref[...]
    nb = _rms(ob_ref[...].astype(F32)) * gb_ref[...]
    ox = _dot(na.astype(BF16), woa_ref[...]) + _dot(nb.astype(BF16), wob_ref[...])
    x1 = x_ref[...] + g1_ref[0, 0] * (_rms(ox) * gpost_ref[...])
    x1_ref[...] = x1
    h2 = _rms(x1) * gpre2_ref[...] * (1.0 + sc2_ref[0, 0]) + sh2_ref[0, 0]
    h_hi, h_lo = _split_bf16(h2)
    h2_ref[...] = h_hi

    both = _dot(h_hi, wr_ref[...])
    logits = (both[:, :LANES] + _dot(h_lo, wr_ref[:, :LANES]) + both[:, LANES:]
              + br_ref[...])
    tm = logits.shape[0]
    lt = logits.T
    row = lax.broadcasted_iota(jnp.int32, lt.shape, 0)
    rowf = row.astype(F32)
    big = jnp.float32(1e9)
    ninf = jnp.float32(-jnp.inf)
    colmax = lambda v: jnp.max(v, axis=0, keepdims=True)
    colmin = lambda v: jnp.min(v, axis=0, keepdims=True)
    colsum = lambda v: jnp.sum(v, axis=0, keepdims=True)

    gmask = (row >= N_EXPERTS) & (row < N_EXPERTS + N_GROUPS)
    lg = jnp.where(gmask, lt, ninf)
    gmax = colmax(lg)
    gidx = colmin(jnp.where(lg == gmax, rowf, big)) - N_EXPERTS
    g_w = 1.0 / colsum(jnp.exp(lg - gmax))
    row_group = (row // EXPERTS_PER_GROUP).astype(F32)
    emask = (row < N_EXPERTS) & (row_group == gidx)
    le = jnp.where(emask, lt, ninf)
    m1 = colmax(le)
    i1 = colmin(jnp.where(le == m1, rowf, big))
    le2 = jnp.where(rowf == i1, ninf, le)
    m2 = colmax(le2)
    i2 = colmin(jnp.where(le2 == m2, rowf, big))
    e2 = jnp.exp(m2 - m1)
    w0 = g_w / (1.0 + e2)
    w1 = g_w * e2 / (1.0 + e2)

    hit1 = rowf == i1
    hit2 = rowf == i2
    onehot = jnp.where(hit1, 1.0, jnp.where(hit2, 1.0, 0.0)).astype(F32)
    r = lax.broadcasted_iota(jnp.int32, (tm, tm), 0)
    c = lax.broadcasted_iota(jnp.int32, (tm, tm), 1)
    earlier = jnp.where(r < c, 1.0, 0.0).astype(BF16)
    within = _dot(onehot.astype(BF16), earlier)
    tile_cnt = jnp.broadcast_to(jnp.sum(onehot, axis=1, keepdims=True), (LANES, LANES))
    er = lax.broadcasted_iota(jnp.int32, (LANES, LANES), 0)
    ec = lax.broadcasted_iota(jnp.int32, (LANES, LANES), 1)
    below = jnp.where(er > ec, 1.0, 0.0).astype(BF16)
    cnt_hi = jnp.floor(tile_cnt * (1.0 / 32.0))
    cnt_lo = tile_cnt - 32.0 * cnt_hi
    run_start = 32.0 * _dot(below, cnt_hi.astype(BF16)) + _dot(below, cnt_lo.astype(BF16))
    local = within + run_start[:, 0:1]
    pos0 = colsum(jnp.where(hit1, local, 0.0))
    pos1 = colsum(jnp.where(hit2, local, 0.0))
    cnt_row = tile_cnt.T[0:1, :]
    tcarry_ref[0] = carry_ref[...]
    tcnt_ref[0] = cnt_row
    carry_ref[...] += cnt_row
    cnt_ref[...] = carry_ref[...]

    fields = jnp.concatenate([i1, i2, pos0, pos1, w0, w1, jnp.zeros((2, tm), F32)], axis=0)
    rt_ref[...] = fields
    rinfo_ref[...] = jnp.concatenate(
        [fields, jnp.zeros((LANES - 8, tm), F32)], axis=0).T


def _out_and_route(oa, ob, x2, mod, ga, gb, gpost, gpre2, w_out, wr, br, seq):
    t, d = x2.shape
    tm = OUT_TM
    tpb = seq // tm
    nq = oa.shape[1]
    const = lambda shape: pl.BlockSpec(shape, lambda i: (0,) * len(shape))
    batch_of = lambda i: i // tpb
    rows = lambda n: pl.BlockSpec((tm, n), lambda i: (i, 0))
    per_tile = pl.BlockSpec((1, 1, LANES), lambda i: (i, 0, 0))
    w_half = lambda g: pl.BlockSpec((nq, d), lambda i: (g, 0))
    return pl.pallas_call(
        _out_kernel,
        grid=(t // tm,),
        in_specs=[rows(nq), rows(nq), rows(d),
                  _mod_spec(MOD_G1, batch_of)(d), _mod_spec(MOD_SC2, batch_of)(d),
                  _mod_spec(MOD_SH2, batch_of)(d),
                  const((1, nq)), const((1, nq)), const((1, d)), const((1, d)),
                  w_half(0), w_half(1), const(wr.shape),
                  const((1, LANES))],
        out_specs=[rows(d), rows(d), rows(LANES), pl.BlockSpec((8, tm), lambda i: (0, i)),
                   per_tile, per_tile, const((1, LANES))],
        out_shape=[jax.ShapeDtypeStruct((t, d), F32), jax.ShapeDtypeStruct((t, d), BF16),
                   jax.ShapeDtypeStruct((t, LANES), F32), jax.ShapeDtypeStruct((8, t), F32),
                   jax.ShapeDtypeStruct((t // tm, 1, LANES), F32),
                   jax.ShapeDtypeStruct((t // tm, 1, LANES), F32),
                   jax.ShapeDtypeStruct((1, LANES), F32)],
        scratch_shapes=[pltpu.VMEM((1, LANES), F32)],
        compiler_params=_params("arbitrary"),
    )(oa, ob, x2, mod, mod, mod, ga, gb, gpost, gpre2, w_out, w_out, wr, br)


PACK_ROWS = 8
ROW_DTYPE = F32


def _pack_rows(ref, x):
    n = x.shape[0]
    for c in range(PACK_ROWS):
        ref[pl.ds(c, n, stride=PACK_ROWS), :] = x[:, c * LANES:(c + 1) * LANES]


def _unpack_rows(ref):
    n = ref.shape[0] // PACK_ROWS
    return jnp.concatenate(
        [ref[pl.ds(c, n, stride=PACK_ROWS), :].astype(BF16) for c in range(PACK_ROWS)], axis=1)


def _for_each_run_piece(rdst_ref, rlen_ref, tile, max_len, fn):
    n_bits = max_len.bit_length()

    def run(e, local):
        length = rlen_ref[tile * N_EXPERTS + e]
        dst = rdst_ref[tile * N_EXPERTS + e]
        for b in range(n_bits):
            size = 1 << b

            @pl.when(((length >> b) & 1) == 1)
            def _():
                done = length & (size - 1)
                fn(local + done, dst + done, size)
        return local + length

    lax.fori_loop(0, N_EXPERTS, run, 0)


def _token_rows(ref, row0, n_rows):
    start = row0 * PACK_ROWS
    if not isinstance(start, int):
        start = pl.multiple_of(start, PACK_ROWS)
    return ref.at[pl.ds(start, n_rows * PACK_ROWS)]


def _dispatch_kernel(rdst_ref, rlen_ref, h_ref, rt_ref, xs_ref, sorted_ref, sem):
    k = pl.program_id(0)
    nk = pl.num_programs(0)
    tm = h_ref.shape[0]
    rows = 2 * tm
    slot = k % 2

    def wait_slot(s):
        pltpu.make_async_copy(sorted_ref.at[s], _token_rows(xs_ref, 0, rows), sem.at[s]).wait()

    @pl.when(k >= 2)
    def _():
        wait_slot(slot)

    pos0 = rt_ref[2:3, :]
    pos1 = rt_ref[3:4, :]
    r = lax.broadcasted_iota(jnp.int32, (rows, tm), 0).astype(F32)
    perm = jnp.where((r == pos0) | (r == pos1), 1.0, 0.0).astype(BF16)
    srt = _dot(perm, h_ref[...].astype(BF16))
    buf = sorted_ref.at[slot]
    _pack_rows(buf, srt)

    def copy_piece(local, dst, size):
        pltpu.make_async_copy(_token_rows(buf, local, size), _token_rows(xs_ref, dst, size),
                              sem.at[slot]).start()

    _for_each_run_piece(rdst_ref, rlen_ref, k, tm, copy_piece)

    @pl.when(k == nk - 1)
    def _():
        wait_slot(slot)

        @pl.when(nk >= 2)
        def _():
            wait_slot(1 - slot)


def _dispatch(run_dst, run_len, h2, rt):
    t, d = h2.shape
    assert d == PACK_ROWS * LANES
    tm = MOVE_TM
    return pl.pallas_call(
        _dispatch_kernel,
        grid_spec=pltpu.PrefetchScalarGridSpec(
            num_scalar_prefetch=2,
            grid=(t // tm,),
            in_specs=[pl.BlockSpec((tm, d), lambda i, *_: (i, 0)),
                      pl.BlockSpec((8, tm), lambda i, *_: (0, i))],
            out_specs=pl.BlockSpec(memory_space=pl.ANY),
            scratch_shapes=[pltpu.VMEM((2, 2 * tm * PACK_ROWS, LANES), ROW_DTYPE),
                            pltpu.SemaphoreType.DMA((2,))]),
        out_shape=jax.ShapeDtypeStruct((2 * t * PACK_ROWS, LANES), ROW_DTYPE),
        compiler_params=_params("arbitrary"),
    )(run_dst, run_len, h2, rt)


def _expert_kernel(vt_ref, ve_ref, va_ref, vb_ref, nv_ref, xs_ref, wg_ref, wu_ref, wd_ref, ys_ref,
                   wg_bf, wu_bf, wd_bf):
    v = pl.program_id(0)
    valid = v < nv_ref[0]
    prev = jnp.maximum(v - 1, 0)
    new_expert = (v == 0) | (ve_ref[v] != ve_ref[prev])
    new_tile = (v == 0) | (vt_ref[v] != vt_ref[prev])

    @pl.when(valid & new_expert)
    def _():
        wg_bf[...] = wg_ref[0].astype(BF16)
        wu_bf[...] = wu_ref[0].astype(BF16)
        wd_bf[...] = wd_ref[0].astype(BF16)

    def expert_rows():
        xb = _unpack_rows(xs_ref)
        gate = _dot(xb, wg_bf[...])
        up = _dot(xb, wu_bf[...])
        act = gate * jax.nn.sigmoid(gate) * up
        return _dot(act.astype(BF16), wd_bf[...])

    @pl.when(valid & new_tile)
    def _():
        _pack_rows(ys_ref, expert_rows())

    @pl.when(valid & jnp.logical_not(new_tile))
    def _():
        y = expert_rows()
        te = y.shape[0]
        row = lax.broadcasted_iota(jnp.int32, (te, 1), 0)
        mine = (row >= va_ref[v]) & (row < vb_ref[v])
        for c in range(PACK_ROWS):
            rows = pl.ds(c, te, stride=PACK_ROWS)
            ys_ref[rows, :] = jnp.where(mine, y[:, c * LANES:(c + 1) * LANES], ys_ref[rows, :])


def _expert_mlp(visit_tile, visit_expert, visit_lo, visit_hi, n_visits, xs, w_gate, w_up, w_down):
    te = EXPERT_TE
    d, ff = w_gate.shape[1:]
    blk = (te * PACK_ROWS, LANES)
    tile = lambda v, vt, *_: (vt[v], 0)
    wsel = lambda v, vt, ve, *_: (ve[v], 0, 0)
    return pl.pallas_call(
        _expert_kernel,
        grid_spec=pltpu.PrefetchScalarGridSpec(
            num_scalar_prefetch=5,
            grid=(visit_tile.shape[0],),
            in_specs=[pl.BlockSpec(blk, tile),
                      pl.BlockSpec((1, d, ff), wsel), pl.BlockSpec((1, d, ff), wsel),
                      pl.BlockSpec((1, ff, d), wsel)],
            out_specs=pl.BlockSpec(blk, tile),
            scratch_shapes=[pltpu.VMEM((d, ff), BF16), pltpu.VMEM((d, ff), BF16),
                            pltpu.VMEM((ff, d), BF16)]),
        out_shape=jax.ShapeDtypeStruct(xs.shape, ROW_DTYPE),
        compiler_params=_params("arbitrary"),
    )(visit_tile, visit_expert, visit_lo, visit_hi, n_visits, xs, w_gate, w_up, w_down)


def _combine_kernel(rdst_ref, rlen_ref, x1_ref, rinfo_ref, g2_ref, gpost_ref, ys_ref, o_ref,
                    gath_ref, sem):
    k = pl.program_id(0)
    nk = pl.num_programs(0)
    tm = x1_ref.shape[0]
    rows = 2 * tm
    slot = k % 2

    def gather_runs(tile, s):
        buf = gath_ref.at[s]

        def copy_piece(local, src, size):
            pltpu.make_async_copy(_token_rows(ys_ref, src, size), _token_rows(buf, local, size),
                                  sem.at[s]).start()

        _for_each_run_piece(rdst_ref, rlen_ref, tile, tm, copy_piece)

    @pl.when(k == 0)
    def _():
        gather_runs(0, 0)

    @pl.when(k + 1 < nk)
    def _():
        gather_runs(k + 1, 1 - slot)

    buf = gath_ref.at[slot]
    pltpu.make_async_copy(_token_rows(ys_ref, 0, rows), buf, sem.at[slot]).wait()
    g = _unpack_rows(buf)
    info = rinfo_ref[...]
    col = lax.broadcasted_iota(jnp.int32, (tm, rows), 1).astype(F32)
    pick = jnp.where(col == info[:, 2:3], info[:, 4:5],
                     jnp.where(col == info[:, 3:4], info[:, 5:6], 0.0)).astype(BF16)
    fx = _dot(pick, g)
    o_ref[...] = x1_ref[...] + g2_ref[0, 0] * (_rms(fx) * gpost_ref[...])


def _combine(run_dst, run_len, x1, rinfo, mod, gpost, ys, seq):
    t, d = x1.shape
    tm = MOVE_TM
    tpb = seq // tm
    batch_of = lambda i: i // tpb
    return pl.pallas_call(
        _combine_kernel,
        grid_spec=pltpu.PrefetchScalarGridSpec(
            num_scalar_prefetch=2,
            grid=(t // tm,),
            in_specs=[pl.BlockSpec((tm, d), lambda i, *_: (i, 0)),
                      pl.BlockSpec((tm, LANES), lambda i, *_: (i, 0)),
                      _mod_spec(MOD_G2, batch_of)(d),
                      pl.BlockSpec((1, d), lambda i, *_: (0, 0)),
                      pl.BlockSpec(memory_space=pl.ANY)],
            out_specs=pl.BlockSpec((tm, d), lambda i, *_: (i, 0)),
            scratch_shapes=[pltpu.VMEM((2, 2 * tm * PACK_ROWS, LANES), ROW_DTYPE),
                            pltpu.SemaphoreType.DMA((2,))]),
        out_shape=jax.ShapeDtypeStruct((t, d), F32),
        compiler_params=_params("arbitrary"),
    )(run_dst, run_len, x1, rinfo, mod, gpost, ys)


def _rope_tables(seq):
    pos = np.arange(seq)
    row = (pos // GRID_W).astype(np.float32)
    col = (pos % GRID_W).astype(np.float32)
    axis_dim = HEAD_DIM // 2
    inv_freq = (ROPE_THETA ** (-np.arange(0, axis_dim, 2, dtype=np.float32) / axis_dim)).astype(
        np.float32)
    ang = np.concatenate([row[:, None] * inv_freq, col[:, None] * inv_freq], axis=-1)
    pair = (np.arange(LANES) % HEAD_DIM) // 2
    cos = np.cos(ang)[:, pair]
    sin = np.sin(ang)[:, pair]
    even = (np.arange(LANES) % 2) == 0
    tables = (cos, np.where(even, -sin, 0.0), np.where(even, 0.0, sin))
    tables = tables + tuple(tb.T for tb in tables)
    return tuple(jnp.asarray(tb, F32) for tb in tables)


def _segment_ones(n):
    seg = np.arange(n) // HEAD_DIM
    return jnp.asarray(seg[:, None] == seg[None, :], BF16)


def kernel(x, c, ctx, c_ctx, w_mod, b_mod, attn_pre_norm, attn_post_norm, w_in, a_sink,
           b_q_norm, b_k_norm, a_out_norm, b_out_norm, w_out, ffn_pre_norm, ffn_post_norm,
           w_group, b_group, w_router, b_router, w_gate, w_up, w_down):
    batch, seq, d = x.shape
    ctx_len = ctx.shape[1]
    assert w_mod.shape[0] == 1, "single-layer stack only (context stream is never updated)"
    assert seq % ATTN_A_TQ == 0 and seq >= ATTN_A_TQ + 2 * WINDOW
    assert seq % PROJ_TM == 0 and seq % ATTN_B_TQ == 0 and seq % OUT_TM == 0 and seq % MOVE_TM == 0
    t = batch * seq
    nq = d // 2
    nkv = nq // KV_GROUP
    assert nkv == LANES and w_in.shape[2] == 2 * nq + 4 * nkv

    cc = jnp.concatenate([c, c_ctx[None, :], jnp.zeros((16 - batch - 1, d), F32)], axis=0)
    mod = _modulation(cc, w_mod[0], b_mod[0]).reshape(cc.shape[0], 6, 1, d)

    x2 = x.reshape(t, d)
    c2 = ctx.reshape(batch * ctx_len, d)
    gpre = attn_pre_norm[0].reshape(1, d)
    w_in_bf = w_in[0].astype(BF16)
    qn = jnp.broadcast_to(jnp.tile(b_q_norm[0], nq // HEAD_DIM)[:, None], (nq, PROJ_TM))
    kn = jnp.tile(b_k_norm[0], nkv // HEAD_DIM).reshape(1, nkv)
    seg_k = _segment_ones(nkv)
    qat, ka, vat, qbt, kb, vbt, stats = _project_latents(
        x2, mod, gpre, w_in_bf, _rope_tables(seq), qn, kn, seg_k, seq)
    kca, vcat, kcb, vcbt, ctx_stats = _project_context(
        c2, mod, batch, gpre, w_in_bf, kn, seg_k, ctx_len)

    q_sq = jnp.max(stats[:, 0, 0])
    k_sq = jnp.maximum(jnp.max(stats[:, 0, 1]), jnp.max(ctx_stats[:, 0, 1]))
    bound_a = jnp.maximum(1.01 * jnp.sqrt(q_sq * k_sq), jnp.max(a_sink[0]) * LOG2_E).reshape(1)
    oa = _attention_a(bound_a, a_sink[0], qat, ka, vat, kca, vcat, batch, seq, ctx_len)
    score_bound = (1.01 * HEAD_DIM ** 0.5 * LOG2_E
                   * jnp.max(jnp.abs(b_q_norm[0])) * jnp.max(jnp.abs(b_k_norm[0]))).reshape(1)
    ob = _attention_b(score_bound, qbt, kb, vbt, kcb, vcbt, batch, seq, ctx_len)

    w_out_bf = w_out[0].astype(BF16)
    lane_pad = LANES - N_EXPERTS - N_GROUPS
    w_r = jnp.pad(jnp.concatenate([w_router[0], w_group[0]], axis=1), ((0, 0), (0, lane_pad)))
    w_r_hi = w_r.astype(BF16)
    w_r_lo = (w_r - w_r_hi.astype(F32)).astype(BF16)
    w_r2 = jnp.concatenate([w_r_hi, w_r_lo], axis=1)
    b_r = jnp.pad(jnp.concatenate([b_router[0], b_group[0]]), (0, lane_pad)).reshape(1, LANES)
    x1, h2, rinfo, rt, tcarry, tcnt, counts = _out_and_route(
        oa, ob, x2, mod, a_out_norm[0].reshape(1, nq), b_out_norm[0].reshape(1, nq),
        attn_post_norm[0].reshape(1, d), ffn_pre_norm[0].reshape(1, d),
        w_out_bf, w_r2, b_r, seq)

    te = EXPERT_TE
    assert (2 * t) % te == 0
    cnt = counts[0, :N_EXPERTS].astype(jnp.int32)
    ends = jnp.cumsum(cnt)
    starts = ends - cnt
    run_dst = (starts[None, :] + tcarry[:, 0, :N_EXPERTS].astype(jnp.int32)).reshape(-1)
    run_len = tcnt[:, 0, :N_EXPERTS].astype(jnp.int32).reshape(-1)
    first_tile = starts // te
    n_vis = jnp.where(cnt > 0, (ends - 1) // te - first_tile + 1, 0)
    vis_end = jnp.cumsum(n_vis)
    n_visits = vis_end[-1]
    v = jnp.minimum(jnp.arange(2 * t // te + N_EXPERTS, dtype=jnp.int32), n_visits - 1)
    v_expert = jnp.sum(vis_end[None, :] <= v[:, None], axis=1).astype(jnp.int32)
    pick = (v_expert[:, None] == jnp.arange(N_EXPERTS)[None, :]).astype(jnp.int32)
    of_expert = lambda table: jnp.sum(pick * table[None, :], axis=1)
    v_tile = of_expert(first_tile) + v - of_expert(vis_end - n_vis)
    v_lo = jnp.maximum(of_expert(starts) - v_tile * te, 0)
    v_hi = jnp.minimum(of_expert(ends) - v_tile * te, te)

    xs = _dispatch(run_dst, run_len, h2, rt)
    ys = _expert_mlp(v_tile, v_expert, v_lo, v_hi, n_visits.reshape(1), xs,
                     w_gate[0], w_up[0], w_down[0])
    out = _combine(run_dst, run_len, x1, rinfo, mod, ffn_post_norm[0].reshape(1, d), ys, seq)
    return out.reshape(batch, seq, d)
```

```python
import functools

import jax
import jax.numpy as jnp
import numpy as np
from jax import lax
from jax.experimental import pallas as pl
from jax.experimental.pallas import tpu as pltpu

F32 = jnp.float32
BF16 = jnp.bfloat16

GRID_W = 64
HEAD_DIM = 64
KV_GROUP = 4
WINDOW = 128
ROPE_THETA = 10000.0
N_GROUPS = 4
EXPERTS_PER_GROUP = 8
N_EXPERTS = N_GROUPS * EXPERTS_PER_GROUP
EPS = 1e-6
NEG_INF = -1e30
LOG2_E = 1.4426950408889634
SAFE_SOFTMAX_SHIFT = 40.0

LANES = 128
V7X_VMEM_LIMIT = 56 * 1024 * 1024

PROJ_TM = 1024
ATTN_A_TQ = 256
ATTN_B_TQ = 256
OUT_TM = 512
EXPERT_TE = 512
MOVE_TM = OUT_TM


def _params(*sem):
    return pltpu.CompilerParams(dimension_semantics=sem, vmem_limit_bytes=V7X_VMEM_LIMIT)


def _dot(a, b):
    return jnp.dot(a, b, preferred_element_type=F32)


def _rms(x):
    return x * lax.rsqrt(jnp.mean(x * x, axis=-1, keepdims=True) + EPS)


def _split_bf16(x):
    hi = x.astype(BF16)
    lo = (x - hi.astype(F32)).astype(BF16)
    return hi, lo


def _mod_kernel(c_ref, w_ref, b_ref, o_ref):
    cc = c_ref[...]
    s = cc * jax.nn.sigmoid(cc)
    s_hi, s_lo = _split_bf16(s)
    w_hi, w_lo = _split_bf16(w_ref[...])
    o_ref[...] = _dot(s_hi, w_hi) + _dot(s_lo, w_hi) + _dot(s_hi, w_lo) + b_ref[...]


def _modulation(cc, w_mod, b_mod):
    rows, d = cc.shape
    n = w_mod.shape[1]
    bn = 1024
    return pl.pallas_call(
        _mod_kernel,
        grid=(n // bn,),
        in_specs=[pl.BlockSpec((rows, d), lambda i: (0, 0)),
                  pl.BlockSpec((d, bn), lambda i: (0, i)),
                  pl.BlockSpec((1, bn), lambda i: (0, i))],
        out_specs=pl.BlockSpec((rows, bn), lambda i: (0, i)),
        out_shape=jax.ShapeDtypeStruct((rows, n), F32),
        compiler_params=_params("arbitrary"),
    )(cc, w_mod, b_mod.reshape(1, n))


def _rope(x, cos, sin_a, sin_b):
    return x * cos + pltpu.roll(x, LANES - 1, 1) * sin_a + pltpu.roll(x, 1, 1) * sin_b


def _head_norm(x, seg_ref, gain):
    ss = _dot((x * x).astype(BF16), seg_ref[...])
    return x * lax.rsqrt(ss * (1.0 / HEAD_DIM) + EPS) * gain


def _max_head_sq_norm(x, seg_ref):
    ss = _dot((x * x).astype(BF16), seg_ref[...])
    return jnp.max(jnp.max(ss, axis=1, keepdims=True), axis=0, keepdims=True)


def _norm_stats(q_sq, k_sq):
    lane = lax.broadcasted_iota(jnp.int32, (1, LANES), 1)
    zero = jnp.zeros((1, LANES), F32)
    return jnp.where(lane == 0, q_sq, zero) + jnp.where(lane == 1, k_sq, zero)


def _rope_t(xt, cos_t, sin_a_t, sin_b_t):
    return (xt * cos_t + pltpu.roll(xt, LANES - 1, 0) * sin_a_t
            + pltpu.roll(xt, 1, 0) * sin_b_t)


def _proj_kernel(x_ref, sc_ref, sh_ref, gpre_ref, w_ref, cos_ref, sa_ref, sb_ref,
                 cos_t_ref, sa_t_ref, sb_t_ref, qn_ref, kn_ref, seg_k_ref,
                 qat_ref, ka_ref, vat_ref, qbt_ref, kb_ref, vbt_ref, stats_ref):
    h = _rms(x_ref[...]) * gpre_ref[...] * (1.0 + sc_ref[0, 0]) + sh_ref[0, 0]
    p = _dot(h.astype(BF16), w_ref[...])
    cos, sa, sb = cos_ref[...], sa_ref[...], sb_ref[...]
    q_scale = HEAD_DIM ** -0.5 * LOG2_E
    cos_t, sa_t, sb_t = cos_t_ref[...] * q_scale, sa_t_ref[...] * q_scale, sb_t_ref[...] * q_scale
    nq = qat_ref.shape[0]
    q_sq = None
    for c in range(nq // LANES):
        xt = p[:, c * LANES:(c + 1) * LANES].T
        qat_ref[c * LANES:(c + 1) * LANES, :] = _rope_t(xt, cos_t, sa_t, sb_t).astype(BF16)
        sq = xt * xt
        for head in range(LANES // HEAD_DIM):
            norm = jnp.sum(sq[head * HEAD_DIM:(head + 1) * HEAD_DIM, :], axis=0, keepdims=True)
            q_sq = norm if q_sq is None else jnp.maximum(q_sq, norm)
    q_sq = jnp.max(q_sq, axis=1, keepdims=True) * (q_scale * q_scale)
    o = nq
    ka_ref[...] = _rope(p[:, o:o + LANES], cos, sa, sb).astype(BF16)
    vat_ref[...] = p[:, o + LANES:o + 2 * LANES].T.astype(BF16)
    stats_ref[0] = _norm_stats(q_sq, _max_head_sq_norm(p[:, o:o + LANES], seg_k_ref))
    o += 2 * LANES
    for c in range(nq // LANES):
        xt = p[:, o + c * LANES:o + (c + 1) * LANES].T
        sq = xt * xt
        halves = []
        for head in range(LANES // HEAD_DIM):
            rows = slice(head * HEAD_DIM, (head + 1) * HEAD_DIM)
            ms = jnp.sum(sq[rows, :], axis=0, keepdims=True) * (1.0 / HEAD_DIM)
            halves.append(xt[rows, :] * lax.rsqrt(ms + EPS))
        qn = jnp.concatenate(halves, axis=0) * qn_ref[c * LANES:(c + 1) * LANES, :]
        qbt_ref[c * LANES:(c + 1) * LANES, :] = _rope_t(qn, cos_t, sa_t, sb_t).astype(BF16)
    o += nq
    kb = _head_norm(p[:, o:o + LANES], seg_k_ref, kn_ref[...])
    kb_ref[...] = _rope(kb, cos, sa, sb).astype(BF16)
    vbt_ref[...] = p[:, o + LANES:o + 2 * LANES].T.astype(BF16)


def _ctx_proj_kernel(x_ref, sc_ref, sh_ref, gpre_ref, wa_ref, wb_ref, kn_ref, seg_k_ref,
                     ka_ref, vat_ref, kb_ref, vbt_ref, stats_ref):
    h = (_rms(x_ref[...]) * gpre_ref[...] * (1.0 + sc_ref[0, 0]) + sh_ref[0, 0]).astype(BF16)
    pa = _dot(h, wa_ref[...])
    pb = _dot(h, wb_ref[...])
    stats_ref[0] = _norm_stats(0.0, _max_head_sq_norm(pa[:, 0:LANES], seg_k_ref))
    ka_ref[...] = pa[:, 0:LANES].astype(BF16)
    vat_ref[...] = pa[:, LANES:2 * LANES].T.astype(BF16)
    kb_ref[...] = _head_norm(pb[:, 0:LANES], seg_k_ref, kn_ref[...]).astype(BF16)
    vbt_ref[...] = pb[:, LANES:2 * LANES].T.astype(BF16)


def _mod_spec(chunk, row_of_step):
    return lambda d: pl.BlockSpec((1, 1, 1, d), lambda i, *_: (row_of_step(i), chunk, 0, 0))


MOD_SH1, MOD_SC1, MOD_G1, MOD_SH2, MOD_SC2, MOD_G2 = range(6)


def _project_latents(x2, mod, gpre, w_in, tables, qn, kn, seg_k, seq):
    t, d = x2.shape
    tm = PROJ_TM
    tpb = seq // tm
    nq = qn.shape[0]
    const = lambda shape: pl.BlockSpec(shape, lambda i: (0,) * len(shape))
    batch_of = lambda i: i // tpb
    table = pl.BlockSpec((tm, LANES), lambda i: (i % tpb, 0))
    table_t = pl.BlockSpec((LANES, tm), lambda i: (0, i % tpb))
    k_spec = pl.BlockSpec((tm, LANES), lambda i: (i, 0))
    k_shape = jax.ShapeDtypeStruct((t, LANES), BF16)
    vt_spec = pl.BlockSpec((LANES, tm), lambda i: (0, i))
    vt_shape = jax.ShapeDtypeStruct((LANES, t), BF16)
    qt_spec = pl.BlockSpec((nq, tm), lambda i: (0, i))
    qt_shape = jax.ShapeDtypeStruct((nq, t), BF16)
    return pl.pallas_call(
        _proj_kernel,
        grid=(t // tm,),
        in_specs=[pl.BlockSpec((tm, d), lambda i: (i, 0)),
                  _mod_spec(MOD_SC1, batch_of)(d), _mod_spec(MOD_SH1, batch_of)(d), const((1, d)),
                  const(w_in.shape), table, table, table, table_t, table_t, table_t,
                  const((nq, tm)), const((1, LANES)), const(seg_k.shape)],
        out_specs=[qt_spec, k_spec, vt_spec, qt_spec, k_spec, vt_spec,
                   pl.BlockSpec((1, 1, LANES), lambda i: (i, 0, 0))],
        out_shape=[qt_shape, k_shape, vt_shape, qt_shape, k_shape, vt_shape,
                   jax.ShapeDtypeStruct((t // tm, 1, LANES), F32)],
        compiler_params=_params("arbitrary"),
    )(x2, mod, mod, gpre, w_in, *tables, qn, kn, seg_k)


def _project_context(c2, mod, ctx_row, gpre, w_in, kn, seg_k, ctx_len):
    t, d = c2.shape
    ctx_mod = lambda i: ctx_row
    nq = (w_in.shape[1] - 4 * LANES) // 2
    kv = 2 * LANES
    assert nq % kv == 0
    group_kv = lambda g: pl.BlockSpec((d, kv), lambda i: (0, (g * (nq + kv) + nq) // kv))
    const = lambda shape: pl.BlockSpec(shape, lambda i: (0,) * len(shape))
    k_spec = pl.BlockSpec((ctx_len, LANES), lambda i: (i, 0))
    k_shape = jax.ShapeDtypeStruct((t, LANES), BF16)
    vt_spec = pl.BlockSpec((LANES, ctx_len), lambda i: (0, i))
    vt_shape = jax.ShapeDtypeStruct((LANES, t), BF16)
    return pl.pallas_call(
        _ctx_proj_kernel,
        grid=(t // ctx_len,),
        in_specs=[pl.BlockSpec((ctx_len, d), lambda i: (i, 0)),
                  _mod_spec(MOD_SC1, ctx_mod)(d), _mod_spec(MOD_SH1, ctx_mod)(d),
                  const((1, d)), group_kv(0), group_kv(1), const((1, LANES)), const(seg_k.shape)],
        out_specs=[k_spec, vt_spec, k_spec, vt_spec,
                   pl.BlockSpec((1, 1, LANES), lambda i: (i, 0, 0))],
        out_shape=[k_shape, vt_shape, k_shape, vt_shape,
                   jax.ShapeDtypeStruct((t // ctx_len, 1, LANES), F32)],
        compiler_params=_params("arbitrary"),
    )(c2, mod, mod, gpre, w_in, w_in, kn, seg_k)


def _attend_t(w, k, kc, vt, vct, shift=None, bias=None, sink=None):
    st = _dot(k, w)
    sct = _dot(kc, w)
    if bias is not None:
        tq = bias.shape[1]
        st = jnp.concatenate([st[:, c * tq:(c + 1) * tq] + bias
                              for c in range(st.shape[1] // tq)], axis=1)
    if shift is None:
        shift = jnp.maximum(jnp.max(st, axis=0, keepdims=True),
                            jnp.max(sct, axis=0, keepdims=True))
        if sink is not None:
            shift = jnp.maximum(shift, sink)
    pt = jnp.exp2(st - shift)
    pct = jnp.exp2(sct - shift)
    denom = jnp.sum(pt, axis=0, keepdims=True) + jnp.sum(pct, axis=0, keepdims=True)
    if sink is not None:
        denom = denom + jnp.exp2(sink - shift)
    o2 = _dot(vt, pt.astype(BF16)) + _dot(vct, pct.astype(BF16))
    return o2, denom


def _all_heads_t(qt_ref, o_ref, attend):
    tq = qt_ref.shape[1]
    n_kv = LANES // HEAD_DIM
    zeros = jnp.zeros((HEAD_DIM, tq), BF16)
    outs = []
    for h in range(qt_ref.shape[0] // HEAD_DIM):
        g = h // KV_GROUP
        qh = qt_ref[h * HEAD_DIM:(h + 1) * HEAD_DIM, :]
        w = jnp.concatenate([zeros] * g + [qh] + [zeros] * (n_kv - 1 - g), axis=0)
        o2, denom = attend(h, w)
        outs.append(o2[g * HEAD_DIM:(g + 1) * HEAD_DIM, :] / denom)
    o_ref[...] = jnp.concatenate(outs, axis=0).T.astype(BF16)


def _all_heads_fused_t(qt_ref, o_ref, attend):
    tq = qt_ref.shape[1]
    n_heads = qt_ref.shape[0] // HEAD_DIM
    n_kv = LANES // HEAD_DIM
    rows = []
    for g in range(n_kv):
        heads = [qt_ref[h * HEAD_DIM:(h + 1) * HEAD_DIM, :] if h // KV_GROUP == g
                 else jnp.zeros((HEAD_DIM, tq), BF16) for h in range(n_heads)]
        rows.append(jnp.concatenate(heads, axis=1))
    w = jnp.concatenate(rows, axis=0)
    o2, denom = attend(0, w)
    o2 = o2 / denom
    outs = [o2[(h // KV_GROUP) * HEAD_DIM:(h // KV_GROUP + 1) * HEAD_DIM, h * tq:(h + 1) * tq]
            for h in range(n_heads)]
    o_ref[...] = jnp.concatenate(outs, axis=0).T.astype(BF16)


def _attn_b_kernel(bound_ref, qt_ref, k_ref, vt_ref, kc_ref, vct_ref, o_ref):
    k, kc, vt, vct = k_ref[...], kc_ref[...], vt_ref[...], vct_ref[...]
    bound = bound_ref[0]

    @pl.when(bound <= SAFE_SOFTMAX_SHIFT)
    def _():
        _all_heads_fused_t(qt_ref, o_ref, lambda h, w: _attend_t(w, k, kc, vt, vct, shift=bound))

    @pl.when(jnp.logical_not(bound <= SAFE_SOFTMAX_SHIFT))
    def _():
        _all_heads_t(qt_ref, o_ref, lambda h, w: _attend_t(w, k, kc, vt, vct))


def _attn_specs(nq, tq, seq, ctx_len):
    nqb = seq // tq
    return dict(
        qt=pl.BlockSpec((nq, tq), lambda b, i: (0, b * nqb + i)),
        k=pl.BlockSpec((seq, LANES), lambda b, i: (b, 0)),
        vt=pl.BlockSpec((LANES, seq), lambda b, i: (0, b)),
        kc=pl.BlockSpec((ctx_len, LANES), lambda b, i: (b, 0)),
        vct=pl.BlockSpec((LANES, ctx_len), lambda b, i: (0, b)),
        out=pl.BlockSpec((tq, nq), lambda b, i: (b * nqb + i, 0)))


def _attention_b(score_bound, qbt, kb, vbt, kcb, vcbt, batch, seq, ctx_len):
    nq, t = qbt.shape
    tq = ATTN_B_TQ
    sp = _attn_specs(nq, tq, seq, ctx_len)
    return pl.pallas_call(
        _attn_b_kernel,
        grid=(batch, seq // tq),
        in_specs=[pl.BlockSpec(memory_space=pltpu.SMEM),
                  sp["qt"], sp["k"], sp["vt"], sp["kc"], sp["vct"]],
        out_specs=sp["out"],
        out_shape=jax.ShapeDtypeStruct((t, nq), BF16),
        compiler_params=_params("arbitrary", "arbitrary"),
    )(score_bound, qbt, kb, vbt, kcb, vcbt)


def _attn_a_kernel(bound_ref, sink_ref, qt_ref, k_ref, vt_ref, kc_ref, vct_ref, bias_ref, o_ref,
                   *, seq):
    i = pl.program_id(1)
    tq = qt_ref.shape[1]
    n_heads = qt_ref.shape[0] // HEAD_DIM
    win = tq + 2 * WINDOW
    start = pl.multiple_of(jnp.clip(i * tq - WINDOW, 0, seq - win), WINDOW)
    k = k_ref[pl.ds(start, win), :]
    vt = vt_ref[:, pl.ds(start, win)]
    kc, vct = kc_ref[...], vct_ref[...]
    sinks = [sink_ref[h] * LOG2_E for h in range(n_heads)]

    bound = bound_ref[0]
    small = bound <= SAFE_SOFTMAX_SHIFT

    @pl.when(small)
    def _():
        sink_row = jnp.concatenate([jnp.full((1, tq), s, F32) for s in sinks], axis=1)
        _all_heads_fused_t(qt_ref, o_ref, lambda h, w: _attend_t(
            w, k, kc, vt, vct, shift=bound, bias=bias_ref[0], sink=sink_row))

    @pl.when(jnp.logical_not(small))
    def _():
        _all_heads_t(qt_ref, o_ref, lambda h, w: _attend_t(
            w, k, kc, vt, vct, bias=bias_ref[0], sink=sinks[h]))


def _band_bias(tq):
    win = tq + 2 * WINDOW
    r = np.arange(win)[:, None]
    j = np.arange(tq)[None, :]
    tables = [np.where(np.abs(off + r - j) <= WINDOW, 0.0, NEG_INF)
              for off in (0, -WINDOW, -2 * WINDOW)]
    return jnp.asarray(np.stack(tables), F32)


def _attention_a(score_bound, sink, qat, ka, vat, kca, vcat, batch, seq, ctx_len):
    nq, t = qat.shape
    tq = ATTN_A_TQ
    nqb = seq // tq
    win = tq + 2 * WINDOW
    assert nqb >= 2 and tq >= WINDOW
    sp = _attn_specs(nq, tq, seq, ctx_len)
    which = lambda b, i: (jnp.where(i == 0, 0, jnp.where(i == nqb - 1, 2, 1)), 0, 0)
    return pl.pallas_call(
        functools.partial(_attn_a_kernel, seq=seq),
        grid=(batch, nqb),
        in_specs=[pl.BlockSpec(memory_space=pltpu.SMEM), pl.BlockSpec(memory_space=pltpu.SMEM),
                  sp["qt"], sp["k"], sp["vt"], sp["kc"], sp["vct"],
                  pl.BlockSpec((1, win, tq), which)],
        out_specs=sp["out"],
        out_shape=jax.ShapeDtypeStruct((t, nq), BF16),
        compiler_params=_params("arbitrary", "arbitrary"),
    )(score_bound, sink, qat, ka, vat, kca, vcat, _band_bias(tq))


def _out_kernel(oa_ref, ob_ref, x_ref, g1_ref, sc2_ref, sh2_ref, ga_ref, gb_ref, gpost_ref,
                gpre2_ref, woa_ref, wob_ref, wr_ref, br_ref,
                x1_ref, h2_ref, rinfo_ref, rt_ref, tcarry_ref, tcnt_ref, cnt_ref, carry_ref):
    step = pl.program_id(0)

    @pl.when(step == 0)
    def _():
        carry_ref[...] = jnp.zeros_like(carry_ref)

    na = _rms(oa_ref[...].astype(F32)) * ga_ref[...]
    nb = _rms(ob_ref[...].astype(F32)) * gb_ref[...]
    ox = _dot(na.astype(BF16), woa_ref[...]) + _dot(nb.astype(BF16), wob_ref[...])
    x1 = x_ref[...] + g1_ref[0, 0] * (_rms(ox) * gpost_ref[...])
    x1_ref[...] = x1
    h2 = _rms(x1) * gpre2_ref[...] * (1.0 + sc2_ref[0, 0]) + sh2_ref[0, 0]
    h_hi, h_lo = _split_bf16(h2)
    h2_ref[...] = h_hi

    both = _dot(h_hi, wr_ref[...])
    logits = (both[:, :LANES] + _dot(h_lo, wr_ref[:, :LANES]) + both[:, LANES:]
              + br_ref[...])
    tm = logits.shape[0]
    lt = logits.T
    row = lax.broadcasted_iota(jnp.int32, lt.shape, 0)
    rowf = row.astype(F32)
    big = jnp.float32(1e9)
    ninf = jnp.float32(-jnp.inf)
    colmax = lambda v: jnp.max(v, axis=0, keepdims=True)
    colmin = lambda v: jnp.min(v, axis=0, keepdims=True)
    colsum = lambda v: jnp.sum(v, axis=0, keepdims=True)

    gmask = (row >= N_EXPERTS) & (row < N_EXPERTS + N_GROUPS)
    lg = jnp.where(gmask, lt, ninf)
    gmax = colmax(lg)
    gidx = colmin(jnp.where(lg == gmax, rowf, big)) - N_EXPERTS
    g_w = 1.0 / colsum(jnp.exp(lg - gmax))
    row_group = (row // EXPERTS_PER_GROUP).astype(F32)
    emask = (row < N_EXPERTS) & (row_group == gidx)
    le = jnp.where(emask, lt, ninf)
    m1 = colmax(le)
    i1 = colmin(jnp.where(le == m1, rowf, big))
    le2 = jnp.where(rowf == i1, ninf, le)
    m2 = colmax(le2)
    i2 = colmin(jnp.where(le2 == m2, rowf, big))
    e2 = jnp.exp(m2 - m1)
    w0 = g_w / (1.0 + e2)
    w1 = g_w * e2 / (1.0 + e2)

    hit1 = rowf == i1
    hit2 = rowf == i2
    onehot = jnp.where(hit1, 1.0, jnp.where(hit2, 1.0, 0.0)).astype(F32)
    r = lax.broadcasted_iota(jnp.int32, (tm, tm), 0)
    c = lax.broadcasted_iota(jnp.int32, (tm, tm), 1)
    earlier = jnp.where(r < c, 1.0, 0.0).astype(BF16)
    within = _dot(onehot.astype(BF16), earlier)
    tile_cnt = jnp.broadcast_to(jnp.sum(onehot, axis=1, keepdims=True), (LANES, LANES))
    er = lax.broadcasted_iota(jnp.int32, (LANES, LANES), 0)
    ec = lax.broadcasted_iota(jnp.int32, (LANES, LANES), 1)
    below = jnp.where(er > ec, 1.0, 0.0).astype(BF16)
    cnt_hi = jnp.floor(tile_cnt * (1.0 / 32.0))
    cnt_lo = tile_cnt - 32.0 * cnt_hi
    run_start = 32.0 * _dot(below, cnt_hi.astype(BF16)) + _dot(below, cnt_lo.astype(BF16))
    local = within + run_start[:, 0:1]
    pos0 = colsum(jnp.where(hit1, local, 0.0))
    pos1 = colsum(jnp.where(hit2, local, 0.0))
    cnt_row = tile_cnt.T[0:1, :]
    tcarry_ref[0] = carry_ref[...]
    tcnt_ref[0] = cnt_row
    carry_ref[...] += cnt_row
    cnt_ref[...] = carry_ref[...]

    fields = jnp.concatenate([i1, i2, pos0, pos1, w0, w1, jnp.zeros((2, tm), F32)], axis=0)
    rt_ref[...] = fields
    rinfo_ref[...] = jnp.concatenate(
        [fields, jnp.zeros((LANES - 8, tm), F32)], axis=0).T


def _out_and_route(oa, ob, x2, mod, ga, gb, gpost, gpre2, w_out, wr, br, seq):
    t, d = x2.shape
    tm = OUT_TM
    tpb = seq // tm
    nq = oa.shape[1]
    const = lambda shape: pl.BlockSpec(shape, lambda i: (0,) * len(shape))
    batch_of = lambda i: i // tpb
    rows = lambda n: pl.BlockSpec((tm, n), lambda i: (i, 0))
    per_tile = pl.BlockSpec((1, 1, LANES), lambda i: (i, 0, 0))
    w_half = lambda g: pl.BlockSpec((nq, d), lambda i: (g, 0))
    return pl.pallas_call(
        _out_kernel,
        grid=(t // tm,),
        in_specs=[rows(nq), rows(nq), rows(d),
                  _mod_spec(MOD_G1, batch_of)(d), _mod_spec(MOD_SC2, batch_of)(d),
                  _mod_spec(MOD_SH2, batch_of)(d),
                  const((1, nq)), const((1, nq)), const((1, d)), const((1, d)),
                  w_half(0), w_half(1), const(wr.shape),
                  const((1, LANES))],
        out_specs=[rows(d), rows(d), rows(LANES), pl.BlockSpec((8, tm), lambda i: (0, i)),
                   per_tile, per_tile, const((1, LANES))],
        out_shape=[jax.ShapeDtypeStruct((t, d), F32), jax.ShapeDtypeStruct((t, d), BF16),
                   jax.ShapeDtypeStruct((t, LANES), F32), jax.ShapeDtypeStruct((8, t), F32),
                   jax.ShapeDtypeStruct((t // tm, 1, LANES), F32),
                   jax.ShapeDtypeStruct((t // tm, 1, LANES), F32),
                   jax.ShapeDtypeStruct((1, LANES), F32)],
        scratch_shapes=[pltpu.VMEM((1, LANES), F32)],
        compiler_params=_params("arbitrary"),
    )(oa, ob, x2, mod, mod, mod, ga, gb, gpost, gpre2, w_out, w_out, wr, br)


PACK_ROWS = 8
ROW_DTYPE = F32


def _pack_rows(ref, x):
    n = x.shape[0]
    for c in range(PACK_ROWS):
        ref[pl.ds(c, n, stride=PACK_ROWS), :] = x[:, c * LANES:(c + 1) * LANES]


def _unpack_rows(ref):
    n = ref.shape[0] // PACK_ROWS
    return jnp.concatenate(
        [ref[pl.ds(c, n, stride=PACK_ROWS), :].astype(BF16) for c in range(PACK_ROWS)], axis=1)


def _for_each_run_piece(rdst_ref, rlen_ref, tile, max_len, fn):
    n_bits = max_len.bit_length()

    def run(e, local):
        length = rlen_ref[tile * N_EXPERTS + e]
        dst = rdst_ref[tile * N_EXPERTS + e]
        for b in range(n_bits):
            size = 1 << b

            @pl.when(((length >> b) & 1) == 1)
            def _():
                done = length & (size - 1)
                fn(local + done, dst + done, size)
        return local + length

    lax.fori_loop(0, N_EXPERTS, run, 0)


def _token_rows(ref, row0, n_rows):
    start = row0 * PACK_ROWS
    if not isinstance(start, int):
        start = pl.multiple_of(start, PACK_ROWS)
    return ref.at[pl.ds(start, n_rows * PACK_ROWS)]


def _dispatch_kernel(rdst_ref, rlen_ref, h_ref, rt_ref, xs_ref, sorted_ref, sem):
    k = pl.program_id(0)
    nk = pl.num_programs(0)
    tm = h_ref.shape[0]
    rows = 2 * tm
    slot = k % 2

    def wait_slot(s):
        pltpu.make_async_copy(sorted_ref.at[s], _token_rows(xs_ref, 0, rows), sem.at[s]).wait()

    @pl.when(k >= 2)
    def _():
        wait_slot(slot)

    pos0 = rt_ref[2:3, :]
    pos1 = rt_ref[3:4, :]
    r = lax.broadcasted_iota(jnp.int32, (rows, tm), 0).astype(F32)
    perm = jnp.where((r == pos0) | (r == pos1), 1.0, 0.0).astype(BF16)
    srt = _dot(perm, h_ref[...].astype(BF16))
    buf = sorted_ref.at[slot]
    _pack_rows(buf, srt)

    def copy_piece(local, dst, size):
        pltpu.make_async_copy(_token_rows(buf, local, size), _token_rows(xs_ref, dst, size),
                              sem.at[slot]).start()

    _for_each_run_piece(rdst_ref, rlen_ref, k, tm, copy_piece)

    @pl.when(k == nk - 1)
    def _():
        wait_slot(slot)

        @pl.when(nk >= 2)
        def _():
            wait_slot(1 - slot)


def _dispatch(run_dst, run_len, h2, rt):
    t, d = h2.shape
    assert d == PACK_ROWS * LANES
    tm = MOVE_TM
    return pl.pallas_call(
        _dispatch_kernel,
        grid_spec=pltpu.PrefetchScalarGridSpec(
            num_scalar_prefetch=2,
            grid=(t // tm,),
            in_specs=[pl.BlockSpec((tm, d), lambda i, *_: (i, 0)),
                      pl.BlockSpec((8, tm), lambda i, *_: (0, i))],
            out_specs=pl.BlockSpec(memory_space=pl.ANY),
            scratch_shapes=[pltpu.VMEM((2, 2 * tm * PACK_ROWS, LANES), ROW_DTYPE),
                            pltpu.SemaphoreType.DMA((2,))]),
        out_shape=jax.ShapeDtypeStruct((2 * t * PACK_ROWS, LANES), ROW_DTYPE),
        compiler_params=_params("arbitrary"),
    )(run_dst, run_len, h2, rt)


def _expert_kernel(vt_ref, ve_ref, va_ref, vb_ref, nv_ref, xs_ref, wg_ref, wu_ref, wd_ref, ys_ref,
                   wg_bf, wu_bf, wd_bf):
    v = pl.program_id(0)
    valid = v < nv_ref[0]
    prev = jnp.maximum(v - 1, 0)
    new_expert = (v == 0) | (ve_ref[v] != ve_ref[prev])
    new_tile = (v == 0) | (vt_ref[v] != vt_ref[prev])

    @pl.when(valid & new_expert)
    def _():
        wg_bf[...] = wg_ref[0].astype(BF16)
        wu_bf[...] = wu_ref[0].astype(BF16)
        wd_bf[...] = wd_ref[0].astype(BF16)

    def expert_rows():
        xb = _unpack_rows(xs_ref)
        gate = _dot(xb, wg_bf[...])
        up = _dot(xb, wu_bf[...])
        act = gate * jax.nn.sigmoid(gate) * up
        return _dot(act.astype(BF16), wd_bf[...])

    @pl.when(valid & new_tile)
    def _():
        _pack_rows(ys_ref, expert_rows())

    @pl.when(valid & jnp.logical_not(new_tile))
    def _():
        y = expert_rows()
        te = y.shape[0]
        row = lax.broadcasted_iota(jnp.int32, (te, 1), 0)
        mine = (row >= va_ref[v]) & (row < vb_ref[v])
        for c in range(PACK_ROWS):
            rows = pl.ds(c, te, stride=PACK_ROWS)
            ys_ref[rows, :] = jnp.where(mine, y[:, c * LANES:(c + 1) * LANES], ys_ref[rows, :])


def _expert_mlp(visit_tile, visit_expert, visit_lo, visit_hi, n_visits, xs, w_gate, w_up, w_down):
    te = EXPERT_TE
    d, ff = w_gate.shape[1:]
    blk = (te * PACK_ROWS, LANES)
    tile = lambda v, vt, *_: (vt[v], 0)
    wsel = lambda v, vt, ve, *_: (ve[v], 0, 0)
    return pl.pallas_call(
        _expert_kernel,
        grid_spec=pltpu.PrefetchScalarGridSpec(
            num_scalar_prefetch=5,
            grid=(visit_tile.shape[0],),
            in_specs=[pl.BlockSpec(blk, tile),
                      pl.BlockSpec((1, d, ff), wsel), pl.BlockSpec((1, d, ff), wsel),
                      pl.BlockSpec((1, ff, d), wsel)],
            out_specs=pl.BlockSpec(blk, tile),
            scratch_shapes=[pltpu.VMEM((d, ff), BF16), pltpu.VMEM((d, ff), BF16),
                            pltpu.VMEM((ff, d), BF16)]),
        out_shape=jax.ShapeDtypeStruct(xs.shape, ROW_DTYPE),
        compiler_params=_params("arbitrary"),
    )(visit_tile, visit_expert, visit_lo, visit_hi, n_visits, xs, w_gate, w_up, w_down)


def _combine_kernel(rdst_ref, rlen_ref, x1_ref, rinfo_ref, g2_ref, gpost_ref, ys_ref, o_ref,
                    gath_ref, sem):
    k = pl.program_id(0)
    nk = pl.num_programs(0)
    tm = x1_ref.shape[0]
    rows = 2 * tm
    slot = k % 2

    def gather_runs(tile, s):
        buf = gath_ref.at[s]

        def copy_piece(local, src, size):
            pltpu.make_async_copy(_token_rows(ys_ref, src, size), _token_rows(buf, local, size),
                                  sem.at[s]).start()

        _for_each_run_piece(rdst_ref, rlen_ref, tile, tm, copy_piece)

    @pl.when(k == 0)
    def _():
        gather_runs(0, 0)

    @pl.when(k + 1 < nk)
    def _():
        gather_runs(k + 1, 1 - slot)

    buf = gath_ref.at[slot]
    pltpu.make_async_copy(_token_rows(ys_ref, 0, rows), buf, sem.at[slot]).wait()
    g = _unpack_rows(buf)
    info = rinfo_ref[...]
    col = lax.broadcasted_iota(jnp.int32, (tm, rows), 1).astype(F32)
    pick = jnp.where(col == info[:, 2:3], info[:, 4:5],
                     jnp.where(col == info[:, 3:4], info[:, 5:6], 0.0)).astype(BF16)
    fx = _dot(pick, g)
    o_ref[...] = x1_ref[...] + g2_ref[0, 0] * (_rms(fx) * gpost_ref[...])


def _combine(run_dst, run_len, x1, rinfo, mod, gpost, ys, seq):
    t, d = x1.shape
    tm = MOVE_TM
    tpb = seq // tm
    batch_of = lambda i: i // tpb
    return pl.pallas_call(
        _combine_kernel,
        grid_spec=pltpu.PrefetchScalarGridSpec(
            num_scalar_prefetch=2,
            grid=(t // tm,),
            in_specs=[pl.BlockSpec((tm, d), lambda i, *_: (i, 0)),
                      pl.BlockSpec((tm, LANES), lambda i, *_: (i, 0)),
                      _mod_spec(MOD_G2, batch_of)(d),
                      pl.BlockSpec((1, d), lambda i, *_: (0, 0)),
                      pl.BlockSpec(memory_space=pl.ANY)],
            out_specs=pl.BlockSpec((tm, d), lambda i, *_: (i, 0)),
            scratch_shapes=[pltpu.VMEM((2, 2 * tm * PACK_ROWS, LANES), ROW_DTYPE),
                            pltpu.SemaphoreType.DMA((2,))]),
        out_shape=jax.ShapeDtypeStruct((t, d), F32),
        compiler_params=_params("arbitrary"),
    )(run_dst, run_len, x1, rinfo, mod, gpost, ys)


def _rope_tables(seq):
    pos = np.arange(seq)
    row = (pos // GRID_W).astype(np.float32)
    col = (pos % GRID_W).astype(np.float32)
    axis_dim = HEAD_DIM // 2
    inv_freq = (ROPE_THETA ** (-np.arange(0, axis_dim, 2, dtype=np.float32) / axis_dim)).astype(
        np.float32)
    ang = np.concatenate([row[:, None] * inv_freq, col[:, None] * inv_freq], axis=-1)
    pair = (np.arange(LANES) % HEAD_DIM) // 2
    cos = np.cos(ang)[:, pair]
    sin = np.sin(ang)[:, pair]
    even = (np.arange(LANES) % 2) == 0
    tables = (cos, np.where(even, -sin, 0.0), np.where(even, 0.0, sin))
    tables = tables + tuple(tb.T for tb in tables)
    return tuple(jnp.asarray(tb, F32) for tb in tables)


def _segment_ones(n):
    seg = np.arange(n) // HEAD_DIM
    return jnp.asarray(seg[:, None] == seg[None, :], BF16)


def kernel(x, c, ctx, c_ctx, w_mod, b_mod, attn_pre_norm, attn_post_norm, w_in, a_sink,
           b_q_norm, b_k_norm, a_out_norm, b_out_norm, w_out, ffn_pre_norm, ffn_post_norm,
           w_group, b_group, w_router, b_router, w_gate, w_up, w_down):
    batch, seq, d = x.shape
    ctx_len = ctx.shape[1]
    assert w_mod.shape[0] == 1, "single-layer stack only (context stream is never updated)"
    assert seq % ATTN_A_TQ == 0 and seq >= ATTN_A_TQ + 2 * WINDOW
    assert seq % PROJ_TM == 0 and seq % ATTN_B_TQ == 0 and seq % OUT_TM == 0 and seq % MOVE_TM == 0
    t = batch * seq
    nq = d // 2
    nkv = nq // KV_GROUP
    assert nkv == LANES and w_in.shape[2] == 2 * nq + 4 * nkv

    cc = jnp.concatenate([c, c_ctx[None, :], jnp.zeros((16 - batch - 1, d), F32)], axis=0)
    mod = _modulation(cc, w_mod[0], b_mod[0]).reshape(cc.shape[0], 6, 1, d)

    x2 = x.reshape(t, d)
    c2 = ctx.reshape(batch * ctx_len, d)
    gpre = attn_pre_norm[0].reshape(1, d)
    w_in_bf = w_in[0].astype(BF16)
    qn = jnp.broadcast_to(jnp.tile(b_q_norm[0], nq // HEAD_DIM)[:, None], (nq, PROJ_TM))
    kn = jnp.tile(b_k_norm[0], nkv // HEAD_DIM).reshape(1, nkv)
    seg_k = _segment_ones(nkv)
    qat, ka, vat, qbt, kb, vbt, stats = _project_latents(
        x2, mod, gpre, w_in_bf, _rope_tables(seq), qn, kn, seg_k, seq)
    kca, vcat, kcb, vcbt, ctx_stats = _project_context(
        c2, mod, batch, gpre, w_in_bf, kn, seg_k, ctx_len)

    q_sq = jnp.max(stats[:, 0, 0])
    k_sq = jnp.maximum(jnp.max(stats[:, 0, 1]), jnp.max(ctx_stats[:, 0, 1]))
    bound_a = jnp.maximum(1.01 * jnp.sqrt(q_sq * k_sq), jnp.max(a_sink[0]) * LOG2_E).reshape(1)
    oa = _attention_a(bound_a, a_sink[0], qat, ka, vat, kca, vcat, batch, seq, ctx_len)
    score_bound = (1.01 * HEAD_DIM ** 0.5 * LOG2_E
                   * jnp.max(jnp.abs(b_q_norm[0])) * jnp.max(jnp.abs(b_k_norm[0]))).reshape(1)
    ob = _attention_b(score_bound, qbt, kb, vbt, kcb, vcbt, batch, seq, ctx_len)

    w_out_bf = w_out[0].astype(BF16)
    lane_pad = LANES - N_EXPERTS - N_GROUPS
    w_r = jnp.pad(jnp.concatenate([w_router[0], w_group[0]], axis=1), ((0, 0), (0, lane_pad)))
    w_r_hi = w_r.astype(BF16)
    w_r_lo = (w_r - w_r_hi.astype(F32)).astype(BF16)
    w_r2 = jnp.concatenate([w_r_hi, w_r_lo], axis=1)
    b_r = jnp.pad(jnp.concatenate([b_router[0], b_group[0]]), (0, lane_pad)).reshape(1, LANES)
    x1, h2, rinfo, rt, tcarry, tcnt, counts = _out_and_route(
        oa, ob, x2, mod, a_out_norm[0].reshape(1, nq), b_out_norm[0].reshape(1, nq),
        attn_post_norm[0].reshape(1, d), ffn_pre_norm[0].reshape(1, d),
        w_out_bf, w_r2, b_r, seq)

    te = EXPERT_TE
    assert (2 * t) % te == 0
    cnt = counts[0, :N_EXPERTS].astype(jnp.int32)
    ends = jnp.cumsum(cnt)
    starts = ends - cnt
    run_dst = (starts[None, :] + tcarry[:, 0, :N_EXPERTS].astype(jnp.int32)).reshape(-1)
    run_len = tcnt[:, 0, :N_EXPERTS].astype(jnp.int32).reshape(-1)
    first_tile = starts // te
    n_vis = jnp.where(cnt > 0, (ends - 1) // te - first_tile + 1, 0)
    vis_end = jnp.cumsum(n_vis)
    n_visits = vis_end[-1]
    v = jnp.minimum(jnp.arange(2 * t // te + N_EXPERTS, dtype=jnp.int32), n_visits - 1)
    v_expert = jnp.sum(vis_end[None, :] <= v[:, None], axis=1).astype(jnp.int32)
    pick = (v_expert[:, None] == jnp.arange(N_EXPERTS)[None, :]).astype(jnp.int32)
    of_expert = lambda table: jnp.sum(pick * table[None, :], axis=1)
    v_tile = of_expert(first_tile) + v - of_expert(vis_end - n_vis)
    v_lo = jnp.maximum(of_expert(starts) - v_tile * te, 0)
    v_hi = jnp.minimum(of_expert(ends) - v_tile * te, te)

    xs = _dispatch(run_dst, run_len, h2, rt)
    ys = _expert_mlp(v_tile, v_expert, v_lo, v_hi, n_visits.reshape(1), xs,
                     w_gate[0], w_up[0], w_down[0])
    out = _combine(run_dst, run_len, x1, rinfo, mod, ffn_post_norm[0].reshape(1, d), ys, seq)
    return out.reshape(batch, seq, d)
```

```python
import functools

import jax
import jax.numpy as jnp
import numpy as np
from jax import lax
from jax.experimental import pallas as pl
from jax.experimental.pallas import tpu as pltpu

F32 = jnp.float32
BF16 = jnp.bfloat16

GRID_W = 64
HEAD_DIM = 64
KV_GROUP = 4
WINDOW = 128
ROPE_THETA = 10000.0
N_GROUPS = 4
EXPERTS_PER_GROUP = 8
N_EXPERTS = N_GROUPS * EXPERTS_PER_GROUP
EPS = 1e-6
NEG_INF = -1e30
LOG2_E = 1.4426950408889634
SAFE_SOFTMAX_SHIFT = 40.0

LANES = 128
V7X_VMEM_LIMIT = 56 * 1024 * 1024

MOD_ROWS = 16
MOD_BN = 1024
PROJ_TM = 512
ATTN_A_TQ = 256
ATTN_B_TQ = 256
OUT_TM = 512
EXPERT_TE = 512
MOVE_TM = OUT_TM


def _params(*sem):
    return pltpu.CompilerParams(dimension_semantics=sem, vmem_limit_bytes=V7X_VMEM_LIMIT)


def _dot(a, b):
    return jnp.dot(a, b, preferred_element_type=F32)


def _rms(x):
    return x * lax.rsqrt(jnp.mean(x * x, axis=-1, keepdims=True) + EPS)


def _split_bf16(x):
    hi = x.astype(BF16)
    lo = (x - hi.astype(F32)).astype(BF16)
    return hi, lo


def _mod_kernel(c_ref, w_ref, b_ref, o_ref):
    cc = c_ref[...]
    s = cc * jax.nn.sigmoid(cc)
    s_hi, s_lo = _split_bf16(s)
    w_hi, w_lo = _split_bf16(w_ref[...])
    o_ref[...] = _dot(s_hi, w_hi) + _dot(s_lo, w_hi) + _dot(s_hi, w_lo) + b_ref[...]


def _modulation(cc, w_mod, b_mod):
    rows, d = cc.shape
    n = w_mod.shape[1]
    bn = MOD_BN
    return pl.pallas_call(
        _mod_kernel,
        grid=(n // bn,),
        in_specs=[pl.BlockSpec((rows, d), lambda i: (0, 0)),
                  pl.BlockSpec((d, bn), lambda i: (0, i)),
                  pl.BlockSpec((1, bn), lambda i: (0, i))],
        out_specs=pl.BlockSpec((rows, bn), lambda i: (0, i)),
        out_shape=jax.ShapeDtypeStruct((rows, n), F32),
        compiler_params=_params("arbitrary"),
    )(cc, w_mod, b_mod.reshape(1, n))


def _rope(x, cos, sin_a, sin_b):
    return x * cos + pltpu.roll(x, LANES - 1, 1) * sin_a + pltpu.roll(x, 1, 1) * sin_b


def _head_norm(x, seg_ref, gain):
    ss = _dot((x * x).astype(BF16), seg_ref[...])
    return x * lax.rsqrt(ss * (1.0 / HEAD_DIM) + EPS) * gain


def _max_head_sq_norm(x, seg_ref):
    ss = _dot((x * x).astype(BF16), seg_ref[...])
    return jnp.max(jnp.max(ss, axis=1, keepdims=True), axis=0, keepdims=True)


def _norm_stats(q_sq, k_sq):
    lane = lax.broadcasted_iota(jnp.int32, (1, LANES), 1)
    zero = jnp.zeros((1, LANES), F32)
    return jnp.where(lane == 0, q_sq, zero) + jnp.where(lane == 1, k_sq, zero)


def _rope_t(xt, cos_t, sin_a_t, sin_b_t):
    return (xt * cos_t + pltpu.roll(xt, LANES - 1, 0) * sin_a_t
            + pltpu.roll(xt, 1, 0) * sin_b_t)


def _proj_kernel(x_ref, sc_ref, sh_ref, gpre_ref, w_ref, cos_ref, sa_ref, sb_ref,
                 cos_t_ref, sa_t_ref, sb_t_ref, qn_ref, kn_ref, seg_k_ref,
                 qat_ref, ka_ref, vat_ref, qbt_ref, kb_ref, vbt_ref, stats_ref):
    h = _rms(x_ref[...]) * gpre_ref[...] * (1.0 + sc_ref[0, 0]) + sh_ref[0, 0]
    p = _dot(h.astype(BF16), w_ref[...])
    cos, sa, sb = cos_ref[...], sa_ref[...], sb_ref[...]
    q_scale = HEAD_DIM ** -0.5 * LOG2_E
    cos_t, sa_t, sb_t = cos_t_ref[...] * q_scale, sa_t_ref[...] * q_scale, sb_t_ref[...] * q_scale
    nq = qat_ref.shape[0]
    q_sq = None
    for c in range(nq // LANES):
        xt = p[:, c * LANES:(c + 1) * LANES].T
        qat_ref[c * LANES:(c + 1) * LANES, :] = _rope_t(xt, cos_t, sa_t, sb_t).astype(BF16)
        sq = xt * xt
        for head in range(LANES // HEAD_DIM):
            norm = jnp.sum(sq[head * HEAD_DIM:(head + 1) * HEAD_DIM, :], axis=0, keepdims=True)
            q_sq = norm if q_sq is None else jnp.maximum(q_sq, norm)
    q_sq = jnp.max(q_sq, axis=1, keepdims=True) * (q_scale * q_scale)
    o = nq
    ka_ref[...] = _rope(p[:, o:o + LANES], cos, sa, sb).astype(BF16)
    vat_ref[...] = p[:, o + LANES:o + 2 * LANES].T.astype(BF16)
    stats_ref[0] = _norm_stats(q_sq, _max_head_sq_norm(p[:, o:o + LANES], seg_k_ref))
    o += 2 * LANES
    for c in range(nq // LANES):
        xt = p[:, o + c * LANES:o + (c + 1) * LANES].T
        sq = xt * xt
        halves = []
        for head in range(LANES // HEAD_DIM):
            rows = slice(head * HEAD_DIM, (head + 1) * HEAD_DIM)
            ms = jnp.sum(sq[rows, :], axis=0, keepdims=True) * (1.0 / HEAD_DIM)
            halves.append(xt[rows, :] * lax.rsqrt(ms + EPS))
        qn = jnp.concatenate(halves, axis=0) * qn_ref[c * LANES:(c + 1) * LANES, :]
        qbt_ref[c * LANES:(c + 1) * LANES, :] = _rope_t(qn, cos_t, sa_t, sb_t).astype(BF16)
    o += nq
    kb = _head_norm(p[:, o:o + LANES], seg_k_ref, kn_ref[...])
    kb_ref[...] = _rope(kb, cos, sa, sb).astype(BF16)
    vbt_ref[...] = p[:, o + LANES:o + 2 * LANES].T.astype(BF16)


def _ctx_proj_kernel(x_ref, sc_ref, sh_ref, gpre_ref, wa_ref, wb_ref, kn_ref, seg_k_ref,
                     ka_ref, vat_ref, kb_ref, vbt_ref, stats_ref):
    h = (_rms(x_ref[...]) * gpre_ref[...] * (1.0 + sc_ref[0, 0]) + sh_ref[0, 0]).astype(BF16)
    pa = _dot(h, wa_ref[...])
    pb = _dot(h, wb_ref[...])
    stats_ref[0] = _norm_stats(0.0, _max_head_sq_norm(pa[:, 0:LANES], seg_k_ref))
    ka_ref[...] = pa[:, 0:LANES].astype(BF16)
    vat_ref[...] = pa[:, LANES:2 * LANES].T.astype(BF16)
    kb_ref[...] = _head_norm(pb[:, 0:LANES], seg_k_ref, kn_ref[...]).astype(BF16)
    vbt_ref[...] = pb[:, LANES:2 * LANES].T.astype(BF16)


def _mod_spec(chunk, row_of_step):
    return lambda d: pl.BlockSpec((1, 1, 1, d), lambda i, *_: (row_of_step(i), chunk, 0, 0))


MOD_SH1, MOD_SC1, MOD_G1, MOD_SH2, MOD_SC2, MOD_G2 = range(6)


def _project_latents(x2, mod, gpre, w_in, tables, qn, kn, seg_k, seq):
    t, d = x2.shape
    tm = PROJ_TM
    tpb = seq // tm
    nq = qn.shape[0]
    const = lambda shape: pl.BlockSpec(shape, lambda i: (0,) * len(shape))
    batch_of = lambda i: i // tpb
    table = pl.BlockSpec((tm, LANES), lambda i: (i % tpb, 0))
    table_t = pl.BlockSpec((LANES, tm), lambda i: (0, i % tpb))
    k_spec = pl.BlockSpec((tm, LANES), lambda i: (i, 0))
    k_shape = jax.ShapeDtypeStruct((t, LANES), BF16)
    vt_spec = pl.BlockSpec((LANES, tm), lambda i: (0, i))
    vt_shape = jax.ShapeDtypeStruct((LANES, t), BF16)
    qt_spec = pl.BlockSpec((nq, tm), lambda i: (0, i))
    qt_shape = jax.ShapeDtypeStruct((nq, t), BF16)
    return pl.pallas_call(
        _proj_kernel,
        grid=(t // tm,),
        in_specs=[pl.BlockSpec((tm, d), lambda i: (i, 0)),
                  _mod_spec(MOD_SC1, batch_of)(d), _mod_spec(MOD_SH1, batch_of)(d), const((1, d)),
                  const(w_in.shape), table, table, table, table_t, table_t, table_t,
                  const((nq, tm)), const((1, LANES)), const(seg_k.shape)],
        out_specs=[qt_spec, k_spec, vt_spec, qt_spec, k_spec, vt_spec,
                   pl.BlockSpec((1, 1, LANES), lambda i: (i, 0, 0))],
        out_shape=[qt_shape, k_shape, vt_shape, qt_shape, k_shape, vt_shape,
                   jax.ShapeDtypeStruct((t // tm, 1, LANES), F32)],
        compiler_params=_params("arbitrary"),
    )(x2, mod, mod, gpre, w_in, *tables, qn, kn, seg_k)


def _project_context(c2, mod, ctx_row, gpre, w_in, kn, seg_k, ctx_len):
    t, d = c2.shape
    ctx_mod = lambda i: ctx_row
    nq = (w_in.shape[1] - 4 * LANES) // 2
    kv = 2 * LANES
    assert nq % kv == 0
    group_kv = lambda g: pl.BlockSpec((d, kv), lambda i: (0, (g * (nq + kv) + nq) // kv))
    const = lambda shape: pl.BlockSpec(shape, lambda i: (0,) * len(shape))
    k_spec = pl.BlockSpec((ctx_len, LANES), lambda i: (i, 0))
    k_shape = jax.ShapeDtypeStruct((t, LANES), BF16)
    vt_spec = pl.BlockSpec((LANES, ctx_len), lambda i: (0, i))
    vt_shape = jax.ShapeDtypeStruct((LANES, t), BF16)
    return pl.pallas_call(
        _ctx_proj_kernel,
        grid=(t // ctx_len,),
        in_specs=[pl.BlockSpec((ctx_len, d), lambda i: (i, 0)),
                  _mod_spec(MOD_SC1, ctx_mod)(d), _mod_spec(MOD_SH1, ctx_mod)(d),
                  const((1, d)), group_kv(0), group_kv(1), const((1, LANES)), const(seg_k.shape)],
        out_specs=[k_spec, vt_spec, k_spec, vt_spec,
                   pl.BlockSpec((1, 1, LANES), lambda i: (i, 0, 0))],
        out_shape=[k_shape, vt_shape, k_shape, vt_shape,
                   jax.ShapeDtypeStruct((t // ctx_len, 1, LANES), F32)],
        compiler_params=_params("arbitrary"),
    )(c2, mod, mod, gpre, w_in, w_in, kn, seg_k)


def _attend_t(w, k, kc, vt, vct, shift=None, bias=None, sink=None):
    st = _dot(k, w)
    sct = _dot(kc, w)
    if bias is not None:
        tq = bias.shape[1]
        st = jnp.concatenate([st[:, c * tq:(c + 1) * tq] + bias
                              for c in range(st.shape[1] // tq)], axis=1)
    if shift is None:
        shift = jnp.maximum(jnp.max(st, axis=0, keepdims=True),
                            jnp.max(sct, axis=0, keepdims=True))
        if sink is not None:
            shift = jnp.maximum(shift, sink)
    pt = jnp.exp2(st - shift)
    pct = jnp.exp2(sct - shift)
    denom = jnp.sum(pt, axis=0, keepdims=True) + jnp.sum(pct, axis=0, keepdims=True)
    if sink is not None:
        denom = denom + jnp.exp2(sink - shift)
    o2 = _dot(vt, pt.astype(BF16)) + _dot(vct, pct.astype(BF16))
    return o2, denom


def _all_heads_t(qt_ref, o_ref, attend):
    tq = qt_ref.shape[1]
    n_kv = LANES // HEAD_DIM
    zeros = jnp.zeros((HEAD_DIM, tq), BF16)
    outs = []
    for h in range(qt_ref.shape[0] // HEAD_DIM):
        g = h // KV_GROUP
        qh = qt_ref[h * HEAD_DIM:(h + 1) * HEAD_DIM, :]
        w = jnp.concatenate([zeros] * g + [qh] + [zeros] * (n_kv - 1 - g), axis=0)
        o2, denom = attend(h, w)
        outs.append(o2[g * HEAD_DIM:(g + 1) * HEAD_DIM, :] / denom)
    o_ref[...] = jnp.concatenate(outs, axis=0).T.astype(BF16)


def _all_heads_fused_t(qt_ref, o_ref, attend):
    tq = qt_ref.shape[1]
    n_heads = qt_ref.shape[0] // HEAD_DIM
    n_kv = LANES // HEAD_DIM
    rows = []
    for g in range(n_kv):
        heads = [qt_ref[h * HEAD_DIM:(h + 1) * HEAD_DIM, :] if h // KV_GROUP == g
                 else jnp.zeros((HEAD_DIM, tq), BF16) for h in range(n_heads)]
        rows.append(jnp.concatenate(heads, axis=1))
    w = jnp.concatenate(rows, axis=0)
    o2, denom = attend(0, w)
    o2 = o2 / denom
    outs = [o2[(h // KV_GROUP) * HEAD_DIM:(h // KV_GROUP + 1) * HEAD_DIM, h * tq:(h + 1) * tq]
            for h in range(n_heads)]
    o_ref[...] = jnp.concatenate(outs, axis=0).T.astype(BF16)


def _attn_b_kernel(bound_ref, qt_ref, k_ref, vt_ref, kc_ref, vct_ref, o_ref):
    k, kc, vt, vct = k_ref[...], kc_ref[...], vt_ref[...], vct_ref[...]
    bound = bound_ref[0]

    @pl.when(bound <= SAFE_SOFTMAX_SHIFT)
    def _():
        _all_heads_fused_t(qt_ref, o_ref, lambda h, w: _attend_t(w, k, kc, vt, vct, shift=bound))

    @pl.when(jnp.logical_not(bound <= SAFE_SOFTMAX_SHIFT))
    def _():
        _all_heads_t(qt_ref, o_ref, lambda h, w: _attend_t(w, k, kc, vt, vct))


def _attn_specs(nq, tq, seq, ctx_len):
    nqb = seq // tq
    return dict(
        qt=pl.BlockSpec((nq, tq), lambda b, i: (0, b * nqb + i)),
        k=pl.BlockSpec((seq, LANES), lambda b, i: (b, 0)),
        vt=pl.BlockSpec((LANES, seq), lambda b, i: (0, b)),
        kc=pl.BlockSpec((ctx_len, LANES), lambda b, i: (b, 0)),
        vct=pl.BlockSpec((LANES, ctx_len), lambda b, i: (0, b)),
        out=pl.BlockSpec((tq, nq), lambda b, i: (b * nqb + i, 0)))


def _attention_b(score_bound, qbt, kb, vbt, kcb, vcbt, batch, seq, ctx_len):
    nq, t = qbt.shape
    tq = ATTN_B_TQ
    sp = _attn_specs(nq, tq, seq, ctx_len)
    return pl.pallas_call(
        _attn_b_kernel,
        grid=(batch, seq // tq),
        in_specs=[pl.BlockSpec(memory_space=pltpu.SMEM),
                  sp["qt"], sp["k"], sp["vt"], sp["kc"], sp["vct"]],
        out_specs=sp["out"],
        out_shape=jax.ShapeDtypeStruct((t, nq), BF16),
        compiler_params=_params("arbitrary", "arbitrary"),
    )(score_bound, qbt, kb, vbt, kcb, vcbt)


def _attn_a_kernel(bound_ref, sink_ref, qt_ref, k_ref, vt_ref, kc_ref, vct_ref, bias_ref, o_ref,
                   *, seq):
    i = pl.program_id(1)
    tq = qt_ref.shape[1]
    n_heads = qt_ref.shape[0] // HEAD_DIM
    win = tq + 2 * WINDOW
    start = pl.multiple_of(jnp.clip(i * tq - WINDOW, 0, seq - win), WINDOW)
    k = k_ref[pl.ds(start, win), :]
    vt = vt_ref[:, pl.ds(start, win)]
    kc, vct = kc_ref[...], vct_ref[...]
    sinks = [sink_ref[h] * LOG2_E for h in range(n_heads)]

    bound = bound_ref[0]
    small = bound <= SAFE_SOFTMAX_SHIFT

    @pl.when(small)
    def _():
        sink_row = jnp.concatenate([jnp.full((1, tq), s, F32) for s in sinks], axis=1)
        _all_heads_fused_t(qt_ref, o_ref, lambda h, w: _attend_t(
            w, k, kc, vt, vct, shift=bound, bias=bias_ref[0], sink=sink_row))

    @pl.when(jnp.logical_not(small))
    def _():
        _all_heads_t(qt_ref, o_ref, lambda h, w: _attend_t(
            w, k, kc, vt, vct, bias=bias_ref[0], sink=sinks[h]))


def _band_bias(tq):
    win = tq + 2 * WINDOW
    r = np.arange(win)[:, None]
    j = np.arange(tq)[None, :]
    tables = [np.where(np.abs(off + r - j) <= WINDOW, 0.0, NEG_INF)
              for off in (0, -WINDOW, -2 * WINDOW)]
    return jnp.asarray(np.stack(tables), F32)


def _attention_a(score_bound, sink, qat, ka, vat, kca, vcat, batch, seq, ctx_len):
    nq, t = qat.shape
    tq = ATTN_A_TQ
    nqb = seq // tq
    win = tq + 2 * WINDOW
    assert nqb >= 2 and tq >= WINDOW
    sp = _attn_specs(nq, tq, seq, ctx_len)
    which = lambda b, i: (jnp.where(i == 0, 0, jnp.where(i == nqb - 1, 2, 1)), 0, 0)
    return pl.pallas_call(
        functools.partial(_attn_a_kernel, seq=seq),
        grid=(batch, nqb),
        in_specs=[pl.BlockSpec(memory_space=pltpu.SMEM), pl.BlockSpec(memory_space=pltpu.SMEM),
                  sp["qt"], sp["k"], sp["vt"], sp["kc"], sp["vct"],
                  pl.BlockSpec((1, win, tq), which)],
        out_specs=sp["out"],
        out_shape=jax.ShapeDtypeStruct((t, nq), BF16),
        compiler_params=_params("arbitrary", "arbitrary"),
    )(score_bound, sink, qat, ka, vat, kca, vcat, _band_bias(tq))


def _out_kernel(oa_ref, ob_ref, x_ref, g1_ref, sc2_ref, sh2_ref, ga_ref, gb_ref, gpost_ref,
                gpre2_ref, woa_ref, wob_ref, wr_ref, br_ref,
                x1_ref, h2_ref, rinfo_ref, rt_ref, tcarry_ref, tcnt_ref, cnt_ref, carry_ref):
    step = pl.program_id(0)

    @pl.when(step == 0)
    def _():
        carry_ref[...] = jnp.zeros_like(carry_ref)

    na = _rms(oa_ref[...].astype(F32)) * ga_ref[...]
    nb = _rms(ob_ref[...].astype(F32)) * gb_ref[...]
    ox = _dot(na.astype(BF16), woa_ref[...]) + _dot(nb.astype(BF16), wob_ref[...])
    x1 = x_ref[...] + g1_ref[0, 0] * (_rms(ox) * gpost_ref[...])
    x1_ref[...] = x1
    h2 = _rms(x1) * gpre2_ref[...] * (1.0 + sc2_ref[0, 0]) + sh2_ref[0, 0]
    h_hi, h_lo = _split_bf16(h2)
    h2_ref[...] = h_hi

    both = _dot(h_hi, wr_ref[...])
    logits = (both[:, :LANES] + _dot(h_lo, wr_ref[:, :LANES]) + both[:, LANES:]
              + br_ref[...])
    tm = logits.shape[0]
    lt = logits.T
    row = lax.broadcasted_iota(jnp.int32, lt.shape, 0)
    rowf = row.astype(F32)
    big = jnp.float32(1e9)
    ninf = jnp.float32(-jnp.inf)
    colmax = lambda v: jnp.max(v, axis=0, keepdims=True)
    colmin = lambda v: jnp.min(v, axis=0, keepdims=True)
    colsum = lambda v: jnp.sum(v, axis=0, keepdims=True)

    gmask = (row >= N_EXPERTS) & (row < N_EXPERTS + N_GROUPS)
    lg = jnp.where(gmask, lt, ninf)
    gmax = colmax(lg)
    gidx = colmin(jnp.where(lg == gmax, rowf, big)) - N_EXPERTS
    g_w = 1.0 / colsum(jnp.exp(lg - gmax))
    row_group = (row // EXPERTS_PER_GROUP).astype(F32)
    emask = (row < N_EXPERTS) & (row_group == gidx)
    le = jnp.where(emask, lt, ninf)
    m1 = colmax(le)
    i1 = colmin(jnp.where(le == m1, rowf, big))
    le2 = jnp.where(rowf == i1, ninf, le)
    m2 = colmax(le2)
    i2 = colmin(jnp.where(le2 == m2, rowf, big))
    e2 = jnp.exp(m2 - m1)
    w0 = g_w / (1.0 + e2)
    w1 = g_w * e2 / (1.0 + e2)

    hit1 = rowf == i1
    hit2 = rowf == i2
    onehot = jnp.where(hit1, 1.0, jnp.where(hit2, 1.0, 0.0)).astype(F32)
    r = lax.broadcasted_iota(jnp.int32, (tm, tm), 0)
    c = lax.broadcasted_iota(jnp.int32, (tm, tm), 1)
    earlier = jnp.where(r < c, 1.0, 0.0).astype(BF16)
    within = _dot(onehot.astype(BF16), earlier)
    tile_cnt = jnp.broadcast_to(jnp.sum(onehot, axis=1, keepdims=True), (LANES, LANES))
    er = lax.broadcasted_iota(jnp.int32, (LANES, LANES), 0)
    ec = lax.broadcasted_iota(jnp.int32, (LANES, LANES), 1)
    below = jnp.where(er > ec, 1.0, 0.0).astype(BF16)
    cnt_hi = jnp.floor(tile_cnt * (1.0 / 32.0))
    cnt_lo = tile_cnt - 32.0 * cnt_hi
    run_start = 32.0 * _dot(below, cnt_hi.astype(BF16)) + _dot(below, cnt_lo.astype(BF16))
    local = within + run_start[:, 0:1]
    pos0 = colsum(jnp.where(hit1, local, 0.0))
    pos1 = colsum(jnp.where(hit2, local, 0.0))
    cnt_row = tile_cnt.T[0:1, :]
    tcarry_ref[0] = carry_ref[...]
    tcnt_ref[0] = cnt_row
    carry_ref[...] += cnt_row
    cnt_ref[...] = carry_ref[...]

    fields = jnp.concatenate([i1, i2, pos0, pos1, w0, w1, jnp.zeros((2, tm), F32)], axis=0)
    rt_ref[...] = fields
    rinfo_ref[...] = jnp.concatenate(
        [fields, jnp.zeros((LANES - 8, tm), F32)], axis=0).T


def _out_and_route(oa, ob, x2, mod, ga, gb, gpost, gpre2, w_out, wr, br, seq):
    t, d = x2.shape
    tm = OUT_TM
    tpb = seq // tm
    nq = oa.shape[1]
    const = lambda shape: pl.BlockSpec(shape, lambda i: (0,) * len(shape))
    batch_of = lambda i: i // tpb
    rows = lambda n: pl.BlockSpec((tm, n), lambda i: (i, 0))
    per_tile = pl.BlockSpec((1, 1, LANES), lambda i: (i, 0, 0))
    w_half = lambda g: pl.BlockSpec((nq, d), lambda i: (g, 0))
    return pl.pallas_call(
        _out_kernel,
        grid=(t // tm,),
        in_specs=[rows(nq), rows(nq), rows(d),
                  _mod_spec(MOD_G1, batch_of)(d), _mod_spec(MOD_SC2, batch_of)(d),
                  _mod_spec(MOD_SH2, batch_of)(d),
                  const((1, nq)), const((1, nq)), const((1, d)), const((1, d)),
                  w_half(0), w_half(1), const(wr.shape),
                  const((1, LANES))],
        out_specs=[rows(d), rows(d), rows(LANES), pl.BlockSpec((8, tm), lambda i: (0, i)),
                   per_tile, per_tile, const((1, LANES))],
        out_shape=[jax.ShapeDtypeStruct((t, d), F32), jax.ShapeDtypeStruct((t, d), BF16),
                   jax.ShapeDtypeStruct((t, LANES), F32), jax.ShapeDtypeStruct((8, t), F32),
                   jax.ShapeDtypeStruct((t // tm, 1, LANES), F32),
                   jax.ShapeDtypeStruct((t // tm, 1, LANES), F32),
                   jax.ShapeDtypeStruct((1, LANES), F32)],
        scratch_shapes=[pltpu.VMEM((1, LANES), F32)],
        compiler_params=_params("arbitrary"),
    )(oa, ob, x2, mod, mod, mod, ga, gb, gpost, gpre2, w_out, w_out, wr, br)


PACK_ROWS = 8
ROW_DTYPE = F32


def _pack_rows(ref, x):
    n = x.shape[0]
    for c in range(PACK_ROWS):
        ref[pl.ds(c, n, stride=PACK_ROWS), :] = x[:, c * LANES:(c + 1) * LANES]


def _unpack_rows(ref):
    n = ref.shape[0] // PACK_ROWS
    return jnp.concatenate(
        [ref[pl.ds(c, n, stride=PACK_ROWS), :].astype(BF16) for c in range(PACK_ROWS)], axis=1)


def _for_each_run_piece(rdst_ref, rlen_ref, tile, max_len, fn):
    n_bits = max_len.bit_length()

    def run(e, local):
        length = rlen_ref[tile * N_EXPERTS + e]
        dst = rdst_ref[tile * N_EXPERTS + e]
        for b in range(n_bits):
            size = 1 << b

            @pl.when(((length >> b) & 1) == 1)
            def _():
                done = length & (size - 1)
                fn(local + done, dst + done, size)
        return local + length

    lax.fori_loop(0, N_EXPERTS, run, 0)


def _token_rows(ref, row0, n_rows):
    start = row0 * PACK_ROWS
    if not isinstance(start, int):
        start = pl.multiple_of(start, PACK_ROWS)
    return ref.at[pl.ds(start, n_rows * PACK_ROWS)]


def _dispatch_kernel(rdst_ref, rlen_ref, h_ref, rt_ref, xs_ref, sorted_ref, sem):
    k = pl.program_id(0)
    nk = pl.num_programs(0)
    tm = h_ref.shape[0]
    rows = 2 * tm
    slot = k % 2

    def wait_slot(s):
        pltpu.make_async_copy(sorted_ref.at[s], _token_rows(xs_ref, 0, rows), sem.at[s]).wait()

    @pl.when(k >= 2)
    def _():
        wait_slot(slot)

    pos0 = rt_ref[2:3, :]
    pos1 = rt_ref[3:4, :]
    r = lax.broadcasted_iota(jnp.int32, (rows, tm), 0).astype(F32)
    perm = jnp.where((r == pos0) | (r == pos1), 1.0, 0.0).astype(BF16)
    srt = _dot(perm, h_ref[...].astype(BF16))
    buf = sorted_ref.at[slot]
    _pack_rows(buf, srt)

    def copy_piece(local, dst, size):
        pltpu.make_async_copy(_token_rows(buf, local, size), _token_rows(xs_ref, dst, size),
                              sem.at[slot]).start()

    _for_each_run_piece(rdst_ref, rlen_ref, k, tm, copy_piece)

    @pl.when(k == nk - 1)
    def _():
        wait_slot(slot)

        @pl.when(nk >= 2)
        def _():
            wait_slot(1 - slot)


def _dispatch(run_dst, run_len, h2, rt):
    t, d = h2.shape
    assert d == PACK_ROWS * LANES
    tm = MOVE_TM
    return pl.pallas_call(
        _dispatch_kernel,
        grid_spec=pltpu.PrefetchScalarGridSpec(
            num_scalar_prefetch=2,
            grid=(t // tm,),
            in_specs=[pl.BlockSpec((tm, d), lambda i, *_: (i, 0)),
                      pl.BlockSpec((8, tm), lambda i, *_: (0, i))],
            out_specs=pl.BlockSpec(memory_space=pl.ANY),
            scratch_shapes=[pltpu.VMEM((2, 2 * tm * PACK_ROWS, LANES), ROW_DTYPE),
                            pltpu.SemaphoreType.DMA((2,))]),
        out_shape=jax.ShapeDtypeStruct((2 * t * PACK_ROWS, LANES), ROW_DTYPE),
        compiler_params=_params("arbitrary"),
    )(run_dst, run_len, h2, rt)


def _expert_kernel(vt_ref, ve_ref, va_ref, vb_ref, nv_ref, xs_ref, wg_ref, wu_ref, wd_ref, ys_ref,
                   wg_bf, wu_bf, wd_bf):
    v = pl.program_id(0)
    valid = v < nv_ref[0]
    prev = jnp.maximum(v - 1, 0)
    new_expert = (v == 0) | (ve_ref[v] != ve_ref[prev])
    new_tile = (v == 0) | (vt_ref[v] != vt_ref[prev])

    @pl.when(valid & new_expert)
    def _():
        wg_bf[...] = wg_ref[0].astype(BF16)
        wu_bf[...] = wu_ref[0].astype(BF16)
        wd_bf[...] = wd_ref[0].astype(BF16)

    def expert_rows():
        xb = _unpack_rows(xs_ref)
        gate = _dot(xb, wg_bf[...])
        up = _dot(xb, wu_bf[...])
        act = gate * jax.nn.sigmoid(gate) * up
        return _dot(act.astype(BF16), wd_bf[...])

    @pl.when(valid & new_tile)
    def _():
        _pack_rows(ys_ref, expert_rows())

    @pl.when(valid & jnp.logical_not(new_tile))
    def _():
        y = expert_rows()
        te = y.shape[0]
        row = lax.broadcasted_iota(jnp.int32, (te, 1), 0)
        mine = (row >= va_ref[v]) & (row < vb_ref[v])
        for c in range(PACK_ROWS):
            rows = pl.ds(c, te, stride=PACK_ROWS)
            ys_ref[rows, :] = jnp.where(mine, y[:, c * LANES:(c + 1) * LANES], ys_ref[rows, :])


def _expert_mlp(visit_tile, visit_expert, visit_lo, visit_hi, n_visits, xs, w_gate, w_up, w_down):
    te = EXPERT_TE
    d, ff = w_gate.shape[1:]
    blk = (te * PACK_ROWS, LANES)
    tile = lambda v, vt, *_: (vt[v], 0)
    wsel = lambda v, vt, ve, *_: (ve[v], 0, 0)
    return pl.pallas_call(
        _expert_kernel,
        grid_spec=pltpu.PrefetchScalarGridSpec(
            num_scalar_prefetch=5,
            grid=(visit_tile.shape[0],),
            in_specs=[pl.BlockSpec(blk, tile),
                      pl.BlockSpec((1, d, ff), wsel), pl.BlockSpec((1, d, ff), wsel),
                      pl.BlockSpec((1, ff, d), wsel)],
            out_specs=pl.BlockSpec(blk, tile),
            scratch_shapes=[pltpu.VMEM((d, ff), BF16), pltpu.VMEM((d, ff), BF16),
                            pltpu.VMEM((ff, d), BF16)]),
        out_shape=jax.ShapeDtypeStruct(xs.shape, ROW_DTYPE),
        compiler_params=_params("arbitrary"),
    )(visit_tile, visit_expert, visit_lo, visit_hi, n_visits, xs, w_gate, w_up, w_down)


def _combine_kernel(rdst_ref, rlen_ref, x1_ref, rinfo_ref, g2_ref, gpost_ref, ys_ref, o_ref,
                    gath_ref, sem):
    k = pl.program_id(0)
    nk = pl.num_programs(0)
    tm = x1_ref.shape[0]
    rows = 2 * tm
    slot = k % 2

    def gather_runs(tile, s):
        buf = gath_ref.at[s]

        def copy_piece(local, src, size):
            pltpu.make_async_copy(_token_rows(ys_ref, src, size), _token_rows(buf, local, size),
                                  sem.at[s]).start()

        _for_each_run_piece(rdst_ref, rlen_ref, tile, tm, copy_piece)

    @pl.when(k == 0)
    def _():
        gather_runs(0, 0)

    @pl.when(k + 1 < nk)
    def _():
        gather_runs(k + 1, 1 - slot)

    buf = gath_ref.at[slot]
    pltpu.make_async_copy(_token_rows(ys_ref, 0, rows), buf, sem.at[slot]).wait()
    g = _unpack_rows(buf)
    info = rinfo_ref[...]
    col = lax.broadcasted_iota(jnp.int32, (tm, rows), 1).astype(F32)
    pick = jnp.where(col == info[:, 2:3], info[:, 4:5],
                     jnp.where(col == info[:, 3:4], info[:, 5:6], 0.0)).astype(BF16)
    fx = _dot(pick, g)
    o_ref[...] = x1_ref[...] + g2_ref[0, 0] * (_rms(fx) * gpost_ref[...])


def _combine(run_dst, run_len, x1, rinfo, mod, gpost, ys, seq):
    t, d = x1.shape
    tm = MOVE_TM
    tpb = seq // tm
    batch_of = lambda i: i // tpb
    return pl.pallas_call(
        _combine_kernel,
        grid_spec=pltpu.PrefetchScalarGridSpec(
            num_scalar_prefetch=2,
            grid=(t // tm,),
            in_specs=[pl.BlockSpec((tm, d), lambda i, *_: (i, 0)),
                      pl.BlockSpec((tm, LANES), lambda i, *_: (i, 0)),
                      _mod_spec(MOD_G2, batch_of)(d),
                      pl.BlockSpec((1, d), lambda i, *_: (0, 0)),
                      pl.BlockSpec(memory_space=pl.ANY)],
            out_specs=pl.BlockSpec((tm, d), lambda i, *_: (i, 0)),
            scratch_shapes=[pltpu.VMEM((2, 2 * tm * PACK_ROWS, LANES), ROW_DTYPE),
                            pltpu.SemaphoreType.DMA((2,))]),
        out_shape=jax.ShapeDtypeStruct((t, d), F32),
        compiler_params=_params("arbitrary"),
    )(run_dst, run_len, x1, rinfo, mod, gpost, ys)


def _rope_tables(seq):
    pos = np.arange(seq)
    row = (pos // GRID_W).astype(np.float32)
    col = (pos % GRID_W).astype(np.float32)
    axis_dim = HEAD_DIM // 2
    inv_freq = (ROPE_THETA ** (-np.arange(0, axis_dim, 2, dtype=np.float32) / axis_dim)).astype(
        np.float32)
    ang = np.concatenate([row[:, None] * inv_freq, col[:, None] * inv_freq], axis=-1)
    pair = (np.arange(LANES) % HEAD_DIM) // 2
    cos = np.cos(ang)[:, pair]
    sin = np.sin(ang)[:, pair]
    even = (np.arange(LANES) % 2) == 0
    tables = (cos, np.where(even, -sin, 0.0), np.where(even, 0.0, sin))
    tables = tables + tuple(tb.T for tb in tables)
    return tuple(jnp.asarray(tb, F32) for tb in tables)


def _segment_ones(n):
    seg = np.arange(n) // HEAD_DIM
    return jnp.asarray(seg[:, None] == seg[None, :], BF16)


def kernel(x, c, ctx, c_ctx, w_mod, b_mod, attn_pre_norm, attn_post_norm, w_in, a_sink,
           b_q_norm, b_k_norm, a_out_norm, b_out_norm, w_out, ffn_pre_norm, ffn_post_norm,
           w_group, b_group, w_router, b_router, w_gate, w_up, w_down):
    batch, seq, d = x.shape
    ctx_len = ctx.shape[1]
    assert w_mod.shape[0] == 1, "single-layer stack only (context stream is never updated)"
    assert seq % ATTN_A_TQ == 0 and seq >= ATTN_A_TQ + 2 * WINDOW
    assert seq % PROJ_TM == 0 and seq % ATTN_B_TQ == 0 and seq % OUT_TM == 0 and seq % MOVE_TM == 0
    t = batch * seq
    nq = d // 2
    nkv = nq // KV_GROUP
    assert nkv == LANES and w_in.shape[2] == 2 * nq + 4 * nkv

    assert batch + 1 <= MOD_ROWS
    cc = jnp.concatenate([c, c_ctx[None, :], jnp.zeros((MOD_ROWS - batch - 1, d), F32)], axis=0)
    mod = _modulation(cc, w_mod[0], b_mod[0]).reshape(cc.shape[0], 6, 1, d)

    x2 = x.reshape(t, d)
    c2 = ctx.reshape(batch * ctx_len, d)
    gpre = attn_pre_norm[0].reshape(1, d)
    w_in_bf = w_in[0].astype(BF16)
    qn = jnp.broadcast_to(jnp.tile(b_q_norm[0], nq // HEAD_DIM)[:, None], (nq, PROJ_TM))
    kn = jnp.tile(b_k_norm[0], nkv // HEAD_DIM).reshape(1, nkv)
    seg_k = _segment_ones(nkv)
    qat, ka, vat, qbt, kb, vbt, stats = _project_latents(
        x2, mod, gpre, w_in_bf, _rope_tables(seq), qn, kn, seg_k, seq)
    kca, vcat, kcb, vcbt, ctx_stats = _project_context(
        c2, mod, batch, gpre, w_in_bf, kn, seg_k, ctx_len)

    q_sq = jnp.max(stats[:, 0, 0])
    k_sq = jnp.maximum(jnp.max(stats[:, 0, 1]), jnp.max(ctx_stats[:, 0, 1]))
    bound_a = jnp.maximum(1.01 * jnp.sqrt(q_sq * k_sq), jnp.max(a_sink[0]) * LOG2_E).reshape(1)
    oa = _attention_a(bound_a, a_sink[0], qat, ka, vat, kca, vcat, batch, seq, ctx_len)
    score_bound = (1.01 * HEAD_DIM ** 0.5 * LOG2_E
                   * jnp.max(jnp.abs(b_q_norm[0])) * jnp.max(jnp.abs(b_k_norm[0]))).reshape(1)
    ob = _attention_b(score_bound, qbt, kb, vbt, kcb, vcbt, batch, seq, ctx_len)

    w_out_bf = w_out[0].astype(BF16)
    lane_pad = LANES - N_EXPERTS - N_GROUPS
    w_r = jnp.pad(jnp.concatenate([w_router[0], w_group[0]], axis=1), ((0, 0), (0, lane_pad)))
    w_r_hi = w_r.astype(BF16)
    w_r_lo = (w_r - w_r_hi.astype(F32)).astype(BF16)
    w_r2 = jnp.concatenate([w_r_hi, w_r_lo], axis=1)
    b_r = jnp.pad(jnp.concatenate([b_router[0], b_group[0]]), (0, lane_pad)).reshape(1, LANES)
    x1, h2, rinfo, rt, tcarry, tcnt, counts = _out_and_route(
        oa, ob, x2, mod, a_out_norm[0].reshape(1, nq), b_out_norm[0].reshape(1, nq),
        attn_post_norm[0].reshape(1, d), ffn_pre_norm[0].reshape(1, d),
        w_out_bf, w_r2, b_r, seq)

    te = EXPERT_TE
    assert (2 * t) % te == 0
    cnt = counts[0, :N_EXPERTS].astype(jnp.int32)
    ends = jnp.cumsum(cnt)
    starts = ends - cnt
    run_dst = (starts[None, :] + tcarry[:, 0, :N_EXPERTS].astype(jnp.int32)).reshape(-1)
    run_len = tcnt[:, 0, :N_EXPERTS].astype(jnp.int32).reshape(-1)
    first_tile = starts // te
    n_vis = jnp.where(cnt > 0, (ends - 1) // te - first_tile + 1, 0)
    vis_end = jnp.cumsum(n_vis)
    n_visits = vis_end[-1]
    v = jnp.minimum(jnp.arange(2 * t // te + N_EXPERTS, dtype=jnp.int32), n_visits - 1)
    v_expert = jnp.sum(vis_end[None, :] <= v[:, None], axis=1).astype(jnp.int32)
    pick = (v_expert[:, None] == jnp.arange(N_EXPERTS)[None, :]).astype(jnp.int32)
    of_expert = lambda table: jnp.sum(pick * table[None, :], axis=1)
    v_tile = of_expert(first_tile) + v - of_expert(vis_end - n_vis)
    v_lo = jnp.maximum(of_expert(starts) - v_tile * te, 0)
    v_hi = jnp.minimum(of_expert(ends) - v_tile * te, te)

    xs = _dispatch(run_dst, run_len, h2, rt)
    ys = _expert_mlp(v_tile, v_expert, v_lo, v_hi, n_visits.reshape(1), xs,
                     w_gate[0], w_up[0], w_down[0])
    out = _combine(run_dst, run_len, x1, rinfo, mod, ffn_post_norm[0].reshape(1, d), ys, seq)
    return out.reshape(batch, seq, d)
```

```python
import functools

import jax
import jax.numpy as jnp
import numpy as np
from jax import lax
from jax.experimental import pallas as pl
from jax.experimental.pallas import tpu as pltpu

F32 = jnp.float32
BF16 = jnp.bfloat16

GRID_W = 64
HEAD_DIM = 64
KV_GROUP = 4
WINDOW = 128
ROPE_THETA = 10000.0
N_GROUPS = 4
EXPERTS_PER_GROUP = 8
N_EXPERTS = N_GROUPS * EXPERTS_PER_GROUP
EPS = 1e-6
NEG_INF = -1e30
LOG2_E = 1.4426950408889634
SAFE_SCORE_BOUND = 40.0

LANES = 128
V7X_VMEM_LIMIT = 56 * 1024 * 1024

MOD_ROWS = 16
MOD_BN = 1024
PROJ_TM = 512
ATTN_A_TQ = 256
ATTN_B_TQ = 256
OUT_TM = 512
EXPERT_TE = 512
MOVE_TM = OUT_TM


def _params(*sem):
    return pltpu.CompilerParams(dimension_semantics=sem, vmem_limit_bytes=V7X_VMEM_LIMIT)


def _dot(a, b):
    return jnp.dot(a, b, preferred_element_type=F32)


def _rms(x):
    return x * lax.rsqrt(jnp.mean(x * x, axis=-1, keepdims=True) + EPS)


def _split_bf16(x):
    hi = x.astype(BF16)
    lo = (x - hi.astype(F32)).astype(BF16)
    return hi, lo


def _mod_kernel(c_ref, w_ref, b_ref, o_ref):
    cc = c_ref[...]
    s = cc * jax.nn.sigmoid(cc)
    s_hi, s_lo = _split_bf16(s)
    w_hi, w_lo = _split_bf16(w_ref[...])
    o_ref[...] = _dot(s_hi, w_hi) + _dot(s_lo, w_hi) + _dot(s_hi, w_lo) + b_ref[...]


def _modulation(cc, w_mod, b_mod):
    rows, d = cc.shape
    n = w_mod.shape[1]
    bn = MOD_BN
    return pl.pallas_call(
        _mod_kernel,
        grid=(n // bn,),
        in_specs=[pl.BlockSpec((rows, d), lambda i: (0, 0)),
                  pl.BlockSpec((d, bn), lambda i: (0, i)),
                  pl.BlockSpec((1, bn), lambda i: (0, i))],
        out_specs=pl.BlockSpec((rows, bn), lambda i: (0, i)),
        out_shape=jax.ShapeDtypeStruct((rows, n), F32),
        compiler_params=_params("arbitrary"),
    )(cc, w_mod, b_mod.reshape(1, n))


def _rope(x, cos, sin_a, sin_b):
    return x * cos + pltpu.roll(x, LANES - 1, 1) * sin_a + pltpu.roll(x, 1, 1) * sin_b


def _head_norm(x, seg_ref, gain):
    ss = _dot((x * x).astype(BF16), seg_ref[...])
    return x * lax.rsqrt(ss * (1.0 / HEAD_DIM) + EPS) * gain


def _max_head_sq_norm(x, seg_ref):
    ss = _dot((x * x).astype(BF16), seg_ref[...])
    return jnp.max(jnp.max(ss, axis=1, keepdims=True), axis=0, keepdims=True)


def _norm_stats(q_sq, k_sq):
    lane = lax.broadcasted_iota(jnp.int32, (1, LANES), 1)
    zero = jnp.zeros((1, LANES), F32)
    return jnp.where(lane == 0, q_sq, zero) + jnp.where(lane == 1, k_sq, zero)


def _rope_t(xt, cos_t, sin_a_t, sin_b_t):
    return (xt * cos_t + pltpu.roll(xt, LANES - 1, 0) * sin_a_t
            + pltpu.roll(xt, 1, 0) * sin_b_t)


def _proj_kernel(x_ref, sc_ref, sh_ref, gpre_ref, w_ref, cos_ref, sa_ref, sb_ref,
                 cos_t_ref, sa_t_ref, sb_t_ref, qn_ref, kn_ref, seg_k_ref,
                 qat_ref, ka_ref, vat_ref, qbt_ref, kb_ref, vbt_ref, stats_ref):
    h = _rms(x_ref[...]) * gpre_ref[...] * (1.0 + sc_ref[0, 0]) + sh_ref[0, 0]
    p = _dot(h.astype(BF16), w_ref[...])
    cos, sa, sb = cos_ref[...], sa_ref[...], sb_ref[...]
    q_scale = HEAD_DIM ** -0.5 * LOG2_E
    cos_t, sa_t, sb_t = cos_t_ref[...] * q_scale, sa_t_ref[...] * q_scale, sb_t_ref[...] * q_scale
    nq = qat_ref.shape[0]
    q_sq = None
    for c in range(nq // LANES):
        xt = p[:, c * LANES:(c + 1) * LANES].T
        qat_ref[c * LANES:(c + 1) * LANES, :] = _rope_t(xt, cos_t, sa_t, sb_t).astype(BF16)
        sq = xt * xt
        for head in range(LANES // HEAD_DIM):
            norm = jnp.sum(sq[head * HEAD_DIM:(head + 1) * HEAD_DIM, :], axis=0, keepdims=True)
            q_sq = norm if q_sq is None else jnp.maximum(q_sq, norm)
    q_sq = jnp.max(q_sq, axis=1, keepdims=True) * (q_scale * q_scale)
    o = nq
    ka_ref[...] = _rope(p[:, o:o + LANES], cos, sa, sb).astype(BF16)
    vat_ref[...] = p[:, o + LANES:o + 2 * LANES].T.astype(BF16)
    stats_ref[0] = _norm_stats(q_sq, _max_head_sq_norm(p[:, o:o + LANES], seg_k_ref))
    o += 2 * LANES
    for c in range(nq // LANES):
        xt = p[:, o + c * LANES:o + (c + 1) * LANES].T
        sq = xt * xt
        halves = []
        for head in range(LANES // HEAD_DIM):
            rows = slice(head * HEAD_DIM, (head + 1) * HEAD_DIM)
            ms = jnp.sum(sq[rows, :], axis=0, keepdims=True) * (1.0 / HEAD_DIM)
            halves.append(xt[rows, :] * lax.rsqrt(ms + EPS))
        qn = jnp.concatenate(halves, axis=0) * qn_ref[c * LANES:(c + 1) * LANES, :]
        qbt_ref[c * LANES:(c + 1) * LANES, :] = _rope_t(qn, cos_t, sa_t, sb_t).astype(BF16)
    o += nq
    kb = _head_norm(p[:, o:o + LANES], seg_k_ref, kn_ref[...])
    kb_ref[...] = _rope(kb, cos, sa, sb).astype(BF16)
    vbt_ref[...] = p[:, o + LANES:o + 2 * LANES].T.astype(BF16)


def _ctx_proj_kernel(x_ref, sc_ref, sh_ref, gpre_ref, wa_ref, wb_ref, kn_ref, seg_k_ref,
                     ka_ref, vat_ref, kb_ref, vbt_ref, stats_ref):
    h = (_rms(x_ref[...]) * gpre_ref[...] * (1.0 + sc_ref[0, 0]) + sh_ref[0, 0]).astype(BF16)
    pa = _dot(h, wa_ref[...])
    pb = _dot(h, wb_ref[...])
    stats_ref[0] = _norm_stats(0.0, _max_head_sq_norm(pa[:, 0:LANES], seg_k_ref))
    ka_ref[...] = pa[:, 0:LANES].astype(BF16)
    vat_ref[...] = pa[:, LANES:2 * LANES].T.astype(BF16)
    kb_ref[...] = _head_norm(pb[:, 0:LANES], seg_k_ref, kn_ref[...]).astype(BF16)
    vbt_ref[...] = pb[:, LANES:2 * LANES].T.astype(BF16)


def _mod_spec(chunk, row_of_step):
    return lambda d: pl.BlockSpec((1, 1, 1, d), lambda i, *_: (row_of_step(i), chunk, 0, 0))


MOD_SH1, MOD_SC1, MOD_G1, MOD_SH2, MOD_SC2, MOD_G2 = range(6)


def _project_latents(x2, mod, gpre, w_in, tables, qn, kn, seg_k, seq):
    t, d = x2.shape
    tm = PROJ_TM
    tpb = seq // tm
    nq = qn.shape[0]
    const = lambda shape: pl.BlockSpec(shape, lambda i: (0,) * len(shape))
    batch_of = lambda i: i // tpb
    table = pl.BlockSpec((tm, LANES), lambda i: (i % tpb, 0))
    table_t = pl.BlockSpec((LANES, tm), lambda i: (0, i % tpb))
    k_spec = pl.BlockSpec((tm, LANES), lambda i: (i, 0))
    k_shape = jax.ShapeDtypeStruct((t, LANES), BF16)
    vt_spec = pl.BlockSpec((LANES, tm), lambda i: (0, i))
    vt_shape = jax.ShapeDtypeStruct((LANES, t), BF16)
    qt_spec = pl.BlockSpec((nq, tm), lambda i: (0, i))
    qt_shape = jax.ShapeDtypeStruct((nq, t), BF16)
    return pl.pallas_call(
        _proj_kernel,
        grid=(t // tm,),
        in_specs=[pl.BlockSpec((tm, d), lambda i: (i, 0)),
                  _mod_spec(MOD_SC1, batch_of)(d), _mod_spec(MOD_SH1, batch_of)(d), const((1, d)),
                  const(w_in.shape), table, table, table, table_t, table_t, table_t,
                  const((nq, tm)), const((1, LANES)), const(seg_k.shape)],
        out_specs=[qt_spec, k_spec, vt_spec, qt_spec, k_spec, vt_spec,
                   pl.BlockSpec((1, 1, LANES), lambda i: (i, 0, 0))],
        out_shape=[qt_shape, k_shape, vt_shape, qt_shape, k_shape, vt_shape,
                   jax.ShapeDtypeStruct((t // tm, 1, LANES), F32)],
        compiler_params=_params("arbitrary"),
    )(x2, mod, mod, gpre, w_in, *tables, qn, kn, seg_k)


def _project_context(c2, mod, ctx_row, gpre, w_in, kn, seg_k, ctx_len):
    t, d = c2.shape
    ctx_mod = lambda i: ctx_row
    nq = (w_in.shape[1] - 4 * LANES) // 2
    kv = 2 * LANES
    assert nq % kv == 0
    group_kv = lambda g: pl.BlockSpec((d, kv), lambda i: (0, (g * (nq + kv) + nq) // kv))
    const = lambda shape: pl.BlockSpec(shape, lambda i: (0,) * len(shape))
    k_spec = pl.BlockSpec((ctx_len, LANES), lambda i: (i, 0))
    k_shape = jax.ShapeDtypeStruct((t, LANES), BF16)
    vt_spec = pl.BlockSpec((LANES, ctx_len), lambda i: (0, i))
    vt_shape = jax.ShapeDtypeStruct((LANES, t), BF16)
    return pl.pallas_call(
        _ctx_proj_kernel,
        grid=(t // ctx_len,),
        in_specs=[pl.BlockSpec((ctx_len, d), lambda i: (i, 0)),
                  _mod_spec(MOD_SC1, ctx_mod)(d), _mod_spec(MOD_SH1, ctx_mod)(d),
                  const((1, d)), group_kv(0), group_kv(1), const((1, LANES)), const(seg_k.shape)],
        out_specs=[k_spec, vt_spec, k_spec, vt_spec,
                   pl.BlockSpec((1, 1, LANES), lambda i: (i, 0, 0))],
        out_shape=[k_shape, vt_shape, k_shape, vt_shape,
                   jax.ShapeDtypeStruct((t // ctx_len, 1, LANES), F32)],
        compiler_params=_params("arbitrary"),
    )(c2, mod, mod, gpre, w_in, w_in, kn, seg_k)


def _attend_t(w, k, kc, vt, vct, bounded=False, bias=None, sink=None):
    st = _dot(k, w)
    sct = _dot(kc, w)
    if bias is not None:
        tq = bias.shape[1]
        st = jnp.concatenate([st[:, c * tq:(c + 1) * tq] + bias
                              for c in range(st.shape[1] // tq)], axis=1)
    if not bounded:
        shift = jnp.maximum(jnp.max(st, axis=0, keepdims=True),
                            jnp.max(sct, axis=0, keepdims=True))
        if sink is not None:
            shift = jnp.maximum(shift, sink)
            sink = sink - shift
        st, sct = st - shift, sct - shift
    pt = jnp.exp2(st)
    pct = jnp.exp2(sct)
    denom = jnp.sum(pt, axis=0, keepdims=True) + jnp.sum(pct, axis=0, keepdims=True)
    if sink is not None:
        denom = denom + jnp.exp2(sink)
    o2 = _dot(vt, pt.astype(BF16)) + _dot(vct, pct.astype(BF16))
    return o2, denom


def _all_heads_t(qt_ref, o_ref, attend):
    tq = qt_ref.shape[1]
    n_kv = LANES // HEAD_DIM
    zeros = jnp.zeros((HEAD_DIM, tq), BF16)
    outs = []
    for h in range(qt_ref.shape[0] // HEAD_DIM):
        g = h // KV_GROUP
        qh = qt_ref[h * HEAD_DIM:(h + 1) * HEAD_DIM, :]
        w = jnp.concatenate([zeros] * g + [qh] + [zeros] * (n_kv - 1 - g), axis=0)
        o2, denom = attend(h, w)
        outs.append(o2[g * HEAD_DIM:(g + 1) * HEAD_DIM, :] / denom)
    o_ref[...] = jnp.concatenate(outs, axis=0).T.astype(BF16)


def _all_heads_fused_t(qt_ref, o_ref, attend):
    tq = qt_ref.shape[1]
    n_heads = qt_ref.shape[0] // HEAD_DIM
    n_kv = LANES // HEAD_DIM
    rows = []
    for g in range(n_kv):
        heads = [qt_ref[h * HEAD_DIM:(h + 1) * HEAD_DIM, :] if h // KV_GROUP == g
                 else jnp.zeros((HEAD_DIM, tq), BF16) for h in range(n_heads)]
        rows.append(jnp.concatenate(heads, axis=1))
    w = jnp.concatenate(rows, axis=0)
    o2, denom = attend(0, w)
    o2 = o2 / denom
    outs = [o2[(h // KV_GROUP) * HEAD_DIM:(h // KV_GROUP + 1) * HEAD_DIM, h * tq:(h + 1) * tq]
            for h in range(n_heads)]
    o_ref[...] = jnp.concatenate(outs, axis=0).T.astype(BF16)


def _attn_b_kernel(bound_ref, qt_ref, k_ref, vt_ref, kc_ref, vct_ref, o_ref):
    k, kc, vt, vct = k_ref[...], kc_ref[...], vt_ref[...], vct_ref[...]
    bound = bound_ref[0]

    @pl.when(bound <= SAFE_SCORE_BOUND)
    def _():
        _all_heads_fused_t(qt_ref, o_ref, lambda h, w: _attend_t(w, k, kc, vt, vct, bounded=True))

    @pl.when(jnp.logical_not(bound <= SAFE_SCORE_BOUND))
    def _():
        _all_heads_t(qt_ref, o_ref, lambda h, w: _attend_t(w, k, kc, vt, vct))


def _attn_specs(nq, tq, seq, ctx_len):
    nqb = seq // tq
    return dict(
        qt=pl.BlockSpec((nq, tq), lambda b, i: (0, b * nqb + i)),
        k=pl.BlockSpec((seq, LANES), lambda b, i: (b, 0)),
        vt=pl.BlockSpec((LANES, seq), lambda b, i: (0, b)),
        kc=pl.BlockSpec((ctx_len, LANES), lambda b, i: (b, 0)),
        vct=pl.BlockSpec((LANES, ctx_len), lambda b, i: (0, b)),
        out=pl.BlockSpec((tq, nq), lambda b, i: (b * nqb + i, 0)))


def _attention_b(score_bound, qbt, kb, vbt, kcb, vcbt, batch, seq, ctx_len):
    nq, t = qbt.shape
    tq = ATTN_B_TQ
    sp = _attn_specs(nq, tq, seq, ctx_len)
    return pl.pallas_call(
        _attn_b_kernel,
        grid=(batch, seq // tq),
        in_specs=[pl.BlockSpec(memory_space=pltpu.SMEM),
                  sp["qt"], sp["k"], sp["vt"], sp["kc"], sp["vct"]],
        out_specs=sp["out"],
        out_shape=jax.ShapeDtypeStruct((t, nq), BF16),
        compiler_params=_params("arbitrary", "arbitrary"),
    )(score_bound, qbt, kb, vbt, kcb, vcbt)


def _attn_a_kernel(bound_ref, sink_ref, qt_ref, k_ref, vt_ref, kc_ref, vct_ref, bias_ref, o_ref,
                   *, seq):
    i = pl.program_id(1)
    tq = qt_ref.shape[1]
    n_heads = qt_ref.shape[0] // HEAD_DIM
    win = tq + 2 * WINDOW
    start = pl.multiple_of(jnp.clip(i * tq - WINDOW, 0, seq - win), WINDOW)
    k = k_ref[pl.ds(start, win), :]
    vt = vt_ref[:, pl.ds(start, win)]
    kc, vct = kc_ref[...], vct_ref[...]
    sinks = [sink_ref[h] * LOG2_E for h in range(n_heads)]

    bound = bound_ref[0]
    small = bound <= SAFE_SCORE_BOUND

    @pl.when(small)
    def _():
        sink_row = jnp.concatenate([jnp.full((1, tq), s, F32) for s in sinks], axis=1)
        _all_heads_fused_t(qt_ref, o_ref, lambda h, w: _attend_t(
            w, k, kc, vt, vct, bounded=True, bias=bias_ref[0], sink=sink_row))

    @pl.when(jnp.logical_not(small))
    def _():
        _all_heads_t(qt_ref, o_ref, lambda h, w: _attend_t(
            w, k, kc, vt, vct, bias=bias_ref[0], sink=sinks[h]))


def _band_bias(tq):
    win = tq + 2 * WINDOW
    r = np.arange(win)[:, None]
    j = np.arange(tq)[None, :]
    tables = [np.where(np.abs(off + r - j) <= WINDOW, 0.0, NEG_INF)
              for off in (0, -WINDOW, -2 * WINDOW)]
    return jnp.asarray(np.stack(tables), F32)


def _attention_a(score_bound, sink, qat, ka, vat, kca, vcat, batch, seq, ctx_len):
    nq, t = qat.shape
    tq = ATTN_A_TQ
    nqb = seq // tq
    win = tq + 2 * WINDOW
    assert nqb >= 2 and tq >= WINDOW
    sp = _attn_specs(nq, tq, seq, ctx_len)
    which = lambda b, i: (jnp.where(i == 0, 0, jnp.where(i == nqb - 1, 2, 1)), 0, 0)
    return pl.pallas_call(
        functools.partial(_attn_a_kernel, seq=seq),
        grid=(batch, nqb),
        in_specs=[pl.BlockSpec(memory_space=pltpu.SMEM), pl.BlockSpec(memory_space=pltpu.SMEM),
                  sp["qt"], sp["k"], sp["vt"], sp["kc"], sp["vct"],
                  pl.BlockSpec((1, win, tq), which)],
        out_specs=sp["out"],
        out_shape=jax.ShapeDtypeStruct((t, nq), BF16),
        compiler_params=_params("arbitrary", "arbitrary"),
    )(score_bound, sink, qat, ka, vat, kca, vcat, _band_bias(tq))


def _out_kernel(oa_ref, ob_ref, x_ref, g1_ref, sc2_ref, sh2_ref, ga_ref, gb_ref, gpost_ref,
                gpre2_ref, woa_ref, wob_ref, wr_ref, br_ref,
                x1_ref, h2_ref, rinfo_ref, rt_ref, tcarry_ref, tcnt_ref, cnt_ref, carry_ref):
    step = pl.program_id(0)

    @pl.when(step == 0)
    def _():
        carry_ref[...] = jnp.zeros_like(carry_ref)

    na = _rms(oa_ref[...].astype(F32)) * ga_ref[...]
    nb = _rms(ob_ref[...].astype(F32)) * gb_ref[...]
    ox = _dot(na.astype(BF16), woa_ref[...]) + _dot(nb.astype(BF16), wob_ref[...])
    x1 = x_ref[...] + g1_ref[0, 0] * (_rms(ox) * gpost_ref[...])
    x1_ref[...] = x1
    h2 = _rms(x1) * gpre2_ref[...] * (1.0 + sc2_ref[0, 0]) + sh2_ref[0, 0]
    h_hi, h_lo = _split_bf16(h2)
    h2_ref[...] = h_hi

    both = _dot(h_hi, wr_ref[...])
    logits = (both[:, :LANES] + _dot(h_lo, wr_ref[:, :LANES]) + both[:, LANES:]
              + br_ref[...])
    tm = logits.shape[0]
    lt = logits.T
    row = lax.broadcasted_iota(jnp.int32, lt.shape, 0)
    rowf = row.astype(F32)
    big = jnp.float32(1e9)
    ninf = jnp.float32(-jnp.inf)
    colmax = lambda v: jnp.max(v, axis=0, keepdims=True)
    colmin = lambda v: jnp.min(v, axis=0, keepdims=True)
    colsum = lambda v: jnp.sum(v, axis=0, keepdims=True)

    gmask = (row >= N_EXPERTS) & (row < N_EXPERTS + N_GROUPS)
    lg = jnp.where(gmask, lt, ninf)
    gmax = colmax(lg)
    gidx = colmin(jnp.where(lg == gmax, rowf, big)) - N_EXPERTS
    g_w = 1.0 / colsum(jnp.exp(lg - gmax))
    row_group = (row // EXPERTS_PER_GROUP).astype(F32)
    emask = (row < N_EXPERTS) & (row_group == gidx)
    le = jnp.where(emask, lt, ninf)
    m1 = colmax(le)
    i1 = colmin(jnp.where(le == m1, rowf, big))
    le2 = jnp.where(rowf == i1, ninf, le)
    m2 = colmax(le2)
    i2 = colmin(jnp.where(le2 == m2, rowf, big))
    e2 = jnp.exp(m2 - m1)
    w0 = g_w / (1.0 + e2)
    w1 = g_w * e2 / (1.0 + e2)

    hit1 = rowf == i1
    hit2 = rowf == i2
    onehot = jnp.where(hit1, 1.0, jnp.where(hit2, 1.0, 0.0)).astype(F32)
    r = lax.broadcasted_iota(jnp.int32, (tm, tm), 0)
    c = lax.broadcasted_iota(jnp.int32, (tm, tm), 1)
    earlier = jnp.where(r < c, 1.0, 0.0).astype(BF16)
    within = _dot(onehot.astype(BF16), earlier)
    tile_cnt = jnp.broadcast_to(jnp.sum(onehot, axis=1, keepdims=True), (LANES, LANES))
    er = lax.broadcasted_iota(jnp.int32, (LANES, LANES), 0)
    ec = lax.broadcasted_iota(jnp.int32, (LANES, LANES), 1)
    below = jnp.where(er > ec, 1.0, 0.0).astype(BF16)
    cnt_hi = jnp.floor(tile_cnt * (1.0 / 32.0))
    cnt_lo = tile_cnt - 32.0 * cnt_hi
    run_start = 32.0 * _dot(below, cnt_hi.astype(BF16)) + _dot(below, cnt_lo.astype(BF16))
    local = within + run_start[:, 0:1]
    pos0 = colsum(jnp.where(hit1, local, 0.0))
    pos1 = colsum(jnp.where(hit2, local, 0.0))
    cnt_row = tile_cnt.T[0:1, :]
    tcarry_ref[0] = carry_ref[...]
    tcnt_ref[0] = cnt_row
    carry_ref[...] += cnt_row
    cnt_ref[...] = carry_ref[...]

    fields = jnp.concatenate([i1, i2, pos0, pos1, w0, w1, jnp.zeros((2, tm), F32)], axis=0)
    rt_ref[...] = fields
    rinfo_ref[...] = jnp.concatenate(
        [fields, jnp.zeros((LANES - 8, tm), F32)], axis=0).T


def _out_and_route(oa, ob, x2, mod, ga, gb, gpost, gpre2, w_out, wr, br, seq):
    t, d = x2.shape
    tm = OUT_TM
    tpb = seq // tm
    nq = oa.shape[1]
    const = lambda shape: pl.BlockSpec(shape, lambda i: (0,) * len(shape))
    batch_of = lambda i: i // tpb
    rows = lambda n: pl.BlockSpec((tm, n), lambda i: (i, 0))
    per_tile = pl.BlockSpec((1, 1, LANES), lambda i: (i, 0, 0))
    w_half = lambda g: pl.BlockSpec((nq, d), lambda i: (g, 0))
    return pl.pallas_call(
        _out_kernel,
        grid=(t // tm,),
        in_specs=[rows(nq), rows(nq), rows(d),
                  _mod_spec(MOD_G1, batch_of)(d), _mod_spec(MOD_SC2, batch_of)(d),
                  _mod_spec(MOD_SH2, batch_of)(d),
                  const((1, nq)), const((1, nq)), const((1, d)), const((1, d)),
                  w_half(0), w_half(1), const(wr.shape),
                  const((1, LANES))],
        out_specs=[rows(d), rows(d), rows(LANES), pl.BlockSpec((8, tm), lambda i: (0, i)),
                   per_tile, per_tile, const((1, LANES))],
        out_shape=[jax.ShapeDtypeStruct((t, d), F32), jax.ShapeDtypeStruct((t, d), BF16),
                   jax.ShapeDtypeStruct((t, LANES), F32), jax.ShapeDtypeStruct((8, t), F32),
                   jax.ShapeDtypeStruct((t // tm, 1, LANES), F32),
                   jax.ShapeDtypeStruct((t // tm, 1, LANES), F32),
                   jax.ShapeDtypeStruct((1, LANES), F32)],
        scratch_shapes=[pltpu.VMEM((1, LANES), F32)],
        compiler_params=_params("arbitrary"),
    )(oa, ob, x2, mod, mod, mod, ga, gb, gpost, gpre2, w_out, w_out, wr, br)


PACK_ROWS = 8
ROW_DTYPE = F32


def _pack_rows(ref, x):
    n = x.shape[0]
    for c in range(PACK_ROWS):
        ref[pl.ds(c, n, stride=PACK_ROWS), :] = x[:, c * LANES:(c + 1) * LANES]


def _unpack_rows(ref):
    n = ref.shape[0] // PACK_ROWS
    return jnp.concatenate(
        [ref[pl.ds(c, n, stride=PACK_ROWS), :].astype(BF16) for c in range(PACK_ROWS)], axis=1)


def _for_each_run_piece(rdst_ref, rlen_ref, tile, max_len, fn):
    n_bits = max_len.bit_length()

    def run(e, local):
        length = rlen_ref[tile * N_EXPERTS + e]
        dst = rdst_ref[tile * N_EXPERTS + e]
        for b in range(n_bits):
            size = 1 << b

            @pl.when(((length >> b) & 1) == 1)
            def _():
                done = length & (size - 1)
                fn(local + done, dst + done, size)
        return local + length

    lax.fori_loop(0, N_EXPERTS, run, 0)


def _token_rows(ref, row0, n_rows):
    start = row0 * PACK_ROWS
    if not isinstance(start, int):
        start = pl.multiple_of(start, PACK_ROWS)
    return ref.at[pl.ds(start, n_rows * PACK_ROWS)]


def _dispatch_kernel(rdst_ref, rlen_ref, h_ref, rt_ref, xs_ref, sorted_ref, sem):
    k = pl.program_id(0)
    nk = pl.num_programs(0)
    tm = h_ref.shape[0]
    rows = 2 * tm
    slot = k % 2

    def wait_slot(s):
        pltpu.make_async_copy(sorted_ref.at[s], _token_rows(xs_ref, 0, rows), sem.at[s]).wait()

    @pl.when(k >= 2)
    def _():
        wait_slot(slot)

    pos0 = rt_ref[2:3, :]
    pos1 = rt_ref[3:4, :]
    r = lax.broadcasted_iota(jnp.int32, (rows, tm), 0).astype(F32)
    perm = jnp.where((r == pos0) | (r == pos1), 1.0, 0.0).astype(BF16)
    srt = _dot(perm, h_ref[...].astype(BF16))
    buf = sorted_ref.at[slot]
    _pack_rows(buf, srt)

    def copy_piece(local, dst, size):
        pltpu.make_async_copy(_token_rows(buf, local, size), _token_rows(xs_ref, dst, size),
                              sem.at[slot]).start()

    _for_each_run_piece(rdst_ref, rlen_ref, k, tm, copy_piece)

    @pl.when(k == nk - 1)
    def _():
        wait_slot(slot)

        @pl.when(nk >= 2)
        def _():
            wait_slot(1 - slot)


def _dispatch(run_dst, run_len, h2, rt):
    t, d = h2.shape
    assert d == PACK_ROWS * LANES
    tm = MOVE_TM
    return pl.pallas_call(
        _dispatch_kernel,
        grid_spec=pltpu.PrefetchScalarGridSpec(
            num_scalar_prefetch=2,
            grid=(t // tm,),
            in_specs=[pl.BlockSpec((tm, d), lambda i, *_: (i, 0)),
                      pl.BlockSpec((8, tm), lambda i, *_: (0, i))],
            out_specs=pl.BlockSpec(memory_space=pl.ANY),
            scratch_shapes=[pltpu.VMEM((2, 2 * tm * PACK_ROWS, LANES), ROW_DTYPE),
                            pltpu.SemaphoreType.DMA((2,))]),
        out_shape=jax.ShapeDtypeStruct((2 * t * PACK_ROWS, LANES), ROW_DTYPE),
        compiler_params=_params("arbitrary"),
    )(run_dst, run_len, h2, rt)


def _expert_kernel(vt_ref, ve_ref, va_ref, vb_ref, nv_ref, xs_ref, wg_ref, wu_ref, wd_ref, ys_ref,
                   wg_bf, wu_bf, wd_bf):
    v = pl.program_id(0)
    valid = v < nv_ref[0]
    prev = jnp.maximum(v - 1, 0)
    new_expert = (v == 0) | (ve_ref[v] != ve_ref[prev])
    new_tile = (v == 0) | (vt_ref[v] != vt_ref[prev])

    @pl.when(valid & new_expert)
    def _():
        wg_bf[...] = wg_ref[0].astype(BF16)
        wu_bf[...] = wu_ref[0].astype(BF16)
        wd_bf[...] = wd_ref[0].astype(BF16)

    def expert_rows():
        xb = _unpack_rows(xs_ref)
        gate = _dot(xb, wg_bf[...])
        up = _dot(xb, wu_bf[...])
        act = gate * jax.nn.sigmoid(gate) * up
        return _dot(act.astype(BF16), wd_bf[...])

    @pl.when(valid & new_tile)
    def _():
        _pack_rows(ys_ref, expert_rows())

    @pl.when(valid & jnp.logical_not(new_tile))
    def _():
        y = expert_rows()
        te = y.shape[0]
        row = lax.broadcasted_iota(jnp.int32, (te, 1), 0)
        mine = (row >= va_ref[v]) & (row < vb_ref[v])
        for c in range(PACK_ROWS):
            rows = pl.ds(c, te, stride=PACK_ROWS)
            ys_ref[rows, :] = jnp.where(mine, y[:, c * LANES:(c + 1) * LANES], ys_ref[rows, :])


def _expert_mlp(visit_tile, visit_expert, visit_lo, visit_hi, n_visits, xs, w_gate, w_up, w_down):
    te = EXPERT_TE
    d, ff = w_gate.shape[1:]
    blk = (te * PACK_ROWS, LANES)
    tile = lambda v, vt, *_: (vt[v], 0)
    wsel = lambda v, vt, ve, *_: (ve[v], 0, 0)
    return pl.pallas_call(
        _expert_kernel,
        grid_spec=pltpu.PrefetchScalarGridSpec(
            num_scalar_prefetch=5,
            grid=(visit_tile.shape[0],),
            in_specs=[pl.BlockSpec(blk, tile),
                      pl.BlockSpec((1, d, ff), wsel), pl.BlockSpec((1, d, ff), wsel),
                      pl.BlockSpec((1, ff, d), wsel)],
            out_specs=pl.BlockSpec(blk, tile),
            scratch_shapes=[pltpu.VMEM((d, ff), BF16), pltpu.VMEM((d, ff), BF16),
                            pltpu.VMEM((ff, d), BF16)]),
        out_shape=jax.ShapeDtypeStruct(xs.shape, ROW_DTYPE),
        compiler_params=_params("arbitrary"),
    )(visit_tile, visit_expert, visit_lo, visit_hi, n_visits, xs, w_gate, w_up, w_down)


def _combine_kernel(rdst_ref, rlen_ref, x1_ref, rinfo_ref, g2_ref, gpost_ref, ys_ref, o_ref,
                    gath_ref, sem):
    k = pl.program_id(0)
    nk = pl.num_programs(0)
    tm = x1_ref.shape[0]
    rows = 2 * tm
    slot = k % 2

    def gather_runs(tile, s):
        buf = gath_ref.at[s]

        def copy_piece(local, src, size):
            pltpu.make_async_copy(_token_rows(ys_ref, src, size), _token_rows(buf, local, size),
                                  sem.at[s]).start()

        _for_each_run_piece(rdst_ref, rlen_ref, tile, tm, copy_piece)

    @pl.when(k == 0)
    def _():
        gather_runs(0, 0)

    @pl.when(k + 1 < nk)
    def _():
        gather_runs(k + 1, 1 - slot)

    buf = gath_ref.at[slot]
    pltpu.make_async_copy(_token_rows(ys_ref, 0, rows), buf, sem.at[slot]).wait()
    g = _unpack_rows(buf)
    info = rinfo_ref[...]
    col = lax.broadcasted_iota(jnp.int32, (tm, rows), 1).astype(F32)
    pick = jnp.where(col == info[:, 2:3], info[:, 4:5],
                     jnp.where(col == info[:, 3:4], info[:, 5:6], 0.0)).astype(BF16)
    fx = _dot(pick, g)
    o_ref[...] = x1_ref[...] + g2_ref[0, 0] * (_rms(fx) * gpost_ref[...])


def _combine(run_dst, run_len, x1, rinfo, mod, gpost, ys, seq):
    t, d = x1.shape
    tm = MOVE_TM
    tpb = seq // tm
    batch_of = lambda i: i // tpb
    return pl.pallas_call(
        _combine_kernel,
        grid_spec=pltpu.PrefetchScalarGridSpec(
            num_scalar_prefetch=2,
            grid=(t // tm,),
            in_specs=[pl.BlockSpec((tm, d), lambda i, *_: (i, 0)),
                      pl.BlockSpec((tm, LANES), lambda i, *_: (i, 0)),
                      _mod_spec(MOD_G2, batch_of)(d),
                      pl.BlockSpec((1, d), lambda i, *_: (0, 0)),
                      pl.BlockSpec(memory_space=pl.ANY)],
            out_specs=pl.BlockSpec((tm, d), lambda i, *_: (i, 0)),
            scratch_shapes=[pltpu.VMEM((2, 2 * tm * PACK_ROWS, LANES), ROW_DTYPE),
                            pltpu.SemaphoreType.DMA((2,))]),
        out_shape=jax.ShapeDtypeStruct((t, d), F32),
        compiler_params=_params("arbitrary"),
    )(run_dst, run_len, x1, rinfo, mod, gpost, ys)


def _rope_tables(seq):
    pos = np.arange(seq)
    row = (pos // GRID_W).astype(np.float32)
    col = (pos % GRID_W).astype(np.float32)
    axis_dim = HEAD_DIM // 2
    inv_freq = (ROPE_THETA ** (-np.arange(0, axis_dim, 2, dtype=np.float32) / axis_dim)).astype(
        np.float32)
    ang = np.concatenate([row[:, None] * inv_freq, col[:, None] * inv_freq], axis=-1)
    pair = (np.arange(LANES) % HEAD_DIM) // 2
    cos = np.cos(ang)[:, pair]
    sin = np.sin(ang)[:, pair]
    even = (np.arange(LANES) % 2) == 0
    tables = (cos, np.where(even, -sin, 0.0), np.where(even, 0.0, sin))
    tables = tables + tuple(tb.T for tb in tables)
    return tuple(jnp.asarray(tb, F32) for tb in tables)


def _segment_ones(n):
    seg = np.arange(n) // HEAD_DIM
    return jnp.asarray(seg[:, None] == seg[None, :], BF16)


def kernel(x, c, ctx, c_ctx, w_mod, b_mod, attn_pre_norm, attn_post_norm, w_in, a_sink,
           b_q_norm, b_k_norm, a_out_norm, b_out_norm, w_out, ffn_pre_norm, ffn_post_norm,
           w_group, b_group, w_router, b_router, w_gate, w_up, w_down):
    batch, seq, d = x.shape
    ctx_len = ctx.shape[1]
    assert w_mod.shape[0] == 1, "single-layer stack only (context stream is never updated)"
    assert seq % ATTN_A_TQ == 0 and seq >= ATTN_A_TQ + 2 * WINDOW
    assert seq % PROJ_TM == 0 and seq % ATTN_B_TQ == 0 and seq % OUT_TM == 0 and seq % MOVE_TM == 0
    t = batch * seq
    nq = d // 2
    nkv = nq // KV_GROUP
    assert nkv == LANES and w_in.shape[2] == 2 * nq + 4 * nkv

    assert batch + 1 <= MOD_ROWS
    cc = jnp.concatenate([c, c_ctx[None, :], jnp.zeros((MOD_ROWS - batch - 1, d), F32)], axis=0)
    mod = _modulation(cc, w_mod[0], b_mod[0]).reshape(cc.shape[0], 6, 1, d)

    x2 = x.reshape(t, d)
    c2 = ctx.reshape(batch * ctx_len, d)
    gpre = attn_pre_norm[0].reshape(1, d)
    w_in_bf = w_in[0].astype(BF16)
    qn = jnp.broadcast_to(jnp.tile(b_q_norm[0], nq // HEAD_DIM)[:, None], (nq, PROJ_TM))
    kn = jnp.tile(b_k_norm[0], nkv // HEAD_DIM).reshape(1, nkv)
    seg_k = _segment_ones(nkv)
    qat, ka, vat, qbt, kb, vbt, stats = _project_latents(
        x2, mod, gpre, w_in_bf, _rope_tables(seq), qn, kn, seg_k, seq)
    kca, vcat, kcb, vcbt, ctx_stats = _project_context(
        c2, mod, batch, gpre, w_in_bf, kn, seg_k, ctx_len)

    q_sq = jnp.max(stats[:, 0, 0])
    k_sq = jnp.maximum(jnp.max(stats[:, 0, 1]), jnp.max(ctx_stats[:, 0, 1]))
    bound_a = jnp.maximum(1.01 * jnp.sqrt(q_sq * k_sq), jnp.max(a_sink[0]) * LOG2_E).reshape(1)
    oa = _attention_a(bound_a, a_sink[0], qat, ka, vat, kca, vcat, batch, seq, ctx_len)
    score_bound = (1.01 * HEAD_DIM ** 0.5 * LOG2_E
                   * jnp.max(jnp.abs(b_q_norm[0])) * jnp.max(jnp.abs(b_k_norm[0]))).reshape(1)
    ob = _attention_b(score_bound, qbt, kb, vbt, kcb, vcbt, batch, seq, ctx_len)

    w_out_bf = w_out[0].astype(BF16)
    lane_pad = LANES - N_EXPERTS - N_GROUPS
    w_r = jnp.pad(jnp.concatenate([w_router[0], w_group[0]], axis=1), ((0, 0), (0, lane_pad)))
    w_r_hi = w_r.astype(BF16)
    w_r_lo = (w_r - w_r_hi.astype(F32)).astype(BF16)
    w_r2 = jnp.concatenate([w_r_hi, w_r_lo], axis=1)
    b_r = jnp.pad(jnp.concatenate([b_router[0], b_group[0]]), (0, lane_pad)).reshape(1, LANES)
    x1, h2, rinfo, rt, tcarry, tcnt, counts = _out_and_route(
        oa, ob, x2, mod, a_out_norm[0].reshape(1, nq), b_out_norm[0].reshape(1, nq),
        attn_post_norm[0].reshape(1, d), ffn_pre_norm[0].reshape(1, d),
        w_out_bf, w_r2, b_r, seq)

    te = EXPERT_TE
    assert (2 * t) % te == 0
    cnt = counts[0, :N_EXPERTS].astype(jnp.int32)
    ends = jnp.cumsum(cnt)
    starts = ends - cnt
    run_dst = (starts[None, :] + tcarry[:, 0, :N_EXPERTS].astype(jnp.int32)).reshape(-1)
    run_len = tcnt[:, 0, :N_EXPERTS].astype(jnp.int32).reshape(-1)
    first_tile = starts // te
    n_vis = jnp.where(cnt > 0, (ends - 1) // te - first_tile + 1, 0)
    vis_end = jnp.cumsum(n_vis)
    n_visits = vis_end[-1]
    v = jnp.minimum(jnp.arange(2 * t // te + N_EXPERTS, dtype=jnp.int32), n_visits - 1)
    v_expert = jnp.sum(vis_end[None, :] <= v[:, None], axis=1).astype(jnp.int32)
    pick = (v_expert[:, None] == jnp.arange(N_EXPERTS)[None, :]).astype(jnp.int32)
    of_expert = lambda table: jnp.sum(pick * table[None, :], axis=1)
    v_tile = of_expert(first_tile) + v - of_expert(vis_end - n_vis)
    v_lo = jnp.maximum(of_expert(starts) - v_tile * te, 0)
    v_hi = jnp.minimum(of_expert(ends) - v_tile * te, te)

    xs = _dispatch(run_dst, run_len, h2, rt)
    ys = _expert_mlp(v_tile, v_expert, v_lo, v_hi, n_visits.reshape(1), xs,
                     w_gate[0], w_up[0], w_down[0])
    out = _combine(run_dst, run_len, x1, rinfo, mod, ffn_post_norm[0].reshape(1, d), ys, seq)
    return out.reshape(batch, seq, d)
```

```python
import functools

import jax
import jax.numpy as jnp
import numpy as np
from jax import lax
from jax.experimental import pallas as pl
from jax.experimental.pallas import tpu as pltpu

F32 = jnp.float32
BF16 = jnp.bfloat16

GRID_W = 64
HEAD_DIM = 64
KV_GROUP = 4
WINDOW = 128
ROPE_THETA = 10000.0
N_GROUPS = 4
EXPERTS_PER_GROUP = 8
N_EXPERTS = N_GROUPS * EXPERTS_PER_GROUP
EPS = 1e-6
NEG_INF = -1e30
LOG2_E = 1.4426950408889634
SAFE_SCORE_BOUND = 40.0

LANES = 128
V7X_VMEM_LIMIT = 56 * 1024 * 1024

MOD_ROWS = 16
MOD_BN = 1024
PROJ_TM = 512
ATTN_A_TQ = 256
ATTN_B_TQ = 256
OUT_TM = 512
EXPERT_TE = 512
MOVE_TM = OUT_TM


def _params(*sem):
    return pltpu.CompilerParams(dimension_semantics=sem, vmem_limit_bytes=V7X_VMEM_LIMIT)


def _dot(a, b):
    return jnp.dot(a, b, preferred_element_type=F32)


def _rms(x):
    return x * lax.rsqrt(jnp.mean(x * x, axis=-1, keepdims=True) + EPS)


def _split_bf16(x):
    hi = x.astype(BF16)
    lo = (x - hi.astype(F32)).astype(BF16)
    return hi, lo


def _mod_kernel(c_ref, w_ref, b_ref, o_ref):
    cc = c_ref[...]
    s = cc * jax.nn.sigmoid(cc)
    s_hi, s_lo = _split_bf16(s)
    w_hi, w_lo = _split_bf16(w_ref[...])
    o_ref[...] = _dot(s_hi, w_hi) + _dot(s_lo, w_hi) + _dot(s_hi, w_lo) + b_ref[...]


def _modulation(cc, w_mod, b_mod):
    rows, d = cc.shape
    n = w_mod.shape[1]
    bn = MOD_BN
    return pl.pallas_call(
        _mod_kernel,
        grid=(n // bn,),
        in_specs=[pl.BlockSpec((rows, d), lambda i: (0, 0)),
                  pl.BlockSpec((d, bn), lambda i: (0, i)),
                  pl.BlockSpec((1, bn), lambda i: (0, i))],
        out_specs=pl.BlockSpec((rows, bn), lambda i: (0, i)),
        out_shape=jax.ShapeDtypeStruct((rows, n), F32),
        compiler_params=_params("arbitrary"),
    )(cc, w_mod, b_mod.reshape(1, n))


def _rope(x, cos, sin_a, sin_b):
    return x * cos + pltpu.roll(x, LANES - 1, 1) * sin_a + pltpu.roll(x, 1, 1) * sin_b


def _head_norm(x, seg_ref, gain):
    ss = _dot((x * x).astype(BF16), seg_ref[...])
    return x * lax.rsqrt(ss * (1.0 / HEAD_DIM) + EPS) * gain


def _max_head_sq_norm(x, seg_ref):
    ss = _dot((x * x).astype(BF16), seg_ref[...])
    return jnp.max(jnp.max(ss, axis=1, keepdims=True), axis=0, keepdims=True)


def _norm_stats(q_sq, k_sq):
    lane = lax.broadcasted_iota(jnp.int32, (1, LANES), 1)
    zero = jnp.zeros((1, LANES), F32)
    return jnp.where(lane == 0, q_sq, zero) + jnp.where(lane == 1, k_sq, zero)


def _rope_t(xt, cos_t, sin_a_t, sin_b_t):
    return (xt * cos_t + pltpu.roll(xt, LANES - 1, 0) * sin_a_t
            + pltpu.roll(xt, 1, 0) * sin_b_t)


def _proj_kernel(x_ref, sc_ref, sh_ref, gpre_ref, w_ref, cos_ref, sa_ref, sb_ref,
                 cos_t_ref, sa_t_ref, sb_t_ref, qn_ref, kn_ref, seg_k_ref,
                 qat_ref, ka_ref, vat_ref, qbt_ref, kb_ref, vbt_ref, stats_ref):
    h = _rms(x_ref[...]) * gpre_ref[...] * (1.0 + sc_ref[0, 0]) + sh_ref[0, 0]
    p = _dot(h.astype(BF16), w_ref[...])
    cos, sa, sb = cos_ref[...], sa_ref[...], sb_ref[...]
    q_scale = HEAD_DIM ** -0.5 * LOG2_E
    cos_t, sa_t, sb_t = cos_t_ref[...] * q_scale, sa_t_ref[...] * q_scale, sb_t_ref[...] * q_scale
    nq = qat_ref.shape[0]
    q_sq = None
    for c in range(nq // LANES):
        xt = p[:, c * LANES:(c + 1) * LANES].T
        qat_ref[c * LANES:(c + 1) * LANES, :] = _rope_t(xt, cos_t, sa_t, sb_t).astype(BF16)
        sq = xt * xt
        for head in range(LANES // HEAD_DIM):
            norm = jnp.sum(sq[head * HEAD_DIM:(head + 1) * HEAD_DIM, :], axis=0, keepdims=True)
            q_sq = norm if q_sq is None else jnp.maximum(q_sq, norm)
    q_sq = jnp.max(q_sq, axis=1, keepdims=True) * (q_scale * q_scale)
    o = nq
    ka_ref[...] = _rope(p[:, o:o + LANES], cos, sa, sb).astype(BF16)
    vat_ref[...] = p[:, o + LANES:o + 2 * LANES].T.astype(BF16)
    stats_ref[0] = _norm_stats(q_sq, _max_head_sq_norm(p[:, o:o + LANES], seg_k_ref))
    o += 2 * LANES
    for c in range(nq // LANES):
        xt = p[:, o + c * LANES:o + (c + 1) * LANES].T
        sq = xt * xt
        halves = []
        for head in range(LANES // HEAD_DIM):
            rows = slice(head * HEAD_DIM, (head + 1) * HEAD_DIM)
            ms = jnp.sum(sq[rows, :], axis=0, keepdims=True) * (1.0 / HEAD_DIM)
            halves.append(xt[rows, :] * lax.rsqrt(ms + EPS))
        qn = jnp.concatenate(halves, axis=0) * qn_ref[c * LANES:(c + 1) * LANES, :]
        qbt_ref[c * LANES:(c + 1) * LANES, :] = _rope_t(qn, cos_t, sa_t, sb_t).astype(BF16)
    o += nq
    kb = _head_norm(p[:, o:o + LANES], seg_k_ref, kn_ref[...])
    kb_ref[...] = _rope(kb, cos, sa, sb).astype(BF16)
    vbt_ref[...] = p[:, o + LANES:o + 2 * LANES].T.astype(BF16)


def _ctx_proj_kernel(x_ref, sc_ref, sh_ref, gpre_ref, wa_ref, wb_ref, kn_ref, seg_k_ref,
                     ka_ref, vat_ref, kb_ref, vbt_ref, stats_ref):
    h = (_rms(x_ref[...]) * gpre_ref[...] * (1.0 + sc_ref[0, 0]) + sh_ref[0, 0]).astype(BF16)
    pa = _dot(h, wa_ref[...])
    pb = _dot(h, wb_ref[...])
    stats_ref[0] = _norm_stats(0.0, _max_head_sq_norm(pa[:, 0:LANES], seg_k_ref))
    ka_ref[...] = pa[:, 0:LANES].astype(BF16)
    vat_ref[...] = pa[:, LANES:2 * LANES].T.astype(BF16)
    kb_ref[...] = _head_norm(pb[:, 0:LANES], seg_k_ref, kn_ref[...]).astype(BF16)
    vbt_ref[...] = pb[:, LANES:2 * LANES].T.astype(BF16)


def _mod_spec(chunk, row_of_step):
    return lambda d: pl.BlockSpec((1, 1, 1, d), lambda i, *_: (row_of_step(i), chunk, 0, 0))


MOD_SH1, MOD_SC1, MOD_G1, MOD_SH2, MOD_SC2, MOD_G2 = range(6)


def _project_latents(x2, mod, gpre, w_in, tables, qn, kn, seg_k, seq):
    t, d = x2.shape
    tm = PROJ_TM
    tpb = seq // tm
    nq = qn.shape[0]
    const = lambda shape: pl.BlockSpec(shape, lambda i: (0,) * len(shape))
    batch_of = lambda i: i // tpb
    table = pl.BlockSpec((tm, LANES), lambda i: (i % tpb, 0))
    table_t = pl.BlockSpec((LANES, tm), lambda i: (0, i % tpb))
    k_spec = pl.BlockSpec((tm, LANES), lambda i: (i, 0))
    k_shape = jax.ShapeDtypeStruct((t, LANES), BF16)
    vt_spec = pl.BlockSpec((LANES, tm), lambda i: (0, i))
    vt_shape = jax.ShapeDtypeStruct((LANES, t), BF16)
    qt_spec = pl.BlockSpec((nq, tm), lambda i: (0, i))
    qt_shape = jax.ShapeDtypeStruct((nq, t), BF16)
    return pl.pallas_call(
        _proj_kernel,
        grid=(t // tm,),
        in_specs=[pl.BlockSpec((tm, d), lambda i: (i, 0)),
                  _mod_spec(MOD_SC1, batch_of)(d), _mod_spec(MOD_SH1, batch_of)(d), const((1, d)),
                  const(w_in.shape), table, table, table, table_t, table_t, table_t,
                  const((nq, tm)), const((1, LANES)), const(seg_k.shape)],
        out_specs=[qt_spec, k_spec, vt_spec, qt_spec, k_spec, vt_spec,
                   pl.BlockSpec((1, 1, LANES), lambda i: (i, 0, 0))],
        out_shape=[qt_shape, k_shape, vt_shape, qt_shape, k_shape, vt_shape,
                   jax.ShapeDtypeStruct((t // tm, 1, LANES), F32)],
        compiler_params=_params("arbitrary"),
    )(x2, mod, mod, gpre, w_in, *tables, qn, kn, seg_k)


def _project_context(c2, mod, ctx_row, gpre, w_in, kn, seg_k, ctx_len):
    t, d = c2.shape
    ctx_mod = lambda i: ctx_row
    nq = (w_in.shape[1] - 4 * LANES) // 2
    kv = 2 * LANES
    assert nq % kv == 0
    group_kv = lambda g: pl.BlockSpec((d, kv), lambda i: (0, (g * (nq + kv) + nq) // kv))
    const = lambda shape: pl.BlockSpec(shape, lambda i: (0,) * len(shape))
    k_spec = pl.BlockSpec((ctx_len, LANES), lambda i: (i, 0))
    k_shape = jax.ShapeDtypeStruct((t, LANES), BF16)
    vt_spec = pl.BlockSpec((LANES, ctx_len), lambda i: (0, i))
    vt_shape = jax.ShapeDtypeStruct((LANES, t), BF16)
    return pl.pallas_call(
        _ctx_proj_kernel,
        grid=(t // ctx_len,),
        in_specs=[pl.BlockSpec((ctx_len, d), lambda i: (i, 0)),
                  _mod_spec(MOD_SC1, ctx_mod)(d), _mod_spec(MOD_SH1, ctx_mod)(d),
                  const((1, d)), group_kv(0), group_kv(1), const((1, LANES)), const(seg_k.shape)],
        out_specs=[k_spec, vt_spec, k_spec, vt_spec,
                   pl.BlockSpec((1, 1, LANES), lambda i: (i, 0, 0))],
        out_shape=[k_shape, vt_shape, k_shape, vt_shape,
                   jax.ShapeDtypeStruct((t // ctx_len, 1, LANES), F32)],
        compiler_params=_params("arbitrary"),
    )(c2, mod, mod, gpre, w_in, w_in, kn, seg_k)


def _attend_t(w, k, kc, vt, vct, bounded=False, bias=None, sink=None):
    st = _dot(k, w)
    sct = _dot(kc, w)
    if bias is not None:
        tq = bias.shape[1]
        st = jnp.concatenate([st[:, c * tq:(c + 1) * tq] + bias
                              for c in range(st.shape[1] // tq)], axis=1)
    if not bounded:
        shift = jnp.maximum(jnp.max(st, axis=0, keepdims=True),
                            jnp.max(sct, axis=0, keepdims=True))
        if sink is not None:
            shift = jnp.maximum(shift, sink)
            sink = sink - shift
        st, sct = st - shift, sct - shift
    pt = jnp.exp2(st)
    pct = jnp.exp2(sct)
    denom = jnp.sum(pt, axis=0, keepdims=True) + jnp.sum(pct, axis=0, keepdims=True)
    if sink is not None:
        denom = denom + jnp.exp2(sink)
    o2 = _dot(vt, pt.astype(BF16)) + _dot(vct, pct.astype(BF16))
    return o2, denom


def _all_heads_t(qt_ref, o_ref, attend):
    tq = qt_ref.shape[1]
    n_kv = LANES // HEAD_DIM
    zeros = jnp.zeros((HEAD_DIM, tq), BF16)
    outs = []
    for h in range(qt_ref.shape[0] // HEAD_DIM):
        g = h // KV_GROUP
        qh = qt_ref[h * HEAD_DIM:(h + 1) * HEAD_DIM, :]
        w = jnp.concatenate([zeros] * g + [qh] + [zeros] * (n_kv - 1 - g), axis=0)
        o2, denom = attend(h, w)
        outs.append(o2[g * HEAD_DIM:(g + 1) * HEAD_DIM, :] / denom)
    o_ref[...] = jnp.concatenate(outs, axis=0).T.astype(BF16)


def _all_heads_fused_t(qt_ref, o_ref, attend):
    tq = qt_ref.shape[1]
    n_heads = qt_ref.shape[0] // HEAD_DIM
    n_kv = LANES // HEAD_DIM
    rows = []
    for g in range(n_kv):
        heads = [qt_ref[h * HEAD_DIM:(h + 1) * HEAD_DIM, :] if h // KV_GROUP == g
                 else jnp.zeros((HEAD_DIM, tq), BF16) for h in range(n_heads)]
        rows.append(jnp.concatenate(heads, axis=1))
    w = jnp.concatenate(rows, axis=0)
    o2, denom = attend(0, w)
    o2 = o2 / denom
    outs = [o2[(h // KV_GROUP) * HEAD_DIM:(h // KV_GROUP + 1) * HEAD_DIM, h * tq:(h + 1) * tq]
            for h in range(n_heads)]
    o_ref[...] = jnp.concatenate(outs, axis=0).T.astype(BF16)


def _attn_b_kernel(bound_ref, qt_ref, k_ref, vt_ref, kc_ref, vct_ref, o_ref):
    k, kc, vt, vct = k_ref[...], kc_ref[...], vt_ref[...], vct_ref[...]
    bound = bound_ref[0]

    @pl.when(bound <= SAFE_SCORE_BOUND)
    def _():
        _all_heads_fused_t(qt_ref, o_ref, lambda h, w: _attend_t(w, k, kc, vt, vct, bounded=True))

    @pl.when(jnp.logical_not(bound <= SAFE_SCORE_BOUND))
    def _():
        _all_heads_t(qt_ref, o_ref, lambda h, w: _attend_t(w, k, kc, vt, vct))


def _attn_specs(nq, tq, seq, ctx_len):
    nqb = seq // tq
    return dict(
        qt=pl.BlockSpec((nq, tq), lambda b, i: (0, b * nqb + i)),
        k=pl.BlockSpec((seq, LANES), lambda b, i: (b, 0)),
        vt=pl.BlockSpec((LANES, seq), lambda b, i: (0, b)),
        kc=pl.BlockSpec((ctx_len, LANES), lambda b, i: (b, 0)),
        vct=pl.BlockSpec((LANES, ctx_len), lambda b, i: (0, b)),
        out=pl.BlockSpec((tq, nq), lambda b, i: (b * nqb + i, 0)))


def _attention_b(score_bound, qbt, kb, vbt, kcb, vcbt, batch, seq, ctx_len):
    nq, t = qbt.shape
    tq = ATTN_B_TQ
    sp = _attn_specs(nq, tq, seq, ctx_len)
    return pl.pallas_call(
        _attn_b_kernel,
        grid=(batch, seq // tq),
        in_specs=[pl.BlockSpec(memory_space=pltpu.SMEM),
                  sp["qt"], sp["k"], sp["vt"], sp["kc"], sp["vct"]],
        out_specs=sp["out"],
        out_shape=jax.ShapeDtypeStruct((t, nq), BF16),
        compiler_params=_params("arbitrary", "arbitrary"),
    )(score_bound, qbt, kb, vbt, kcb, vcbt)


def _attn_a_kernel(bound_ref, sink_ref, qt_ref, k_ref, vt_ref, kc_ref, vct_ref, bias_ref, o_ref,
                   *, seq):
    i = pl.program_id(1)
    tq = qt_ref.shape[1]
    n_heads = qt_ref.shape[0] // HEAD_DIM
    win = tq + 2 * WINDOW
    start = pl.multiple_of(jnp.clip(i * tq - WINDOW, 0, seq - win), WINDOW)
    k = k_ref[pl.ds(start, win), :]
    vt = vt_ref[:, pl.ds(start, win)]
    kc, vct = kc_ref[...], vct_ref[...]
    sinks = [sink_ref[h] * LOG2_E for h in range(n_heads)]

    bound = bound_ref[0]
    small = bound <= SAFE_SCORE_BOUND

    @pl.when(small)
    def _():
        sink_row = jnp.concatenate([jnp.full((1, tq), s, F32) for s in sinks], axis=1)
        _all_heads_fused_t(qt_ref, o_ref, lambda h, w: _attend_t(
            w, k, kc, vt, vct, bounded=True, bias=bias_ref[0], sink=sink_row))

    @pl.when(jnp.logical_not(small))
    def _():
        _all_heads_t(qt_ref, o_ref, lambda h, w: _attend_t(
            w, k, kc, vt, vct, bias=bias_ref[0], sink=sinks[h]))


def _band_bias(tq):
    win = tq + 2 * WINDOW
    r = np.arange(win)[:, None]
    j = np.arange(tq)[None, :]
    tables = [np.where(np.abs(off + r - j) <= WINDOW, 0.0, NEG_INF)
              for off in (0, -WINDOW, -2 * WINDOW)]
    return jnp.asarray(np.stack(tables), F32)


def _attention_a(score_bound, sink, qat, ka, vat, kca, vcat, batch, seq, ctx_len):
    nq, t = qat.shape
    tq = ATTN_A_TQ
    nqb = seq // tq
    win = tq + 2 * WINDOW
    assert nqb >= 2 and tq >= WINDOW
    sp = _attn_specs(nq, tq, seq, ctx_len)
    which = lambda b, i: (jnp.where(i == 0, 0, jnp.where(i == nqb - 1, 2, 1)), 0, 0)
    return pl.pallas_call(
        functools.partial(_attn_a_kernel, seq=seq),
        grid=(batch, nqb),
        in_specs=[pl.BlockSpec(memory_space=pltpu.SMEM), pl.BlockSpec(memory_space=pltpu.SMEM),
                  sp["qt"], sp["k"], sp["vt"], sp["kc"], sp["vct"],
                  pl.BlockSpec((1, win, tq), which)],
        out_specs=sp["out"],
        out_shape=jax.ShapeDtypeStruct((t, nq), BF16),
        compiler_params=_params("arbitrary", "arbitrary"),
    )(score_bound, sink, qat, ka, vat, kca, vcat, _band_bias(tq))


def _out_kernel(oa_ref, ob_ref, x_ref, g1_ref, sc2_ref, sh2_ref, ga_ref, gb_ref, gpost_ref,
                gpre2_ref, woa_ref, wob_ref, wr_ref, br_ref,
                x1_ref, h2_ref, rinfo_ref, rt_ref, tcarry_ref, tcnt_ref, cnt_ref, carry_ref):
    step = pl.program_id(0)

    @pl.when(step == 0)
    def _():
        carry_ref[...] = jnp.zeros_like(carry_ref)

    na = _rms(oa_ref[...].astype(F32)) * ga_ref[...]
    nb = _rms(ob_ref[...].astype(F32)) * gb_ref[...]
    ox = _dot(na.astype(BF16), woa_ref[...]) + _dot(nb.astype(BF16), wob_ref[...])
    x1 = x_ref[...] + g1_ref[0, 0] * (_rms(ox) * gpost_ref[...])
    x1_ref[...] = x1
    h2 = _rms(x1) * gpre2_ref[...] * (1.0 + sc2_ref[0, 0]) + sh2_ref[0, 0]
    h_hi, h_lo = _split_bf16(h2)
    h2_ref[...] = h_hi

    both = _dot(h_hi, wr_ref[...])
    logits = (both[:, :LANES] + _dot(h_lo, wr_ref[:, :LANES]) + both[:, LANES:]
              + br_ref[...])
    tm = logits.shape[0]
    lt = logits.T
    row = lax.broadcasted_iota(jnp.int32, lt.shape, 0)
    rowf = row.astype(F32)
    big = jnp.float32(1e9)
    ninf = jnp.float32(-jnp.inf)
    colmax = lambda v: jnp.max(v, axis=0, keepdims=True)
    colmin = lambda v: jnp.min(v, axis=0, keepdims=True)
    colsum = lambda v: jnp.sum(v, axis=0, keepdims=True)

    gmask = (row >= N_EXPERTS) & (row < N_EXPERTS + N_GROUPS)
    lg = jnp.where(gmask, lt, ninf)
    gmax = colmax(lg)
    gidx = colmin(jnp.where(lg == gmax, rowf, big)) - N_EXPERTS
    g_w = 1.0 / colsum(jnp.exp(lg - gmax))
    row_group = (row // EXPERTS_PER_GROUP).astype(F32)
    emask = (row < N_EXPERTS) & (row_group == gidx)
    le = jnp.where(emask, lt, ninf)
    m1 = colmax(le)
    i1 = colmin(jnp.where(le == m1, rowf, big))
    le2 = jnp.where(rowf == i1, ninf, le)
    m2 = colmax(le2)
    i2 = colmin(jnp.where(le2 == m2, rowf, big))
    e2 = jnp.exp(m2 - m1)
    w0 = g_w / (1.0 + e2)
    w1 = g_w * e2 / (1.0 + e2)

    hit1 = rowf == i1
    hit2 = rowf == i2
    onehot = jnp.where(hit1, 1.0, jnp.where(hit2, 1.0, 0.0)).astype(F32)
    r = lax.broadcasted_iota(jnp.int32, (tm, tm), 0)
    c = lax.broadcasted_iota(jnp.int32, (tm, tm), 1)
    earlier = jnp.where(r < c, 1.0, 0.0).astype(BF16)
    within = _dot(onehot.astype(BF16), earlier)
    tile_cnt = jnp.broadcast_to(jnp.sum(onehot, axis=1, keepdims=True), (LANES, LANES))
    er = lax.broadcasted_iota(jnp.int32, (LANES, LANES), 0)
    ec = lax.broadcasted_iota(jnp.int32, (LANES, LANES), 1)
    below = jnp.where(er > ec, 1.0, 0.0).astype(BF16)
    cnt_hi = jnp.floor(tile_cnt * (1.0 / 32.0))
    cnt_lo = tile_cnt - 32.0 * cnt_hi
    run_start = 32.0 * _dot(below, cnt_hi.astype(BF16)) + _dot(below, cnt_lo.astype(BF16))
    local = within + run_start[:, 0:1]
    pos0 = colsum(jnp.where(hit1, local, 0.0))
    pos1 = colsum(jnp.where(hit2, local, 0.0))
    cnt_row = tile_cnt.T[0:1, :]
    tcarry_ref[0] = carry_ref[...]
    tcnt_ref[0] = cnt_row
    carry_ref[...] += cnt_row
    cnt_ref[...] = carry_ref[...]

    fields = jnp.concatenate([i1, i2, pos0, pos1, w0, w1, jnp.zeros((2, tm), F32)], axis=0)
    rt_ref[...] = fields
    rinfo_ref[...] = jnp.concatenate(
        [fields, jnp.zeros((LANES - 8, tm), F32)], axis=0).T


def _out_and_route(oa, ob, x2, mod, ga, gb, gpost, gpre2, w_out, wr, br, seq):
    t, d = x2.shape
    tm = OUT_TM
    tpb = seq // tm
    nq = oa.shape[1]
    const = lambda shape: pl.BlockSpec(shape, lambda i: (0,) * len(shape))
    batch_of = lambda i: i // tpb
    rows = lambda n: pl.BlockSpec((tm, n), lambda i: (i, 0))
    per_tile = pl.BlockSpec((1, 1, LANES), lambda i: (i, 0, 0))
    w_half = lambda g: pl.BlockSpec((nq, d), lambda i: (g, 0))
    return pl.pallas_call(
        _out_kernel,
        grid=(t // tm,),
        in_specs=[rows(nq), rows(nq), rows(d),
                  _mod_spec(MOD_G1, batch_of)(d), _mod_spec(MOD_SC2, batch_of)(d),
                  _mod_spec(MOD_SH2, batch_of)(d),
                  const((1, nq)), const((1, nq)), const((1, d)), const((1, d)),
                  w_half(0), w_half(1), const(wr.shape),
                  const((1, LANES))],
        out_specs=[rows(d), rows(d), rows(LANES), pl.BlockSpec((8, tm), lambda i: (0, i)),
                   per_tile, per_tile, const((1, LANES))],
        out_shape=[jax.ShapeDtypeStruct((t, d), F32), jax.ShapeDtypeStruct((t, d), BF16),
                   jax.ShapeDtypeStruct((t, LANES), F32), jax.ShapeDtypeStruct((8, t), F32),
                   jax.ShapeDtypeStruct((t // tm, 1, LANES), F32),
                   jax.ShapeDtypeStruct((t // tm, 1, LANES), F32),
                   jax.ShapeDtypeStruct((1, LANES), F32)],
        scratch_shapes=[pltpu.VMEM((1, LANES), F32)],
        compiler_params=_params("arbitrary"),
    )(oa, ob, x2, mod, mod, mod, ga, gb, gpost, gpre2, w_out, w_out, wr, br)


PACK_ROWS = 8
ROW_DTYPE = F32


def _pack_rows(ref, x):
    n = x.shape[0]
    for c in range(PACK_ROWS):
        ref[pl.ds(c, n, stride=PACK_ROWS), :] = x[:, c * LANES:(c + 1) * LANES]


def _unpack_rows(ref):
    n = ref.shape[0] // PACK_ROWS
    return jnp.concatenate(
        [ref[pl.ds(c, n, stride=PACK_ROWS), :].astype(BF16) for c in range(PACK_ROWS)], axis=1)


def _for_each_run_piece(rdst_ref, rlen_ref, tile, max_len, fn):
    n_bits = max_len.bit_length()

    def run(e, local):
        length = rlen_ref[tile * N_EXPERTS + e]
        dst = rdst_ref[tile * N_EXPERTS + e]
        for b in range(n_bits):
            size = 1 << b

            @pl.when(((length >> b) & 1) == 1)
            def _():
                done = length & (size - 1)
                fn(local + done, dst + done, size)
        return local + length

    lax.fori_loop(0, N_EXPERTS, run, 0)


def _token_rows(ref, row0, n_rows):
    start = row0 * PACK_ROWS
    if not isinstance(start, int):
        start = pl.multiple_of(start, PACK_ROWS)
    return ref.at[pl.ds(start, n_rows * PACK_ROWS)]


def _dispatch_kernel(rdst_ref, rlen_ref, h_ref, rt_ref, xs_ref, sorted_ref, sem):
    k = pl.program_id(0)
    nk = pl.num_programs(0)
    tm = h_ref.shape[0]
    rows = 2 * tm
    slot = k % 2

    def wait_slot(s):
        pltpu.make_async_copy(sorted_ref.at[s], _token_rows(xs_ref, 0, rows), sem.at[s]).wait()

    @pl.when(k >= 2)
    def _():
        wait_slot(slot)

    pos0 = rt_ref[2:3, :]
    pos1 = rt_ref[3:4, :]
    r = lax.broadcasted_iota(jnp.int32, (rows, tm), 0).astype(F32)
    perm = jnp.where((r == pos0) | (r == pos1), 1.0, 0.0).astype(BF16)
    srt = _dot(perm, h_ref[...].astype(BF16))
    buf = sorted_ref.at[slot]
    _pack_rows(buf, srt)

    def copy_piece(local, dst, size):
        pltpu.make_async_copy(_token_rows(buf, local, size), _token_rows(xs_ref, dst, size),
                              sem.at[slot]).start()

    _for_each_run_piece(rdst_ref, rlen_ref, k, tm, copy_piece)

    @pl.when(k == nk - 1)
    def _():
        wait_slot(slot)

        @pl.when(nk >= 2)
        def _():
            wait_slot(1 - slot)


def _dispatch(run_dst, run_len, h2, rt):
    t, d = h2.shape
    assert d == PACK_ROWS * LANES
    tm = MOVE_TM
    return pl.pallas_call(
        _dispatch_kernel,
        grid_spec=pltpu.PrefetchScalarGridSpec(
            num_scalar_prefetch=2,
            grid=(t // tm,),
            in_specs=[pl.BlockSpec((tm, d), lambda i, *_: (i, 0)),
                      pl.BlockSpec((8, tm), lambda i, *_: (0, i))],
            out_specs=pl.BlockSpec(memory_space=pl.ANY),
            scratch_shapes=[pltpu.VMEM((2, 2 * tm * PACK_ROWS, LANES), ROW_DTYPE),
                            pltpu.SemaphoreType.DMA((2,))]),
        out_shape=jax.ShapeDtypeStruct((2 * t * PACK_ROWS, LANES), ROW_DTYPE),
        compiler_params=_params("arbitrary"),
    )(run_dst, run_len, h2, rt)


def _expert_kernel(vt_ref, ve_ref, va_ref, vb_ref, vn_ref, nv_ref, xs_ref, wg_hbm, wu_hbm, wd_hbm,
                   ys_ref, wg_bf, wu_bf, wd_bf, wg_f32, wu_f32, wd_f32, wsem):
    v = pl.program_id(0)
    valid = v < nv_ref[0]
    prev = jnp.maximum(v - 1, 0)
    new_expert = (v == 0) | (ve_ref[v] != ve_ref[prev])
    new_tile = (v == 0) | (vt_ref[v] != vt_ref[prev])

    def weight_copies(e):
        return [pltpu.make_async_copy(src.at[e], dst, wsem)
                for src, dst in ((wg_hbm, wg_f32), (wu_hbm, wu_f32), (wd_hbm, wd_f32))]

    @pl.when(v == 0)
    def _():
        for cp in weight_copies(ve_ref[0]):
            cp.start()

    @pl.when(valid & new_expert)
    def _():
        for cp in weight_copies(ve_ref[v]):
            cp.wait()
        wg_bf[...] = wg_f32[...].astype(BF16)
        wu_bf[...] = wu_f32[...].astype(BF16)
        wd_bf[...] = wd_f32[...].astype(BF16)

        @pl.when(vn_ref[v] >= 0)
        def _():
            for cp in weight_copies(vn_ref[v]):
                cp.start()

    def expert_rows():
        xb = _unpack_rows(xs_ref)
        gate = _dot(xb, wg_bf[...])
        up = _dot(xb, wu_bf[...])
        act = gate * jax.nn.sigmoid(gate) * up
        return _dot(act.astype(BF16), wd_bf[...])

    @pl.when(valid & new_tile)
    def _():
        _pack_rows(ys_ref, expert_rows())

    @pl.when(valid & jnp.logical_not(new_tile))
    def _():
        y = expert_rows()
        te = y.shape[0]
        row = lax.broadcasted_iota(jnp.int32, (te, 1), 0)
        mine = (row >= va_ref[v]) & (row < vb_ref[v])
        for c in range(PACK_ROWS):
            rows = pl.ds(c, te, stride=PACK_ROWS)
            ys_ref[rows, :] = jnp.where(mine, y[:, c * LANES:(c + 1) * LANES], ys_ref[rows, :])


def _expert_mlp(visit_tile, visit_expert, visit_lo, visit_hi, visit_next, n_visits, xs,
                w_gate, w_up, w_down):
    te = EXPERT_TE
    d, ff = w_gate.shape[1:]
    blk = (te * PACK_ROWS, LANES)
    tile = lambda v, vt, *_: (vt[v], 0)
    hbm = pl.BlockSpec(memory_space=pl.ANY)
    return pl.pallas_call(
        _expert_kernel,
        grid_spec=pltpu.PrefetchScalarGridSpec(
            num_scalar_prefetch=6,
            grid=(visit_tile.shape[0],),
            in_specs=[pl.BlockSpec(blk, tile), hbm, hbm, hbm],
            out_specs=pl.BlockSpec(blk, tile),
            scratch_shapes=[pltpu.VMEM((d, ff), BF16), pltpu.VMEM((d, ff), BF16),
                            pltpu.VMEM((ff, d), BF16),
                            pltpu.VMEM((d, ff), F32), pltpu.VMEM((d, ff), F32),
                            pltpu.VMEM((ff, d), F32), pltpu.SemaphoreType.DMA(())]),
        out_shape=jax.ShapeDtypeStruct(xs.shape, ROW_DTYPE),
        compiler_params=_params("arbitrary"),
    )(visit_tile, visit_expert, visit_lo, visit_hi, visit_next, n_visits, xs,
      w_gate, w_up, w_down)


def _combine_kernel(rdst_ref, rlen_ref, x1_ref, rinfo_ref, g2_ref, gpost_ref, ys_ref, o_ref,
                    gath_ref, sem):
    k = pl.program_id(0)
    nk = pl.num_programs(0)
    tm = x1_ref.shape[0]
    rows = 2 * tm
    slot = k % 2

    def gather_runs(tile, s):
        buf = gath_ref.at[s]

        def copy_piece(local, src, size):
            pltpu.make_async_copy(_token_rows(ys_ref, src, size), _token_rows(buf, local, size),
                                  sem.at[s]).start()

        _for_each_run_piece(rdst_ref, rlen_ref, tile, tm, copy_piece)

    @pl.when(k == 0)
    def _():
        gather_runs(0, 0)

    @pl.when(k + 1 < nk)
    def _():
        gather_runs(k + 1, 1 - slot)

    buf = gath_ref.at[slot]
    pltpu.make_async_copy(_token_rows(ys_ref, 0, rows), buf, sem.at[slot]).wait()
    g = _unpack_rows(buf)
    info = rinfo_ref[...]
    col = lax.broadcasted_iota(jnp.int32, (tm, rows), 1).astype(F32)
    pick = jnp.where(col == info[:, 2:3], info[:, 4:5],
                     jnp.where(col == info[:, 3:4], info[:, 5:6], 0.0)).astype(BF16)
    fx = _dot(pick, g)
    o_ref[...] = x1_ref[...] + g2_ref[0, 0] * (_rms(fx) * gpost_ref[...])


def _combine(run_dst, run_len, x1, rinfo, mod, gpost, ys, seq):
    t, d = x1.shape
    tm = MOVE_TM
    tpb = seq // tm
    batch_of = lambda i: i // tpb
    return pl.pallas_call(
        _combine_kernel,
        grid_spec=pltpu.PrefetchScalarGridSpec(
            num_scalar_prefetch=2,
            grid=(t // tm,),
            in_specs=[pl.BlockSpec((tm, d), lambda i, *_: (i, 0)),
                      pl.BlockSpec((tm, LANES), lambda i, *_: (i, 0)),
                      _mod_spec(MOD_G2, batch_of)(d),
                      pl.BlockSpec((1, d), lambda i, *_: (0, 0)),
                      pl.BlockSpec(memory_space=pl.ANY)],
            out_specs=pl.BlockSpec((tm, d), lambda i, *_: (i, 0)),
            scratch_shapes=[pltpu.VMEM((2, 2 * tm * PACK_ROWS, LANES), ROW_DTYPE),
                            pltpu.SemaphoreType.DMA((2,))]),
        out_shape=jax.ShapeDtypeStruct((t, d), F32),
        compiler_params=_params("arbitrary"),
    )(run_dst, run_len, x1, rinfo, mod, gpost, ys)


def _rope_tables(seq):
    pos = np.arange(seq)
    row = (pos // GRID_W).astype(np.float32)
    col = (pos % GRID_W).astype(np.float32)
    axis_dim = HEAD_DIM // 2
    inv_freq = (ROPE_THETA ** (-np.arange(0, axis_dim, 2, dtype=np.float32) / axis_dim)).astype(
        np.float32)
    ang = np.concatenate([row[:, None] * inv_freq, col[:, None] * inv_freq], axis=-1)
    pair = (np.arange(LANES) % HEAD_DIM) // 2
    cos = np.cos(ang)[:, pair]
    sin = np.sin(ang)[:, pair]
    even = (np.arange(LANES) % 2) == 0
    tables = (cos, np.where(even, -sin, 0.0), np.where(even, 0.0, sin))
    tables = tables + tuple(tb.T for tb in tables)
    return tuple(jnp.asarray(tb, F32) for tb in tables)


def _segment_ones(n):
    seg = np.arange(n) // HEAD_DIM
    return jnp.asarray(seg[:, None] == seg[None, :], BF16)


def kernel(x, c, ctx, c_ctx, w_mod, b_mod, attn_pre_norm, attn_post_norm, w_in, a_sink,
           b_q_norm, b_k_norm, a_out_norm, b_out_norm, w_out, ffn_pre_norm, ffn_post_norm,
           w_group, b_group, w_router, b_router, w_gate, w_up, w_down):
    batch, seq, d = x.shape
    ctx_len = ctx.shape[1]
    assert w_mod.shape[0] == 1, "single-layer stack only (context stream is never updated)"
    assert seq % ATTN_A_TQ == 0 and seq >= ATTN_A_TQ + 2 * WINDOW
    assert seq % PROJ_TM == 0 and seq % ATTN_B_TQ == 0 and seq % OUT_TM == 0 and seq % MOVE_TM == 0
    t = batch * seq
    nq = d // 2
    nkv = nq // KV_GROUP
    assert nkv == LANES and w_in.shape[2] == 2 * nq + 4 * nkv

    assert batch + 1 <= MOD_ROWS
    cc = jnp.concatenate([c, c_ctx[None, :], jnp.zeros((MOD_ROWS - batch - 1, d), F32)], axis=0)
    mod = _modulation(cc, w_mod[0], b_mod[0]).reshape(cc.shape[0], 6, 1, d)

    x2 = x.reshape(t, d)
    c2 = ctx.reshape(batch * ctx_len, d)
    gpre = attn_pre_norm[0].reshape(1, d)
    w_in_bf = w_in[0].astype(BF16)
    qn = jnp.broadcast_to(jnp.tile(b_q_norm[0], nq // HEAD_DIM)[:, None], (nq, PROJ_TM))
    kn = jnp.tile(b_k_norm[0], nkv // HEAD_DIM).reshape(1, nkv)
    seg_k = _segment_ones(nkv)
    qat, ka, vat, qbt, kb, vbt, stats = _project_latents(
        x2, mod, gpre, w_in_bf, _rope_tables(seq), qn, kn, seg_k, seq)
    kca, vcat, kcb, vcbt, ctx_stats = _project_context(
        c2, mod, batch, gpre, w_in_bf, kn, seg_k, ctx_len)

    q_sq = jnp.max(stats[:, 0, 0])
    k_sq = jnp.maximum(jnp.max(stats[:, 0, 1]), jnp.max(ctx_stats[:, 0, 1]))
    bound_a = jnp.maximum(1.01 * jnp.sqrt(q_sq * k_sq), jnp.max(a_sink[0]) * LOG2_E).reshape(1)
    oa = _attention_a(bound_a, a_sink[0], qat, ka, vat, kca, vcat, batch, seq, ctx_len)
    score_bound = (1.01 * HEAD_DIM ** 0.5 * LOG2_E
                   * jnp.max(jnp.abs(b_q_norm[0])) * jnp.max(jnp.abs(b_k_norm[0]))).reshape(1)
    ob = _attention_b(score_bound, qbt, kb, vbt, kcb, vcbt, batch, seq, ctx_len)

    w_out_bf = w_out[0].astype(BF16)
    lane_pad = LANES - N_EXPERTS - N_GROUPS
    w_r = jnp.pad(jnp.concatenate([w_router[0], w_group[0]], axis=1), ((0, 0), (0, lane_pad)))
    w_r_hi = w_r.astype(BF16)
    w_r_lo = (w_r - w_r_hi.astype(F32)).astype(BF16)
    w_r2 = jnp.concatenate([w_r_hi, w_r_lo], axis=1)
    b_r = jnp.pad(jnp.concatenate([b_router[0], b_group[0]]), (0, lane_pad)).reshape(1, LANES)
    x1, h2, rinfo, rt, tcarry, tcnt, counts = _out_and_route(
        oa, ob, x2, mod, a_out_norm[0].reshape(1, nq), b_out_norm[0].reshape(1, nq),
        attn_post_norm[0].reshape(1, d), ffn_pre_norm[0].reshape(1, d),
        w_out_bf, w_r2, b_r, seq)

    te = EXPERT_TE
    assert (2 * t) % te == 0
    cnt = counts[0, :N_EXPERTS].astype(jnp.int32)
    ends = jnp.cumsum(cnt)
    starts = ends - cnt
    run_dst = (starts[None, :] + tcarry[:, 0, :N_EXPERTS].astype(jnp.int32)).reshape(-1)
    run_len = tcnt[:, 0, :N_EXPERTS].astype(jnp.int32).reshape(-1)
    first_tile = starts // te
    n_vis = jnp.where(cnt > 0, (ends - 1) // te - first_tile + 1, 0)
    vis_end = jnp.cumsum(n_vis)
    n_visits = vis_end[-1]
    v = jnp.minimum(jnp.arange(2 * t // te + N_EXPERTS, dtype=jnp.int32), n_visits - 1)
    v_expert = jnp.sum(vis_end[None, :] <= v[:, None], axis=1).astype(jnp.int32)
    pick = (v_expert[:, None] == jnp.arange(N_EXPERTS)[None, :]).astype(jnp.int32)
    of_expert = lambda table: jnp.sum(pick * table[None, :], axis=1)
    v_tile = of_expert(first_tile) + v - of_expert(vis_end - n_vis)
    v_lo = jnp.maximum(of_expert(starts) - v_tile * te, 0)
    v_hi = jnp.minimum(of_expert(ends) - v_tile * te, te)
    ids = jnp.arange(N_EXPERTS, dtype=jnp.int32)
    later = (ids[None, :] > ids[:, None]) & (cnt[None, :] > 0)
    next_expert = jnp.min(jnp.where(later, ids[None, :], N_EXPERTS), axis=1)
    v_next = of_expert(jnp.where(next_expert < N_EXPERTS, next_expert, -1))

    xs = _dispatch(run_dst, run_len, h2, rt)
    ys = _expert_mlp(v_tile, v_expert, v_lo, v_hi, v_next, n_visits.reshape(1), xs,
                     w_gate[0], w_up[0], w_down[0])
    out = _combine(run_dst, run_len, x1, rinfo, mod, ffn_post_norm[0].reshape(1, d), ys, seq)
    return out.reshape(batch, seq, d)
```

```python
import functools

import jax
import jax.numpy as jnp
import numpy as np
from jax import lax
from jax.experimental import pallas as pl
from jax.experimental.pallas import tpu as pltpu

F32 = jnp.float32
BF16 = jnp.bfloat16

GRID_W = 64
HEAD_DIM = 64
KV_GROUP = 4
WINDOW = 128
ROPE_THETA = 10000.0
N_GROUPS = 4
EXPERTS_PER_GROUP = 8
N_EXPERTS = N_GROUPS * EXPERTS_PER_GROUP
EPS = 1e-6
NEG_INF = -1e30
LOG2_E = 1.4426950408889634
SAFE_SCORE_BOUND = 40.0

LANES = 128
V7X_VMEM_LIMIT = 56 * 1024 * 1024

MOD_ROWS = 16
MOD_BN = 1024
PROJ_TM = 512
ATTN_A_TQ = 256
ATTN_B_TQ = 256
OUT_TM = 512
EXPERT_TE = 256
MOVE_TM = OUT_TM


def _params(*sem):
    return pltpu.CompilerParams(dimension_semantics=sem, vmem_limit_bytes=V7X_VMEM_LIMIT)


def _dot(a, b):
    return jnp.dot(a, b, preferred_element_type=F32)


def _rms(x):
    return x * lax.rsqrt(jnp.mean(x * x, axis=-1, keepdims=True) + EPS)


def _split_bf16(x):
    hi = x.astype(BF16)
    lo = (x - hi.astype(F32)).astype(BF16)
    return hi, lo


def _mod_kernel(c_ref, w_ref, b_ref, o_ref):
    cc = c_ref[...]
    s = cc * jax.nn.sigmoid(cc)
    s_hi, s_lo = _split_bf16(s)
    w_hi, w_lo = _split_bf16(w_ref[...])
    o_ref[...] = _dot(s_hi, w_hi) + _dot(s_lo, w_hi) + _dot(s_hi, w_lo) + b_ref[...]


def _modulation(cc, w_mod, b_mod):
    rows, d = cc.shape
    n = w_mod.shape[1]
    bn = MOD_BN
    return pl.pallas_call(
        _mod_kernel,
        grid=(n // bn,),
        in_specs=[pl.BlockSpec((rows, d), lambda i: (0, 0)),
                  pl.BlockSpec((d, bn), lambda i: (0, i)),
                  pl.BlockSpec((1, bn), lambda i: (0, i))],
        out_specs=pl.BlockSpec((rows, bn), lambda i: (0, i)),
        out_shape=jax.ShapeDtypeStruct((rows, n), F32),
        compiler_params=_params("arbitrary"),
    )(cc, w_mod, b_mod.reshape(1, n))


def _rope(x, cos, sin_a, sin_b):
    return x * cos + pltpu.roll(x, LANES - 1, 1) * sin_a + pltpu.roll(x, 1, 1) * sin_b


def _head_norm(x, seg_ref, gain):
    ss = _dot((x * x).astype(BF16), seg_ref[...])
    return x * lax.rsqrt(ss * (1.0 / HEAD_DIM) + EPS) * gain


def _max_head_sq_norm(x, seg_ref):
    ss = _dot((x * x).astype(BF16), seg_ref[...])
    return jnp.max(jnp.max(ss, axis=1, keepdims=True), axis=0, keepdims=True)


def _norm_stats(q_sq, k_sq):
    lane = lax.broadcasted_iota(jnp.int32, (1, LANES), 1)
    zero = jnp.zeros((1, LANES), F32)
    return jnp.where(lane == 0, q_sq, zero) + jnp.where(lane == 1, k_sq, zero)


def _rope_t(xt, cos_t, sin_a_t, sin_b_t):
    return (xt * cos_t + pltpu.roll(xt, LANES - 1, 0) * sin_a_t
            + pltpu.roll(xt, 1, 0) * sin_b_t)


def _proj_kernel(x_ref, sc_ref, sh_ref, gpre_ref, w_ref, cos_ref, sa_ref, sb_ref,
                 cos_t_ref, sa_t_ref, sb_t_ref, qn_ref, kn_ref, seg_k_ref,
                 qat_ref, ka_ref, vat_ref, qbt_ref, kb_ref, vbt_ref, stats_ref):
    h = _rms(x_ref[...]) * gpre_ref[...] * (1.0 + sc_ref[0, 0]) + sh_ref[0, 0]
    p = _dot(h.astype(BF16), w_ref[...])
    cos, sa, sb = cos_ref[...], sa_ref[...], sb_ref[...]
    q_scale = HEAD_DIM ** -0.5 * LOG2_E
    cos_t, sa_t, sb_t = cos_t_ref[...] * q_scale, sa_t_ref[...] * q_scale, sb_t_ref[...] * q_scale
    nq = qat_ref.shape[0]
    q_sq = None
    for c in range(nq // LANES):
        xt = p[:, c * LANES:(c + 1) * LANES].T
        qat_ref[c * LANES:(c + 1) * LANES, :] = _rope_t(xt, cos_t, sa_t, sb_t).astype(BF16)
        sq = xt * xt
        for head in range(LANES // HEAD_DIM):
            norm = jnp.sum(sq[head * HEAD_DIM:(head + 1) * HEAD_DIM, :], axis=0, keepdims=True)
            q_sq = norm if q_sq is None else jnp.maximum(q_sq, norm)
    q_sq = jnp.max(q_sq, axis=1, keepdims=True) * (q_scale * q_scale)
    o = nq
    ka_ref[...] = _rope(p[:, o:o + LANES], cos, sa, sb).astype(BF16)
    vat_ref[...] = p[:, o + LANES:o + 2 * LANES].T.astype(BF16)
    stats_ref[0] = _norm_stats(q_sq, _max_head_sq_norm(p[:, o:o + LANES], seg_k_ref))
    o += 2 * LANES
    for c in range(nq // LANES):
        xt = p[:, o + c * LANES:o + (c + 1) * LANES].T
        sq = xt * xt
        halves = []
        for head in range(LANES // HEAD_DIM):
            rows = slice(head * HEAD_DIM, (head + 1) * HEAD_DIM)
            ms = jnp.sum(sq[rows, :], axis=0, keepdims=True) * (1.0 / HEAD_DIM)
            halves.append(xt[rows, :] * lax.rsqrt(ms + EPS))
        qn = jnp.concatenate(halves, axis=0) * qn_ref[c * LANES:(c + 1) * LANES, :]
        qbt_ref[c * LANES:(c + 1) * LANES, :] = _rope_t(qn, cos_t, sa_t, sb_t).astype(BF16)
    o += nq
    kb = _head_norm(p[:, o:o + LANES], seg_k_ref, kn_ref[...])
    kb_ref[...] = _rope(kb, cos, sa, sb).astype(BF16)
    vbt_ref[...] = p[:, o + LANES:o + 2 * LANES].T.astype(BF16)


def _ctx_proj_kernel(x_ref, sc_ref, sh_ref, gpre_ref, wa_ref, wb_ref, kn_ref, seg_k_ref,
                     ka_ref, vat_ref, kb_ref, vbt_ref, stats_ref):
    h = (_rms(x_ref[...]) * gpre_ref[...] * (1.0 + sc_ref[0, 0]) + sh_ref[0, 0]).astype(BF16)
    pa = _dot(h, wa_ref[...])
    pb = _dot(h, wb_ref[...])
    stats_ref[0] = _norm_stats(0.0, _max_head_sq_norm(pa[:, 0:LANES], seg_k_ref))
    ka_ref[...] = pa[:, 0:LANES].astype(BF16)
    vat_ref[...] = pa[:, LANES:2 * LANES].T.astype(BF16)
    kb_ref[...] = _head_norm(pb[:, 0:LANES], seg_k_ref, kn_ref[...]).astype(BF16)
    vbt_ref[...] = pb[:, LANES:2 * LANES].T.astype(BF16)


def _mod_spec(chunk, row_of_step):
    return lambda d: pl.BlockSpec((1, 1, 1, d), lambda i, *_: (row_of_step(i), chunk, 0, 0))


MOD_SH1, MOD_SC1, MOD_G1, MOD_SH2, MOD_SC2, MOD_G2 = range(6)


def _project_latents(x2, mod, gpre, w_in, tables, qn, kn, seg_k, seq):
    t, d = x2.shape
    tm = PROJ_TM
    tpb = seq // tm
    nq = qn.shape[0]
    const = lambda shape: pl.BlockSpec(shape, lambda i: (0,) * len(shape))
    batch_of = lambda i: i // tpb
    table = pl.BlockSpec((tm, LANES), lambda i: (i % tpb, 0))
    table_t = pl.BlockSpec((LANES, tm), lambda i: (0, i % tpb))
    k_spec = pl.BlockSpec((tm, LANES), lambda i: (i, 0))
    k_shape = jax.ShapeDtypeStruct((t, LANES), BF16)
    vt_spec = pl.BlockSpec((LANES, tm), lambda i: (0, i))
    vt_shape = jax.ShapeDtypeStruct((LANES, t), BF16)
    qt_spec = pl.BlockSpec((nq, tm), lambda i: (0, i))
    qt_shape = jax.ShapeDtypeStruct((nq, t), BF16)
    return pl.pallas_call(
        _proj_kernel,
        grid=(t // tm,),
        in_specs=[pl.BlockSpec((tm, d), lambda i: (i, 0)),
                  _mod_spec(MOD_SC1, batch_of)(d), _mod_spec(MOD_SH1, batch_of)(d), const((1, d)),
                  const(w_in.shape), table, table, table, table_t, table_t, table_t,
                  const((nq, tm)), const((1, LANES)), const(seg_k.shape)],
        out_specs=[qt_spec, k_spec, vt_spec, qt_spec, k_spec, vt_spec,
                   pl.BlockSpec((1, 1, LANES), lambda i: (i, 0, 0))],
        out_shape=[qt_shape, k_shape, vt_shape, qt_shape, k_shape, vt_shape,
                   jax.ShapeDtypeStruct((t // tm, 1, LANES), F32)],
        compiler_params=_params("arbitrary"),
    )(x2, mod, mod, gpre, w_in, *tables, qn, kn, seg_k)


def _project_context(c2, mod, ctx_row, gpre, w_in, kn, seg_k, ctx_len):
    t, d = c2.shape
    ctx_mod = lambda i: ctx_row
    nq = (w_in.shape[1] - 4 * LANES) // 2
    kv = 2 * LANES
    assert nq % kv == 0
    group_kv = lambda g: pl.BlockSpec((d, kv), lambda i: (0, (g * (nq + kv) + nq) // kv))
    const = lambda shape: pl.BlockSpec(shape, lambda i: (0,) * len(shape))
    k_spec = pl.BlockSpec((ctx_len, LANES), lambda i: (i, 0))
    k_shape = jax.ShapeDtypeStruct((t, LANES), BF16)
    vt_spec = pl.BlockSpec((LANES, ctx_len), lambda i: (0, i))
    vt_shape = jax.ShapeDtypeStruct((LANES, t), BF16)
    return pl.pallas_call(
        _ctx_proj_kernel,
        grid=(t // ctx_len,),
        in_specs=[pl.BlockSpec((ctx_len, d), lambda i: (i, 0)),
                  _mod_spec(MOD_SC1, ctx_mod)(d), _mod_spec(MOD_SH1, ctx_mod)(d),
                  const((1, d)), group_kv(0), group_kv(1), const((1, LANES)), const(seg_k.shape)],
        out_specs=[k_spec, vt_spec, k_spec, vt_spec,
                   pl.BlockSpec((1, 1, LANES), lambda i: (i, 0, 0))],
        out_shape=[k_shape, vt_shape, k_shape, vt_shape,
                   jax.ShapeDtypeStruct((t // ctx_len, 1, LANES), F32)],
        compiler_params=_params("arbitrary"),
    )(c2, mod, mod, gpre, w_in, w_in, kn, seg_k)


def _attend_t(w, k, kc, vt, vct, bounded=False, bias=None, sink=None):
    st = _dot(k, w)
    sct = _dot(kc, w)
    if bias is not None:
        tq = bias.shape[1]
        st = jnp.concatenate([st[:, c * tq:(c + 1) * tq] + bias
                              for c in range(st.shape[1] // tq)], axis=1)
    if not bounded:
        shift = jnp.maximum(jnp.max(st, axis=0, keepdims=True),
                            jnp.max(sct, axis=0, keepdims=True))
        if sink is not None:
            shift = jnp.maximum(shift, sink)
            sink = sink - shift
        st, sct = st - shift, sct - shift
    pt = jnp.exp2(st)
    pct = jnp.exp2(sct)
    denom = jnp.sum(pt, axis=0, keepdims=True) + jnp.sum(pct, axis=0, keepdims=True)
    if sink is not None:
        denom = denom + jnp.exp2(sink)
    o2 = _dot(vt, pt.astype(BF16)) + _dot(vct, pct.astype(BF16))
    return o2, denom


def _all_heads_t(qt_ref, o_ref, attend):
    tq = qt_ref.shape[1]
    n_kv = LANES // HEAD_DIM
    zeros = jnp.zeros((HEAD_DIM, tq), BF16)
    outs = []
    for h in range(qt_ref.shape[0] // HEAD_DIM):
        g = h // KV_GROUP
        qh = qt_ref[h * HEAD_DIM:(h + 1) * HEAD_DIM, :]
        w = jnp.concatenate([zeros] * g + [qh] + [zeros] * (n_kv - 1 - g), axis=0)
        o2, denom = attend(h, w)
        outs.append(o2[g * HEAD_DIM:(g + 1) * HEAD_DIM, :] / denom)
    o_ref[...] = jnp.concatenate(outs, axis=0).T.astype(BF16)


def _all_heads_fused_t(qt_ref, o_ref, attend):
    tq = qt_ref.shape[1]
    n_heads = qt_ref.shape[0] // HEAD_DIM
    n_kv = LANES // HEAD_DIM
    rows = []
    for g in range(n_kv):
        heads = [qt_ref[h * HEAD_DIM:(h + 1) * HEAD_DIM, :] if h // KV_GROUP == g
                 else jnp.zeros((HEAD_DIM, tq), BF16) for h in range(n_heads)]
        rows.append(jnp.concatenate(heads, axis=1))
    w = jnp.concatenate(rows, axis=0)
    o2, denom = attend(0, w)
    o2 = o2 / denom
    outs = [o2[(h // KV_GROUP) * HEAD_DIM:(h // KV_GROUP + 1) * HEAD_DIM, h * tq:(h + 1) * tq]
            for h in range(n_heads)]
    o_ref[...] = jnp.concatenate(outs, axis=0).T.astype(BF16)


def _attn_b_kernel(bound_ref, qt_ref, k_ref, vt_ref, kc_ref, vct_ref, o_ref):
    k, kc, vt, vct = k_ref[...], kc_ref[...], vt_ref[...], vct_ref[...]
    bound = bound_ref[0]

    @pl.when(bound <= SAFE_SCORE_BOUND)
    def _():
        _all_heads_fused_t(qt_ref, o_ref, lambda h, w: _attend_t(w, k, kc, vt, vct, bounded=True))

    @pl.when(jnp.logical_not(bound <= SAFE_SCORE_BOUND))
    def _():
        _all_heads_t(qt_ref, o_ref, lambda h, w: _attend_t(w, k, kc, vt, vct))


def _attn_specs(nq, tq, seq, ctx_len):
    nqb = seq // tq
    return dict(
        qt=pl.BlockSpec((nq, tq), lambda b, i: (0, b * nqb + i)),
        k=pl.BlockSpec((seq, LANES), lambda b, i: (b, 0)),
        vt=pl.BlockSpec((LANES, seq), lambda b, i: (0, b)),
        kc=pl.BlockSpec((ctx_len, LANES), lambda b, i: (b, 0)),
        vct=pl.BlockSpec((LANES, ctx_len), lambda b, i: (0, b)),
        out=pl.BlockSpec((tq, nq), lambda b, i: (b * nqb + i, 0)))


def _attention_b(score_bound, qbt, kb, vbt, kcb, vcbt, batch, seq, ctx_len):
    nq, t = qbt.shape
    tq = ATTN_B_TQ
    sp = _attn_specs(nq, tq, seq, ctx_len)
    return pl.pallas_call(
        _attn_b_kernel,
        grid=(batch, seq // tq),
        in_specs=[pl.BlockSpec(memory_space=pltpu.SMEM),
                  sp["qt"], sp["k"], sp["vt"], sp["kc"], sp["vct"]],
        out_specs=sp["out"],
        out_shape=jax.ShapeDtypeStruct((t, nq), BF16),
        compiler_params=_params("arbitrary", "arbitrary"),
    )(score_bound, qbt, kb, vbt, kcb, vcbt)


def _attn_a_kernel(bound_ref, sink_ref, qt_ref, k_ref, vt_ref, kc_ref, vct_ref, bias_ref, o_ref,
                   *, seq):
    i = pl.program_id(1)
    tq = qt_ref.shape[1]
    n_heads = qt_ref.shape[0] // HEAD_DIM
    win = tq + 2 * WINDOW
    start = pl.multiple_of(jnp.clip(i * tq - WINDOW, 0, seq - win), WINDOW)
    k = k_ref[pl.ds(start, win), :]
    vt = vt_ref[:, pl.ds(start, win)]
    kc, vct = kc_ref[...], vct_ref[...]
    sinks = [sink_ref[h] * LOG2_E for h in range(n_heads)]

    bound = bound_ref[0]
    small = bound <= SAFE_SCORE_BOUND

    @pl.when(small)
    def _():
        sink_row = jnp.concatenate([jnp.full((1, tq), s, F32) for s in sinks], axis=1)
        _all_heads_fused_t(qt_ref, o_ref, lambda h, w: _attend_t(
            w, k, kc, vt, vct, bounded=True, bias=bias_ref[0], sink=sink_row))

    @pl.when(jnp.logical_not(small))
    def _():
        _all_heads_t(qt_ref, o_ref, lambda h, w: _attend_t(
            w, k, kc, vt, vct, bias=bias_ref[0], sink=sinks[h]))


def _band_bias(tq):
    win = tq + 2 * WINDOW
    r = np.arange(win)[:, None]
    j = np.arange(tq)[None, :]
    tables = [np.where(np.abs(off + r - j) <= WINDOW, 0.0, NEG_INF)
              for off in (0, -WINDOW, -2 * WINDOW)]
    return jnp.asarray(np.stack(tables), F32)


def _attention_a(score_bound, sink, qat, ka, vat, kca, vcat, batch, seq, ctx_len):
    nq, t = qat.shape
    tq = ATTN_A_TQ
    nqb = seq // tq
    win = tq + 2 * WINDOW
    assert nqb >= 2 and tq >= WINDOW
    sp = _attn_specs(nq, tq, seq, ctx_len)
    which = lambda b, i: (jnp.where(i == 0, 0, jnp.where(i == nqb - 1, 2, 1)), 0, 0)
    return pl.pallas_call(
        functools.partial(_attn_a_kernel, seq=seq),
        grid=(batch, nqb),
        in_specs=[pl.BlockSpec(memory_space=pltpu.SMEM), pl.BlockSpec(memory_space=pltpu.SMEM),
                  sp["qt"], sp["k"], sp["vt"], sp["kc"], sp["vct"],
                  pl.BlockSpec((1, win, tq), which)],
        out_specs=sp["out"],
        out_shape=jax.ShapeDtypeStruct((t, nq), BF16),
        compiler_params=_params("arbitrary", "arbitrary"),
    )(score_bound, sink, qat, ka, vat, kca, vcat, _band_bias(tq))


def _out_kernel(oa_ref, ob_ref, x_ref, g1_ref, sc2_ref, sh2_ref, ga_ref, gb_ref, gpost_ref,
                gpre2_ref, woa_ref, wob_ref, wr_ref, br_ref,
                x1_ref, h2_ref, rinfo_ref, rt_ref, tcarry_ref, tcnt_ref, cnt_ref, carry_ref):
    step = pl.program_id(0)

    @pl.when(step == 0)
    def _():
        carry_ref[...] = jnp.zeros_like(carry_ref)

    na = _rms(oa_ref[...].astype(F32)) * ga_ref[...]
    nb = _rms(ob_ref[...].astype(F32)) * gb_ref[...]
    ox = _dot(na.astype(BF16), woa_ref[...]) + _dot(nb.astype(BF16), wob_ref[...])
    x1 = x_ref[...] + g1_ref[0, 0] * (_rms(ox) * gpost_ref[...])
    x1_ref[...] = x1
    h2 = _rms(x1) * gpre2_ref[...] * (1.0 + sc2_ref[0, 0]) + sh2_ref[0, 0]
    h_hi, h_lo = _split_bf16(h2)
    h2_ref[...] = h_hi

    both = _dot(h_hi, wr_ref[...])
    logits = (both[:, :LANES] + _dot(h_lo, wr_ref[:, :LANES]) + both[:, LANES:]
              + br_ref[...])
    tm = logits.shape[0]
    lt = logits.T
    row = lax.broadcasted_iota(jnp.int32, lt.shape, 0)
    rowf = row.astype(F32)
    big = jnp.float32(1e9)
    ninf = jnp.float32(-jnp.inf)
    colmax = lambda v: jnp.max(v, axis=0, keepdims=True)
    colmin = lambda v: jnp.min(v, axis=0, keepdims=True)
    colsum = lambda v: jnp.sum(v, axis=0, keepdims=True)

    gmask = (row >= N_EXPERTS) & (row < N_EXPERTS + N_GROUPS)
    lg = jnp.where(gmask, lt, ninf)
    gmax = colmax(lg)
    gidx = colmin(jnp.where(lg == gmax, rowf, big)) - N_EXPERTS
    g_w = 1.0 / colsum(jnp.exp(lg - gmax))
    row_group = (row // EXPERTS_PER_GROUP).astype(F32)
    emask = (row < N_EXPERTS) & (row_group == gidx)
    le = jnp.where(emask, lt, ninf)
    m1 = colmax(le)
    i1 = colmin(jnp.where(le == m1, rowf, big))
    le2 = jnp.where(rowf == i1, ninf, le)
    m2 = colmax(le2)
    i2 = colmin(jnp.where(le2 == m2, rowf, big))
    e2 = jnp.exp(m2 - m1)
    w0 = g_w / (1.0 + e2)
    w1 = g_w * e2 / (1.0 + e2)

    hit1 = rowf == i1
    hit2 = rowf == i2
    onehot = jnp.where(hit1, 1.0, jnp.where(hit2, 1.0, 0.0)).astype(F32)
    r = lax.broadcasted_iota(jnp.int32, (tm, tm), 0)
    c = lax.broadcasted_iota(jnp.int32, (tm, tm), 1)
    earlier = jnp.where(r < c, 1.0, 0.0).astype(BF16)
    within = _dot(onehot.astype(BF16), earlier)
    tile_cnt = jnp.broadcast_to(jnp.sum(onehot, axis=1, keepdims=True), (LANES, LANES))
    er = lax.broadcasted_iota(jnp.int32, (LANES, LANES), 0)
    ec = lax.broadcasted_iota(jnp.int32, (LANES, LANES), 1)
    below = jnp.where(er > ec, 1.0, 0.0).astype(BF16)
    cnt_hi = jnp.floor(tile_cnt * (1.0 / 32.0))
    cnt_lo = tile_cnt - 32.0 * cnt_hi
    run_start = 32.0 * _dot(below, cnt_hi.astype(BF16)) + _dot(below, cnt_lo.astype(BF16))
    local = within + run_start[:, 0:1]
    pos0 = colsum(jnp.where(hit1, local, 0.0))
    pos1 = colsum(jnp.where(hit2, local, 0.0))
    cnt_row = tile_cnt.T[0:1, :]
    tcarry_ref[0] = carry_ref[...]
    tcnt_ref[0] = cnt_row
    carry_ref[...] += cnt_row
    cnt_ref[...] = carry_ref[...]

    fields = jnp.concatenate([i1, i2, pos0, pos1, w0, w1, jnp.zeros((2, tm), F32)], axis=0)
    rt_ref[...] = fields
    rinfo_ref[...] = jnp.concatenate(
        [fields, jnp.zeros((LANES - 8, tm), F32)], axis=0).T


def _out_and_route(oa, ob, x2, mod, ga, gb, gpost, gpre2, w_out, wr, br, seq):
    t, d = x2.shape
    tm = OUT_TM
    tpb = seq // tm
    nq = oa.shape[1]
    const = lambda shape: pl.BlockSpec(shape, lambda i: (0,) * len(shape))
    batch_of = lambda i: i // tpb
    rows = lambda n: pl.BlockSpec((tm, n), lambda i: (i, 0))
    per_tile = pl.BlockSpec((1, 1, LANES), lambda i: (i, 0, 0))
    w_half = lambda g: pl.BlockSpec((nq, d), lambda i: (g, 0))
    return pl.pallas_call(
        _out_kernel,
        grid=(t // tm,),
        in_specs=[rows(nq), rows(nq), rows(d),
                  _mod_spec(MOD_G1, batch_of)(d), _mod_spec(MOD_SC2, batch_of)(d),
                  _mod_spec(MOD_SH2, batch_of)(d),
                  const((1, nq)), const((1, nq)), const((1, d)), const((1, d)),
                  w_half(0), w_half(1), const(wr.shape),
                  const((1, LANES))],
        out_specs=[rows(d), rows(d), rows(LANES), pl.BlockSpec((8, tm), lambda i: (0, i)),
                   per_tile, per_tile, const((1, LANES))],
        out_shape=[jax.ShapeDtypeStruct((t, d), F32), jax.ShapeDtypeStruct((t, d), BF16),
                   jax.ShapeDtypeStruct((t, LANES), F32), jax.ShapeDtypeStruct((8, t), F32),
                   jax.ShapeDtypeStruct((t // tm, 1, LANES), F32),
                   jax.ShapeDtypeStruct((t // tm, 1, LANES), F32),
                   jax.ShapeDtypeStruct((1, LANES), F32)],
        scratch_shapes=[pltpu.VMEM((1, LANES), F32)],
        compiler_params=_params("arbitrary"),
    )(oa, ob, x2, mod, mod, mod, ga, gb, gpost, gpre2, w_out, w_out, wr, br)


PACK_ROWS = 8
ROW_DTYPE = F32


def _pack_rows(ref, x):
    n = x.shape[0]
    for c in range(PACK_ROWS):
        ref[pl.ds(c, n, stride=PACK_ROWS), :] = x[:, c * LANES:(c + 1) * LANES]


def _unpack_rows(ref):
    n = ref.shape[0] // PACK_ROWS
    return jnp.concatenate(
        [ref[pl.ds(c, n, stride=PACK_ROWS), :].astype(BF16) for c in range(PACK_ROWS)], axis=1)


def _for_each_run_piece(rdst_ref, rlen_ref, tile, max_len, fn):
    n_bits = max_len.bit_length()

    def run(e, local):
        length = rlen_ref[tile * N_EXPERTS + e]
        dst = rdst_ref[tile * N_EXPERTS + e]
        for b in range(n_bits):
            size = 1 << b

            @pl.when(((length >> b) & 1) == 1)
            def _():
                done = length & (size - 1)
                fn(local + done, dst + done, size)
        return local + length

    lax.fori_loop(0, N_EXPERTS, run, 0)


def _token_rows(ref, row0, n_rows):
    start = row0 * PACK_ROWS
    if not isinstance(start, int):
        start = pl.multiple_of(start, PACK_ROWS)
    return ref.at[pl.ds(start, n_rows * PACK_ROWS)]


def _dispatch_kernel(rdst_ref, rlen_ref, h_ref, rt_ref, xs_ref, sorted_ref, sem):
    k = pl.program_id(0)
    nk = pl.num_programs(0)
    tm = h_ref.shape[0]
    rows = 2 * tm
    slot = k % 2

    def wait_slot(s):
        pltpu.make_async_copy(sorted_ref.at[s], _token_rows(xs_ref, 0, rows), sem.at[s]).wait()

    @pl.when(k >= 2)
    def _():
        wait_slot(slot)

    pos0 = rt_ref[2:3, :]
    pos1 = rt_ref[3:4, :]
    r = lax.broadcasted_iota(jnp.int32, (rows, tm), 0).astype(F32)
    perm = jnp.where((r == pos0) | (r == pos1), 1.0, 0.0).astype(BF16)
    srt = _dot(perm, h_ref[...].astype(BF16))
    buf = sorted_ref.at[slot]
    _pack_rows(buf, srt)

    def copy_piece(local, dst, size):
        pltpu.make_async_copy(_token_rows(buf, local, size), _token_rows(xs_ref, dst, size),
                              sem.at[slot]).start()

    _for_each_run_piece(rdst_ref, rlen_ref, k, tm, copy_piece)

    @pl.when(k == nk - 1)
    def _():
        wait_slot(slot)

        @pl.when(nk >= 2)
        def _():
            wait_slot(1 - slot)


def _dispatch(run_dst, run_len, h2, rt):
    t, d = h2.shape
    assert d == PACK_ROWS * LANES
    tm = MOVE_TM
    return pl.pallas_call(
        _dispatch_kernel,
        grid_spec=pltpu.PrefetchScalarGridSpec(
            num_scalar_prefetch=2,
            grid=(t // tm,),
            in_specs=[pl.BlockSpec((tm, d), lambda i, *_: (i, 0)),
                      pl.BlockSpec((8, tm), lambda i, *_: (0, i))],
            out_specs=pl.BlockSpec(memory_space=pl.ANY),
            scratch_shapes=[pltpu.VMEM((2, 2 * tm * PACK_ROWS, LANES), ROW_DTYPE),
                            pltpu.SemaphoreType.DMA((2,))]),
        out_shape=jax.ShapeDtypeStruct((2 * t * PACK_ROWS, LANES), ROW_DTYPE),
        compiler_params=_params("arbitrary"),
    )(run_dst, run_len, h2, rt)


def _expert_kernel(vt_ref, ve_ref, va_ref, vb_ref, vn_ref, nv_ref, xs_ref, wg_hbm, wu_hbm, wd_hbm,
                   ys_ref, wg_bf, wu_bf, wd_bf, wg_f32, wu_f32, wd_f32, wsem):
    v = pl.program_id(0)
    valid = v < nv_ref[0]
    prev = jnp.maximum(v - 1, 0)
    new_expert = (v == 0) | (ve_ref[v] != ve_ref[prev])
    new_tile = (v == 0) | (vt_ref[v] != vt_ref[prev])

    def weight_copies(e):
        return [pltpu.make_async_copy(src.at[e], dst, wsem)
                for src, dst in ((wg_hbm, wg_f32), (wu_hbm, wu_f32), (wd_hbm, wd_f32))]

    @pl.when(v == 0)
    def _():
        for cp in weight_copies(ve_ref[0]):
            cp.start()

    @pl.when(valid & new_expert)
    def _():
        for cp in weight_copies(ve_ref[v]):
            cp.wait()
        wg_bf[...] = wg_f32[...].astype(BF16)
        wu_bf[...] = wu_f32[...].astype(BF16)
        wd_bf[...] = wd_f32[...].astype(BF16)

        @pl.when(vn_ref[v] >= 0)
        def _():
            for cp in weight_copies(vn_ref[v]):
                cp.start()

    def expert_rows():
        xb = _unpack_rows(xs_ref)
        gate = _dot(xb, wg_bf[...])
        up = _dot(xb, wu_bf[...])
        act = gate * jax.nn.sigmoid(gate) * up
        return _dot(act.astype(BF16), wd_bf[...])

    @pl.when(valid & new_tile)
    def _():
        _pack_rows(ys_ref, expert_rows())

    @pl.when(valid & jnp.logical_not(new_tile))
    def _():
        y = expert_rows()
        te = y.shape[0]
        row = lax.broadcasted_iota(jnp.int32, (te, 1), 0)
        mine = (row >= va_ref[v]) & (row < vb_ref[v])
        for c in range(PACK_ROWS):
            rows = pl.ds(c, te, stride=PACK_ROWS)
            ys_ref[rows, :] = jnp.where(mine, y[:, c * LANES:(c + 1) * LANES], ys_ref[rows, :])


def _expert_mlp(visit_tile, visit_expert, visit_lo, visit_hi, visit_next, n_visits, xs,
                w_gate, w_up, w_down):
    te = EXPERT_TE
    d, ff = w_gate.shape[1:]
    blk = (te * PACK_ROWS, LANES)
    tile = lambda v, vt, *_: (vt[v], 0)
    hbm = pl.BlockSpec(memory_space=pl.ANY)
    return pl.pallas_call(
        _expert_kernel,
        grid_spec=pltpu.PrefetchScalarGridSpec(
            num_scalar_prefetch=6,
            grid=(visit_tile.shape[0],),
            in_specs=[pl.BlockSpec(blk, tile), hbm, hbm, hbm],
            out_specs=pl.BlockSpec(blk, tile),
            scratch_shapes=[pltpu.VMEM((d, ff), BF16), pltpu.VMEM((d, ff), BF16),
                            pltpu.VMEM((ff, d), BF16),
                            pltpu.VMEM((d, ff), F32), pltpu.VMEM((d, ff), F32),
                            pltpu.VMEM((ff, d), F32), pltpu.SemaphoreType.DMA(())]),
        out_shape=jax.ShapeDtypeStruct(xs.shape, ROW_DTYPE),
        compiler_params=_params("arbitrary"),
    )(visit_tile, visit_expert, visit_lo, visit_hi, visit_next, n_visits, xs,
      w_gate, w_up, w_down)


def _combine_kernel(rdst_ref, rlen_ref, x1_ref, rinfo_ref, g2_ref, gpost_ref, ys_ref, o_ref,
                    gath_ref, sem):
    k = pl.program_id(0)
    nk = pl.num_programs(0)
    tm = x1_ref.shape[0]
    rows = 2 * tm
    slot = k % 2

    def gather_runs(tile, s):
        buf = gath_ref.at[s]

        def copy_piece(local, src, size):
            pltpu.make_async_copy(_token_rows(ys_ref, src, size), _token_rows(buf, local, size),
                                  sem.at[s]).start()

        _for_each_run_piece(rdst_ref, rlen_ref, tile, tm, copy_piece)

    @pl.when(k == 0)
    def _():
        gather_runs(0, 0)

    @pl.when(k + 1 < nk)
    def _():
        gather_runs(k + 1, 1 - slot)

    buf = gath_ref.at[slot]
    pltpu.make_async_copy(_token_rows(ys_ref, 0, rows), buf, sem.at[slot]).wait()
    g = _unpack_rows(buf)
    info = rinfo_ref[...]
    col = lax.broadcasted_iota(jnp.int32, (tm, rows), 1).astype(F32)
    pick = jnp.where(col == info[:, 2:3], info[:, 4:5],
                     jnp.where(col == info[:, 3:4], info[:, 5:6], 0.0)).astype(BF16)
    fx = _dot(pick, g)
    o_ref[...] = x1_ref[...] + g2_ref[0, 0] * (_rms(fx) * gpost_ref[...])


def _combine(run_dst, run_len, x1, rinfo, mod, gpost, ys, seq):
    t, d = x1.shape
    tm = MOVE_TM
    tpb = seq // tm
    batch_of = lambda i: i // tpb
    return pl.pallas_call(
        _combine_kernel,
        grid_spec=pltpu.PrefetchScalarGridSpec(
            num_scalar_prefetch=2,
            grid=(t // tm,),
            in_specs=[pl.BlockSpec((tm, d), lambda i, *_: (i, 0)),
                      pl.BlockSpec((tm, LANES), lambda i, *_: (i, 0)),
                      _mod_spec(MOD_G2, batch_of)(d),
                      pl.BlockSpec((1, d), lambda i, *_: (0, 0)),
                      pl.BlockSpec(memory_space=pl.ANY)],
            out_specs=pl.BlockSpec((tm, d), lambda i, *_: (i, 0)),
            scratch_shapes=[pltpu.VMEM((2, 2 * tm * PACK_ROWS, LANES), ROW_DTYPE),
                            pltpu.SemaphoreType.DMA((2,))]),
        out_shape=jax.ShapeDtypeStruct((t, d), F32),
        compiler_params=_params("arbitrary"),
    )(run_dst, run_len, x1, rinfo, mod, gpost, ys)


def _rope_tables(seq):
    pos = np.arange(seq)
    row = (pos // GRID_W).astype(np.float32)
    col = (pos % GRID_W).astype(np.float32)
    axis_dim = HEAD_DIM // 2
    inv_freq = (ROPE_THETA ** (-np.arange(0, axis_dim, 2, dtype=np.float32) / axis_dim)).astype(
        np.float32)
    ang = np.concatenate([row[:, None] * inv_freq, col[:, None] * inv_freq], axis=-1)
    pair = (np.arange(LANES) % HEAD_DIM) // 2
    cos = np.cos(ang)[:, pair]
    sin = np.sin(ang)[:, pair]
    even = (np.arange(LANES) % 2) == 0
    tables = (cos, np.where(even, -sin, 0.0), np.where(even, 0.0, sin))
    tables = tables + tuple(tb.T for tb in tables)
    return tuple(jnp.asarray(tb, F32) for tb in tables)


def _segment_ones(n):
    seg = np.arange(n) // HEAD_DIM
    return jnp.asarray(seg[:, None] == seg[None, :], BF16)


def kernel(x, c, ctx, c_ctx, w_mod, b_mod, attn_pre_norm, attn_post_norm, w_in, a_sink,
           b_q_norm, b_k_norm, a_out_norm, b_out_norm, w_out, ffn_pre_norm, ffn_post_norm,
           w_group, b_group, w_router, b_router, w_gate, w_up, w_down):
    batch, seq, d = x.shape
    ctx_len = ctx.shape[1]
    assert w_mod.shape[0] == 1, "single-layer stack only (context stream is never updated)"
    assert seq % ATTN_A_TQ == 0 and seq >= ATTN_A_TQ + 2 * WINDOW
    assert seq % PROJ_TM == 0 and seq % ATTN_B_TQ == 0 and seq % OUT_TM == 0 and seq % MOVE_TM == 0
    t = batch * seq
    nq = d // 2
    nkv = nq // KV_GROUP
    assert nkv == LANES and w_in.shape[2] == 2 * nq + 4 * nkv

    assert batch + 1 <= MOD_ROWS
    cc = jnp.concatenate([c, c_ctx[None, :], jnp.zeros((MOD_ROWS - batch - 1, d), F32)], axis=0)
    mod = _modulation(cc, w_mod[0], b_mod[0]).reshape(cc.shape[0], 6, 1, d)

    x2 = x.reshape(t, d)
    c2 = ctx.reshape(batch * ctx_len, d)
    gpre = attn_pre_norm[0].reshape(1, d)
    w_in_bf = w_in[0].astype(BF16)
    qn = jnp.broadcast_to(jnp.tile(b_q_norm[0], nq // HEAD_DIM)[:, None], (nq, PROJ_TM))
    kn = jnp.tile(b_k_norm[0], nkv // HEAD_DIM).reshape(1, nkv)
    seg_k = _segment_ones(nkv)
    qat, ka, vat, qbt, kb, vbt, stats = _project_latents(
        x2, mod, gpre, w_in_bf, _rope_tables(seq), qn, kn, seg_k, seq)
    kca, vcat, kcb, vcbt, ctx_stats = _project_context(
        c2, mod, batch, gpre, w_in_bf, kn, seg_k, ctx_len)

    q_sq = jnp.max(stats[:, 0, 0])
    k_sq = jnp.maximum(jnp.max(stats[:, 0, 1]), jnp.max(ctx_stats[:, 0, 1]))
    bound_a = jnp.maximum(1.01 * jnp.sqrt(q_sq * k_sq), jnp.max(a_sink[0]) * LOG2_E).reshape(1)
    oa = _attention_a(bound_a, a_sink[0], qat, ka, vat, kca, vcat, batch, seq, ctx_len)
    score_bound = (1.01 * HEAD_DIM ** 0.5 * LOG2_E
                   * jnp.max(jnp.abs(b_q_norm[0])) * jnp.max(jnp.abs(b_k_norm[0]))).reshape(1)
    ob = _attention_b(score_bound, qbt, kb, vbt, kcb, vcbt, batch, seq, ctx_len)

    w_out_bf = w_out[0].astype(BF16)
    lane_pad = LANES - N_EXPERTS - N_GROUPS
    w_r = jnp.pad(jnp.concatenate([w_router[0], w_group[0]], axis=1), ((0, 0), (0, lane_pad)))
    w_r_hi = w_r.astype(BF16)
    w_r_lo = (w_r - w_r_hi.astype(F32)).astype(BF16)
    w_r2 = jnp.concatenate([w_r_hi, w_r_lo], axis=1)
    b_r = jnp.pad(jnp.concatenate([b_router[0], b_group[0]]), (0, lane_pad)).reshape(1, LANES)
    x1, h2, rinfo, rt, tcarry, tcnt, counts = _out_and_route(
        oa, ob, x2, mod, a_out_norm[0].reshape(1, nq), b_out_norm[0].reshape(1, nq),
        attn_post_norm[0].reshape(1, d), ffn_pre_norm[0].reshape(1, d),
        w_out_bf, w_r2, b_r, seq)

    te = EXPERT_TE
    assert (2 * t) % te == 0
    cnt = counts[0, :N_EXPERTS].astype(jnp.int32)
    ends = jnp.cumsum(cnt)
    starts = ends - cnt
    run_dst = (starts[None, :] + tcarry[:, 0, :N_EXPERTS].astype(jnp.int32)).reshape(-1)
    run_len = tcnt[:, 0, :N_EXPERTS].astype(jnp.int32).reshape(-1)
    first_tile = starts // te
    n_vis = jnp.where(cnt > 0, (ends - 1) // te - first_tile + 1, 0)
    vis_end = jnp.cumsum(n_vis)
    n_visits = vis_end[-1]
    v = jnp.minimum(jnp.arange(2 * t // te + N_EXPERTS, dtype=jnp.int32), n_visits - 1)
    v_expert = jnp.sum(vis_end[None, :] <= v[:, None], axis=1).astype(jnp.int32)
    pick = (v_expert[:, None] == jnp.arange(N_EXPERTS)[None, :]).astype(jnp.int32)
    of_expert = lambda table: jnp.sum(pick * table[None, :], axis=1)
    v_tile = of_expert(first_tile) + v - of_expert(vis_end - n_vis)
    v_lo = jnp.maximum(of_expert(starts) - v_tile * te, 0)
    v_hi = jnp.minimum(of_expert(ends) - v_tile * te, te)
    ids = jnp.arange(N_EXPERTS, dtype=jnp.int32)
    later = (ids[None, :] > ids[:, None]) & (cnt[None, :] > 0)
    next_expert = jnp.min(jnp.where(later, ids[None, :], N_EXPERTS), axis=1)
    v_next = of_expert(jnp.where(next_expert < N_EXPERTS, next_expert, -1))

    xs = _dispatch(run_dst, run_len, h2, rt)
    ys = _expert_mlp(v_tile, v_expert, v_lo, v_hi, v_next, n_visits.reshape(1), xs,
                     w_gate[0], w_up[0], w_down[0])
    out = _combine(run_dst, run_len, x1, rinfo, mod, ffn_post_norm[0].reshape(1, d), ys, seq)
    return out.reshape(batch, seq, d)
```

```python
import functools

import jax
import jax.numpy as jnp
import numpy as np
from jax import lax
from jax.experimental import pallas as pl
from jax.experimental.pallas import tpu as pltpu

F32 = jnp.float32
BF16 = jnp.bfloat16

GRID_W = 64
HEAD_DIM = 64
KV_GROUP = 4
WINDOW = 128
ROPE_THETA = 10000.0
N_GROUPS = 4
EXPERTS_PER_GROUP = 8
N_EXPERTS = N_GROUPS * EXPERTS_PER_GROUP
EPS = 1e-6
NEG_INF = -1e30
LOG2_E = 1.4426950408889634
SAFE_SCORE_BOUND = 40.0

LANES = 128
V7X_VMEM_LIMIT = 56 * 1024 * 1024

MOD_ROWS = 16
MOD_BN = 1024
PROJ_TM = 512
ATTN_A_TQ = 256
ATTN_B_TQ = 256
OUT_TM = 512
EXPERT_TE = 512
MOVE_TM = OUT_TM


def _params(*sem):
    return pltpu.CompilerParams(dimension_semantics=sem, vmem_limit_bytes=V7X_VMEM_LIMIT)


def _dot(a, b):
    return jnp.dot(a, b, preferred_element_type=F32)


def _rms(x):
    return x * lax.rsqrt(jnp.mean(x * x, axis=-1, keepdims=True) + EPS)


def _split_bf16(x):
    hi = x.astype(BF16)
    lo = (x - hi.astype(F32)).astype(BF16)
    return hi, lo


def _mod_kernel(c_ref, w_ref, b_ref, o_ref):
    cc = c_ref[...]
    s = cc * jax.nn.sigmoid(cc)
    s_hi, s_lo = _split_bf16(s)
    w_hi, w_lo = _split_bf16(w_ref[...])
    o_ref[...] = _dot(s_hi, w_hi) + _dot(s_lo, w_hi) + _dot(s_hi, w_lo) + b_ref[...]


def _modulation(cc, w_mod, b_mod):
    rows, d = cc.shape
    n = w_mod.shape[1]
    bn = MOD_BN
    return pl.pallas_call(
        _mod_kernel,
        grid=(n // bn,),
        in_specs=[pl.BlockSpec((rows, d), lambda i: (0, 0)),
                  pl.BlockSpec((d, bn), lambda i: (0, i)),
                  pl.BlockSpec((1, bn), lambda i: (0, i))],
        out_specs=pl.BlockSpec((rows, bn), lambda i: (0, i)),
        out_shape=jax.ShapeDtypeStruct((rows, n), F32),
        compiler_params=_params("arbitrary"),
    )(cc, w_mod, b_mod.reshape(1, n))


def _rope(x, cos, sin_a, sin_b):
    return x * cos + pltpu.roll(x, LANES - 1, 1) * sin_a + pltpu.roll(x, 1, 1) * sin_b


def _head_norm(x, seg_ref, gain):
    ss = _dot((x * x).astype(BF16), seg_ref[...])
    return x * lax.rsqrt(ss * (1.0 / HEAD_DIM) + EPS) * gain


def _max_head_sq_norm(x, seg_ref):
    ss = _dot((x * x).astype(BF16), seg_ref[...])
    return jnp.max(jnp.max(ss, axis=1, keepdims=True), axis=0, keepdims=True)


def _norm_stats(q_sq, k_sq):
    lane = lax.broadcasted_iota(jnp.int32, (1, LANES), 1)
    zero = jnp.zeros((1, LANES), F32)
    return jnp.where(lane == 0, q_sq, zero) + jnp.where(lane == 1, k_sq, zero)


def _rope_t(xt, cos_t, sin_a_t, sin_b_t):
    return (xt * cos_t + pltpu.roll(xt, LANES - 1, 0) * sin_a_t
            + pltpu.roll(xt, 1, 0) * sin_b_t)


def _proj_kernel(x_ref, sc_ref, sh_ref, gpre_ref, w_ref, cos_ref, sa_ref, sb_ref,
                 cos_t_ref, sa_t_ref, sb_t_ref, qn_ref, kn_ref, seg_k_ref,
                 qat_ref, ka_ref, vat_ref, qbt_ref, kb_ref, vbt_ref, stats_ref):
    h = _rms(x_ref[...]) * gpre_ref[...] * (1.0 + sc_ref[0, 0]) + sh_ref[0, 0]
    p = _dot(h.astype(BF16), w_ref[...])
    cos, sa, sb = cos_ref[...], sa_ref[...], sb_ref[...]
    q_scale = HEAD_DIM ** -0.5 * LOG2_E
    cos_t, sa_t, sb_t = cos_t_ref[...] * q_scale, sa_t_ref[...] * q_scale, sb_t_ref[...] * q_scale
    nq = qat_ref.shape[0]
    q_sq = None
    for c in range(nq // LANES):
        xt = p[:, c * LANES:(c + 1) * LANES].T
        qat_ref[c * LANES:(c + 1) * LANES, :] = _rope_t(xt, cos_t, sa_t, sb_t).astype(BF16)
        sq = xt * xt
        for head in range(LANES // HEAD_DIM):
            norm = jnp.sum(sq[head * HEAD_DIM:(head + 1) * HEAD_DIM, :], axis=0, keepdims=True)
            q_sq = norm if q_sq is None else jnp.maximum(q_sq, norm)
    q_sq = jnp.max(q_sq, axis=1, keepdims=True) * (q_scale * q_scale)
    o = nq
    ka_ref[...] = _rope(p[:, o:o + LANES], cos, sa, sb).astype(BF16)
    vat_ref[...] = p[:, o + LANES:o + 2 * LANES].T.astype(BF16)
    stats_ref[0] = _norm_stats(q_sq, _max_head_sq_norm(p[:, o:o + LANES], seg_k_ref))
    o += 2 * LANES
    for c in range(nq // LANES):
        xt = p[:, o + c * LANES:o + (c + 1) * LANES].T
        sq = xt * xt
        halves = []
        for head in range(LANES // HEAD_DIM):
            rows = slice(head * HEAD_DIM, (head + 1) * HEAD_DIM)
            ms = jnp.sum(sq[rows, :], axis=0, keepdims=True) * (1.0 / HEAD_DIM)
            halves.append(xt[rows, :] * lax.rsqrt(ms + EPS))
        qn = jnp.concatenate(halves, axis=0) * qn_ref[c * LANES:(c + 1) * LANES, :]
        qbt_ref[c * LANES:(c + 1) * LANES, :] = _rope_t(qn, cos_t, sa_t, sb_t).astype(BF16)
    o += nq
    kb = _head_norm(p[:, o:o + LANES], seg_k_ref, kn_ref[...])
    kb_ref[...] = _rope(kb, cos, sa, sb).astype(BF16)
    vbt_ref[...] = p[:, o + LANES:o + 2 * LANES].T.astype(BF16)


def _ctx_proj_kernel(x_ref, sc_ref, sh_ref, gpre_ref, wa_ref, wb_ref, kn_ref, seg_k_ref,
                     ka_ref, vat_ref, kb_ref, vbt_ref, stats_ref):
    h = (_rms(x_ref[...]) * gpre_ref[...] * (1.0 + sc_ref[0, 0]) + sh_ref[0, 0]).astype(BF16)
    pa = _dot(h, wa_ref[...])
    pb = _dot(h, wb_ref[...])
    stats_ref[0] = _norm_stats(0.0, _max_head_sq_norm(pa[:, 0:LANES], seg_k_ref))
    ka_ref[...] = pa[:, 0:LANES].astype(BF16)
    vat_ref[...] = pa[:, LANES:2 * LANES].T.astype(BF16)
    kb_ref[...] = _head_norm(pb[:, 0:LANES], seg_k_ref, kn_ref[...]).astype(BF16)
    vbt_ref[...] = pb[:, LANES:2 * LANES].T.astype(BF16)


def _mod_spec(chunk, row_of_step):
    return lambda d: pl.BlockSpec((1, 1, 1, d), lambda i, *_: (row_of_step(i), chunk, 0, 0))


MOD_SH1, MOD_SC1, MOD_G1, MOD_SH2, MOD_SC2, MOD_G2 = range(6)


def _project_latents(x2, mod, gpre, w_in, tables, qn, kn, seg_k, seq):
    t, d = x2.shape
    tm = PROJ_TM
    tpb = seq // tm
    nq = qn.shape[0]
    const = lambda shape: pl.BlockSpec(shape, lambda i: (0,) * len(shape))
    batch_of = lambda i: i // tpb
    table = pl.BlockSpec((tm, LANES), lambda i: (i % tpb, 0))
    table_t = pl.BlockSpec((LANES, tm), lambda i: (0, i % tpb))
    k_spec = pl.BlockSpec((tm, LANES), lambda i: (i, 0))
    k_shape = jax.ShapeDtypeStruct((t, LANES), BF16)
    vt_spec = pl.BlockSpec((LANES, tm), lambda i: (0, i))
    vt_shape = jax.ShapeDtypeStruct((LANES, t), BF16)
    qt_spec = pl.BlockSpec((nq, tm), lambda i: (0, i))
    qt_shape = jax.ShapeDtypeStruct((nq, t), BF16)
    return pl.pallas_call(
        _proj_kernel,
        grid=(t // tm,),
        in_specs=[pl.BlockSpec((tm, d), lambda i: (i, 0)),
                  _mod_spec(MOD_SC1, batch_of)(d), _mod_spec(MOD_SH1, batch_of)(d), const((1, d)),
                  const(w_in.shape), table, table, table, table_t, table_t, table_t,
                  const((nq, tm)), const((1, LANES)), const(seg_k.shape)],
        out_specs=[qt_spec, k_spec, vt_spec, qt_spec, k_spec, vt_spec,
                   pl.BlockSpec((1, 1, LANES), lambda i: (i, 0, 0))],
        out_shape=[qt_shape, k_shape, vt_shape, qt_shape, k_shape, vt_shape,
                   jax.ShapeDtypeStruct((t // tm, 1, LANES), F32)],
        compiler_params=_params("arbitrary"),
    )(x2, mod, mod, gpre, w_in, *tables, qn, kn, seg_k)


def _project_context(c2, mod, ctx_row, gpre, w_in, kn, seg_k, ctx_len):
    t, d = c2.shape
    ctx_mod = lambda i: ctx_row
    nq = (w_in.shape[1] - 4 * LANES) // 2
    kv = 2 * LANES
    assert nq % kv == 0
    group_kv = lambda g: pl.BlockSpec((d, kv), lambda i: (0, (g * (nq + kv) + nq) // kv))
    const = lambda shape: pl.BlockSpec(shape, lambda i: (0,) * len(shape))
    k_spec = pl.BlockSpec((ctx_len, LANES), lambda i: (i, 0))
    k_shape = jax.ShapeDtypeStruct((t, LANES), BF16)
    vt_spec = pl.BlockSpec((LANES, ctx_len), lambda i: (0, i))
    vt_shape = jax.ShapeDtypeStruct((LANES, t), BF16)
    return pl.pallas_call(
        _ctx_proj_kernel,
        grid=(t // ctx_len,),
        in_specs=[pl.BlockSpec((ctx_len, d), lambda i: (i, 0)),
                  _mod_spec(MOD_SC1, ctx_mod)(d), _mod_spec(MOD_SH1, ctx_mod)(d),
                  const((1, d)), group_kv(0), group_kv(1), const((1, LANES)), const(seg_k.shape)],
        out_specs=[k_spec, vt_spec, k_spec, vt_spec,
                   pl.BlockSpec((1, 1, LANES), lambda i: (i, 0, 0))],
        out_shape=[k_shape, vt_shape, k_shape, vt_shape,
                   jax.ShapeDtypeStruct((t // ctx_len, 1, LANES), F32)],
        compiler_params=_params("arbitrary"),
    )(c2, mod, mod, gpre, w_in, w_in, kn, seg_k)


def _attend_t(w, k, kc, vt, vct, bounded=False, bias=None, sink=None):
    st = _dot(k, w)
    sct = _dot(kc, w)
    if bias is not None:
        tq = bias.shape[1]
        st = jnp.concatenate([st[:, c * tq:(c + 1) * tq] + bias
                              for c in range(st.shape[1] // tq)], axis=1)
    if not bounded:
        shift = jnp.maximum(jnp.max(st, axis=0, keepdims=True),
                            jnp.max(sct, axis=0, keepdims=True))
        if sink is not None:
            shift = jnp.maximum(shift, sink)
            sink = sink - shift
        st, sct = st - shift, sct - shift
    pt = jnp.exp2(st)
    pct = jnp.exp2(sct)
    denom = jnp.sum(pt, axis=0, keepdims=True) + jnp.sum(pct, axis=0, keepdims=True)
    if sink is not None:
        denom = denom + jnp.exp2(sink)
    o2 = _dot(vt, pt.astype(BF16)) + _dot(vct, pct.astype(BF16))
    return o2, denom


def _all_heads_t(qt_ref, o_ref, attend):
    tq = qt_ref.shape[1]
    n_kv = LANES // HEAD_DIM
    zeros = jnp.zeros((HEAD_DIM, tq), BF16)
    outs = []
    for h in range(qt_ref.shape[0] // HEAD_DIM):
        g = h // KV_GROUP
        qh = qt_ref[h * HEAD_DIM:(h + 1) * HEAD_DIM, :]
        w = jnp.concatenate([zeros] * g + [qh] + [zeros] * (n_kv - 1 - g), axis=0)
        o2, denom = attend(h, w)
        outs.append(o2[g * HEAD_DIM:(g + 1) * HEAD_DIM, :] / denom)
    o_ref[...] = jnp.concatenate(outs, axis=0).T.astype(BF16)


def _all_heads_fused_t(qt_ref, o_ref, attend):
    tq = qt_ref.shape[1]
    n_heads = qt_ref.shape[0] // HEAD_DIM
    n_kv = LANES // HEAD_DIM
    rows = []
    for g in range(n_kv):
        heads = [qt_ref[h * HEAD_DIM:(h + 1) * HEAD_DIM, :] if h // KV_GROUP == g
                 else jnp.zeros((HEAD_DIM, tq), BF16) for h in range(n_heads)]
        rows.append(jnp.concatenate(heads, axis=1))
    w = jnp.concatenate(rows, axis=0)
    o2, denom = attend(0, w)
    o2 = o2 / denom
    outs = [o2[(h // KV_GROUP) * HEAD_DIM:(h // KV_GROUP + 1) * HEAD_DIM, h * tq:(h + 1) * tq]
            for h in range(n_heads)]
    o_ref[...] = jnp.concatenate(outs, axis=0).T.astype(BF16)


def _attn_b_kernel(bound_ref, qt_ref, k_ref, vt_ref, kc_ref, vct_ref, o_ref):
    k, kc, vt, vct = k_ref[...], kc_ref[...], vt_ref[...], vct_ref[...]
    bound = bound_ref[0]

    @pl.when(bound <= SAFE_SCORE_BOUND)
    def _():
        _all_heads_fused_t(qt_ref, o_ref, lambda h, w: _attend_t(w, k, kc, vt, vct, bounded=True))

    @pl.when(jnp.logical_not(bound <= SAFE_SCORE_BOUND))
    def _():
        _all_heads_t(qt_ref, o_ref, lambda h, w: _attend_t(w, k, kc, vt, vct))


def _attn_specs(nq, tq, seq, ctx_len):
    nqb = seq // tq
    return dict(
        qt=pl.BlockSpec((nq, tq), lambda b, i: (0, b * nqb + i)),
        k=pl.BlockSpec((seq, LANES), lambda b, i: (b, 0)),
        vt=pl.BlockSpec((LANES, seq), lambda b, i: (0, b)),
        kc=pl.BlockSpec((ctx_len, LANES), lambda b, i: (b, 0)),
        vct=pl.BlockSpec((LANES, ctx_len), lambda b, i: (0, b)),
        out=pl.BlockSpec((tq, nq), lambda b, i: (b * nqb + i, 0)))


def _attention_b(score_bound, qbt, kb, vbt, kcb, vcbt, batch, seq, ctx_len):
    nq, t = qbt.shape
    tq = ATTN_B_TQ
    sp = _attn_specs(nq, tq, seq, ctx_len)
    return pl.pallas_call(
        _attn_b_kernel,
        grid=(batch, seq // tq),
        in_specs=[pl.BlockSpec(memory_space=pltpu.SMEM),
                  sp["qt"], sp["k"], sp["vt"], sp["kc"], sp["vct"]],
        out_specs=sp["out"],
        out_shape=jax.ShapeDtypeStruct((t, nq), BF16),
        compiler_params=_params("arbitrary", "arbitrary"),
    )(score_bound, qbt, kb, vbt, kcb, vcbt)


def _attn_a_kernel(bound_ref, sink_ref, qt_ref, k_ref, vt_ref, kc_ref, vct_ref, bias_ref, o_ref,
                   *, seq):
    i = pl.program_id(1)
    tq = qt_ref.shape[1]
    n_heads = qt_ref.shape[0] // HEAD_DIM
    win = tq + 2 * WINDOW
    start = pl.multiple_of(jnp.clip(i * tq - WINDOW, 0, seq - win), WINDOW)
    k = k_ref[pl.ds(start, win), :]
    vt = vt_ref[:, pl.ds(start, win)]
    kc, vct = kc_ref[...], vct_ref[...]
    sinks = [sink_ref[h] * LOG2_E for h in range(n_heads)]

    bound = bound_ref[0]
    small = bound <= SAFE_SCORE_BOUND

    @pl.when(small)
    def _():
        sink_row = jnp.concatenate([jnp.full((1, tq), s, F32) for s in sinks], axis=1)
        _all_heads_fused_t(qt_ref, o_ref, lambda h, w: _attend_t(
            w, k, kc, vt, vct, bounded=True, bias=bias_ref[0], sink=sink_row))

    @pl.when(jnp.logical_not(small))
    def _():
        _all_heads_t(qt_ref, o_ref, lambda h, w: _attend_t(
            w, k, kc, vt, vct, bias=bias_ref[0], sink=sinks[h]))


def _band_bias(tq):
    win = tq + 2 * WINDOW
    r = np.arange(win)[:, None]
    j = np.arange(tq)[None, :]
    tables = [np.where(np.abs(off + r - j) <= WINDOW, 0.0, NEG_INF)
              for off in (0, -WINDOW, -2 * WINDOW)]
    return jnp.asarray(np.stack(tables), F32)


def _attention_a(score_bound, sink, qat, ka, vat, kca, vcat, batch, seq, ctx_len):
    nq, t = qat.shape
    tq = ATTN_A_TQ
    nqb = seq // tq
    win = tq + 2 * WINDOW
    assert nqb >= 2 and tq >= WINDOW
    sp = _attn_specs(nq, tq, seq, ctx_len)
    which = lambda b, i: (jnp.where(i == 0, 0, jnp.where(i == nqb - 1, 2, 1)), 0, 0)
    return pl.pallas_call(
        functools.partial(_attn_a_kernel, seq=seq),
        grid=(batch, nqb),
        in_specs=[pl.BlockSpec(memory_space=pltpu.SMEM), pl.BlockSpec(memory_space=pltpu.SMEM),
                  sp["qt"], sp["k"], sp["vt"], sp["kc"], sp["vct"],
                  pl.BlockSpec((1, win, tq), which)],
        out_specs=sp["out"],
        out_shape=jax.ShapeDtypeStruct((t, nq), BF16),
        compiler_params=_params("arbitrary", "arbitrary"),
    )(score_bound, sink, qat, ka, vat, kca, vcat, _band_bias(tq))


def _out_kernel(oa_ref, ob_ref, x_ref, g1_ref, sc2_ref, sh2_ref, ga_ref, gb_ref, gpost_ref,
                gpre2_ref, woa_ref, wob_ref, wr_ref, br_ref,
                x1_ref, h2_ref, rinfo_ref, rt_ref, tcarry_ref, tcnt_ref, cnt_ref, carry_ref):
    step = pl.program_id(0)

    @pl.when(step == 0)
    def _():
        carry_ref[...] = jnp.zeros_like(carry_ref)

    na = _rms(oa_ref[...].astype(F32)) * ga_ref[...]
    nb = _rms(ob_ref[...].astype(F32)) * gb_ref[...]
    ox = _dot(na.astype(BF16), woa_ref[...]) + _dot(nb.astype(BF16), wob_ref[...])
    x1 = x_ref[...] + g1_ref[0, 0] * (_rms(ox) * gpost_ref[...])
    x1_ref[...] = x1
    h2 = _rms(x1) * gpre2_ref[...] * (1.0 + sc2_ref[0, 0]) + sh2_ref[0, 0]
    h_hi, h_lo = _split_bf16(h2)
    h2_ref[...] = h_hi

    both = _dot(h_hi, wr_ref[...])
    logits = (both[:, :LANES] + _dot(h_lo, wr_ref[:, :LANES]) + both[:, LANES:]
              + br_ref[...])
    tm = logits.shape[0]
    lt = logits.T
    row = lax.broadcasted_iota(jnp.int32, lt.shape, 0)
    rowf = row.astype(F32)
    big = jnp.float32(1e9)
    ninf = jnp.float32(-jnp.inf)
    colmax = lambda v: jnp.max(v, axis=0, keepdims=True)
    colmin = lambda v: jnp.min(v, axis=0, keepdims=True)
    colsum = lambda v: jnp.sum(v, axis=0, keepdims=True)

    gmask = (row >= N_EXPERTS) & (row < N_EXPERTS + N_GROUPS)
    lg = jnp.where(gmask, lt, ninf)
    gmax = colmax(lg)
    gidx = colmin(jnp.where(lg == gmax, rowf, big)) - N_EXPERTS
    g_w = 1.0 / colsum(jnp.exp(lg - gmax))
    row_group = (row // EXPERTS_PER_GROUP).astype(F32)
    emask = (row < N_EXPERTS) & (row_group == gidx)
    le = jnp.where(emask, lt, ninf)
    m1 = colmax(le)
    i1 = colmin(jnp.where(le == m1, rowf, big))
    le2 = jnp.where(rowf == i1, ninf, le)
    m2 = colmax(le2)
    i2 = colmin(jnp.where(le2 == m2, rowf, big))
    e2 = jnp.exp(m2 - m1)
    w0 = g_w / (1.0 + e2)
    w1 = g_w * e2 / (1.0 + e2)

    hit1 = rowf == i1
    hit2 = rowf == i2
    onehot = jnp.where(hit1, 1.0, jnp.where(hit2, 1.0, 0.0)).astype(F32)
    r = lax.broadcasted_iota(jnp.int32, (tm, tm), 0)
    c = lax.broadcasted_iota(jnp.int32, (tm, tm), 1)
    earlier = jnp.where(r < c, 1.0, 0.0).astype(BF16)
    within = _dot(onehot.astype(BF16), earlier)
    tile_cnt = jnp.broadcast_to(jnp.sum(onehot, axis=1, keepdims=True), (LANES, LANES))
    er = lax.broadcasted_iota(jnp.int32, (LANES, LANES), 0)
    ec = lax.broadcasted_iota(jnp.int32, (LANES, LANES), 1)
    below = jnp.where(er > ec, 1.0, 0.0).astype(BF16)
    cnt_hi = jnp.floor(tile_cnt * (1.0 / 32.0))
    cnt_lo = tile_cnt - 32.0 * cnt_hi
    run_start = 32.0 * _dot(below, cnt_hi.astype(BF16)) + _dot(below, cnt_lo.astype(BF16))
    local = within + run_start[:, 0:1]
    pos0 = colsum(jnp.where(hit1, local, 0.0))
    pos1 = colsum(jnp.where(hit2, local, 0.0))
    cnt_row = tile_cnt.T[0:1, :]
    tcarry_ref[0] = carry_ref[...]
    tcnt_ref[0] = cnt_row
    carry_ref[...] += cnt_row
    cnt_ref[...] = carry_ref[...]

    fields = jnp.concatenate([i1, i2, pos0, pos1, w0, w1, jnp.zeros((2, tm), F32)], axis=0)
    rt_ref[...] = fields
    rinfo_ref[...] = jnp.concatenate(
        [fields, jnp.zeros((LANES - 8, tm), F32)], axis=0).T


def _out_and_route(oa, ob, x2, mod, ga, gb, gpost, gpre2, w_out, wr, br, seq):
    t, d = x2.shape
    tm = OUT_TM
    tpb = seq // tm
    nq = oa.shape[1]
    const = lambda shape: pl.BlockSpec(shape, lambda i: (0,) * len(shape))
    batch_of = lambda i: i // tpb
    rows = lambda n: pl.BlockSpec((tm, n), lambda i: (i, 0))
    per_tile = pl.BlockSpec((1, 1, LANES), lambda i: (i, 0, 0))
    w_half = lambda g: pl.BlockSpec((nq, d), lambda i: (g, 0))
    return pl.pallas_call(
        _out_kernel,
        grid=(t // tm,),
        in_specs=[rows(nq), rows(nq), rows(d),
                  _mod_spec(MOD_G1, batch_of)(d), _mod_spec(MOD_SC2, batch_of)(d),
                  _mod_spec(MOD_SH2, batch_of)(d),
                  const((1, nq)), const((1, nq)), const((1, d)), const((1, d)),
                  w_half(0), w_half(1), const(wr.shape),
                  const((1, LANES))],
        out_specs=[rows(d), rows(d), rows(LANES), pl.BlockSpec((8, tm), lambda i: (0, i)),
                   per_tile, per_tile, const((1, LANES))],
        out_shape=[jax.ShapeDtypeStruct((t, d), F32), jax.ShapeDtypeStruct((t, d), BF16),
                   jax.ShapeDtypeStruct((t, LANES), F32), jax.ShapeDtypeStruct((8, t), F32),
                   jax.ShapeDtypeStruct((t // tm, 1, LANES), F32),
                   jax.ShapeDtypeStruct((t // tm, 1, LANES), F32),
                   jax.ShapeDtypeStruct((1, LANES), F32)],
        scratch_shapes=[pltpu.VMEM((1, LANES), F32)],
        compiler_params=_params("arbitrary"),
    )(oa, ob, x2, mod, mod, mod, ga, gb, gpost, gpre2, w_out, w_out, wr, br)


PACK_ROWS = 8
ROW_DTYPE = F32


def _pack_rows(ref, x):
    n = x.shape[0]
    for c in range(PACK_ROWS):
        ref[pl.ds(c, n, stride=PACK_ROWS), :] = x[:, c * LANES:(c + 1) * LANES]


def _unpack_rows(ref):
    n = ref.shape[0] // PACK_ROWS
    return jnp.concatenate(
        [ref[pl.ds(c, n, stride=PACK_ROWS), :].astype(BF16) for c in range(PACK_ROWS)], axis=1)


def _for_each_run_piece(rdst_ref, rlen_ref, rmax_ref, tile, max_len, fn):
    def copy_runs(n_bits):
        def run(e, local):
            length = rlen_ref[tile * N_EXPERTS + e]
            dst = rdst_ref[tile * N_EXPERTS + e]
            for b in range(n_bits):
                size = 1 << b

                @pl.when(((length >> b) & 1) == 1)
                def _():
                    done = length & (size - 1)
                    fn(local + done, dst + done, size)
            return local + length

        lax.fori_loop(0, N_EXPERTS, run, 0)

    all_bits = max_len.bit_length()
    low_bits = min(all_bits, (4 * max_len // N_EXPERTS).bit_length())
    short = rmax_ref[tile] < (1 << low_bits)

    @pl.when(short)
    def _():
        copy_runs(low_bits)

    @pl.when(jnp.logical_not(short))
    def _():
        copy_runs(all_bits)


def _token_rows(ref, row0, n_rows):
    start = row0 * PACK_ROWS
    if not isinstance(start, int):
        start = pl.multiple_of(start, PACK_ROWS)
    return ref.at[pl.ds(start, n_rows * PACK_ROWS)]


def _dispatch_kernel(rdst_ref, rlen_ref, rmax_ref, h_ref, rt_ref, xs_ref, sorted_ref, sem):
    k = pl.program_id(0)
    nk = pl.num_programs(0)
    tm = h_ref.shape[0]
    rows = 2 * tm
    slot = k % 2

    def wait_slot(s):
        pltpu.make_async_copy(sorted_ref.at[s], _token_rows(xs_ref, 0, rows), sem.at[s]).wait()

    @pl.when(k >= 2)
    def _():
        wait_slot(slot)

    pos0 = rt_ref[2:3, :]
    pos1 = rt_ref[3:4, :]
    r = lax.broadcasted_iota(jnp.int32, (rows, tm), 0).astype(F32)
    perm = jnp.where((r == pos0) | (r == pos1), 1.0, 0.0).astype(BF16)
    srt = _dot(perm, h_ref[...].astype(BF16))
    buf = sorted_ref.at[slot]
    _pack_rows(buf, srt)

    def copy_piece(local, dst, size):
        pltpu.make_async_copy(_token_rows(buf, local, size), _token_rows(xs_ref, dst, size),
                              sem.at[slot]).start()

    _for_each_run_piece(rdst_ref, rlen_ref, rmax_ref, k, tm, copy_piece)

    @pl.when(k == nk - 1)
    def _():
        wait_slot(slot)

        @pl.when(nk >= 2)
        def _():
            wait_slot(1 - slot)


def _dispatch(run_dst, run_len, run_max, h2, rt):
    t, d = h2.shape
    assert d == PACK_ROWS * LANES
    tm = MOVE_TM
    return pl.pallas_call(
        _dispatch_kernel,
        grid_spec=pltpu.PrefetchScalarGridSpec(
            num_scalar_prefetch=3,
            grid=(t // tm,),
            in_specs=[pl.BlockSpec((tm, d), lambda i, *_: (i, 0)),
                      pl.BlockSpec((8, tm), lambda i, *_: (0, i))],
            out_specs=pl.BlockSpec(memory_space=pl.ANY),
            scratch_shapes=[pltpu.VMEM((2, 2 * tm * PACK_ROWS, LANES), ROW_DTYPE),
                            pltpu.SemaphoreType.DMA((2,))]),
        out_shape=jax.ShapeDtypeStruct((2 * t * PACK_ROWS, LANES), ROW_DTYPE),
        compiler_params=_params("arbitrary"),
    )(run_dst, run_len, run_max, h2, rt)


def _expert_kernel(vt_ref, ve_ref, va_ref, vb_ref, vn_ref, nv_ref, xs_ref, wg_hbm, wu_hbm, wd_hbm,
                   ys_ref, wg_bf, wu_bf, wd_bf, wg_f32, wu_f32, wd_f32, wsem):
    v = pl.program_id(0)
    valid = v < nv_ref[0]
    prev = jnp.maximum(v - 1, 0)
    new_expert = (v == 0) | (ve_ref[v] != ve_ref[prev])
    new_tile = (v == 0) | (vt_ref[v] != vt_ref[prev])

    def weight_copies(e):
        return [pltpu.make_async_copy(src.at[e], dst, wsem)
                for src, dst in ((wg_hbm, wg_f32), (wu_hbm, wu_f32), (wd_hbm, wd_f32))]

    @pl.when(v == 0)
    def _():
        for cp in weight_copies(ve_ref[0]):
            cp.start()

    @pl.when(valid & new_expert)
    def _():
        for cp in weight_copies(ve_ref[v]):
            cp.wait()
        wg_bf[...] = wg_f32[...].astype(BF16)
        wu_bf[...] = wu_f32[...].astype(BF16)
        wd_bf[...] = wd_f32[...].astype(BF16)

        @pl.when(vn_ref[v] >= 0)
        def _():
            for cp in weight_copies(vn_ref[v]):
                cp.start()

    def expert_rows():
        xb = _unpack_rows(xs_ref)
        gate = _dot(xb, wg_bf[...])
        up = _dot(xb, wu_bf[...])
        act = gate * jax.nn.sigmoid(gate) * up
        return _dot(act.astype(BF16), wd_bf[...])

    @pl.when(valid & new_tile)
    def _():
        _pack_rows(ys_ref, expert_rows())

    @pl.when(valid & jnp.logical_not(new_tile))
    def _():
        y = expert_rows()
        te = y.shape[0]
        row = lax.broadcasted_iota(jnp.int32, (te, 1), 0)
        mine = (row >= va_ref[v]) & (row < vb_ref[v])
        for c in range(PACK_ROWS):
            rows = pl.ds(c, te, stride=PACK_ROWS)
            ys_ref[rows, :] = jnp.where(mine, y[:, c * LANES:(c + 1) * LANES], ys_ref[rows, :])


def _expert_mlp(visit_tile, visit_expert, visit_lo, visit_hi, visit_next, n_visits, xs,
                w_gate, w_up, w_down):
    te = EXPERT_TE
    d, ff = w_gate.shape[1:]
    blk = (te * PACK_ROWS, LANES)
    tile = lambda v, vt, *_: (vt[v], 0)
    hbm = pl.BlockSpec(memory_space=pl.ANY)
    return pl.pallas_call(
        _expert_kernel,
        grid_spec=pltpu.PrefetchScalarGridSpec(
            num_scalar_prefetch=6,
            grid=(visit_tile.shape[0],),
            in_specs=[pl.BlockSpec(blk, tile), hbm, hbm, hbm],
            out_specs=pl.BlockSpec(blk, tile),
            scratch_shapes=[pltpu.VMEM((d, ff), BF16), pltpu.VMEM((d, ff), BF16),
                            pltpu.VMEM((ff, d), BF16),
                            pltpu.VMEM((d, ff), F32), pltpu.VMEM((d, ff), F32),
                            pltpu.VMEM((ff, d), F32), pltpu.SemaphoreType.DMA(())]),
        out_shape=jax.ShapeDtypeStruct(xs.shape, ROW_DTYPE),
        compiler_params=_params("arbitrary"),
    )(visit_tile, visit_expert, visit_lo, visit_hi, visit_next, n_visits, xs,
      w_gate, w_up, w_down)


def _combine_kernel(rdst_ref, rlen_ref, rmax_ref, x1_ref, rinfo_ref, g2_ref, gpost_ref, ys_ref, o_ref,
                    gath_ref, sem):
    k = pl.program_id(0)
    nk = pl.num_programs(0)
    tm = x1_ref.shape[0]
    rows = 2 * tm
    slot = k % 2

    def gather_runs(tile, s):
        buf = gath_ref.at[s]

        def copy_piece(local, src, size):
            pltpu.make_async_copy(_token_rows(ys_ref, src, size), _token_rows(buf, local, size),
                                  sem.at[s]).start()

        _for_each_run_piece(rdst_ref, rlen_ref, rmax_ref, tile, tm, copy_piece)

    @pl.when(k == 0)
    def _():
        gather_runs(0, 0)

    @pl.when(k + 1 < nk)
    def _():
        gather_runs(k + 1, 1 - slot)

    buf = gath_ref.at[slot]
    pltpu.make_async_copy(_token_rows(ys_ref, 0, rows), buf, sem.at[slot]).wait()
    g = _unpack_rows(buf)
    info = rinfo_ref[...]
    col = lax.broadcasted_iota(jnp.int32, (tm, rows), 1).astype(F32)
    pick = jnp.where(col == info[:, 2:3], info[:, 4:5],
                     jnp.where(col == info[:, 3:4], info[:, 5:6], 0.0)).astype(BF16)
    fx = _dot(pick, g)
    o_ref[...] = x1_ref[...] + g2_ref[0, 0] * (_rms(fx) * gpost_ref[...])


def _combine(run_dst, run_len, run_max, x1, rinfo, mod, gpost, ys, seq):
    t, d = x1.shape
    tm = MOVE_TM
    tpb = seq // tm
    batch_of = lambda i: i // tpb
    return pl.pallas_call(
        _combine_kernel,
        grid_spec=pltpu.PrefetchScalarGridSpec(
            num_scalar_prefetch=3,
            grid=(t // tm,),
            in_specs=[pl.BlockSpec((tm, d), lambda i, *_: (i, 0)),
                      pl.BlockSpec((tm, LANES), lambda i, *_: (i, 0)),
                      _mod_spec(MOD_G2, batch_of)(d),
                      pl.BlockSpec((1, d), lambda i, *_: (0, 0)),
                      pl.BlockSpec(memory_space=pl.ANY)],
            out_specs=pl.BlockSpec((tm, d), lambda i, *_: (i, 0)),
            scratch_shapes=[pltpu.VMEM((2, 2 * tm * PACK_ROWS, LANES), ROW_DTYPE),
                            pltpu.SemaphoreType.DMA((2,))]),
        out_shape=jax.ShapeDtypeStruct((t, d), F32),
        compiler_params=_params("arbitrary"),
    )(run_dst, run_len, run_max, x1, rinfo, mod, gpost, ys)


def _rope_tables(seq):
    pos = np.arange(seq)
    row = (pos // GRID_W).astype(np.float32)
    col = (pos % GRID_W).astype(np.float32)
    axis_dim = HEAD_DIM // 2
    inv_freq = (ROPE_THETA ** (-np.arange(0, axis_dim, 2, dtype=np.float32) / axis_dim)).astype(
        np.float32)
    ang = np.concatenate([row[:, None] * inv_freq, col[:, None] * inv_freq], axis=-1)
    pair = (np.arange(LANES) % HEAD_DIM) // 2
    cos = np.cos(ang)[:, pair]
    sin = np.sin(ang)[:, pair]
    even = (np.arange(LANES) % 2) == 0
    tables = (cos, np.where(even, -sin, 0.0), np.where(even, 0.0, sin))
    tables = tables + tuple(tb.T for tb in tables)
    return tuple(jnp.asarray(tb, F32) for tb in tables)


def _segment_ones(n):
    seg = np.arange(n) // HEAD_DIM
    return jnp.asarray(seg[:, None] == seg[None, :], BF16)


def kernel(x, c, ctx, c_ctx, w_mod, b_mod, attn_pre_norm, attn_post_norm, w_in, a_sink,
           b_q_norm, b_k_norm, a_out_norm, b_out_norm, w_out, ffn_pre_norm, ffn_post_norm,
           w_group, b_group, w_router, b_router, w_gate, w_up, w_down):
    batch, seq, d = x.shape
    ctx_len = ctx.shape[1]
    assert w_mod.shape[0] == 1, "single-layer stack only (context stream is never updated)"
    assert seq % ATTN_A_TQ == 0 and seq >= ATTN_A_TQ + 2 * WINDOW
    assert seq % PROJ_TM == 0 and seq % ATTN_B_TQ == 0 and seq % OUT_TM == 0 and seq % MOVE_TM == 0
    t = batch * seq
    nq = d // 2
    nkv = nq // KV_GROUP
    assert nkv == LANES and w_in.shape[2] == 2 * nq + 4 * nkv

    assert batch + 1 <= MOD_ROWS
    cc = jnp.concatenate([c, c_ctx[None, :], jnp.zeros((MOD_ROWS - batch - 1, d), F32)], axis=0)
    mod = _modulation(cc, w_mod[0], b_mod[0]).reshape(cc.shape[0], 6, 1, d)

    x2 = x.reshape(t, d)
    c2 = ctx.reshape(batch * ctx_len, d)
    gpre = attn_pre_norm[0].reshape(1, d)
    w_in_bf = w_in[0].astype(BF16)
    qn = jnp.broadcast_to(jnp.tile(b_q_norm[0], nq // HEAD_DIM)[:, None], (nq, PROJ_TM))
    kn = jnp.tile(b_k_norm[0], nkv // HEAD_DIM).reshape(1, nkv)
    seg_k = _segment_ones(nkv)
    qat, ka, vat, qbt, kb, vbt, stats = _project_latents(
        x2, mod, gpre, w_in_bf, _rope_tables(seq), qn, kn, seg_k, seq)
    kca, vcat, kcb, vcbt, ctx_stats = _project_context(
        c2, mod, batch, gpre, w_in_bf, kn, seg_k, ctx_len)

    q_sq = jnp.max(stats[:, 0, 0])
    k_sq = jnp.maximum(jnp.max(stats[:, 0, 1]), jnp.max(ctx_stats[:, 0, 1]))
    bound_a = jnp.maximum(1.01 * jnp.sqrt(q_sq * k_sq), jnp.max(a_sink[0]) * LOG2_E).reshape(1)
    oa = _attention_a(bound_a, a_sink[0], qat, ka, vat, kca, vcat, batch, seq, ctx_len)
    score_bound = (1.01 * HEAD_DIM ** 0.5 * LOG2_E
                   * jnp.max(jnp.abs(b_q_norm[0])) * jnp.max(jnp.abs(b_k_norm[0]))).reshape(1)
    ob = _attention_b(score_bound, qbt, kb, vbt, kcb, vcbt, batch, seq, ctx_len)

    w_out_bf = w_out[0].astype(BF16)
    lane_pad = LANES - N_EXPERTS - N_GROUPS
    w_r = jnp.pad(jnp.concatenate([w_router[0], w_group[0]], axis=1), ((0, 0), (0, lane_pad)))
    w_r_hi = w_r.astype(BF16)
    w_r_lo = (w_r - w_r_hi.astype(F32)).astype(BF16)
    w_r2 = jnp.concatenate([w_r_hi, w_r_lo], axis=1)
    b_r = jnp.pad(jnp.concatenate([b_router[0], b_group[0]]), (0, lane_pad)).reshape(1, LANES)
    x1, h2, rinfo, rt, tcarry, tcnt, counts = _out_and_route(
        oa, ob, x2, mod, a_out_norm[0].reshape(1, nq), b_out_norm[0].reshape(1, nq),
        attn_post_norm[0].reshape(1, d), ffn_pre_norm[0].reshape(1, d),
        w_out_bf, w_r2, b_r, seq)

    te = EXPERT_TE
    assert (2 * t) % te == 0
    cnt = counts[0, :N_EXPERTS].astype(jnp.int32)
    ends = jnp.cumsum(cnt)
    starts = ends - cnt
    run_dst = (starts[None, :] + tcarry[:, 0, :N_EXPERTS].astype(jnp.int32)).reshape(-1)
    tile_runs = tcnt[:, 0, :N_EXPERTS].astype(jnp.int32)
    run_len = tile_runs.reshape(-1)
    run_max = jnp.max(tile_runs, axis=1)
    first_tile = starts // te
    n_vis = jnp.where(cnt > 0, (ends - 1) // te - first_tile + 1, 0)
    vis_end = jnp.cumsum(n_vis)
    n_visits = vis_end[-1]
    v = jnp.minimum(jnp.arange(2 * t // te + N_EXPERTS, dtype=jnp.int32), n_visits - 1)
    v_expert = jnp.sum(vis_end[None, :] <= v[:, None], axis=1).astype(jnp.int32)
    pick = (v_expert[:, None] == jnp.arange(N_EXPERTS)[None, :]).astype(jnp.int32)
    of_expert = lambda table: jnp.sum(pick * table[None, :], axis=1)
    v_tile = of_expert(first_tile) + v - of_expert(vis_end - n_vis)
    v_lo = jnp.maximum(of_expert(starts) - v_tile * te, 0)
    v_hi = jnp.minimum(of_expert(ends) - v_tile * te, te)
    ids = jnp.arange(N_EXPERTS, dtype=jnp.int32)
    later = (ids[None, :] > ids[:, None]) & (cnt[None, :] > 0)
    next_expert = jnp.min(jnp.where(later, ids[None, :], N_EXPERTS), axis=1)
    v_next = of_expert(jnp.where(next_expert < N_EXPERTS, next_expert, -1))

    xs = _dispatch(run_dst, run_len, run_max, h2, rt)
    ys = _expert_mlp(v_tile, v_expert, v_lo, v_hi, v_next, n_visits.reshape(1), xs,
                     w_gate[0], w_up[0], w_down[0])
    out = _combine(run_dst, run_len, run_max, x1, rinfo, mod, ffn_post_norm[0].reshape(1, d), ys,
                   seq)
    return out.reshape(batch, seq, d)
```

```python
import functools

import jax
import jax.numpy as jnp
import numpy as np
from jax import lax
from jax.experimental import pallas as pl
from jax.experimental.pallas import tpu as pltpu

F32 = jnp.float32
BF16 = jnp.bfloat16

GRID_W = 64
HEAD_DIM = 64
KV_GROUP = 4
WINDOW = 128
ROPE_THETA = 10000.0
N_GROUPS = 4
EXPERTS_PER_GROUP = 8
N_EXPERTS = N_GROUPS * EXPERTS_PER_GROUP
EPS = 1e-6
NEG_INF = -1e30
LOG2_E = 1.4426950408889634
SAFE_SCORE_BOUND = 40.0

LANES = 128
V7X_VMEM_LIMIT = 56 * 1024 * 1024

MOD_ROWS = 16
MOD_BN = 1024
PROJ_TM = 512
ATTN_A_TQ = 256
ATTN_B_TQ = 256
OUT_TM = 512
EXPERT_TE = 512
MOVE_TM = OUT_TM


def _params(*sem):
    return pltpu.CompilerParams(dimension_semantics=sem, vmem_limit_bytes=V7X_VMEM_LIMIT)


def _dot(a, b):
    return jnp.dot(a, b, preferred_element_type=F32)


def _rms(x):
    return x * lax.rsqrt(jnp.mean(x * x, axis=-1, keepdims=True) + EPS)


def _split_bf16(x):
    hi = x.astype(BF16)
    lo = (x - hi.astype(F32)).astype(BF16)
    return hi, lo


def _mod_kernel(c_ref, w_ref, b_ref, o_ref):
    cc = c_ref[...]
    s = cc * jax.nn.sigmoid(cc)
    s_hi, s_lo = _split_bf16(s)
    w_hi, w_lo = _split_bf16(w_ref[...])
    o_ref[...] = _dot(s_hi, w_hi) + _dot(s_lo, w_hi) + _dot(s_hi, w_lo) + b_ref[...]


def _modulation(cc, w_mod, b_mod):
    rows, d = cc.shape
    n = w_mod.shape[1]
    bn = MOD_BN
    return pl.pallas_call(
        _mod_kernel,
        grid=(n // bn,),
        in_specs=[pl.BlockSpec((rows, d), lambda i: (0, 0)),
                  pl.BlockSpec((d, bn), lambda i: (0, i)),
                  pl.BlockSpec((1, bn), lambda i: (0, i))],
        out_specs=pl.BlockSpec((rows, bn), lambda i: (0, i)),
        out_shape=jax.ShapeDtypeStruct((rows, n), F32),
        compiler_params=_params("arbitrary"),
    )(cc, w_mod, b_mod.reshape(1, n))


def _rope(x, cos, sin_a, sin_b):
    return x * cos + pltpu.roll(x, LANES - 1, 1) * sin_a + pltpu.roll(x, 1, 1) * sin_b


def _head_norm(x, seg_ref, gain):
    ss = _dot((x * x).astype(BF16), seg_ref[...])
    return x * lax.rsqrt(ss * (1.0 / HEAD_DIM) + EPS) * gain


def _max_head_sq_norm(x, seg_ref):
    ss = _dot((x * x).astype(BF16), seg_ref[...])
    return jnp.max(jnp.max(ss, axis=1, keepdims=True), axis=0, keepdims=True)


def _norm_stats(q_sq, k_sq):
    lane = lax.broadcasted_iota(jnp.int32, (1, LANES), 1)
    zero = jnp.zeros((1, LANES), F32)
    return jnp.where(lane == 0, q_sq, zero) + jnp.where(lane == 1, k_sq, zero)


def _rope_t(xt, cos_t, sin_a_t, sin_b_t):
    return (xt * cos_t + pltpu.roll(xt, LANES - 1, 0) * sin_a_t
            + pltpu.roll(xt, 1, 0) * sin_b_t)


def _proj_kernel(x_ref, sc_ref, sh_ref, gpre_ref, w_ref, cos_ref, sa_ref, sb_ref,
                 cos_t_ref, sa_t_ref, sb_t_ref, qn_ref, kn_ref, seg_k_ref,
                 qat_ref, ka_ref, vat_ref, qbt_ref, kb_ref, vbt_ref, stats_ref):
    h = _rms(x_ref[...]) * gpre_ref[...] * (1.0 + sc_ref[0, 0]) + sh_ref[0, 0]
    p = _dot(h.astype(BF16), w_ref[...])
    cos, sa, sb = cos_ref[...], sa_ref[...], sb_ref[...]
    q_scale = HEAD_DIM ** -0.5 * LOG2_E
    cos_t, sa_t, sb_t = cos_t_ref[...] * q_scale, sa_t_ref[...] * q_scale, sb_t_ref[...] * q_scale
    nq = qat_ref.shape[0]
    q_sq = None
    for c in range(nq // LANES):
        xt = p[:, c * LANES:(c + 1) * LANES].T
        qat_ref[c * LANES:(c + 1) * LANES, :] = _rope_t(xt, cos_t, sa_t, sb_t).astype(BF16)
        sq = xt * xt
        for head in range(LANES // HEAD_DIM):
            norm = jnp.sum(sq[head * HEAD_DIM:(head + 1) * HEAD_DIM, :], axis=0, keepdims=True)
            q_sq = norm if q_sq is None else jnp.maximum(q_sq, norm)
    q_sq = jnp.max(q_sq, axis=1, keepdims=True) * (q_scale * q_scale)
    o = nq
    ka_ref[...] = _rope(p[:, o:o + LANES], cos, sa, sb).astype(BF16)
    vat_ref[...] = p[:, o + LANES:o + 2 * LANES].T.astype(BF16)
    stats_ref[0] = _norm_stats(q_sq, _max_head_sq_norm(p[:, o:o + LANES], seg_k_ref))
    o += 2 * LANES
    for c in range(nq // LANES):
        xt = p[:, o + c * LANES:o + (c + 1) * LANES].T
        sq = xt * xt
        halves = []
        for head in range(LANES // HEAD_DIM):
            rows = slice(head * HEAD_DIM, (head + 1) * HEAD_DIM)
            ms = jnp.sum(sq[rows, :], axis=0, keepdims=True) * (1.0 / HEAD_DIM)
            halves.append(xt[rows, :] * lax.rsqrt(ms + EPS))
        qn = jnp.concatenate(halves, axis=0) * qn_ref[c * LANES:(c + 1) * LANES, :]
        qbt_ref[c * LANES:(c + 1) * LANES, :] = _rope_t(qn, cos_t, sa_t, sb_t).astype(BF16)
    o += nq
    kb = _head_norm(p[:, o:o + LANES], seg_k_ref, kn_ref[...])
    kb_ref[...] = _rope(kb, cos, sa, sb).astype(BF16)
    vbt_ref[...] = p[:, o + LANES:o + 2 * LANES].T.astype(BF16)


def _ctx_proj_kernel(x_ref, sc_ref, sh_ref, gpre_ref, wa_ref, wb_ref, kn_ref, seg_k_ref,
                     ka_ref, vat_ref, kb_ref, vbt_ref, stats_ref):
    h = (_rms(x_ref[...]) * gpre_ref[...] * (1.0 + sc_ref[0, 0]) + sh_ref[0, 0]).astype(BF16)
    pa = _dot(h, wa_ref[...])
    pb = _dot(h, wb_ref[...])
    stats_ref[0] = _norm_stats(0.0, _max_head_sq_norm(pa[:, 0:LANES], seg_k_ref))
    ka_ref[...] = pa[:, 0:LANES].astype(BF16)
    vat_ref[...] = pa[:, LANES:2 * LANES].T.astype(BF16)
    kb_ref[...] = _head_norm(pb[:, 0:LANES], seg_k_ref, kn_ref[...]).astype(BF16)
    vbt_ref[...] = pb[:, LANES:2 * LANES].T.astype(BF16)


def _mod_spec(chunk, row_of_step):
    return lambda d: pl.BlockSpec((1, 1, 1, d), lambda i, *_: (row_of_step(i), chunk, 0, 0))


MOD_SH1, MOD_SC1, MOD_G1, MOD_SH2, MOD_SC2, MOD_G2 = range(6)


def _project_latents(x2, mod, gpre, w_in, tables, qn, kn, seg_k, seq):
    t, d = x2.shape
    tm = PROJ_TM
    tpb = seq // tm
    nq = qn.shape[0]
    const = lambda shape: pl.BlockSpec(shape, lambda i: (0,) * len(shape))
    batch_of = lambda i: i // tpb
    table = pl.BlockSpec((tm, LANES), lambda i: (i % tpb, 0))
    table_t = pl.BlockSpec((LANES, tm), lambda i: (0, i % tpb))
    k_spec = pl.BlockSpec((tm, LANES), lambda i: (i, 0))
    k_shape = jax.ShapeDtypeStruct((t, LANES), BF16)
    vt_spec = pl.BlockSpec((LANES, tm), lambda i: (0, i))
    vt_shape = jax.ShapeDtypeStruct((LANES, t), BF16)
    qt_spec = pl.BlockSpec((nq, tm), lambda i: (0, i))
    qt_shape = jax.ShapeDtypeStruct((nq, t), BF16)
    return pl.pallas_call(
        _proj_kernel,
        grid=(t // tm,),
        in_specs=[pl.BlockSpec((tm, d), lambda i: (i, 0)),
                  _mod_spec(MOD_SC1, batch_of)(d), _mod_spec(MOD_SH1, batch_of)(d), const((1, d)),
                  const(w_in.shape), table, table, table, table_t, table_t, table_t,
                  const((nq, tm)), const((1, LANES)), const(seg_k.shape)],
        out_specs=[qt_spec, k_spec, vt_spec, qt_spec, k_spec, vt_spec,
                   pl.BlockSpec((1, 1, LANES), lambda i: (i, 0, 0))],
        out_shape=[qt_shape, k_shape, vt_shape, qt_shape, k_shape, vt_shape,
                   jax.ShapeDtypeStruct((t // tm, 1, LANES), F32)],
        compiler_params=_params("arbitrary"),
    )(x2, mod, mod, gpre, w_in, *tables, qn, kn, seg_k)


def _project_context(c2, mod, ctx_row, gpre, w_in, kn, seg_k, ctx_len):
    t, d = c2.shape
    ctx_mod = lambda i: ctx_row
    nq = (w_in.shape[1] - 4 * LANES) // 2
    kv = 2 * LANES
    assert nq % kv == 0
    group_kv = lambda g: pl.BlockSpec((d, kv), lambda i: (0, (g * (nq + kv) + nq) // kv))
    const = lambda shape: pl.BlockSpec(shape, lambda i: (0,) * len(shape))
    k_spec = pl.BlockSpec((ctx_len, LANES), lambda i: (i, 0))
    k_shape = jax.ShapeDtypeStruct((t, LANES), BF16)
    vt_spec = pl.BlockSpec((LANES, ctx_len), lambda i: (0, i))
    vt_shape = jax.ShapeDtypeStruct((LANES, t), BF16)
    return pl.pallas_call(
        _ctx_proj_kernel,
        grid=(t // ctx_len,),
        in_specs=[pl.BlockSpec((ctx_len, d), lambda i: (i, 0)),
                  _mod_spec(MOD_SC1, ctx_mod)(d), _mod_spec(MOD_SH1, ctx_mod)(d),
                  const((1, d)), group_kv(0), group_kv(1), const((1, LANES)), const(seg_k.shape)],
        out_specs=[k_spec, vt_spec, k_spec, vt_spec,
                   pl.BlockSpec((1, 1, LANES), lambda i: (i, 0, 0))],
        out_shape=[k_shape, vt_shape, k_shape, vt_shape,
                   jax.ShapeDtypeStruct((t // ctx_len, 1, LANES), F32)],
        compiler_params=_params("arbitrary"),
    )(c2, mod, mod, gpre, w_in, w_in, kn, seg_k)


def _attend_t(w, k, kc, vt, vct, bounded=False, bias=None, sink=None):
    st = _dot(k, w)
    sct = _dot(kc, w)
    if bias is not None:
        tq = bias.shape[1]
        st = jnp.concatenate([st[:, c * tq:(c + 1) * tq] + bias
                              for c in range(st.shape[1] // tq)], axis=1)
    if not bounded:
        shift = jnp.maximum(jnp.max(st, axis=0, keepdims=True),
                            jnp.max(sct, axis=0, keepdims=True))
        if sink is not None:
            shift = jnp.maximum(shift, sink)
            sink = sink - shift
        st, sct = st - shift, sct - shift
    pt = jnp.exp2(st)
    pct = jnp.exp2(sct)
    denom = jnp.sum(pt, axis=0, keepdims=True) + jnp.sum(pct, axis=0, keepdims=True)
    if sink is not None:
        denom = denom + jnp.exp2(sink)
    o2 = _dot(vt, pt.astype(BF16)) + _dot(vct, pct.astype(BF16))
    return o2, denom


def _all_heads_t(qt_ref, o_ref, attend):
    tq = qt_ref.shape[1]
    n_kv = LANES // HEAD_DIM
    zeros = jnp.zeros((HEAD_DIM, tq), BF16)
    outs = []
    for h in range(qt_ref.shape[0] // HEAD_DIM):
        g = h // KV_GROUP
        qh = qt_ref[h * HEAD_DIM:(h + 1) * HEAD_DIM, :]
        w = jnp.concatenate([zeros] * g + [qh] + [zeros] * (n_kv - 1 - g), axis=0)
        o2, denom = attend(h, w)
        outs.append(o2[g * HEAD_DIM:(g + 1) * HEAD_DIM, :] / denom)
    o_ref[...] = jnp.concatenate(outs, axis=0).T.astype(BF16)


def _all_heads_fused_t(qt_ref, o_ref, attend):
    tq = qt_ref.shape[1]
    n_heads = qt_ref.shape[0] // HEAD_DIM
    n_kv = LANES // HEAD_DIM
    rows = []
    for g in range(n_kv):
        heads = [qt_ref[h * HEAD_DIM:(h + 1) * HEAD_DIM, :] if h // KV_GROUP == g
                 else jnp.zeros((HEAD_DIM, tq), BF16) for h in range(n_heads)]
        rows.append(jnp.concatenate(heads, axis=1))
    w = jnp.concatenate(rows, axis=0)
    o2, denom = attend(0, w)
    o2 = o2 / denom
    outs = [o2[(h // KV_GROUP) * HEAD_DIM:(h // KV_GROUP + 1) * HEAD_DIM, h * tq:(h + 1) * tq]
            for h in range(n_heads)]
    o_ref[...] = jnp.concatenate(outs, axis=0).T.astype(BF16)


def _attn_b_kernel(bound_ref, qt_ref, k_ref, vt_ref, kc_ref, vct_ref, o_ref):
    k, kc, vt, vct = k_ref[...], kc_ref[...], vt_ref[...], vct_ref[...]
    bound = bound_ref[0]

    @pl.when(bound <= SAFE_SCORE_BOUND)
    def _():
        _all_heads_fused_t(qt_ref, o_ref, lambda h, w: _attend_t(w, k, kc, vt, vct, bounded=True))

    @pl.when(jnp.logical_not(bound <= SAFE_SCORE_BOUND))
    def _():
        _all_heads_t(qt_ref, o_ref, lambda h, w: _attend_t(w, k, kc, vt, vct))


def _attn_specs(nq, tq, seq, ctx_len):
    nqb = seq // tq
    return dict(
        qt=pl.BlockSpec((nq, tq), lambda b, i: (0, b * nqb + i)),
        k=pl.BlockSpec((seq, LANES), lambda b, i: (b, 0)),
        vt=pl.BlockSpec((LANES, seq), lambda b, i: (0, b)),
        kc=pl.BlockSpec((ctx_len, LANES), lambda b, i: (b, 0)),
        vct=pl.BlockSpec((LANES, ctx_len), lambda b, i: (0, b)),
        out=pl.BlockSpec((tq, nq), lambda b, i: (b * nqb + i, 0)))


def _attention_b(score_bound, qbt, kb, vbt, kcb, vcbt, batch, seq, ctx_len):
    nq, t = qbt.shape
    tq = ATTN_B_TQ
    sp = _attn_specs(nq, tq, seq, ctx_len)
    return pl.pallas_call(
        _attn_b_kernel,
        grid=(batch, seq // tq),
        in_specs=[pl.BlockSpec(memory_space=pltpu.SMEM),
                  sp["qt"], sp["k"], sp["vt"], sp["kc"], sp["vct"]],
        out_specs=sp["out"],
        out_shape=jax.ShapeDtypeStruct((t, nq), BF16),
        compiler_params=_params("arbitrary", "arbitrary"),
    )(score_bound, qbt, kb, vbt, kcb, vcbt)


def _attn_a_kernel(bound_ref, sink_ref, qt_ref, k_ref, vt_ref, kc_ref, vct_ref, bias_ref, o_ref,
                   *, seq):
    i = pl.program_id(1)
    tq = qt_ref.shape[1]
    n_heads = qt_ref.shape[0] // HEAD_DIM
    win = tq + 2 * WINDOW
    start = pl.multiple_of(jnp.clip(i * tq - WINDOW, 0, seq - win), WINDOW)
    k = k_ref[pl.ds(start, win), :]
    vt = vt_ref[:, pl.ds(start, win)]
    kc, vct = kc_ref[...], vct_ref[...]
    sinks = [sink_ref[h] * LOG2_E for h in range(n_heads)]

    bound = bound_ref[0]
    small = bound <= SAFE_SCORE_BOUND

    @pl.when(small)
    def _():
        sink_row = jnp.concatenate([jnp.full((1, tq), s, F32) for s in sinks], axis=1)
        _all_heads_fused_t(qt_ref, o_ref, lambda h, w: _attend_t(
            w, k, kc, vt, vct, bounded=True, bias=bias_ref[0], sink=sink_row))

    @pl.when(jnp.logical_not(small))
    def _():
        _all_heads_t(qt_ref, o_ref, lambda h, w: _attend_t(
            w, k, kc, vt, vct, bias=bias_ref[0], sink=sinks[h]))


def _band_bias(tq):
    win = tq + 2 * WINDOW
    r = np.arange(win)[:, None]
    j = np.arange(tq)[None, :]
    tables = [np.where(np.abs(off + r - j) <= WINDOW, 0.0, NEG_INF)
              for off in (0, -WINDOW, -2 * WINDOW)]
    return jnp.asarray(np.stack(tables), F32)


def _attention_a(score_bound, sink, qat, ka, vat, kca, vcat, batch, seq, ctx_len):
    nq, t = qat.shape
    tq = ATTN_A_TQ
    nqb = seq // tq
    win = tq + 2 * WINDOW
    assert nqb >= 2 and tq >= WINDOW
    sp = _attn_specs(nq, tq, seq, ctx_len)
    which = lambda b, i: (jnp.where(i == 0, 0, jnp.where(i == nqb - 1, 2, 1)), 0, 0)
    return pl.pallas_call(
        functools.partial(_attn_a_kernel, seq=seq),
        grid=(batch, nqb),
        in_specs=[pl.BlockSpec(memory_space=pltpu.SMEM), pl.BlockSpec(memory_space=pltpu.SMEM),
                  sp["qt"], sp["k"], sp["vt"], sp["kc"], sp["vct"],
                  pl.BlockSpec((1, win, tq), which)],
        out_specs=sp["out"],
        out_shape=jax.ShapeDtypeStruct((t, nq), BF16),
        compiler_params=_params("arbitrary", "arbitrary"),
    )(score_bound, sink, qat, ka, vat, kca, vcat, _band_bias(tq))


def _out_kernel(oa_ref, ob_ref, x_ref, g1_ref, sc2_ref, sh2_ref, ga_ref, gb_ref, gpost_ref,
                gpre2_ref, woa_ref, wob_ref, wr_ref, br_ref,
                x1_ref, h2_ref, rinfo_ref, rt_ref, tcarry_ref, tcnt_ref, cnt_ref, carry_ref):
    step = pl.program_id(0)

    @pl.when(step == 0)
    def _():
        carry_ref[...] = jnp.zeros_like(carry_ref)

    na = _rms(oa_ref[...].astype(F32)) * ga_ref[...]
    nb = _rms(ob_ref[...].astype(F32)) * gb_ref[...]
    ox = _dot(na.astype(BF16), woa_ref[...]) + _dot(nb.astype(BF16), wob_ref[...])
    x1 = x_ref[...] + g1_ref[0, 0] * (_rms(ox) * gpost_ref[...])
    x1_ref[...] = x1
    h2 = _rms(x1) * gpre2_ref[...] * (1.0 + sc2_ref[0, 0]) + sh2_ref[0, 0]
    h_hi, h_lo = _split_bf16(h2)
    h2_ref[...] = h_hi

    both = _dot(h_hi, wr_ref[...])
    logits = (both[:, :LANES] + _dot(h_lo, wr_ref[:, :LANES]) + both[:, LANES:]
              + br_ref[...])
    tm = logits.shape[0]
    lt = logits.T
    row = lax.broadcasted_iota(jnp.int32, lt.shape, 0)
    rowf = row.astype(F32)
    big = jnp.float32(1e9)
    ninf = jnp.float32(-jnp.inf)
    colmax = lambda v: jnp.max(v, axis=0, keepdims=True)
    colmin = lambda v: jnp.min(v, axis=0, keepdims=True)
    colsum = lambda v: jnp.sum(v, axis=0, keepdims=True)

    gmask = (row >= N_EXPERTS) & (row < N_EXPERTS + N_GROUPS)
    lg = jnp.where(gmask, lt, ninf)
    gmax = colmax(lg)
    gidx = colmin(jnp.where(lg == gmax, rowf, big)) - N_EXPERTS
    g_w = 1.0 / colsum(jnp.exp(lg - gmax))
    row_group = (row // EXPERTS_PER_GROUP).astype(F32)
    emask = (row < N_EXPERTS) & (row_group == gidx)
    le = jnp.where(emask, lt, ninf)
    m1 = colmax(le)
    i1 = colmin(jnp.where(le == m1, rowf, big))
    le2 = jnp.where(rowf == i1, ninf, le)
    m2 = colmax(le2)
    i2 = colmin(jnp.where(le2 == m2, rowf, big))
    e2 = jnp.exp(m2 - m1)
    w0 = g_w / (1.0 + e2)
    w1 = g_w * e2 / (1.0 + e2)

    hit1 = rowf == i1
    hit2 = rowf == i2
    onehot = jnp.where(hit1, 1.0, jnp.where(hit2, 1.0, 0.0)).astype(F32)
    r = lax.broadcasted_iota(jnp.int32, (tm, tm), 0)
    c = lax.broadcasted_iota(jnp.int32, (tm, tm), 1)
    earlier = jnp.where(r < c, 1.0, 0.0).astype(BF16)
    within = _dot(onehot.astype(BF16), earlier)
    tile_cnt = jnp.broadcast_to(jnp.sum(onehot, axis=1, keepdims=True), (LANES, LANES))
    er = lax.broadcasted_iota(jnp.int32, (LANES, LANES), 0)
    ec = lax.broadcasted_iota(jnp.int32, (LANES, LANES), 1)
    below = jnp.where(er > ec, 1.0, 0.0).astype(BF16)
    cnt_hi = jnp.floor(tile_cnt * (1.0 / 32.0))
    cnt_lo = tile_cnt - 32.0 * cnt_hi
    run_start = 32.0 * _dot(below, cnt_hi.astype(BF16)) + _dot(below, cnt_lo.astype(BF16))
    local = within + run_start[:, 0:1]
    pos0 = colsum(jnp.where(hit1, local, 0.0))
    pos1 = colsum(jnp.where(hit2, local, 0.0))
    cnt_row = tile_cnt.T[0:1, :]
    tcarry_ref[0] = carry_ref[...]
    tcnt_ref[0] = cnt_row
    carry_ref[...] += cnt_row
    cnt_ref[...] = carry_ref[...]

    fields = jnp.concatenate([i1, i2, pos0, pos1, w0, w1, jnp.zeros((2, tm), F32)], axis=0)
    rt_ref[...] = fields
    rinfo_ref[...] = jnp.concatenate(
        [fields, jnp.zeros((LANES - 8, tm), F32)], axis=0).T


def _out_and_route(oa, ob, x2, mod, ga, gb, gpost, gpre2, w_out, wr, br, seq):
    t, d = x2.shape
    tm = OUT_TM
    tpb = seq // tm
    nq = oa.shape[1]
    const = lambda shape: pl.BlockSpec(shape, lambda i: (0,) * len(shape))
    batch_of = lambda i: i // tpb
    rows = lambda n: pl.BlockSpec((tm, n), lambda i: (i, 0))
    per_tile = pl.BlockSpec((1, 1, LANES), lambda i: (i, 0, 0))
    w_half = lambda g: pl.BlockSpec((nq, d), lambda i: (g, 0))
    return pl.pallas_call(
        _out_kernel,
        grid=(t // tm,),
        in_specs=[rows(nq), rows(nq), rows(d),
                  _mod_spec(MOD_G1, batch_of)(d), _mod_spec(MOD_SC2, batch_of)(d),
                  _mod_spec(MOD_SH2, batch_of)(d),
                  const((1, nq)), const((1, nq)), const((1, d)), const((1, d)),
                  w_half(0), w_half(1), const(wr.shape),
                  const((1, LANES))],
        out_specs=[rows(d), rows(d), rows(LANES), pl.BlockSpec((8, tm), lambda i: (0, i)),
                   per_tile, per_tile, const((1, LANES))],
        out_shape=[jax.ShapeDtypeStruct((t, d), F32), jax.ShapeDtypeStruct((t, d), BF16),
                   jax.ShapeDtypeStruct((t, LANES), F32), jax.ShapeDtypeStruct((8, t), F32),
                   jax.ShapeDtypeStruct((t // tm, 1, LANES), F32),
                   jax.ShapeDtypeStruct((t // tm, 1, LANES), F32),
                   jax.ShapeDtypeStruct((1, LANES), F32)],
        scratch_shapes=[pltpu.VMEM((1, LANES), F32)],
        compiler_params=_params("arbitrary"),
    )(oa, ob, x2, mod, mod, mod, ga, gb, gpost, gpre2, w_out, w_out, wr, br)


PACK_ROWS = 8
ROW_DTYPE = F32


def _pack_rows(ref, x):
    n = x.shape[0]
    for c in range(PACK_ROWS):
        ref[pl.ds(c, n, stride=PACK_ROWS), :] = x[:, c * LANES:(c + 1) * LANES]


def _unpack_rows(ref):
    n = ref.shape[0] // PACK_ROWS
    return jnp.concatenate(
        [ref[pl.ds(c, n, stride=PACK_ROWS), :].astype(BF16) for c in range(PACK_ROWS)], axis=1)


def _for_each_run_piece(rdst_ref, rlen_ref, rmax_ref, tile, max_len, fn):
    def copy_runs(n_bits):
        def run(e, local):
            length = rlen_ref[tile * N_EXPERTS + e]
            dst = rdst_ref[tile * N_EXPERTS + e]
            for b in range(n_bits):
                size = 1 << b

                @pl.when(((length >> b) & 1) == 1)
                def _():
                    done = length & (size - 1)
                    fn(local + done, dst + done, size)
            return local + length

        lax.fori_loop(0, N_EXPERTS, run, 0)

    all_bits = max_len.bit_length()
    low_bits = min(all_bits, (4 * max_len // N_EXPERTS).bit_length())
    short = rmax_ref[tile] < (1 << low_bits)

    @pl.when(short)
    def _():
        copy_runs(low_bits)

    @pl.when(jnp.logical_not(short))
    def _():
        copy_runs(all_bits)


def _token_rows(ref, row0, n_rows):
    start = row0 * PACK_ROWS
    if not isinstance(start, int):
        start = pl.multiple_of(start, PACK_ROWS)
    return ref.at[pl.ds(start, n_rows * PACK_ROWS)]


def _dispatch_kernel(rdst_ref, rlen_ref, rmax_ref, h_ref, rt_ref, xs_ref, sorted_ref, sem):
    k = pl.program_id(0)
    nk = pl.num_programs(0)
    tm = h_ref.shape[0]
    rows = 2 * tm
    slot = k % 2

    def wait_slot(s):
        pltpu.make_async_copy(sorted_ref.at[s], _token_rows(xs_ref, 0, rows), sem.at[s]).wait()

    @pl.when(k >= 2)
    def _():
        wait_slot(slot)

    pos0 = rt_ref[2:3, :]
    pos1 = rt_ref[3:4, :]
    r = lax.broadcasted_iota(jnp.int32, (rows, tm), 0).astype(F32)
    perm = jnp.where((r == pos0) | (r == pos1), 1.0, 0.0).astype(BF16)
    srt = _dot(perm, h_ref[...].astype(BF16))
    buf = sorted_ref.at[slot]
    _pack_rows(buf, srt)

    def copy_piece(local, dst, size):
        pltpu.make_async_copy(_token_rows(buf, local, size), _token_rows(xs_ref, dst, size),
                              sem.at[slot]).start()

    _for_each_run_piece(rdst_ref, rlen_ref, rmax_ref, k, tm, copy_piece)

    @pl.when(k == nk - 1)
    def _():
        wait_slot(slot)

        @pl.when(nk >= 2)
        def _():
            wait_slot(1 - slot)


def _dispatch(run_dst, run_len, run_max, h2, rt):
    t, d = h2.shape
    assert d == PACK_ROWS * LANES
    tm = MOVE_TM
    return pl.pallas_call(
        _dispatch_kernel,
        grid_spec=pltpu.PrefetchScalarGridSpec(
            num_scalar_prefetch=3,
            grid=(t // tm,),
            in_specs=[pl.BlockSpec((tm, d), lambda i, *_: (i, 0)),
                      pl.BlockSpec((8, tm), lambda i, *_: (0, i))],
            out_specs=pl.BlockSpec(memory_space=pl.ANY),
            scratch_shapes=[pltpu.VMEM((2, 2 * tm * PACK_ROWS, LANES), ROW_DTYPE),
                            pltpu.SemaphoreType.DMA((2,))]),
        out_shape=jax.ShapeDtypeStruct((2 * t * PACK_ROWS, LANES), ROW_DTYPE),
        compiler_params=_params("arbitrary"),
    )(run_dst, run_len, run_max, h2, rt)


def _expert_kernel(vt_ref, ve_ref, va_ref, vb_ref, vn_ref, nv_ref, xs_ref, wg_hbm, wu_hbm, wd_hbm,
                   ys_ref, wg_bf, wu_bf, wd_bf, wg_f32, wu_f32, wd_f32, wsem):
    v = pl.program_id(0)
    valid = v < nv_ref[0]
    prev = jnp.maximum(v - 1, 0)
    new_expert = (v == 0) | (ve_ref[v] != ve_ref[prev])
    new_tile = (v == 0) | (vt_ref[v] != vt_ref[prev])

    def weight_copies(e):
        return [pltpu.make_async_copy(src.at[e], dst, wsem)
                for src, dst in ((wg_hbm, wg_f32), (wu_hbm, wu_f32), (wd_hbm, wd_f32))]

    @pl.when(v == 0)
    def _():
        for cp in weight_copies(ve_ref[0]):
            cp.start(priority=1)

    @pl.when(valid & new_expert)
    def _():
        for cp in weight_copies(ve_ref[v]):
            cp.wait()
        wg_bf[...] = wg_f32[...].astype(BF16)
        wu_bf[...] = wu_f32[...].astype(BF16)
        wd_bf[...] = wd_f32[...].astype(BF16)

        @pl.when(vn_ref[v] >= 0)
        def _():
            for cp in weight_copies(vn_ref[v]):
                cp.start(priority=1)

    def expert_rows():
        xb = _unpack_rows(xs_ref)
        gate = _dot(xb, wg_bf[...])
        up = _dot(xb, wu_bf[...])
        act = gate * jax.nn.sigmoid(gate) * up
        return _dot(act.astype(BF16), wd_bf[...])

    @pl.when(valid & new_tile)
    def _():
        _pack_rows(ys_ref, expert_rows())

    @pl.when(valid & jnp.logical_not(new_tile))
    def _():
        y = expert_rows()
        te = y.shape[0]
        row = lax.broadcasted_iota(jnp.int32, (te, 1), 0)
        mine = (row >= va_ref[v]) & (row < vb_ref[v])
        for c in range(PACK_ROWS):
            rows = pl.ds(c, te, stride=PACK_ROWS)
            ys_ref[rows, :] = jnp.where(mine, y[:, c * LANES:(c + 1) * LANES], ys_ref[rows, :])


def _expert_mlp(visit_tile, visit_expert, visit_lo, visit_hi, visit_next, n_visits, xs,
                w_gate, w_up, w_down):
    te = EXPERT_TE
    d, ff = w_gate.shape[1:]
    blk = (te * PACK_ROWS, LANES)
    tile = lambda v, vt, *_: (vt[v], 0)
    hbm = pl.BlockSpec(memory_space=pl.ANY)
    return pl.pallas_call(
        _expert_kernel,
        grid_spec=pltpu.PrefetchScalarGridSpec(
            num_scalar_prefetch=6,
            grid=(visit_tile.shape[0],),
            in_specs=[pl.BlockSpec(blk, tile), hbm, hbm, hbm],
            out_specs=pl.BlockSpec(blk, tile),
            scratch_shapes=[pltpu.VMEM((d, ff), BF16), pltpu.VMEM((d, ff), BF16),
                            pltpu.VMEM((ff, d), BF16),
                            pltpu.VMEM((d, ff), F32), pltpu.VMEM((d, ff), F32),
                            pltpu.VMEM((ff, d), F32), pltpu.SemaphoreType.DMA(())]),
        out_shape=jax.ShapeDtypeStruct(xs.shape, ROW_DTYPE),
        compiler_params=_params("arbitrary"),
    )(visit_tile, visit_expert, visit_lo, visit_hi, visit_next, n_visits, xs,
      w_gate, w_up, w_down)


def _combine_kernel(rdst_ref, rlen_ref, rmax_ref, x1_ref, rinfo_ref, g2_ref, gpost_ref, ys_ref, o_ref,
                    gath_ref, sem):
    k = pl.program_id(0)
    nk = pl.num_programs(0)
    tm = x1_ref.shape[0]
    rows = 2 * tm
    slot = k % 2

    def gather_runs(tile, s):
        buf = gath_ref.at[s]

        def copy_piece(local, src, size):
            pltpu.make_async_copy(_token_rows(ys_ref, src, size), _token_rows(buf, local, size),
                                  sem.at[s]).start()

        _for_each_run_piece(rdst_ref, rlen_ref, rmax_ref, tile, tm, copy_piece)

    @pl.when(k == 0)
    def _():
        gather_runs(0, 0)

    @pl.when(k + 1 < nk)
    def _():
        gather_runs(k + 1, 1 - slot)

    buf = gath_ref.at[slot]
    pltpu.make_async_copy(_token_rows(ys_ref, 0, rows), buf, sem.at[slot]).wait()
    g = _unpack_rows(buf)
    info = rinfo_ref[...]
    col = lax.broadcasted_iota(jnp.int32, (tm, rows), 1).astype(F32)
    pick = jnp.where(col == info[:, 2:3], info[:, 4:5],
                     jnp.where(col == info[:, 3:4], info[:, 5:6], 0.0)).astype(BF16)
    fx = _dot(pick, g)
    o_ref[...] = x1_ref[...] + g2_ref[0, 0] * (_rms(fx) * gpost_ref[...])


def _combine(run_dst, run_len, run_max, x1, rinfo, mod, gpost, ys, seq):
    t, d = x1.shape
    tm = MOVE_TM
    tpb = seq // tm
    batch_of = lambda i: i // tpb
    return pl.pallas_call(
        _combine_kernel,
        grid_spec=pltpu.PrefetchScalarGridSpec(
            num_scalar_prefetch=3,
            grid=(t // tm,),
            in_specs=[pl.BlockSpec((tm, d), lambda i, *_: (i, 0)),
                      pl.BlockSpec((tm, LANES), lambda i, *_: (i, 0)),
                      _mod_spec(MOD_G2, batch_of)(d),
                      pl.BlockSpec((1, d), lambda i, *_: (0, 0)),
                      pl.BlockSpec(memory_space=pl.ANY)],
            out_specs=pl.BlockSpec((tm, d), lambda i, *_: (i, 0)),
            scratch_shapes=[pltpu.VMEM((2, 2 * tm * PACK_ROWS, LANES), ROW_DTYPE),
                            pltpu.SemaphoreType.DMA((2,))]),
        out_shape=jax.ShapeDtypeStruct((t, d), F32),
        compiler_params=_params("arbitrary"),
    )(run_dst, run_len, run_max, x1, rinfo, mod, gpost, ys)


def _rope_tables(seq):
    pos = np.arange(seq)
    row = (pos // GRID_W).astype(np.float32)
    col = (pos % GRID_W).astype(np.float32)
    axis_dim = HEAD_DIM // 2
    inv_freq = (ROPE_THETA ** (-np.arange(0, axis_dim, 2, dtype=np.float32) / axis_dim)).astype(
        np.float32)
    ang = np.concatenate([row[:, None] * inv_freq, col[:, None] * inv_freq], axis=-1)
    pair = (np.arange(LANES) % HEAD_DIM) // 2
    cos = np.cos(ang)[:, pair]
    sin = np.sin(ang)[:, pair]
    even = (np.arange(LANES) % 2) == 0
    tables = (cos, np.where(even, -sin, 0.0), np.where(even, 0.0, sin))
    tables = tables + tuple(tb.T for tb in tables)
    return tuple(jnp.asarray(tb, F32) for tb in tables)


def _segment_ones(n):
    seg = np.arange(n) // HEAD_DIM
    return jnp.asarray(seg[:, None] == seg[None, :], BF16)


def kernel(x, c, ctx, c_ctx, w_mod, b_mod, attn_pre_norm, attn_post_norm, w_in, a_sink,
           b_q_norm, b_k_norm, a_out_norm, b_out_norm, w_out, ffn_pre_norm, ffn_post_norm,
           w_group, b_group, w_router, b_router, w_gate, w_up, w_down):
    batch, seq, d = x.shape
    ctx_len = ctx.shape[1]
    assert w_mod.shape[0] == 1, "single-layer stack only (context stream is never updated)"
    assert seq % ATTN_A_TQ == 0 and seq >= ATTN_A_TQ + 2 * WINDOW
    assert seq % PROJ_TM == 0 and seq % ATTN_B_TQ == 0 and seq % OUT_TM == 0 and seq % MOVE_TM == 0
    t = batch * seq
    nq = d // 2
    nkv = nq // KV_GROUP
    assert nkv == LANES and w_in.shape[2] == 2 * nq + 4 * nkv

    assert batch + 1 <= MOD_ROWS
    cc = jnp.concatenate([c, c_ctx[None, :], jnp.zeros((MOD_ROWS - batch - 1, d), F32)], axis=0)
    mod = _modulation(cc, w_mod[0], b_mod[0]).reshape(cc.shape[0], 6, 1, d)

    x2 = x.reshape(t, d)
    c2 = ctx.reshape(batch * ctx_len, d)
    gpre = attn_pre_norm[0].reshape(1, d)
    w_in_bf = w_in[0].astype(BF16)
    qn = jnp.broadcast_to(jnp.tile(b_q_norm[0], nq // HEAD_DIM)[:, None], (nq, PROJ_TM))
    kn = jnp.tile(b_k_norm[0], nkv // HEAD_DIM).reshape(1, nkv)
    seg_k = _segment_ones(nkv)
    qat, ka, vat, qbt, kb, vbt, stats = _project_latents(
        x2, mod, gpre, w_in_bf, _rope_tables(seq), qn, kn, seg_k, seq)
    kca, vcat, kcb, vcbt, ctx_stats = _project_context(
        c2, mod, batch, gpre, w_in_bf, kn, seg_k, ctx_len)

    q_sq = jnp.max(stats[:, 0, 0])
    k_sq = jnp.maximum(jnp.max(stats[:, 0, 1]), jnp.max(ctx_stats[:, 0, 1]))
    bound_a = jnp.maximum(1.01 * jnp.sqrt(q_sq * k_sq), jnp.max(a_sink[0]) * LOG2_E).reshape(1)
    oa = _attention_a(bound_a, a_sink[0], qat, ka, vat, kca, vcat, batch, seq, ctx_len)
    score_bound = (1.01 * HEAD_DIM ** 0.5 * LOG2_E
                   * jnp.max(jnp.abs(b_q_norm[0])) * jnp.max(jnp.abs(b_k_norm[0]))).reshape(1)
    ob = _attention_b(score_bound, qbt, kb, vbt, kcb, vcbt, batch, seq, ctx_len)

    w_out_bf = w_out[0].astype(BF16)
    lane_pad = LANES - N_EXPERTS - N_GROUPS
    w_r = jnp.pad(jnp.concatenate([w_router[0], w_group[0]], axis=1), ((0, 0), (0, lane_pad)))
    w_r_hi = w_r.astype(BF16)
    w_r_lo = (w_r - w_r_hi.astype(F32)).astype(BF16)
    w_r2 = jnp.concatenate([w_r_hi, w_r_lo], axis=1)
    b_r = jnp.pad(jnp.concatenate([b_router[0], b_group[0]]), (0, lane_pad)).reshape(1, LANES)
    x1, h2, rinfo, rt, tcarry, tcnt, counts = _out_and_route(
        oa, ob, x2, mod, a_out_norm[0].reshape(1, nq), b_out_norm[0].reshape(1, nq),
        attn_post_norm[0].reshape(1, d), ffn_pre_norm[0].reshape(1, d),
        w_out_bf, w_r2, b_r, seq)

    te = EXPERT_TE
    assert (2 * t) % te == 0
    cnt = counts[0, :N_EXPERTS].astype(jnp.int32)
    ends = jnp.cumsum(cnt)
    starts = ends - cnt
    run_dst = (starts[None, :] + tcarry[:, 0, :N_EXPERTS].astype(jnp.int32)).reshape(-1)
    tile_runs = tcnt[:, 0, :N_EXPERTS].astype(jnp.int32)
    run_len = tile_runs.reshape(-1)
    run_max = jnp.max(tile_runs, axis=1)
    first_tile = starts // te
    n_vis = jnp.where(cnt > 0, (ends - 1) // te - first_tile + 1, 0)
    vis_end = jnp.cumsum(n_vis)
    n_visits = vis_end[-1]
    v = jnp.minimum(jnp.arange(2 * t // te + N_EXPERTS, dtype=jnp.int32), n_visits - 1)
    v_expert = jnp.sum(vis_end[None, :] <= v[:, None], axis=1).astype(jnp.int32)
    pick = (v_expert[:, None] == jnp.arange(N_EXPERTS)[None, :]).astype(jnp.int32)
    of_expert = lambda table: jnp.sum(pick * table[None, :], axis=1)
    v_tile = of_expert(first_tile) + v - of_expert(vis_end - n_vis)
    v_lo = jnp.maximum(of_expert(starts) - v_tile * te, 0)
    v_hi = jnp.minimum(of_expert(ends) - v_tile * te, te)
    ids = jnp.arange(N_EXPERTS, dtype=jnp.int32)
    later = (ids[None, :] > ids[:, None]) & (cnt[None, :] > 0)
    next_expert = jnp.min(jnp.where(later, ids[None, :], N_EXPERTS), axis=1)
    v_next = of_expert(jnp.where(next_expert < N_EXPERTS, next_expert, -1))

    xs = _dispatch(run_dst, run_len, run_max, h2, rt)
    ys = _expert_mlp(v_tile, v_expert, v_lo, v_hi, v_next, n_visits.reshape(1), xs,
                     w_gate[0], w_up[0], w_down[0])
    out = _combine(run_dst, run_len, run_max, x1, rinfo, mod, ffn_post_norm[0].reshape(1, d), ys,
                   seq)
    return out.reshape(batch, seq, d)
```

```python
import functools

import jax
import jax.numpy as jnp
import numpy as np
from jax import lax
from jax.experimental import pallas as pl
from jax.experimental.pallas import tpu as pltpu

F32 = jnp.float32
BF16 = jnp.bfloat16

GRID_W = 64
HEAD_DIM = 64
KV_GROUP = 4
WINDOW = 128
ROPE_THETA = 10000.0
N_GROUPS = 4
EXPERTS_PER_GROUP = 8
N_EXPERTS = N_GROUPS * EXPERTS_PER_GROUP
EPS = 1e-6
NEG_INF = -1e30
LOG2_E = 1.4426950408889634
SAFE_SCORE_BOUND = 40.0

LANES = 128
V7X_VMEM_LIMIT = 56 * 1024 * 1024

MOD_ROWS = 16
MOD_BN = 1024
PROJ_TM = 512
ATTN_A_TQ = 256
ATTN_B_TQ = 256
OUT_TM = 512
EXPERT_TE = 512
MOVE_TM = OUT_TM


def _params(*sem):
    return pltpu.CompilerParams(dimension_semantics=sem, vmem_limit_bytes=V7X_VMEM_LIMIT)


def _dot(a, b):
    return jnp.dot(a, b, preferred_element_type=F32)


def _rms(x):
    return x * lax.rsqrt(jnp.mean(x * x, axis=-1, keepdims=True) + EPS)


def _split_bf16(x):
    hi = x.astype(BF16)
    lo = (x - hi.astype(F32)).astype(BF16)
    return hi, lo


def _mod_kernel(c_ref, w_ref, b_ref, o_ref):
    cc = c_ref[...]
    s = cc * jax.nn.sigmoid(cc)
    s_hi, s_lo = _split_bf16(s)
    w_hi, w_lo = _split_bf16(w_ref[...])
    o_ref[...] = _dot(s_hi, w_hi) + _dot(s_lo, w_hi) + _dot(s_hi, w_lo) + b_ref[...]


def _modulation(cc, w_mod, b_mod):
    rows, d = cc.shape
    n = w_mod.shape[1]
    bn = MOD_BN
    return pl.pallas_call(
        _mod_kernel,
        grid=(n // bn,),
        in_specs=[pl.BlockSpec((rows, d), lambda i: (0, 0)),
                  pl.BlockSpec((d, bn), lambda i: (0, i)),
                  pl.BlockSpec((1, bn), lambda i: (0, i))],
        out_specs=pl.BlockSpec((rows, bn), lambda i: (0, i)),
        out_shape=jax.ShapeDtypeStruct((rows, n), F32),
        compiler_params=_params("arbitrary"),
    )(cc, w_mod, b_mod.reshape(1, n))


def _rope(x, cos, sin_a, sin_b):
    return x * cos + pltpu.roll(x, LANES - 1, 1) * sin_a + pltpu.roll(x, 1, 1) * sin_b


def _head_norm(x, seg_ref, gain):
    ss = _dot((x * x).astype(BF16), seg_ref[...])
    return x * lax.rsqrt(ss * (1.0 / HEAD_DIM) + EPS) * gain


def _max_head_sq_norm(x, seg_ref):
    ss = _dot((x * x).astype(BF16), seg_ref[...])
    return jnp.max(jnp.max(ss, axis=1, keepdims=True), axis=0, keepdims=True)


def _norm_stats(q_sq, k_sq):
    lane = lax.broadcasted_iota(jnp.int32, (1, LANES), 1)
    zero = jnp.zeros((1, LANES), F32)
    return jnp.where(lane == 0, q_sq, zero) + jnp.where(lane == 1, k_sq, zero)


def _rope_t(xt, cos_t, sin_a_t, sin_b_t):
    return (xt * cos_t + pltpu.roll(xt, LANES - 1, 0) * sin_a_t
            + pltpu.roll(xt, 1, 0) * sin_b_t)


def _proj_kernel(x_ref, sc_ref, sh_ref, gpre_ref, w_ref, cos_ref, sa_ref, sb_ref,
                 cos_t_ref, sa_t_ref, sb_t_ref, qn_ref, kn_ref, seg_k_ref,
                 qat_ref, ka_ref, vat_ref, qbt_ref, kb_ref, vbt_ref, stats_ref):
    h = _rms(x_ref[...]) * gpre_ref[...] * (1.0 + sc_ref[0, 0]) + sh_ref[0, 0]
    p = _dot(h.astype(BF16), w_ref[...])
    cos, sa, sb = cos_ref[...], sa_ref[...], sb_ref[...]
    q_scale = HEAD_DIM ** -0.5 * LOG2_E
    cos_t, sa_t, sb_t = cos_t_ref[...] * q_scale, sa_t_ref[...] * q_scale, sb_t_ref[...] * q_scale
    nq = qat_ref.shape[0]
    q_sq = None
    for c in range(nq // LANES):
        xt = p[:, c * LANES:(c + 1) * LANES].T
        qat_ref[c * LANES:(c + 1) * LANES, :] = _rope_t(xt, cos_t, sa_t, sb_t).astype(BF16)
        sq = xt * xt
        for head in range(LANES // HEAD_DIM):
            norm = jnp.sum(sq[head * HEAD_DIM:(head + 1) * HEAD_DIM, :], axis=0, keepdims=True)
            q_sq = norm if q_sq is None else jnp.maximum(q_sq, norm)
    q_sq = jnp.max(q_sq, axis=1, keepdims=True) * (q_scale * q_scale)
    o = nq
    ka_ref[...] = _rope(p[:, o:o + LANES], cos, sa, sb).astype(BF16)
    vat_ref[...] = p[:, o + LANES:o + 2 * LANES].T.astype(BF16)
    stats_ref[0] = _norm_stats(q_sq, _max_head_sq_norm(p[:, o:o + LANES], seg_k_ref))
    o += 2 * LANES
    for c in range(nq // LANES):
        xt = p[:, o + c * LANES:o + (c + 1) * LANES].T
        sq = xt * xt
        halves = []
        for head in range(LANES // HEAD_DIM):
            rows = slice(head * HEAD_DIM, (head + 1) * HEAD_DIM)
            ms = jnp.sum(sq[rows, :], axis=0, keepdims=True) * (1.0 / HEAD_DIM)
            halves.append(xt[rows, :] * lax.rsqrt(ms + EPS))
        qn = jnp.concatenate(halves, axis=0) * qn_ref[c * LANES:(c + 1) * LANES, :]
        qbt_ref[c * LANES:(c + 1) * LANES, :] = _rope_t(qn, cos_t, sa_t, sb_t).astype(BF16)
    o += nq
    kb = _head_norm(p[:, o:o + LANES], seg_k_ref, kn_ref[...])
    kb_ref[...] = _rope(kb, cos, sa, sb).astype(BF16)
    vbt_ref[...] = p[:, o + LANES:o + 2 * LANES].T.astype(BF16)


def _ctx_proj_kernel(x_ref, sc_ref, sh_ref, gpre_ref, wa_ref, wb_ref, kn_ref, seg_k_ref,
                     ka_ref, vat_ref, kb_ref, vbt_ref, stats_ref):
    h = (_rms(x_ref[...]) * gpre_ref[...] * (1.0 + sc_ref[0, 0]) + sh_ref[0, 0]).astype(BF16)
    pa = _dot(h, wa_ref[...])
    pb = _dot(h, wb_ref[...])
    stats_ref[0] = _norm_stats(0.0, _max_head_sq_norm(pa[:, 0:LANES], seg_k_ref))
    ka_ref[...] = pa[:, 0:LANES].astype(BF16)
    vat_ref[...] = pa[:, LANES:2 * LANES].T.astype(BF16)
    kb_ref[...] = _head_norm(pb[:, 0:LANES], seg_k_ref, kn_ref[...]).astype(BF16)
    vbt_ref[...] = pb[:, LANES:2 * LANES].T.astype(BF16)


def _mod_spec(chunk, row_of_step):
    return lambda d: pl.BlockSpec((1, 1, 1, d), lambda i, *_: (row_of_step(i), chunk, 0, 0))


MOD_SH1, MOD_SC1, MOD_G1, MOD_SH2, MOD_SC2, MOD_G2 = range(6)


def _project_latents(x2, mod, gpre, w_in, tables, qn, kn, seg_k, seq):
    t, d = x2.shape
    tm = PROJ_TM
    tpb = seq // tm
    nq = qn.shape[0]
    const = lambda shape: pl.BlockSpec(shape, lambda i: (0,) * len(shape))
    batch_of = lambda i: i // tpb
    table = pl.BlockSpec((tm, LANES), lambda i: (i % tpb, 0))
    table_t = pl.BlockSpec((LANES, tm), lambda i: (0, i % tpb))
    k_spec = pl.BlockSpec((tm, LANES), lambda i: (i, 0))
    k_shape = jax.ShapeDtypeStruct((t, LANES), BF16)
    vt_spec = pl.BlockSpec((LANES, tm), lambda i: (0, i))
    vt_shape = jax.ShapeDtypeStruct((LANES, t), BF16)
    qt_spec = pl.BlockSpec((nq, tm), lambda i: (0, i))
    qt_shape = jax.ShapeDtypeStruct((nq, t), BF16)
    return pl.pallas_call(
        _proj_kernel,
        grid=(t // tm,),
        in_specs=[pl.BlockSpec((tm, d), lambda i: (i, 0)),
                  _mod_spec(MOD_SC1, batch_of)(d), _mod_spec(MOD_SH1, batch_of)(d), const((1, d)),
                  const(w_in.shape), table, table, table, table_t, table_t, table_t,
                  const((nq, tm)), const((1, LANES)), const(seg_k.shape)],
        out_specs=[qt_spec, k_spec, vt_spec, qt_spec, k_spec, vt_spec,
                   pl.BlockSpec((1, 1, LANES), lambda i: (i, 0, 0))],
        out_shape=[qt_shape, k_shape, vt_shape, qt_shape, k_shape, vt_shape,
                   jax.ShapeDtypeStruct((t // tm, 1, LANES), F32)],
        compiler_params=_params("arbitrary"),
    )(x2, mod, mod, gpre, w_in, *tables, qn, kn, seg_k)


def _project_context(c2, mod, ctx_row, gpre, w_in, kn, seg_k, ctx_len):
    t, d = c2.shape
    ctx_mod = lambda i: ctx_row
    nq = (w_in.shape[1] - 4 * LANES) // 2
    kv = 2 * LANES
    assert nq % kv == 0
    group_kv = lambda g: pl.BlockSpec((d, kv), lambda i: (0, (g * (nq + kv) + nq) // kv))
    const = lambda shape: pl.BlockSpec(shape, lambda i: (0,) * len(shape))
    k_spec = pl.BlockSpec((ctx_len, LANES), lambda i: (i, 0))
    k_shape = jax.ShapeDtypeStruct((t, LANES), BF16)
    vt_spec = pl.BlockSpec((LANES, ctx_len), lambda i: (0, i))
    vt_shape = jax.ShapeDtypeStruct((LANES, t), BF16)
    return pl.pallas_call(
        _ctx_proj_kernel,
        grid=(t // ctx_len,),
        in_specs=[pl.BlockSpec((ctx_len, d), lambda i: (i, 0)),
                  _mod_spec(MOD_SC1, ctx_mod)(d), _mod_spec(MOD_SH1, ctx_mod)(d),
                  const((1, d)), group_kv(0), group_kv(1), const((1, LANES)), const(seg_k.shape)],
        out_specs=[k_spec, vt_spec, k_spec, vt_spec,
                   pl.BlockSpec((1, 1, LANES), lambda i: (i, 0, 0))],
        out_shape=[k_shape, vt_shape, k_shape, vt_shape,
                   jax.ShapeDtypeStruct((t // ctx_len, 1, LANES), F32)],
        compiler_params=_params("arbitrary"),
    )(c2, mod, mod, gpre, w_in, w_in, kn, seg_k)


def _attend_t(w, k, kc, vt, vct, bounded=False, bias=None, sink=None):
    st = _dot(k, w)
    sct = _dot(kc, w)
    if bias is not None:
        tq = bias.shape[1]
        st = jnp.concatenate([st[:, c * tq:(c + 1) * tq] + bias
                              for c in range(st.shape[1] // tq)], axis=1)
    if not bounded:
        shift = jnp.maximum(jnp.max(st, axis=0, keepdims=True),
                            jnp.max(sct, axis=0, keepdims=True))
        if sink is not None:
            shift = jnp.maximum(shift, sink)
            sink = sink - shift
        st, sct = st - shift, sct - shift
    pt = jnp.exp2(st)
    pct = jnp.exp2(sct)
    denom = jnp.sum(pt, axis=0, keepdims=True) + jnp.sum(pct, axis=0, keepdims=True)
    if sink is not None:
        denom = denom + jnp.exp2(sink)
    o2 = _dot(vt, pt.astype(BF16)) + _dot(vct, pct.astype(BF16))
    return o2, denom


def _all_heads_t(qt_ref, o_ref, attend):
    tq = qt_ref.shape[1]
    n_kv = LANES // HEAD_DIM
    zeros = jnp.zeros((HEAD_DIM, tq), BF16)
    outs = []
    for h in range(qt_ref.shape[0] // HEAD_DIM):
        g = h // KV_GROUP
        qh = qt_ref[h * HEAD_DIM:(h + 1) * HEAD_DIM, :]
        w = jnp.concatenate([zeros] * g + [qh] + [zeros] * (n_kv - 1 - g), axis=0)
        o2, denom = attend(h, w)
        outs.append(o2[g * HEAD_DIM:(g + 1) * HEAD_DIM, :] / denom)
    o_ref[...] = jnp.concatenate(outs, axis=0).T.astype(BF16)


def _all_heads_fused_t(qt_ref, o_ref, attend):
    tq = qt_ref.shape[1]
    n_heads = qt_ref.shape[0] // HEAD_DIM
    n_kv = LANES // HEAD_DIM
    rows = []
    for g in range(n_kv):
        heads = [qt_ref[h * HEAD_DIM:(h + 1) * HEAD_DIM, :] if h // KV_GROUP == g
                 else jnp.zeros((HEAD_DIM, tq), BF16) for h in range(n_heads)]
        rows.append(jnp.concatenate(heads, axis=1))
    w = jnp.concatenate(rows, axis=0)
    o2, denom = attend(0, w)
    o2 = o2 / denom
    outs = [o2[(h // KV_GROUP) * HEAD_DIM:(h // KV_GROUP + 1) * HEAD_DIM, h * tq:(h + 1) * tq]
            for h in range(n_heads)]
    o_ref[...] = jnp.concatenate(outs, axis=0).T.astype(BF16)


def _attn_b_kernel(bound_ref, qt_ref, k_ref, vt_ref, kc_ref, vct_ref, o_ref):
    k, kc, vt, vct = k_ref[...], kc_ref[...], vt_ref[...], vct_ref[...]
    bound = bound_ref[0]

    @pl.when(bound <= SAFE_SCORE_BOUND)
    def _():
        _all_heads_fused_t(qt_ref, o_ref, lambda h, w: _attend_t(w, k, kc, vt, vct, bounded=True))

    @pl.when(jnp.logical_not(bound <= SAFE_SCORE_BOUND))
    def _():
        _all_heads_t(qt_ref, o_ref, lambda h, w: _attend_t(w, k, kc, vt, vct))


def _attn_specs(nq, tq, seq, ctx_len):
    nqb = seq // tq
    return dict(
        qt=pl.BlockSpec((nq, tq), lambda b, i: (0, b * nqb + i)),
        k=pl.BlockSpec((seq, LANES), lambda b, i: (b, 0)),
        vt=pl.BlockSpec((LANES, seq), lambda b, i: (0, b)),
        kc=pl.BlockSpec((ctx_len, LANES), lambda b, i: (b, 0)),
        vct=pl.BlockSpec((LANES, ctx_len), lambda b, i: (0, b)),
        out=pl.BlockSpec((tq, nq), lambda b, i: (b * nqb + i, 0)))


def _attention_b(score_bound, qbt, kb, vbt, kcb, vcbt, batch, seq, ctx_len):
    nq, t = qbt.shape
    tq = ATTN_B_TQ
    sp = _attn_specs(nq, tq, seq, ctx_len)
    return pl.pallas_call(
        _attn_b_kernel,
        grid=(batch, seq // tq),
        in_specs=[pl.BlockSpec(memory_space=pltpu.SMEM),
                  sp["qt"], sp["k"], sp["vt"], sp["kc"], sp["vct"]],
        out_specs=sp["out"],
        out_shape=jax.ShapeDtypeStruct((t, nq), BF16),
        compiler_params=_params("arbitrary", "arbitrary"),
    )(score_bound, qbt, kb, vbt, kcb, vcbt)


def _attn_a_kernel(bound_ref, sink_ref, qt_ref, k_ref, vt_ref, kc_ref, vct_ref, bias_ref, o_ref,
                   *, seq):
    i = pl.program_id(1)
    tq = qt_ref.shape[1]
    n_heads = qt_ref.shape[0] // HEAD_DIM
    win = tq + 2 * WINDOW
    start = pl.multiple_of(jnp.clip(i * tq - WINDOW, 0, seq - win), WINDOW)
    k = k_ref[pl.ds(start, win), :]
    vt = vt_ref[:, pl.ds(start, win)]
    kc, vct = kc_ref[...], vct_ref[...]
    sinks = [sink_ref[h] * LOG2_E for h in range(n_heads)]

    bound = bound_ref[0]
    small = bound <= SAFE_SCORE_BOUND

    @pl.when(small)
    def _():
        sink_row = jnp.concatenate([jnp.full((1, tq), s, F32) for s in sinks], axis=1)
        _all_heads_fused_t(qt_ref, o_ref, lambda h, w: _attend_t(
            w, k, kc, vt, vct, bounded=True, bias=bias_ref[0], sink=sink_row))

    @pl.when(jnp.logical_not(small))
    def _():
        _all_heads_t(qt_ref, o_ref, lambda h, w: _attend_t(
            w, k, kc, vt, vct, bias=bias_ref[0], sink=sinks[h]))


def _band_bias(tq):
    win = tq + 2 * WINDOW
    r = np.arange(win)[:, None]
    j = np.arange(tq)[None, :]
    tables = [np.where(np.abs(off + r - j) <= WINDOW, 0.0, NEG_INF)
              for off in (0, -WINDOW, -2 * WINDOW)]
    return jnp.asarray(np.stack(tables), F32)


def _attention_a(score_bound, sink, qat, ka, vat, kca, vcat, batch, seq, ctx_len):
    nq, t = qat.shape
    tq = ATTN_A_TQ
    nqb = seq // tq
    win = tq + 2 * WINDOW
    assert nqb >= 2 and tq >= WINDOW
    sp = _attn_specs(nq, tq, seq, ctx_len)
    which = lambda b, i: (jnp.where(i == 0, 0, jnp.where(i == nqb - 1, 2, 1)), 0, 0)
    return pl.pallas_call(
        functools.partial(_attn_a_kernel, seq=seq),
        grid=(batch, nqb),
        in_specs=[pl.BlockSpec(memory_space=pltpu.SMEM), pl.BlockSpec(memory_space=pltpu.SMEM),
                  sp["qt"], sp["k"], sp["vt"], sp["kc"], sp["vct"],
                  pl.BlockSpec((1, win, tq), which)],
        out_specs=sp["out"],
        out_shape=jax.ShapeDtypeStruct((t, nq), BF16),
        compiler_params=_params("arbitrary", "arbitrary"),
    )(score_bound, sink, qat, ka, vat, kca, vcat, _band_bias(tq))


def _attn_ab_kernel(bound_a_ref, sink_ref, bound_b_ref,
                    qat_ref, ka_ref, vat_ref, kca_ref, vcat_ref, bias_ref,
                    qbt_ref, kb_ref, vbt_ref, kcb_ref, vcbt_ref, oa_ref, ob_ref, *, seq):
    _attn_a_kernel(bound_a_ref, sink_ref, qat_ref, ka_ref, vat_ref, kca_ref, vcat_ref, bias_ref,
                   oa_ref, seq=seq)
    _attn_b_kernel(bound_b_ref, qbt_ref, kb_ref, vbt_ref, kcb_ref, vcbt_ref, ob_ref)


def _attention_ab(bound_a, sink, bound_b, a_ops, b_ops, batch, seq, ctx_len):
    nq, t = a_ops[0].shape
    assert b_ops[0].shape == (nq, t) and ATTN_A_TQ == ATTN_B_TQ
    tq = ATTN_A_TQ
    nqb = seq // tq
    win = tq + 2 * WINDOW
    assert nqb >= 2 and tq >= WINDOW
    sp = _attn_specs(nq, tq, seq, ctx_len)
    group = [sp["qt"], sp["k"], sp["vt"], sp["kc"], sp["vct"]]
    which = lambda b, i: (jnp.where(i == 0, 0, jnp.where(i == nqb - 1, 2, 1)), 0, 0)
    smem = pl.BlockSpec(memory_space=pltpu.SMEM)
    return pl.pallas_call(
        functools.partial(_attn_ab_kernel, seq=seq),
        grid=(batch, nqb),
        in_specs=[smem, smem, smem] + group + [pl.BlockSpec((1, win, tq), which)] + group,
        out_specs=[sp["out"], sp["out"]],
        out_shape=[jax.ShapeDtypeStruct((t, nq), BF16)] * 2,
        compiler_params=_params("arbitrary", "arbitrary"),
    )(bound_a, sink, bound_b, *a_ops, _band_bias(tq), *b_ops)


def _out_kernel(oa_ref, ob_ref, x_ref, g1_ref, sc2_ref, sh2_ref, ga_ref, gb_ref, gpost_ref,
                gpre2_ref, woa_ref, wob_ref, wr_ref, br_ref,
                x1_ref, h2_ref, rinfo_ref, rt_ref, tcarry_ref, tcnt_ref, cnt_ref, carry_ref):
    step = pl.program_id(0)

    @pl.when(step == 0)
    def _():
        carry_ref[...] = jnp.zeros_like(carry_ref)

    na = _rms(oa_ref[...].astype(F32)) * ga_ref[...]
    nb = _rms(ob_ref[...].astype(F32)) * gb_ref[...]
    ox = _dot(na.astype(BF16), woa_ref[...]) + _dot(nb.astype(BF16), wob_ref[...])
    x1 = x_ref[...] + g1_ref[0, 0] * (_rms(ox) * gpost_ref[...])
    x1_ref[...] = x1
    h2 = _rms(x1) * gpre2_ref[...] * (1.0 + sc2_ref[0, 0]) + sh2_ref[0, 0]
    h_hi, h_lo = _split_bf16(h2)
    h2_ref[...] = h_hi

    both = _dot(h_hi, wr_ref[...])
    logits = (both[:, :LANES] + _dot(h_lo, wr_ref[:, :LANES]) + both[:, LANES:]
              + br_ref[...])
    tm = logits.shape[0]
    lt = logits.T
    row = lax.broadcasted_iota(jnp.int32, lt.shape, 0)
    rowf = row.astype(F32)
    big = jnp.float32(1e9)
    ninf = jnp.float32(-jnp.inf)
    colmax = lambda v: jnp.max(v, axis=0, keepdims=True)
    colmin = lambda v: jnp.min(v, axis=0, keepdims=True)
    colsum = lambda v: jnp.sum(v, axis=0, keepdims=True)

    gmask = (row >= N_EXPERTS) & (row < N_EXPERTS + N_GROUPS)
    lg = jnp.where(gmask, lt, ninf)
    gmax = colmax(lg)
    gidx = colmin(jnp.where(lg == gmax, rowf, big)) - N_EXPERTS
    g_w = 1.0 / colsum(jnp.exp(lg - gmax))
    row_group = (row // EXPERTS_PER_GROUP).astype(F32)
    emask = (row < N_EXPERTS) & (row_group == gidx)
    le = jnp.where(emask, lt, ninf)
    m1 = colmax(le)
    i1 = colmin(jnp.where(le == m1, rowf, big))
    le2 = jnp.where(rowf == i1, ninf, le)
    m2 = colmax(le2)
    i2 = colmin(jnp.where(le2 == m2, rowf, big))
    e2 = jnp.exp(m2 - m1)
    w0 = g_w / (1.0 + e2)
    w1 = g_w * e2 / (1.0 + e2)

    hit1 = rowf == i1
    hit2 = rowf == i2
    onehot = jnp.where(hit1, 1.0, jnp.where(hit2, 1.0, 0.0)).astype(F32)
    r = lax.broadcasted_iota(jnp.int32, (tm, tm), 0)
    c = lax.broadcasted_iota(jnp.int32, (tm, tm), 1)
    earlier = jnp.where(r < c, 1.0, 0.0).astype(BF16)
    within = _dot(onehot.astype(BF16), earlier)
    tile_cnt = jnp.broadcast_to(jnp.sum(onehot, axis=1, keepdims=True), (LANES, LANES))
    er = lax.broadcasted_iota(jnp.int32, (LANES, LANES), 0)
    ec = lax.broadcasted_iota(jnp.int32, (LANES, LANES), 1)
    below = jnp.where(er > ec, 1.0, 0.0).astype(BF16)
    cnt_hi = jnp.floor(tile_cnt * (1.0 / 32.0))
    cnt_lo = tile_cnt - 32.0 * cnt_hi
    run_start = 32.0 * _dot(below, cnt_hi.astype(BF16)) + _dot(below, cnt_lo.astype(BF16))
    local = within + run_start[:, 0:1]
    pos0 = colsum(jnp.where(hit1, local, 0.0))
    pos1 = colsum(jnp.where(hit2, local, 0.0))
    cnt_row = tile_cnt.T[0:1, :]
    tcarry_ref[0] = carry_ref[...]
    tcnt_ref[0] = cnt_row
    carry_ref[...] += cnt_row
    cnt_ref[...] = carry_ref[...]

    fields = jnp.concatenate([i1, i2, pos0, pos1, w0, w1, jnp.zeros((2, tm), F32)], axis=0)
    rt_ref[...] = fields
    rinfo_ref[...] = jnp.concatenate(
        [fields, jnp.zeros((LANES - 8, tm), F32)], axis=0).T


def _out_and_route(oa, ob, x2, mod, ga, gb, gpost, gpre2, w_out, wr, br, seq):
    t, d = x2.shape
    tm = OUT_TM
    tpb = seq // tm
    nq = oa.shape[1]
    const = lambda shape: pl.BlockSpec(shape, lambda i: (0,) * len(shape))
    batch_of = lambda i: i // tpb
    rows = lambda n: pl.BlockSpec((tm, n), lambda i: (i, 0))
    per_tile = pl.BlockSpec((1, 1, LANES), lambda i: (i, 0, 0))
    w_half = lambda g: pl.BlockSpec((nq, d), lambda i: (g, 0))
    return pl.pallas_call(
        _out_kernel,
        grid=(t // tm,),
        in_specs=[rows(nq), rows(nq), rows(d),
                  _mod_spec(MOD_G1, batch_of)(d), _mod_spec(MOD_SC2, batch_of)(d),
                  _mod_spec(MOD_SH2, batch_of)(d),
                  const((1, nq)), const((1, nq)), const((1, d)), const((1, d)),
                  w_half(0), w_half(1), const(wr.shape),
                  const((1, LANES))],
        out_specs=[rows(d), rows(d), rows(LANES), pl.BlockSpec((8, tm), lambda i: (0, i)),
                   per_tile, per_tile, const((1, LANES))],
        out_shape=[jax.ShapeDtypeStruct((t, d), F32), jax.ShapeDtypeStruct((t, d), BF16),
                   jax.ShapeDtypeStruct((t, LANES), F32), jax.ShapeDtypeStruct((8, t), F32),
                   jax.ShapeDtypeStruct((t // tm, 1, LANES), F32),
                   jax.ShapeDtypeStruct((t // tm, 1, LANES), F32),
                   jax.ShapeDtypeStruct((1, LANES), F32)],
        scratch_shapes=[pltpu.VMEM((1, LANES), F32)],
        compiler_params=_params("arbitrary"),
    )(oa, ob, x2, mod, mod, mod, ga, gb, gpost, gpre2, w_out, w_out, wr, br)


PACK_ROWS = 8
ROW_DTYPE = F32


def _pack_rows(ref, x):
    n = x.shape[0]
    for c in range(PACK_ROWS):
        ref[pl.ds(c, n, stride=PACK_ROWS), :] = x[:, c * LANES:(c + 1) * LANES]


def _unpack_rows(ref):
    n = ref.shape[0] // PACK_ROWS
    return jnp.concatenate(
        [ref[pl.ds(c, n, stride=PACK_ROWS), :].astype(BF16) for c in range(PACK_ROWS)], axis=1)


def _for_each_run_piece(rdst_ref, rlen_ref, rmax_ref, tile, max_len, fn):
    def copy_runs(n_bits):
        def run(e, local):
            length = rlen_ref[tile * N_EXPERTS + e]
            dst = rdst_ref[tile * N_EXPERTS + e]
            for b in range(n_bits):
                size = 1 << b

                @pl.when(((length >> b) & 1) == 1)
                def _():
                    done = length & (size - 1)
                    fn(local + done, dst + done, size)
            return local + length

        lax.fori_loop(0, N_EXPERTS, run, 0)

    all_bits = max_len.bit_length()
    low_bits = min(all_bits, (4 * max_len // N_EXPERTS).bit_length())
    short = rmax_ref[tile] < (1 << low_bits)

    @pl.when(short)
    def _():
        copy_runs(low_bits)

    @pl.when(jnp.logical_not(short))
    def _():
        copy_runs(all_bits)


def _token_rows(ref, row0, n_rows):
    start = row0 * PACK_ROWS
    if not isinstance(start, int):
        start = pl.multiple_of(start, PACK_ROWS)
    return ref.at[pl.ds(start, n_rows * PACK_ROWS)]


def _dispatch_kernel(rdst_ref, rlen_ref, rmax_ref, h_ref, rt_ref, xs_ref, sorted_ref, sem):
    k = pl.program_id(0)
    nk = pl.num_programs(0)
    tm = h_ref.shape[0]
    rows = 2 * tm
    slot = k % 2

    def wait_slot(s):
        pltpu.make_async_copy(sorted_ref.at[s], _token_rows(xs_ref, 0, rows), sem.at[s]).wait()

    @pl.when(k >= 2)
    def _():
        wait_slot(slot)

    pos0 = rt_ref[2:3, :]
    pos1 = rt_ref[3:4, :]
    r = lax.broadcasted_iota(jnp.int32, (rows, tm), 0).astype(F32)
    perm = jnp.where((r == pos0) | (r == pos1), 1.0, 0.0).astype(BF16)
    srt = _dot(perm, h_ref[...].astype(BF16))
    buf = sorted_ref.at[slot]
    _pack_rows(buf, srt)

    def copy_piece(local, dst, size):
        pltpu.make_async_copy(_token_rows(buf, local, size), _token_rows(xs_ref, dst, size),
                              sem.at[slot]).start()

    _for_each_run_piece(rdst_ref, rlen_ref, rmax_ref, k, tm, copy_piece)

    @pl.when(k == nk - 1)
    def _():
        wait_slot(slot)

        @pl.when(nk >= 2)
        def _():
            wait_slot(1 - slot)


def _dispatch(run_dst, run_len, run_max, h2, rt):
    t, d = h2.shape
    assert d == PACK_ROWS * LANES
    tm = MOVE_TM
    return pl.pallas_call(
        _dispatch_kernel,
        grid_spec=pltpu.PrefetchScalarGridSpec(
            num_scalar_prefetch=3,
            grid=(t // tm,),
            in_specs=[pl.BlockSpec((tm, d), lambda i, *_: (i, 0)),
                      pl.BlockSpec((8, tm), lambda i, *_: (0, i))],
            out_specs=pl.BlockSpec(memory_space=pl.ANY),
            scratch_shapes=[pltpu.VMEM((2, 2 * tm * PACK_ROWS, LANES), ROW_DTYPE),
                            pltpu.SemaphoreType.DMA((2,))]),
        out_shape=jax.ShapeDtypeStruct((2 * t * PACK_ROWS, LANES), ROW_DTYPE),
        compiler_params=_params("arbitrary"),
    )(run_dst, run_len, run_max, h2, rt)


def _expert_kernel(vt_ref, ve_ref, va_ref, vb_ref, vn_ref, nv_ref, xs_ref, wg_hbm, wu_hbm, wd_hbm,
                   ys_ref, wg_bf, wu_bf, wd_bf, wg_f32, wu_f32, wd_f32, wsem):
    v = pl.program_id(0)
    valid = v < nv_ref[0]
    prev = jnp.maximum(v - 1, 0)
    new_expert = (v == 0) | (ve_ref[v] != ve_ref[prev])
    new_tile = (v == 0) | (vt_ref[v] != vt_ref[prev])

    def weight_copies(e):
        return [pltpu.make_async_copy(src.at[e], dst, wsem)
                for src, dst in ((wg_hbm, wg_f32), (wu_hbm, wu_f32), (wd_hbm, wd_f32))]

    @pl.when(v == 0)
    def _():
        for cp in weight_copies(ve_ref[0]):
            cp.start()

    @pl.when(valid & new_expert)
    def _():
        for cp in weight_copies(ve_ref[v]):
            cp.wait()
        wg_bf[...] = wg_f32[...].astype(BF16)
        wu_bf[...] = wu_f32[...].astype(BF16)
        wd_bf[...] = wd_f32[...].astype(BF16)

        @pl.when(vn_ref[v] >= 0)
        def _():
            for cp in weight_copies(vn_ref[v]):
                cp.start()

    def expert_rows():
        xb = _unpack_rows(xs_ref)
        gate = _dot(xb, wg_bf[...])
        up = _dot(xb, wu_bf[...])
        act = gate * jax.nn.sigmoid(gate) * up
        return _dot(act.astype(BF16), wd_bf[...])

    @pl.when(valid & new_tile)
    def _():
        _pack_rows(ys_ref, expert_rows())

    @pl.when(valid & jnp.logical_not(new_tile))
    def _():
        y = expert_rows()
        te = y.shape[0]
        row = lax.broadcasted_iota(jnp.int32, (te, 1), 0)
        mine = (row >= va_ref[v]) & (row < vb_ref[v])
        for c in range(PACK_ROWS):
            rows = pl.ds(c, te, stride=PACK_ROWS)
            ys_ref[rows, :] = jnp.where(mine, y[:, c * LANES:(c + 1) * LANES], ys_ref[rows, :])


def _expert_mlp(visit_tile, visit_expert, visit_lo, visit_hi, visit_next, n_visits, xs,
                w_gate, w_up, w_down):
    te = EXPERT_TE
    d, ff = w_gate.shape[1:]
    blk = (te * PACK_ROWS, LANES)
    tile = lambda v, vt, *_: (vt[v], 0)
    hbm = pl.BlockSpec(memory_space=pl.ANY)
    return pl.pallas_call(
        _expert_kernel,
        grid_spec=pltpu.PrefetchScalarGridSpec(
            num_scalar_prefetch=6,
            grid=(visit_tile.shape[0],),
            in_specs=[pl.BlockSpec(blk, tile), hbm, hbm, hbm],
            out_specs=pl.BlockSpec(blk, tile),
            scratch_shapes=[pltpu.VMEM((d, ff), BF16), pltpu.VMEM((d, ff), BF16),
                            pltpu.VMEM((ff, d), BF16),
                            pltpu.VMEM((d, ff), F32), pltpu.VMEM((d, ff), F32),
                            pltpu.VMEM((ff, d), F32), pltpu.SemaphoreType.DMA(())]),
        out_shape=jax.ShapeDtypeStruct(xs.shape, ROW_DTYPE),
        compiler_params=_params("arbitrary"),
    )(visit_tile, visit_expert, visit_lo, visit_hi, visit_next, n_visits, xs,
      w_gate, w_up, w_down)


def _combine_kernel(rdst_ref, rlen_ref, rmax_ref, x1_ref, rinfo_ref, g2_ref, gpost_ref, ys_ref, o_ref,
                    gath_ref, sem):
    k = pl.program_id(0)
    nk = pl.num_programs(0)
    tm = x1_ref.shape[0]
    rows = 2 * tm
    slot = k % 2

    def gather_runs(tile, s):
        buf = gath_ref.at[s]

        def copy_piece(local, src, size):
            pltpu.make_async_copy(_token_rows(ys_ref, src, size), _token_rows(buf, local, size),
                                  sem.at[s]).start()

        _for_each_run_piece(rdst_ref, rlen_ref, rmax_ref, tile, tm, copy_piece)

    @pl.when(k == 0)
    def _():
        gather_runs(0, 0)

    @pl.when(k + 1 < nk)
    def _():
        gather_runs(k + 1, 1 - slot)

    buf = gath_ref.at[slot]
    pltpu.make_async_copy(_token_rows(ys_ref, 0, rows), buf, sem.at[slot]).wait()
    g = _unpack_rows(buf)
    info = rinfo_ref[...]
    col = lax.broadcasted_iota(jnp.int32, (tm, rows), 1).astype(F32)
    pick = jnp.where(col == info[:, 2:3], info[:, 4:5],
                     jnp.where(col == info[:, 3:4], info[:, 5:6], 0.0)).astype(BF16)
    fx = _dot(pick, g)
    o_ref[...] = x1_ref[...] + g2_ref[0, 0] * (_rms(fx) * gpost_ref[...])


def _combine(run_dst, run_len, run_max, x1, rinfo, mod, gpost, ys, seq):
    t, d = x1.shape
    tm = MOVE_TM
    tpb = seq // tm
    batch_of = lambda i: i // tpb
    return pl.pallas_call(
        _combine_kernel,
        grid_spec=pltpu.PrefetchScalarGridSpec(
            num_scalar_prefetch=3,
            grid=(t // tm,),
            in_specs=[pl.BlockSpec((tm, d), lambda i, *_: (i, 0)),
                      pl.BlockSpec((tm, LANES), lambda i, *_: (i, 0)),
                      _mod_spec(MOD_G2, batch_of)(d),
                      pl.BlockSpec((1, d), lambda i, *_: (0, 0)),
                      pl.BlockSpec(memory_space=pl.ANY)],
            out_specs=pl.BlockSpec((tm, d), lambda i, *_: (i, 0)),
            scratch_shapes=[pltpu.VMEM((2, 2 * tm * PACK_ROWS, LANES), ROW_DTYPE),
                            pltpu.SemaphoreType.DMA((2,))]),
        out_shape=jax.ShapeDtypeStruct((t, d), F32),
        compiler_params=_params("arbitrary"),
    )(run_dst, run_len, run_max, x1, rinfo, mod, gpost, ys)


def _rope_tables(seq):
    pos = np.arange(seq)
    row = (pos // GRID_W).astype(np.float32)
    col = (pos % GRID_W).astype(np.float32)
    axis_dim = HEAD_DIM // 2
    inv_freq = (ROPE_THETA ** (-np.arange(0, axis_dim, 2, dtype=np.float32) / axis_dim)).astype(
        np.float32)
    ang = np.concatenate([row[:, None] * inv_freq, col[:, None] * inv_freq], axis=-1)
    pair = (np.arange(LANES) % HEAD_DIM) // 2
    cos = np.cos(ang)[:, pair]
    sin = np.sin(ang)[:, pair]
    even = (np.arange(LANES) % 2) == 0
    tables = (cos, np.where(even, -sin, 0.0), np.where(even, 0.0, sin))
    tables = tables + tuple(tb.T for tb in tables)
    return tuple(jnp.asarray(tb, F32) for tb in tables)


def _segment_ones(n):
    seg = np.arange(n) // HEAD_DIM
    return jnp.asarray(seg[:, None] == seg[None, :], BF16)


def kernel(x, c, ctx, c_ctx, w_mod, b_mod, attn_pre_norm, attn_post_norm, w_in, a_sink,
           b_q_norm, b_k_norm, a_out_norm, b_out_norm, w_out, ffn_pre_norm, ffn_post_norm,
           w_group, b_group, w_router, b_router, w_gate, w_up, w_down):
    batch, seq, d = x.shape
    ctx_len = ctx.shape[1]
    assert w_mod.shape[0] == 1, "single-layer stack only (context stream is never updated)"
    assert seq % ATTN_A_TQ == 0 and seq >= ATTN_A_TQ + 2 * WINDOW
    assert seq % PROJ_TM == 0 and seq % ATTN_B_TQ == 0 and seq % OUT_TM == 0 and seq % MOVE_TM == 0
    t = batch * seq
    nq = d // 2
    nkv = nq // KV_GROUP
    assert nkv == LANES and w_in.shape[2] == 2 * nq + 4 * nkv

    assert batch + 1 <= MOD_ROWS
    cc = jnp.concatenate([c, c_ctx[None, :], jnp.zeros((MOD_ROWS - batch - 1, d), F32)], axis=0)
    mod = _modulation(cc, w_mod[0], b_mod[0]).reshape(cc.shape[0], 6, 1, d)

    x2 = x.reshape(t, d)
    c2 = ctx.reshape(batch * ctx_len, d)
    gpre = attn_pre_norm[0].reshape(1, d)
    w_in_bf = w_in[0].astype(BF16)
    qn = jnp.broadcast_to(jnp.tile(b_q_norm[0], nq // HEAD_DIM)[:, None], (nq, PROJ_TM))
    kn = jnp.tile(b_k_norm[0], nkv // HEAD_DIM).reshape(1, nkv)
    seg_k = _segment_ones(nkv)
    qat, ka, vat, qbt, kb, vbt, stats = _project_latents(
        x2, mod, gpre, w_in_bf, _rope_tables(seq), qn, kn, seg_k, seq)
    kca, vcat, kcb, vcbt, ctx_stats = _project_context(
        c2, mod, batch, gpre, w_in_bf, kn, seg_k, ctx_len)

    q_sq = jnp.max(stats[:, 0, 0])
    k_sq = jnp.maximum(jnp.max(stats[:, 0, 1]), jnp.max(ctx_stats[:, 0, 1]))
    bound_a = jnp.maximum(1.01 * jnp.sqrt(q_sq * k_sq), jnp.max(a_sink[0]) * LOG2_E).reshape(1)
    score_bound = (1.01 * HEAD_DIM ** 0.5 * LOG2_E
                   * jnp.max(jnp.abs(b_q_norm[0])) * jnp.max(jnp.abs(b_k_norm[0]))).reshape(1)
    oa, ob = _attention_ab(bound_a, a_sink[0], score_bound, (qat, ka, vat, kca, vcat),
                           (qbt, kb, vbt, kcb, vcbt), batch, seq, ctx_len)

    w_out_bf = w_out[0].astype(BF16)
    lane_pad = LANES - N_EXPERTS - N_GROUPS
    w_r = jnp.pad(jnp.concatenate([w_router[0], w_group[0]], axis=1), ((0, 0), (0, lane_pad)))
    w_r_hi = w_r.astype(BF16)
    w_r_lo = (w_r - w_r_hi.astype(F32)).astype(BF16)
    w_r2 = jnp.concatenate([w_r_hi, w_r_lo], axis=1)
    b_r = jnp.pad(jnp.concatenate([b_router[0], b_group[0]]), (0, lane_pad)).reshape(1, LANES)
    x1, h2, rinfo, rt, tcarry, tcnt, counts = _out_and_route(
        oa, ob, x2, mod, a_out_norm[0].reshape(1, nq), b_out_norm[0].reshape(1, nq),
        attn_post_norm[0].reshape(1, d), ffn_pre_norm[0].reshape(1, d),
        w_out_bf, w_r2, b_r, seq)

    te = EXPERT_TE
    assert (2 * t) % te == 0
    cnt = counts[0, :N_EXPERTS].astype(jnp.int32)
    ends = jnp.cumsum(cnt)
    starts = ends - cnt
    run_dst = (starts[None, :] + tcarry[:, 0, :N_EXPERTS].astype(jnp.int32)).reshape(-1)
    tile_runs = tcnt[:, 0, :N_EXPERTS].astype(jnp.int32)
    run_len = tile_runs.reshape(-1)
    run_max = jnp.max(tile_runs, axis=1)
    first_tile = starts // te
    n_vis = jnp.where(cnt > 0, (ends - 1) // te - first_tile + 1, 0)
    vis_end = jnp.cumsum(n_vis)
    n_visits = vis_end[-1]
    v = jnp.minimum(jnp.arange(2 * t // te + N_EXPERTS, dtype=jnp.int32), n_visits - 1)
    v_expert = jnp.sum(vis_end[None, :] <= v[:, None], axis=1).astype(jnp.int32)
    pick = (v_expert[:, None] == jnp.arange(N_EXPERTS)[None, :]).astype(jnp.int32)
    of_expert = lambda table: jnp.sum(pick * table[None, :], axis=1)
    v_tile = of_expert(first_tile) + v - of_expert(vis_end - n_vis)
    v_lo = jnp.maximum(of_expert(starts) - v_tile * te, 0)
    v_hi = jnp.minimum(of_expert(ends) - v_tile * te, te)
    ids = jnp.arange(N_EXPERTS, dtype=jnp.int32)
    later = (ids[None, :] > ids[:, None]) & (cnt[None, :] > 0)
    next_expert = jnp.min(jnp.where(later, ids[None, :], N_EXPERTS), axis=1)
    v_next = of_expert(jnp.where(next_expert < N_EXPERTS, next_expert, -1))

    xs = _dispatch(run_dst, run_len, run_max, h2, rt)
    ys = _expert_mlp(v_tile, v_expert, v_lo, v_hi, v_next, n_visits.reshape(1), xs,
                     w_gate[0], w_up[0], w_down[0])
    out = _combine(run_dst, run_len, run_max, x1, rinfo, mod, ffn_post_norm[0].reshape(1, d), ys,
                   seq)
    return out.reshape(batch, seq, d)
```

```python
import functools

import jax
import jax.numpy as jnp
import numpy as np
from jax import lax
from jax.experimental import pallas as pl
from jax.experimental.pallas import tpu as pltpu

F32 = jnp.float32
BF16 = jnp.bfloat16

GRID_W = 64
HEAD_DIM = 64
KV_GROUP = 4
WINDOW = 128
ROPE_THETA = 10000.0
N_GROUPS = 4
EXPERTS_PER_GROUP = 8
N_EXPERTS = N_GROUPS * EXPERTS_PER_GROUP
EPS = 1e-6
NEG_INF = -1e30
LOG2_E = 1.4426950408889634
SAFE_SCORE_BOUND = 40.0

LANES = 128
V7X_VMEM_LIMIT = 56 * 1024 * 1024

MOD_ROWS = 16
MOD_BN = 1024
PROJ_TM = 512
ATTN_A_TQ = 256
ATTN_B_TQ = 256
OUT_TM = 512
EXPERT_TE = 512
MOVE_TM = OUT_TM


def _params(*sem):
    return pltpu.CompilerParams(dimension_semantics=sem, vmem_limit_bytes=V7X_VMEM_LIMIT)


def _dot(a, b):
    return jnp.dot(a, b, preferred_element_type=F32)


def _rms(x):
    return x * lax.rsqrt(jnp.mean(x * x, axis=-1, keepdims=True) + EPS)


def _split_bf16(x):
    hi = x.astype(BF16)
    lo = (x - hi.astype(F32)).astype(BF16)
    return hi, lo


def _mod_kernel(c_ref, w_ref, b_ref, o_ref):
    cc = c_ref[...]
    s = cc * jax.nn.sigmoid(cc)
    s_hi, s_lo = _split_bf16(s)
    w_hi, w_lo = _split_bf16(w_ref[...])
    o_ref[...] = _dot(s_hi, w_hi) + _dot(s_lo, w_hi) + _dot(s_hi, w_lo) + b_ref[...]


def _modulation(cc, w_mod, b_mod):
    rows, d = cc.shape
    n = w_mod.shape[1]
    bn = MOD_BN
    return pl.pallas_call(
        _mod_kernel,
        grid=(n // bn,),
        in_specs=[pl.BlockSpec((rows, d), lambda i: (0, 0)),
                  pl.BlockSpec((d, bn), lambda i: (0, i)),
                  pl.BlockSpec((1, bn), lambda i: (0, i))],
        out_specs=pl.BlockSpec((rows, bn), lambda i: (0, i)),
        out_shape=jax.ShapeDtypeStruct((rows, n), F32),
        compiler_params=_params("arbitrary"),
    )(cc, w_mod, b_mod.reshape(1, n))


def _rope(x, cos, sin_a, sin_b):
    return x * cos + pltpu.roll(x, LANES - 1, 1) * sin_a + pltpu.roll(x, 1, 1) * sin_b


def _head_norm(x, seg_ref, gain):
    ss = _dot((x * x).astype(BF16), seg_ref[...])
    return x * lax.rsqrt(ss * (1.0 / HEAD_DIM) + EPS) * gain


def _max_head_sq_norm(x, seg_ref):
    ss = _dot((x * x).astype(BF16), seg_ref[...])
    return jnp.max(jnp.max(ss, axis=1, keepdims=True), axis=0, keepdims=True)


def _norm_stats(q_sq, k_sq):
    lane = lax.broadcasted_iota(jnp.int32, (1, LANES), 1)
    zero = jnp.zeros((1, LANES), F32)
    return jnp.where(lane == 0, q_sq, zero) + jnp.where(lane == 1, k_sq, zero)


def _rope_t(xt, cos_t, sin_a_t, sin_b_t):
    return (xt * cos_t + pltpu.roll(xt, LANES - 1, 0) * sin_a_t
            + pltpu.roll(xt, 1, 0) * sin_b_t)


def _proj_kernel(x_ref, sc_ref, sh_ref, gpre_ref, w_ref, cos_ref, sa_ref, sb_ref,
                 cos_t_ref, sa_t_ref, sb_t_ref, qn_ref, kn_ref, seg_k_ref,
                 qat_ref, ka_ref, vat_ref, qbt_ref, kb_ref, vbt_ref, stats_ref):
    h = _rms(x_ref[...]) * gpre_ref[...] * (1.0 + sc_ref[0, 0]) + sh_ref[0, 0]
    p = _dot(h.astype(BF16), w_ref[...])
    cos, sa, sb = cos_ref[...], sa_ref[...], sb_ref[...]
    q_scale = HEAD_DIM ** -0.5 * LOG2_E
    cos_t, sa_t, sb_t = cos_t_ref[...] * q_scale, sa_t_ref[...] * q_scale, sb_t_ref[...] * q_scale
    nq = qat_ref.shape[0]
    q_sq = None
    for c in range(nq // LANES):
        xt = p[:, c * LANES:(c + 1) * LANES].T
        qat_ref[c * LANES:(c + 1) * LANES, :] = _rope_t(xt, cos_t, sa_t, sb_t).astype(BF16)
        sq = xt * xt
        for head in range(LANES // HEAD_DIM):
            norm = jnp.sum(sq[head * HEAD_DIM:(head + 1) * HEAD_DIM, :], axis=0, keepdims=True)
            q_sq = norm if q_sq is None else jnp.maximum(q_sq, norm)
    q_sq = jnp.max(q_sq, axis=1, keepdims=True) * (q_scale * q_scale)
    o = nq
    ka_ref[...] = _rope(p[:, o:o + LANES], cos, sa, sb).astype(BF16)
    vat_ref[...] = p[:, o + LANES:o + 2 * LANES].T.astype(BF16)
    stats_ref[0] = _norm_stats(q_sq, _max_head_sq_norm(p[:, o:o + LANES], seg_k_ref))
    o += 2 * LANES
    for c in range(nq // LANES):
        xt = p[:, o + c * LANES:o + (c + 1) * LANES].T
        sq = xt * xt
        halves = []
        for head in range(LANES // HEAD_DIM):
            rows = slice(head * HEAD_DIM, (head + 1) * HEAD_DIM)
            ms = jnp.sum(sq[rows, :], axis=0, keepdims=True) * (1.0 / HEAD_DIM)
            halves.append(xt[rows, :] * lax.rsqrt(ms + EPS))
        qn = jnp.concatenate(halves, axis=0) * qn_ref[c * LANES:(c + 1) * LANES, :]
        qbt_ref[c * LANES:(c + 1) * LANES, :] = _rope_t(qn, cos_t, sa_t, sb_t).astype(BF16)
    o += nq
    kb = _head_norm(p[:, o:o + LANES], seg_k_ref, kn_ref[...])
    kb_ref[...] = _rope(kb, cos, sa, sb).astype(BF16)
    vbt_ref[...] = p[:, o + LANES:o + 2 * LANES].T.astype(BF16)


def _ctx_proj_kernel(x_ref, sc_ref, sh_ref, gpre_ref, wa_ref, wb_ref, kn_ref, seg_k_ref,
                     ka_ref, vat_ref, kb_ref, vbt_ref, stats_ref):
    h = (_rms(x_ref[...]) * gpre_ref[...] * (1.0 + sc_ref[0, 0]) + sh_ref[0, 0]).astype(BF16)
    pa = _dot(h, wa_ref[...])
    pb = _dot(h, wb_ref[...])
    stats_ref[0] = _norm_stats(0.0, _max_head_sq_norm(pa[:, 0:LANES], seg_k_ref))
    ka_ref[...] = pa[:, 0:LANES].astype(BF16)
    vat_ref[...] = pa[:, LANES:2 * LANES].T.astype(BF16)
    kb_ref[...] = _head_norm(pb[:, 0:LANES], seg_k_ref, kn_ref[...]).astype(BF16)
    vbt_ref[...] = pb[:, LANES:2 * LANES].T.astype(BF16)


def _mod_spec(chunk, row_of_step):
    return lambda d: pl.BlockSpec((1, 1, 1, d), lambda i, *_: (row_of_step(i), chunk, 0, 0))


MOD_SH1, MOD_SC1, MOD_G1, MOD_SH2, MOD_SC2, MOD_G2 = range(6)


def _project_latents(x2, mod, gpre, w_in, tables, qn, kn, seg_k, seq):
    t, d = x2.shape
    tm = PROJ_TM
    tpb = seq // tm
    nq = qn.shape[0]
    const = lambda shape: pl.BlockSpec(shape, lambda i: (0,) * len(shape))
    batch_of = lambda i: i // tpb
    table = pl.BlockSpec((tm, LANES), lambda i: (i % tpb, 0))
    table_t = pl.BlockSpec((LANES, tm), lambda i: (0, i % tpb))
    k_spec = pl.BlockSpec((tm, LANES), lambda i: (i, 0))
    k_shape = jax.ShapeDtypeStruct((t, LANES), BF16)
    vt_spec = pl.BlockSpec((LANES, tm), lambda i: (0, i))
    vt_shape = jax.ShapeDtypeStruct((LANES, t), BF16)
    qt_spec = pl.BlockSpec((nq, tm), lambda i: (0, i))
    qt_shape = jax.ShapeDtypeStruct((nq, t), BF16)
    return pl.pallas_call(
        _proj_kernel,
        grid=(t // tm,),
        in_specs=[pl.BlockSpec((tm, d), lambda i: (i, 0)),
                  _mod_spec(MOD_SC1, batch_of)(d), _mod_spec(MOD_SH1, batch_of)(d), const((1, d)),
                  const(w_in.shape), table, table, table, table_t, table_t, table_t,
                  const((nq, tm)), const((1, LANES)), const(seg_k.shape)],
        out_specs=[qt_spec, k_spec, vt_spec, qt_spec, k_spec, vt_spec,
                   pl.BlockSpec((1, 1, LANES), lambda i: (i, 0, 0))],
        out_shape=[qt_shape, k_shape, vt_shape, qt_shape, k_shape, vt_shape,
                   jax.ShapeDtypeStruct((t // tm, 1, LANES), F32)],
        compiler_params=_params("arbitrary"),
    )(x2, mod, mod, gpre, w_in, *tables, qn, kn, seg_k)


def _project_context(c2, mod, ctx_row, gpre, w_in, kn, seg_k, ctx_len):
    t, d = c2.shape
    ctx_mod = lambda i: ctx_row
    nq = (w_in.shape[1] - 4 * LANES) // 2
    kv = 2 * LANES
    assert nq % kv == 0
    group_kv = lambda g: pl.BlockSpec((d, kv), lambda i: (0, (g * (nq + kv) + nq) // kv))
    const = lambda shape: pl.BlockSpec(shape, lambda i: (0,) * len(shape))
    k_spec = pl.BlockSpec((ctx_len, LANES), lambda i: (i, 0))
    k_shape = jax.ShapeDtypeStruct((t, LANES), BF16)
    vt_spec = pl.BlockSpec((LANES, ctx_len), lambda i: (0, i))
    vt_shape = jax.ShapeDtypeStruct((LANES, t), BF16)
    return pl.pallas_call(
        _ctx_proj_kernel,
        grid=(t // ctx_len,),
        in_specs=[pl.BlockSpec((ctx_len, d), lambda i: (i, 0)),
                  _mod_spec(MOD_SC1, ctx_mod)(d), _mod_spec(MOD_SH1, ctx_mod)(d),
                  const((1, d)), group_kv(0), group_kv(1), const((1, LANES)), const(seg_k.shape)],
        out_specs=[k_spec, vt_spec, k_spec, vt_spec,
                   pl.BlockSpec((1, 1, LANES), lambda i: (i, 0, 0))],
        out_shape=[k_shape, vt_shape, k_shape, vt_shape,
                   jax.ShapeDtypeStruct((t // ctx_len, 1, LANES), F32)],
        compiler_params=_params("arbitrary"),
    )(c2, mod, mod, gpre, w_in, w_in, kn, seg_k)


def _attend_t(w, k, kc, vt, vct, bounded=False, bias=None, sink=None):
    st = _dot(k, w)
    sct = _dot(kc, w)
    if bias is not None:
        tq = bias.shape[1]
        st = jnp.concatenate([st[:, c * tq:(c + 1) * tq] + bias
                              for c in range(st.shape[1] // tq)], axis=1)
    if not bounded:
        shift = jnp.maximum(jnp.max(st, axis=0, keepdims=True),
                            jnp.max(sct, axis=0, keepdims=True))
        if sink is not None:
            shift = jnp.maximum(shift, sink)
            sink = sink - shift
        st, sct = st - shift, sct - shift
    pt = jnp.exp2(st)
    pct = jnp.exp2(sct)
    denom = jnp.sum(pt, axis=0, keepdims=True) + jnp.sum(pct, axis=0, keepdims=True)
    if sink is not None:
        denom = denom + jnp.exp2(sink)
    o2 = _dot(vt, pt.astype(BF16)) + _dot(vct, pct.astype(BF16))
    return o2, denom


def _all_heads_t(qt_ref, o_ref, attend):
    tq = qt_ref.shape[1]
    n_kv = LANES // HEAD_DIM
    zeros = jnp.zeros((HEAD_DIM, tq), BF16)
    outs = []
    for h in range(qt_ref.shape[0] // HEAD_DIM):
        g = h // KV_GROUP
        qh = qt_ref[h * HEAD_DIM:(h + 1) * HEAD_DIM, :]
        w = jnp.concatenate([zeros] * g + [qh] + [zeros] * (n_kv - 1 - g), axis=0)
        o2, denom = attend(h, w)
        outs.append(o2[g * HEAD_DIM:(g + 1) * HEAD_DIM, :] / denom)
    o_ref[...] = jnp.concatenate(outs, axis=0).T.astype(BF16)


def _all_heads_fused_t(qt_ref, o_ref, attend):
    tq = qt_ref.shape[1]
    n_heads = qt_ref.shape[0] // HEAD_DIM
    n_kv = LANES // HEAD_DIM
    rows = []
    for g in range(n_kv):
        heads = [qt_ref[h * HEAD_DIM:(h + 1) * HEAD_DIM, :] if h // KV_GROUP == g
                 else jnp.zeros((HEAD_DIM, tq), BF16) for h in range(n_heads)]
        rows.append(jnp.concatenate(heads, axis=1))
    w = jnp.concatenate(rows, axis=0)
    o2, denom = attend(0, w)
    o2 = o2 / denom
    outs = [o2[(h // KV_GROUP) * HEAD_DIM:(h // KV_GROUP + 1) * HEAD_DIM, h * tq:(h + 1) * tq]
            for h in range(n_heads)]
    o_ref[...] = jnp.concatenate(outs, axis=0).T.astype(BF16)


def _attn_b_kernel(bound_ref, qt_ref, k_ref, vt_ref, kc_ref, vct_ref, o_ref):
    k, kc, vt, vct = k_ref[...], kc_ref[...], vt_ref[...], vct_ref[...]
    bound = bound_ref[0]

    @pl.when(bound <= SAFE_SCORE_BOUND)
    def _():
        _all_heads_fused_t(qt_ref, o_ref, lambda h, w: _attend_t(w, k, kc, vt, vct, bounded=True))

    @pl.when(jnp.logical_not(bound <= SAFE_SCORE_BOUND))
    def _():
        _all_heads_t(qt_ref, o_ref, lambda h, w: _attend_t(w, k, kc, vt, vct))


def _attn_specs(nq, tq, seq, ctx_len):
    nqb = seq // tq
    return dict(
        qt=pl.BlockSpec((nq, tq), lambda b, i: (0, b * nqb + i)),
        k=pl.BlockSpec((seq, LANES), lambda b, i: (b, 0)),
        vt=pl.BlockSpec((LANES, seq), lambda b, i: (0, b)),
        kc=pl.BlockSpec((ctx_len, LANES), lambda b, i: (b, 0)),
        vct=pl.BlockSpec((LANES, ctx_len), lambda b, i: (0, b)),
        out=pl.BlockSpec((tq, nq), lambda b, i: (b * nqb + i, 0)))


def _attn_a_kernel(bound_ref, sink_ref, qt_ref, k_ref, vt_ref, kc_ref, vct_ref, bias_ref, o_ref,
                   *, seq):
    i = pl.program_id(1)
    tq = qt_ref.shape[1]
    n_heads = qt_ref.shape[0] // HEAD_DIM
    win = tq + 2 * WINDOW
    start = pl.multiple_of(jnp.clip(i * tq - WINDOW, 0, seq - win), WINDOW)
    k = k_ref[pl.ds(start, win), :]
    vt = vt_ref[:, pl.ds(start, win)]
    kc, vct = kc_ref[...], vct_ref[...]
    sinks = [sink_ref[h] * LOG2_E for h in range(n_heads)]

    bound = bound_ref[0]
    small = bound <= SAFE_SCORE_BOUND

    @pl.when(small)
    def _():
        sink_row = jnp.concatenate([jnp.full((1, tq), s, F32) for s in sinks], axis=1)
        _all_heads_fused_t(qt_ref, o_ref, lambda h, w: _attend_t(
            w, k, kc, vt, vct, bounded=True, bias=bias_ref[0], sink=sink_row))

    @pl.when(jnp.logical_not(small))
    def _():
        _all_heads_t(qt_ref, o_ref, lambda h, w: _attend_t(
            w, k, kc, vt, vct, bias=bias_ref[0], sink=sinks[h]))


def _band_bias(tq):
    win = tq + 2 * WINDOW
    r = np.arange(win)[:, None]
    j = np.arange(tq)[None, :]
    tables = [np.where(np.abs(off + r - j) <= WINDOW, 0.0, NEG_INF)
              for off in (0, -WINDOW, -2 * WINDOW)]
    return jnp.asarray(np.stack(tables), F32)


def _attn_ab_kernel(bound_a_ref, sink_ref, bound_b_ref,
                    qat_ref, ka_ref, vat_ref, kca_ref, vcat_ref, bias_ref,
                    qbt_ref, kb_ref, vbt_ref, kcb_ref, vcbt_ref, oa_ref, ob_ref, *, seq):
    _attn_a_kernel(bound_a_ref, sink_ref, qat_ref, ka_ref, vat_ref, kca_ref, vcat_ref, bias_ref,
                   oa_ref, seq=seq)
    _attn_b_kernel(bound_b_ref, qbt_ref, kb_ref, vbt_ref, kcb_ref, vcbt_ref, ob_ref)


def _attention_ab(bound_a, sink, bound_b, a_ops, b_ops, batch, seq, ctx_len):
    nq, t = a_ops[0].shape
    assert b_ops[0].shape == (nq, t) and ATTN_A_TQ == ATTN_B_TQ
    tq = ATTN_A_TQ
    nqb = seq // tq
    win = tq + 2 * WINDOW
    assert nqb >= 2 and tq >= WINDOW
    sp = _attn_specs(nq, tq, seq, ctx_len)
    group = [sp["qt"], sp["k"], sp["vt"], sp["kc"], sp["vct"]]
    which = lambda b, i: (jnp.where(i == 0, 0, jnp.where(i == nqb - 1, 2, 1)), 0, 0)
    smem = pl.BlockSpec(memory_space=pltpu.SMEM)
    return pl.pallas_call(
        functools.partial(_attn_ab_kernel, seq=seq),
        grid=(batch, nqb),
        in_specs=[smem, smem, smem] + group + [pl.BlockSpec((1, win, tq), which)] + group,
        out_specs=[sp["out"], sp["out"]],
        out_shape=[jax.ShapeDtypeStruct((t, nq), BF16)] * 2,
        compiler_params=_params("arbitrary", "arbitrary"),
    )(bound_a, sink, bound_b, *a_ops, _band_bias(tq), *b_ops)


def _out_kernel(oa_ref, ob_ref, x_ref, g1_ref, sc2_ref, sh2_ref, ga_ref, gb_ref, gpost_ref,
                gpre2_ref, woa_ref, wob_ref, wr_ref, br_ref,
                x1_ref, h2_ref, rinfo_ref, rt_ref, tcarry_ref, tcnt_ref, cnt_ref, carry_ref):
    step = pl.program_id(0)

    @pl.when(step == 0)
    def _():
        carry_ref[...] = jnp.zeros_like(carry_ref)

    na = _rms(oa_ref[...].astype(F32)) * ga_ref[...]
    nb = _rms(ob_ref[...].astype(F32)) * gb_ref[...]
    ox = _dot(na.astype(BF16), woa_ref[...]) + _dot(nb.astype(BF16), wob_ref[...])
    x1 = x_ref[...] + g1_ref[0, 0] * (_rms(ox) * gpost_ref[...])
    x1_ref[...] = x1
    h2 = _rms(x1) * gpre2_ref[...] * (1.0 + sc2_ref[0, 0]) + sh2_ref[0, 0]
    h_hi, h_lo = _split_bf16(h2)
    h2_ref[...] = h_hi

    both = _dot(h_hi, wr_ref[...])
    logits = (both[:, :LANES] + _dot(h_lo, wr_ref[:, :LANES]) + both[:, LANES:]
              + br_ref[...])
    tm = logits.shape[0]
    lt = logits.T
    row = lax.broadcasted_iota(jnp.int32, lt.shape, 0)
    rowf = row.astype(F32)
    big = jnp.float32(1e9)
    ninf = jnp.float32(-jnp.inf)
    colmax = lambda v: jnp.max(v, axis=0, keepdims=True)
    colmin = lambda v: jnp.min(v, axis=0, keepdims=True)
    colsum = lambda v: jnp.sum(v, axis=0, keepdims=True)

    gmask = (row >= N_EXPERTS) & (row < N_EXPERTS + N_GROUPS)
    lg = jnp.where(gmask, lt, ninf)
    gmax = colmax(lg)
    gidx = colmin(jnp.where(lg == gmax, rowf, big)) - N_EXPERTS
    g_w = 1.0 / colsum(jnp.exp(lg - gmax))
    row_group = (row // EXPERTS_PER_GROUP).astype(F32)
    emask = (row < N_EXPERTS) & (row_group == gidx)
    le = jnp.where(emask, lt, ninf)
    m1 = colmax(le)
    i1 = colmin(jnp.where(le == m1, rowf, big))
    le2 = jnp.where(rowf == i1, ninf, le)
    m2 = colmax(le2)
    i2 = colmin(jnp.where(le2 == m2, rowf, big))
    e2 = jnp.exp(m2 - m1)
    w0 = g_w / (1.0 + e2)
    w1 = g_w * e2 / (1.0 + e2)

    hit1 = rowf == i1
    hit2 = rowf == i2
    onehot = jnp.where(hit1, 1.0, jnp.where(hit2, 1.0, 0.0)).astype(F32)
    r = lax.broadcasted_iota(jnp.int32, (tm, tm), 0)
    c = lax.broadcasted_iota(jnp.int32, (tm, tm), 1)
    earlier = jnp.where(r < c, 1.0, 0.0).astype(BF16)
    within = _dot(onehot.astype(BF16), earlier)
    tile_cnt = jnp.broadcast_to(jnp.sum(onehot, axis=1, keepdims=True), (LANES, LANES))
    er = lax.broadcasted_iota(jnp.int32, (LANES, LANES), 0)
    ec = lax.broadcasted_iota(jnp.int32, (LANES, LANES), 1)
    below = jnp.where(er > ec, 1.0, 0.0).astype(BF16)
    cnt_hi = jnp.floor(tile_cnt * (1.0 / 32.0))
    cnt_lo = tile_cnt - 32.0 * cnt_hi
    run_start = 32.0 * _dot(below, cnt_hi.astype(BF16)) + _dot(below, cnt_lo.astype(BF16))
    local = within + run_start[:, 0:1]
    pos0 = colsum(jnp.where(hit1, local, 0.0))
    pos1 = colsum(jnp.where(hit2, local, 0.0))
    cnt_row = tile_cnt.T[0:1, :]
    tcarry_ref[0] = carry_ref[...]
    tcnt_ref[0] = cnt_row
    carry_ref[...] += cnt_row
    cnt_ref[...] = carry_ref[...]

    fields = jnp.concatenate([i1, i2, pos0, pos1, w0, w1, jnp.zeros((2, tm), F32)], axis=0)
    rt_ref[...] = fields
    rinfo_ref[...] = jnp.concatenate(
        [fields, jnp.zeros((LANES - 8, tm), F32)], axis=0).T


def _out_and_route(oa, ob, x2, mod, ga, gb, gpost, gpre2, w_out, wr, br, seq):
    t, d = x2.shape
    tm = OUT_TM
    tpb = seq // tm
    nq = oa.shape[1]
    const = lambda shape: pl.BlockSpec(shape, lambda i: (0,) * len(shape))
    batch_of = lambda i: i // tpb
    rows = lambda n: pl.BlockSpec((tm, n), lambda i: (i, 0))
    per_tile = pl.BlockSpec((1, 1, LANES), lambda i: (i, 0, 0))
    w_half = lambda g: pl.BlockSpec((nq, d), lambda i: (g, 0))
    return pl.pallas_call(
        _out_kernel,
        grid=(t // tm,),
        in_specs=[rows(nq), rows(nq), rows(d),
                  _mod_spec(MOD_G1, batch_of)(d), _mod_spec(MOD_SC2, batch_of)(d),
                  _mod_spec(MOD_SH2, batch_of)(d),
                  const((1, nq)), const((1, nq)), const((1, d)), const((1, d)),
                  w_half(0), w_half(1), const(wr.shape),
                  const((1, LANES))],
        out_specs=[rows(d), rows(d), rows(LANES), pl.BlockSpec((8, tm), lambda i: (0, i)),
                   per_tile, per_tile, const((1, LANES))],
        out_shape=[jax.ShapeDtypeStruct((t, d), F32), jax.ShapeDtypeStruct((t, d), BF16),
                   jax.ShapeDtypeStruct((t, LANES), F32), jax.ShapeDtypeStruct((8, t), F32),
                   jax.ShapeDtypeStruct((t // tm, 1, LANES), F32),
                   jax.ShapeDtypeStruct((t // tm, 1, LANES), F32),
                   jax.ShapeDtypeStruct((1, LANES), F32)],
        scratch_shapes=[pltpu.VMEM((1, LANES), F32)],
        compiler_params=_params("arbitrary"),
    )(oa, ob, x2, mod, mod, mod, ga, gb, gpost, gpre2, w_out, w_out, wr, br)


PACK_ROWS = 8
ROW_DTYPE = F32


def _pack_rows(ref, x):
    n = x.shape[0]
    for c in range(PACK_ROWS):
        ref[pl.ds(c, n, stride=PACK_ROWS), :] = x[:, c * LANES:(c + 1) * LANES]


def _unpack_rows(ref):
    n = ref.shape[0] // PACK_ROWS
    return jnp.concatenate(
        [ref[pl.ds(c, n, stride=PACK_ROWS), :].astype(BF16) for c in range(PACK_ROWS)], axis=1)


def _for_each_run_piece(rdst_ref, rlen_ref, rmax_ref, tile, max_len, fn):
    def copy_runs(n_bits):
        def run(e, local):
            length = rlen_ref[tile * N_EXPERTS + e]
            dst = rdst_ref[tile * N_EXPERTS + e]
            for b in range(n_bits):
                size = 1 << b

                @pl.when(((length >> b) & 1) == 1)
                def _():
                    done = length & (size - 1)
                    fn(local + done, dst + done, size)
            return local + length

        lax.fori_loop(0, N_EXPERTS, run, 0)

    all_bits = max_len.bit_length()
    low_bits = min(all_bits, (4 * max_len // N_EXPERTS).bit_length())
    short = rmax_ref[tile] < (1 << low_bits)

    @pl.when(short)
    def _():
        copy_runs(low_bits)

    @pl.when(jnp.logical_not(short))
    def _():
        copy_runs(all_bits)


def _token_rows(ref, row0, n_rows):
    start = row0 * PACK_ROWS
    if not isinstance(start, int):
        start = pl.multiple_of(start, PACK_ROWS)
    return ref.at[pl.ds(start, n_rows * PACK_ROWS)]


def _dispatch_kernel(rdst_ref, rlen_ref, rmax_ref, h_ref, rt_ref, xs_ref, sorted_ref, sem):
    k = pl.program_id(0)
    nk = pl.num_programs(0)
    tm = h_ref.shape[0]
    rows = 2 * tm
    slot = k % 2

    def wait_slot(s):
        pltpu.make_async_copy(sorted_ref.at[s], _token_rows(xs_ref, 0, rows), sem.at[s]).wait()

    @pl.when(k >= 2)
    def _():
        wait_slot(slot)

    pos0 = rt_ref[2:3, :]
    pos1 = rt_ref[3:4, :]
    r = lax.broadcasted_iota(jnp.int32, (rows, tm), 0).astype(F32)
    perm = jnp.where((r == pos0) | (r == pos1), 1.0, 0.0).astype(BF16)
    srt = _dot(perm, h_ref[...].astype(BF16))
    buf = sorted_ref.at[slot]
    _pack_rows(buf, srt)

    def copy_piece(local, dst, size):
        pltpu.make_async_copy(_token_rows(buf, local, size), _token_rows(xs_ref, dst, size),
                              sem.at[slot]).start()

    _for_each_run_piece(rdst_ref, rlen_ref, rmax_ref, k, tm, copy_piece)

    @pl.when(k == nk - 1)
    def _():
        wait_slot(slot)

        @pl.when(nk >= 2)
        def _():
            wait_slot(1 - slot)


def _dispatch(run_dst, run_len, run_max, h2, rt):
    t, d = h2.shape
    assert d == PACK_ROWS * LANES
    tm = MOVE_TM
    return pl.pallas_call(
        _dispatch_kernel,
        grid_spec=pltpu.PrefetchScalarGridSpec(
            num_scalar_prefetch=3,
            grid=(t // tm,),
            in_specs=[pl.BlockSpec((tm, d), lambda i, *_: (i, 0)),
                      pl.BlockSpec((8, tm), lambda i, *_: (0, i))],
            out_specs=pl.BlockSpec(memory_space=pl.ANY),
            scratch_shapes=[pltpu.VMEM((2, 2 * tm * PACK_ROWS, LANES), ROW_DTYPE),
                            pltpu.SemaphoreType.DMA((2,))]),
        out_shape=jax.ShapeDtypeStruct((2 * t * PACK_ROWS, LANES), ROW_DTYPE),
        compiler_params=_params("arbitrary"),
    )(run_dst, run_len, run_max, h2, rt)


def _expert_kernel(vt_ref, ve_ref, va_ref, vb_ref, vn_ref, nv_ref, xs_ref, wg_hbm, wu_hbm, wd_hbm,
                   ys_ref, wg_bf, wu_bf, wd_bf, wg_f32, wu_f32, wd_f32, wsem):
    v = pl.program_id(0)
    valid = v < nv_ref[0]
    prev = jnp.maximum(v - 1, 0)
    new_expert = (v == 0) | (ve_ref[v] != ve_ref[prev])
    new_tile = (v == 0) | (vt_ref[v] != vt_ref[prev])

    def weight_copies(e):
        return [pltpu.make_async_copy(src.at[e], dst, wsem)
                for src, dst in ((wg_hbm, wg_f32), (wu_hbm, wu_f32), (wd_hbm, wd_f32))]

    @pl.when(v == 0)
    def _():
        for cp in weight_copies(ve_ref[0]):
            cp.start()

    @pl.when(valid & new_expert)
    def _():
        for cp in weight_copies(ve_ref[v]):
            cp.wait()
        wg_bf[...] = wg_f32[...].astype(BF16)
        wu_bf[...] = wu_f32[...].astype(BF16)
        wd_bf[...] = wd_f32[...].astype(BF16)

        @pl.when(vn_ref[v] >= 0)
        def _():
            for cp in weight_copies(vn_ref[v]):
                cp.start()

    def expert_rows():
        xb = _unpack_rows(xs_ref)
        gate = _dot(xb, wg_bf[...])
        up = _dot(xb, wu_bf[...])
        act = gate * jax.nn.sigmoid(gate) * up
        return _dot(act.astype(BF16), wd_bf[...])

    @pl.when(valid & new_tile)
    def _():
        _pack_rows(ys_ref, expert_rows())

    @pl.when(valid & jnp.logical_not(new_tile))
    def _():
        y = expert_rows()
        te = y.shape[0]
        row = lax.broadcasted_iota(jnp.int32, (te, 1), 0)
        mine = (row >= va_ref[v]) & (row < vb_ref[v])
        for c in range(PACK_ROWS):
            rows = pl.ds(c, te, stride=PACK_ROWS)
            ys_ref[rows, :] = jnp.where(mine, y[:, c * LANES:(c + 1) * LANES], ys_ref[rows, :])


def _expert_mlp(visit_tile, visit_expert, visit_lo, visit_hi, visit_next, n_visits, xs,
                w_gate, w_up, w_down):
    te = EXPERT_TE
    d, ff = w_gate.shape[1:]
    blk = (te * PACK_ROWS, LANES)
    tile = lambda v, vt, *_: (vt[v], 0)
    hbm = pl.BlockSpec(memory_space=pl.ANY)
    return pl.pallas_call(
        _expert_kernel,
        grid_spec=pltpu.PrefetchScalarGridSpec(
            num_scalar_prefetch=6,
            grid=(visit_tile.shape[0],),
            in_specs=[pl.BlockSpec(blk, tile), hbm, hbm, hbm],
            out_specs=pl.BlockSpec(blk, tile),
            scratch_shapes=[pltpu.VMEM((d, ff), BF16), pltpu.VMEM((d, ff), BF16),
                            pltpu.VMEM((ff, d), BF16),
                            pltpu.VMEM((d, ff), F32), pltpu.VMEM((d, ff), F32),
                            pltpu.VMEM((ff, d), F32), pltpu.SemaphoreType.DMA(())]),
        out_shape=jax.ShapeDtypeStruct(xs.shape, ROW_DTYPE),
        compiler_params=_params("arbitrary"),
    )(visit_tile, visit_expert, visit_lo, visit_hi, visit_next, n_visits, xs,
      w_gate, w_up, w_down)


def _combine_kernel(rdst_ref, rlen_ref, rmax_ref, x1_ref, rinfo_ref, g2_ref, gpost_ref, ys_ref, o_ref,
                    gath_ref, sem):
    k = pl.program_id(0)
    nk = pl.num_programs(0)
    tm = x1_ref.shape[0]
    rows = 2 * tm
    slot = k % 2

    def gather_runs(tile, s):
        buf = gath_ref.at[s]

        def copy_piece(local, src, size):
            pltpu.make_async_copy(_token_rows(ys_ref, src, size), _token_rows(buf, local, size),
                                  sem.at[s]).start()

        _for_each_run_piece(rdst_ref, rlen_ref, rmax_ref, tile, tm, copy_piece)

    @pl.when(k == 0)
    def _():
        gather_runs(0, 0)

    @pl.when(k + 1 < nk)
    def _():
        gather_runs(k + 1, 1 - slot)

    buf = gath_ref.at[slot]
    pltpu.make_async_copy(_token_rows(ys_ref, 0, rows), buf, sem.at[slot]).wait()
    g = _unpack_rows(buf)
    info = rinfo_ref[...]
    col = lax.broadcasted_iota(jnp.int32, (tm, rows), 1).astype(F32)
    pick = jnp.where(col == info[:, 2:3], info[:, 4:5],
                     jnp.where(col == info[:, 3:4], info[:, 5:6], 0.0)).astype(BF16)
    fx = _dot(pick, g)
    o_ref[...] = x1_ref[...] + g2_ref[0, 0] * (_rms(fx) * gpost_ref[...])


def _combine(run_dst, run_len, run_max, x1, rinfo, mod, gpost, ys, seq):
    t, d = x1.shape
    tm = MOVE_TM
    tpb = seq // tm
    batch_of = lambda i: i // tpb
    return pl.pallas_call(
        _combine_kernel,
        grid_spec=pltpu.PrefetchScalarGridSpec(
            num_scalar_prefetch=3,
            grid=(t // tm,),
            in_specs=[pl.BlockSpec((tm, d), lambda i, *_: (i, 0)),
                      pl.BlockSpec((tm, LANES), lambda i, *_: (i, 0)),
                      _mod_spec(MOD_G2, batch_of)(d),
                      pl.BlockSpec((1, d), lambda i, *_: (0, 0)),
                      pl.BlockSpec(memory_space=pl.ANY)],
            out_specs=pl.BlockSpec((tm, d), lambda i, *_: (i, 0)),
            scratch_shapes=[pltpu.VMEM((2, 2 * tm * PACK_ROWS, LANES), ROW_DTYPE),
                            pltpu.SemaphoreType.DMA((2,))]),
        out_shape=jax.ShapeDtypeStruct((t, d), F32),
        compiler_params=_params("arbitrary"),
    )(run_dst, run_len, run_max, x1, rinfo, mod, gpost, ys)


def _rope_tables(seq):
    pos = np.arange(seq)
    row = (pos // GRID_W).astype(np.float32)
    col = (pos % GRID_W).astype(np.float32)
    axis_dim = HEAD_DIM // 2
    inv_freq = (ROPE_THETA ** (-np.arange(0, axis_dim, 2, dtype=np.float32) / axis_dim)).astype(
        np.float32)
    ang = np.concatenate([row[:, None] * inv_freq, col[:, None] * inv_freq], axis=-1)
    pair = (np.arange(LANES) % HEAD_DIM) // 2
    cos = np.cos(ang)[:, pair]
    sin = np.sin(ang)[:, pair]
    even = (np.arange(LANES) % 2) == 0
    tables = (cos, np.where(even, -sin, 0.0), np.where(even, 0.0, sin))
    tables = tables + tuple(tb.T for tb in tables)
    return tuple(jnp.asarray(tb, F32) for tb in tables)


def _segment_ones(n):
    seg = np.arange(n) // HEAD_DIM
    return jnp.asarray(seg[:, None] == seg[None, :], BF16)


def kernel(x, c, ctx, c_ctx, w_mod, b_mod, attn_pre_norm, attn_post_norm, w_in, a_sink,
           b_q_norm, b_k_norm, a_out_norm, b_out_norm, w_out, ffn_pre_norm, ffn_post_norm,
           w_group, b_group, w_router, b_router, w_gate, w_up, w_down):
    batch, seq, d = x.shape
    ctx_len = ctx.shape[1]
    assert w_mod.shape[0] == 1, "single-layer stack only (context stream is never updated)"
    assert seq % ATTN_A_TQ == 0 and seq >= ATTN_A_TQ + 2 * WINDOW
    assert seq % PROJ_TM == 0 and seq % ATTN_B_TQ == 0 and seq % OUT_TM == 0 and seq % MOVE_TM == 0
    t = batch * seq
    nq = d // 2
    nkv = nq // KV_GROUP
    assert nkv == LANES and w_in.shape[2] == 2 * nq + 4 * nkv

    assert batch + 1 <= MOD_ROWS
    cc = jnp.concatenate([c, c_ctx[None, :], jnp.zeros((MOD_ROWS - batch - 1, d), F32)], axis=0)
    mod = _modulation(cc, w_mod[0], b_mod[0]).reshape(cc.shape[0], 6, 1, d)

    x2 = x.reshape(t, d)
    c2 = ctx.reshape(batch * ctx_len, d)
    gpre = attn_pre_norm[0].reshape(1, d)
    w_in_bf = w_in[0].astype(BF16)
    qn = jnp.broadcast_to(jnp.tile(b_q_norm[0], nq // HEAD_DIM)[:, None], (nq, PROJ_TM))
    kn = jnp.tile(b_k_norm[0], nkv // HEAD_DIM).reshape(1, nkv)
    seg_k = _segment_ones(nkv)
    qat, ka, vat, qbt, kb, vbt, stats = _project_latents(
        x2, mod, gpre, w_in_bf, _rope_tables(seq), qn, kn, seg_k, seq)
    kca, vcat, kcb, vcbt, ctx_stats = _project_context(
        c2, mod, batch, gpre, w_in_bf, kn, seg_k, ctx_len)

    q_sq = jnp.max(stats[:, 0, 0])
    k_sq = jnp.maximum(jnp.max(stats[:, 0, 1]), jnp.max(ctx_stats[:, 0, 1]))
    bound_a = jnp.maximum(1.01 * jnp.sqrt(q_sq * k_sq), jnp.max(a_sink[0]) * LOG2_E).reshape(1)
    score_bound = (1.01 * HEAD_DIM ** 0.5 * LOG2_E
                   * jnp.max(jnp.abs(b_q_norm[0])) * jnp.max(jnp.abs(b_k_norm[0]))).reshape(1)
    oa, ob = _attention_ab(bound_a, a_sink[0], score_bound, (qat, ka, vat, kca, vcat),
                           (qbt, kb, vbt, kcb, vcbt), batch, seq, ctx_len)

    w_out_bf = w_out[0].astype(BF16)
    lane_pad = LANES - N_EXPERTS - N_GROUPS
    w_r = jnp.pad(jnp.concatenate([w_router[0], w_group[0]], axis=1), ((0, 0), (0, lane_pad)))
    w_r_hi = w_r.astype(BF16)
    w_r_lo = (w_r - w_r_hi.astype(F32)).astype(BF16)
    w_r2 = jnp.concatenate([w_r_hi, w_r_lo], axis=1)
    b_r = jnp.pad(jnp.concatenate([b_router[0], b_group[0]]), (0, lane_pad)).reshape(1, LANES)
    x1, h2, rinfo, rt, tcarry, tcnt, counts = _out_and_route(
        oa, ob, x2, mod, a_out_norm[0].reshape(1, nq), b_out_norm[0].reshape(1, nq),
        attn_post_norm[0].reshape(1, d), ffn_pre_norm[0].reshape(1, d),
        w_out_bf, w_r2, b_r, seq)

    te = EXPERT_TE
    assert (2 * t) % te == 0
    cnt = counts[0, :N_EXPERTS].astype(jnp.int32)
    ends = jnp.cumsum(cnt)
    starts = ends - cnt
    run_dst = (starts[None, :] + tcarry[:, 0, :N_EXPERTS].astype(jnp.int32)).reshape(-1)
    tile_runs = tcnt[:, 0, :N_EXPERTS].astype(jnp.int32)
    run_len = tile_runs.reshape(-1)
    run_max = jnp.max(tile_runs, axis=1)
    first_tile = starts // te
    n_vis = jnp.where(cnt > 0, (ends - 1) // te - first_tile + 1, 0)
    vis_end = jnp.cumsum(n_vis)
    n_visits = vis_end[-1]
    v = jnp.minimum(jnp.arange(2 * t // te + N_EXPERTS, dtype=jnp.int32), n_visits - 1)
    v_expert = jnp.sum(vis_end[None, :] <= v[:, None], axis=1).astype(jnp.int32)
    pick = (v_expert[:, None] == jnp.arange(N_EXPERTS)[None, :]).astype(jnp.int32)
    of_expert = lambda table: jnp.sum(pick * table[None, :], axis=1)
    v_tile = of_expert(first_tile) + v - of_expert(vis_end - n_vis)
    v_lo = jnp.maximum(of_expert(starts) - v_tile * te, 0)
    v_hi = jnp.minimum(of_expert(ends) - v_tile * te, te)
    ids = jnp.arange(N_EXPERTS, dtype=jnp.int32)
    later = (ids[None, :] > ids[:, None]) & (cnt[None, :] > 0)
    next_expert = jnp.min(jnp.where(later, ids[None, :], N_EXPERTS), axis=1)
    v_next = of_expert(jnp.where(next_expert < N_EXPERTS, next_expert, -1))

    xs = _dispatch(run_dst, run_len, run_max, h2, rt)
    ys = _expert_mlp(v_tile, v_expert, v_lo, v_hi, v_next, n_visits.reshape(1), xs,
                     w_gate[0], w_up[0], w_down[0])
    out = _combine(run_dst, run_len, run_max, x1, rinfo, mod, ffn_post_norm[0].reshape(1, d), ys,
                   seq)
    return out.reshape(batch, seq, d)
```

```python
import functools

import jax
import jax.numpy as jnp
import numpy as np
from jax import lax
from jax.experimental import pallas as pl
from jax.experimental.pallas import tpu as pltpu

F32 = jnp.float32
BF16 = jnp.bfloat16

GRID_W = 64
HEAD_DIM = 64
KV_GROUP = 4
WINDOW = 128
ROPE_THETA = 10000.0
N_GROUPS = 4
EXPERTS_PER_GROUP = 8
N_EXPERTS = N_GROUPS * EXPERTS_PER_GROUP
EPS = 1e-6
NEG_INF = -1e30
LOG2_E = 1.4426950408889634
SAFE_SCORE_BOUND = 40.0

LANES = 128
V7X_VMEM_LIMIT = 56 * 1024 * 1024

MOD_ROWS = 16
MOD_BN = 1024
PROJ_TM = 512
ATTN_A_TQ = 256
ATTN_B_TQ = 256
OUT_TM = 512
EXPERT_TE = 512
MOVE_TM = OUT_TM


def _params(*sem):
    return pltpu.CompilerParams(dimension_semantics=sem, vmem_limit_bytes=V7X_VMEM_LIMIT)


def _dot(a, b):
    return jnp.dot(a, b, preferred_element_type=F32)


def _rms(x):
    return x * lax.rsqrt(jnp.mean(x * x, axis=-1, keepdims=True) + EPS)


def _split_bf16(x):
    hi = x.astype(BF16)
    lo = (x - hi.astype(F32)).astype(BF16)
    return hi, lo


def _mod_kernel(c_ref, w_ref, b_ref, o_ref):
    cc = c_ref[...]
    s = cc * jax.nn.sigmoid(cc)
    s_hi, s_lo = _split_bf16(s)
    w_hi, w_lo = _split_bf16(w_ref[...])
    o_ref[...] = _dot(s_hi, w_hi) + _dot(s_lo, w_hi) + _dot(s_hi, w_lo) + b_ref[...]


def _modulation(cc, w_mod, b_mod):
    rows, d = cc.shape
    n = w_mod.shape[1]
    bn = MOD_BN
    return pl.pallas_call(
        _mod_kernel,
        grid=(n // bn,),
        in_specs=[pl.BlockSpec((rows, d), lambda i: (0, 0)),
                  pl.BlockSpec((d, bn), lambda i: (0, i)),
                  pl.BlockSpec((1, bn), lambda i: (0, i))],
        out_specs=pl.BlockSpec((rows, bn), lambda i: (0, i)),
        out_shape=jax.ShapeDtypeStruct((rows, n), F32),
        compiler_params=_params("arbitrary"),
    )(cc, w_mod, b_mod.reshape(1, n))


def _rope(x, cos, sin_a, sin_b):
    return x * cos + pltpu.roll(x, LANES - 1, 1) * sin_a + pltpu.roll(x, 1, 1) * sin_b


def _head_norm(x, seg_ref, gain):
    ss = _dot((x * x).astype(BF16), seg_ref[...])
    return x * lax.rsqrt(ss * (1.0 / HEAD_DIM) + EPS) * gain


def _max_head_sq_norm(x, seg_ref):
    ss = _dot((x * x).astype(BF16), seg_ref[...])
    return jnp.max(jnp.max(ss, axis=1, keepdims=True), axis=0, keepdims=True)


def _norm_stats(q_sq, k_sq):
    lane = lax.broadcasted_iota(jnp.int32, (1, LANES), 1)
    zero = jnp.zeros((1, LANES), F32)
    return jnp.where(lane == 0, q_sq, zero) + jnp.where(lane == 1, k_sq, zero)


def _rope_t(xt, cos_t, sin_a_t, sin_b_t):
    return (xt * cos_t + pltpu.roll(xt, LANES - 1, 0) * sin_a_t
            + pltpu.roll(xt, 1, 0) * sin_b_t)


def _proj_kernel(x_ref, sc_ref, sh_ref, gpre_ref, w_ref, cos_ref, sa_ref, sb_ref,
                 cos_t_ref, sa_t_ref, sb_t_ref, qn_ref, kn_ref, seg_k_ref,
                 qat_ref, ka_ref, vat_ref, qbt_ref, kb_ref, vbt_ref, stats_ref):
    h = _rms(x_ref[...]) * gpre_ref[...] * (1.0 + sc_ref[0, 0]) + sh_ref[0, 0]
    p = _dot(h.astype(BF16), w_ref[...])
    cos, sa, sb = cos_ref[...], sa_ref[...], sb_ref[...]
    q_scale = HEAD_DIM ** -0.5 * LOG2_E
    cos_t, sa_t, sb_t = cos_t_ref[...] * q_scale, sa_t_ref[...] * q_scale, sb_t_ref[...] * q_scale
    nq = qat_ref.shape[0]
    q_sq = None
    for c in range(nq // LANES):
        xt = p[:, c * LANES:(c + 1) * LANES].T
        qat_ref[c * LANES:(c + 1) * LANES, :] = _rope_t(xt, cos_t, sa_t, sb_t).astype(BF16)
        sq = xt * xt
        for head in range(LANES // HEAD_DIM):
            norm = jnp.sum(sq[head * HEAD_DIM:(head + 1) * HEAD_DIM, :], axis=0, keepdims=True)
            q_sq = norm if q_sq is None else jnp.maximum(q_sq, norm)
    q_sq = jnp.max(q_sq, axis=1, keepdims=True) * (q_scale * q_scale)
    o = nq
    ka_ref[...] = _rope(p[:, o:o + LANES], cos, sa, sb).astype(BF16)
    vat_ref[...] = p[:, o + LANES:o + 2 * LANES].T.astype(BF16)
    stats_ref[0] = _norm_stats(q_sq, _max_head_sq_norm(p[:, o:o + LANES], seg_k_ref))
    o += 2 * LANES
    for c in range(nq // LANES):
        xt = p[:, o + c * LANES:o + (c + 1) * LANES].T
        sq = xt * xt
        halves = []
        for head in range(LANES // HEAD_DIM):
            rows = slice(head * HEAD_DIM, (head + 1) * HEAD_DIM)
            ms = jnp.sum(sq[rows, :], axis=0, keepdims=True) * (1.0 / HEAD_DIM)
            halves.append(xt[rows, :] * lax.rsqrt(ms + EPS))
        qn = jnp.concatenate(halves, axis=0) * qn_ref[c * LANES:(c + 1) * LANES, :]
        qbt_ref[c * LANES:(c + 1) * LANES, :] = _rope_t(qn, cos_t, sa_t, sb_t).astype(BF16)
    o += nq
    kb = _head_norm(p[:, o:o + LANES], seg_k_ref, kn_ref[...])
    kb_ref[...] = _rope(kb, cos, sa, sb).astype(BF16)
    vbt_ref[...] = p[:, o + LANES:o + 2 * LANES].T.astype(BF16)


def _ctx_proj_kernel(x_ref, sc_ref, sh_ref, gpre_ref, wa_ref, wb_ref, kn_ref, seg_k_ref,
                     ka_ref, vat_ref, kb_ref, vbt_ref, stats_ref):
    h = (_rms(x_ref[...]) * gpre_ref[...] * (1.0 + sc_ref[0, 0]) + sh_ref[0, 0]).astype(BF16)
    pa = _dot(h, wa_ref[...])
    pb = _dot(h, wb_ref[...])
    stats_ref[0] = _norm_stats(0.0, _max_head_sq_norm(pa[:, 0:LANES], seg_k_ref))
    ka_ref[...] = pa[:, 0:LANES].astype(BF16)
    vat_ref[...] = pa[:, LANES:2 * LANES].T.astype(BF16)
    kb_ref[...] = _head_norm(pb[:, 0:LANES], seg_k_ref, kn_ref[...]).astype(BF16)
    vbt_ref[...] = pb[:, LANES:2 * LANES].T.astype(BF16)


def _mod_spec(chunk, row_of_step):
    return lambda d: pl.BlockSpec((1, 1, 1, d), lambda i, *_: (row_of_step(i), chunk, 0, 0))


MOD_SH1, MOD_SC1, MOD_G1, MOD_SH2, MOD_SC2, MOD_G2 = range(6)


def _project_latents(x2, mod, gpre, w_in, tables, qn, kn, seg_k, seq):
    t, d = x2.shape
    tm = PROJ_TM
    tpb = seq // tm
    nq = qn.shape[0]
    const = lambda shape: pl.BlockSpec(shape, lambda i: (0,) * len(shape))
    batch_of = lambda i: i // tpb
    table = pl.BlockSpec((tm, LANES), lambda i: (i % tpb, 0))
    table_t = pl.BlockSpec((LANES, tm), lambda i: (0, i % tpb))
    k_spec = pl.BlockSpec((tm, LANES), lambda i: (i, 0))
    k_shape = jax.ShapeDtypeStruct((t, LANES), BF16)
    vt_spec = pl.BlockSpec((LANES, tm), lambda i: (0, i))
    vt_shape = jax.ShapeDtypeStruct((LANES, t), BF16)
    qt_spec = pl.BlockSpec((nq, tm), lambda i: (0, i))
    qt_shape = jax.ShapeDtypeStruct((nq, t), BF16)
    return pl.pallas_call(
        _proj_kernel,
        grid=(t // tm,),
        in_specs=[pl.BlockSpec((tm, d), lambda i: (i, 0)),
                  _mod_spec(MOD_SC1, batch_of)(d), _mod_spec(MOD_SH1, batch_of)(d), const((1, d)),
                  const(w_in.shape), table, table, table, table_t, table_t, table_t,
                  const((nq, tm)), const((1, LANES)), const(seg_k.shape)],
        out_specs=[qt_spec, k_spec, vt_spec, qt_spec, k_spec, vt_spec,
                   pl.BlockSpec((1, 1, LANES), lambda i: (i, 0, 0))],
        out_shape=[qt_shape, k_shape, vt_shape, qt_shape, k_shape, vt_shape,
                   jax.ShapeDtypeStruct((t // tm, 1, LANES), F32)],
        compiler_params=_params("arbitrary"),
    )(x2, mod, mod, gpre, w_in, *tables, qn, kn, seg_k)


def _project_context(c2, mod, ctx_row, gpre, w_in, kn, seg_k, ctx_len):
    t, d = c2.shape
    ctx_mod = lambda i: ctx_row
    nq = (w_in.shape[1] - 4 * LANES) // 2
    kv = 2 * LANES
    assert nq % kv == 0
    group_kv = lambda g: pl.BlockSpec((d, kv), lambda i: (0, (g * (nq + kv) + nq) // kv))
    const = lambda shape: pl.BlockSpec(shape, lambda i: (0,) * len(shape))
    k_spec = pl.BlockSpec((ctx_len, LANES), lambda i: (i, 0))
    k_shape = jax.ShapeDtypeStruct((t, LANES), BF16)
    vt_spec = pl.BlockSpec((LANES, ctx_len), lambda i: (0, i))
    vt_shape = jax.ShapeDtypeStruct((LANES, t), BF16)
    return pl.pallas_call(
        _ctx_proj_kernel,
        grid=(t // ctx_len,),
        in_specs=[pl.BlockSpec((ctx_len, d), lambda i: (i, 0)),
                  _mod_spec(MOD_SC1, ctx_mod)(d), _mod_spec(MOD_SH1, ctx_mod)(d),
                  const((1, d)), group_kv(0), group_kv(1), const((1, LANES)), const(seg_k.shape)],
        out_specs=[k_spec, vt_spec, k_spec, vt_spec,
                   pl.BlockSpec((1, 1, LANES), lambda i: (i, 0, 0))],
        out_shape=[k_shape, vt_shape, k_shape, vt_shape,
                   jax.ShapeDtypeStruct((t // ctx_len, 1, LANES), F32)],
        compiler_params=_params("arbitrary"),
    )(c2, mod, mod, gpre, w_in, w_in, kn, seg_k)


def _attend_t(w, k, kc, vt, vct, bounded=False, bias=None, sink=None):
    st = _dot(k, w)
    sct = _dot(kc, w)
    if bias is not None:
        tq = bias.shape[1]
        st = jnp.concatenate([st[:, c * tq:(c + 1) * tq] + bias
                              for c in range(st.shape[1] // tq)], axis=1)
    if not bounded:
        shift = jnp.maximum(jnp.max(st, axis=0, keepdims=True),
                            jnp.max(sct, axis=0, keepdims=True))
        if sink is not None:
            shift = jnp.maximum(shift, sink)
            sink = sink - shift
        st, sct = st - shift, sct - shift
    pt = jnp.exp2(st)
    pct = jnp.exp2(sct)
    denom = jnp.sum(pt, axis=0, keepdims=True) + jnp.sum(pct, axis=0, keepdims=True)
    if sink is not None:
        denom = denom + jnp.exp2(sink)
    o2 = _dot(vt, pt.astype(BF16)) + _dot(vct, pct.astype(BF16))
    return o2, denom


def _all_heads_t(qt_ref, o_ref, attend):
    tq = qt_ref.shape[1]
    n_kv = LANES // HEAD_DIM
    zeros = jnp.zeros((HEAD_DIM, tq), BF16)
    outs = []
    for h in range(qt_ref.shape[0] // HEAD_DIM):
        g = h // KV_GROUP
        qh = qt_ref[h * HEAD_DIM:(h + 1) * HEAD_DIM, :]
        w = jnp.concatenate([zeros] * g + [qh] + [zeros] * (n_kv - 1 - g), axis=0)
        o2, denom = attend(h, w)
        outs.append(o2[g * HEAD_DIM:(g + 1) * HEAD_DIM, :] / denom)
    o_ref[...] = jnp.concatenate(outs, axis=0).T.astype(BF16)


def _all_heads_fused_t(qt_ref, o_ref, attend):
    tq = qt_ref.shape[1]
    n_heads = qt_ref.shape[0] // HEAD_DIM
    n_kv = LANES // HEAD_DIM
    rows = []
    for g in range(n_kv):
        heads = [qt_ref[h * HEAD_DIM:(h + 1) * HEAD_DIM, :] if h // KV_GROUP == g
                 else jnp.zeros((HEAD_DIM, tq), BF16) for h in range(n_heads)]
        rows.append(jnp.concatenate(heads, axis=1))
    w = jnp.concatenate(rows, axis=0)
    o2, denom = attend(0, w)
    o2 = o2 / denom
    outs = [o2[(h // KV_GROUP) * HEAD_DIM:(h // KV_GROUP + 1) * HEAD_DIM, h * tq:(h + 1) * tq]
            for h in range(n_heads)]
    o_ref[...] = jnp.concatenate(outs, axis=0).T.astype(BF16)


def _bounded_or_max(small, known_small, bounded_path, max_path):
    if known_small:
        bounded_path()
    else:
        pl.when(small)(bounded_path)
        pl.when(jnp.logical_not(small))(max_path)


def _attn_b_kernel(bound_ref, qt_ref, k_ref, vt_ref, kc_ref, vct_ref, o_ref, *, known_small=False):
    k, kc, vt, vct = k_ref[...], kc_ref[...], vt_ref[...], vct_ref[...]
    def bounded_path():
        _all_heads_fused_t(qt_ref, o_ref, lambda h, w: _attend_t(w, k, kc, vt, vct, bounded=True))

    def max_path():
        _all_heads_t(qt_ref, o_ref, lambda h, w: _attend_t(w, k, kc, vt, vct))

    _bounded_or_max(bound_ref[0] <= SAFE_SCORE_BOUND, known_small, bounded_path, max_path)


def _attn_specs(nq, tq, seq, ctx_len):
    nqb = seq // tq
    return dict(
        qt=pl.BlockSpec((nq, tq), lambda b, i: (0, b * nqb + i)),
        k=pl.BlockSpec((seq, LANES), lambda b, i: (b, 0)),
        vt=pl.BlockSpec((LANES, seq), lambda b, i: (0, b)),
        kc=pl.BlockSpec((ctx_len, LANES), lambda b, i: (b, 0)),
        vct=pl.BlockSpec((LANES, ctx_len), lambda b, i: (0, b)),
        out=pl.BlockSpec((tq, nq), lambda b, i: (b * nqb + i, 0)))


def _attn_a_kernel(bound_ref, sink_ref, qt_ref, k_ref, vt_ref, kc_ref, vct_ref, bias_ref, o_ref,
                   *, seq, known_small=False):
    i = pl.program_id(1)
    tq = qt_ref.shape[1]
    n_heads = qt_ref.shape[0] // HEAD_DIM
    win = tq + 2 * WINDOW
    start = pl.multiple_of(jnp.clip(i * tq - WINDOW, 0, seq - win), WINDOW)
    k = k_ref[pl.ds(start, win), :]
    vt = vt_ref[:, pl.ds(start, win)]
    kc, vct = kc_ref[...], vct_ref[...]
    sinks = [sink_ref[h] * LOG2_E for h in range(n_heads)]

    def bounded_path():
        sink_row = jnp.concatenate([jnp.full((1, tq), s, F32) for s in sinks], axis=1)
        _all_heads_fused_t(qt_ref, o_ref, lambda h, w: _attend_t(
            w, k, kc, vt, vct, bounded=True, bias=bias_ref[0], sink=sink_row))

    def max_path():
        _all_heads_t(qt_ref, o_ref, lambda h, w: _attend_t(
            w, k, kc, vt, vct, bias=bias_ref[0], sink=sinks[h]))

    _bounded_or_max(bound_ref[0] <= SAFE_SCORE_BOUND, known_small, bounded_path, max_path)


def _band_bias(tq):
    win = tq + 2 * WINDOW
    r = np.arange(win)[:, None]
    j = np.arange(tq)[None, :]
    tables = [np.where(np.abs(off + r - j) <= WINDOW, 0.0, NEG_INF)
              for off in (0, -WINDOW, -2 * WINDOW)]
    return jnp.asarray(np.stack(tables), F32)


def _attn_ab_kernel(bound_a_ref, sink_ref, bound_b_ref,
                    qat_ref, ka_ref, vat_ref, kca_ref, vcat_ref, bias_ref,
                    qbt_ref, kb_ref, vbt_ref, kcb_ref, vcbt_ref, oa_ref, ob_ref, *, seq):
    def both_groups(known_small):
        _attn_a_kernel(bound_a_ref, sink_ref, qat_ref, ka_ref, vat_ref, kca_ref, vcat_ref,
                       bias_ref, oa_ref, seq=seq, known_small=known_small)
        _attn_b_kernel(bound_b_ref, qbt_ref, kb_ref, vbt_ref, kcb_ref, vcbt_ref, ob_ref,
                       known_small=known_small)

    small = (bound_a_ref[0] <= SAFE_SCORE_BOUND) & (bound_b_ref[0] <= SAFE_SCORE_BOUND)
    pl.when(small)(lambda: both_groups(True))
    pl.when(jnp.logical_not(small))(lambda: both_groups(False))


def _attention_ab(bound_a, sink, bound_b, a_ops, b_ops, batch, seq, ctx_len):
    nq, t = a_ops[0].shape
    assert b_ops[0].shape == (nq, t) and ATTN_A_TQ == ATTN_B_TQ
    tq = ATTN_A_TQ
    nqb = seq // tq
    win = tq + 2 * WINDOW
    assert nqb >= 2 and tq >= WINDOW
    sp = _attn_specs(nq, tq, seq, ctx_len)
    group = [sp["qt"], sp["k"], sp["vt"], sp["kc"], sp["vct"]]
    which = lambda b, i: (jnp.where(i == 0, 0, jnp.where(i == nqb - 1, 2, 1)), 0, 0)
    smem = pl.BlockSpec(memory_space=pltpu.SMEM)
    return pl.pallas_call(
        functools.partial(_attn_ab_kernel, seq=seq),
        grid=(batch, nqb),
        in_specs=[smem, smem, smem] + group + [pl.BlockSpec((1, win, tq), which)] + group,
        out_specs=[sp["out"], sp["out"]],
        out_shape=[jax.ShapeDtypeStruct((t, nq), BF16)] * 2,
        compiler_params=_params("arbitrary", "arbitrary"),
    )(bound_a, sink, bound_b, *a_ops, _band_bias(tq), *b_ops)


def _out_kernel(oa_ref, ob_ref, x_ref, g1_ref, sc2_ref, sh2_ref, ga_ref, gb_ref, gpost_ref,
                gpre2_ref, woa_ref, wob_ref, wr_ref, br_ref,
                x1_ref, h2_ref, rinfo_ref, rt_ref, tcarry_ref, tcnt_ref, cnt_ref, carry_ref):
    step = pl.program_id(0)

    @pl.when(step == 0)
    def _():
        carry_ref[...] = jnp.zeros_like(carry_ref)

    na = _rms(oa_ref[...].astype(F32)) * ga_ref[...]
    nb = _rms(ob_ref[...].astype(F32)) * gb_ref[...]
    ox = _dot(na.astype(BF16), woa_ref[...]) + _dot(nb.astype(BF16), wob_ref[...])
    x1 = x_ref[...] + g1_ref[0, 0] * (_rms(ox) * gpost_ref[...])
    x1_ref[...] = x1
    h2 = _rms(x1) * gpre2_ref[...] * (1.0 + sc2_ref[0, 0]) + sh2_ref[0, 0]
    h_hi, h_lo = _split_bf16(h2)
    h2_ref[...] = h_hi

    both = _dot(h_hi, wr_ref[...])
    logits = (both[:, :LANES] + _dot(h_lo, wr_ref[:, :LANES]) + both[:, LANES:]
              + br_ref[...])
    tm = logits.shape[0]
    lt = logits.T
    row = lax.broadcasted_iota(jnp.int32, lt.shape, 0)
    rowf = row.astype(F32)
    big = jnp.float32(1e9)
    ninf = jnp.float32(-jnp.inf)
    colmax = lambda v: jnp.max(v, axis=0, keepdims=True)
    colmin = lambda v: jnp.min(v, axis=0, keepdims=True)
    colsum = lambda v: jnp.sum(v, axis=0, keepdims=True)

    gmask = (row >= N_EXPERTS) & (row < N_EXPERTS + N_GROUPS)
    lg = jnp.where(gmask, lt, ninf)
    gmax = colmax(lg)
    gidx = colmin(jnp.where(lg == gmax, rowf, big)) - N_EXPERTS
    g_w = 1.0 / colsum(jnp.exp(lg - gmax))
    row_group = (row // EXPERTS_PER_GROUP).astype(F32)
    emask = (row < N_EXPERTS) & (row_group == gidx)
    le = jnp.where(emask, lt, ninf)
    m1 = colmax(le)
    i1 = colmin(jnp.where(le == m1, rowf, big))
    le2 = jnp.where(rowf == i1, ninf, le)
    m2 = colmax(le2)
    i2 = colmin(jnp.where(le2 == m2, rowf, big))
    e2 = jnp.exp(m2 - m1)
    w0 = g_w / (1.0 + e2)
    w1 = g_w * e2 / (1.0 + e2)

    hit1 = rowf == i1
    hit2 = rowf == i2
    onehot = jnp.where(hit1, 1.0, jnp.where(hit2, 1.0, 0.0)).astype(F32)
    r = lax.broadcasted_iota(jnp.int32, (tm, tm), 0)
    c = lax.broadcasted_iota(jnp.int32, (tm, tm), 1)
    earlier = jnp.where(r < c, 1.0, 0.0).astype(BF16)
    within = _dot(onehot.astype(BF16), earlier)
    tile_cnt = jnp.broadcast_to(jnp.sum(onehot, axis=1, keepdims=True), (LANES, LANES))
    er = lax.broadcasted_iota(jnp.int32, (LANES, LANES), 0)
    ec = lax.broadcasted_iota(jnp.int32, (LANES, LANES), 1)
    below = jnp.where(er > ec, 1.0, 0.0).astype(BF16)
    cnt_hi = jnp.floor(tile_cnt * (1.0 / 32.0))
    cnt_lo = tile_cnt - 32.0 * cnt_hi
    run_start = 32.0 * _dot(below, cnt_hi.astype(BF16)) + _dot(below, cnt_lo.astype(BF16))
    local = within + run_start[:, 0:1]
    pos0 = colsum(jnp.where(hit1, local, 0.0))
    pos1 = colsum(jnp.where(hit2, local, 0.0))
    cnt_row = tile_cnt.T[0:1, :]
    tcarry_ref[0] = carry_ref[...]
    tcnt_ref[0] = cnt_row
    carry_ref[...] += cnt_row
    cnt_ref[...] = carry_ref[...]

    fields = jnp.concatenate([i1, i2, pos0, pos1, w0, w1, jnp.zeros((2, tm), F32)], axis=0)
    rt_ref[...] = fields
    rinfo_ref[...] = jnp.concatenate(
        [fields, jnp.zeros((LANES - 8, tm), F32)], axis=0).T


def _out_and_route(oa, ob, x2, mod, ga, gb, gpost, gpre2, w_out, wr, br, seq):
    t, d = x2.shape
    tm = OUT_TM
    tpb = seq // tm
    nq = oa.shape[1]
    const = lambda shape: pl.BlockSpec(shape, lambda i: (0,) * len(shape))
    batch_of = lambda i: i // tpb
    rows = lambda n: pl.BlockSpec((tm, n), lambda i: (i, 0))
    per_tile = pl.BlockSpec((1, 1, LANES), lambda i: (i, 0, 0))
    w_half = lambda g: pl.BlockSpec((nq, d), lambda i: (g, 0))
    return pl.pallas_call(
        _out_kernel,
        grid=(t // tm,),
        in_specs=[rows(nq), rows(nq), rows(d),
                  _mod_spec(MOD_G1, batch_of)(d), _mod_spec(MOD_SC2, batch_of)(d),
                  _mod_spec(MOD_SH2, batch_of)(d),
                  const((1, nq)), const((1, nq)), const((1, d)), const((1, d)),
                  w_half(0), w_half(1), const(wr.shape),
                  const((1, LANES))],
        out_specs=[rows(d), rows(d), rows(LANES), pl.BlockSpec((8, tm), lambda i: (0, i)),
                   per_tile, per_tile, const((1, LANES))],
        out_shape=[jax.ShapeDtypeStruct((t, d), F32), jax.ShapeDtypeStruct((t, d), BF16),
                   jax.ShapeDtypeStruct((t, LANES), F32), jax.ShapeDtypeStruct((8, t), F32),
                   jax.ShapeDtypeStruct((t // tm, 1, LANES), F32),
                   jax.ShapeDtypeStruct((t // tm, 1, LANES), F32),
                   jax.ShapeDtypeStruct((1, LANES), F32)],
        scratch_shapes=[pltpu.VMEM((1, LANES), F32)],
        compiler_params=_params("arbitrary"),
    )(oa, ob, x2, mod, mod, mod, ga, gb, gpost, gpre2, w_out, w_out, wr, br)


PACK_ROWS = 8
ROW_DTYPE = F32


def _pack_rows(ref, x):
    n = x.shape[0]
    for c in range(PACK_ROWS):
        ref[pl.ds(c, n, stride=PACK_ROWS), :] = x[:, c * LANES:(c + 1) * LANES]


def _unpack_rows(ref):
    n = ref.shape[0] // PACK_ROWS
    return jnp.concatenate(
        [ref[pl.ds(c, n, stride=PACK_ROWS), :].astype(BF16) for c in range(PACK_ROWS)], axis=1)


def _for_each_run_piece(rdst_ref, rlen_ref, rmax_ref, tile, max_len, fn):
    def copy_runs(n_bits):
        def run(e, local):
            length = rlen_ref[tile * N_EXPERTS + e]
            dst = rdst_ref[tile * N_EXPERTS + e]
            for b in range(n_bits):
                size = 1 << b

                @pl.when(((length >> b) & 1) == 1)
                def _():
                    done = length & (size - 1)
                    fn(local + done, dst + done, size)
            return local + length

        lax.fori_loop(0, N_EXPERTS, run, 0)

    all_bits = max_len.bit_length()
    low_bits = min(all_bits, (4 * max_len // N_EXPERTS).bit_length())
    short = rmax_ref[tile] < (1 << low_bits)

    @pl.when(short)
    def _():
        copy_runs(low_bits)

    @pl.when(jnp.logical_not(short))
    def _():
        copy_runs(all_bits)


def _token_rows(ref, row0, n_rows):
    start = row0 * PACK_ROWS
    if not isinstance(start, int):
        start = pl.multiple_of(start, PACK_ROWS)
    return ref.at[pl.ds(start, n_rows * PACK_ROWS)]


def _dispatch_kernel(rdst_ref, rlen_ref, rmax_ref, h_ref, rt_ref, xs_ref, sorted_ref, sem):
    k = pl.program_id(0)
    nk = pl.num_programs(0)
    tm = h_ref.shape[0]
    rows = 2 * tm
    slot = k % 2

    def wait_slot(s):
        pltpu.make_async_copy(sorted_ref.at[s], _token_rows(xs_ref, 0, rows), sem.at[s]).wait()

    @pl.when(k >= 2)
    def _():
        wait_slot(slot)

    pos0 = rt_ref[2:3, :]
    pos1 = rt_ref[3:4, :]
    r = lax.broadcasted_iota(jnp.int32, (rows, tm), 0).astype(F32)
    perm = jnp.where((r == pos0) | (r == pos1), 1.0, 0.0).astype(BF16)
    srt = _dot(perm, h_ref[...].astype(BF16))
    buf = sorted_ref.at[slot]
    _pack_rows(buf, srt)

    def copy_piece(local, dst, size):
        pltpu.make_async_copy(_token_rows(buf, local, size), _token_rows(xs_ref, dst, size),
                              sem.at[slot]).start()

    _for_each_run_piece(rdst_ref, rlen_ref, rmax_ref, k, tm, copy_piece)

    @pl.when(k == nk - 1)
    def _():
        wait_slot(slot)

        @pl.when(nk >= 2)
        def _():
            wait_slot(1 - slot)


def _dispatch(run_dst, run_len, run_max, h2, rt):
    t, d = h2.shape
    assert d == PACK_ROWS * LANES
    tm = MOVE_TM
    return pl.pallas_call(
        _dispatch_kernel,
        grid_spec=pltpu.PrefetchScalarGridSpec(
            num_scalar_prefetch=3,
            grid=(t // tm,),
            in_specs=[pl.BlockSpec((tm, d), lambda i, *_: (i, 0)),
                      pl.BlockSpec((8, tm), lambda i, *_: (0, i))],
            out_specs=pl.BlockSpec(memory_space=pl.ANY),
            scratch_shapes=[pltpu.VMEM((2, 2 * tm * PACK_ROWS, LANES), ROW_DTYPE),
                            pltpu.SemaphoreType.DMA((2,))]),
        out_shape=jax.ShapeDtypeStruct((2 * t * PACK_ROWS, LANES), ROW_DTYPE),
        compiler_params=_params("arbitrary"),
    )(run_dst, run_len, run_max, h2, rt)


def _expert_kernel(vt_ref, ve_ref, va_ref, vb_ref, vn_ref, nv_ref, xs_ref, wg_hbm, wu_hbm, wd_hbm,
                   ys_ref, wg_bf, wu_bf, wd_bf, wg_f32, wu_f32, wd_f32, wsem):
    v = pl.program_id(0)
    valid = v < nv_ref[0]
    prev = jnp.maximum(v - 1, 0)
    new_expert = (v == 0) | (ve_ref[v] != ve_ref[prev])
    new_tile = (v == 0) | (vt_ref[v] != vt_ref[prev])

    def weight_copies(e):
        return [pltpu.make_async_copy(src.at[e], dst, wsem)
                for src, dst in ((wg_hbm, wg_f32), (wu_hbm, wu_f32), (wd_hbm, wd_f32))]

    @pl.when(v == 0)
    def _():
        for cp in weight_copies(ve_ref[0]):
            cp.start()

    @pl.when(valid & new_expert)
    def _():
        for cp in weight_copies(ve_ref[v]):
            cp.wait()
        wg_bf[...] = wg_f32[...].astype(BF16)
        wu_bf[...] = wu_f32[...].astype(BF16)
        wd_bf[...] = wd_f32[...].astype(BF16)

        @pl.when(vn_ref[v] >= 0)
        def _():
            for cp in weight_copies(vn_ref[v]):
                cp.start()

    def expert_rows():
        xb = _unpack_rows(xs_ref)
        gate = _dot(xb, wg_bf[...])
        up = _dot(xb, wu_bf[...])
        act = gate * jax.nn.sigmoid(gate) * up
        return _dot(act.astype(BF16), wd_bf[...])

    @pl.when(valid & new_tile)
    def _():
        _pack_rows(ys_ref, expert_rows())

    @pl.when(valid & jnp.logical_not(new_tile))
    def _():
        y = expert_rows()
        te = y.shape[0]
        row = lax.broadcasted_iota(jnp.int32, (te, 1), 0)
        mine = (row >= va_ref[v]) & (row < vb_ref[v])
        for c in range(PACK_ROWS):
            rows = pl.ds(c, te, stride=PACK_ROWS)
            ys_ref[rows, :] = jnp.where(mine, y[:, c * LANES:(c + 1) * LANES], ys_ref[rows, :])


def _expert_mlp(visit_tile, visit_expert, visit_lo, visit_hi, visit_next, n_visits, xs,
                w_gate, w_up, w_down):
    te = EXPERT_TE
    d, ff = w_gate.shape[1:]
    blk = (te * PACK_ROWS, LANES)
    tile = lambda v, vt, *_: (vt[v], 0)
    hbm = pl.BlockSpec(memory_space=pl.ANY)
    return pl.pallas_call(
        _expert_kernel,
        grid_spec=pltpu.PrefetchScalarGridSpec(
            num_scalar_prefetch=6,
            grid=(visit_tile.shape[0],),
            in_specs=[pl.BlockSpec(blk, tile), hbm, hbm, hbm],
            out_specs=pl.BlockSpec(blk, tile),
            scratch_shapes=[pltpu.VMEM((d, ff), BF16), pltpu.VMEM((d, ff), BF16),
                            pltpu.VMEM((ff, d), BF16),
                            pltpu.VMEM((d, ff), F32), pltpu.VMEM((d, ff), F32),
                            pltpu.VMEM((ff, d), F32), pltpu.SemaphoreType.DMA(())]),
        out_shape=jax.ShapeDtypeStruct(xs.shape, ROW_DTYPE),
        compiler_params=_params("arbitrary"),
    )(visit_tile, visit_expert, visit_lo, visit_hi, visit_next, n_visits, xs,
      w_gate, w_up, w_down)


def _combine_kernel(rdst_ref, rlen_ref, rmax_ref, x1_ref, rinfo_ref, g2_ref, gpost_ref, ys_ref, o_ref,
                    gath_ref, sem):
    k = pl.program_id(0)
    nk = pl.num_programs(0)
    tm = x1_ref.shape[0]
    rows = 2 * tm
    slot = k % 2

    def gather_runs(tile, s):
        buf = gath_ref.at[s]

        def copy_piece(local, src, size):
            pltpu.make_async_copy(_token_rows(ys_ref, src, size), _token_rows(buf, local, size),
                                  sem.at[s]).start()

        _for_each_run_piece(rdst_ref, rlen_ref, rmax_ref, tile, tm, copy_piece)

    @pl.when(k == 0)
    def _():
        gather_runs(0, 0)

    @pl.when(k + 1 < nk)
    def _():
        gather_runs(k + 1, 1 - slot)

    buf = gath_ref.at[slot]
    pltpu.make_async_copy(_token_rows(ys_ref, 0, rows), buf, sem.at[slot]).wait()
    g = _unpack_rows(buf)
    info = rinfo_ref[...]
    col = lax.broadcasted_iota(jnp.int32, (tm, rows), 1).astype(F32)
    pick = jnp.where(col == info[:, 2:3], info[:, 4:5],
                     jnp.where(col == info[:, 3:4], info[:, 5:6], 0.0)).astype(BF16)
    fx = _dot(pick, g)
    o_ref[...] = x1_ref[...] + g2_ref[0, 0] * (_rms(fx) * gpost_ref[...])


def _combine(run_dst, run_len, run_max, x1, rinfo, mod, gpost, ys, seq):
    t, d = x1.shape
    tm = MOVE_TM
    tpb = seq // tm
    batch_of = lambda i: i // tpb
    return pl.pallas_call(
        _combine_kernel,
        grid_spec=pltpu.PrefetchScalarGridSpec(
            num_scalar_prefetch=3,
            grid=(t // tm,),
            in_specs=[pl.BlockSpec((tm, d), lambda i, *_: (i, 0)),
                      pl.BlockSpec((tm, LANES), lambda i, *_: (i, 0)),
                      _mod_spec(MOD_G2, batch_of)(d),
                      pl.BlockSpec((1, d), lambda i, *_: (0, 0)),
                      pl.BlockSpec(memory_space=pl.ANY)],
            out_specs=pl.BlockSpec((tm, d), lambda i, *_: (i, 0)),
            scratch_shapes=[pltpu.VMEM((2, 2 * tm * PACK_ROWS, LANES), ROW_DTYPE),
                            pltpu.SemaphoreType.DMA((2,))]),
        out_shape=jax.ShapeDtypeStruct((t, d), F32),
        compiler_params=_params("arbitrary"),
    )(run_dst, run_len, run_max, x1, rinfo, mod, gpost, ys)


def _rope_tables(seq):
    pos = np.arange(seq)
    row = (pos // GRID_W).astype(np.float32)
    col = (pos % GRID_W).astype(np.float32)
    axis_dim = HEAD_DIM // 2
    inv_freq = (ROPE_THETA ** (-np.arange(0, axis_dim, 2, dtype=np.float32) / axis_dim)).astype(
        np.float32)
    ang = np.concatenate([row[:, None] * inv_freq, col[:, None] * inv_freq], axis=-1)
    pair = (np.arange(LANES) % HEAD_DIM) // 2
    cos = np.cos(ang)[:, pair]
    sin = np.sin(ang)[:, pair]
    even = (np.arange(LANES) % 2) == 0
    tables = (cos, np.where(even, -sin, 0.0), np.where(even, 0.0, sin))
    tables = tables + tuple(tb.T for tb in tables)
    return tuple(jnp.asarray(tb, F32) for tb in tables)


def _segment_ones(n):
    seg = np.arange(n) // HEAD_DIM
    return jnp.asarray(seg[:, None] == seg[None, :], BF16)


def kernel(x, c, ctx, c_ctx, w_mod, b_mod, attn_pre_norm, attn_post_norm, w_in, a_sink,
           b_q_norm, b_k_norm, a_out_norm, b_out_norm, w_out, ffn_pre_norm, ffn_post_norm,
           w_group, b_group, w_router, b_router, w_gate, w_up, w_down):
    batch, seq, d = x.shape
    ctx_len = ctx.shape[1]
    assert w_mod.shape[0] == 1, "single-layer stack only (context stream is never updated)"
    assert seq % ATTN_A_TQ == 0 and seq >= ATTN_A_TQ + 2 * WINDOW
    assert seq % PROJ_TM == 0 and seq % ATTN_B_TQ == 0 and seq % OUT_TM == 0 and seq % MOVE_TM == 0
    t = batch * seq
    nq = d // 2
    nkv = nq // KV_GROUP
    assert nkv == LANES and w_in.shape[2] == 2 * nq + 4 * nkv

    assert batch + 1 <= MOD_ROWS
    cc = jnp.concatenate([c, c_ctx[None, :], jnp.zeros((MOD_ROWS - batch - 1, d), F32)], axis=0)
    mod = _modulation(cc, w_mod[0], b_mod[0]).reshape(cc.shape[0], 6, 1, d)

    x2 = x.reshape(t, d)
    c2 = ctx.reshape(batch * ctx_len, d)
    gpre = attn_pre_norm[0].reshape(1, d)
    w_in_bf = w_in[0].astype(BF16)
    qn = jnp.broadcast_to(jnp.tile(b_q_norm[0], nq // HEAD_DIM)[:, None], (nq, PROJ_TM))
    kn = jnp.tile(b_k_norm[0], nkv // HEAD_DIM).reshape(1, nkv)
    seg_k = _segment_ones(nkv)
    qat, ka, vat, qbt, kb, vbt, stats = _project_latents(
        x2, mod, gpre, w_in_bf, _rope_tables(seq), qn, kn, seg_k, seq)
    kca, vcat, kcb, vcbt, ctx_stats = _project_context(
        c2, mod, batch, gpre, w_in_bf, kn, seg_k, ctx_len)

    q_sq = jnp.max(stats[:, 0, 0])
    k_sq = jnp.maximum(jnp.max(stats[:, 0, 1]), jnp.max(ctx_stats[:, 0, 1]))
    bound_a = jnp.maximum(1.01 * jnp.sqrt(q_sq * k_sq), jnp.max(a_sink[0]) * LOG2_E).reshape(1)
    score_bound = (1.01 * HEAD_DIM ** 0.5 * LOG2_E
                   * jnp.max(jnp.abs(b_q_norm[0])) * jnp.max(jnp.abs(b_k_norm[0]))).reshape(1)
    oa, ob = _attention_ab(bound_a, a_sink[0], score_bound, (qat, ka, vat, kca, vcat),
                           (qbt, kb, vbt, kcb, vcbt), batch, seq, ctx_len)

    w_out_bf = w_out[0].astype(BF16)
    lane_pad = LANES - N_EXPERTS - N_GROUPS
    w_r = jnp.pad(jnp.concatenate([w_router[0], w_group[0]], axis=1), ((0, 0), (0, lane_pad)))
    w_r_hi = w_r.astype(BF16)
    w_r_lo = (w_r - w_r_hi.astype(F32)).astype(BF16)
    w_r2 = jnp.concatenate([w_r_hi, w_r_lo], axis=1)
    b_r = jnp.pad(jnp.concatenate([b_router[0], b_group[0]]), (0, lane_pad)).reshape(1, LANES)
    x1, h2, rinfo, rt, tcarry, tcnt, counts = _out_and_route(
        oa, ob, x2, mod, a_out_norm[0].reshape(1, nq), b_out_norm[0].reshape(1, nq),
        attn_post_norm[0].reshape(1, d), ffn_pre_norm[0].reshape(1, d),
        w_out_bf, w_r2, b_r, seq)

    te = EXPERT_TE
    assert (2 * t) % te == 0
    cnt = counts[0, :N_EXPERTS].astype(jnp.int32)
    ends = jnp.cumsum(cnt)
    starts = ends - cnt
    run_dst = (starts[None, :] + tcarry[:, 0, :N_EXPERTS].astype(jnp.int32)).reshape(-1)
    tile_runs = tcnt[:, 0, :N_EXPERTS].astype(jnp.int32)
    run_len = tile_runs.reshape(-1)
    run_max = jnp.max(tile_runs, axis=1)
    first_tile = starts // te
    n_vis = jnp.where(cnt > 0, (ends - 1) // te - first_tile + 1, 0)
    vis_end = jnp.cumsum(n_vis)
    n_visits = vis_end[-1]
    v = jnp.minimum(jnp.arange(2 * t // te + N_EXPERTS, dtype=jnp.int32), n_visits - 1)
    v_expert = jnp.sum(vis_end[None, :] <= v[:, None], axis=1).astype(jnp.int32)
    pick = (v_expert[:, None] == jnp.arange(N_EXPERTS)[None, :]).astype(jnp.int32)
    of_expert = lambda table: jnp.sum(pick * table[None, :], axis=1)
    v_tile = of_expert(first_tile) + v - of_expert(vis_end - n_vis)
    v_lo = jnp.maximum(of_expert(starts) - v_tile * te, 0)
    v_hi = jnp.minimum(of_expert(ends) - v_tile * te, te)
    ids = jnp.arange(N_EXPERTS, dtype=jnp.int32)
    later = (ids[None, :] > ids[:, None]) & (cnt[None, :] > 0)
    next_expert = jnp.min(jnp.where(later, ids[None, :], N_EXPERTS), axis=1)
    v_next = of_expert(jnp.where(next_expert < N_EXPERTS, next_expert, -1))

    xs = _dispatch(run_dst, run_len, run_max, h2, rt)
    ys = _expert_mlp(v_tile, v_expert, v_lo, v_hi, v_next, n_visits.reshape(1), xs,
                     w_gate[0], w_up[0], w_down[0])
    out = _combine(run_dst, run_len, run_max, x1, rinfo, mod, ffn_post_norm[0].reshape(1, d), ys,
                   seq)
    return out.reshape(batch, seq, d)
```

```python
import functools

import jax
import jax.numpy as jnp
import numpy as np
from jax import lax
from jax.experimental import pallas as pl
from jax.experimental.pallas import tpu as pltpu

F32 = jnp.float32
BF16 = jnp.bfloat16

GRID_W = 64
HEAD_DIM = 64
KV_GROUP = 4
WINDOW = 128
ROPE_THETA = 10000.0
N_GROUPS = 4
EXPERTS_PER_GROUP = 8
N_EXPERTS = N_GROUPS * EXPERTS_PER_GROUP
EPS = 1e-6
NEG_INF = -1e30
LOG2_E = 1.4426950408889634
SAFE_SCORE_BOUND = 40.0

LANES = 128
V7X_VMEM_LIMIT = 56 * 1024 * 1024

MOD_ROWS = 16
MOD_BN = 1024
PROJ_TM = 512
ATTN_A_TQ = 256
ATTN_B_TQ = 256
OUT_TM = 512
EXPERT_TE = 512
MOVE_TM = OUT_TM


def _params(*sem):
    return pltpu.CompilerParams(dimension_semantics=sem, vmem_limit_bytes=V7X_VMEM_LIMIT)


def _dot(a, b):
    return jnp.dot(a, b, preferred_element_type=F32)


def _rms(x):
    return x * lax.rsqrt(jnp.mean(x * x, axis=-1, keepdims=True) + EPS)


def _split_bf16(x):
    hi = x.astype(BF16)
    lo = (x - hi.astype(F32)).astype(BF16)
    return hi, lo


def _mod_kernel(c_ref, w_ref, b_ref, o_ref):
    cc = c_ref[...]
    s = cc * jax.nn.sigmoid(cc)
    s_hi, s_lo = _split_bf16(s)
    w_hi, w_lo = _split_bf16(w_ref[...])
    o_ref[...] = _dot(s_hi, w_hi) + _dot(s_lo, w_hi) + _dot(s_hi, w_lo) + b_ref[...]


def _modulation(cc, w_mod, b_mod):
    rows, d = cc.shape
    n = w_mod.shape[1]
    bn = MOD_BN
    return pl.pallas_call(
        _mod_kernel,
        grid=(n // bn,),
        in_specs=[pl.BlockSpec((rows, d), lambda i: (0, 0)),
                  pl.BlockSpec((d, bn), lambda i: (0, i)),
                  pl.BlockSpec((1, bn), lambda i: (0, i))],
        out_specs=pl.BlockSpec((rows, bn), lambda i: (0, i)),
        out_shape=jax.ShapeDtypeStruct((rows, n), F32),
        compiler_params=_params("arbitrary"),
    )(cc, w_mod, b_mod.reshape(1, n))


def _rope(x, cos, sin_a, sin_b):
    return x * cos + pltpu.roll(x, LANES - 1, 1) * sin_a + pltpu.roll(x, 1, 1) * sin_b


def _head_norm(x, seg_ref, gain):
    ss = _dot((x * x).astype(BF16), seg_ref[...])
    return x * lax.rsqrt(ss * (1.0 / HEAD_DIM) + EPS) * gain


def _max_head_sq_norm(x, seg_ref):
    ss = _dot((x * x).astype(BF16), seg_ref[...])
    return jnp.max(jnp.max(ss, axis=1, keepdims=True), axis=0, keepdims=True)


def _norm_stats(q_sq, k_sq):
    lane = lax.broadcasted_iota(jnp.int32, (1, LANES), 1)
    zero = jnp.zeros((1, LANES), F32)
    return jnp.where(lane == 0, q_sq, zero) + jnp.where(lane == 1, k_sq, zero)


def _rope_t(xt, cos_t, sin_a_t, sin_b_t):
    return (xt * cos_t + pltpu.roll(xt, LANES - 1, 0) * sin_a_t
            + pltpu.roll(xt, 1, 0) * sin_b_t)


def _proj_kernel(x_ref, sc_ref, sh_ref, gpre_ref, w_ref, cos_ref, sa_ref, sb_ref,
                 cos_t_ref, sa_t_ref, sb_t_ref, qn_ref, kn_ref, seg_k_ref,
                 qat_ref, ka_ref, vat_ref, qbt_ref, kb_ref, vbt_ref, stats_ref):
    h = _rms(x_ref[...]) * gpre_ref[...] * (1.0 + sc_ref[0, 0]) + sh_ref[0, 0]
    p = _dot(h.astype(BF16), w_ref[...])
    cos, sa, sb = cos_ref[...], sa_ref[...], sb_ref[...]
    q_scale = HEAD_DIM ** -0.5 * LOG2_E
    cos_t, sa_t, sb_t = cos_t_ref[...] * q_scale, sa_t_ref[...] * q_scale, sb_t_ref[...] * q_scale
    nq = qat_ref.shape[0]
    q_sq = None
    for c in range(nq // LANES):
        xt = p[:, c * LANES:(c + 1) * LANES].T
        qat_ref[c * LANES:(c + 1) * LANES, :] = _rope_t(xt, cos_t, sa_t, sb_t).astype(BF16)
        sq = xt * xt
        for head in range(LANES // HEAD_DIM):
            norm = jnp.sum(sq[head * HEAD_DIM:(head + 1) * HEAD_DIM, :], axis=0, keepdims=True)
            q_sq = norm if q_sq is None else jnp.maximum(q_sq, norm)
    q_sq = jnp.max(q_sq, axis=1, keepdims=True) * (q_scale * q_scale)
    o = nq
    ka_ref[...] = _rope(p[:, o:o + LANES], cos, sa, sb).astype(BF16)
    vat_ref[...] = p[:, o + LANES:o + 2 * LANES].T.astype(BF16)
    stats_ref[0] = _norm_stats(q_sq, _max_head_sq_norm(p[:, o:o + LANES], seg_k_ref))
    o += 2 * LANES
    for c in range(nq // LANES):
        xt = p[:, o + c * LANES:o + (c + 1) * LANES].T
        sq = xt * xt
        halves = []
        for head in range(LANES // HEAD_DIM):
            rows = slice(head * HEAD_DIM, (head + 1) * HEAD_DIM)
            ms = jnp.sum(sq[rows, :], axis=0, keepdims=True) * (1.0 / HEAD_DIM)
            halves.append(xt[rows, :] * lax.rsqrt(ms + EPS))
        qn = jnp.concatenate(halves, axis=0) * qn_ref[c * LANES:(c + 1) * LANES, :]
        qbt_ref[c * LANES:(c + 1) * LANES, :] = _rope_t(qn, cos_t, sa_t, sb_t).astype(BF16)
    o += nq
    kb = _head_norm(p[:, o:o + LANES], seg_k_ref, kn_ref[...])
    kb_ref[...] = _rope(kb, cos, sa, sb).astype(BF16)
    vbt_ref[...] = p[:, o + LANES:o + 2 * LANES].T.astype(BF16)


def _ctx_proj_kernel(x_ref, sc_ref, sh_ref, gpre_ref, wa_ref, wb_ref, kn_ref, seg_k_ref,
                     ka_ref, vat_ref, kb_ref, vbt_ref, stats_ref):
    h = (_rms(x_ref[...]) * gpre_ref[...] * (1.0 + sc_ref[0, 0]) + sh_ref[0, 0]).astype(BF16)
    pa = _dot(h, wa_ref[...])
    pb = _dot(h, wb_ref[...])
    stats_ref[0] = _norm_stats(0.0, _max_head_sq_norm(pa[:, 0:LANES], seg_k_ref))
    ka_ref[...] = pa[:, 0:LANES].astype(BF16)
    vat_ref[...] = pa[:, LANES:2 * LANES].T.astype(BF16)
    kb_ref[...] = _head_norm(pb[:, 0:LANES], seg_k_ref, kn_ref[...]).astype(BF16)
    vbt_ref[...] = pb[:, LANES:2 * LANES].T.astype(BF16)


def _mod_spec(chunk, row_of_step):
    return lambda d: pl.BlockSpec((1, 1, 1, d), lambda i, *_: (row_of_step(i), chunk, 0, 0))


MOD_SH1, MOD_SC1, MOD_G1, MOD_SH2, MOD_SC2, MOD_G2 = range(6)


def _project_latents(x2, mod, gpre, w_in, tables, qn, kn, seg_k, seq):
    t, d = x2.shape
    tm = PROJ_TM
    tpb = seq // tm
    nq = qn.shape[0]
    const = lambda shape: pl.BlockSpec(shape, lambda i: (0,) * len(shape))
    batch_of = lambda i: i // tpb
    table = pl.BlockSpec((tm, LANES), lambda i: (i % tpb, 0))
    table_t = pl.BlockSpec((LANES, tm), lambda i: (0, i % tpb))
    k_spec = pl.BlockSpec((tm, LANES), lambda i: (i, 0))
    k_shape = jax.ShapeDtypeStruct((t, LANES), BF16)
    vt_spec = pl.BlockSpec((LANES, tm), lambda i: (0, i))
    vt_shape = jax.ShapeDtypeStruct((LANES, t), BF16)
    qt_spec = pl.BlockSpec((nq, tm), lambda i: (0, i))
    qt_shape = jax.ShapeDtypeStruct((nq, t), BF16)
    return pl.pallas_call(
        _proj_kernel,
        grid=(t // tm,),
        in_specs=[pl.BlockSpec((tm, d), lambda i: (i, 0)),
                  _mod_spec(MOD_SC1, batch_of)(d), _mod_spec(MOD_SH1, batch_of)(d), const((1, d)),
                  const(w_in.shape), table, table, table, table_t, table_t, table_t,
                  const((nq, tm)), const((1, LANES)), const(seg_k.shape)],
        out_specs=[qt_spec, k_spec, vt_spec, qt_spec, k_spec, vt_spec,
                   pl.BlockSpec((1, 1, LANES), lambda i: (i, 0, 0))],
        out_shape=[qt_shape, k_shape, vt_shape, qt_shape, k_shape, vt_shape,
                   jax.ShapeDtypeStruct((t // tm, 1, LANES), F32)],
        compiler_params=_params("arbitrary"),
    )(x2, mod, mod, gpre, w_in, *tables, qn, kn, seg_k)


def _project_context(c2, mod, ctx_row, gpre, w_in, kn, seg_k, ctx_len):
    t, d = c2.shape
    ctx_mod = lambda i: ctx_row
    nq = (w_in.shape[1] - 4 * LANES) // 2
    kv = 2 * LANES
    assert nq % kv == 0
    group_kv = lambda g: pl.BlockSpec((d, kv), lambda i: (0, (g * (nq + kv) + nq) // kv))
    const = lambda shape: pl.BlockSpec(shape, lambda i: (0,) * len(shape))
    k_spec = pl.BlockSpec((ctx_len, LANES), lambda i: (i, 0))
    k_shape = jax.ShapeDtypeStruct((t, LANES), BF16)
    vt_spec = pl.BlockSpec((LANES, ctx_len), lambda i: (0, i))
    vt_shape = jax.ShapeDtypeStruct((LANES, t), BF16)
    return pl.pallas_call(
        _ctx_proj_kernel,
        grid=(t // ctx_len,),
        in_specs=[pl.BlockSpec((ctx_len, d), lambda i: (i, 0)),
                  _mod_spec(MOD_SC1, ctx_mod)(d), _mod_spec(MOD_SH1, ctx_mod)(d),
                  const((1, d)), group_kv(0), group_kv(1), const((1, LANES)), const(seg_k.shape)],
        out_specs=[k_spec, vt_spec, k_spec, vt_spec,
                   pl.BlockSpec((1, 1, LANES), lambda i: (i, 0, 0))],
        out_shape=[k_shape, vt_shape, k_shape, vt_shape,
                   jax.ShapeDtypeStruct((t // ctx_len, 1, LANES), F32)],
        compiler_params=_params("arbitrary"),
    )(c2, mod, mod, gpre, w_in, w_in, kn, seg_k)


def _attend_t(w, k, kc, vt, vct, bounded=False, bias=None, sink=None):
    st = _dot(k, w)
    sct = _dot(kc, w)
    if bias is not None:
        tq = bias.shape[1]
        st = jnp.concatenate([st[:, c * tq:(c + 1) * tq] + bias
                              for c in range(st.shape[1] // tq)], axis=1)
    if not bounded:
        shift = jnp.maximum(jnp.max(st, axis=0, keepdims=True),
                            jnp.max(sct, axis=0, keepdims=True))
        if sink is not None:
            shift = jnp.maximum(shift, sink)
            sink = sink - shift
        st, sct = st - shift, sct - shift
    pt = jnp.exp2(st)
    pct = jnp.exp2(sct)
    denom = jnp.sum(pt, axis=0, keepdims=True) + jnp.sum(pct, axis=0, keepdims=True)
    if sink is not None:
        denom = denom + jnp.exp2(sink)
    o2 = _dot(vt, pt.astype(BF16)) + _dot(vct, pct.astype(BF16))
    return o2, denom


def _all_heads_t(qt_ref, o_ref, attend):
    tq = qt_ref.shape[1]
    n_kv = LANES // HEAD_DIM
    zeros = jnp.zeros((HEAD_DIM, tq), BF16)
    outs = []
    for h in range(qt_ref.shape[0] // HEAD_DIM):
        g = h // KV_GROUP
        qh = qt_ref[h * HEAD_DIM:(h + 1) * HEAD_DIM, :]
        w = jnp.concatenate([zeros] * g + [qh] + [zeros] * (n_kv - 1 - g), axis=0)
        o2, denom = attend(h, w)
        outs.append(o2[g * HEAD_DIM:(g + 1) * HEAD_DIM, :] / denom)
    o_ref[...] = jnp.concatenate(outs, axis=0).T.astype(BF16)


def _all_heads_fused_t(qt_ref, o_ref, attend):
    tq = qt_ref.shape[1]
    n_heads = qt_ref.shape[0] // HEAD_DIM
    n_kv = LANES // HEAD_DIM
    rows = []
    for g in range(n_kv):
        heads = [qt_ref[h * HEAD_DIM:(h + 1) * HEAD_DIM, :] if h // KV_GROUP == g
                 else jnp.zeros((HEAD_DIM, tq), BF16) for h in range(n_heads)]
        rows.append(jnp.concatenate(heads, axis=1))
    w = jnp.concatenate(rows, axis=0)
    o2, denom = attend(0, w)
    o2 = o2 / denom
    outs = [o2[(h // KV_GROUP) * HEAD_DIM:(h // KV_GROUP + 1) * HEAD_DIM, h * tq:(h + 1) * tq]
            for h in range(n_heads)]
    o_ref[...] = jnp.concatenate(outs, axis=0).T.astype(BF16)


def _bounded_or_max(small, known_small, bounded_path, max_path):
    if known_small:
        bounded_path()
    else:
        pl.when(small)(bounded_path)
        pl.when(jnp.logical_not(small))(max_path)


def _attn_b_kernel(bound_ref, qt_ref, k_ref, vt_ref, kc_ref, vct_ref, o_ref, *, known_small=False):
    k, kc, vt, vct = k_ref[...], kc_ref[...], vt_ref[...], vct_ref[...]
    def bounded_path():
        _all_heads_fused_t(qt_ref, o_ref, lambda h, w: _attend_t(w, k, kc, vt, vct, bounded=True))

    def max_path():
        _all_heads_t(qt_ref, o_ref, lambda h, w: _attend_t(w, k, kc, vt, vct))

    _bounded_or_max(bound_ref[0] <= SAFE_SCORE_BOUND, known_small, bounded_path, max_path)


def _attn_specs(nq, tq, seq, ctx_len):
    nqb = seq // tq
    return dict(
        qt=pl.BlockSpec((nq, tq), lambda b, i: (0, b * nqb + i)),
        k=pl.BlockSpec((seq, LANES), lambda b, i: (b, 0)),
        vt=pl.BlockSpec((LANES, seq), lambda b, i: (0, b)),
        kc=pl.BlockSpec((ctx_len, LANES), lambda b, i: (b, 0)),
        vct=pl.BlockSpec((LANES, ctx_len), lambda b, i: (0, b)),
        out=pl.BlockSpec((tq, nq), lambda b, i: (b * nqb + i, 0)))


def _attn_a_kernel(bound_ref, sink_ref, qt_ref, k_ref, vt_ref, kc_ref, vct_ref, bias_ref, o_ref,
                   *, seq, known_small=False):
    i = pl.program_id(1)
    tq = qt_ref.shape[1]
    n_heads = qt_ref.shape[0] // HEAD_DIM
    win = tq + 2 * WINDOW
    start = pl.multiple_of(jnp.clip(i * tq - WINDOW, 0, seq - win), WINDOW)
    k = k_ref[pl.ds(start, win), :]
    vt = vt_ref[:, pl.ds(start, win)]
    kc, vct = kc_ref[...], vct_ref[...]
    sinks = [sink_ref[h] * LOG2_E for h in range(n_heads)]

    def bounded_path():
        sink_row = jnp.concatenate([jnp.full((1, tq), s, F32) for s in sinks], axis=1)
        _all_heads_fused_t(qt_ref, o_ref, lambda h, w: _attend_t(
            w, k, kc, vt, vct, bounded=True, bias=bias_ref[0], sink=sink_row))

    def max_path():
        _all_heads_t(qt_ref, o_ref, lambda h, w: _attend_t(
            w, k, kc, vt, vct, bias=bias_ref[0], sink=sinks[h]))

    _bounded_or_max(bound_ref[0] <= SAFE_SCORE_BOUND, known_small, bounded_path, max_path)


def _band_bias(tq):
    win = tq + 2 * WINDOW
    r = np.arange(win)[:, None]
    j = np.arange(tq)[None, :]
    tables = [np.where(np.abs(off + r - j) <= WINDOW, 0.0, NEG_INF)
              for off in (0, -WINDOW, -2 * WINDOW)]
    return jnp.asarray(np.stack(tables), F32)


def _attn_ab_kernel(bound_a_ref, sink_ref, bound_b_ref,
                    qat_ref, ka_ref, vat_ref, kca_ref, vcat_ref, bias_ref,
                    qbt_ref, kb_ref, vbt_ref, kcb_ref, vcbt_ref, oa_ref, ob_ref, *, seq):
    def both_groups(known_small):
        _attn_b_kernel(bound_b_ref, qbt_ref, kb_ref, vbt_ref, kcb_ref, vcbt_ref, ob_ref,
                       known_small=known_small)
        _attn_a_kernel(bound_a_ref, sink_ref, qat_ref, ka_ref, vat_ref, kca_ref, vcat_ref,
                       bias_ref, oa_ref, seq=seq, known_small=known_small)

    small = (bound_a_ref[0] <= SAFE_SCORE_BOUND) & (bound_b_ref[0] <= SAFE_SCORE_BOUND)
    pl.when(small)(lambda: both_groups(True))
    pl.when(jnp.logical_not(small))(lambda: both_groups(False))


def _attention_ab(bound_a, sink, bound_b, a_ops, b_ops, batch, seq, ctx_len):
    nq, t = a_ops[0].shape
    assert b_ops[0].shape == (nq, t) and ATTN_A_TQ == ATTN_B_TQ
    tq = ATTN_A_TQ
    nqb = seq // tq
    win = tq + 2 * WINDOW
    assert nqb >= 2 and tq >= WINDOW
    sp = _attn_specs(nq, tq, seq, ctx_len)
    group = [sp["qt"], sp["k"], sp["vt"], sp["kc"], sp["vct"]]
    which = lambda b, i: (jnp.where(i == 0, 0, jnp.where(i == nqb - 1, 2, 1)), 0, 0)
    smem = pl.BlockSpec(memory_space=pltpu.SMEM)
    return pl.pallas_call(
        functools.partial(_attn_ab_kernel, seq=seq),
        grid=(batch, nqb),
        in_specs=[smem, smem, smem] + group + [pl.BlockSpec((1, win, tq), which)] + group,
        out_specs=[sp["out"], sp["out"]],
        out_shape=[jax.ShapeDtypeStruct((t, nq), BF16)] * 2,
        compiler_params=_params("arbitrary", "arbitrary"),
    )(bound_a, sink, bound_b, *a_ops, _band_bias(tq), *b_ops)


def _out_kernel(oa_ref, ob_ref, x_ref, g1_ref, sc2_ref, sh2_ref, ga_ref, gb_ref, gpost_ref,
                gpre2_ref, woa_ref, wob_ref, wr_ref, br_ref,
                x1_ref, h2_ref, rinfo_ref, rt_ref, tcarry_ref, tcnt_ref, cnt_ref, carry_ref):
    step = pl.program_id(0)

    @pl.when(step == 0)
    def _():
        carry_ref[...] = jnp.zeros_like(carry_ref)

    na = _rms(oa_ref[...].astype(F32)) * ga_ref[...]
    nb = _rms(ob_ref[...].astype(F32)) * gb_ref[...]
    ox = _dot(na.astype(BF16), woa_ref[...]) + _dot(nb.astype(BF16), wob_ref[...])
    x1 = x_ref[...] + g1_ref[0, 0] * (_rms(ox) * gpost_ref[...])
    x1_ref[...] = x1
    h2 = _rms(x1) * gpre2_ref[...] * (1.0 + sc2_ref[0, 0]) + sh2_ref[0, 0]
    h_hi, h_lo = _split_bf16(h2)
    h2_ref[...] = h_hi

    both = _dot(h_hi, wr_ref[...])
    logits = (both[:, :LANES] + _dot(h_lo, wr_ref[:, :LANES]) + both[:, LANES:]
              + br_ref[...])
    tm = logits.shape[0]
    lt = logits.T
    row = lax.broadcasted_iota(jnp.int32, lt.shape, 0)
    rowf = row.astype(F32)
    big = jnp.float32(1e9)
    ninf = jnp.float32(-jnp.inf)
    colmax = lambda v: jnp.max(v, axis=0, keepdims=True)
    colmin = lambda v: jnp.min(v, axis=0, keepdims=True)
    colsum = lambda v: jnp.sum(v, axis=0, keepdims=True)

    gmask = (row >= N_EXPERTS) & (row < N_EXPERTS + N_GROUPS)
    lg = jnp.where(gmask, lt, ninf)
    gmax = colmax(lg)
    gidx = colmin(jnp.where(lg == gmax, rowf, big)) - N_EXPERTS
    g_w = 1.0 / colsum(jnp.exp(lg - gmax))
    row_group = (row // EXPERTS_PER_GROUP).astype(F32)
    emask = (row < N_EXPERTS) & (row_group == gidx)
    le = jnp.where(emask, lt, ninf)
    m1 = colmax(le)
    i1 = colmin(jnp.where(le == m1, rowf, big))
    le2 = jnp.where(rowf == i1, ninf, le)
    m2 = colmax(le2)
    i2 = colmin(jnp.where(le2 == m2, rowf, big))
    e2 = jnp.exp(m2 - m1)
    w0 = g_w / (1.0 + e2)
    w1 = g_w * e2 / (1.0 + e2)

    hit1 = rowf == i1
    hit2 = rowf == i2
    onehot = jnp.where(hit1, 1.0, jnp.where(hit2, 1.0, 0.0)).astype(F32)
    r = lax.broadcasted_iota(jnp.int32, (tm, tm), 0)
    c = lax.broadcasted_iota(jnp.int32, (tm, tm), 1)
    earlier = jnp.where(r < c, 1.0, 0.0).astype(BF16)
    within = _dot(onehot.astype(BF16), earlier)
    tile_cnt = jnp.broadcast_to(jnp.sum(onehot, axis=1, keepdims=True), (LANES, LANES))
    er = lax.broadcasted_iota(jnp.int32, (LANES, LANES), 0)
    ec = lax.broadcasted_iota(jnp.int32, (LANES, LANES), 1)
    below = jnp.where(er > ec, 1.0, 0.0).astype(BF16)
    cnt_hi = jnp.floor(tile_cnt * (1.0 / 32.0))
    cnt_lo = tile_cnt - 32.0 * cnt_hi
    run_start = 32.0 * _dot(below, cnt_hi.astype(BF16)) + _dot(below, cnt_lo.astype(BF16))
    local = within + run_start[:, 0:1]
    pos0 = colsum(jnp.where(hit1, local, 0.0))
    pos1 = colsum(jnp.where(hit2, local, 0.0))
    cnt_row = tile_cnt.T[0:1, :]
    tcarry_ref[0] = carry_ref[...]
    tcnt_ref[0] = cnt_row
    carry_ref[...] += cnt_row
    cnt_ref[...] = carry_ref[...]

    fields = jnp.concatenate([i1, i2, pos0, pos1, w0, w1, jnp.zeros((2, tm), F32)], axis=0)
    rt_ref[...] = fields
    rinfo_ref[...] = jnp.concatenate(
        [fields, jnp.zeros((LANES - 8, tm), F32)], axis=0).T


def _out_and_route(oa, ob, x2, mod, ga, gb, gpost, gpre2, w_out, wr, br, seq):
    t, d = x2.shape
    tm = OUT_TM
    tpb = seq // tm
    nq = oa.shape[1]
    const = lambda shape: pl.BlockSpec(shape, lambda i: (0,) * len(shape))
    batch_of = lambda i: i // tpb
    rows = lambda n: pl.BlockSpec((tm, n), lambda i: (i, 0))
    per_tile = pl.BlockSpec((1, 1, LANES), lambda i: (i, 0, 0))
    w_half = lambda g: pl.BlockSpec((nq, d), lambda i: (g, 0))
    return pl.pallas_call(
        _out_kernel,
        grid=(t // tm,),
        in_specs=[rows(nq), rows(nq), rows(d),
                  _mod_spec(MOD_G1, batch_of)(d), _mod_spec(MOD_SC2, batch_of)(d),
                  _mod_spec(MOD_SH2, batch_of)(d),
                  const((1, nq)), const((1, nq)), const((1, d)), const((1, d)),
                  w_half(0), w_half(1), const(wr.shape),
                  const((1, LANES))],
        out_specs=[rows(d), rows(d), rows(LANES), pl.BlockSpec((8, tm), lambda i: (0, i)),
                   per_tile, per_tile, const((1, LANES))],
        out_shape=[jax.ShapeDtypeStruct((t, d), F32), jax.ShapeDtypeStruct((t, d), BF16),
                   jax.ShapeDtypeStruct((t, LANES), F32), jax.ShapeDtypeStruct((8, t), F32),
                   jax.ShapeDtypeStruct((t // tm, 1, LANES), F32),
                   jax.ShapeDtypeStruct((t // tm, 1, LANES), F32),
                   jax.ShapeDtypeStruct((1, LANES), F32)],
        scratch_shapes=[pltpu.VMEM((1, LANES), F32)],
        compiler_params=_params("arbitrary"),
    )(oa, ob, x2, mod, mod, mod, ga, gb, gpost, gpre2, w_out, w_out, wr, br)


PACK_ROWS = 8
ROW_DTYPE = F32


def _pack_rows(ref, x):
    n = x.shape[0]
    for c in range(PACK_ROWS):
        ref[pl.ds(c, n, stride=PACK_ROWS), :] = x[:, c * LANES:(c + 1) * LANES]


def _unpack_rows(ref):
    n = ref.shape[0] // PACK_ROWS
    return jnp.concatenate(
        [ref[pl.ds(c, n, stride=PACK_ROWS), :].astype(BF16) for c in range(PACK_ROWS)], axis=1)


def _for_each_run_piece(rdst_ref, rlen_ref, rmax_ref, tile, max_len, fn):
    def copy_runs(n_bits):
        def run(e, local):
            length = rlen_ref[tile * N_EXPERTS + e]
            dst = rdst_ref[tile * N_EXPERTS + e]
            for b in range(n_bits):
                size = 1 << b

                @pl.when(((length >> b) & 1) == 1)
                def _():
                    done = length & (size - 1)
                    fn(local + done, dst + done, size)
            return local + length

        lax.fori_loop(0, N_EXPERTS, run, 0)

    all_bits = max_len.bit_length()
    low_bits = min(all_bits, (4 * max_len // N_EXPERTS).bit_length())
    short = rmax_ref[tile] < (1 << low_bits)

    @pl.when(short)
    def _():
        copy_runs(low_bits)

    @pl.when(jnp.logical_not(short))
    def _():
        copy_runs(all_bits)


def _token_rows(ref, row0, n_rows):
    start = row0 * PACK_ROWS
    if not isinstance(start, int):
        start = pl.multiple_of(start, PACK_ROWS)
    return ref.at[pl.ds(start, n_rows * PACK_ROWS)]


def _dispatch_kernel(rdst_ref, rlen_ref, rmax_ref, h_ref, rt_ref, xs_ref, sorted_ref, sem):
    k = pl.program_id(0)
    nk = pl.num_programs(0)
    tm = h_ref.shape[0]
    rows = 2 * tm
    slot = k % 2

    def wait_slot(s):
        pltpu.make_async_copy(sorted_ref.at[s], _token_rows(xs_ref, 0, rows), sem.at[s]).wait()

    @pl.when(k >= 2)
    def _():
        wait_slot(slot)

    pos0 = rt_ref[2:3, :]
    pos1 = rt_ref[3:4, :]
    r = lax.broadcasted_iota(jnp.int32, (rows, tm), 0).astype(F32)
    perm = jnp.where((r == pos0) | (r == pos1), 1.0, 0.0).astype(BF16)
    srt = _dot(perm, h_ref[...].astype(BF16))
    buf = sorted_ref.at[slot]
    _pack_rows(buf, srt)

    def copy_piece(local, dst, size):
        pltpu.make_async_copy(_token_rows(buf, local, size), _token_rows(xs_ref, dst, size),
                              sem.at[slot]).start()

    _for_each_run_piece(rdst_ref, rlen_ref, rmax_ref, k, tm, copy_piece)

    @pl.when(k == nk - 1)
    def _():
        wait_slot(slot)

        @pl.when(nk >= 2)
        def _():
            wait_slot(1 - slot)


def _dispatch(run_dst, run_len, run_max, h2, rt):
    t, d = h2.shape
    assert d == PACK_ROWS * LANES
    tm = MOVE_TM
    return pl.pallas_call(
        _dispatch_kernel,
        grid_spec=pltpu.PrefetchScalarGridSpec(
            num_scalar_prefetch=3,
            grid=(t // tm,),
            in_specs=[pl.BlockSpec((tm, d), lambda i, *_: (i, 0)),
                      pl.BlockSpec((8, tm), lambda i, *_: (0, i))],
            out_specs=pl.BlockSpec(memory_space=pl.ANY),
            scratch_shapes=[pltpu.VMEM((2, 2 * tm * PACK_ROWS, LANES), ROW_DTYPE),
                            pltpu.SemaphoreType.DMA((2,))]),
        out_shape=jax.ShapeDtypeStruct((2 * t * PACK_ROWS, LANES), ROW_DTYPE),
        compiler_params=_params("arbitrary"),
    )(run_dst, run_len, run_max, h2, rt)


def _expert_kernel(vt_ref, ve_ref, va_ref, vb_ref, vn_ref, nv_ref, xs_ref, wg_hbm, wu_hbm, wd_hbm,
                   ys_ref, wg_bf, wu_bf, wd_bf, wg_f32, wu_f32, wd_f32, wsem):
    v = pl.program_id(0)
    valid = v < nv_ref[0]
    prev = jnp.maximum(v - 1, 0)
    new_expert = (v == 0) | (ve_ref[v] != ve_ref[prev])
    new_tile = (v == 0) | (vt_ref[v] != vt_ref[prev])

    def weight_copies(e):
        return [pltpu.make_async_copy(src.at[e], dst, wsem)
                for src, dst in ((wg_hbm, wg_f32), (wu_hbm, wu_f32), (wd_hbm, wd_f32))]

    @pl.when(v == 0)
    def _():
        for cp in weight_copies(ve_ref[0]):
            cp.start()

    @pl.when(valid & new_expert)
    def _():
        for cp in weight_copies(ve_ref[v]):
            cp.wait()
        wg_bf[...] = wg_f32[...].astype(BF16)
        wu_bf[...] = wu_f32[...].astype(BF16)
        wd_bf[...] = wd_f32[...].astype(BF16)

        @pl.when(vn_ref[v] >= 0)
        def _():
            for cp in weight_copies(vn_ref[v]):
                cp.start()

    def expert_rows():
        xb = _unpack_rows(xs_ref)
        gate = _dot(xb, wg_bf[...])
        up = _dot(xb, wu_bf[...])
        act = gate * jax.nn.sigmoid(gate) * up
        return _dot(act.astype(BF16), wd_bf[...])

    @pl.when(valid & new_tile)
    def _():
        _pack_rows(ys_ref, expert_rows())

    @pl.when(valid & jnp.logical_not(new_tile))
    def _():
        y = expert_rows()
        te = y.shape[0]
        row = lax.broadcasted_iota(jnp.int32, (te, 1), 0)
        mine = (row >= va_ref[v]) & (row < vb_ref[v])
        for c in range(PACK_ROWS):
            rows = pl.ds(c, te, stride=PACK_ROWS)
            ys_ref[rows, :] = jnp.where(mine, y[:, c * LANES:(c + 1) * LANES], ys_ref[rows, :])


def _expert_mlp(visit_tile, visit_expert, visit_lo, visit_hi, visit_next, n_visits, xs,
                w_gate, w_up, w_down):
    te = EXPERT_TE
    d, ff = w_gate.shape[1:]
    blk = (te * PACK_ROWS, LANES)
    tile = lambda v, vt, *_: (vt[v], 0)
    hbm = pl.BlockSpec(memory_space=pl.ANY)
    return pl.pallas_call(
        _expert_kernel,
        grid_spec=pltpu.PrefetchScalarGridSpec(
            num_scalar_prefetch=6,
            grid=(visit_tile.shape[0],),
            in_specs=[pl.BlockSpec(blk, tile), hbm, hbm, hbm],
            out_specs=pl.BlockSpec(blk, tile),
            scratch_shapes=[pltpu.VMEM((d, ff), BF16), pltpu.VMEM((d, ff), BF16),
                            pltpu.VMEM((ff, d), BF16),
                            pltpu.VMEM((d, ff), F32), pltpu.VMEM((d, ff), F32),
                            pltpu.VMEM((ff, d), F32), pltpu.SemaphoreType.DMA(())]),
        out_shape=jax.ShapeDtypeStruct(xs.shape, ROW_DTYPE),
        compiler_params=_params("arbitrary"),
    )(visit_tile, visit_expert, visit_lo, visit_hi, visit_next, n_visits, xs,
      w_gate, w_up, w_down)


def _combine_kernel(rdst_ref, rlen_ref, rmax_ref, x1_ref, rinfo_ref, g2_ref, gpost_ref, ys_ref, o_ref,
                    gath_ref, sem):
    k = pl.program_id(0)
    nk = pl.num_programs(0)
    tm = x1_ref.shape[0]
    rows = 2 * tm
    slot = k % 2

    def gather_runs(tile, s):
        buf = gath_ref.at[s]

        def copy_piece(local, src, size):
            pltpu.make_async_copy(_token_rows(ys_ref, src, size), _token_rows(buf, local, size),
                                  sem.at[s]).start()

        _for_each_run_piece(rdst_ref, rlen_ref, rmax_ref, tile, tm, copy_piece)

    @pl.when(k == 0)
    def _():
        gather_runs(0, 0)

    @pl.when(k + 1 < nk)
    def _():
        gather_runs(k + 1, 1 - slot)

    buf = gath_ref.at[slot]
    pltpu.make_async_copy(_token_rows(ys_ref, 0, rows), buf, sem.at[slot]).wait()
    g = _unpack_rows(buf)
    info = rinfo_ref[...]
    col = lax.broadcasted_iota(jnp.int32, (tm, rows), 1).astype(F32)
    pick = jnp.where(col == info[:, 2:3], info[:, 4:5],
                     jnp.where(col == info[:, 3:4], info[:, 5:6], 0.0)).astype(BF16)
    fx = _dot(pick, g)
    o_ref[...] = x1_ref[...] + g2_ref[0, 0] * (_rms(fx) * gpost_ref[...])


def _combine(run_dst, run_len, run_max, x1, rinfo, mod, gpost, ys, seq):
    t, d = x1.shape
    tm = MOVE_TM
    tpb = seq // tm
    batch_of = lambda i: i // tpb
    return pl.pallas_call(
        _combine_kernel,
        grid_spec=pltpu.PrefetchScalarGridSpec(
            num_scalar_prefetch=3,
            grid=(t // tm,),
            in_specs=[pl.BlockSpec((tm, d), lambda i, *_: (i, 0)),
                      pl.BlockSpec((tm, LANES), lambda i, *_: (i, 0)),
                      _mod_spec(MOD_G2, batch_of)(d),
                      pl.BlockSpec((1, d), lambda i, *_: (0, 0)),
                      pl.BlockSpec(memory_space=pl.ANY)],
            out_specs=pl.BlockSpec((tm, d), lambda i, *_: (i, 0)),
            scratch_shapes=[pltpu.VMEM((2, 2 * tm * PACK_ROWS, LANES), ROW_DTYPE),
                            pltpu.SemaphoreType.DMA((2,))]),
        out_shape=jax.ShapeDtypeStruct((t, d), F32),
        compiler_params=_params("arbitrary"),
    )(run_dst, run_len, run_max, x1, rinfo, mod, gpost, ys)


def _rope_tables(seq):
    pos = np.arange(seq)
    row = (pos // GRID_W).astype(np.float32)
    col = (pos % GRID_W).astype(np.float32)
    axis_dim = HEAD_DIM // 2
    inv_freq = (ROPE_THETA ** (-np.arange(0, axis_dim, 2, dtype=np.float32) / axis_dim)).astype(
        np.float32)
    ang = np.concatenate([row[:, None] * inv_freq, col[:, None] * inv_freq], axis=-1)
    pair = (np.arange(LANES) % HEAD_DIM) // 2
    cos = np.cos(ang)[:, pair]
    sin = np.sin(ang)[:, pair]
    even = (np.arange(LANES) % 2) == 0
    tables = (cos, np.where(even, -sin, 0.0), np.where(even, 0.0, sin))
    tables = tables + tuple(tb.T for tb in tables)
    return tuple(jnp.asarray(tb, F32) for tb in tables)


def _segment_ones(n):
    seg = np.arange(n) // HEAD_DIM
    return jnp.asarray(seg[:, None] == seg[None, :], BF16)


def kernel(x, c, ctx, c_ctx, w_mod, b_mod, attn_pre_norm, attn_post_norm, w_in, a_sink,
           b_q_norm, b_k_norm, a_out_norm, b_out_norm, w_out, ffn_pre_norm, ffn_post_norm,
           w_group, b_group, w_router, b_router, w_gate, w_up, w_down):
    batch, seq, d = x.shape
    ctx_len = ctx.shape[1]
    assert w_mod.shape[0] == 1, "single-layer stack only (context stream is never updated)"
    assert seq % ATTN_A_TQ == 0 and seq >= ATTN_A_TQ + 2 * WINDOW
    assert seq % PROJ_TM == 0 and seq % ATTN_B_TQ == 0 and seq % OUT_TM == 0 and seq % MOVE_TM == 0
    t = batch * seq
    nq = d // 2
    nkv = nq // KV_GROUP
    assert nkv == LANES and w_in.shape[2] == 2 * nq + 4 * nkv

    assert batch + 1 <= MOD_ROWS
    cc = jnp.concatenate([c, c_ctx[None, :], jnp.zeros((MOD_ROWS - batch - 1, d), F32)], axis=0)
    mod = _modulation(cc, w_mod[0], b_mod[0]).reshape(cc.shape[0], 6, 1, d)

    x2 = x.reshape(t, d)
    c2 = ctx.reshape(batch * ctx_len, d)
    gpre = attn_pre_norm[0].reshape(1, d)
    w_in_bf = w_in[0].astype(BF16)
    qn = jnp.broadcast_to(jnp.tile(b_q_norm[0], nq // HEAD_DIM)[:, None], (nq, PROJ_TM))
    kn = jnp.tile(b_k_norm[0], nkv // HEAD_DIM).reshape(1, nkv)
    seg_k = _segment_ones(nkv)
    qat, ka, vat, qbt, kb, vbt, stats = _project_latents(
        x2, mod, gpre, w_in_bf, _rope_tables(seq), qn, kn, seg_k, seq)
    kca, vcat, kcb, vcbt, ctx_stats = _project_context(
        c2, mod, batch, gpre, w_in_bf, kn, seg_k, ctx_len)

    q_sq = jnp.max(stats[:, 0, 0])
    k_sq = jnp.maximum(jnp.max(stats[:, 0, 1]), jnp.max(ctx_stats[:, 0, 1]))
    bound_a = jnp.maximum(1.01 * jnp.sqrt(q_sq * k_sq), jnp.max(a_sink[0]) * LOG2_E).reshape(1)
    score_bound = (1.01 * HEAD_DIM ** 0.5 * LOG2_E
                   * jnp.max(jnp.abs(b_q_norm[0])) * jnp.max(jnp.abs(b_k_norm[0]))).reshape(1)
    oa, ob = _attention_ab(bound_a, a_sink[0], score_bound, (qat, ka, vat, kca, vcat),
                           (qbt, kb, vbt, kcb, vcbt), batch, seq, ctx_len)

    w_out_bf = w_out[0].astype(BF16)
    lane_pad = LANES - N_EXPERTS - N_GROUPS
    w_r = jnp.pad(jnp.concatenate([w_router[0], w_group[0]], axis=1), ((0, 0), (0, lane_pad)))
    w_r_hi = w_r.astype(BF16)
    w_r_lo = (w_r - w_r_hi.astype(F32)).astype(BF16)
    w_r2 = jnp.concatenate([w_r_hi, w_r_lo], axis=1)
    b_r = jnp.pad(jnp.concatenate([b_router[0], b_group[0]]), (0, lane_pad)).reshape(1, LANES)
    x1, h2, rinfo, rt, tcarry, tcnt, counts = _out_and_route(
        oa, ob, x2, mod, a_out_norm[0].reshape(1, nq), b_out_norm[0].reshape(1, nq),
        attn_post_norm[0].reshape(1, d), ffn_pre_norm[0].reshape(1, d),
        w_out_bf, w_r2, b_r, seq)

    te = EXPERT_TE
    assert (2 * t) % te == 0
    cnt = counts[0, :N_EXPERTS].astype(jnp.int32)
    ends = jnp.cumsum(cnt)
    starts = ends - cnt
    run_dst = (starts[None, :] + tcarry[:, 0, :N_EXPERTS].astype(jnp.int32)).reshape(-1)
    tile_runs = tcnt[:, 0, :N_EXPERTS].astype(jnp.int32)
    run_len = tile_runs.reshape(-1)
    run_max = jnp.max(tile_runs, axis=1)
    first_tile = starts // te
    n_vis = jnp.where(cnt > 0, (ends - 1) // te - first_tile + 1, 0)
    vis_end = jnp.cumsum(n_vis)
    n_visits = vis_end[-1]
    v = jnp.minimum(jnp.arange(2 * t // te + N_EXPERTS, dtype=jnp.int32), n_visits - 1)
    v_expert = jnp.sum(vis_end[None, :] <= v[:, None], axis=1).astype(jnp.int32)
    pick = (v_expert[:, None] == jnp.arange(N_EXPERTS)[None, :]).astype(jnp.int32)
    of_expert = lambda table: jnp.sum(pick * table[None, :], axis=1)
    v_tile = of_expert(first_tile) + v - of_expert(vis_end - n_vis)
    v_lo = jnp.maximum(of_expert(starts) - v_tile * te, 0)
    v_hi = jnp.minimum(of_expert(ends) - v_tile * te, te)
    ids = jnp.arange(N_EXPERTS, dtype=jnp.int32)
    later = (ids[None, :] > ids[:, None]) & (cnt[None, :] > 0)
    next_expert = jnp.min(jnp.where(later, ids[None, :], N_EXPERTS), axis=1)
    v_next = of_expert(jnp.where(next_expert < N_EXPERTS, next_expert, -1))

    xs = _dispatch(run_dst, run_len, run_max, h2, rt)
    ys = _expert_mlp(v_tile, v_expert, v_lo, v_hi, v_next, n_visits.reshape(1), xs,
                     w_gate[0], w_up[0], w_down[0])
    out = _combine(run_dst, run_len, run_max, x1, rinfo, mod, ffn_post_norm[0].reshape(1, d), ys,
                   seq)
    return out.reshape(batch, seq, d)
```
